```python
import jax, jax.numpy as jnp
from jax import lax
import numpy as np

D_MODEL = 1024
BATCH = 16
SEQ = 256
DEPTH = 4
DEC_BATCH = 2
DEC_SEQ = 1024
PAST_LEN = 512

GRID_W = 64
D_BRANCH = 512
D_POOL = D_BRANCH
POOL_WINDOWS = (2, 4, 8, 16)
N_POOL_GROUPS = 4
POOL_GROUP = D_POOL // N_POOL_GROUPS
D_SGU = D_BRANCH
N_SGU_GROUPS = 4
SGU_GROUP = D_SGU // N_SGU_GROUPS
CHUNK = 128
D_MLSTM = D_BRANCH
N_MLSTM_HEADS = 4
HEAD_DIM = D_MLSTM // N_MLSTM_HEADS
MLSTM_CHUNK = 64
N_BRANCH = 3
D_FF = 4 * D_MODEL
EPS = 1e-6
IN_WIDTHS = (D_POOL, D_SGU, D_SGU, D_MLSTM, D_MLSTM, D_MLSTM, D_MLSTM, 4 * N_MLSTM_HEADS)
D_IN = D_POOL + 2 * D_SGU + 4 * D_MLSTM + 4 * N_MLSTM_HEADS + N_BRANCH * D_MODEL

kernel_name = "hybrid_pool_sgu_mlstm_diffusion_step"


def rmsnorm(x, g):
    x32 = x.astype(jnp.float32)
    y = x32 * lax.rsqrt(jnp.mean(x32 * x32, axis=-1, keepdims=True) + EPS)
    return (y * g.astype(jnp.float32)).astype(x.dtype)


def centred_mean(x, w, axis):
    n = x.shape[axis]
    s = jnp.cumsum(x.astype(jnp.float32), axis=axis)
    pad = [(0, 0)] * x.ndim
    pad[axis] = (1, 0)
    s = jnp.pad(s, pad)
    t = jnp.arange(n)
    lo = jnp.clip(t - w // 2, 0, n)
    hi = jnp.clip(t + (w - w // 2), 0, n)
    total = jnp.take(s, hi, axis=axis) - jnp.take(s, lo, axis=axis)
    shape = [1] * x.ndim
    shape[axis] = n
    cnt = (hi - lo).astype(jnp.float32).reshape(shape)
    return total / cnt


def pool_branch(xp, w_pool, pool_scale, grid):
    B, T, _ = xp.shape
    groups = jnp.split(xp, N_POOL_GROUPS, axis=-1)
    outs = []
    for g, win in zip(groups, POOL_WINDOWS):
        if grid:
            rows = T // GRID_W
            gg = g.reshape(B, rows, GRID_W, POOL_GROUP)
            pooled = centred_mean(centred_mean(gg, win, 2), win, 1).reshape(B, T, POOL_GROUP)
        else:
            pooled = centred_mean(g, win, 1)
        outs.append(pooled - g.astype(jnp.float32))
    d = jnp.stack(outs, axis=2).astype(xp.dtype)
    y = jnp.einsum('btgc,gcd->btgd', d, w_pool).reshape(B, T, D_POOL)
    return y * pool_scale


def sgu_branch(u, v, g_sgu, w_sgu, b_sgu):
    B, T, _ = u.shape
    vn = rmsnorm(v, g_sgu).reshape(B, T // CHUNK, CHUNK, N_SGU_GROUPS, SGU_GROUP)
    mixed = jnp.einsum('gpq,bnqgc->bnpgc', w_sgu, vn) + jnp.swapaxes(b_sgu, 0, 1)[None, None, :, :, None]
    return u * mixed.reshape(B, T, D_SGU)


def mlstm_scan(q, k, v, i_pre, logf, C0, n0, m0):
    B, T, H, Dh = q.shape
    L = MLSTM_CHUNK
    nc = T // L

    def to_chunks(a):
        return jnp.moveaxis(a.reshape(B, nc, L, *a.shape[2:]), 1, 0)

    xs = (to_chunks(q), to_chunks(k), to_chunks(v), to_chunks(i_pre), to_chunks(logf))
    causal = jnp.tril(jnp.ones((L, L), dtype=bool))

    def step(carry, xc):
        C, n, m = carry
        qc, kc, vc, ic, fc = xc
        b = jnp.cumsum(fc, axis=1)
        dmat = b[:, :, None, :] - b[:, None, :, :] + ic[:, None, :, :]
        dmat = jnp.where(causal[None, :, :, None], dmat, -jnp.inf)
        inter = b + m[:, None, :]
        m_t = jnp.maximum(inter, jnp.max(dmat, axis=2))
        w_intra = jnp.exp(dmat - m_t[:, :, None, :])
        w_inter = jnp.exp(inter - m_t)
        s = jnp.einsum('bthd,bshd->btsh', qc, kc) * w_intra
        num = jnp.einsum('btsh,bshd->bthd', s, vc) + w_inter[..., None] * jnp.einsum('bthd,bhde->bthe', qc, C)
        den = jnp.sum(s, axis=2) + w_inter * jnp.einsum('bthd,bhd->bth', qc, n)
        h = num / jnp.maximum(jnp.abs(den), jnp.exp(-m_t))[..., None]
        bL = b[:, -1, :]
        m_new = m_t[:, -1, :]
        w_state = jnp.exp(bL[:, None, :] - b + ic - m_new[:, None, :])
        decay = jnp.exp(bL + m - m_new)
        C_new = decay[..., None, None] * C + jnp.einsum('bsh,bshd,bshe->bhde', w_state, kc, vc)
        n_new = decay[..., None] * n + jnp.einsum('bsh,bshd->bhd', w_state, kc)
        return (C_new, n_new, m_new), h

    (C, n, m), hs = lax.scan(step, (C0, n0, m0), xs)
    h = jnp.moveaxis(hs, 0, 1).reshape(B, T, H, Dh)
    return h, C, n, m


def mlstm_branch(q, k, v, o_pre, gate_pre, b_gates, g_mlstm, C0, n0, m0):
    B, T, _ = q.shape
    f32 = jnp.float32

    def heads(a):
        return a.astype(f32).reshape(B, T, N_MLSTM_HEADS, HEAD_DIM)

    qh, kh, vh = heads(q), heads(k) * (HEAD_DIM ** -0.5), heads(v)
    gp = gate_pre.astype(f32) + b_gates.astype(f32)
    i_f, i_b, f_f, f_b = jnp.split(gp, 4, axis=-1)
    logf_f = jax.nn.log_sigmoid(f_f)
    logf_b = jax.nn.log_sigmoid(f_b)
    C0, n0, m0 = C0.astype(f32), n0.astype(f32), m0.astype(f32)
    h_f, Cf, nf, mf = mlstm_scan(qh, kh, vh, i_f, logf_f, C0[:, 0], n0[:, 0], m0[:, 0])

    def flip(a):
        return jnp.flip(a, axis=1)

    h_b, Cb, nb, mb = mlstm_scan(flip(qh), flip(kh), flip(vh), flip(i_b), flip(logf_b), C0[:, 1], n0[:, 1], m0[:, 1])
    h = h_f + flip(h_b)
    h = h * lax.rsqrt(jnp.mean(h * h, axis=-1, keepdims=True) + EPS) * g_mlstm.astype(f32).reshape(N_MLSTM_HEADS, HEAD_DIM)
    y = jax.nn.sigmoid(o_pre.astype(f32)) * h.reshape(B, T, D_MLSTM)
    C_out = jnp.stack([Cf, Cb], axis=1)
    n_out = jnp.stack([nf, nb], axis=1)
    m_out = jnp.stack([mf, mb], axis=1)
    return y.astype(q.dtype), C_out, n_out, m_out


def trunk_layer(x, cond, grid, C0, n0, m0, w_ada, b_ada, g_norm1, g_norm2, w_in, b_gates, w_pool, pool_scale,
                g_sgu, w_sgu, b_sgu, g_mlstm, w_branch, w_out, w_ff1, w_ff2):
    B, T, _ = x.shape
    mod = (jax.nn.silu(cond) @ w_ada + b_ada)[:, None, :]
    sh1, sc1, gt1, sh2, sc2, gt2 = jnp.split(mod, 6, axis=-1)
    u = rmsnorm(x, g_norm1) * (1 + sc1) + sh1
    proj = u @ w_in
    points = np.cumsum(IN_WIDTHS).tolist()
    xp, su, sv, q, k, v, o_pre, gate_pre, br_pre = jnp.split(proj, points, axis=-1)
    y_a = pool_branch(xp, w_pool, pool_scale, grid)
    y_b = sgu_branch(su, sv, g_sgu, w_sgu, b_sgu)
    y_c, C_out, n_out, m_out = mlstm_branch(q, k, v, o_pre, gate_pre, b_gates, g_mlstm, C0, n0, m0)
    ys = jnp.stack([y_a.astype(x.dtype), y_b.astype(x.dtype), y_c.astype(x.dtype)], axis=2)
    branches = jnp.einsum('btrc,rcd->btrd', ys, w_branch)
    gates = jax.nn.sigmoid(br_pre).reshape(B, T, N_BRANCH, D_MODEL)
    merged = jnp.sum(gates * branches, axis=2)
    x = x + gt1 * (merged @ w_out)
    u2 = rmsnorm(x, g_norm2) * (1 + sc2) + sh2
    x = x + gt2 * (jnp.square(jax.nn.relu(u2 @ w_ff1)) @ w_ff2)
    return x, C_out, n_out, m_out


def setup_inputs(seed: int = 0) -> dict:
    key = jax.random.key(seed)
    ks = jax.random.split(key, 25)
    f32 = jnp.float32
    H, Dh, D = N_MLSTM_HEADS, HEAD_DIM, D_MODEL

    def nrm(k, shape, s):
        return jax.random.normal(k, shape, f32) * s

    f_bias = 3.0 + 3.0 * jnp.linspace(0.0, 1.0, H, dtype=f32)
    b_i = nrm(ks[12], (DEPTH, 2 * H), 0.1)
    b_f = jnp.tile(f_bias, 2)[None, :] + nrm(ks[13], (DEPTH, 2 * H), 0.1)
    return {
        "x_prompt": nrm(ks[0], (BATCH, SEQ, D), 1.0),
        "x_sample": nrm(ks[1], (DEC_BATCH, DEC_SEQ, D), 1.0),
        "state_C": nrm(ks[2], (DEC_BATCH, DEPTH, 2, H, Dh, Dh), 0.1),
        "state_n": nrm(ks[3], (DEC_BATCH, DEPTH, 2, H, Dh), 0.1),
        "state_m": 1.0 + nrm(ks[4], (DEC_BATCH, DEPTH, 2, H), 0.5),
        "c": nrm(ks[5], (DEC_BATCH, D), 1.0),
        "c_ctx": nrm(ks[6], (D,), 1.0),
        "w_ada": nrm(ks[7], (DEPTH, D, 6 * D), 0.5 * D ** -0.5),
        "b_ada": nrm(ks[8], (DEPTH, 6 * D), 0.01),
        "g_norm1": 1.0 + nrm(ks[9], (DEPTH, D), 0.05),
        "g_norm2": 1.0 + nrm(ks[10], (DEPTH, D), 0.05),
        "w_in": nrm(ks[11], (DEPTH, D, D_IN), D ** -0.5),
        "b_gates": jnp.concatenate([b_i, b_f], axis=-1),
        "w_pool": nrm(ks[14], (DEPTH, N_POOL_GROUPS, POOL_GROUP, POOL_GROUP), POOL_GROUP ** -0.5),
        "pool_scale": 1.0 + nrm(ks[15], (DEPTH, D_POOL), 0.1),
        "g_sgu": 1.0 + nrm(ks[16], (DEPTH, D_SGU), 0.05),
        "w_sgu": nrm(ks[17], (DEPTH, N_SGU_GROUPS, CHUNK, CHUNK), CHUNK ** -0.5),
        "b_sgu": 1.0 + nrm(ks[18], (DEPTH, N_SGU_GROUPS, CHUNK), 0.1),
        "g_mlstm": 1.0 + nrm(ks[19], (DEPTH, D_MLSTM), 0.05),
        "w_branch": nrm(ks[20], (DEPTH, N_BRANCH, D_BRANCH, D), D_BRANCH ** -0.5),
        "w_out": nrm(ks[21], (DEPTH, D, D), D ** -0.5),
        "w_ff1": nrm(ks[22], (DEPTH, D, D_FF), D ** -0.5),
        "w_ff2": nrm(ks[23], (DEPTH, D_FF, D), D_FF ** -0.5),
        "g_final": 1.0 + nrm(ks[24], (D,), 0.05),
    }


def reference(x_prompt, x_sample, state_C, state_n, state_m, c, c_ctx, w_ada, b_ada, g_norm1, g_norm2, w_in,
              b_gates, w_pool, pool_scale, g_sgu, w_sgu, b_sgu, g_mlstm, w_branch, w_out, w_ff1, w_ff2, g_final):
    B = x_prompt.shape[0]
    H, Dh = N_MLSTM_HEADS, HEAD_DIM
    ctx_cond = c_ctx[None, :]
    C_zero = jnp.zeros((B, 2, H, Dh, Dh), jnp.float32)
    n_zero = jnp.zeros((B, 2, H, Dh), jnp.float32)
    m_zero = jnp.zeros((B, 2, H), jnp.float32)
    xc, xs = x_prompt, x_sample
    new_C, new_n, new_m = [], [], []
    for l in range(DEPTH):
        lp = (w_ada[l], b_ada[l], g_norm1[l], g_norm2[l], w_in[l], b_gates[l], w_pool[l], pool_scale[l],
              g_sgu[l], w_sgu[l], b_sgu[l], g_mlstm[l], w_branch[l], w_out[l], w_ff1[l], w_ff2[l])
        xc, C_l, n_l, m_l = trunk_layer(xc, ctx_cond, False, C_zero, n_zero, m_zero, *lp)
        new_C.append(C_l)
        new_n.append(n_l)
        new_m.append(m_l)
        xs, _, _, _ = trunk_layer(xs, c, True, state_C[:, l], state_n[:, l], state_m[:, l], *lp)
    y_prompt = rmsnorm(xc, g_final)
    y_sample = rmsnorm(xs, g_final)
    new_C = jnp.stack(new_C, axis=1)
    new_n = jnp.stack(new_n, axis=1)
    new_m = jnp.stack(new_m, axis=1)
    return (y_prompt, y_sample, new_C, new_n, new_m)
```

```python
import functools

import jax
import jax.numpy as jnp
from jax import lax
from jax.experimental import pallas as pl
from jax.experimental.pallas import tpu as pltpu

F32 = jnp.float32
BF16 = jnp.bfloat16

D_MODEL = 1024
DEPTH = 4
BATCH = 16
SEQ = 256
DEC_BATCH = 2
DEC_SEQ = 1024
GRID_W = 64
D_BRANCH = 512
POOL_WINDOWS = (2, 4, 8, 16)
N_GROUPS = 4
GROUP = D_BRANCH // N_GROUPS
SGU_CHUNK = 128
N_HEADS = 4
HEAD_DIM = D_BRANCH // N_HEADS
N_BRANCH = 3
D_FF = 4 * D_MODEL
EPS = 1e-6

N_MAIN = 7 * D_BRANCH
N_GATE = 4 * N_HEADS
GATE_PAD = 128
N_TOK_CTX = BATCH * SEQ
N_TOK = N_TOK_CTX + DEC_BATCH * DEC_SEQ
COND_ROWS = 8

TILE_M = 256
MLSTM_L = 256
POOL_PAD = 8
FFN_TILE_M = 512
FFN_CHUNK = 512
ADA_TILE_N = 1536
VMEM_LIMIT_BYTES = 60 * 1024 * 1024


def _dot(a, b):
    return jnp.dot(a, b, preferred_element_type=F32)


def _dot_nt(a, b):
    return lax.dot_general(a, b, (((1,), (1,)), ((), ())), preferred_element_type=F32)


def _ada_kernel(cond_ref, w_ref, b_ref, out_ref):
    cond = cond_ref[...]
    s = (cond * jax.nn.sigmoid(cond)).astype(BF16)
    out_ref[...] = _dot(s, w_ref[...].astype(BF16)) + b_ref[...]


def _ada_call(cond8, w_ada, b_ada):
    n_out = 6 * D_MODEL
    return pl.pallas_call(
        _ada_kernel,
        grid=(DEPTH, n_out // ADA_TILE_N),
        in_specs=[
            pl.BlockSpec((COND_ROWS, D_MODEL), lambda l, j: (0, 0)),
            pl.BlockSpec((None, D_MODEL, ADA_TILE_N), lambda l, j: (l, 0, j)),
            pl.BlockSpec((None, 1, ADA_TILE_N), lambda l, j: (l, 0, j)),
        ],
        out_specs=pl.BlockSpec((None, COND_ROWS, ADA_TILE_N), lambda l, j: (l, 0, j)),
        out_shape=jax.ShapeDtypeStruct((DEPTH, COND_ROWS, n_out), F32),
        compiler_params=pltpu.CompilerParams(
            dimension_semantics=("arbitrary", "arbitrary"), vmem_limit_bytes=VMEM_LIMIT_BYTES),
        name="ada_mod",
    )(cond8, w_ada, b_ada.reshape(DEPTH, 1, n_out))


def _seg_scan(x, pos, seg, total, op, reverse):
    d = 1
    while d < seg:
        if reverse:
            shifted = pltpu.roll(x, total - d, 1)
            valid = pos < seg - d
        else:
            shifted = pltpu.roll(x, d, 1)
            valid = pos >= d
        if op == "sum":
            x = x + jnp.where(valid, shifted, 0.0)
        else:
            x = jnp.maximum(x, jnp.where(valid, shifted, -jnp.inf))
        d *= 2
    return x


def _mixer_kernel(*refs, T, grid2d, has_state):
    it = iter(refs)
    x_ref = next(it)
    mod_ref = next(it)
    g1_ref = next(it)
    wmain_ref = next(it)
    wbrg_ref = next(it)
    wbranch_ref = next(it)
    wout_ref = next(it)
    wpool_ref = next(it)
    pscale_ref = next(it)
    gsgu_ref = next(it)
    wsgu_ref = next(it)
    bsgu_ref = next(it)
    bgates_ref = next(it)
    gml_ref = next(it)
    if has_state:
        c0_ref, n0_ref, m0_ref = next(it), next(it), next(it)
    else:
        next(it), next(it), next(it)
    out_ref = next(it)
    if not has_state:
        cout_ref, nout_ref, mout_ref = next(it), next(it), next(it)
    (u_scr, xp_scr, su_scr, vn_scr, q_scr, k_scr, v_scr, o_scr, gate_scr, grow_scr, colf_scr,
     hf_scr, hb_scr, ya_scr, yb_scr, yc_scr, merged_scr) = [next(it) for _ in range(17)]
    if has_state:
        c_scr, n_scr = next(it), next(it)
    if grid2d:
        m1_scr = next(it)

    n_tiles = T // TILE_M
    L = MLSTM_L
    n_chunks = T // L

    mod = mod_ref[...]
    sh1 = mod[:, 0:D_MODEL]
    sc1 = mod[:, D_MODEL:2 * D_MODEL]
    gt1 = mod[:, 2 * D_MODEL:3 * D_MODEL]
    g1 = g1_ref[...]

    def phase_a(i, carry):
        r0 = pl.multiple_of(i * TILE_M, TILE_M)
        rows = pl.ds(r0, TILE_M)
        x = x_ref[rows, :]
        ms = jnp.mean(x * x, axis=-1, keepdims=True)
        u = x * lax.rsqrt(ms + EPS) * g1 * (1.0 + sc1) + sh1
        ub = u.astype(BF16)
        u_scr[rows, :] = ub

        def proj(k):
            return _dot(ub, wmain_ref[:, k * D_BRANCH:(k + 1) * D_BRANCH])

        xp_scr[pl.ds(pl.multiple_of(r0 + POOL_PAD, 8), TILE_M), :] = proj(0)
        su_scr[...] = proj(1)
        sv = proj(2)
        vn = sv * lax.rsqrt(jnp.mean(sv * sv, axis=-1, keepdims=True) + EPS) * gsgu_ref[...]
        vn_scr[...] = vn.astype(BF16)
        q_scr[rows, :] = proj(3).astype(BF16)
        k_scr[rows, :] = (proj(4) * (HEAD_DIM ** -0.5)).astype(BF16)
        v_scr[rows, :] = proj(5).astype(BF16)
        o_scr[rows, :] = proj(6)
        gate_scr[rows, :] = _dot(ub, wmain_ref[:, N_MAIN:N_MAIN + GATE_PAD])

        for ch in range(TILE_M // SGU_CHUNK):
            crow = slice(ch * SGU_CHUNK, (ch + 1) * SGU_CHUNK)
            for g in range(N_GROUPS):
                gc = slice(g * GROUP, (g + 1) * GROUP)
                mixed = _dot(wsgu_ref[g], vn_scr[crow, gc]) + bsgu_ref[:, gc]
                yb = su_scr[crow, gc] * mixed
                yb_scr[pl.ds(pl.multiple_of(r0 + ch * SGU_CHUNK, SGU_CHUNK), SGU_CHUNK), gc] = yb.astype(BF16)
        return carry

    lax.fori_loop(0, n_tiles, phase_a, 0)

    zpad = jnp.zeros((POOL_PAD, D_BRANCH), F32)
    xp_scr[0:POOL_PAD, :] = zpad
    xp_scr[T + POOL_PAD:T + 2 * POOL_PAD, :] = zpad
    pscale = pscale_ref[...]

    if not grid2d:
        t_idx = lax.broadcasted_iota(jnp.int32, (T, GROUP), 0)
        for g, win in enumerate(POOL_WINDOWS):
            gc = slice(g * GROUP, (g + 1) * GROUP)
            half = win // 2
            total = jnp.zeros((T, GROUP), F32)
            for j in range(-half, win - half):
                total = total + xp_scr[POOL_PAD + j:POOL_PAD + j + T, gc]
            cnt = (jnp.minimum(t_idx + (win - half), T) - jnp.maximum(t_idx - half, 0)).astype(F32)
            dlt = total / cnt - xp_scr[POOL_PAD:POOL_PAD + T, gc]
            ya = _dot(dlt.astype(BF16), wpool_ref[g]) * pscale[:, gc]
            ya_scr[:, gc] = ya.astype(BF16)
    else:
        n_rows = T // GRID_W
        c_idx = lax.broadcasted_iota(jnp.int32, (T, GROUP), 0) & (GRID_W - 1)
        for g, win in enumerate(POOL_WINDOWS):
            gc = slice(g * GROUP, (g + 1) * GROUP)
            half = win // 2
            total = jnp.zeros((T, GROUP), F32)
            for j in range(-half, win - half):
                valid = (c_idx + j >= 0) & (c_idx + j < GRID_W)
                total = total + jnp.where(valid, xp_scr[POOL_PAD + j:POOL_PAD + j + T, gc], 0.0)
            cnt = (jnp.minimum(c_idx + (win - half), GRID_W) - jnp.maximum(c_idx - half, 0)).astype(F32)
            m1_scr[...] = total / cnt
            for r in range(n_rows):
                lo = max(r - half, 0)
                hi = min(r + (win - half), n_rows)
                acc = m1_scr[lo * GRID_W:(lo + 1) * GRID_W, :]
                for rr in range(lo + 1, hi):
                    acc = acc + m1_scr[rr * GRID_W:(rr + 1) * GRID_W, :]
                xg = xp_scr[POOL_PAD + r * GRID_W:POOL_PAD + (r + 1) * GRID_W, gc]
                dlt = acc / float(hi - lo) - xg
                ya = _dot(dlt.astype(BF16), wpool_ref[g]) * pscale[:, gc]
                ya_scr[r * GRID_W:(r + 1) * GRID_W, gc] = ya.astype(BF16)

    gate_t = gate_scr[...].T
    gp = gate_t[0:N_GATE, :] + bgates_ref[...]
    i_all = gp[0:2 * N_HEADS, :]
    f_all = gp[2 * N_HEADS:4 * N_HEADS, :]
    logf = jnp.minimum(f_all, 0.0) - jnp.log1p(jnp.exp(-jnp.abs(f_all)))
    lane = lax.broadcasted_iota(jnp.int32, (2 * N_HEADS, T), 1)
    pos = lane & (L - 1)
    is_fwd = lax.broadcasted_iota(jnp.int32, (2 * N_HEADS, T), 0) < N_HEADS
    b_row = jnp.where(is_fwd, _seg_scan(logf, pos, L, T, "sum", False), _seg_scan(logf, pos, L, T, "sum", True))
    g_row = i_all - b_row
    cm_row = jnp.where(is_fwd, _seg_scan(g_row, pos, L, T, "max", False), _seg_scan(g_row, pos, L, T, "max", True))
    grow_scr[...] = g_row
    stacked = jnp.concatenate(
        [g_row, b_row, cm_row, jnp.zeros((GATE_PAD - 6 * N_HEADS, T), F32)], axis=0)
    colf_scr[...] = stacked.T

    if has_state:
        c_scr[...] = c0_ref[...]
        n_scr[...] = n0_ref[...]
        m_init = m0_ref[...]
        m_start = (m_init[0:1, :], m_init[1:2, :])
    else:
        m_start = (jnp.zeros((1, N_HEADS), F32), jnp.zeros((1, N_HEADS), F32))

    t_i = lax.broadcasted_iota(jnp.int32, (L, L), 0)
    s_i = lax.broadcasted_iota(jnp.int32, (L, L), 1)
    masks = (s_i <= t_i, s_i >= t_i)

    def mlstm_step(j, carry):
        new_m = []
        for d in range(2):
            c = j if d == 0 else n_chunks - 1 - j
            c0 = c * L if isinstance(c, int) else pl.multiple_of(c * L, L)
            rows = pl.ds(c0, L)
            h_scr = hf_scr if d == 0 else hb_scr
            colf = colf_scr[rows, :]
            gcol = colf[:, N_HEADS * d:N_HEADS * (d + 1)]
            bcol = colf[:, 2 * N_HEADS + N_HEADS * d:2 * N_HEADS + N_HEADS * (d + 1)]
            cmcol = colf[:, 4 * N_HEADS + N_HEADS * d:4 * N_HEADS + N_HEADS * (d + 1)]
            m_in = carry[d]
            mx = jnp.maximum(m_in, cmcol)
            last = L - 1 if d == 0 else 0
            mx_last = mx[last:last + 1, :]
            w_inter = jnp.exp(m_in - mx)
            e_negm = jnp.exp(-(bcol + mx))
            w_state = jnp.exp(gcol - mx_last)
            decay = jnp.exp(m_in - mx_last)
            new_m.append(bcol[last:last + 1, :] + mx_last)
            for h in range(N_HEADS):
                hc = slice(h * HEAD_DIM, (h + 1) * HEAD_DIM)
                qh = q_scr[rows, hc]
                kh = k_scr[rows, hc]
                vh = v_scr[rows, hc]
                qk = _dot_nt(qh, kh)
                arg = grow_scr[N_HEADS * d + h:N_HEADS * d + h + 1, rows] - mx[:, h:h + 1]
                s = qk * jnp.where(masks[d], jnp.exp(arg), 0.0)
                den = jnp.sum(s, axis=1, keepdims=True)
                num = _dot(s.astype(BF16), vh)
                if has_state:
                    c_dh = c_scr[d, h]
                    n_dh = n_scr[d, h:h + 1, :]
                    num = num + w_inter[:, h:h + 1] * _dot(qh, c_dh.astype(BF16))
                    den = den + w_inter[:, h:h + 1] * jnp.sum(qh.astype(F32) * n_dh, axis=1, keepdims=True)
                h_scr[rows, hc] = num / jnp.maximum(jnp.abs(den), e_negm[:, h:h + 1])
                kw = kh.astype(F32) * w_state[:, h:h + 1]
                c_new = _dot(kw.T.astype(BF16), vh)
                n_new = jnp.sum(kw, axis=0, keepdims=True)
                if has_state:
                    c_scr[d, h] = decay[:, h:h + 1] * c_dh + c_new
                    n_scr[d, h:h + 1, :] = decay[:, h:h + 1] * n_dh + n_new
                else:
                    cout_ref[d, h] = c_new
                    nout_ref[d, h:h + 1, :] = n_new
        return tuple(new_m)

    if n_chunks == 1:
        m_end = mlstm_step(0, m_start)
    else:
        m_end = lax.fori_loop(0, n_chunks, mlstm_step, m_start)
    if not has_state:
        mout_ref[0:1, :] = m_end[0]
        mout_ref[1:2, :] = m_end[1]

    gml = gml_ref[...]

    def phase_c(i, carry):
        r0 = pl.multiple_of(i * TILE_M, TILE_M)
        rows = pl.ds(r0, TILE_M)
        for h in range(N_HEADS):
            hc = slice(h * HEAD_DIM, (h + 1) * HEAD_DIM)
            hh = hf_scr[rows, hc] + hb_scr[rows, hc]
            hn = hh * lax.rsqrt(jnp.mean(hh * hh, axis=-1, keepdims=True) + EPS) * gml[:, hc]
            yc_scr[rows, hc] = (jax.nn.sigmoid(o_scr[rows, hc]) * hn).astype(BF16)
        ub = u_scr[rows, :]
        y_refs = (ya_scr, yb_scr, yc_scr)
        nw = 256
        for nj in range(D_MODEL // nw):
            nc = slice(nj * nw, (nj + 1) * nw)
            acc = jnp.zeros((TILE_M, nw), F32)
            for r in range(N_BRANCH):
                gate = jax.nn.sigmoid(_dot(ub, wbrg_ref[:, r * D_MODEL + nj * nw:r * D_MODEL + (nj + 1) * nw]))
                acc = acc + gate * _dot(y_refs[r][rows, :], wbranch_ref[r, :, nc])
            merged_scr[:, nc] = acc.astype(BF16)
        out_ref[rows, :] = x_ref[rows, :] + gt1 * _dot(merged_scr[...], wout_ref[...])
        return carry

    lax.fori_loop(0, n_tiles, phase_c, 0)


def _mixer_call(l, xall, mods4, prm, stream, states=None, state_outs=None):
    ctx = stream == "ctx"
    T = SEQ if ctx else DEC_SEQ
    n_b = BATCH if ctx else DEC_BATCH
    blk0 = 0 if ctx else N_TOK_CTX // T
    once = pl.Buffered(1)

    def wspec(shape):
        nd = len(shape)
        return pl.BlockSpec((None,) + shape, lambda b: (l,) + (0,) * nd, pipeline_mode=once)

    x_spec = pl.BlockSpec((T, D_MODEL), lambda b: (blk0 + b, 0), pipeline_mode=None if ctx else once)
    if ctx:
        mod_spec = pl.BlockSpec((None, None, 1, 6 * D_MODEL), lambda b: (l, 0, 0, 0))
    else:
        mod_spec = pl.BlockSpec((None, None, 1, 6 * D_MODEL), lambda b: (l, 1 + b, 0, 0))
    in_specs = [
        x_spec, mod_spec,
        wspec((1, D_MODEL)),
        wspec((D_MODEL, N_MAIN + GATE_PAD)),
        wspec((D_MODEL, N_BRANCH * D_MODEL)),
        wspec((N_BRANCH, D_BRANCH, D_MODEL)),
        wspec((D_MODEL, D_MODEL)),
        wspec((N_GROUPS, GROUP, GROUP)),
        wspec((1, D_BRANCH)),
        wspec((1, D_BRANCH)),
        wspec((N_GROUPS, SGU_CHUNK, SGU_CHUNK)),
        wspec((SGU_CHUNK, D_BRANCH)),
        wspec((N_GATE, 1)),
        wspec((1, D_BRANCH)),
    ]
    args = [xall, mods4, prm["g_norm1"], prm["w_main"], prm["w_brg"], prm["w_branch"], prm["w_out"],
            prm["w_pool"], prm["pool_scale"], prm["g_sgu"], prm["w_sgu"], prm["b_sgu_tile"],
            prm["b_gates"], prm["g_mlstm"]]
    out_shape = [jax.ShapeDtypeStruct(xall.shape, xall.dtype)]
    out_specs = [x_spec]
    aliases = {0: 0}
    if ctx:
        new_c, new_n, new_m = state_outs
        in_specs += [pl.BlockSpec(memory_space=pl.ANY)] * 3
        aliases.update({len(args): 1, len(args) + 1: 2, len(args) + 2: 3})
        args += [new_c, new_n, new_m]
        out_shape += [jax.ShapeDtypeStruct(a.shape, a.dtype) for a in state_outs]
        out_specs += [
            pl.BlockSpec((None, None, 2, N_HEADS, HEAD_DIM, HEAD_DIM), lambda b: (b, l, 0, 0, 0, 0)),
            pl.BlockSpec((None, None, 2, N_HEADS, HEAD_DIM), lambda b: (b, l, 0, 0, 0)),
            pl.BlockSpec((None, None, 2, N_HEADS), lambda b: (b, l, 0, 0)),
        ]
    else:
        state_c, state_n, state_m = states
        in_specs += [
            pl.BlockSpec((None, None, 2, N_HEADS, HEAD_DIM, HEAD_DIM), lambda b: (b, l, 0, 0, 0, 0)),
            pl.BlockSpec((None, None, 2, N_HEADS, HEAD_DIM), lambda b: (b, l, 0, 0, 0)),
            pl.BlockSpec((None, None, 2, N_HEADS), lambda b: (b, l, 0, 0)),
        ]
        args += [state_c, state_n, state_m]

    scratch = [
        pltpu.VMEM((T, D_MODEL), BF16),
        pltpu.VMEM((T + 2 * POOL_PAD, D_BRANCH), F32),
        pltpu.VMEM((TILE_M, D_BRANCH), F32),
        pltpu.VMEM((TILE_M, D_BRANCH), BF16),
        pltpu.VMEM((T, D_BRANCH), BF16),
        pltpu.VMEM((T, D_BRANCH), BF16),
        pltpu.VMEM((T, D_BRANCH), BF16),
        pltpu.VMEM((T, D_BRANCH), F32),
        pltpu.VMEM((T, GATE_PAD), F32),
        pltpu.VMEM((2 * N_HEADS, T), F32),
        pltpu.VMEM((T, GATE_PAD), F32),
        pltpu.VMEM((T, D_BRANCH), F32),
        pltpu.VMEM((T, D_BRANCH), F32),
        pltpu.VMEM((T, D_BRANCH), BF16),
        pltpu.VMEM((T, D_BRANCH), BF16),
        pltpu.VMEM((T, D_BRANCH), BF16),
        pltpu.VMEM((TILE_M, D_MODEL), BF16),
    ]
    if not ctx:
        scratch += [pltpu.VMEM((2, N_HEADS, HEAD_DIM, HEAD_DIM), F32), pltpu.VMEM((2, N_HEADS, HEAD_DIM), F32)]
        scratch += [pltpu.VMEM((T, GROUP), F32)]

    outs = pl.pallas_call(
        functools.partial(_mixer_kernel, T=T, grid2d=not ctx, has_state=not ctx),
        grid=(n_b,),
        in_specs=in_specs,
        out_specs=out_specs,
        out_shape=out_shape,
        scratch_shapes=scratch,
        input_output_aliases=aliases,
        compiler_params=pltpu.CompilerParams(
            dimension_semantics=("arbitrary",), vmem_limit_bytes=VMEM_LIMIT_BYTES),
        name="mixer_ctx" if ctx else "mixer_lat",
    )(*args)
    return outs


def _ffn_kernel(x_ref, mod_ref, g2_ref, w1_ref, w2_ref, gfin_ref, out_ref, acc_scr, *, final):
    mod = mod_ref[...]
    sh2 = mod[:, 3 * D_MODEL:4 * D_MODEL]
    sc2 = mod[:, 4 * D_MODEL:5 * D_MODEL]
    gt2 = mod[:, 5 * D_MODEL:6 * D_MODEL]
    x = x_ref[...]
    ms = jnp.mean(x * x, axis=-1, keepdims=True)
    ub = (x * lax.rsqrt(ms + EPS) * g2_ref[...] * (1.0 + sc2) + sh2).astype(BF16)
    acc_scr[...] = jnp.zeros_like(acc_scr)

    def chunk(c, carry):
        c0 = pl.multiple_of(c * FFN_CHUNK, FFN_CHUNK)
        h = _dot(ub, w1_ref[:, pl.ds(c0, FFN_CHUNK)])
        h = jnp.square(jnp.maximum(h, 0.0))
        acc_scr[...] += _dot(h.astype(BF16), w2_ref[pl.ds(c0, FFN_CHUNK), :])
        return carry

    lax.fori_loop(0, D_FF // FFN_CHUNK, chunk, 0)
    y = x + gt2 * acc_scr[...]
    if final:
        y = y * lax.rsqrt(jnp.mean(y * y, axis=-1, keepdims=True) + EPS) * gfin_ref[...]
    out_ref[...] = y


def _ffn_call(l, xall, mods4, prm, g_final, final):
    ctx_tiles = N_TOK_CTX // FFN_TILE_M
    lat_tiles = DEC_SEQ // FFN_TILE_M
    once = pl.Buffered(1)

    def mod_map(t):
        return (l, jnp.where(t < ctx_tiles, 0, 1 + (t - ctx_tiles) // lat_tiles), 0, 0)

    x_spec = pl.BlockSpec((FFN_TILE_M, D_MODEL), lambda t: (t, 0))
    return pl.pallas_call(
        functools.partial(_ffn_kernel, final=final),
        grid=(N_TOK // FFN_TILE_M,),
        in_specs=[
            x_spec,
            pl.BlockSpec((None, None, 1, 6 * D_MODEL), mod_map),
            pl.BlockSpec((None, 1, D_MODEL), lambda t: (l, 0, 0), pipeline_mode=once),
            pl.BlockSpec((None, D_MODEL, D_FF), lambda t: (l, 0, 0), pipeline_mode=once),
            pl.BlockSpec((None, D_FF, D_MODEL), lambda t: (l, 0, 0), pipeline_mode=once),
            pl.BlockSpec((1, D_MODEL), lambda t: (0, 0)),
        ],
        out_specs=x_spec,
        out_shape=jax.ShapeDtypeStruct(xall.shape, xall.dtype),
        scratch_shapes=[pltpu.VMEM((FFN_TILE_M, D_MODEL), F32)],
        input_output_aliases={0: 0},
        compiler_params=pltpu.CompilerParams(
            dimension_semantics=("arbitrary",), vmem_limit_bytes=VMEM_LIMIT_BYTES),
        name="ffn_final" if final else "ffn",
    )(xall, mods4, prm["g_norm2"], prm["w_ff1"], prm["w_ff2"], g_final.reshape(1, D_MODEL))


def kernel(x_prompt, x_sample, state_C, state_n, state_m, c, c_ctx, w_ada, b_ada, g_norm1, g_norm2, w_in,
           b_gates, w_pool, pool_scale, g_sgu, w_sgu, b_sgu, g_mlstm, w_branch, w_out, w_ff1, w_ff2, g_final):
    gate_cols = jnp.pad(w_in[:, :, N_MAIN:N_MAIN + N_GATE], ((0, 0), (0, 0), (0, GATE_PAD - N_GATE)))
    prm = {
        "w_main": jnp.concatenate([w_in[:, :, :N_MAIN], gate_cols], axis=-1).astype(BF16),
        "w_brg": w_in[:, :, N_MAIN + N_GATE:].astype(BF16),
        "w_branch": w_branch.astype(BF16),
        "w_out": w_out.astype(BF16),
        "w_pool": w_pool.astype(BF16),
        "w_sgu": w_sgu.astype(BF16),
        "w_ff1": w_ff1.astype(BF16),
        "w_ff2": w_ff2.astype(BF16),
        "g_norm1": g_norm1.reshape(DEPTH, 1, D_MODEL),
        "g_norm2": g_norm2.reshape(DEPTH, 1, D_MODEL),
        "pool_scale": pool_scale.reshape(DEPTH, 1, D_BRANCH),
        "g_sgu": g_sgu.reshape(DEPTH, 1, D_BRANCH),
        "g_mlstm": g_mlstm.reshape(DEPTH, 1, D_BRANCH),
        "b_gates": b_gates.reshape(DEPTH, N_GATE, 1),
        "b_sgu_tile": jnp.repeat(jnp.swapaxes(b_sgu, 1, 2), GROUP, axis=2),
    }
    cond8 = jnp.zeros((COND_ROWS, D_MODEL), F32).at[0].set(c_ctx).at[1:1 + DEC_BATCH].set(c)
    mods = _ada_call(cond8, w_ada, b_ada)
    mods4 = mods.reshape(DEPTH, COND_ROWS, 1, 6 * D_MODEL)

    xall = jnp.concatenate([x_prompt.reshape(N_TOK_CTX, D_MODEL), x_sample.reshape(-1, D_MODEL)], axis=0)
    new_c = jnp.zeros((BATCH, DEPTH, 2, N_HEADS, HEAD_DIM, HEAD_DIM), F32)
    new_n = jnp.zeros((BATCH, DEPTH, 2, N_HEADS, HEAD_DIM), F32)
    new_m = jnp.zeros((BATCH, DEPTH, 2, N_HEADS), F32)
    for l in range(DEPTH):
        xall, new_c, new_n, new_m = _mixer_call(l, xall, mods4, prm, "ctx", state_outs=(new_c, new_n, new_m))
        (xall,) = _mixer_call(l, xall, mods4, prm, "lat", states=(state_C, state_n, state_m))
        xall = _ffn_call(l, xall, mods4, prm, g_final, final=(l == DEPTH - 1))
    y_prompt = xall[:N_TOK_CTX].reshape(BATCH, SEQ, D_MODEL)
    y_sample = xall[N_TOK_CTX:].reshape(DEC_BATCH, DEC_SEQ, D_MODEL)
    return (y_prompt, y_sample, new_c, new_n, new_m)
```

```python
import functools

import jax
import jax.numpy as jnp
from jax import lax
from jax.experimental import pallas as pl
from jax.experimental.pallas import tpu as pltpu

F32 = jnp.float32
BF16 = jnp.bfloat16

D_MODEL = 1024
DEPTH = 4
BATCH = 16
SEQ = 256
DEC_BATCH = 2
DEC_SEQ = 1024
GRID_W = 64
D_BRANCH = 512
POOL_WINDOWS = (2, 4, 8, 16)
N_GROUPS = 4
GROUP = D_BRANCH // N_GROUPS
SGU_CHUNK = 128
N_HEADS = 4
HEAD_DIM = D_BRANCH // N_HEADS
N_BRANCH = 3
D_FF = 4 * D_MODEL
EPS = 1e-6

N_MAIN = 7 * D_BRANCH
N_GATE = 4 * N_HEADS
GATE_PAD = 128
N_TOK_CTX = BATCH * SEQ
N_TOK = N_TOK_CTX + DEC_BATCH * DEC_SEQ
COND_ROWS = 8

TILE_M = 256
MLSTM_L = 256
POOL_PAD = 8
FFN_TILE_M = 512
FFN_CHUNK = 512
FFN_STAGES = 2
ADA_TILE_N = 1536
VMEM_LIMIT_BYTES = 60 * 1024 * 1024


def _dot(a, b):
    return jnp.dot(a, b, preferred_element_type=F32)


def _dot_nt(a, b):
    return lax.dot_general(a, b, (((1,), (1,)), ((), ())), preferred_element_type=F32)


def _aligned(offset, align):
    return offset if isinstance(offset, int) else pl.multiple_of(offset, align)


def _loop(n, body):
    if n == 1:
        body(0, 0)
    else:
        lax.fori_loop(0, n, body, 0)


def _ada_kernel(cond_ref, w_ref, b_ref, out_ref):
    cond = cond_ref[...]
    s = (cond * jax.nn.sigmoid(cond)).astype(BF16)
    out_ref[...] = _dot(s, w_ref[...].astype(BF16)) + b_ref[...]


def _ada_call(cond8, w_ada, b_ada):
    n_out = 6 * D_MODEL
    return pl.pallas_call(
        _ada_kernel,
        grid=(DEPTH, n_out // ADA_TILE_N),
        in_specs=[
            pl.BlockSpec((COND_ROWS, D_MODEL), lambda l, j: (0, 0)),
            pl.BlockSpec((None, D_MODEL, ADA_TILE_N), lambda l, j: (l, 0, j)),
            pl.BlockSpec((None, 1, ADA_TILE_N), lambda l, j: (l, 0, j)),
        ],
        out_specs=pl.BlockSpec((None, COND_ROWS, ADA_TILE_N), lambda l, j: (l, 0, j)),
        out_shape=jax.ShapeDtypeStruct((DEPTH, COND_ROWS, n_out), F32),
        compiler_params=pltpu.CompilerParams(
            dimension_semantics=("arbitrary", "arbitrary"), vmem_limit_bytes=VMEM_LIMIT_BYTES),
        name="ada_mod",
    )(cond8, w_ada, b_ada.reshape(DEPTH, 1, n_out))


def _seg_scan(x, pos, seg, total, op, reverse):
    d = 1
    while d < seg:
        if reverse:
            shifted = pltpu.roll(x, total - d, 1)
            valid = pos < seg - d
        else:
            shifted = pltpu.roll(x, d, 1)
            valid = pos >= d
        if op == "sum":
            x = x + jnp.where(valid, shifted, 0.0)
        else:
            x = jnp.maximum(x, jnp.where(valid, shifted, -jnp.inf))
        d *= 2
    return x


def _mixer_kernel(*refs, T, grid2d, has_state):
    it = iter(refs)
    x_ref = next(it)
    mod_ref = next(it)
    g1_ref = next(it)
    wmain_ref = next(it)
    wgate_ref = next(it)
    wbrg_ref = next(it)
    wbranch_ref = next(it)
    wout_ref = next(it)
    wpool_ref = next(it)
    pscale_ref = next(it)
    gsgu_ref = next(it)
    wsgu_ref = next(it)
    bsgu_ref = next(it)
    bgates_ref = next(it)
    gml_ref = next(it)
    if has_state:
        c0_ref, n0_ref, m0_ref = next(it), next(it), next(it)
    else:
        next(it), next(it), next(it)
    out_ref = next(it)
    if not has_state:
        cout_ref, nout_ref, mout_ref = next(it), next(it), next(it)
    (u_scr, xp_scr, su_scr, vn_scr, q_scr, k_scr, v_scr, o_scr, gate_scr, grow_scr, colf_scr,
     hf_scr, hb_scr, ya_scr, yb_scr, yc_scr, merged_scr) = [next(it) for _ in range(17)]
    if has_state:
        c_scr, n_scr = next(it), next(it)
    if grid2d:
        m1_scr = next(it)

    n_tiles = T // TILE_M
    L = MLSTM_L
    n_chunks = T // L

    mod = mod_ref[...]
    sh1 = mod[:, 0:D_MODEL]
    sc1 = mod[:, D_MODEL:2 * D_MODEL]
    gt1 = mod[:, 2 * D_MODEL:3 * D_MODEL]
    g1 = g1_ref[...]

    def phase_a(i, carry):
        r0 = _aligned(i * TILE_M, TILE_M)
        rows = pl.ds(r0, TILE_M)
        x = x_ref[rows, :]
        ms = jnp.mean(x * x, axis=-1, keepdims=True)
        u = x * lax.rsqrt(ms + EPS) * g1 * (1.0 + sc1) + sh1
        ub = u.astype(BF16)
        u_scr[rows, :] = ub

        def proj(k):
            return _dot(ub, wmain_ref[:, k * D_BRANCH:(k + 1) * D_BRANCH])

        xp_scr[pl.ds(_aligned(r0 + POOL_PAD, 8), TILE_M), :] = proj(0)
        su_scr[...] = proj(1)
        sv = proj(2)
        vn = sv * lax.rsqrt(jnp.mean(sv * sv, axis=-1, keepdims=True) + EPS) * gsgu_ref[...]
        vn_scr[...] = vn.astype(BF16)
        q_scr[rows, :] = proj(3).astype(BF16)
        k_scr[rows, :] = (proj(4) * (HEAD_DIM ** -0.5)).astype(BF16)
        v_scr[rows, :] = proj(5).astype(BF16)
        o_scr[rows, :] = proj(6)
        gate_scr[rows, :] = _dot(ub, wgate_ref[...])

        for ch in range(TILE_M // SGU_CHUNK):
            crow = slice(ch * SGU_CHUNK, (ch + 1) * SGU_CHUNK)
            for g in range(N_GROUPS):
                gc = slice(g * GROUP, (g + 1) * GROUP)
                mixed = _dot(wsgu_ref[g], vn_scr[crow, gc]) + bsgu_ref[:, gc]
                yb = su_scr[crow, gc] * mixed
                yb_scr[pl.ds(_aligned(r0 + ch * SGU_CHUNK, SGU_CHUNK), SGU_CHUNK), gc] = yb.astype(BF16)
        return carry

    _loop(n_tiles, phase_a)

    zpad = jnp.zeros((POOL_PAD, D_BRANCH), F32)
    xp_scr[0:POOL_PAD, :] = zpad
    xp_scr[T + POOL_PAD:T + 2 * POOL_PAD, :] = zpad
    pscale = pscale_ref[...]

    if not grid2d:
        t_idx = lax.broadcasted_iota(jnp.int32, (T, GROUP), 0)
        for g, win in enumerate(POOL_WINDOWS):
            gc = slice(g * GROUP, (g + 1) * GROUP)
            half = win // 2
            total = jnp.zeros((T, GROUP), F32)
            for j in range(-half, win - half):
                total = total + xp_scr[POOL_PAD + j:POOL_PAD + j + T, gc]
            cnt = (jnp.minimum(t_idx + (win - half), T) - jnp.maximum(t_idx - half, 0)).astype(F32)
            dlt = total / cnt - xp_scr[POOL_PAD:POOL_PAD + T, gc]
            ya = _dot(dlt.astype(BF16), wpool_ref[g]) * pscale[:, gc]
            ya_scr[:, gc] = ya.astype(BF16)
    else:
        n_rows = T // GRID_W
        c_idx = lax.broadcasted_iota(jnp.int32, (T, GROUP), 0) & (GRID_W - 1)
        for g, win in enumerate(POOL_WINDOWS):
            gc = slice(g * GROUP, (g + 1) * GROUP)
            half = win // 2
            total = jnp.zeros((T, GROUP), F32)
            for j in range(-half, win - half):
                valid = (c_idx + j >= 0) & (c_idx + j < GRID_W)
                total = total + jnp.where(valid, xp_scr[POOL_PAD + j:POOL_PAD + j + T, gc], 0.0)
            cnt = (jnp.minimum(c_idx + (win - half), GRID_W) - jnp.maximum(c_idx - half, 0)).astype(F32)
            m1_scr[...] = total / cnt
            for r in range(n_rows):
                lo = max(r - half, 0)
                hi = min(r + (win - half), n_rows)
                acc = m1_scr[lo * GRID_W:(lo + 1) * GRID_W, :]
                for rr in range(lo + 1, hi):
                    acc = acc + m1_scr[rr * GRID_W:(rr + 1) * GRID_W, :]
                xg = xp_scr[POOL_PAD + r * GRID_W:POOL_PAD + (r + 1) * GRID_W, gc]
                dlt = acc / float(hi - lo) - xg
                ya = _dot(dlt.astype(BF16), wpool_ref[g]) * pscale[:, gc]
                ya_scr[r * GRID_W:(r + 1) * GRID_W, gc] = ya.astype(BF16)

    gate_t = gate_scr[...].T
    gp = gate_t[0:N_GATE, :] + bgates_ref[...]
    i_all = gp[0:2 * N_HEADS, :]
    f_all = gp[2 * N_HEADS:4 * N_HEADS, :]
    logf = jnp.minimum(f_all, 0.0) - jnp.log1p(jnp.exp(-jnp.abs(f_all)))
    lane = lax.broadcasted_iota(jnp.int32, (2 * N_HEADS, T), 1)
    pos = lane & (L - 1)
    is_fwd = lax.broadcasted_iota(jnp.int32, (2 * N_HEADS, T), 0) < N_HEADS
    b_row = jnp.where(is_fwd, _seg_scan(logf, pos, L, T, "sum", False), _seg_scan(logf, pos, L, T, "sum", True))
    g_row = i_all - b_row
    cm_row = jnp.where(is_fwd, _seg_scan(g_row, pos, L, T, "max", False), _seg_scan(g_row, pos, L, T, "max", True))
    grow_scr[...] = g_row
    stacked = jnp.concatenate(
        [g_row, b_row, cm_row, jnp.zeros((GATE_PAD - 6 * N_HEADS, T), F32)], axis=0)
    colf_scr[...] = stacked.T

    if has_state:
        c_scr[...] = c0_ref[...]
        n_scr[...] = n0_ref[...]
        m_init = m0_ref[...]
        m_start = (m_init[0:1, :], m_init[1:2, :])
    else:
        m_start = (jnp.zeros((1, N_HEADS), F32), jnp.zeros((1, N_HEADS), F32))

    t_i = lax.broadcasted_iota(jnp.int32, (L, L), 0)
    s_i = lax.broadcasted_iota(jnp.int32, (L, L), 1)
    masks = (s_i <= t_i, s_i >= t_i)

    def mlstm_step(j, carry):
        new_m = []
        for d in range(2):
            c = j if d == 0 else n_chunks - 1 - j
            c0 = _aligned(c * L, L)
            rows = pl.ds(c0, L)
            h_scr = hf_scr if d == 0 else hb_scr
            colf = colf_scr[rows, :]
            gcol = colf[:, N_HEADS * d:N_HEADS * (d + 1)]
            bcol = colf[:, 2 * N_HEADS + N_HEADS * d:2 * N_HEADS + N_HEADS * (d + 1)]
            cmcol = colf[:, 4 * N_HEADS + N_HEADS * d:4 * N_HEADS + N_HEADS * (d + 1)]
            m_in = carry[d]
            mx = jnp.maximum(m_in, cmcol)
            last = L - 1 if d == 0 else 0
            mx_last = mx[last:last + 1, :]
            w_inter = jnp.exp(m_in - mx)
            e_negm = jnp.exp(-(bcol + mx))
            w_state = jnp.exp(gcol - mx_last)
            decay = jnp.exp(m_in - mx_last)
            new_m.append(bcol[last:last + 1, :] + mx_last)
            for h in range(N_HEADS):
                hc = slice(h * HEAD_DIM, (h + 1) * HEAD_DIM)
                qh = q_scr[rows, hc]
                kh = k_scr[rows, hc]
                vh = v_scr[rows, hc]
                qk = _dot_nt(qh, kh)
                arg = grow_scr[N_HEADS * d + h:N_HEADS * d + h + 1, rows] - mx[:, h:h + 1]
                s = qk * jnp.where(masks[d], jnp.exp(arg), 0.0)
                den = jnp.sum(s, axis=1, keepdims=True)
                num = _dot(s.astype(BF16), vh)
                if has_state:
                    c_dh = c_scr[d, h]
                    n_dh = n_scr[d, h:h + 1, :]
                    num = num + w_inter[:, h:h + 1] * _dot(qh, c_dh.astype(BF16))
                    den = den + w_inter[:, h:h + 1] * jnp.sum(qh.astype(F32) * n_dh, axis=1, keepdims=True)
                h_scr[rows, hc] = num / jnp.maximum(jnp.abs(den), e_negm[:, h:h + 1])
                kw = kh.astype(F32) * w_state[:, h:h + 1]
                c_new = _dot(kw.T.astype(BF16), vh)
                n_new = jnp.sum(kw, axis=0, keepdims=True)
                if has_state:
                    c_scr[d, h] = decay[:, h:h + 1] * c_dh + c_new
                    n_scr[d, h:h + 1, :] = decay[:, h:h + 1] * n_dh + n_new
                else:
                    cout_ref[d, h] = c_new
                    nout_ref[d, h:h + 1, :] = n_new
        return tuple(new_m)

    if n_chunks == 1:
        m_end = mlstm_step(0, m_start)
    else:
        m_end = lax.fori_loop(0, n_chunks, mlstm_step, m_start)
    if not has_state:
        mout_ref[0:1, :] = m_end[0]
        mout_ref[1:2, :] = m_end[1]

    gml = gml_ref[...]

    def phase_c(i, carry):
        r0 = _aligned(i * TILE_M, TILE_M)
        rows = pl.ds(r0, TILE_M)
        for h in range(N_HEADS):
            hc = slice(h * HEAD_DIM, (h + 1) * HEAD_DIM)
            hh = hf_scr[rows, hc] + hb_scr[rows, hc]
            hn = hh * lax.rsqrt(jnp.mean(hh * hh, axis=-1, keepdims=True) + EPS) * gml[:, hc]
            yc_scr[rows, hc] = (jax.nn.sigmoid(o_scr[rows, hc]) * hn).astype(BF16)
        ub = u_scr[rows, :]
        y_refs = (ya_scr, yb_scr, yc_scr)
        nw = 256
        for nj in range(D_MODEL // nw):
            nc = slice(nj * nw, (nj + 1) * nw)
            acc = jnp.zeros((TILE_M, nw), F32)
            for r in range(N_BRANCH):
                gate = jax.nn.sigmoid(_dot(ub, wbrg_ref[:, r * D_MODEL + nj * nw:r * D_MODEL + (nj + 1) * nw]))
                acc = acc + gate * _dot(y_refs[r][rows, :], wbranch_ref[r, :, nc])
            merged_scr[:, nc] = acc.astype(BF16)
        out_ref[rows, :] = x_ref[rows, :] + gt1 * _dot(merged_scr[...], wout_ref[...])
        return carry

    _loop(n_tiles, phase_c)


def _mixer_call(l, xall, mods4, prm, stream, states=None, state_outs=None):
    ctx = stream == "ctx"
    T = SEQ if ctx else DEC_SEQ
    n_b = BATCH if ctx else DEC_BATCH
    blk0 = 0 if ctx else N_TOK_CTX // T
    once = pl.Buffered(1)

    def wspec(shape):
        nd = len(shape)
        return pl.BlockSpec((None,) + shape, lambda b: (l,) + (0,) * nd, pipeline_mode=once)

    x_spec = pl.BlockSpec((T, D_MODEL), lambda b: (blk0 + b, 0), pipeline_mode=None if ctx else once)
    if ctx:
        mod_spec = pl.BlockSpec((None, None, 1, 6 * D_MODEL), lambda b: (l, 0, 0, 0))
    else:
        mod_spec = pl.BlockSpec((None, None, 1, 6 * D_MODEL), lambda b: (l, 1 + b, 0, 0))
    in_specs = [
        x_spec, mod_spec,
        wspec((1, D_MODEL)),
        wspec((D_MODEL, N_MAIN)),
        wspec((D_MODEL, GATE_PAD)),
        wspec((D_MODEL, N_BRANCH * D_MODEL)),
        wspec((N_BRANCH, D_BRANCH, D_MODEL)),
        wspec((D_MODEL, D_MODEL)),
        wspec((N_GROUPS, GROUP, GROUP)),
        wspec((1, D_BRANCH)),
        wspec((1, D_BRANCH)),
        wspec((N_GROUPS, SGU_CHUNK, SGU_CHUNK)),
        wspec((SGU_CHUNK, D_BRANCH)),
        wspec((N_GATE, 1)),
        wspec((1, D_BRANCH)),
    ]
    args = [xall, mods4, prm["g_norm1"], prm["w_in16"], prm["w_gate"], prm["w_brg"], prm["w_branch"], prm["w_out"],
            prm["w_pool"], prm["pool_scale"], prm["g_sgu"], prm["w_sgu"], prm["b_sgu_tile"],
            prm["b_gates"], prm["g_mlstm"]]
    out_shape = [jax.ShapeDtypeStruct(xall.shape, xall.dtype)]
    out_specs = [x_spec]
    aliases = {0: 0}
    if ctx:
        new_c, new_n, new_m = state_outs
        in_specs += [pl.BlockSpec(memory_space=pl.ANY)] * 3
        aliases.update({len(args): 1, len(args) + 1: 2, len(args) + 2: 3})
        args += [new_c, new_n, new_m]
        out_shape += [jax.ShapeDtypeStruct(a.shape, a.dtype) for a in state_outs]
        out_specs += [
            pl.BlockSpec((None, None, 2, N_HEADS, HEAD_DIM, HEAD_DIM), lambda b: (b, l, 0, 0, 0, 0)),
            pl.BlockSpec((None, None, 2, N_HEADS, HEAD_DIM), lambda b: (b, l, 0, 0, 0)),
            pl.BlockSpec((None, None, 2, N_HEADS), lambda b: (b, l, 0, 0)),
        ]
    else:
        state_c, state_n, state_m = states
        in_specs += [
            pl.BlockSpec((None, None, 2, N_HEADS, HEAD_DIM, HEAD_DIM), lambda b: (b, l, 0, 0, 0, 0)),
            pl.BlockSpec((None, None, 2, N_HEADS, HEAD_DIM), lambda b: (b, l, 0, 0, 0)),
            pl.BlockSpec((None, None, 2, N_HEADS), lambda b: (b, l, 0, 0)),
        ]
        args += [state_c, state_n, state_m]

    scratch = [
        pltpu.VMEM((T, D_MODEL), BF16),
        pltpu.VMEM((T + 2 * POOL_PAD, D_BRANCH), F32),
        pltpu.VMEM((TILE_M, D_BRANCH), F32),
        pltpu.VMEM((TILE_M, D_BRANCH), BF16),
        pltpu.VMEM((T, D_BRANCH), BF16),
        pltpu.VMEM((T, D_BRANCH), BF16),
        pltpu.VMEM((T, D_BRANCH), BF16),
        pltpu.VMEM((T, D_BRANCH), F32),
        pltpu.VMEM((T, GATE_PAD), F32),
        pltpu.VMEM((2 * N_HEADS, T), F32),
        pltpu.VMEM((T, GATE_PAD), F32),
        pltpu.VMEM((T, D_BRANCH), F32),
        pltpu.VMEM((T, D_BRANCH), F32),
        pltpu.VMEM((T, D_BRANCH), BF16),
        pltpu.VMEM((T, D_BRANCH), BF16),
        pltpu.VMEM((T, D_BRANCH), BF16),
        pltpu.VMEM((TILE_M, D_MODEL), BF16),
    ]
    if not ctx:
        scratch += [pltpu.VMEM((2, N_HEADS, HEAD_DIM, HEAD_DIM), F32), pltpu.VMEM((2, N_HEADS, HEAD_DIM), F32)]
        scratch += [pltpu.VMEM((T, GROUP), F32)]

    outs = pl.pallas_call(
        functools.partial(_mixer_kernel, T=T, grid2d=not ctx, has_state=not ctx),
        grid=(n_b,),
        in_specs=in_specs,
        out_specs=out_specs,
        out_shape=out_shape,
        scratch_shapes=scratch,
        input_output_aliases=aliases,
        compiler_params=pltpu.CompilerParams(
            dimension_semantics=("arbitrary",), vmem_limit_bytes=VMEM_LIMIT_BYTES),
        name="mixer_ctx" if ctx else "mixer_lat",
    )(*args)
    return outs


def _ffn_kernel(x_ref, mod_ref, g2_ref, w1_hbm, w2_hbm, gfin_ref, out_ref,
                acc_scr, w1_scr, w2_scr, st1_scr, st2_scr, sem, *, layer, final):
    n_chunks = D_FF // FFN_CHUNK
    first_tile = pl.program_id(0) == 0

    def weight_copies(c, slot):
        c0 = _aligned(c * FFN_CHUNK, FFN_CHUNK)
        return (
            pltpu.make_async_copy(w1_hbm.at[layer, :, pl.ds(c0, FFN_CHUNK)], st1_scr.at[slot], sem.at[0, slot]),
            pltpu.make_async_copy(w2_hbm.at[layer, pl.ds(c0, FFN_CHUNK), :], st2_scr.at[slot], sem.at[1, slot]),
        )

    def start_chunk(c, slot):
        for cp in weight_copies(c, slot):
            cp.start()

    @pl.when(first_tile)
    def _():
        for c in range(FFN_STAGES):
            start_chunk(c, c)

    mod = mod_ref[...]
    sh2 = mod[:, 3 * D_MODEL:4 * D_MODEL]
    sc2 = mod[:, 4 * D_MODEL:5 * D_MODEL]
    gt2 = mod[:, 5 * D_MODEL:6 * D_MODEL]
    x = x_ref[...]
    ms = jnp.mean(x * x, axis=-1, keepdims=True)
    ub = (x * lax.rsqrt(ms + EPS) * g2_ref[...] * (1.0 + sc2) + sh2).astype(BF16)
    acc_scr[...] = jnp.zeros_like(acc_scr)

    def chunk(c, carry):
        c0 = _aligned(c * FFN_CHUNK, FFN_CHUNK)

        @pl.when(first_tile)
        def _():
            slot = c % FFN_STAGES
            for cp in weight_copies(c, slot):
                cp.wait()
            w1_scr[:, pl.ds(c0, FFN_CHUNK)] = st1_scr[slot].astype(BF16)
            w2_scr[pl.ds(c0, FFN_CHUNK), :] = st2_scr[slot].astype(BF16)

            @pl.when(c + FFN_STAGES < n_chunks)
            def _():
                start_chunk(c + FFN_STAGES, slot)

        h = _dot(ub, w1_scr[:, pl.ds(c0, FFN_CHUNK)])
        h = jnp.square(jnp.maximum(h, 0.0))
        acc_scr[...] += _dot(h.astype(BF16), w2_scr[pl.ds(c0, FFN_CHUNK), :])
        return carry

    lax.fori_loop(0, n_chunks, chunk, 0)
    y = x + gt2 * acc_scr[...]
    if final:
        y = y * lax.rsqrt(jnp.mean(y * y, axis=-1, keepdims=True) + EPS) * gfin_ref[...]
    out_ref[...] = y


def _ffn_call(l, xall, mods4, prm, g_final, final):
    ctx_tiles = N_TOK_CTX // FFN_TILE_M
    lat_tiles = DEC_SEQ // FFN_TILE_M
    once = pl.Buffered(1)

    def mod_map(t):
        return (l, jnp.where(t < ctx_tiles, 0, 1 + (t - ctx_tiles) // lat_tiles), 0, 0)

    x_spec = pl.BlockSpec((FFN_TILE_M, D_MODEL), lambda t: (t, 0))
    return pl.pallas_call(
        functools.partial(_ffn_kernel, layer=l, final=final),
        grid=(N_TOK // FFN_TILE_M,),
        in_specs=[
            x_spec,
            pl.BlockSpec((None, None, 1, 6 * D_MODEL), mod_map),
            pl.BlockSpec((None, 1, D_MODEL), lambda t: (l, 0, 0), pipeline_mode=once),
            pl.BlockSpec(memory_space=pl.ANY),
            pl.BlockSpec(memory_space=pl.ANY),
            pl.BlockSpec((1, D_MODEL), lambda t: (0, 0)),
        ],
        out_specs=x_spec,
        out_shape=jax.ShapeDtypeStruct(xall.shape, xall.dtype),
        scratch_shapes=[
            pltpu.VMEM((FFN_TILE_M, D_MODEL), F32),
            pltpu.VMEM((D_MODEL, D_FF), BF16),
            pltpu.VMEM((D_FF, D_MODEL), BF16),
            pltpu.VMEM((FFN_STAGES, D_MODEL, FFN_CHUNK), F32),
            pltpu.VMEM((FFN_STAGES, FFN_CHUNK, D_MODEL), F32),
            pltpu.SemaphoreType.DMA((2, FFN_STAGES)),
        ],
        input_output_aliases={0: 0},
        compiler_params=pltpu.CompilerParams(
            dimension_semantics=("arbitrary",), vmem_limit_bytes=VMEM_LIMIT_BYTES),
        name="ffn_final" if final else "ffn",
    )(xall, mods4, prm["g_norm2"], prm["w_ff1"], prm["w_ff2"], g_final.reshape(1, D_MODEL))


def kernel(x_prompt, x_sample, state_C, state_n, state_m, c, c_ctx, w_ada, b_ada, g_norm1, g_norm2, w_in,
           b_gates, w_pool, pool_scale, g_sgu, w_sgu, b_sgu, g_mlstm, w_branch, w_out, w_ff1, w_ff2, g_final):
    w_in16 = w_in.astype(BF16)
    prm = {
        "w_in16": w_in16,
        "w_gate": jnp.pad(w_in16[:, :, N_MAIN:N_MAIN + N_GATE], ((0, 0), (0, 0), (0, GATE_PAD - N_GATE))),
        "w_brg": w_in16[:, :, N_MAIN + N_GATE:],
        "w_branch": w_branch.astype(BF16),
        "w_out": w_out.astype(BF16),
        "w_pool": w_pool.astype(BF16),
        "w_sgu": w_sgu.astype(BF16),
        "w_ff1": w_ff1,
        "w_ff2": w_ff2,
        "g_norm1": g_norm1.reshape(DEPTH, 1, D_MODEL),
        "g_norm2": g_norm2.reshape(DEPTH, 1, D_MODEL),
        "pool_scale": pool_scale.reshape(DEPTH, 1, D_BRANCH),
        "g_sgu": g_sgu.reshape(DEPTH, 1, D_BRANCH),
        "g_mlstm": g_mlstm.reshape(DEPTH, 1, D_BRANCH),
        "b_gates": b_gates.reshape(DEPTH, N_GATE, 1),
        "b_sgu_tile": jnp.repeat(jnp.swapaxes(b_sgu, 1, 2), GROUP, axis=2),
    }
    cond8 = jnp.zeros((COND_ROWS, D_MODEL), F32).at[0].set(c_ctx).at[1:1 + DEC_BATCH].set(c)
    mods = _ada_call(cond8, w_ada, b_ada)
    mods4 = mods.reshape(DEPTH, COND_ROWS, 1, 6 * D_MODEL)

    xall = jnp.concatenate([x_prompt.reshape(N_TOK_CTX, D_MODEL), x_sample.reshape(-1, D_MODEL)], axis=0)
    new_c = jnp.zeros((BATCH, DEPTH, 2, N_HEADS, HEAD_DIM, HEAD_DIM), F32)
    new_n = jnp.zeros((BATCH, DEPTH, 2, N_HEADS, HEAD_DIM), F32)
    new_m = jnp.zeros((BATCH, DEPTH, 2, N_HEADS), F32)
    for l in range(DEPTH):
        xall, new_c, new_n, new_m = _mixer_call(l, xall, mods4, prm, "ctx", state_outs=(new_c, new_n, new_m))
        (xall,) = _mixer_call(l, xall, mods4, prm, "lat", states=(state_C, state_n, state_m))
        xall = _ffn_call(l, xall, mods4, prm, g_final, final=(l == DEPTH - 1))
    y_prompt = xall[:N_TOK_CTX].reshape(BATCH, SEQ, D_MODEL)
    y_sample = xall[N_TOK_CTX:].reshape(DEC_BATCH, DEC_SEQ, D_MODEL)
    return (y_prompt, y_sample, new_c, new_n, new_m)
```

```python
import functools

import jax
import jax.numpy as jnp
from jax import lax
from jax.experimental import pallas as pl
from jax.experimental.pallas import tpu as pltpu

F32 = jnp.float32
BF16 = jnp.bfloat16

D_MODEL = 1024
DEPTH = 4
BATCH = 16
SEQ = 256
DEC_BATCH = 2
DEC_SEQ = 1024
GRID_W = 64
D_BRANCH = 512
POOL_WINDOWS = (2, 4, 8, 16)
N_GROUPS = 4
GROUP = D_BRANCH // N_GROUPS
SGU_CHUNK = 128
N_HEADS = 4
HEAD_DIM = D_BRANCH // N_HEADS
N_BRANCH = 3
D_FF = 4 * D_MODEL
EPS = 1e-6

N_MAIN = 7 * D_BRANCH
N_GATE = 4 * N_HEADS
GATE_PAD = 128
N_TOK_CTX = BATCH * SEQ
N_TOK = N_TOK_CTX + DEC_BATCH * DEC_SEQ
COND_ROWS = 8

TILE_M = 256
MLSTM_L = 256
MLSTM_ROW_BLOCK = 128
MERGE_COLS = 256
POOL_PAD = 8
FFN_TILE_M = 512
FFN_CHUNK = 512
FFN_STAGES = 2
ADA_TILE_N = 1536
VMEM_LIMIT_BYTES = 60 * 1024 * 1024


def _dot(a, b):
    return jnp.dot(a, b, preferred_element_type=F32)


def _dot_nt(a, b):
    return lax.dot_general(a, b, (((1,), (1,)), ((), ())), preferred_element_type=F32)


def _aligned(offset, align):
    return offset if isinstance(offset, int) else pl.multiple_of(offset, align)


def _loop(n, body):
    if n == 1:
        body(0, 0)
    else:
        lax.fori_loop(0, n, body, 0)


def _ada_kernel(cond_ref, w_ref, b_ref, out_ref):
    cond = cond_ref[...]
    s = (cond * jax.nn.sigmoid(cond)).astype(BF16)
    out_ref[...] = _dot(s, w_ref[...].astype(BF16)) + b_ref[...]


def _ada_call(cond8, w_ada, b_ada):
    n_out = 6 * D_MODEL
    return pl.pallas_call(
        _ada_kernel,
        grid=(DEPTH, n_out // ADA_TILE_N),
        in_specs=[
            pl.BlockSpec((COND_ROWS, D_MODEL), lambda l, j: (0, 0)),
            pl.BlockSpec((None, D_MODEL, ADA_TILE_N), lambda l, j: (l, 0, j)),
            pl.BlockSpec((None, 1, ADA_TILE_N), lambda l, j: (l, 0, j)),
        ],
        out_specs=pl.BlockSpec((None, COND_ROWS, ADA_TILE_N), lambda l, j: (l, 0, j)),
        out_shape=jax.ShapeDtypeStruct((DEPTH, COND_ROWS, n_out), F32),
        compiler_params=pltpu.CompilerParams(
            dimension_semantics=("arbitrary", "arbitrary"), vmem_limit_bytes=VMEM_LIMIT_BYTES),
        name="ada_mod",
    )(cond8, w_ada, b_ada.reshape(DEPTH, 1, n_out))


def _seg_scan(x, pos, seg, total, op, reverse):
    d = 1
    while d < seg:
        if reverse:
            shifted = pltpu.roll(x, total - d, 1)
            valid = pos < seg - d
        else:
            shifted = pltpu.roll(x, d, 1)
            valid = pos >= d
        if op == "sum":
            x = x + jnp.where(valid, shifted, 0.0)
        else:
            x = jnp.maximum(x, jnp.where(valid, shifted, -jnp.inf))
        d *= 2
    return x


def _mixer_kernel(*refs, T, grid2d, has_state):
    it = iter(refs)
    x_ref = next(it)
    mod_ref = next(it)
    g1_ref = next(it)
    wmain_ref = next(it)
    wkt_ref = next(it)
    wgate_ref = next(it)
    wbrg_ref = next(it)
    wbranch_ref = next(it)
    wout_ref = next(it)
    wpool_ref = next(it)
    pscale_ref = next(it)
    gsgu_ref = next(it)
    wsgu_ref = next(it)
    bsgu_ref = next(it)
    bgates_ref = next(it)
    gml_ref = next(it)
    if has_state:
        cn0_ref, m0_ref = next(it), next(it)
    else:
        next(it), next(it), next(it)
    out_ref = next(it)
    if not has_state:
        cout_ref, nout_ref, mout_ref = next(it), next(it), next(it)
    (u_scr, xp_scr, su_scr, vn_scr, q_scr, kt_scr, vaug_scr, o_scr, grow_scr, colf_scr,
     h_scr, ya_scr, yb_scr, yc_scr, mab_scr, merged_scr) = [next(it) for _ in range(16)]
    if has_state:
        cn_scr = next(it)
    if grid2d:
        m1_scr = next(it)

    L = MLSTM_L
    n_tiles = T // TILE_M
    n_chunks = T // L
    RB = MLSTM_ROW_BLOCK
    NW = MERGE_COLS

    mod = mod_ref[...]
    sh1 = mod[:, 0:D_MODEL]
    sc1 = mod[:, D_MODEL:2 * D_MODEL]
    gt1 = mod[:, 2 * D_MODEL:3 * D_MODEL]
    g1 = g1_ref[...]

    lane = lax.broadcasted_iota(jnp.int32, (2 * N_HEADS, TILE_M), 1)
    is_fwd = lax.broadcasted_iota(jnp.int32, (2 * N_HEADS, TILE_M), 0) < N_HEADS
    ones_blk = jnp.ones((TILE_M, HEAD_DIM), BF16)

    def phase_a(i, carry):
        r0 = _aligned(i * TILE_M, TILE_M)
        rows = pl.ds(r0, TILE_M)
        x = x_ref[rows, :]
        ms = jnp.mean(x * x, axis=-1, keepdims=True)
        u = x * lax.rsqrt(ms + EPS) * g1 * (1.0 + sc1) + sh1
        ub = u.astype(BF16)
        u_scr[rows, :] = ub

        gate_t = _dot(ub, wgate_ref[...]).T
        gp = gate_t[0:N_GATE, :] + bgates_ref[...]
        i_all = gp[0:2 * N_HEADS, :]
        f_all = gp[2 * N_HEADS:4 * N_HEADS, :]
        logf = jnp.minimum(f_all, 0.0) - jnp.log1p(jnp.exp(-jnp.abs(f_all)))
        b_row = jnp.where(is_fwd, _seg_scan(logf, lane, L, L, "sum", False),
                          _seg_scan(logf, lane, L, L, "sum", True))
        g_row = i_all - b_row
        cm_row = jnp.where(is_fwd, _seg_scan(g_row, lane, L, L, "max", False),
                           _seg_scan(g_row, lane, L, L, "max", True))
        grow_scr[:, rows] = g_row
        stacked = jnp.concatenate(
            [b_row, cm_row, jnp.zeros((GATE_PAD - 4 * N_HEADS, TILE_M), F32)], axis=0)
        colf_scr[rows, :] = stacked.T

        def proj(k):
            return _dot(ub, wmain_ref[:, k * D_BRANCH:(k + 1) * D_BRANCH])

        xp_scr[pl.ds(_aligned(r0 + POOL_PAD, 8), TILE_M), :] = proj(0)
        su_scr[...] = proj(1)
        sv = proj(2)
        vn = sv * lax.rsqrt(jnp.mean(sv * sv, axis=-1, keepdims=True) + EPS) * gsgu_ref[...]
        vn_scr[...] = vn.astype(BF16)
        q_scr[rows, :] = proj(3).astype(BF16)
        kt_scr[:, rows] = (_dot_nt(wkt_ref[...], ub) * (HEAD_DIM ** -0.5)).astype(BF16)
        v = proj(5).astype(BF16)
        for h in range(N_HEADS):
            vaug_scr[rows, 2 * h * HEAD_DIM:(2 * h + 1) * HEAD_DIM] = v[:, h * HEAD_DIM:(h + 1) * HEAD_DIM]
            vaug_scr[rows, (2 * h + 1) * HEAD_DIM:(2 * h + 2) * HEAD_DIM] = ones_blk
        o_scr[rows, :] = proj(6)

        for ch in range(TILE_M // SGU_CHUNK):
            crow = slice(ch * SGU_CHUNK, (ch + 1) * SGU_CHUNK)
            for g in range(N_GROUPS):
                gc = slice(g * GROUP, (g + 1) * GROUP)
                mixed = _dot(wsgu_ref[g], vn_scr[crow, gc]) + bsgu_ref[:, gc]
                yb = su_scr[crow, gc] * mixed
                yb_scr[pl.ds(_aligned(r0 + ch * SGU_CHUNK, SGU_CHUNK), SGU_CHUNK), gc] = yb.astype(BF16)
        return carry

    _loop(n_tiles, phase_a)

    zpad = jnp.zeros((POOL_PAD, D_BRANCH), F32)
    xp_scr[0:POOL_PAD, :] = zpad
    xp_scr[T + POOL_PAD:T + 2 * POOL_PAD, :] = zpad
    pscale = pscale_ref[...]

    if not grid2d:
        t_idx = lax.broadcasted_iota(jnp.int32, (T, GROUP), 0)
        for g, win in enumerate(POOL_WINDOWS):
            gc = slice(g * GROUP, (g + 1) * GROUP)
            half = win // 2
            total = jnp.zeros((T, GROUP), F32)
            for j in range(-half, win - half):
                total = total + xp_scr[POOL_PAD + j:POOL_PAD + j + T, gc]
            cnt = (jnp.minimum(t_idx + (win - half), T) - jnp.maximum(t_idx - half, 0)).astype(F32)
            dlt = total / cnt - xp_scr[POOL_PAD:POOL_PAD + T, gc]
            ya = _dot(dlt.astype(BF16), wpool_ref[g]) * pscale[:, gc]
            ya_scr[:, gc] = ya.astype(BF16)
    else:
        n_rows = T // GRID_W
        c_idx = lax.broadcasted_iota(jnp.int32, (T, GROUP), 0) & (GRID_W - 1)
        for g, win in enumerate(POOL_WINDOWS):
            gc = slice(g * GROUP, (g + 1) * GROUP)
            half = win // 2
            total = jnp.zeros((T, GROUP), F32)
            for j in range(-half, win - half):
                valid = (c_idx + j >= 0) & (c_idx + j < GRID_W)
                total = total + jnp.where(valid, xp_scr[POOL_PAD + j:POOL_PAD + j + T, gc], 0.0)
            cnt = (jnp.minimum(c_idx + (win - half), GRID_W) - jnp.maximum(c_idx - half, 0)).astype(F32)
            m1_scr[...] = total / cnt
            for r in range(n_rows):
                lo = max(r - half, 0)
                hi = min(r + (win - half), n_rows)
                acc = m1_scr[lo * GRID_W:(lo + 1) * GRID_W, :]
                for rr in range(lo + 1, hi):
                    acc = acc + m1_scr[rr * GRID_W:(rr + 1) * GRID_W, :]
                xg = xp_scr[POOL_PAD + r * GRID_W:POOL_PAD + (r + 1) * GRID_W, gc]
                dlt = acc / float(hi - lo) - xg
                ya = _dot(dlt.astype(BF16), wpool_ref[g]) * pscale[:, gc]
                ya_scr[r * GRID_W:(r + 1) * GRID_W, gc] = ya.astype(BF16)

    if has_state:
        cn_scr[...] = cn0_ref[...]
        m_init = m0_ref[...]
        m_start = (m_init[0:1, :], m_init[1:2, :])
    else:
        m_start = (jnp.zeros((1, N_HEADS), F32), jnp.zeros((1, N_HEADS), F32))

    def keep_mask(d, rb, col_lo, ncols):
        t_loc = rb * RB + lax.broadcasted_iota(jnp.int32, (RB, ncols), 0)
        s_loc = col_lo + lax.broadcasted_iota(jnp.int32, (RB, ncols), 1)
        return s_loc <= t_loc if d == 0 else s_loc >= t_loc

    def mlstm_chunk(d, c, m_in, fillers):
        c0 = _aligned(c * L, L)
        rows = pl.ds(c0, L)
        colf = colf_scr[rows, :]
        bcol = colf[:, N_HEADS * d:N_HEADS * (d + 1)]
        cmcol = colf[:, 2 * N_HEADS + N_HEADS * d:2 * N_HEADS + N_HEADS * (d + 1)]
        mx = jnp.maximum(m_in, cmcol)
        last = L - 1 if d == 0 else 0
        mx_last = mx[last:last + 1, :]
        w_inter = jnp.exp(m_in - mx)
        e_negm = jnp.exp(-(bcol + mx))
        decay = jnp.exp(m_in - mx_last)
        m_new = bcol[last:last + 1, :] + mx_last
        for h in range(N_HEADS):
            hc = slice(h * HEAD_DIM, (h + 1) * HEAD_DIM)
            ac = slice(2 * h * HEAD_DIM, (2 * h + 2) * HEAD_DIM)
            grow = grow_scr[N_HEADS * d + h:N_HEADS * d + h + 1, rows]
            if has_state:
                cn_dh = cn_scr[d, h]
                cn_bf = cn_dh.astype(BF16)
            for rb in range(L // RB):
                col_lo, col_hi = (0, (rb + 1) * RB) if d == 0 else (rb * RB, L)
                ncols = col_hi - col_lo
                trow = slice(rb * RB, (rb + 1) * RB)
                rrows = pl.ds(_aligned(c0 + rb * RB, RB), RB)
                crows = pl.ds(_aligned(c0 + col_lo, RB), ncols)
                qb = q_scr[rrows, hc]
                qk = _dot(qb, kt_scr[hc, crows])
                w = jnp.exp(grow[:, col_lo:col_hi] - mx[trow, h:h + 1])
                s = (qk * jnp.where(keep_mask(d, rb, col_lo, ncols), w, 0.0)).astype(BF16)
                na = _dot(s, vaug_scr[crows, ac])
                if has_state:
                    na = na + w_inter[trow, h:h + 1] * _dot(qb, cn_bf)
                hval = na[:, 0:HEAD_DIM] / jnp.maximum(jnp.abs(na[:, HEAD_DIM:]), e_negm[trow, h:h + 1])
                if d == 0:
                    h_scr[rrows, hc] = hval
                else:
                    h_scr[rrows, hc] += hval
            w_state = jnp.exp(grow - mx_last[:, h:h + 1])
            kw_t = (kt_scr[hc, rows].astype(F32) * w_state).astype(BF16)
            upd = _dot(kw_t, vaug_scr[rows, ac])
            if has_state:
                cn_scr[d, h] = decay[:, h:h + 1] * cn_dh + upd
            else:
                cout_ref[d, h] = upd[:, 0:HEAD_DIM]
                nout_ref[d, h:h + 1, :] = upd[:, HEAD_DIM:].T[0:1, :]
            if h < len(fillers):
                fillers[h]()
        for f in fillers[N_HEADS:]:
            f()
        return m_new

    def merge_ab(c, nj):
        rows = pl.ds(_aligned(c * TILE_M, TILE_M), TILE_M)
        nc = slice(nj * NW, (nj + 1) * NW)
        ub = u_scr[rows, :]
        acc = None
        for r, y_scr in enumerate((ya_scr, yb_scr)):
            gate = jax.nn.sigmoid(_dot(ub, wbrg_ref[:, r * D_MODEL + nj * NW:r * D_MODEL + (nj + 1) * NW]))
            term = gate * _dot(y_scr[rows, :], wbranch_ref[r, :, nc])
            acc = term if acc is None else acc + term
        mab_scr[rows, nc] = acc

    def fwd_step(j, m_in):
        return mlstm_chunk(0, j, m_in, [functools.partial(merge_ab, j, nj) for nj in range(D_MODEL // NW)])

    def bwd_step(j, m_in):
        c = n_chunks - 1 - j
        m_new = mlstm_chunk(1, c, m_in, [])
        rows = pl.ds(_aligned(c * TILE_M, TILE_M), TILE_M)
        gml = gml_ref[...]
        for h in range(N_HEADS):
            hc = slice(h * HEAD_DIM, (h + 1) * HEAD_DIM)
            hh = h_scr[rows, hc]
            hn = hh * lax.rsqrt(jnp.mean(hh * hh, axis=-1, keepdims=True) + EPS) * gml[:, hc]
            yc_scr[:, hc] = (jax.nn.sigmoid(o_scr[rows, hc]) * hn).astype(BF16)
        ub = u_scr[rows, :]
        for nj in range(D_MODEL // NW):
            nc = slice(nj * NW, (nj + 1) * NW)
            gate = jax.nn.sigmoid(_dot(ub, wbrg_ref[:, 2 * D_MODEL + nj * NW:2 * D_MODEL + (nj + 1) * NW]))
            acc = mab_scr[rows, nc] + gate * _dot(yc_scr[...], wbranch_ref[2, :, nc])
            merged_scr[:, nc] = acc.astype(BF16)
        out_ref[rows, :] = x_ref[rows, :] + gt1 * _dot(merged_scr[...], wout_ref[...])
        return m_new

    if n_chunks == 1:
        m_f = fwd_step(0, m_start[0])
        m_b = bwd_step(0, m_start[1])
    else:
        m_f = lax.fori_loop(0, n_chunks, fwd_step, m_start[0])
        m_b = lax.fori_loop(0, n_chunks, bwd_step, m_start[1])
    if not has_state:
        mout_ref[0:1, :] = m_f
        mout_ref[1:2, :] = m_b


def _mixer_call(l, xall, mods4, prm, stream, states=None, state_outs=None):
    ctx = stream == "ctx"
    T = SEQ if ctx else DEC_SEQ
    assert T % TILE_M == 0 and TILE_M == MLSTM_L and MLSTM_L % MLSTM_ROW_BLOCK == 0
    n_b = BATCH if ctx else DEC_BATCH
    blk0 = 0 if ctx else N_TOK_CTX // T
    once = pl.Buffered(1)

    def wspec(shape):
        nd = len(shape)
        return pl.BlockSpec((None,) + shape, lambda b: (l,) + (0,) * nd, pipeline_mode=once)

    x_spec = pl.BlockSpec((T, D_MODEL), lambda b: (blk0 + b, 0), pipeline_mode=None if ctx else once)
    if ctx:
        mod_spec = pl.BlockSpec((None, None, 1, 6 * D_MODEL), lambda b: (l, 0, 0, 0))
    else:
        mod_spec = pl.BlockSpec((None, None, 1, 6 * D_MODEL), lambda b: (l, 1 + b, 0, 0))
    in_specs = [
        x_spec, mod_spec,
        wspec((1, D_MODEL)),
        wspec((D_MODEL, N_MAIN)),
        wspec((D_BRANCH, D_MODEL)),
        wspec((D_MODEL, GATE_PAD)),
        wspec((D_MODEL, N_BRANCH * D_MODEL)),
        wspec((N_BRANCH, D_BRANCH, D_MODEL)),
        wspec((D_MODEL, D_MODEL)),
        wspec((N_GROUPS, GROUP, GROUP)),
        wspec((1, D_BRANCH)),
        wspec((1, D_BRANCH)),
        wspec((N_GROUPS, SGU_CHUNK, SGU_CHUNK)),
        wspec((SGU_CHUNK, D_BRANCH)),
        wspec((N_GATE, 1)),
        wspec((1, D_BRANCH)),
    ]
    args = [xall, mods4, prm["g_norm1"], prm["w_in16"], prm["w_kt"], prm["w_gate"], prm["w_brg"],
            prm["w_branch"], prm["w_out"], prm["w_pool"], prm["pool_scale"], prm["g_sgu"], prm["w_sgu"],
            prm["b_sgu_tile"], prm["b_gates"], prm["g_mlstm"]]
    out_shape = [jax.ShapeDtypeStruct(xall.shape, xall.dtype)]
    out_specs = [x_spec]
    aliases = {0: 0}
    if ctx:
        in_specs += [pl.BlockSpec(memory_space=pl.ANY)] * 3
        aliases.update({len(args): 1, len(args) + 1: 2, len(args) + 2: 3})
        args += list(state_outs)
        out_shape += [jax.ShapeDtypeStruct(a.shape, a.dtype) for a in state_outs]
        out_specs += [
            pl.BlockSpec((None, None, 2, N_HEADS, HEAD_DIM, HEAD_DIM), lambda b: (b, l, 0, 0, 0, 0)),
            pl.BlockSpec((None, None, 2, N_HEADS, HEAD_DIM), lambda b: (b, l, 0, 0, 0)),
            pl.BlockSpec((None, None, 2, N_HEADS), lambda b: (b, l, 0, 0)),
        ]
    else:
        state_cn, state_m = states
        in_specs += [
            pl.BlockSpec((None, None, 2, N_HEADS, HEAD_DIM, 2 * HEAD_DIM), lambda b: (b, l, 0, 0, 0, 0)),
            pl.BlockSpec((None, None, 2, N_HEADS), lambda b: (b, l, 0, 0)),
        ]
        args += [state_cn, state_m]

    scratch = [
        pltpu.VMEM((T, D_MODEL), BF16),
        pltpu.VMEM((T + 2 * POOL_PAD, D_BRANCH), F32),
        pltpu.VMEM((TILE_M, D_BRANCH), F32),
        pltpu.VMEM((TILE_M, D_BRANCH), BF16),
        pltpu.VMEM((T, D_BRANCH), BF16),
        pltpu.VMEM((D_BRANCH, T), BF16),
        pltpu.VMEM((T, 2 * D_BRANCH), BF16),
        pltpu.VMEM((T, D_BRANCH), F32),
        pltpu.VMEM((2 * N_HEADS, T), F32),
        pltpu.VMEM((T, GATE_PAD), F32),
        pltpu.VMEM((T, D_BRANCH), F32),
        pltpu.VMEM((T, D_BRANCH), BF16),
        pltpu.VMEM((T, D_BRANCH), BF16),
        pltpu.VMEM((TILE_M, D_BRANCH), BF16),
        pltpu.VMEM((T, D_MODEL), F32),
        pltpu.VMEM((TILE_M, D_MODEL), BF16),
    ]
    if not ctx:
        scratch += [pltpu.VMEM((2, N_HEADS, HEAD_DIM, 2 * HEAD_DIM), F32)]
        scratch += [pltpu.VMEM((T, GROUP), F32)]

    outs = pl.pallas_call(
        functools.partial(_mixer_kernel, T=T, grid2d=not ctx, has_state=not ctx),
        grid=(n_b,),
        in_specs=in_specs,
        out_specs=out_specs,
        out_shape=out_shape,
        scratch_shapes=scratch,
        input_output_aliases=aliases,
        compiler_params=pltpu.CompilerParams(
            dimension_semantics=("arbitrary",), vmem_limit_bytes=VMEM_LIMIT_BYTES),
        name="mixer_ctx" if ctx else "mixer_lat",
    )(*args)
    return outs


def _ffn_kernel(x_ref, mod_ref, g2_ref, w1_hbm, w2_hbm, gfin_ref, out_ref,
                acc_scr, w1_scr, w2_scr, st1_scr, st2_scr, sem, *, layer, final):
    n_chunks = D_FF // FFN_CHUNK
    first_tile = pl.program_id(0) == 0

    def weight_copies(c, slot):
        c0 = _aligned(c * FFN_CHUNK, FFN_CHUNK)
        return (
            pltpu.make_async_copy(w1_hbm.at[layer, :, pl.ds(c0, FFN_CHUNK)], st1_scr.at[slot], sem.at[0, slot]),
            pltpu.make_async_copy(w2_hbm.at[layer, pl.ds(c0, FFN_CHUNK), :], st2_scr.at[slot], sem.at[1, slot]),
        )

    def start_chunk(c, slot):
        for cp in weight_copies(c, slot):
            cp.start()

    @pl.when(first_tile)
    def _():
        for c in range(FFN_STAGES):
            start_chunk(c, c)

    mod = mod_ref[...]
    sh2 = mod[:, 3 * D_MODEL:4 * D_MODEL]
    sc2 = mod[:, 4 * D_MODEL:5 * D_MODEL]
    gt2 = mod[:, 5 * D_MODEL:6 * D_MODEL]
    x = x_ref[...]
    ms = jnp.mean(x * x, axis=-1, keepdims=True)
    ub = (x * lax.rsqrt(ms + EPS) * g2_ref[...] * (1.0 + sc2) + sh2).astype(BF16)
    acc_scr[...] = jnp.zeros_like(acc_scr)

    def chunk(c, carry):
        c0 = _aligned(c * FFN_CHUNK, FFN_CHUNK)

        @pl.when(first_tile)
        def _():
            slot = c % FFN_STAGES
            for cp in weight_copies(c, slot):
                cp.wait()
            w1_scr[:, pl.ds(c0, FFN_CHUNK)] = st1_scr[slot].astype(BF16)
            w2_scr[pl.ds(c0, FFN_CHUNK), :] = st2_scr[slot].astype(BF16)

            @pl.when(c + FFN_STAGES < n_chunks)
            def _():
                start_chunk(c + FFN_STAGES, slot)

        h = _dot(ub, w1_scr[:, pl.ds(c0, FFN_CHUNK)])
        h = jnp.square(jnp.maximum(h, 0.0))
        acc_scr[...] += _dot(h.astype(BF16), w2_scr[pl.ds(c0, FFN_CHUNK), :])
        return carry

    lax.fori_loop(0, n_chunks, chunk, 0)
    y = x + gt2 * acc_scr[...]
    if final:
        y = y * lax.rsqrt(jnp.mean(y * y, axis=-1, keepdims=True) + EPS) * gfin_ref[...]
    out_ref[...] = y


def _ffn_call(l, xall, mods4, prm, g_final, final):
    ctx_tiles = N_TOK_CTX // FFN_TILE_M
    lat_tiles = DEC_SEQ // FFN_TILE_M
    once = pl.Buffered(1)

    def mod_map(t):
        return (l, jnp.where(t < ctx_tiles, 0, 1 + (t - ctx_tiles) // lat_tiles), 0, 0)

    x_spec = pl.BlockSpec((FFN_TILE_M, D_MODEL), lambda t: (t, 0))
    return pl.pallas_call(
        functools.partial(_ffn_kernel, layer=l, final=final),
        grid=(N_TOK // FFN_TILE_M,),
        in_specs=[
            x_spec,
            pl.BlockSpec((None, None, 1, 6 * D_MODEL), mod_map),
            pl.BlockSpec((None, 1, D_MODEL), lambda t: (l, 0, 0), pipeline_mode=once),
            pl.BlockSpec(memory_space=pl.ANY),
            pl.BlockSpec(memory_space=pl.ANY),
            pl.BlockSpec((1, D_MODEL), lambda t: (0, 0)),
        ],
        out_specs=x_spec,
        out_shape=jax.ShapeDtypeStruct(xall.shape, xall.dtype),
        scratch_shapes=[
            pltpu.VMEM((FFN_TILE_M, D_MODEL), F32),
            pltpu.VMEM((D_MODEL, D_FF), BF16),
            pltpu.VMEM((D_FF, D_MODEL), BF16),
            pltpu.VMEM((FFN_STAGES, D_MODEL, FFN_CHUNK), F32),
            pltpu.VMEM((FFN_STAGES, FFN_CHUNK, D_MODEL), F32),
            pltpu.SemaphoreType.DMA((2, FFN_STAGES)),
        ],
        input_output_aliases={0: 0},
        compiler_params=pltpu.CompilerParams(
            dimension_semantics=("arbitrary",), vmem_limit_bytes=VMEM_LIMIT_BYTES),
        name="ffn_final" if final else "ffn",
    )(xall, mods4, prm["g_norm2"], prm["w_ff1"], prm["w_ff2"], g_final.reshape(1, D_MODEL))


def kernel(x_prompt, x_sample, state_C, state_n, state_m, c, c_ctx, w_ada, b_ada, g_norm1, g_norm2, w_in,
           b_gates, w_pool, pool_scale, g_sgu, w_sgu, b_sgu, g_mlstm, w_branch, w_out, w_ff1, w_ff2, g_final):
    w_in16 = w_in.astype(BF16)
    prm = {
        "w_in16": w_in16,
        "w_gate": jnp.pad(w_in16[:, :, N_MAIN:N_MAIN + N_GATE], ((0, 0), (0, 0), (0, GATE_PAD - N_GATE))),
        "w_brg": w_in16[:, :, N_MAIN + N_GATE:],
        "w_kt": jnp.swapaxes(w_in16[:, :, 4 * D_BRANCH:5 * D_BRANCH], 1, 2),
        "w_branch": w_branch.astype(BF16),
        "w_out": w_out.astype(BF16),
        "w_pool": w_pool.astype(BF16),
        "w_sgu": w_sgu.astype(BF16),
        "w_ff1": w_ff1,
        "w_ff2": w_ff2,
        "g_norm1": g_norm1.reshape(DEPTH, 1, D_MODEL),
        "g_norm2": g_norm2.reshape(DEPTH, 1, D_MODEL),
        "pool_scale": pool_scale.reshape(DEPTH, 1, D_BRANCH),
        "g_sgu": g_sgu.reshape(DEPTH, 1, D_BRANCH),
        "g_mlstm": g_mlstm.reshape(DEPTH, 1, D_BRANCH),
        "b_gates": b_gates.reshape(DEPTH, N_GATE, 1),
        "b_sgu_tile": jnp.repeat(jnp.swapaxes(b_sgu, 1, 2), GROUP, axis=2),
    }
    cond8 = jnp.zeros((COND_ROWS, D_MODEL), F32).at[0].set(c_ctx).at[1:1 + DEC_BATCH].set(c)
    mods = _ada_call(cond8, w_ada, b_ada)
    mods4 = mods.reshape(DEPTH, COND_ROWS, 1, 6 * D_MODEL)

    xall = jnp.concatenate([x_prompt.reshape(N_TOK_CTX, D_MODEL), x_sample.reshape(-1, D_MODEL)], axis=0)
    state_cn = jnp.concatenate(
        [state_C, jnp.broadcast_to(state_n[..., None], state_n.shape + (HEAD_DIM,))], axis=-1)
    new_c = jnp.zeros((BATCH, DEPTH, 2, N_HEADS, HEAD_DIM, HEAD_DIM), F32)
    new_n = jnp.zeros((BATCH, DEPTH, 2, N_HEADS, HEAD_DIM), F32)
    new_m = jnp.zeros((BATCH, DEPTH, 2, N_HEADS), F32)
    for l in range(DEPTH):
        xall, new_c, new_n, new_m = _mixer_call(l, xall, mods4, prm, "ctx", state_outs=(new_c, new_n, new_m))
        (xall,) = _mixer_call(l, xall, mods4, prm, "lat", states=(state_cn, state_m))
        xall = _ffn_call(l, xall, mods4, prm, g_final, final=(l == DEPTH - 1))
    y_prompt = xall[:N_TOK_CTX].reshape(BATCH, SEQ, D_MODEL)
    y_sample = xall[N_TOK_CTX:].reshape(DEC_BATCH, DEC_SEQ, D_MODEL)
    return (y_prompt, y_sample, new_c, new_n, new_m)
```

```python
import functools

import jax
import jax.numpy as jnp
from jax import lax
from jax.experimental import pallas as pl
from jax.experimental.pallas import tpu as pltpu

F32 = jnp.float32
BF16 = jnp.bfloat16

D_MODEL = 1024
DEPTH = 4
BATCH = 16
SEQ = 256
DEC_BATCH = 2
DEC_SEQ = 1024
GRID_W = 64
D_BRANCH = 512
POOL_WINDOWS = (2, 4, 8, 16)
N_GROUPS = 4
GROUP = D_BRANCH // N_GROUPS
SGU_CHUNK = 128
N_HEADS = 4
HEAD_DIM = D_BRANCH // N_HEADS
N_BRANCH = 3
D_FF = 4 * D_MODEL
EPS = 1e-6

N_MAIN = 7 * D_BRANCH
N_GATE = 4 * N_HEADS
GATE_PAD = 128
N_TOK_CTX = BATCH * SEQ
N_TOK = N_TOK_CTX + DEC_BATCH * DEC_SEQ
COND_ROWS = 8

TILE_M = 256
MLSTM_L = 256
MLSTM_ROW_BLOCK = 128
MERGE_COLS = 256
POOL_PAD = 8
FFN_TILE_M = 512
FFN_CHUNK = 512
FFN_STAGES = 2
ADA_TILE_N = 1536
VMEM_LIMIT_BYTES = 60 * 1024 * 1024


def _dot(a, b):
    return jnp.dot(a, b, preferred_element_type=F32)


def _dot_nt(a, b):
    return lax.dot_general(a, b, (((1,), (1,)), ((), ())), preferred_element_type=F32)


def _aligned(offset, align):
    return offset if isinstance(offset, int) else pl.multiple_of(offset, align)


def _loop(n, body):
    if n == 1:
        body(0, 0)
    else:
        lax.fori_loop(0, n, body, 0)


def _ada_kernel(cond_ref, w_ref, b_ref, out_ref):
    cond = cond_ref[...]
    s = (cond * jax.nn.sigmoid(cond)).astype(BF16)
    out_ref[...] = _dot(s, w_ref[...].astype(BF16)) + b_ref[...]


def _ada_call(cond8, w_ada, b_ada):
    n_out = 6 * D_MODEL
    return pl.pallas_call(
        _ada_kernel,
        grid=(DEPTH, n_out // ADA_TILE_N),
        in_specs=[
            pl.BlockSpec((COND_ROWS, D_MODEL), lambda l, j: (0, 0)),
            pl.BlockSpec((None, D_MODEL, ADA_TILE_N), lambda l, j: (l, 0, j)),
            pl.BlockSpec((None, 1, ADA_TILE_N), lambda l, j: (l, 0, j)),
        ],
        out_specs=pl.BlockSpec((None, COND_ROWS, ADA_TILE_N), lambda l, j: (l, 0, j)),
        out_shape=jax.ShapeDtypeStruct((DEPTH, COND_ROWS, n_out), F32),
        compiler_params=pltpu.CompilerParams(
            dimension_semantics=("arbitrary", "arbitrary"), vmem_limit_bytes=VMEM_LIMIT_BYTES),
        name="ada_mod",
    )(cond8, w_ada, b_ada.reshape(DEPTH, 1, n_out))


def _seg_scan(x, pos, seg, total, op, reverse):
    d = 1
    while d < seg:
        if reverse:
            shifted = pltpu.roll(x, total - d, 1)
            valid = pos < seg - d
        else:
            shifted = pltpu.roll(x, d, 1)
            valid = pos >= d
        if op == "sum":
            x = x + jnp.where(valid, shifted, 0.0)
        else:
            x = jnp.maximum(x, jnp.where(valid, shifted, -jnp.inf))
        d *= 2
    return x


def _mixer_kernel(*refs, T, grid2d, has_state):
    it = iter(refs)
    x_ref = next(it)
    mod_ref = next(it)
    g1_ref = next(it)
    wmain_ref = next(it)
    wkt_ref = next(it)
    wgate_ref = next(it)
    wbrg_ref = next(it)
    wbranch_ref = next(it)
    wout_ref = next(it)
    wpool_ref = next(it)
    pscale_ref = next(it)
    gsgu_ref = next(it)
    wsgu_ref = next(it)
    bsgu_ref = next(it)
    bgates_ref = next(it)
    gml_ref = next(it)
    if has_state:
        cn0_ref, m0_ref = next(it), next(it)
    else:
        next(it), next(it), next(it)
    out_ref = next(it)
    if not has_state:
        cout_ref, nout_ref, mout_ref = next(it), next(it), next(it)
    (u_scr, xp_scr, su_scr, vn_scr, q_scr, kt_scr, vaug_scr, o_scr, grow_scr, colf_scr,
     h_scr, ya_scr, yb_scr, yc_scr, mab_scr, gc_scr, merged_scr) = [next(it) for _ in range(17)]
    if has_state:
        cn_scr = next(it)
    if grid2d:
        m1_scr = next(it)

    L = MLSTM_L
    n_tiles = T // TILE_M
    n_chunks = T // L
    RB = MLSTM_ROW_BLOCK
    NW = MERGE_COLS

    mod = mod_ref[...]
    sh1 = mod[:, 0:D_MODEL]
    sc1 = mod[:, D_MODEL:2 * D_MODEL]
    gt1 = mod[:, 2 * D_MODEL:3 * D_MODEL]
    g1 = g1_ref[...]

    lane = lax.broadcasted_iota(jnp.int32, (2 * N_HEADS, TILE_M), 1)
    is_fwd = lax.broadcasted_iota(jnp.int32, (2 * N_HEADS, TILE_M), 0) < N_HEADS
    ones_blk = jnp.ones((TILE_M, HEAD_DIM), BF16)

    def phase_a(i, carry):
        r0 = _aligned(i * TILE_M, TILE_M)
        rows = pl.ds(r0, TILE_M)
        x = x_ref[rows, :]
        ms = jnp.mean(x * x, axis=-1, keepdims=True)
        u = x * lax.rsqrt(ms + EPS) * g1 * (1.0 + sc1) + sh1
        ub = u.astype(BF16)
        u_scr[rows, :] = ub

        gate_t = _dot(ub, wgate_ref[...]).T
        gp = gate_t[0:N_GATE, :] + bgates_ref[...]
        i_all = gp[0:2 * N_HEADS, :]
        f_all = gp[2 * N_HEADS:4 * N_HEADS, :]
        logf = jnp.minimum(f_all, 0.0) - jnp.log1p(jnp.exp(-jnp.abs(f_all)))
        b_row = jnp.where(is_fwd, _seg_scan(logf, lane, L, L, "sum", False),
                          _seg_scan(logf, lane, L, L, "sum", True))
        g_row = i_all - b_row
        cm_row = jnp.where(is_fwd, _seg_scan(g_row, lane, L, L, "max", False),
                           _seg_scan(g_row, lane, L, L, "max", True))
        grow_scr[:, rows] = g_row
        stacked = jnp.concatenate(
            [b_row, cm_row, jnp.zeros((GATE_PAD - 4 * N_HEADS, TILE_M), F32)], axis=0)
        colf_scr[rows, :] = stacked.T

        def proj(k):
            return _dot(ub, wmain_ref[:, k * D_BRANCH:(k + 1) * D_BRANCH])

        xp_scr[pl.ds(_aligned(r0 + POOL_PAD, 8), TILE_M), :] = proj(0)
        su_scr[...] = proj(1)
        sv = proj(2)
        vn = sv * lax.rsqrt(jnp.mean(sv * sv, axis=-1, keepdims=True) + EPS) * gsgu_ref[...]
        vn_scr[...] = vn.astype(BF16)
        q_scr[rows, :] = proj(3).astype(BF16)
        kt_scr[:, rows] = (_dot_nt(wkt_ref[...], ub) * (HEAD_DIM ** -0.5)).astype(BF16)
        v = proj(5).astype(BF16)
        for h in range(N_HEADS):
            vaug_scr[rows, 2 * h * HEAD_DIM:(2 * h + 1) * HEAD_DIM] = v[:, h * HEAD_DIM:(h + 1) * HEAD_DIM]
            vaug_scr[rows, (2 * h + 1) * HEAD_DIM:(2 * h + 2) * HEAD_DIM] = ones_blk
        o_scr[rows, :] = proj(6)

        for ch in range(TILE_M // SGU_CHUNK):
            crow = slice(ch * SGU_CHUNK, (ch + 1) * SGU_CHUNK)
            for g in range(N_GROUPS):
                gc = slice(g * GROUP, (g + 1) * GROUP)
                mixed = _dot(wsgu_ref[g], vn_scr[crow, gc]) + bsgu_ref[:, gc]
                yb = su_scr[crow, gc] * mixed
                yb_scr[pl.ds(_aligned(r0 + ch * SGU_CHUNK, SGU_CHUNK), SGU_CHUNK), gc] = yb.astype(BF16)
        return carry

    _loop(n_tiles, phase_a)

    zpad = jnp.zeros((POOL_PAD, D_BRANCH), F32)
    xp_scr[0:POOL_PAD, :] = zpad
    xp_scr[T + POOL_PAD:T + 2 * POOL_PAD, :] = zpad
    pscale = pscale_ref[...]

    if not grid2d:
        t_idx = lax.broadcasted_iota(jnp.int32, (T, GROUP), 0)
        for g, win in enumerate(POOL_WINDOWS):
            gc = slice(g * GROUP, (g + 1) * GROUP)
            half = win // 2
            total = jnp.zeros((T, GROUP), F32)
            for j in range(-half, win - half):
                total = total + xp_scr[POOL_PAD + j:POOL_PAD + j + T, gc]
            cnt = (jnp.minimum(t_idx + (win - half), T) - jnp.maximum(t_idx - half, 0)).astype(F32)
            dlt = total / cnt - xp_scr[POOL_PAD:POOL_PAD + T, gc]
            ya = _dot(dlt.astype(BF16), wpool_ref[g]) * pscale[:, gc]
            ya_scr[:, gc] = ya.astype(BF16)
    else:
        n_rows = T // GRID_W
        c_idx = lax.broadcasted_iota(jnp.int32, (T, GROUP), 0) & (GRID_W - 1)
        for g, win in enumerate(POOL_WINDOWS):
            gc = slice(g * GROUP, (g + 1) * GROUP)
            half = win // 2
            total = jnp.zeros((T, GROUP), F32)
            for j in range(-half, win - half):
                valid = (c_idx + j >= 0) & (c_idx + j < GRID_W)
                total = total + jnp.where(valid, xp_scr[POOL_PAD + j:POOL_PAD + j + T, gc], 0.0)
            cnt = (jnp.minimum(c_idx + (win - half), GRID_W) - jnp.maximum(c_idx - half, 0)).astype(F32)
            m1_scr[...] = total / cnt
            for r in range(n_rows):
                lo = max(r - half, 0)
                hi = min(r + (win - half), n_rows)
                acc = m1_scr[lo * GRID_W:(lo + 1) * GRID_W, :]
                for rr in range(lo + 1, hi):
                    acc = acc + m1_scr[rr * GRID_W:(rr + 1) * GRID_W, :]
                xg = xp_scr[POOL_PAD + r * GRID_W:POOL_PAD + (r + 1) * GRID_W, gc]
                dlt = acc / float(hi - lo) - xg
                ya = _dot(dlt.astype(BF16), wpool_ref[g]) * pscale[:, gc]
                ya_scr[r * GRID_W:(r + 1) * GRID_W, gc] = ya.astype(BF16)

    if has_state:
        cn_scr[...] = cn0_ref[...]
        m_init = m0_ref[...]
        m_start = (m_init[0:1, :], m_init[1:2, :])
    else:
        m_start = (jnp.zeros((1, N_HEADS), F32), jnp.zeros((1, N_HEADS), F32))

    def keep_mask(d, rb, col_lo, ncols):
        t_loc = rb * RB + lax.broadcasted_iota(jnp.int32, (RB, ncols), 0)
        s_loc = col_lo + lax.broadcasted_iota(jnp.int32, (RB, ncols), 1)
        return s_loc <= t_loc if d == 0 else s_loc >= t_loc

    def mlstm_chunk(d, c, m_in, fillers):
        c0 = _aligned(c * L, L)
        rows = pl.ds(c0, L)
        colf = colf_scr[rows, :]
        bcol = colf[:, N_HEADS * d:N_HEADS * (d + 1)]
        cmcol = colf[:, 2 * N_HEADS + N_HEADS * d:2 * N_HEADS + N_HEADS * (d + 1)]
        mx = jnp.maximum(m_in, cmcol)
        last = L - 1 if d == 0 else 0
        mx_last = mx[last:last + 1, :]
        w_inter = jnp.exp(m_in - mx)
        e_negm = jnp.exp(-(bcol + mx))
        decay = jnp.exp(m_in - mx_last)
        m_new = bcol[last:last + 1, :] + mx_last
        for h in range(N_HEADS):
            hc = slice(h * HEAD_DIM, (h + 1) * HEAD_DIM)
            ac = slice(2 * h * HEAD_DIM, (2 * h + 2) * HEAD_DIM)
            grow = grow_scr[N_HEADS * d + h:N_HEADS * d + h + 1, rows]
            if has_state:
                cn_dh = cn_scr[d, h]
                cn_bf = cn_dh.astype(BF16)
            for rb in range(L // RB):
                col_lo, col_hi = (0, (rb + 1) * RB) if d == 0 else (rb * RB, L)
                ncols = col_hi - col_lo
                trow = slice(rb * RB, (rb + 1) * RB)
                rrows = pl.ds(_aligned(c0 + rb * RB, RB), RB)
                crows = pl.ds(_aligned(c0 + col_lo, RB), ncols)
                qb = q_scr[rrows, hc]
                qk = _dot(qb, kt_scr[hc, crows])
                w = jnp.exp(grow[:, col_lo:col_hi] - mx[trow, h:h + 1])
                s = (qk * jnp.where(keep_mask(d, rb, col_lo, ncols), w, 0.0)).astype(BF16)
                na = _dot(s, vaug_scr[crows, ac])
                if has_state:
                    na = na + w_inter[trow, h:h + 1] * _dot(qb, cn_bf)
                hval = na[:, 0:HEAD_DIM] / jnp.maximum(jnp.abs(na[:, HEAD_DIM:]), e_negm[trow, h:h + 1])
                if d == 0:
                    h_scr[rrows, hc] = hval
                else:
                    h_scr[rrows, hc] += hval
            w_state = jnp.exp(grow - mx_last[:, h:h + 1])
            kw_t = (kt_scr[hc, rows].astype(F32) * w_state).astype(BF16)
            upd = _dot(kw_t, vaug_scr[rows, ac])
            if has_state:
                cn_scr[d, h] = decay[:, h:h + 1] * cn_dh + upd
            else:
                cout_ref[d, h] = upd[:, 0:HEAD_DIM]
                nout_ref[d, h:h + 1, :] = upd[:, HEAD_DIM:].T[0:1, :]
            if h < len(fillers):
                fillers[h]()
        for f in fillers[N_HEADS:]:
            f()
        return m_new

    def merge_ab(c, nj):
        rows = pl.ds(_aligned(c * TILE_M, TILE_M), TILE_M)
        nc = slice(nj * NW, (nj + 1) * NW)
        ub = u_scr[rows, :]
        acc = None
        for r, y_scr in enumerate((ya_scr, yb_scr)):
            gate = jax.nn.sigmoid(_dot(ub, wbrg_ref[:, r * D_MODEL + nj * NW:r * D_MODEL + (nj + 1) * NW]))
            term = gate * _dot(y_scr[rows, :], wbranch_ref[r, :, nc])
            acc = term if acc is None else acc + term
        mab_scr[rows, nc] = acc

    def fwd_step(j, m_in):
        return mlstm_chunk(0, j, m_in, [functools.partial(merge_ab, j, nj) for nj in range(D_MODEL // NW)])

    def gate_c(c, nj):
        rows = pl.ds(_aligned(c * TILE_M, TILE_M), TILE_M)
        w = wbrg_ref[:, 2 * D_MODEL + nj * NW:2 * D_MODEL + (nj + 1) * NW]
        gc_scr[:, nj * NW:(nj + 1) * NW] = jax.nn.sigmoid(_dot(u_scr[rows, :], w))

    def bwd_step(j, m_in):
        c = n_chunks - 1 - j
        m_new = mlstm_chunk(1, c, m_in, [functools.partial(gate_c, c, nj) for nj in range(D_MODEL // NW)])
        rows = pl.ds(_aligned(c * TILE_M, TILE_M), TILE_M)
        gml = gml_ref[...]
        for h in range(N_HEADS):
            hc = slice(h * HEAD_DIM, (h + 1) * HEAD_DIM)
            hh = h_scr[rows, hc]
            hn = hh * lax.rsqrt(jnp.mean(hh * hh, axis=-1, keepdims=True) + EPS) * gml[:, hc]
            yc_scr[:, hc] = (jax.nn.sigmoid(o_scr[rows, hc]) * hn).astype(BF16)
        for nj in range(D_MODEL // NW):
            nc = slice(nj * NW, (nj + 1) * NW)
            acc = mab_scr[rows, nc] + gc_scr[:, nc] * _dot(yc_scr[...], wbranch_ref[2, :, nc])
            merged_scr[:, nc] = acc.astype(BF16)
        out_ref[rows, :] = x_ref[rows, :] + gt1 * _dot(merged_scr[...], wout_ref[...])
        return m_new

    if n_chunks == 1:
        m_f = fwd_step(0, m_start[0])
        m_b = bwd_step(0, m_start[1])
    else:
        m_f = lax.fori_loop(0, n_chunks, fwd_step, m_start[0])
        m_b = lax.fori_loop(0, n_chunks, bwd_step, m_start[1])
    if not has_state:
        mout_ref[0:1, :] = m_f
        mout_ref[1:2, :] = m_b


def _mixer_call(l, xall, mods4, prm, stream, states=None, state_outs=None):
    ctx = stream == "ctx"
    T = SEQ if ctx else DEC_SEQ
    assert T % TILE_M == 0 and TILE_M == MLSTM_L and MLSTM_L % MLSTM_ROW_BLOCK == 0
    n_b = BATCH if ctx else DEC_BATCH
    blk0 = 0 if ctx else N_TOK_CTX // T
    once = pl.Buffered(1)

    def wspec(shape):
        nd = len(shape)
        return pl.BlockSpec((None,) + shape, lambda b: (l,) + (0,) * nd, pipeline_mode=once)

    x_spec = pl.BlockSpec((T, D_MODEL), lambda b: (blk0 + b, 0), pipeline_mode=None if ctx else once)
    if ctx:
        mod_spec = pl.BlockSpec((None, None, 1, 6 * D_MODEL), lambda b: (l, 0, 0, 0))
    else:
        mod_spec = pl.BlockSpec((None, None, 1, 6 * D_MODEL), lambda b: (l, 1 + b, 0, 0))
    in_specs = [
        x_spec, mod_spec,
        wspec((1, D_MODEL)),
        wspec((D_MODEL, N_MAIN)),
        wspec((D_BRANCH, D_MODEL)),
        wspec((D_MODEL, GATE_PAD)),
        wspec((D_MODEL, N_BRANCH * D_MODEL)),
        wspec((N_BRANCH, D_BRANCH, D_MODEL)),
        wspec((D_MODEL, D_MODEL)),
        wspec((N_GROUPS, GROUP, GROUP)),
        wspec((1, D_BRANCH)),
        wspec((1, D_BRANCH)),
        wspec((N_GROUPS, SGU_CHUNK, SGU_CHUNK)),
        wspec((SGU_CHUNK, D_BRANCH)),
        wspec((N_GATE, 1)),
        wspec((1, D_BRANCH)),
    ]
    args = [xall, mods4, prm["g_norm1"], prm["w_in16"], prm["w_kt"], prm["w_gate"], prm["w_brg"],
            prm["w_branch"], prm["w_out"], prm["w_pool"], prm["pool_scale"], prm["g_sgu"], prm["w_sgu"],
            prm["b_sgu_tile"], prm["b_gates"], prm["g_mlstm"]]
    out_shape = [jax.ShapeDtypeStruct(xall.shape, xall.dtype)]
    out_specs = [x_spec]
    aliases = {0: 0}
    if ctx:
        in_specs += [pl.BlockSpec(memory_space=pl.ANY)] * 3
        aliases.update({len(args): 1, len(args) + 1: 2, len(args) + 2: 3})
        args += list(state_outs)
        out_shape += [jax.ShapeDtypeStruct(a.shape, a.dtype) for a in state_outs]
        out_specs += [
            pl.BlockSpec((None, None, 2, N_HEADS, HEAD_DIM, HEAD_DIM), lambda b: (b, l, 0, 0, 0, 0)),
            pl.BlockSpec((None, None, 2, N_HEADS, HEAD_DIM), lambda b: (b, l, 0, 0, 0)),
            pl.BlockSpec((None, None, 2, N_HEADS), lambda b: (b, l, 0, 0)),
        ]
    else:
        state_cn, state_m = states
        in_specs += [
            pl.BlockSpec((None, None, 2, N_HEADS, HEAD_DIM, 2 * HEAD_DIM), lambda b: (b, l, 0, 0, 0, 0)),
            pl.BlockSpec((None, None, 2, N_HEADS), lambda b: (b, l, 0, 0)),
        ]
        args += [state_cn, state_m]

    scratch = [
        pltpu.VMEM((T, D_MODEL), BF16),
        pltpu.VMEM((T + 2 * POOL_PAD, D_BRANCH), F32),
        pltpu.VMEM((TILE_M, D_BRANCH), F32),
        pltpu.VMEM((TILE_M, D_BRANCH), BF16),
        pltpu.VMEM((T, D_BRANCH), BF16),
        pltpu.VMEM((D_BRANCH, T), BF16),
        pltpu.VMEM((T, 2 * D_BRANCH), BF16),
        pltpu.VMEM((T, D_BRANCH), F32),
        pltpu.VMEM((2 * N_HEADS, T), F32),
        pltpu.VMEM((T, GATE_PAD), F32),
        pltpu.VMEM((T, D_BRANCH), F32),
        pltpu.VMEM((T, D_BRANCH), BF16),
        pltpu.VMEM((T, D_BRANCH), BF16),
        pltpu.VMEM((TILE_M, D_BRANCH), BF16),
        pltpu.VMEM((T, D_MODEL), F32),
        pltpu.VMEM((TILE_M, D_MODEL), F32),
        pltpu.VMEM((TILE_M, D_MODEL), BF16),
    ]
    if not ctx:
        scratch += [pltpu.VMEM((2, N_HEADS, HEAD_DIM, 2 * HEAD_DIM), F32)]
        scratch += [pltpu.VMEM((T, GROUP), F32)]

    outs = pl.pallas_call(
        functools.partial(_mixer_kernel, T=T, grid2d=not ctx, has_state=not ctx),
        grid=(n_b,),
        in_specs=in_specs,
        out_specs=out_specs,
        out_shape=out_shape,
        scratch_shapes=scratch,
        input_output_aliases=aliases,
        compiler_params=pltpu.CompilerParams(
            dimension_semantics=("arbitrary",), vmem_limit_bytes=VMEM_LIMIT_BYTES),
        name="mixer_ctx" if ctx else "mixer_lat",
    )(*args)
    return outs


def _ffn_kernel(x_ref, mod_ref, g2_ref, w1_hbm, w2_hbm, gfin_ref, out_ref,
                acc_scr, h_scr, w1_scr, w2_scr, st1_scr, st2_scr, sem, *, layer, final):
    n_chunks = D_FF // FFN_CHUNK
    first_tile = pl.program_id(0) == 0

    def w1_copy(c):
        slot = c % FFN_STAGES
        c0 = _aligned(c * FFN_CHUNK, FFN_CHUNK)
        return pltpu.make_async_copy(w1_hbm.at[layer, :, pl.ds(c0, FFN_CHUNK)], st1_scr.at[slot], sem.at[0, slot])

    def w2_copy(c):
        slot = c % FFN_STAGES
        c0 = _aligned(c * FFN_CHUNK, FFN_CHUNK)
        return pltpu.make_async_copy(w2_hbm.at[layer, pl.ds(c0, FFN_CHUNK), :], st2_scr.at[slot], sem.at[1, slot])

    def land_w1(c):
        w1_copy(c).wait()
        c0 = _aligned(c * FFN_CHUNK, FFN_CHUNK)
        w1_scr[:, pl.ds(c0, FFN_CHUNK)] = st1_scr[c % FFN_STAGES].astype(BF16)

    def land_w2(c):
        w2_copy(c).wait()
        c0 = _aligned(c * FFN_CHUNK, FFN_CHUNK)
        w2_scr[pl.ds(c0, FFN_CHUNK), :] = st2_scr[c % FFN_STAGES].astype(BF16)

    def normed_input():
        mod = mod_ref[...]
        sh2 = mod[:, 3 * D_MODEL:4 * D_MODEL]
        sc2 = mod[:, 4 * D_MODEL:5 * D_MODEL]
        x = x_ref[...]
        ms = jnp.mean(x * x, axis=-1, keepdims=True)
        return (x * lax.rsqrt(ms + EPS) * g2_ref[...] * (1.0 + sc2) + sh2).astype(BF16)

    def hidden(ub, c):
        c0 = _aligned(c * FFN_CHUNK, FFN_CHUNK)
        h = _dot(ub, w1_scr[:, pl.ds(c0, FFN_CHUNK)])
        h_scr[:, pl.ds(c0, FFN_CHUNK)] = jnp.square(jnp.maximum(h, 0.0)).astype(BF16)

    def finish(ffn_out):
        gt2 = mod_ref[...][:, 5 * D_MODEL:6 * D_MODEL]
        y = x_ref[...] + gt2 * ffn_out
        if final:
            y = y * lax.rsqrt(jnp.mean(y * y, axis=-1, keepdims=True) + EPS) * gfin_ref[...]
        out_ref[...] = y

    @pl.when(first_tile)
    def _():
        for c in range(FFN_STAGES):
            w1_copy(c).start()
            w2_copy(c).start()
        ub = normed_input()
        acc_scr[...] = jnp.zeros_like(acc_scr)

        def chunk(c, carry):
            land_w1(c)
            land_w2(c)

            @pl.when(c + FFN_STAGES < n_chunks)
            def _():
                w1_copy(c + FFN_STAGES).start()
                w2_copy(c + FFN_STAGES).start()

            hidden(ub, c)
            c0 = _aligned(c * FFN_CHUNK, FFN_CHUNK)
            acc_scr[...] += _dot(h_scr[:, pl.ds(c0, FFN_CHUNK)], w2_scr[pl.ds(c0, FFN_CHUNK), :])
            return carry

        lax.fori_loop(0, n_chunks, chunk, 0)
        finish(acc_scr[...])

    @pl.when(jnp.logical_not(first_tile))
    def _():
        ub = normed_input()
        for c in range(n_chunks):
            hidden(ub, c)
        finish(_dot(h_scr[...], w2_scr[...]))


def _ffn_call(l, xall, mods4, prm, g_final, final):
    ctx_tiles = N_TOK_CTX // FFN_TILE_M
    lat_tiles = DEC_SEQ // FFN_TILE_M
    once = pl.Buffered(1)

    def mod_map(t):
        return (l, jnp.where(t < ctx_tiles, 0, 1 + (t - ctx_tiles) // lat_tiles), 0, 0)

    x_spec = pl.BlockSpec((FFN_TILE_M, D_MODEL), lambda t: (t, 0))
    return pl.pallas_call(
        functools.partial(_ffn_kernel, layer=l, final=final),
        grid=(N_TOK // FFN_TILE_M,),
        in_specs=[
            x_spec,
            pl.BlockSpec((None, None, 1, 6 * D_MODEL), mod_map),
            pl.BlockSpec((None, 1, D_MODEL), lambda t: (l, 0, 0), pipeline_mode=once),
            pl.BlockSpec(memory_space=pl.ANY),
            pl.BlockSpec(memory_space=pl.ANY),
            pl.BlockSpec((1, D_MODEL), lambda t: (0, 0)),
        ],
        out_specs=x_spec,
        out_shape=jax.ShapeDtypeStruct(xall.shape, xall.dtype),
        scratch_shapes=[
            pltpu.VMEM((FFN_TILE_M, D_MODEL), F32),
            pltpu.VMEM((FFN_TILE_M, D_FF), BF16),
            pltpu.VMEM((D_MODEL, D_FF), BF16),
            pltpu.VMEM((D_FF, D_MODEL), BF16),
            pltpu.VMEM((FFN_STAGES, D_MODEL, FFN_CHUNK), F32),
            pltpu.VMEM((FFN_STAGES, FFN_CHUNK, D_MODEL), F32),
            pltpu.SemaphoreType.DMA((2, FFN_STAGES)),
        ],
        input_output_aliases={0: 0},
        compiler_params=pltpu.CompilerParams(
            dimension_semantics=("arbitrary",), vmem_limit_bytes=VMEM_LIMIT_BYTES),
        name="ffn_final" if final else "ffn",
    )(xall, mods4, prm["g_norm2"], prm["w_ff1"], prm["w_ff2"], g_final.reshape(1, D_MODEL))


def kernel(x_prompt, x_sample, state_C, state_n, state_m, c, c_ctx, w_ada, b_ada, g_norm1, g_norm2, w_in,
           b_gates, w_pool, pool_scale, g_sgu, w_sgu, b_sgu, g_mlstm, w_branch, w_out, w_ff1, w_ff2, g_final):
    w_in16 = w_in.astype(BF16)
    prm = {
        "w_in16": w_in16,
        "w_gate": jnp.pad(w_in16[:, :, N_MAIN:N_MAIN + N_GATE], ((0, 0), (0, 0), (0, GATE_PAD - N_GATE))),
        "w_brg": w_in16[:, :, N_MAIN + N_GATE:],
        "w_kt": jnp.swapaxes(w_in16[:, :, 4 * D_BRANCH:5 * D_BRANCH], 1, 2),
        "w_branch": w_branch.astype(BF16),
        "w_out": w_out.astype(BF16),
        "w_pool": w_pool.astype(BF16),
        "w_sgu": w_sgu.astype(BF16),
        "w_ff1": w_ff1,
        "w_ff2": w_ff2,
        "g_norm1": g_norm1.reshape(DEPTH, 1, D_MODEL),
        "g_norm2": g_norm2.reshape(DEPTH, 1, D_MODEL),
        "pool_scale": pool_scale.reshape(DEPTH, 1, D_BRANCH),
        "g_sgu": g_sgu.reshape(DEPTH, 1, D_BRANCH),
        "g_mlstm": g_mlstm.reshape(DEPTH, 1, D_BRANCH),
        "b_gates": b_gates.reshape(DEPTH, N_GATE, 1),
        "b_sgu_tile": jnp.repeat(jnp.swapaxes(b_sgu, 1, 2), GROUP, axis=2),
    }
    cond8 = jnp.zeros((COND_ROWS, D_MODEL), F32).at[0].set(c_ctx).at[1:1 + DEC_BATCH].set(c)
    mods = _ada_call(cond8, w_ada, b_ada)
    mods4 = mods.reshape(DEPTH, COND_ROWS, 1, 6 * D_MODEL)

    xall = jnp.concatenate([x_prompt.reshape(N_TOK_CTX, D_MODEL), x_sample.reshape(-1, D_MODEL)], axis=0)
    state_cn = jnp.concatenate(
        [state_C, jnp.broadcast_to(state_n[..., None], state_n.shape + (HEAD_DIM,))], axis=-1)
    new_c = jnp.zeros((BATCH, DEPTH, 2, N_HEADS, HEAD_DIM, HEAD_DIM), F32)
    new_n = jnp.zeros((BATCH, DEPTH, 2, N_HEADS, HEAD_DIM), F32)
    new_m = jnp.zeros((BATCH, DEPTH, 2, N_HEADS), F32)
    for l in range(DEPTH):
        xall, new_c, new_n, new_m = _mixer_call(l, xall, mods4, prm, "ctx", state_outs=(new_c, new_n, new_m))
        (xall,) = _mixer_call(l, xall, mods4, prm, "lat", states=(state_cn, state_m))
        xall = _ffn_call(l, xall, mods4, prm, g_final, final=(l == DEPTH - 1))
    y_prompt = xall[:N_TOK_CTX].reshape(BATCH, SEQ, D_MODEL)
    y_sample = xall[N_TOK_CTX:].reshape(DEC_BATCH, DEC_SEQ, D_MODEL)
    return (y_prompt, y_sample, new_c, new_n, new_m)
```

```python
import functools

import jax
import jax.numpy as jnp
from jax import lax
from jax.experimental import pallas as pl
from jax.experimental.pallas import tpu as pltpu

F32 = jnp.float32
BF16 = jnp.bfloat16

D_MODEL = 1024
DEPTH = 4
BATCH = 16
SEQ = 256
DEC_BATCH = 2
DEC_SEQ = 1024
GRID_W = 64
D_BRANCH = 512
POOL_WINDOWS = (2, 4, 8, 16)
N_GROUPS = 4
GROUP = D_BRANCH // N_GROUPS
SGU_CHUNK = 128
N_HEADS = 4
HEAD_DIM = D_BRANCH // N_HEADS
N_BRANCH = 3
D_FF = 4 * D_MODEL
EPS = 1e-6

N_MAIN = 7 * D_BRANCH
N_GATE = 4 * N_HEADS
GATE_PAD = 128
N_TOK_CTX = BATCH * SEQ
N_TOK = N_TOK_CTX + DEC_BATCH * DEC_SEQ
COND_ROWS = 8

TILE_M = 256
CTX_SEQS_PER_STEP = 2
MLSTM_L = 256
MLSTM_ROW_BLOCK = 128
MERGE_COLS = 256
POOL_PAD = 8
FFN_TILE_M = 512
FFN_CHUNK = 512
FFN_STAGES = 2
ADA_TILE_N = 1536
VMEM_LIMIT_BYTES = 60 * 1024 * 1024


_dot = functools.partial(jnp.dot, preferred_element_type=F32)
_dot_nt = functools.partial(lax.dot_general, dimension_numbers=(((1,), (1,)), ((), ())),
                            preferred_element_type=F32)


def _aligned(offset, align):
    return offset if isinstance(offset, int) else pl.multiple_of(offset, align)


def _ada_kernel(cond_ref, w_ref, b_ref, out_ref):
    cond = cond_ref[...]
    s = (cond * jax.nn.sigmoid(cond)).astype(BF16)
    out_ref[...] = _dot(s, w_ref[...].astype(BF16)) + b_ref[...]


def _ada_call(cond8, w_ada, b_ada):
    n_out = 6 * D_MODEL
    return pl.pallas_call(
        _ada_kernel,
        grid=(DEPTH, n_out // ADA_TILE_N),
        in_specs=[
            pl.BlockSpec((COND_ROWS, D_MODEL), lambda l, j: (0, 0)),
            pl.BlockSpec((None, D_MODEL, ADA_TILE_N), lambda l, j: (l, 0, j)),
            pl.BlockSpec((None, 1, ADA_TILE_N), lambda l, j: (l, 0, j)),
        ],
        out_specs=pl.BlockSpec((None, COND_ROWS, ADA_TILE_N), lambda l, j: (l, 0, j)),
        out_shape=jax.ShapeDtypeStruct((DEPTH, COND_ROWS, n_out), F32),
        compiler_params=pltpu.CompilerParams(
            dimension_semantics=("arbitrary", "arbitrary"), vmem_limit_bytes=VMEM_LIMIT_BYTES),
        name="ada_mod",
    )(cond8, w_ada, b_ada.reshape(DEPTH, 1, n_out))


def _seg_scan(x, pos, seg, total, op, reverse):
    d = 1
    while d < seg:
        if reverse:
            shifted = pltpu.roll(x, total - d, 1)
            valid = pos < seg - d
        else:
            shifted = pltpu.roll(x, d, 1)
            valid = pos >= d
        if op == "sum":
            x = x + jnp.where(valid, shifted, 0.0)
        else:
            x = jnp.maximum(x, jnp.where(valid, shifted, -jnp.inf))
        d *= 2
    return x


def _interleave(first, second):
    out, i, j = [], 0, 0
    while i < len(first) or j < len(second):
        if j >= len(second) or (i < len(first) and i * len(second) <= j * len(first)):
            out.append(first[i])
            i += 1
        else:
            out.append(second[j])
            j += 1
    return out


def _mixer_kernel(*refs, T, grid2d, has_state, n_elems, n_in, n_out):
    ins, outs, scr = refs[:n_in], refs[n_in:n_in + n_out], refs[n_in + n_out:]
    stages = []
    for e in range(n_elems):
        rows = slice(e * T, (e + 1) * T)
        ins_e = [ins[0].at[rows, :]] + list(ins[1:])
        outs_e = [outs[0].at[rows, :]] + [o.at[e] for o in outs[1:]]
        scr_e = [s.at[e] for s in scr]
        stages.append(_mixer_sequence(*ins_e, *outs_e, *scr_e, T=T, grid2d=grid2d, has_state=has_state))
    order = list(stages[0][0])
    for e in range(n_elems):
        order += _interleave(stages[e][1], stages[e + 1][0] if e + 1 < n_elems else [])
    for thunk in order:
        thunk()


def _mixer_sequence(*refs, T, grid2d, has_state):
    it = iter(refs)
    x_ref = next(it)
    mod_ref = next(it)
    g1_ref = next(it)
    wmain_ref = next(it)
    wkt_ref = next(it)
    wgate_ref = next(it)
    wbrg_ref = next(it)
    wbranch_ref = next(it)
    wout_ref = next(it)
    wpool_ref = next(it)
    pscale_ref = next(it)
    gsgu_ref = next(it)
    wsgu_ref = next(it)
    bsgu_ref = next(it)
    bgates_ref = next(it)
    gml_ref = next(it)
    if has_state:
        cn0_ref, m0_ref = next(it), next(it)
    else:
        next(it), next(it), next(it)
    out_ref = next(it)
    if not has_state:
        cout_ref, nout_ref, mout_ref = next(it), next(it), next(it)
    (u_scr, xp_scr, su_scr, vn_scr, q_scr, kt_scr, vaug_scr, o_scr, grow_scr, colf_scr,
     h_scr, ya_scr, yb_scr, yc_scr, mab_scr, gc_scr, merged_scr) = [next(it) for _ in range(17)]
    if has_state:
        cn_scr = next(it)
    if grid2d:
        m1_scr = next(it)

    L = MLSTM_L
    n_tiles = T // TILE_M
    n_chunks = T // L
    RB = MLSTM_ROW_BLOCK
    NW = MERGE_COLS
    n_merge = D_MODEL // NW

    def tile_rows(i):
        return pl.ds(_aligned(i * TILE_M, TILE_M), TILE_M)

    def a_norm(i):
        mod = mod_ref[...]
        sh1 = mod[:, 0:D_MODEL]
        sc1 = mod[:, D_MODEL:2 * D_MODEL]
        x = x_ref[tile_rows(i), :]
        ms = jnp.mean(x * x, axis=-1, keepdims=True)
        u = x * lax.rsqrt(ms + EPS) * g1_ref[...] * (1.0 + sc1) + sh1
        u_scr[tile_rows(i), :] = u.astype(BF16)

    def a_gates(i):
        lane = lax.broadcasted_iota(jnp.int32, (2 * N_HEADS, TILE_M), 1)
        is_fwd = lax.broadcasted_iota(jnp.int32, (2 * N_HEADS, TILE_M), 0) < N_HEADS
        gate_t = _dot(u_scr[tile_rows(i), :], wgate_ref[...]).T
        gp = gate_t[0:N_GATE, :] + bgates_ref[...]
        i_all = gp[0:2 * N_HEADS, :]
        f_all = gp[2 * N_HEADS:4 * N_HEADS, :]
        logf = jnp.minimum(f_all, 0.0) - jnp.log1p(jnp.exp(-jnp.abs(f_all)))
        b_row = jnp.where(is_fwd, _seg_scan(logf, lane, L, L, "sum", False),
                          _seg_scan(logf, lane, L, L, "sum", True))
        g_row = i_all - b_row
        cm_row = jnp.where(is_fwd, _seg_scan(g_row, lane, L, L, "max", False),
                           _seg_scan(g_row, lane, L, L, "max", True))
        grow_scr[:, tile_rows(i)] = g_row
        stacked = jnp.concatenate(
            [b_row, cm_row, jnp.zeros((GATE_PAD - 4 * N_HEADS, TILE_M), F32)], axis=0)
        colf_scr[tile_rows(i), :] = stacked.T

    def proj(i, k):
        return _dot(u_scr[tile_rows(i), :], wmain_ref[:, k * D_BRANCH:(k + 1) * D_BRANCH])

    def a_xp(i):
        xp_scr[pl.ds(_aligned(i * TILE_M + POOL_PAD, 8), TILE_M), :] = proj(i, 0)

    def a_su(i):
        su_scr[...] = proj(i, 1)

    def a_sv(i):
        sv = proj(i, 2)
        vn = sv * lax.rsqrt(jnp.mean(sv * sv, axis=-1, keepdims=True) + EPS) * gsgu_ref[...]
        vn_scr[...] = vn.astype(BF16)

    def a_q(i):
        q_scr[tile_rows(i), :] = proj(i, 3).astype(BF16)

    def a_k(i):
        kt = _dot_nt(wkt_ref[...], u_scr[tile_rows(i), :])
        kt_scr[:, tile_rows(i)] = (kt * (HEAD_DIM ** -0.5)).astype(BF16)

    def a_v(i):
        v = proj(i, 5).astype(BF16)
        ones_blk = jnp.ones((TILE_M, HEAD_DIM), BF16)
        for h in range(N_HEADS):
            vaug_scr[tile_rows(i), 2 * h * HEAD_DIM:(2 * h + 1) * HEAD_DIM] = v[:, h * HEAD_DIM:(h + 1) * HEAD_DIM]
            vaug_scr[tile_rows(i), (2 * h + 1) * HEAD_DIM:(2 * h + 2) * HEAD_DIM] = ones_blk

    def a_o(i):
        o_scr[tile_rows(i), :] = proj(i, 6)

    def a_sgu(ch, i):
        crow = slice(ch * SGU_CHUNK, (ch + 1) * SGU_CHUNK)
        for g in range(N_GROUPS):
            gc = slice(g * GROUP, (g + 1) * GROUP)
            mixed = _dot(wsgu_ref[g], vn_scr[crow, gc]) + bsgu_ref[:, gc]
            yb = su_scr[crow, gc] * mixed
            yb_scr[pl.ds(_aligned(i * TILE_M + ch * SGU_CHUNK, SGU_CHUNK), SGU_CHUNK), gc] = yb.astype(BF16)

    a_parts = [a_norm, a_gates, a_xp, a_su, a_sv, a_q, a_k, a_v, a_o]
    a_parts += [functools.partial(a_sgu, ch) for ch in range(TILE_M // SGU_CHUNK)]

    def pool_pad():
        zpad = jnp.zeros((POOL_PAD, D_BRANCH), F32)
        xp_scr[0:POOL_PAD, :] = zpad
        xp_scr[T + POOL_PAD:T + 2 * POOL_PAD, :] = zpad

    def pool_seq(g):
        win = POOL_WINDOWS[g]
        gc = slice(g * GROUP, (g + 1) * GROUP)
        half = win // 2
        t_idx = lax.broadcasted_iota(jnp.int32, (T, GROUP), 0)
        total = jnp.zeros((T, GROUP), F32)
        for j in range(-half, win - half):
            total = total + xp_scr[POOL_PAD + j:POOL_PAD + j + T, gc]
        cnt = (jnp.minimum(t_idx + (win - half), T) - jnp.maximum(t_idx - half, 0)).astype(F32)
        dlt = total / cnt - xp_scr[POOL_PAD:POOL_PAD + T, gc]
        ya = _dot(dlt.astype(BF16), wpool_ref[g]) * pscale_ref[...][:, gc]
        ya_scr[:, gc] = ya.astype(BF16)

    def pool_grid(g):
        win = POOL_WINDOWS[g]
        gc = slice(g * GROUP, (g + 1) * GROUP)
        half = win // 2
        n_rows = T // GRID_W
        c_idx = lax.broadcasted_iota(jnp.int32, (T, GROUP), 0) & (GRID_W - 1)
        total = jnp.zeros((T, GROUP), F32)
        for j in range(-half, win - half):
            valid = (c_idx + j >= 0) & (c_idx + j < GRID_W)
            total = total + jnp.where(valid, xp_scr[POOL_PAD + j:POOL_PAD + j + T, gc], 0.0)
        cnt = (jnp.minimum(c_idx + (win - half), GRID_W) - jnp.maximum(c_idx - half, 0)).astype(F32)
        m1_scr[...] = total / cnt
        for r in range(n_rows):
            lo = max(r - half, 0)
            hi = min(r + (win - half), n_rows)
            acc = m1_scr[lo * GRID_W:(lo + 1) * GRID_W, :]
            for rr in range(lo + 1, hi):
                acc = acc + m1_scr[rr * GRID_W:(rr + 1) * GRID_W, :]
            xg = xp_scr[POOL_PAD + r * GRID_W:POOL_PAD + (r + 1) * GRID_W, gc]
            dlt = acc / float(hi - lo) - xg
            ya = _dot(dlt.astype(BF16), wpool_ref[g]) * pscale_ref[...][:, gc]
            ya_scr[r * GRID_W:(r + 1) * GRID_W, gc] = ya.astype(BF16)

    pool_parts = [pool_pad] + [functools.partial(pool_grid if grid2d else pool_seq, g) for g in range(N_GROUPS)]

    def keep_mask(d, rb, col_lo, ncols):
        t_loc = rb * RB + lax.broadcasted_iota(jnp.int32, (RB, ncols), 0)
        s_loc = col_lo + lax.broadcasted_iota(jnp.int32, (RB, ncols), 1)
        return s_loc <= t_loc if d == 0 else s_loc >= t_loc

    def mlstm_prepare(d, c, m_in):
        c0 = _aligned(c * L, L)
        colf = colf_scr[pl.ds(c0, L), :]
        bcol = colf[:, N_HEADS * d:N_HEADS * (d + 1)]
        cmcol = colf[:, 2 * N_HEADS + N_HEADS * d:2 * N_HEADS + N_HEADS * (d + 1)]
        mx = jnp.maximum(m_in, cmcol)
        last = L - 1 if d == 0 else 0
        mx_last = mx[last:last + 1, :]
        return dict(c0=c0, mx=mx, mx_last=mx_last, w_inter=jnp.exp(m_in - mx), e_negm=jnp.exp(-(bcol + mx)),
                    decay=jnp.exp(m_in - mx_last), m_new=bcol[last:last + 1, :] + mx_last)

    def block_cols(d, rb):
        return (0, (rb + 1) * RB) if d == 0 else (rb * RB, L)

    def mlstm_scores(d, p, h):
        c0 = p["c0"]
        hc = slice(h * HEAD_DIM, (h + 1) * HEAD_DIM)
        qks = []
        for rb in range(L // RB):
            col_lo, col_hi = block_cols(d, rb)
            qb = q_scr[pl.ds(_aligned(c0 + rb * RB, RB), RB), hc]
            qks.append(_dot(qb, kt_scr[hc, pl.ds(_aligned(c0 + col_lo, RB), col_hi - col_lo)]))
        p["qk", h] = qks

    def mlstm_head(d, p, h):
        c0, mx, mx_last = p["c0"], p["mx"], p["mx_last"]
        rows = pl.ds(c0, L)
        hc = slice(h * HEAD_DIM, (h + 1) * HEAD_DIM)
        ac = slice(2 * h * HEAD_DIM, (2 * h + 2) * HEAD_DIM)
        grow = grow_scr[N_HEADS * d + h:N_HEADS * d + h + 1, rows]
        if has_state:
            cn_dh = cn_scr[d, h]
            cn_bf = cn_dh.astype(BF16)
        for rb in range(L // RB):
            col_lo, col_hi = block_cols(d, rb)
            ncols = col_hi - col_lo
            trow = slice(rb * RB, (rb + 1) * RB)
            rrows = pl.ds(_aligned(c0 + rb * RB, RB), RB)
            crows = pl.ds(_aligned(c0 + col_lo, RB), ncols)
            qb = q_scr[rrows, hc]
            qk = p["qk", h][rb]
            w = jnp.exp(grow[:, col_lo:col_hi] - mx[trow, h:h + 1])
            s = (qk * jnp.where(keep_mask(d, rb, col_lo, ncols), w, 0.0)).astype(BF16)
            na = _dot(s, vaug_scr[crows, ac])
            if has_state:
                na = na + p["w_inter"][trow, h:h + 1] * _dot(qb, cn_bf)
            hval = na[:, 0:HEAD_DIM] / jnp.maximum(jnp.abs(na[:, HEAD_DIM:]), p["e_negm"][trow, h:h + 1])
            if d == 0:
                h_scr[rrows, hc] = hval
            else:
                h_scr[rrows, hc] += hval
        w_state = jnp.exp(grow - mx_last[:, h:h + 1])
        kw_t = (kt_scr[hc, rows].astype(F32) * w_state).astype(BF16)
        upd = _dot(kw_t, vaug_scr[rows, ac])
        if has_state:
            cn_scr[d, h] = p["decay"][:, h:h + 1] * cn_dh + upd
        else:
            cout_ref[d, h] = upd[:, 0:HEAD_DIM]
            nout_ref[d, h:h + 1, :] = upd[:, HEAD_DIM:].T[0:1, :]

    def merge_ab(c, nj):
        nc = slice(nj * NW, (nj + 1) * NW)
        ub = u_scr[tile_rows(c), :]
        acc = None
        for r, y_scr in enumerate((ya_scr, yb_scr)):
            gate = jax.nn.sigmoid(_dot(ub, wbrg_ref[:, r * D_MODEL + nj * NW:r * D_MODEL + (nj + 1) * NW]))
            term = gate * _dot(y_scr[tile_rows(c), :], wbranch_ref[r, :, nc])
            acc = term if acc is None else acc + term
        mab_scr[tile_rows(c), nc] = acc

    def gate_c(c, nj):
        w = wbrg_ref[:, 2 * D_MODEL + nj * NW:2 * D_MODEL + (nj + 1) * NW]
        gc_scr[:, nj * NW:(nj + 1) * NW] = jax.nn.sigmoid(_dot(u_scr[tile_rows(c), :], w))

    def tail_yc(c):
        gml = gml_ref[...]
        for h in range(N_HEADS):
            hc = slice(h * HEAD_DIM, (h + 1) * HEAD_DIM)
            hh = h_scr[tile_rows(c), hc]
            hn = hh * lax.rsqrt(jnp.mean(hh * hh, axis=-1, keepdims=True) + EPS) * gml[:, hc]
            yc_scr[:, hc] = (jax.nn.sigmoid(o_scr[tile_rows(c), hc]) * hn).astype(BF16)

    def tail_merge(c, nj):
        nc = slice(nj * NW, (nj + 1) * NW)
        acc = mab_scr[tile_rows(c), nc] + gc_scr[:, nc] * _dot(yc_scr[...], wbranch_ref[2, :, nc])
        merged_scr[:, nc] = acc.astype(BF16)

    def tail_out(c):
        gt1 = mod_ref[...][:, 2 * D_MODEL:3 * D_MODEL]
        out_ref[tile_rows(c), :] = x_ref[tile_rows(c), :] + gt1 * _dot(merged_scr[...], wout_ref[...])

    def tail_parts(c):
        return ([functools.partial(tail_yc, c)] + [functools.partial(tail_merge, c, nj) for nj in range(n_merge)]
                + [functools.partial(tail_out, c)])

    def scan_parts(d, c, m_in, fillers, out):
        p = {}

        def prepare():
            p.update(mlstm_prepare(d, c, m_in))
            out[d] = p["m_new"]

        parts = [prepare]
        for h in range(N_HEADS):
            parts.append(functools.partial(mlstm_scores, d, p, h))
            if h < len(fillers):
                parts.append(fillers[h])
            parts.append(functools.partial(mlstm_head, d, p, h))
        return parts + list(fillers[N_HEADS:])

    if n_tiles == 1 and n_chunks == 1:
        zeros = jnp.zeros((1, N_HEADS), F32)
        m_out = {}
        stage_a = [functools.partial(part, 0) for part in a_parts] + pool_parts
        stage_b = scan_parts(0, 0, zeros, [functools.partial(merge_ab, 0, nj) for nj in range(n_merge)], m_out)
        stage_b += scan_parts(1, 0, zeros, [functools.partial(gate_c, 0, nj) for nj in range(n_merge)], m_out)
        stage_b += tail_parts(0)

        def write_m():
            mout_ref[0:1, :] = m_out[0]
            mout_ref[1:2, :] = m_out[1]

        return stage_a, stage_b + [write_m]

    def run_loops():
        def phase_a(i, carry):
            for part in a_parts:
                part(i)
            return carry

        lax.fori_loop(0, n_tiles, phase_a, 0)
        for part in pool_parts:
            part()
        cn_scr[...] = cn0_ref[...]
        m_init = m0_ref[...]

        def fwd_step(j, m_in):
            m_out = {}
            for part in scan_parts(0, j, m_in, [functools.partial(merge_ab, j, nj) for nj in range(n_merge)], m_out):
                part()
            return m_out[0]

        def bwd_step(j, m_in):
            c = n_chunks - 1 - j
            m_out = {}
            for part in scan_parts(1, c, m_in, [functools.partial(gate_c, c, nj) for nj in range(n_merge)], m_out):
                part()
            for part in tail_parts(c):
                part()
            return m_out[1]

        lax.fori_loop(0, n_chunks, fwd_step, m_init[0:1, :])
        lax.fori_loop(0, n_chunks, bwd_step, m_init[1:2, :])

    assert has_state, "multi-chunk sequences carry the cached state"
    return [run_loops], []


def _mixer_call(l, xall, mods4, prm, stream, states=None, state_outs=None):
    ctx = stream == "ctx"
    T = SEQ if ctx else DEC_SEQ
    assert T % TILE_M == 0 and TILE_M == MLSTM_L and MLSTM_L % MLSTM_ROW_BLOCK == 0
    n_e = CTX_SEQS_PER_STEP if ctx else 1
    n_b = (BATCH if ctx else DEC_BATCH) // n_e
    blk0 = 0 if ctx else N_TOK_CTX // T
    once = pl.Buffered(1)

    def wspec(shape):
        nd = len(shape)
        return pl.BlockSpec((None,) + shape, lambda b: (l,) + (0,) * nd, pipeline_mode=once)

    x_spec = pl.BlockSpec((n_e * T, D_MODEL), lambda b: (blk0 + b, 0), pipeline_mode=None if ctx else once)
    if ctx:
        mod_spec = pl.BlockSpec((None, None, 1, 6 * D_MODEL), lambda b: (l, 0, 0, 0))
    else:
        mod_spec = pl.BlockSpec((None, None, 1, 6 * D_MODEL), lambda b: (l, 1 + b, 0, 0))
    in_specs = [
        x_spec, mod_spec,
        wspec((1, D_MODEL)),
        wspec((D_MODEL, N_MAIN)),
        wspec((D_BRANCH, D_MODEL)),
        wspec((D_MODEL, GATE_PAD)),
        wspec((D_MODEL, N_BRANCH * D_MODEL)),
        wspec((N_BRANCH, D_BRANCH, D_MODEL)),
        wspec((D_MODEL, D_MODEL)),
        wspec((N_GROUPS, GROUP, GROUP)),
        wspec((1, D_BRANCH)),
        wspec((1, D_BRANCH)),
        wspec((N_GROUPS, SGU_CHUNK, SGU_CHUNK)),
        wspec((SGU_CHUNK, D_BRANCH)),
        wspec((N_GATE, 1)),
        wspec((1, D_BRANCH)),
    ]
    args = [xall, mods4, prm["g_norm1"], prm["w_in16"], prm["w_kt"], prm["w_gate"], prm["w_brg"],
            prm["w_branch"], prm["w_out"], prm["w_pool"], prm["pool_scale"], prm["g_sgu"], prm["w_sgu"],
            prm["b_sgu_tile"], prm["b_gates"], prm["g_mlstm"]]
    out_shape = [jax.ShapeDtypeStruct(xall.shape, xall.dtype)]
    out_specs = [x_spec]
    aliases = {0: 0}
    if ctx:
        in_specs += [pl.BlockSpec(memory_space=pl.ANY)] * 3
        aliases.update({len(args): 1, len(args) + 1: 2, len(args) + 2: 3})
        args += list(state_outs)
        out_shape += [jax.ShapeDtypeStruct(a.shape, a.dtype) for a in state_outs]
        out_specs += [
            pl.BlockSpec((n_e, None, 2, N_HEADS, HEAD_DIM, HEAD_DIM), lambda b: (b, l, 0, 0, 0, 0)),
            pl.BlockSpec((n_e, None, 2, N_HEADS, HEAD_DIM), lambda b: (b, l, 0, 0, 0)),
            pl.BlockSpec((n_e, None, 2, N_HEADS), lambda b: (b, l, 0, 0)),
        ]
    else:
        state_cn, state_m = states
        in_specs += [
            pl.BlockSpec((None, None, 2, N_HEADS, HEAD_DIM, 2 * HEAD_DIM), lambda b: (b, l, 0, 0, 0, 0)),
            pl.BlockSpec((None, None, 2, N_HEADS), lambda b: (b, l, 0, 0)),
        ]
        args += [state_cn, state_m]

    scratch = [
        pltpu.VMEM((T, D_MODEL), BF16),
        pltpu.VMEM((T + 2 * POOL_PAD, D_BRANCH), F32),
        pltpu.VMEM((TILE_M, D_BRANCH), F32),
        pltpu.VMEM((TILE_M, D_BRANCH), BF16),
        pltpu.VMEM((T, D_BRANCH), BF16),
        pltpu.VMEM((D_BRANCH, T), BF16),
        pltpu.VMEM((T, 2 * D_BRANCH), BF16),
        pltpu.VMEM((T, D_BRANCH), F32),
        pltpu.VMEM((2 * N_HEADS, T), F32),
        pltpu.VMEM((T, GATE_PAD), F32),
        pltpu.VMEM((T, D_BRANCH), F32),
        pltpu.VMEM((T, D_BRANCH), BF16),
        pltpu.VMEM((T, D_BRANCH), BF16),
        pltpu.VMEM((TILE_M, D_BRANCH), BF16),
        pltpu.VMEM((T, D_MODEL), F32),
        pltpu.VMEM((TILE_M, D_MODEL), F32),
        pltpu.VMEM((TILE_M, D_MODEL), BF16),
    ]
    if not ctx:
        scratch += [pltpu.VMEM((2, N_HEADS, HEAD_DIM, 2 * HEAD_DIM), F32)]
        scratch += [pltpu.VMEM((T, GROUP), F32)]
    scratch = [pltpu.VMEM((n_e,) + tuple(s.shape), s.dtype) for s in scratch]

    outs = pl.pallas_call(
        functools.partial(_mixer_kernel, T=T, grid2d=not ctx, has_state=not ctx, n_elems=n_e,
                          n_in=len(in_specs), n_out=len(out_specs)),
        grid=(n_b,),
        in_specs=in_specs,
        out_specs=out_specs,
        out_shape=out_shape,
        scratch_shapes=scratch,
        input_output_aliases=aliases,
        compiler_params=pltpu.CompilerParams(
            dimension_semantics=("arbitrary",), vmem_limit_bytes=VMEM_LIMIT_BYTES),
        name="mixer_ctx" if ctx else "mixer_lat",
    )(*args)
    return outs


def _ffn_kernel(x_ref, mod_ref, g2_ref, w1_hbm, w2_hbm, gfin_ref, out_ref,
                acc_scr, h_scr, w1_scr, w2_scr, st1_scr, st2_scr, sem, *, layer, final):
    n_chunks = D_FF // FFN_CHUNK
    first_tile = pl.program_id(0) == 0

    def w1_copy(c):
        slot = c % FFN_STAGES
        c0 = _aligned(c * FFN_CHUNK, FFN_CHUNK)
        return pltpu.make_async_copy(w1_hbm.at[layer, :, pl.ds(c0, FFN_CHUNK)], st1_scr.at[slot], sem.at[0, slot])

    def w2_copy(c):
        slot = c % FFN_STAGES
        c0 = _aligned(c * FFN_CHUNK, FFN_CHUNK)
        return pltpu.make_async_copy(w2_hbm.at[layer, pl.ds(c0, FFN_CHUNK), :], st2_scr.at[slot], sem.at[1, slot])

    def land_w1(c):
        w1_copy(c).wait()
        c0 = _aligned(c * FFN_CHUNK, FFN_CHUNK)
        w1_scr[:, pl.ds(c0, FFN_CHUNK)] = st1_scr[c % FFN_STAGES].astype(BF16)

    def land_w2(c):
        w2_copy(c).wait()
        c0 = _aligned(c * FFN_CHUNK, FFN_CHUNK)
        w2_scr[pl.ds(c0, FFN_CHUNK), :] = st2_scr[c % FFN_STAGES].astype(BF16)

    def normed_input():
        mod = mod_ref[...]
        sh2 = mod[:, 3 * D_MODEL:4 * D_MODEL]
        sc2 = mod[:, 4 * D_MODEL:5 * D_MODEL]
        x = x_ref[...]
        ms = jnp.mean(x * x, axis=-1, keepdims=True)
        return (x * lax.rsqrt(ms + EPS) * g2_ref[...] * (1.0 + sc2) + sh2).astype(BF16)

    def hidden(ub, c):
        c0 = _aligned(c * FFN_CHUNK, FFN_CHUNK)
        h = _dot(ub, w1_scr[:, pl.ds(c0, FFN_CHUNK)])
        h_scr[:, pl.ds(c0, FFN_CHUNK)] = jnp.square(jnp.maximum(h, 0.0)).astype(BF16)

    def finish(ffn_out):
        gt2 = mod_ref[...][:, 5 * D_MODEL:6 * D_MODEL]
        y = x_ref[...] + gt2 * ffn_out
        if final:
            y = y * lax.rsqrt(jnp.mean(y * y, axis=-1, keepdims=True) + EPS) * gfin_ref[...]
        out_ref[...] = y

    @pl.when(first_tile)
    def _():
        for c in range(FFN_STAGES):
            w1_copy(c).start()
            w2_copy(c).start()
        ub = normed_input()
        acc_scr[...] = jnp.zeros_like(acc_scr)

        def chunk(c, carry):
            land_w1(c)
            land_w2(c)

            @pl.when(c + FFN_STAGES < n_chunks)
            def _():
                w1_copy(c + FFN_STAGES).start()
                w2_copy(c + FFN_STAGES).start()

            hidden(ub, c)
            c0 = _aligned(c * FFN_CHUNK, FFN_CHUNK)
            acc_scr[...] += _dot(h_scr[:, pl.ds(c0, FFN_CHUNK)], w2_scr[pl.ds(c0, FFN_CHUNK), :])
            return carry

        lax.fori_loop(0, n_chunks, chunk, 0)
        finish(acc_scr[...])

    @pl.when(jnp.logical_not(first_tile))
    def _():
        ub = normed_input()
        for c in range(n_chunks):
            hidden(ub, c)
        finish(_dot(h_scr[...], w2_scr[...]))


def _ffn_call(l, xall, mods4, prm, g_final, final):
    ctx_tiles = N_TOK_CTX // FFN_TILE_M
    lat_tiles = DEC_SEQ // FFN_TILE_M
    once = pl.Buffered(1)

    def mod_map(t):
        return (l, jnp.where(t < ctx_tiles, 0, 1 + (t - ctx_tiles) // lat_tiles), 0, 0)

    x_spec = pl.BlockSpec((FFN_TILE_M, D_MODEL), lambda t: (t, 0))
    return pl.pallas_call(
        functools.partial(_ffn_kernel, layer=l, final=final),
        grid=(N_TOK // FFN_TILE_M,),
        in_specs=[
            x_spec,
            pl.BlockSpec((None, None, 1, 6 * D_MODEL), mod_map),
            pl.BlockSpec((None, 1, D_MODEL), lambda t: (l, 0, 0), pipeline_mode=once),
            pl.BlockSpec(memory_space=pl.ANY),
            pl.BlockSpec(memory_space=pl.ANY),
            pl.BlockSpec((1, D_MODEL), lambda t: (0, 0)),
        ],
        out_specs=x_spec,
        out_shape=jax.ShapeDtypeStruct(xall.shape, xall.dtype),
        scratch_shapes=[
            pltpu.VMEM((FFN_TILE_M, D_MODEL), F32),
            pltpu.VMEM((FFN_TILE_M, D_FF), BF16),
            pltpu.VMEM((D_MODEL, D_FF), BF16),
            pltpu.VMEM((D_FF, D_MODEL), BF16),
            pltpu.VMEM((FFN_STAGES, D_MODEL, FFN_CHUNK), F32),
            pltpu.VMEM((FFN_STAGES, FFN_CHUNK, D_MODEL), F32),
            pltpu.SemaphoreType.DMA((2, FFN_STAGES)),
        ],
        input_output_aliases={0: 0},
        compiler_params=pltpu.CompilerParams(
            dimension_semantics=("arbitrary",), vmem_limit_bytes=VMEM_LIMIT_BYTES),
        name="ffn_final" if final else "ffn",
    )(xall, mods4, prm["g_norm2"], prm["w_ff1"], prm["w_ff2"], g_final.reshape(1, D_MODEL))


def kernel(x_prompt, x_sample, state_C, state_n, state_m, c, c_ctx, w_ada, b_ada, g_norm1, g_norm2, w_in,
           b_gates, w_pool, pool_scale, g_sgu, w_sgu, b_sgu, g_mlstm, w_branch, w_out, w_ff1, w_ff2, g_final):
    w_in16 = w_in.astype(BF16)
    prm = {
        "w_in16": w_in16,
        "w_gate": jnp.pad(w_in16[:, :, N_MAIN:N_MAIN + N_GATE], ((0, 0), (0, 0), (0, GATE_PAD - N_GATE))),
        "w_brg": w_in16[:, :, N_MAIN + N_GATE:],
        "w_kt": jnp.swapaxes(w_in16[:, :, 4 * D_BRANCH:5 * D_BRANCH], 1, 2),
        "w_branch": w_branch.astype(BF16),
        "w_out": w_out.astype(BF16),
        "w_pool": w_pool.astype(BF16),
        "w_sgu": w_sgu.astype(BF16),
        "w_ff1": w_ff1,
        "w_ff2": w_ff2,
        "g_norm1": g_norm1.reshape(DEPTH, 1, D_MODEL),
        "g_norm2": g_norm2.reshape(DEPTH, 1, D_MODEL),
        "pool_scale": pool_scale.reshape(DEPTH, 1, D_BRANCH),
        "g_sgu": g_sgu.reshape(DEPTH, 1, D_BRANCH),
        "g_mlstm": g_mlstm.reshape(DEPTH, 1, D_BRANCH),
        "b_gates": b_gates.reshape(DEPTH, N_GATE, 1),
        "b_sgu_tile": jnp.repeat(jnp.swapaxes(b_sgu, 1, 2), GROUP, axis=2),
    }
    cond8 = jnp.zeros((COND_ROWS, D_MODEL), F32).at[0].set(c_ctx).at[1:1 + DEC_BATCH].set(c)
    mods = _ada_call(cond8, w_ada, b_ada)
    mods4 = mods.reshape(DEPTH, COND_ROWS, 1, 6 * D_MODEL)

    xall = jnp.concatenate([x_prompt.reshape(N_TOK_CTX, D_MODEL), x_sample.reshape(-1, D_MODEL)], axis=0)
    state_cn = jnp.concatenate(
        [state_C, jnp.broadcast_to(state_n[..., None], state_n.shape + (HEAD_DIM,))], axis=-1)
    new_c = jnp.zeros((BATCH, DEPTH, 2, N_HEADS, HEAD_DIM, HEAD_DIM), F32)
    new_n = jnp.zeros((BATCH, DEPTH, 2, N_HEADS, HEAD_DIM), F32)
    new_m = jnp.zeros((BATCH, DEPTH, 2, N_HEADS), F32)
    for l in range(DEPTH):
        xall, new_c, new_n, new_m = _mixer_call(l, xall, mods4, prm, "ctx", state_outs=(new_c, new_n, new_m))
        (xall,) = _mixer_call(l, xall, mods4, prm, "lat", states=(state_cn, state_m))
        xall = _ffn_call(l, xall, mods4, prm, g_final, final=(l == DEPTH - 1))
    y_prompt = xall[:N_TOK_CTX].reshape(BATCH, SEQ, D_MODEL)
    y_sample = xall[N_TOK_CTX:].reshape(DEC_BATCH, DEC_SEQ, D_MODEL)
    return (y_prompt, y_sample, new_c, new_n, new_m)
```

```python
import functools

import jax
import jax.numpy as jnp
from jax import lax
from jax.experimental import pallas as pl
from jax.experimental.pallas import tpu as pltpu

F32 = jnp.float32
BF16 = jnp.bfloat16

D_MODEL = 1024
DEPTH = 4
BATCH = 16
SEQ = 256
DEC_BATCH = 2
DEC_SEQ = 1024
GRID_W = 64
D_BRANCH = 512
POOL_WINDOWS = (2, 4, 8, 16)
N_GROUPS = 4
GROUP = D_BRANCH // N_GROUPS
SGU_CHUNK = 128
N_HEADS = 4
HEAD_DIM = D_BRANCH // N_HEADS
N_BRANCH = 3
D_FF = 4 * D_MODEL
EPS = 1e-6

N_MAIN = 7 * D_BRANCH
N_GATE = 4 * N_HEADS
GATE_PAD = 128
N_TOK_CTX = BATCH * SEQ
N_TOK = N_TOK_CTX + DEC_BATCH * DEC_SEQ
COND_ROWS = 8

TILE_M = 256
CTX_SEQS_PER_STEP = 2
MLSTM_L = 256
MLSTM_ROW_BLOCK = 128
MERGE_COLS = 256
POOL_PAD = 8
FFN_TILE_M = 512
FFN_CHUNK = 512
FFN_STAGES = 2
ADA_TILE_N = 1536
REPACK_ROWS = 256
REPACK_COLS = 512
VMEM_LIMIT_BYTES = 60 * 1024 * 1024


_dot = functools.partial(jnp.dot, preferred_element_type=F32)
_dot_nt = functools.partial(lax.dot_general, dimension_numbers=(((1,), (1,)), ((), ())),
                            preferred_element_type=F32)


def _aligned(offset, align):
    return offset if isinstance(offset, int) else pl.multiple_of(offset, align)


def _ada_kernel(cond_ref, w_ref, b_ref, out_ref):
    cond = cond_ref[...]
    s = (cond * jax.nn.sigmoid(cond)).astype(BF16)
    out_ref[...] = _dot(s, w_ref[...].astype(BF16)) + b_ref[...]


def _ada_call(cond8, w_ada, b_ada):
    n_out = 6 * D_MODEL
    return pl.pallas_call(
        _ada_kernel,
        grid=(DEPTH, n_out // ADA_TILE_N),
        in_specs=[
            pl.BlockSpec((COND_ROWS, D_MODEL), lambda l, j: (0, 0)),
            pl.BlockSpec((None, D_MODEL, ADA_TILE_N), lambda l, j: (l, 0, j)),
            pl.BlockSpec((None, 1, ADA_TILE_N), lambda l, j: (l, 0, j)),
        ],
        out_specs=pl.BlockSpec((None, COND_ROWS, ADA_TILE_N), lambda l, j: (l, 0, j)),
        out_shape=jax.ShapeDtypeStruct((DEPTH, COND_ROWS, n_out), F32),
        compiler_params=pltpu.CompilerParams(
            dimension_semantics=("arbitrary", "arbitrary"), vmem_limit_bytes=VMEM_LIMIT_BYTES),
        name="ada_mod",
    )(cond8, w_ada, b_ada.reshape(DEPTH, 1, n_out))


def _repack_kernel(w_ref, main_ref, brg_ref, gate_ref, kt_ref):
    main_ref[...] = w_ref[:, 0:N_MAIN].astype(BF16)
    kt_ref[...] = w_ref[:, 4 * D_BRANCH:5 * D_BRANCH].T.astype(BF16)
    gate = w_ref[:, N_MAIN:N_MAIN + GATE_PAD][:, 0:N_GATE]
    gate_ref[...] = jnp.concatenate(
        [gate, jnp.zeros((REPACK_ROWS, GATE_PAD - N_GATE), F32)], axis=1).astype(BF16)
    n_br = N_BRANCH * D_MODEL
    for c0 in range(0, n_br, REPACK_COLS):
        width = min(REPACK_COLS + GATE_PAD, n_br + N_GATE - c0)
        window = w_ref[:, N_MAIN + c0:N_MAIN + c0 + width]
        brg_ref[:, c0:c0 + REPACK_COLS] = window[:, N_GATE:N_GATE + REPACK_COLS].astype(BF16)


def _repack_call(w_in):
    d_in = w_in.shape[-1]
    blk = lambda shape, idx: pl.BlockSpec((None,) + shape, idx)
    return pl.pallas_call(
        _repack_kernel,
        grid=(DEPTH, D_MODEL // REPACK_ROWS),
        in_specs=[blk((REPACK_ROWS, d_in), lambda l, r: (l, r, 0))],
        out_specs=[
            blk((REPACK_ROWS, N_MAIN), lambda l, r: (l, r, 0)),
            blk((REPACK_ROWS, N_BRANCH * D_MODEL), lambda l, r: (l, r, 0)),
            blk((REPACK_ROWS, GATE_PAD), lambda l, r: (l, r, 0)),
            blk((D_BRANCH, REPACK_ROWS), lambda l, r: (l, 0, r)),
        ],
        out_shape=[
            jax.ShapeDtypeStruct((DEPTH, D_MODEL, N_MAIN), BF16),
            jax.ShapeDtypeStruct((DEPTH, D_MODEL, N_BRANCH * D_MODEL), BF16),
            jax.ShapeDtypeStruct((DEPTH, D_MODEL, GATE_PAD), BF16),
            jax.ShapeDtypeStruct((DEPTH, D_BRANCH, D_MODEL), BF16),
        ],
        compiler_params=pltpu.CompilerParams(
            dimension_semantics=("arbitrary", "arbitrary"), vmem_limit_bytes=VMEM_LIMIT_BYTES),
        name="repack_w_in",
    )(w_in)


def _seg_scan(x, pos, seg, total, op, reverse):
    d = 1
    while d < seg:
        if reverse:
            shifted = pltpu.roll(x, total - d, 1)
            valid = pos < seg - d
        else:
            shifted = pltpu.roll(x, d, 1)
            valid = pos >= d
        if op == "sum":
            x = x + jnp.where(valid, shifted, 0.0)
        else:
            x = jnp.maximum(x, jnp.where(valid, shifted, -jnp.inf))
        d *= 2
    return x


def _interleave(first, second):
    out, i, j = [], 0, 0
    while i < len(first) or j < len(second):
        if j >= len(second) or (i < len(first) and i * len(second) <= j * len(first)):
            out.append(first[i])
            i += 1
        else:
            out.append(second[j])
            j += 1
    return out


def _mixer_kernel(*refs, T, grid2d, has_state, n_elems, n_in, n_out):
    ins, outs, scr = refs[:n_in], refs[n_in:n_in + n_out], refs[n_in + n_out:]
    stages = []
    for e in range(n_elems):
        rows = slice(e * T, (e + 1) * T)
        ins_e = [ins[0].at[rows, :]] + list(ins[1:])
        outs_e = [outs[0].at[rows, :]] + [o.at[e] for o in outs[1:]]
        scr_e = [s.at[e] for s in scr]
        stages.append(_mixer_sequence(*ins_e, *outs_e, *scr_e, T=T, grid2d=grid2d, has_state=has_state))
    order = list(stages[0][0])
    for e in range(n_elems):
        order += _interleave(stages[e][1], stages[e + 1][0] if e + 1 < n_elems else [])
    for thunk in order:
        thunk()


def _mixer_sequence(*refs, T, grid2d, has_state):
    it = iter(refs)
    x_ref = next(it)
    mod_ref = next(it)
    g1_ref = next(it)
    wmain_ref = next(it)
    wkt_ref = next(it)
    wgate_ref = next(it)
    wbrg_ref = next(it)
    wbranch_ref = next(it)
    wout_ref = next(it)
    wpool_ref = next(it)
    pscale_ref = next(it)
    gsgu_ref = next(it)
    wsgu_ref = next(it)
    bsgu_ref = next(it)
    bgates_ref = next(it)
    gml_ref = next(it)
    if has_state:
        cn0_ref, m0_ref = next(it), next(it)
    else:
        next(it), next(it), next(it)
    out_ref = next(it)
    if not has_state:
        cout_ref, nout_ref, mout_ref = next(it), next(it), next(it)
    (u_scr, xp_scr, su_scr, vn_scr, q_scr, kt_scr, vaug_scr, o_scr, grow_scr, colf_scr,
     h_scr, ya_scr, yb_scr, yc_scr, mab_scr, gc_scr, merged_scr) = [next(it) for _ in range(17)]
    if has_state:
        cn_scr = next(it)
    if grid2d:
        m1_scr = next(it)

    L = MLSTM_L
    n_tiles = T // TILE_M
    n_chunks = T // L
    RB = MLSTM_ROW_BLOCK
    NW = MERGE_COLS
    n_merge = D_MODEL // NW

    def tile_rows(i):
        return pl.ds(_aligned(i * TILE_M, TILE_M), TILE_M)

    def a_norm(i):
        mod = mod_ref[...]
        sh1 = mod[:, 0:D_MODEL]
        sc1 = mod[:, D_MODEL:2 * D_MODEL]
        x = x_ref[tile_rows(i), :]
        ms = jnp.mean(x * x, axis=-1, keepdims=True)
        u = x * lax.rsqrt(ms + EPS) * g1_ref[...] * (1.0 + sc1) + sh1
        u_scr[tile_rows(i), :] = u.astype(BF16)

    def a_gates(i):
        lane = lax.broadcasted_iota(jnp.int32, (2 * N_HEADS, TILE_M), 1)
        is_fwd = lax.broadcasted_iota(jnp.int32, (2 * N_HEADS, TILE_M), 0) < N_HEADS
        gate_t = _dot(u_scr[tile_rows(i), :], wgate_ref[...]).T
        gp = gate_t[0:N_GATE, :] + bgates_ref[...]
        i_all = gp[0:2 * N_HEADS, :]
        f_all = gp[2 * N_HEADS:4 * N_HEADS, :]
        logf = jnp.minimum(f_all, 0.0) - jnp.log1p(jnp.exp(-jnp.abs(f_all)))
        b_row = jnp.where(is_fwd, _seg_scan(logf, lane, L, L, "sum", False),
                          _seg_scan(logf, lane, L, L, "sum", True))
        g_row = i_all - b_row
        cm_row = jnp.where(is_fwd, _seg_scan(g_row, lane, L, L, "max", False),
                           _seg_scan(g_row, lane, L, L, "max", True))
        grow_scr[:, tile_rows(i)] = g_row
        stacked = jnp.concatenate(
            [b_row, cm_row, jnp.zeros((GATE_PAD - 4 * N_HEADS, TILE_M), F32)], axis=0)
        colf_scr[tile_rows(i), :] = stacked.T

    def proj(i, k):
        return _dot(u_scr[tile_rows(i), :], wmain_ref[:, k * D_BRANCH:(k + 1) * D_BRANCH])

    def a_xp(i):
        xp_scr[pl.ds(_aligned(i * TILE_M + POOL_PAD, 8), TILE_M), :] = proj(i, 0)

    def a_su(i):
        su_scr[...] = proj(i, 1)

    def a_sv(i):
        sv = proj(i, 2)
        vn = sv * lax.rsqrt(jnp.mean(sv * sv, axis=-1, keepdims=True) + EPS) * gsgu_ref[...]
        vn_scr[...] = vn.astype(BF16)

    def a_q(i):
        q_scr[tile_rows(i), :] = proj(i, 3).astype(BF16)

    def a_k(i):
        kt = _dot_nt(wkt_ref[...], u_scr[tile_rows(i), :])
        kt_scr[:, tile_rows(i)] = (kt * (HEAD_DIM ** -0.5)).astype(BF16)

    def a_v(i):
        v = proj(i, 5).astype(BF16)
        ones_blk = jnp.ones((TILE_M, HEAD_DIM), BF16)
        for h in range(N_HEADS):
            vaug_scr[tile_rows(i), 2 * h * HEAD_DIM:(2 * h + 1) * HEAD_DIM] = v[:, h * HEAD_DIM:(h + 1) * HEAD_DIM]
            vaug_scr[tile_rows(i), (2 * h + 1) * HEAD_DIM:(2 * h + 2) * HEAD_DIM] = ones_blk

    def a_o(i):
        o_scr[tile_rows(i), :] = proj(i, 6)

    def a_sgu(ch, i):
        crow = slice(ch * SGU_CHUNK, (ch + 1) * SGU_CHUNK)
        for g in range(N_GROUPS):
            gc = slice(g * GROUP, (g + 1) * GROUP)
            mixed = _dot(wsgu_ref[g], vn_scr[crow, gc]) + bsgu_ref[:, gc]
            yb = su_scr[crow, gc] * mixed
            yb_scr[pl.ds(_aligned(i * TILE_M + ch * SGU_CHUNK, SGU_CHUNK), SGU_CHUNK), gc] = yb.astype(BF16)

    a_parts = [a_norm, a_gates, a_xp, a_su, a_sv, a_q, a_k, a_v, a_o]
    a_parts += [functools.partial(a_sgu, ch) for ch in range(TILE_M // SGU_CHUNK)]

    def pool_pad():
        zpad = jnp.zeros((POOL_PAD, D_BRANCH), F32)
        xp_scr[0:POOL_PAD, :] = zpad
        xp_scr[T + POOL_PAD:T + 2 * POOL_PAD, :] = zpad

    def pool_seq(g):
        win = POOL_WINDOWS[g]
        gc = slice(g * GROUP, (g + 1) * GROUP)
        half = win // 2
        t_idx = lax.broadcasted_iota(jnp.int32, (T, GROUP), 0)
        total = jnp.zeros((T, GROUP), F32)
        for j in range(-half, win - half):
            total = total + xp_scr[POOL_PAD + j:POOL_PAD + j + T, gc]
        cnt = (jnp.minimum(t_idx + (win - half), T) - jnp.maximum(t_idx - half, 0)).astype(F32)
        dlt = total / cnt - xp_scr[POOL_PAD:POOL_PAD + T, gc]
        ya = _dot(dlt.astype(BF16), wpool_ref[g]) * pscale_ref[...][:, gc]
        ya_scr[:, gc] = ya.astype(BF16)

    def pool_grid(g):
        win = POOL_WINDOWS[g]
        gc = slice(g * GROUP, (g + 1) * GROUP)
        half = win // 2
        n_rows = T // GRID_W
        c_idx = lax.broadcasted_iota(jnp.int32, (T, GROUP), 0) & (GRID_W - 1)
        total = jnp.zeros((T, GROUP), F32)
        for j in range(-half, win - half):
            valid = (c_idx + j >= 0) & (c_idx + j < GRID_W)
            total = total + jnp.where(valid, xp_scr[POOL_PAD + j:POOL_PAD + j + T, gc], 0.0)
        cnt = (jnp.minimum(c_idx + (win - half), GRID_W) - jnp.maximum(c_idx - half, 0)).astype(F32)
        m1_scr[...] = total / cnt
        for r in range(n_rows):
            lo = max(r - half, 0)
            hi = min(r + (win - half), n_rows)
            acc = m1_scr[lo * GRID_W:(lo + 1) * GRID_W, :]
            for rr in range(lo + 1, hi):
                acc = acc + m1_scr[rr * GRID_W:(rr + 1) * GRID_W, :]
            xg = xp_scr[POOL_PAD + r * GRID_W:POOL_PAD + (r + 1) * GRID_W, gc]
            dlt = acc / float(hi - lo) - xg
            ya = _dot(dlt.astype(BF16), wpool_ref[g]) * pscale_ref[...][:, gc]
            ya_scr[r * GRID_W:(r + 1) * GRID_W, gc] = ya.astype(BF16)

    pool_parts = [pool_pad] + [functools.partial(pool_grid if grid2d else pool_seq, g) for g in range(N_GROUPS)]

    def keep_mask(d, rb, col_lo, ncols):
        t_loc = rb * RB + lax.broadcasted_iota(jnp.int32, (RB, ncols), 0)
        s_loc = col_lo + lax.broadcasted_iota(jnp.int32, (RB, ncols), 1)
        return s_loc <= t_loc if d == 0 else s_loc >= t_loc

    def mlstm_prepare(d, c, m_in):
        c0 = _aligned(c * L, L)
        colf = colf_scr[pl.ds(c0, L), :]
        bcol = colf[:, N_HEADS * d:N_HEADS * (d + 1)]
        cmcol = colf[:, 2 * N_HEADS + N_HEADS * d:2 * N_HEADS + N_HEADS * (d + 1)]
        mx = jnp.maximum(m_in, cmcol)
        last = L - 1 if d == 0 else 0
        mx_last = mx[last:last + 1, :]
        return dict(c0=c0, mx=mx, mx_last=mx_last, w_inter=jnp.exp(m_in - mx), e_negm=jnp.exp(-(bcol + mx)),
                    decay=jnp.exp(m_in - mx_last), m_new=bcol[last:last + 1, :] + mx_last)

    def block_cols(d, rb):
        return (0, (rb + 1) * RB) if d == 0 else (rb * RB, L)

    def mlstm_scores(d, p, h):
        c0 = p["c0"]
        hc = slice(h * HEAD_DIM, (h + 1) * HEAD_DIM)
        qks = []
        for rb in range(L // RB):
            col_lo, col_hi = block_cols(d, rb)
            qb = q_scr[pl.ds(_aligned(c0 + rb * RB, RB), RB), hc]
            qks.append(_dot(qb, kt_scr[hc, pl.ds(_aligned(c0 + col_lo, RB), col_hi - col_lo)]))
        p["qk", h] = qks

    def mlstm_head(d, p, h):
        c0, mx, mx_last = p["c0"], p["mx"], p["mx_last"]
        rows = pl.ds(c0, L)
        hc = slice(h * HEAD_DIM, (h + 1) * HEAD_DIM)
        ac = slice(2 * h * HEAD_DIM, (2 * h + 2) * HEAD_DIM)
        grow = grow_scr[N_HEADS * d + h:N_HEADS * d + h + 1, rows]
        if has_state:
            cn_dh = cn_scr[d, h]
            cn_bf = cn_dh.astype(BF16)
        for rb in range(L // RB):
            col_lo, col_hi = block_cols(d, rb)
            ncols = col_hi - col_lo
            trow = slice(rb * RB, (rb + 1) * RB)
            rrows = pl.ds(_aligned(c0 + rb * RB, RB), RB)
            crows = pl.ds(_aligned(c0 + col_lo, RB), ncols)
            qb = q_scr[rrows, hc]
            qk = p["qk", h][rb]
            w = jnp.exp(grow[:, col_lo:col_hi] - mx[trow, h:h + 1])
            s = (qk * jnp.where(keep_mask(d, rb, col_lo, ncols), w, 0.0)).astype(BF16)
            na = _dot(s, vaug_scr[crows, ac])
            if has_state:
                na = na + p["w_inter"][trow, h:h + 1] * _dot(qb, cn_bf)
            hval = na[:, 0:HEAD_DIM] / jnp.maximum(jnp.abs(na[:, HEAD_DIM:]), p["e_negm"][trow, h:h + 1])
            if d == 0:
                h_scr[rrows, hc] = hval
            else:
                h_scr[rrows, hc] += hval
        w_state = jnp.exp(grow - mx_last[:, h:h + 1])
        kw_t = (kt_scr[hc, rows].astype(F32) * w_state).astype(BF16)
        upd = _dot(kw_t, vaug_scr[rows, ac])
        if has_state:
            cn_scr[d, h] = p["decay"][:, h:h + 1] * cn_dh + upd
        else:
            cout_ref[d, h] = upd[:, 0:HEAD_DIM]
            nout_ref[d, h:h + 1, :] = upd[:, HEAD_DIM:].T[0:1, :]

    def merge_ab(c, nj):
        nc = slice(nj * NW, (nj + 1) * NW)
        ub = u_scr[tile_rows(c), :]
        acc = None
        for r, y_scr in enumerate((ya_scr, yb_scr)):
            gate = jax.nn.sigmoid(_dot(ub, wbrg_ref[:, r * D_MODEL + nj * NW:r * D_MODEL + (nj + 1) * NW]))
            term = gate * _dot(y_scr[tile_rows(c), :], wbranch_ref[r, :, nc])
            acc = term if acc is None else acc + term
        mab_scr[tile_rows(c), nc] = acc

    def gate_c(c, nj):
        w = wbrg_ref[:, 2 * D_MODEL + nj * NW:2 * D_MODEL + (nj + 1) * NW]
        gc_scr[:, nj * NW:(nj + 1) * NW] = jax.nn.sigmoid(_dot(u_scr[tile_rows(c), :], w))

    def tail_yc(c):
        gml = gml_ref[...]
        for h in range(N_HEADS):
            hc = slice(h * HEAD_DIM, (h + 1) * HEAD_DIM)
            hh = h_scr[tile_rows(c), hc]
            hn = hh * lax.rsqrt(jnp.mean(hh * hh, axis=-1, keepdims=True) + EPS) * gml[:, hc]
            yc_scr[:, hc] = (jax.nn.sigmoid(o_scr[tile_rows(c), hc]) * hn).astype(BF16)

    def tail_merge(c, nj):
        nc = slice(nj * NW, (nj + 1) * NW)
        acc = mab_scr[tile_rows(c), nc] + gc_scr[:, nc] * _dot(yc_scr[...], wbranch_ref[2, :, nc])
        merged_scr[:, nc] = acc.astype(BF16)

    def tail_out(c):
        gt1 = mod_ref[...][:, 2 * D_MODEL:3 * D_MODEL]
        out_ref[tile_rows(c), :] = x_ref[tile_rows(c), :] + gt1 * _dot(merged_scr[...], wout_ref[...])

    def tail_parts(c):
        return ([functools.partial(tail_yc, c)] + [functools.partial(tail_merge, c, nj) for nj in range(n_merge)]
                + [functools.partial(tail_out, c)])

    def scan_parts(d, c, m_in, fillers, out):
        p = {}

        def prepare():
            p.update(mlstm_prepare(d, c, m_in))
            out[d] = p["m_new"]

        parts = [prepare]
        for h in range(N_HEADS):
            parts.append(functools.partial(mlstm_scores, d, p, h))
            if h < len(fillers):
                parts.append(fillers[h])
            parts.append(functools.partial(mlstm_head, d, p, h))
        return parts + list(fillers[N_HEADS:])

    if n_tiles == 1 and n_chunks == 1:
        zeros = jnp.zeros((1, N_HEADS), F32)
        m_out = {}
        stage_a = [functools.partial(part, 0) for part in a_parts] + pool_parts
        stage_b = scan_parts(0, 0, zeros, [functools.partial(merge_ab, 0, nj) for nj in range(n_merge)], m_out)
        stage_b += scan_parts(1, 0, zeros, [functools.partial(gate_c, 0, nj) for nj in range(n_merge)], m_out)
        stage_b += tail_parts(0)

        def write_m():
            mout_ref[0:1, :] = m_out[0]
            mout_ref[1:2, :] = m_out[1]

        return stage_a, stage_b + [write_m]

    def run_loops():
        def phase_a(i, carry):
            for part in a_parts:
                part(i)
            return carry

        lax.fori_loop(0, n_tiles, phase_a, 0)
        for part in pool_parts:
            part()
        cn_scr[...] = cn0_ref[...]
        m_init = m0_ref[...]

        def fwd_step(j, m_in):
            m_out = {}
            for part in scan_parts(0, j, m_in, [functools.partial(merge_ab, j, nj) for nj in range(n_merge)], m_out):
                part()
            return m_out[0]

        def bwd_step(j, m_in):
            c = n_chunks - 1 - j
            m_out = {}
            for part in scan_parts(1, c, m_in, [functools.partial(gate_c, c, nj) for nj in range(n_merge)], m_out):
                part()
            for part in tail_parts(c):
                part()
            return m_out[1]

        lax.fori_loop(0, n_chunks, fwd_step, m_init[0:1, :])
        lax.fori_loop(0, n_chunks, bwd_step, m_init[1:2, :])

    assert has_state, "multi-chunk sequences carry the cached state"
    return [run_loops], []


def _mixer_call(l, xall, mods4, prm, stream, states=None, state_outs=None):
    ctx = stream == "ctx"
    T = SEQ if ctx else DEC_SEQ
    assert T % TILE_M == 0 and TILE_M == MLSTM_L and MLSTM_L % MLSTM_ROW_BLOCK == 0
    n_e = CTX_SEQS_PER_STEP if ctx else 1
    n_b = (BATCH if ctx else DEC_BATCH) // n_e
    blk0 = 0 if ctx else N_TOK_CTX // T
    once = pl.Buffered(1)

    def wspec(shape):
        nd = len(shape)
        return pl.BlockSpec((None,) + shape, lambda b: (l,) + (0,) * nd, pipeline_mode=once)

    x_spec = pl.BlockSpec((n_e * T, D_MODEL), lambda b: (blk0 + b, 0), pipeline_mode=None if ctx else once)
    if ctx:
        mod_spec = pl.BlockSpec((None, None, 1, 6 * D_MODEL), lambda b: (l, 0, 0, 0))
    else:
        mod_spec = pl.BlockSpec((None, None, 1, 6 * D_MODEL), lambda b: (l, 1 + b, 0, 0))
    in_specs = [
        x_spec, mod_spec,
        wspec((1, D_MODEL)),
        wspec((D_MODEL, N_MAIN)),
        wspec((D_BRANCH, D_MODEL)),
        wspec((D_MODEL, GATE_PAD)),
        wspec((D_MODEL, N_BRANCH * D_MODEL)),
        wspec((N_BRANCH, D_BRANCH, D_MODEL)),
        wspec((D_MODEL, D_MODEL)),
        wspec((N_GROUPS, GROUP, GROUP)),
        wspec((1, D_BRANCH)),
        wspec((1, D_BRANCH)),
        wspec((N_GROUPS, SGU_CHUNK, SGU_CHUNK)),
        wspec((SGU_CHUNK, D_BRANCH)),
        wspec((N_GATE, 1)),
        wspec((1, D_BRANCH)),
    ]
    args = [xall, mods4, prm["g_norm1"], prm["w_main"], prm["w_kt"], prm["w_gate"], prm["w_brg"],
            prm["w_branch"], prm["w_out"], prm["w_pool"], prm["pool_scale"], prm["g_sgu"], prm["w_sgu"],
            prm["b_sgu_tile"], prm["b_gates"], prm["g_mlstm"]]
    out_shape = [jax.ShapeDtypeStruct(xall.shape, xall.dtype)]
    out_specs = [x_spec]
    aliases = {0: 0}
    if ctx:
        in_specs += [pl.BlockSpec(memory_space=pl.ANY)] * 3
        aliases.update({len(args): 1, len(args) + 1: 2, len(args) + 2: 3})
        args += list(state_outs)
        out_shape += [jax.ShapeDtypeStruct(a.shape, a.dtype) for a in state_outs]
        out_specs += [
            pl.BlockSpec((n_e, None, 2, N_HEADS, HEAD_DIM, HEAD_DIM), lambda b: (b, l, 0, 0, 0, 0)),
            pl.BlockSpec((n_e, None, 2, N_HEADS, HEAD_DIM), lambda b: (b, l, 0, 0, 0)),
            pl.BlockSpec((n_e, None, 2, N_HEADS), lambda b: (b, l, 0, 0)),
        ]
    else:
        state_cn, state_m = states
        in_specs += [
            pl.BlockSpec((None, None, 2, N_HEADS, HEAD_DIM, 2 * HEAD_DIM), lambda b: (b, l, 0, 0, 0, 0)),
            pl.BlockSpec((None, None, 2, N_HEADS), lambda b: (b, l, 0, 0)),
        ]
        args += [state_cn, state_m]

    scratch = [
        pltpu.VMEM((T, D_MODEL), BF16),
        pltpu.VMEM((T + 2 * POOL_PAD, D_BRANCH), F32),
        pltpu.VMEM((TILE_M, D_BRANCH), F32),
        pltpu.VMEM((TILE_M, D_BRANCH), BF16),
        pltpu.VMEM((T, D_BRANCH), BF16),
        pltpu.VMEM((D_BRANCH, T), BF16),
        pltpu.VMEM((T, 2 * D_BRANCH), BF16),
        pltpu.VMEM((T, D_BRANCH), F32),
        pltpu.VMEM((2 * N_HEADS, T), F32),
        pltpu.VMEM((T, GATE_PAD), F32),
        pltpu.VMEM((T, D_BRANCH), F32),
        pltpu.VMEM((T, D_BRANCH), BF16),
        pltpu.VMEM((T, D_BRANCH), BF16),
        pltpu.VMEM((TILE_M, D_BRANCH), BF16),
        pltpu.VMEM((T, D_MODEL), F32),
        pltpu.VMEM((TILE_M, D_MODEL), F32),
        pltpu.VMEM((TILE_M, D_MODEL), BF16),
    ]
    if not ctx:
        scratch += [pltpu.VMEM((2, N_HEADS, HEAD_DIM, 2 * HEAD_DIM), F32)]
        scratch += [pltpu.VMEM((T, GROUP), F32)]
    scratch = [pltpu.VMEM((n_e,) + tuple(s.shape), s.dtype) for s in scratch]

    outs = pl.pallas_call(
        functools.partial(_mixer_kernel, T=T, grid2d=not ctx, has_state=not ctx, n_elems=n_e,
                          n_in=len(in_specs), n_out=len(out_specs)),
        grid=(n_b,),
        in_specs=in_specs,
        out_specs=out_specs,
        out_shape=out_shape,
        scratch_shapes=scratch,
        input_output_aliases=aliases,
        compiler_params=pltpu.CompilerParams(
            dimension_semantics=("arbitrary",), vmem_limit_bytes=VMEM_LIMIT_BYTES),
        name="mixer_ctx" if ctx else "mixer_lat",
    )(*args)
    return outs


def _ffn_kernel(x_ref, mod_ref, g2_ref, w1_hbm, w2_hbm, gfin_ref, out_ref,
                acc_scr, h_scr, w1_scr, w2_scr, st1_scr, st2_scr, sem, *, layer, final):
    n_chunks = D_FF // FFN_CHUNK
    first_tile = pl.program_id(0) == 0

    def w1_copy(c):
        slot = c % FFN_STAGES
        c0 = _aligned(c * FFN_CHUNK, FFN_CHUNK)
        return pltpu.make_async_copy(w1_hbm.at[layer, :, pl.ds(c0, FFN_CHUNK)], st1_scr.at[slot], sem.at[0, slot])

    def w2_copy(c):
        slot = c % FFN_STAGES
        c0 = _aligned(c * FFN_CHUNK, FFN_CHUNK)
        return pltpu.make_async_copy(w2_hbm.at[layer, pl.ds(c0, FFN_CHUNK), :], st2_scr.at[slot], sem.at[1, slot])

    def land_w1(c):
        w1_copy(c).wait()
        c0 = _aligned(c * FFN_CHUNK, FFN_CHUNK)
        w1_scr[:, pl.ds(c0, FFN_CHUNK)] = st1_scr[c % FFN_STAGES].astype(BF16)

    def land_w2(c):
        w2_copy(c).wait()
        c0 = _aligned(c * FFN_CHUNK, FFN_CHUNK)
        w2_scr[pl.ds(c0, FFN_CHUNK), :] = st2_scr[c % FFN_STAGES].astype(BF16)

    def normed_input():
        mod = mod_ref[...]
        sh2 = mod[:, 3 * D_MODEL:4 * D_MODEL]
        sc2 = mod[:, 4 * D_MODEL:5 * D_MODEL]
        x = x_ref[...]
        ms = jnp.mean(x * x, axis=-1, keepdims=True)
        return (x * lax.rsqrt(ms + EPS) * g2_ref[...] * (1.0 + sc2) + sh2).astype(BF16)

    def hidden(ub, c):
        c0 = _aligned(c * FFN_CHUNK, FFN_CHUNK)
        h = _dot(ub, w1_scr[:, pl.ds(c0, FFN_CHUNK)])
        h_scr[:, pl.ds(c0, FFN_CHUNK)] = jnp.square(jnp.maximum(h, 0.0)).astype(BF16)

    def finish(ffn_out):
        gt2 = mod_ref[...][:, 5 * D_MODEL:6 * D_MODEL]
        y = x_ref[...] + gt2 * ffn_out
        if final:
            y = y * lax.rsqrt(jnp.mean(y * y, axis=-1, keepdims=True) + EPS) * gfin_ref[...]
        out_ref[...] = y

    @pl.when(first_tile)
    def _():
        for c in range(FFN_STAGES):
            w1_copy(c).start()
            w2_copy(c).start()
        ub = normed_input()
        acc_scr[...] = jnp.zeros_like(acc_scr)

        def chunk(c, carry):
            land_w1(c)
            land_w2(c)

            @pl.when(c + FFN_STAGES < n_chunks)
            def _():
                w1_copy(c + FFN_STAGES).start()
                w2_copy(c + FFN_STAGES).start()

            hidden(ub, c)
            c0 = _aligned(c * FFN_CHUNK, FFN_CHUNK)
            acc_scr[...] += _dot(h_scr[:, pl.ds(c0, FFN_CHUNK)], w2_scr[pl.ds(c0, FFN_CHUNK), :])
            return carry

        lax.fori_loop(0, n_chunks, chunk, 0)
        finish(acc_scr[...])

    @pl.when(jnp.logical_not(first_tile))
    def _():
        ub = normed_input()
        for c in range(n_chunks):
            hidden(ub, c)
        finish(_dot(h_scr[...], w2_scr[...]))


def _ffn_call(l, xall, mods4, prm, g_final, final):
    ctx_tiles = N_TOK_CTX // FFN_TILE_M
    lat_tiles = DEC_SEQ // FFN_TILE_M
    once = pl.Buffered(1)

    def mod_map(t):
        return (l, jnp.where(t < ctx_tiles, 0, 1 + (t - ctx_tiles) // lat_tiles), 0, 0)

    x_spec = pl.BlockSpec((FFN_TILE_M, D_MODEL), lambda t: (t, 0))
    return pl.pallas_call(
        functools.partial(_ffn_kernel, layer=l, final=final),
        grid=(N_TOK // FFN_TILE_M,),
        in_specs=[
            x_spec,
            pl.BlockSpec((None, None, 1, 6 * D_MODEL), mod_map),
            pl.BlockSpec((None, 1, D_MODEL), lambda t: (l, 0, 0), pipeline_mode=once),
            pl.BlockSpec(memory_space=pl.ANY),
            pl.BlockSpec(memory_space=pl.ANY),
            pl.BlockSpec((1, D_MODEL), lambda t: (0, 0)),
        ],
        out_specs=x_spec,
        out_shape=jax.ShapeDtypeStruct(xall.shape, xall.dtype),
        scratch_shapes=[
            pltpu.VMEM((FFN_TILE_M, D_MODEL), F32),
            pltpu.VMEM((FFN_TILE_M, D_FF), BF16),
            pltpu.VMEM((D_MODEL, D_FF), BF16),
            pltpu.VMEM((D_FF, D_MODEL), BF16),
            pltpu.VMEM((FFN_STAGES, D_MODEL, FFN_CHUNK), F32),
            pltpu.VMEM((FFN_STAGES, FFN_CHUNK, D_MODEL), F32),
            pltpu.SemaphoreType.DMA((2, FFN_STAGES)),
        ],
        input_output_aliases={0: 0},
        compiler_params=pltpu.CompilerParams(
            dimension_semantics=("arbitrary",), vmem_limit_bytes=VMEM_LIMIT_BYTES),
        name="ffn_final" if final else "ffn",
    )(xall, mods4, prm["g_norm2"], prm["w_ff1"], prm["w_ff2"], g_final.reshape(1, D_MODEL))


def kernel(x_prompt, x_sample, state_C, state_n, state_m, c, c_ctx, w_ada, b_ada, g_norm1, g_norm2, w_in,
           b_gates, w_pool, pool_scale, g_sgu, w_sgu, b_sgu, g_mlstm, w_branch, w_out, w_ff1, w_ff2, g_final):
    w_main, w_brg, w_gate, w_kt = _repack_call(w_in)
    prm = {
        "w_main": w_main,
        "w_gate": w_gate,
        "w_brg": w_brg,
        "w_kt": w_kt,
        "w_branch": w_branch.astype(BF16),
        "w_out": w_out.astype(BF16),
        "w_pool": w_pool.astype(BF16),
        "w_sgu": w_sgu.astype(BF16),
        "w_ff1": w_ff1,
        "w_ff2": w_ff2,
        "g_norm1": g_norm1.reshape(DEPTH, 1, D_MODEL),
        "g_norm2": g_norm2.reshape(DEPTH, 1, D_MODEL),
        "pool_scale": pool_scale.reshape(DEPTH, 1, D_BRANCH),
        "g_sgu": g_sgu.reshape(DEPTH, 1, D_BRANCH),
        "g_mlstm": g_mlstm.reshape(DEPTH, 1, D_BRANCH),
        "b_gates": b_gates.reshape(DEPTH, N_GATE, 1),
        "b_sgu_tile": jnp.repeat(jnp.swapaxes(b_sgu, 1, 2), GROUP, axis=2),
    }
    cond8 = jnp.zeros((COND_ROWS, D_MODEL), F32).at[0].set(c_ctx).at[1:1 + DEC_BATCH].set(c)
    mods = _ada_call(cond8, w_ada, b_ada)
    mods4 = mods.reshape(DEPTH, COND_ROWS, 1, 6 * D_MODEL)

    xall = jnp.concatenate([x_prompt.reshape(N_TOK_CTX, D_MODEL), x_sample.reshape(-1, D_MODEL)], axis=0)
    state_cn = jnp.concatenate(
        [state_C, jnp.broadcast_to(state_n[..., None], state_n.shape + (HEAD_DIM,))], axis=-1)
    new_c = jnp.zeros((BATCH, DEPTH, 2, N_HEADS, HEAD_DIM, HEAD_DIM), F32)
    new_n = jnp.zeros((BATCH, DEPTH, 2, N_HEADS, HEAD_DIM), F32)
    new_m = jnp.zeros((BATCH, DEPTH, 2, N_HEADS), F32)
    for l in range(DEPTH):
        xall, new_c, new_n, new_m = _mixer_call(l, xall, mods4, prm, "ctx", state_outs=(new_c, new_n, new_m))
        (xall,) = _mixer_call(l, xall, mods4, prm, "lat", states=(state_cn, state_m))
        xall = _ffn_call(l, xall, mods4, prm, g_final, final=(l == DEPTH - 1))
    y_prompt = xall[:N_TOK_CTX].reshape(BATCH, SEQ, D_MODEL)
    y_sample = xall[N_TOK_CTX:].reshape(DEC_BATCH, DEC_SEQ, D_MODEL)
    return (y_prompt, y_sample, new_c, new_n, new_m)
```

```python
import functools

import jax
import jax.numpy as jnp
from jax import lax
from jax.experimental import pallas as pl
from jax.experimental.pallas import tpu as pltpu

F32 = jnp.float32
BF16 = jnp.bfloat16

D_MODEL = 1024
DEPTH = 4
BATCH = 16
SEQ = 256
DEC_BATCH = 2
DEC_SEQ = 1024
GRID_W = 64
D_BRANCH = 512
POOL_WINDOWS = (2, 4, 8, 16)
N_GROUPS = 4
GROUP = D_BRANCH // N_GROUPS
SGU_CHUNK = 128
N_HEADS = 4
HEAD_DIM = D_BRANCH // N_HEADS
N_BRANCH = 3
D_FF = 4 * D_MODEL
EPS = 1e-6

N_MAIN = 7 * D_BRANCH
N_GATE = 4 * N_HEADS
GATE_PAD = 128
N_TOK_CTX = BATCH * SEQ
N_TOK = N_TOK_CTX + DEC_BATCH * DEC_SEQ
COND_ROWS = 8

TILE_M = 256
CTX_SEQS_PER_STEP = 2
MLSTM_L = 256
MLSTM_ROW_BLOCK = 128
MERGE_COLS = 256
POOL_PAD = 8
FFN_TILE_M = 512
FFN_CHUNK = 512
FFN_STAGES = 2
ADA_TILE_N = 1536
VMEM_LIMIT_BYTES = 60 * 1024 * 1024


_dot = functools.partial(jnp.dot, preferred_element_type=F32)
_dot_nt = functools.partial(lax.dot_general, dimension_numbers=(((1,), (1,)), ((), ())),
                            preferred_element_type=F32)


def _aligned(offset, align):
    return offset if isinstance(offset, int) else pl.multiple_of(offset, align)


def _ada_kernel(cond_ref, w_ref, b_ref, out_ref):
    cond = cond_ref[...]
    s = (cond * jax.nn.sigmoid(cond)).astype(BF16)
    out_ref[...] = _dot(s, w_ref[...].astype(BF16)) + b_ref[...]


def _ada_call(cond8, w_ada, b_ada):
    n_out = 6 * D_MODEL
    return pl.pallas_call(
        _ada_kernel,
        grid=(DEPTH, n_out // ADA_TILE_N),
        in_specs=[
            pl.BlockSpec((COND_ROWS, D_MODEL), lambda l, j: (0, 0)),
            pl.BlockSpec((None, D_MODEL, ADA_TILE_N), lambda l, j: (l, 0, j)),
            pl.BlockSpec((None, 1, ADA_TILE_N), lambda l, j: (l, 0, j)),
        ],
        out_specs=pl.BlockSpec((None, COND_ROWS, ADA_TILE_N), lambda l, j: (l, 0, j)),
        out_shape=jax.ShapeDtypeStruct((DEPTH, COND_ROWS, n_out), F32),
        compiler_params=pltpu.CompilerParams(
            dimension_semantics=("arbitrary", "arbitrary"), vmem_limit_bytes=VMEM_LIMIT_BYTES),
        name="ada_mod",
    )(cond8, w_ada, b_ada.reshape(DEPTH, 1, n_out))


def _seg_scan(x, pos, seg, total, op, reverse):
    d = 1
    while d < seg:
        if reverse:
            shifted = pltpu.roll(x, total - d, 1)
            valid = pos < seg - d
        else:
            shifted = pltpu.roll(x, d, 1)
            valid = pos >= d
        if op == "sum":
            x = x + jnp.where(valid, shifted, 0.0)
        else:
            x = jnp.maximum(x, jnp.where(valid, shifted, -jnp.inf))
        d *= 2
    return x


def _interleave(first, second):
    out, i, j = [], 0, 0
    while i < len(first) or j < len(second):
        if j >= len(second) or (i < len(first) and i * len(second) <= j * len(first)):
            out.append(first[i])
            i += 1
        else:
            out.append(second[j])
            j += 1
    return out


def _mixer_kernel(*refs, T, grid2d, has_state, n_elems, n_in, n_pass, n_out):
    ins, outs, scr = refs[:n_in - n_pass], refs[n_in:n_in + n_out], refs[n_in + n_out:]
    stages = []
    for e in range(n_elems):
        rows = slice(e * T, (e + 1) * T)
        ins_e = [ins[0].at[rows, :]] + list(ins[1:])
        outs_e = [outs[0].at[rows, :]] + [o.at[e] for o in outs[1:]]
        scr_e = [s.at[e] for s in scr]
        stages.append(_mixer_sequence(*ins_e, *outs_e, *scr_e, T=T, grid2d=grid2d, has_state=has_state))
    order = list(stages[0][0])
    for e in range(n_elems):
        order += _interleave(stages[e][1], stages[e + 1][0] if e + 1 < n_elems else [])
    for thunk in order:
        thunk()


def _mixer_sequence(*refs, T, grid2d, has_state):
    it = iter(refs)
    x_ref = next(it)
    mod_ref = next(it)
    g1_ref = next(it)
    wmain_ref = next(it)
    wkt_ref = next(it)
    wgate_ref = next(it)
    wbrg_ref = next(it)
    wbranch_ref = next(it)
    wout_ref = next(it)
    wpool_ref = next(it)
    pscale_ref = next(it)
    gsgu_ref = next(it)
    wsgu_ref = next(it)
    bsgu_ref = next(it)
    bgates_ref = next(it)
    gml_ref = next(it)
    if has_state:
        cn0_ref, m0_ref = next(it), next(it)
    out_ref = next(it)
    if not has_state:
        cout_ref, nout_ref, mout_ref = next(it), next(it), next(it)
    (u_scr, xp_scr, su_scr, vn_scr, q_scr, kt_scr, vaug_scr, o_scr, grow_scr, colf_scr,
     h_scr, ya_scr, yb_scr, yc_scr, mab_scr, gc_scr, merged_scr) = [next(it) for _ in range(17)]
    if has_state:
        cn_scr = next(it)
    if grid2d:
        m1_scr = next(it)

    L = MLSTM_L
    n_tiles = T // TILE_M
    n_chunks = T // L
    RB = MLSTM_ROW_BLOCK
    NW = MERGE_COLS
    n_merge = D_MODEL // NW

    def tile_rows(i):
        return pl.ds(_aligned(i * TILE_M, TILE_M), TILE_M)

    def a_norm(i):
        mod = mod_ref[...]
        sh1 = mod[:, 0:D_MODEL]
        sc1 = mod[:, D_MODEL:2 * D_MODEL]
        x = x_ref[tile_rows(i), :]
        ms = jnp.mean(x * x, axis=-1, keepdims=True)
        u = x * lax.rsqrt(ms + EPS) * g1_ref[...] * (1.0 + sc1) + sh1
        u_scr[tile_rows(i), :] = u.astype(BF16)

    def a_gates(i):
        lane = lax.broadcasted_iota(jnp.int32, (2 * N_HEADS, TILE_M), 1)
        is_fwd = lax.broadcasted_iota(jnp.int32, (2 * N_HEADS, TILE_M), 0) < N_HEADS
        gate_t = _dot(u_scr[tile_rows(i), :], wgate_ref[...]).T
        gp = gate_t[0:N_GATE, :] + bgates_ref[...]
        i_all = gp[0:2 * N_HEADS, :]
        f_all = gp[2 * N_HEADS:4 * N_HEADS, :]
        logf = jnp.minimum(f_all, 0.0) - jnp.log1p(jnp.exp(-jnp.abs(f_all)))
        b_row = jnp.where(is_fwd, _seg_scan(logf, lane, L, L, "sum", False),
                          _seg_scan(logf, lane, L, L, "sum", True))
        g_row = i_all - b_row
        cm_row = jnp.where(is_fwd, _seg_scan(g_row, lane, L, L, "max", False),
                           _seg_scan(g_row, lane, L, L, "max", True))
        grow_scr[:, tile_rows(i)] = g_row
        stacked = jnp.concatenate(
            [b_row, cm_row, jnp.zeros((GATE_PAD - 4 * N_HEADS, TILE_M), F32)], axis=0)
        colf_scr[tile_rows(i), :] = stacked.T

    def proj(i, k):
        return _dot(u_scr[tile_rows(i), :], wmain_ref[:, k * D_BRANCH:(k + 1) * D_BRANCH])

    def a_xp(i):
        xp_scr[pl.ds(_aligned(i * TILE_M + POOL_PAD, 8), TILE_M), :] = proj(i, 0)

    def a_su(i):
        su_scr[...] = proj(i, 1)

    def a_sv(i):
        sv = proj(i, 2)
        vn = sv * lax.rsqrt(jnp.mean(sv * sv, axis=-1, keepdims=True) + EPS) * gsgu_ref[...]
        vn_scr[...] = vn.astype(BF16)

    def a_q(i):
        q_scr[tile_rows(i), :] = proj(i, 3).astype(BF16)

    def a_k(i):
        kt = _dot_nt(wkt_ref[...], u_scr[tile_rows(i), :])
        kt_scr[:, tile_rows(i)] = (kt * (HEAD_DIM ** -0.5)).astype(BF16)

    def a_v(i):
        v = proj(i, 5).astype(BF16)
        ones_blk = jnp.ones((TILE_M, HEAD_DIM), BF16)
        for h in range(N_HEADS):
            vaug_scr[tile_rows(i), 2 * h * HEAD_DIM:(2 * h + 1) * HEAD_DIM] = v[:, h * HEAD_DIM:(h + 1) * HEAD_DIM]
            vaug_scr[tile_rows(i), (2 * h + 1) * HEAD_DIM:(2 * h + 2) * HEAD_DIM] = ones_blk

    def a_o(i):
        o_scr[tile_rows(i), :] = proj(i, 6)

    def a_sgu(ch, i):
        crow = slice(ch * SGU_CHUNK, (ch + 1) * SGU_CHUNK)
        for g in range(N_GROUPS):
            gc = slice(g * GROUP, (g + 1) * GROUP)
            mixed = _dot(wsgu_ref[g], vn_scr[crow, gc]) + bsgu_ref[:, gc]
            yb = su_scr[crow, gc] * mixed
            yb_scr[pl.ds(_aligned(i * TILE_M + ch * SGU_CHUNK, SGU_CHUNK), SGU_CHUNK), gc] = yb.astype(BF16)

    a_parts = [a_norm, a_gates, a_xp, a_su, a_sv, a_q, a_k, a_v, a_o]
    a_parts += [functools.partial(a_sgu, ch) for ch in range(TILE_M // SGU_CHUNK)]

    def pool_pad():
        zpad = jnp.zeros((POOL_PAD, D_BRANCH), F32)
        xp_scr[0:POOL_PAD, :] = zpad
        xp_scr[T + POOL_PAD:T + 2 * POOL_PAD, :] = zpad

    def pool_seq(g):
        win = POOL_WINDOWS[g]
        gc = slice(g * GROUP, (g + 1) * GROUP)
        half = win // 2
        t_idx = lax.broadcasted_iota(jnp.int32, (T, GROUP), 0)
        total = jnp.zeros((T, GROUP), F32)
        for j in range(-half, win - half):
            total = total + xp_scr[POOL_PAD + j:POOL_PAD + j + T, gc]
        cnt = (jnp.minimum(t_idx + (win - half), T) - jnp.maximum(t_idx - half, 0)).astype(F32)
        dlt = total / cnt - xp_scr[POOL_PAD:POOL_PAD + T, gc]
        ya = _dot(dlt.astype(BF16), wpool_ref[g]) * pscale_ref[...][:, gc]
        ya_scr[:, gc] = ya.astype(BF16)

    def pool_grid(g):
        win = POOL_WINDOWS[g]
        gc = slice(g * GROUP, (g + 1) * GROUP)
        half = win // 2
        n_rows = T // GRID_W
        c_idx = lax.broadcasted_iota(jnp.int32, (T, GROUP), 0) & (GRID_W - 1)
        total = jnp.zeros((T, GROUP), F32)
        for j in range(-half, win - half):
            valid = (c_idx + j >= 0) & (c_idx + j < GRID_W)
            total = total + jnp.where(valid, xp_scr[POOL_PAD + j:POOL_PAD + j + T, gc], 0.0)
        cnt = (jnp.minimum(c_idx + (win - half), GRID_W) - jnp.maximum(c_idx - half, 0)).astype(F32)
        m1_scr[...] = total / cnt
        for r in range(n_rows):
            lo = max(r - half, 0)
            hi = min(r + (win - half), n_rows)
            acc = m1_scr[lo * GRID_W:(lo + 1) * GRID_W, :]
            for rr in range(lo + 1, hi):
                acc = acc + m1_scr[rr * GRID_W:(rr + 1) * GRID_W, :]
            xg = xp_scr[POOL_PAD + r * GRID_W:POOL_PAD + (r + 1) * GRID_W, gc]
            dlt = acc / float(hi - lo) - xg
            ya = _dot(dlt.astype(BF16), wpool_ref[g]) * pscale_ref[...][:, gc]
            ya_scr[r * GRID_W:(r + 1) * GRID_W, gc] = ya.astype(BF16)

    pool_parts = [pool_pad] + [functools.partial(pool_grid if grid2d else pool_seq, g) for g in range(N_GROUPS)]

    def keep_mask(d, rb, col_lo, ncols):
        t_loc = rb * RB + lax.broadcasted_iota(jnp.int32, (RB, ncols), 0)
        s_loc = col_lo + lax.broadcasted_iota(jnp.int32, (RB, ncols), 1)
        return s_loc <= t_loc if d == 0 else s_loc >= t_loc

    def mlstm_prepare(d, c, m_in):
        c0 = _aligned(c * L, L)
        colf = colf_scr[pl.ds(c0, L), :]
        bcol = colf[:, N_HEADS * d:N_HEADS * (d + 1)]
        cmcol = colf[:, 2 * N_HEADS + N_HEADS * d:2 * N_HEADS + N_HEADS * (d + 1)]
        mx = jnp.maximum(m_in, cmcol)
        last = L - 1 if d == 0 else 0
        mx_last = mx[last:last + 1, :]
        return dict(c0=c0, mx=mx, mx_last=mx_last, w_inter=jnp.exp(m_in - mx), e_negm=jnp.exp(-(bcol + mx)),
                    decay=jnp.exp(m_in - mx_last), m_new=bcol[last:last + 1, :] + mx_last)

    def block_cols(d, rb):
        return (0, (rb + 1) * RB) if d == 0 else (rb * RB, L)

    def mlstm_scores(d, p, h):
        c0 = p["c0"]
        hc = slice(h * HEAD_DIM, (h + 1) * HEAD_DIM)
        qks = []
        for rb in range(L // RB):
            col_lo, col_hi = block_cols(d, rb)
            qb = q_scr[pl.ds(_aligned(c0 + rb * RB, RB), RB), hc]
            qks.append(_dot(qb, kt_scr[hc, pl.ds(_aligned(c0 + col_lo, RB), col_hi - col_lo)]))
        p["qk", h] = qks

    def mlstm_head(d, p, h):
        c0, mx, mx_last = p["c0"], p["mx"], p["mx_last"]
        rows = pl.ds(c0, L)
        hc = slice(h * HEAD_DIM, (h + 1) * HEAD_DIM)
        ac = slice(2 * h * HEAD_DIM, (2 * h + 2) * HEAD_DIM)
        grow = grow_scr[N_HEADS * d + h:N_HEADS * d + h + 1, rows]
        if has_state:
            cn_dh = cn_scr[d, h]
            cn_bf = cn_dh.astype(BF16)
        for rb in range(L // RB):
            col_lo, col_hi = block_cols(d, rb)
            ncols = col_hi - col_lo
            trow = slice(rb * RB, (rb + 1) * RB)
            rrows = pl.ds(_aligned(c0 + rb * RB, RB), RB)
            crows = pl.ds(_aligned(c0 + col_lo, RB), ncols)
            qb = q_scr[rrows, hc]
            qk = p["qk", h][rb]
            w = jnp.exp(grow[:, col_lo:col_hi] - mx[trow, h:h + 1])
            s = (qk * jnp.where(keep_mask(d, rb, col_lo, ncols), w, 0.0)).astype(BF16)
            na = _dot(s, vaug_scr[crows, ac])
            if has_state:
                na = na + p["w_inter"][trow, h:h + 1] * _dot(qb, cn_bf)
            hval = na[:, 0:HEAD_DIM] / jnp.maximum(jnp.abs(na[:, HEAD_DIM:]), p["e_negm"][trow, h:h + 1])
            if d == 0:
                h_scr[rrows, hc] = hval
            else:
                h_scr[rrows, hc] += hval
        w_state = jnp.exp(grow - mx_last[:, h:h + 1])
        kw_t = (kt_scr[hc, rows].astype(F32) * w_state).astype(BF16)
        upd = _dot(kw_t, vaug_scr[rows, ac])
        if has_state:
            cn_scr[d, h] = p["decay"][:, h:h + 1] * cn_dh + upd
        else:
            cout_ref[d, h] = upd[:, 0:HEAD_DIM]
            nout_ref[d, h:h + 1, :] = upd[:, HEAD_DIM:].T[0:1, :]

    def merge_ab(c, nj):
        nc = slice(nj * NW, (nj + 1) * NW)
        ub = u_scr[tile_rows(c), :]
        acc = None
        for r, y_scr in enumerate((ya_scr, yb_scr)):
            gate = jax.nn.sigmoid(_dot(ub, wbrg_ref[:, r * D_MODEL + nj * NW:r * D_MODEL + (nj + 1) * NW]))
            term = gate * _dot(y_scr[tile_rows(c), :], wbranch_ref[r, :, nc])
            acc = term if acc is None else acc + term
        mab_scr[tile_rows(c), nc] = acc

    def gate_c(c, nj):
        w = wbrg_ref[:, 2 * D_MODEL + nj * NW:2 * D_MODEL + (nj + 1) * NW]
        gc_scr[:, nj * NW:(nj + 1) * NW] = jax.nn.sigmoid(_dot(u_scr[tile_rows(c), :], w))

    def tail_yc(c):
        gml = gml_ref[...]
        for h in range(N_HEADS):
            hc = slice(h * HEAD_DIM, (h + 1) * HEAD_DIM)
            hh = h_scr[tile_rows(c), hc]
            hn = hh * lax.rsqrt(jnp.mean(hh * hh, axis=-1, keepdims=True) + EPS) * gml[:, hc]
            yc_scr[:, hc] = (jax.nn.sigmoid(o_scr[tile_rows(c), hc]) * hn).astype(BF16)

    def tail_merge(c, nj):
        nc = slice(nj * NW, (nj + 1) * NW)
        acc = mab_scr[tile_rows(c), nc] + gc_scr[:, nc] * _dot(yc_scr[...], wbranch_ref[2, :, nc])
        merged_scr[:, nc] = acc.astype(BF16)

    def tail_out(c):
        gt1 = mod_ref[...][:, 2 * D_MODEL:3 * D_MODEL]
        out_ref[tile_rows(c), :] = x_ref[tile_rows(c), :] + gt1 * _dot(merged_scr[...], wout_ref[...])

    def tail_parts(c):
        return ([functools.partial(tail_yc, c)] + [functools.partial(tail_merge, c, nj) for nj in range(n_merge)]
                + [functools.partial(tail_out, c)])

    def scan_parts(d, c, m_in, fillers, out):
        p = {}

        def prepare():
            p.update(mlstm_prepare(d, c, m_in))
            out[d] = p["m_new"]

        parts = [prepare]
        for h in range(N_HEADS):
            parts.append(functools.partial(mlstm_scores, d, p, h))
            if h < len(fillers):
                parts.append(fillers[h])
            parts.append(functools.partial(mlstm_head, d, p, h))
        return parts + list(fillers[N_HEADS:])

    if n_tiles == 1 and n_chunks == 1:
        zeros = jnp.zeros((1, N_HEADS), F32)
        m_out = {}
        stage_a = [functools.partial(part, 0) for part in a_parts] + pool_parts
        stage_b = scan_parts(0, 0, zeros, [functools.partial(merge_ab, 0, nj) for nj in range(n_merge)], m_out)
        stage_b += scan_parts(1, 0, zeros, [functools.partial(gate_c, 0, nj) for nj in range(n_merge)], m_out)
        stage_b += tail_parts(0)

        def write_m():
            mout_ref[0:1, :] = m_out[0]
            mout_ref[1:2, :] = m_out[1]

        return stage_a, stage_b + [write_m]

    def run_loops():
        def phase_a(i, carry):
            for part in a_parts:
                part(i)
            return carry

        lax.fori_loop(0, n_tiles, phase_a, 0)
        for part in pool_parts:
            part()
        cn_scr[...] = cn0_ref[...]
        m_init = m0_ref[...]

        def fwd_step(j, m_in):
            m_out = {}
            for part in scan_parts(0, j, m_in, [functools.partial(merge_ab, j, nj) for nj in range(n_merge)], m_out):
                part()
            return m_out[0]

        def bwd_step(j, m_in):
            c = n_chunks - 1 - j
            m_out = {}
            for part in scan_parts(1, c, m_in, [functools.partial(gate_c, c, nj) for nj in range(n_merge)], m_out):
                part()
            for part in tail_parts(c):
                part()
            return m_out[1]

        lax.fori_loop(0, n_chunks, fwd_step, m_init[0:1, :])
        lax.fori_loop(0, n_chunks, bwd_step, m_init[1:2, :])

    assert has_state, "multi-chunk sequences carry the cached state"
    return [run_loops], []


def _mixer_call(l, xall, mods4, prm, stream, states=None, state_outs=None):
    ctx = stream == "ctx"
    T = SEQ if ctx else DEC_SEQ
    assert T % TILE_M == 0 and TILE_M == MLSTM_L and MLSTM_L % MLSTM_ROW_BLOCK == 0
    n_e = CTX_SEQS_PER_STEP if ctx else 1
    n_b = (BATCH if ctx else DEC_BATCH) // n_e
    blk0 = 0 if ctx else N_TOK_CTX // T
    once = pl.Buffered(1)

    def wspec(shape):
        nd = len(shape)
        return pl.BlockSpec((None,) + shape, lambda b: (l,) + (0,) * nd, pipeline_mode=once)

    x_spec = pl.BlockSpec((n_e * T, D_MODEL), lambda b: (blk0 + b, 0), pipeline_mode=None if ctx else once)
    if ctx:
        mod_spec = pl.BlockSpec((None, None, 1, 6 * D_MODEL), lambda b: (l, 0, 0, 0))
    else:
        mod_spec = pl.BlockSpec((None, None, 1, 6 * D_MODEL), lambda b: (l, 1 + b, 0, 0))
    in_specs = [
        x_spec, mod_spec,
        wspec((1, D_MODEL)),
        wspec((D_MODEL, N_MAIN)),
        wspec((D_BRANCH, D_MODEL)),
        wspec((D_MODEL, GATE_PAD)),
        wspec((D_MODEL, N_BRANCH * D_MODEL)),
        wspec((N_BRANCH, D_BRANCH, D_MODEL)),
        wspec((D_MODEL, D_MODEL)),
        wspec((N_GROUPS, GROUP, GROUP)),
        wspec((1, D_BRANCH)),
        wspec((1, D_BRANCH)),
        wspec((N_GROUPS, SGU_CHUNK, SGU_CHUNK)),
        wspec((SGU_CHUNK, D_BRANCH)),
        wspec((N_GATE, 1)),
        wspec((1, D_BRANCH)),
    ]
    args = [xall, mods4, prm["g_norm1"], prm["w_in16"], prm["w_kt"], prm["w_gate"], prm["w_brg"],
            prm["w_branch"], prm["w_out"], prm["w_pool"], prm["pool_scale"], prm["g_sgu"], prm["w_sgu"],
            prm["b_sgu_tile"], prm["b_gates"], prm["g_mlstm"]]
    out_shape = [jax.ShapeDtypeStruct(xall.shape, xall.dtype)]
    out_specs = [x_spec]
    aliases = {0: 0}
    passthrough = []
    if ctx:
        out_shape += [jax.ShapeDtypeStruct(a.shape, a.dtype) for a in state_outs]
        out_specs += [
            pl.BlockSpec((n_e, None, 2, N_HEADS, HEAD_DIM, HEAD_DIM), lambda b: (b, l, 0, 0, 0, 0)),
            pl.BlockSpec((n_e, None, 2, N_HEADS, HEAD_DIM), lambda b: (b, l, 0, 0, 0)),
            pl.BlockSpec((n_e, None, 2, N_HEADS), lambda b: (b, l, 0, 0)),
        ]
        passthrough += [(a, 1 + k) for k, a in enumerate(state_outs)]
    else:
        state_cn, state_m = states
        in_specs += [
            pl.BlockSpec((None, None, 2, N_HEADS, HEAD_DIM, 2 * HEAD_DIM), lambda b: (b, l, 0, 0, 0, 0)),
            pl.BlockSpec((None, None, 2, N_HEADS), lambda b: (b, l, 0, 0)),
        ]
        args += [state_cn, state_m]
    for buf, out_idx in passthrough:
        aliases[len(args)] = out_idx
        in_specs.append(pl.BlockSpec(memory_space=pl.ANY))
        args.append(buf)

    scratch = [
        pltpu.VMEM((T, D_MODEL), BF16),
        pltpu.VMEM((T + 2 * POOL_PAD, D_BRANCH), F32),
        pltpu.VMEM((TILE_M, D_BRANCH), F32),
        pltpu.VMEM((TILE_M, D_BRANCH), BF16),
        pltpu.VMEM((T, D_BRANCH), BF16),
        pltpu.VMEM((D_BRANCH, T), BF16),
        pltpu.VMEM((T, 2 * D_BRANCH), BF16),
        pltpu.VMEM((T, D_BRANCH), F32),
        pltpu.VMEM((2 * N_HEADS, T), F32),
        pltpu.VMEM((T, GATE_PAD), F32),
        pltpu.VMEM((T, D_BRANCH), F32),
        pltpu.VMEM((T, D_BRANCH), BF16),
        pltpu.VMEM((T, D_BRANCH), BF16),
        pltpu.VMEM((TILE_M, D_BRANCH), BF16),
        pltpu.VMEM((T, D_MODEL), F32),
        pltpu.VMEM((TILE_M, D_MODEL), F32),
        pltpu.VMEM((TILE_M, D_MODEL), BF16),
    ]
    if not ctx:
        scratch += [pltpu.VMEM((2, N_HEADS, HEAD_DIM, 2 * HEAD_DIM), F32)]
        scratch += [pltpu.VMEM((T, GROUP), F32)]
    scratch = [pltpu.VMEM((n_e,) + tuple(s.shape), s.dtype) for s in scratch]

    outs = pl.pallas_call(
        functools.partial(_mixer_kernel, T=T, grid2d=not ctx, has_state=not ctx, n_elems=n_e,
                          n_in=len(in_specs), n_pass=len(passthrough), n_out=len(out_specs)),
        grid=(n_b,),
        in_specs=in_specs,
        out_specs=out_specs,
        out_shape=out_shape,
        scratch_shapes=scratch,
        input_output_aliases=aliases,
        compiler_params=pltpu.CompilerParams(
            dimension_semantics=("arbitrary",), vmem_limit_bytes=VMEM_LIMIT_BYTES),
        name="mixer_ctx" if ctx else "mixer_lat",
    )(*args)
    return outs


def _ffn_kernel(x_ref, mod_ref, g2_ref, w1_hbm, w2_hbm, gfin_ref, *rest, layer, final):
    n_outs = 2 if final else 1
    out_refs = rest[:n_outs]
    acc_scr, h_scr, w1_scr, w2_scr, st1_scr, st2_scr, sem = rest[n_outs:]
    n_chunks = D_FF // FFN_CHUNK
    first_tile = pl.program_id(0) == 0

    def w1_copy(c):
        slot = c % FFN_STAGES
        c0 = _aligned(c * FFN_CHUNK, FFN_CHUNK)
        return pltpu.make_async_copy(w1_hbm.at[layer, :, pl.ds(c0, FFN_CHUNK)], st1_scr.at[slot], sem.at[0, slot])

    def w2_copy(c):
        slot = c % FFN_STAGES
        c0 = _aligned(c * FFN_CHUNK, FFN_CHUNK)
        return pltpu.make_async_copy(w2_hbm.at[layer, pl.ds(c0, FFN_CHUNK), :], st2_scr.at[slot], sem.at[1, slot])

    def land_w1(c):
        w1_copy(c).wait()
        c0 = _aligned(c * FFN_CHUNK, FFN_CHUNK)
        w1_scr[:, pl.ds(c0, FFN_CHUNK)] = st1_scr[c % FFN_STAGES].astype(BF16)

    def land_w2(c):
        w2_copy(c).wait()
        c0 = _aligned(c * FFN_CHUNK, FFN_CHUNK)
        w2_scr[pl.ds(c0, FFN_CHUNK), :] = st2_scr[c % FFN_STAGES].astype(BF16)

    def normed_input():
        mod = mod_ref[...]
        sh2 = mod[:, 3 * D_MODEL:4 * D_MODEL]
        sc2 = mod[:, 4 * D_MODEL:5 * D_MODEL]
        x = x_ref[...]
        ms = jnp.mean(x * x, axis=-1, keepdims=True)
        return (x * lax.rsqrt(ms + EPS) * g2_ref[...] * (1.0 + sc2) + sh2).astype(BF16)

    def hidden(ub, c):
        c0 = _aligned(c * FFN_CHUNK, FFN_CHUNK)
        h = _dot(ub, w1_scr[:, pl.ds(c0, FFN_CHUNK)])
        h_scr[:, pl.ds(c0, FFN_CHUNK)] = jnp.square(jnp.maximum(h, 0.0)).astype(BF16)

    def finish(ffn_out):
        gt2 = mod_ref[...][:, 5 * D_MODEL:6 * D_MODEL]
        y = x_ref[...] + gt2 * ffn_out
        if not final:
            out_refs[0][...] = y
            return
        y = y * lax.rsqrt(jnp.mean(y * y, axis=-1, keepdims=True) + EPS) * gfin_ref[...]
        in_ctx = pl.program_id(0) < N_TOK_CTX // FFN_TILE_M

        @pl.when(in_ctx)
        def _():
            out_refs[0][...] = y

        @pl.when(jnp.logical_not(in_ctx))
        def _():
            out_refs[1][...] = y

    @pl.when(first_tile)
    def _():
        for c in range(FFN_STAGES):
            w1_copy(c).start()
            w2_copy(c).start()
        ub = normed_input()
        acc_scr[...] = jnp.zeros_like(acc_scr)

        def chunk(c, carry):
            land_w1(c)
            land_w2(c)

            @pl.when(c + FFN_STAGES < n_chunks)
            def _():
                w1_copy(c + FFN_STAGES).start()
                w2_copy(c + FFN_STAGES).start()

            hidden(ub, c)
            c0 = _aligned(c * FFN_CHUNK, FFN_CHUNK)
            acc_scr[...] += _dot(h_scr[:, pl.ds(c0, FFN_CHUNK)], w2_scr[pl.ds(c0, FFN_CHUNK), :])
            return carry

        lax.fori_loop(0, n_chunks, chunk, 0)
        finish(acc_scr[...])

    @pl.when(jnp.logical_not(first_tile))
    def _():
        ub = normed_input()
        for c in range(n_chunks):
            hidden(ub, c)
        finish(_dot(h_scr[...], w2_scr[...]))


def _ffn_call(l, xall, mods4, prm, g_final, final):
    ctx_tiles = N_TOK_CTX // FFN_TILE_M
    lat_tiles = DEC_SEQ // FFN_TILE_M
    once = pl.Buffered(1)

    def mod_map(t):
        return (l, jnp.where(t < ctx_tiles, 0, 1 + (t - ctx_tiles) // lat_tiles), 0, 0)

    x_spec = pl.BlockSpec((FFN_TILE_M, D_MODEL), lambda t: (t, 0))
    if final:
        out_specs = [
            pl.BlockSpec((FFN_TILE_M, D_MODEL), lambda t: (jnp.minimum(t, ctx_tiles - 1), 0)),
            pl.BlockSpec((FFN_TILE_M, D_MODEL), lambda t: (jnp.maximum(t - ctx_tiles, 0), 0)),
        ]
        out_shape = [jax.ShapeDtypeStruct((N_TOK_CTX, D_MODEL), F32),
                     jax.ShapeDtypeStruct((N_TOK - N_TOK_CTX, D_MODEL), F32)]
        aliases = {}
    else:
        out_specs, out_shape, aliases = x_spec, jax.ShapeDtypeStruct(xall.shape, xall.dtype), {0: 0}
    return pl.pallas_call(
        functools.partial(_ffn_kernel, layer=l, final=final),
        grid=(N_TOK // FFN_TILE_M,),
        in_specs=[
            x_spec,
            pl.BlockSpec((None, None, 1, 6 * D_MODEL), mod_map),
            pl.BlockSpec((None, 1, D_MODEL), lambda t: (l, 0, 0), pipeline_mode=once),
            pl.BlockSpec(memory_space=pl.ANY),
            pl.BlockSpec(memory_space=pl.ANY),
            pl.BlockSpec((1, D_MODEL), lambda t: (0, 0)),
        ],
        out_specs=out_specs,
        out_shape=out_shape,
        scratch_shapes=[
            pltpu.VMEM((FFN_TILE_M, D_MODEL), F32),
            pltpu.VMEM((FFN_TILE_M, D_FF), BF16),
            pltpu.VMEM((D_MODEL, D_FF), BF16),
            pltpu.VMEM((D_FF, D_MODEL), BF16),
            pltpu.VMEM((FFN_STAGES, D_MODEL, FFN_CHUNK), F32),
            pltpu.VMEM((FFN_STAGES, FFN_CHUNK, D_MODEL), F32),
            pltpu.SemaphoreType.DMA((2, FFN_STAGES)),
        ],
        input_output_aliases=aliases,
        compiler_params=pltpu.CompilerParams(
            dimension_semantics=("arbitrary",), vmem_limit_bytes=VMEM_LIMIT_BYTES),
        name="ffn_final" if final else "ffn",
    )(xall, mods4, prm["g_norm2"], prm["w_ff1"], prm["w_ff2"], g_final.reshape(1, D_MODEL))


def kernel(x_prompt, x_sample, state_C, state_n, state_m, c, c_ctx, w_ada, b_ada, g_norm1, g_norm2, w_in,
           b_gates, w_pool, pool_scale, g_sgu, w_sgu, b_sgu, g_mlstm, w_branch, w_out, w_ff1, w_ff2, g_final):
    w_in16 = w_in.astype(BF16)
    prm = {
        "w_in16": w_in16,
        "w_gate": jnp.pad(w_in16[:, :, N_MAIN:N_MAIN + N_GATE], ((0, 0), (0, 0), (0, GATE_PAD - N_GATE))),
        "w_brg": w_in16[:, :, N_MAIN + N_GATE:],
        "w_kt": jnp.swapaxes(w_in16[:, :, 4 * D_BRANCH:5 * D_BRANCH], 1, 2),
        "w_branch": w_branch.astype(BF16),
        "w_out": w_out.astype(BF16),
        "w_pool": w_pool.astype(BF16),
        "w_sgu": w_sgu.astype(BF16),
        "w_ff1": w_ff1,
        "w_ff2": w_ff2,
        "g_norm1": g_norm1.reshape(DEPTH, 1, D_MODEL),
        "g_norm2": g_norm2.reshape(DEPTH, 1, D_MODEL),
        "pool_scale": pool_scale.reshape(DEPTH, 1, D_BRANCH),
        "g_sgu": g_sgu.reshape(DEPTH, 1, D_BRANCH),
        "g_mlstm": g_mlstm.reshape(DEPTH, 1, D_BRANCH),
        "b_gates": b_gates.reshape(DEPTH, N_GATE, 1),
        "b_sgu_tile": jnp.repeat(jnp.swapaxes(b_sgu, 1, 2), GROUP, axis=2),
    }
    cond8 = jnp.zeros((COND_ROWS, D_MODEL), F32).at[0].set(c_ctx).at[1:1 + DEC_BATCH].set(c)
    mods = _ada_call(cond8, w_ada, b_ada)
    mods4 = mods.reshape(DEPTH, COND_ROWS, 1, 6 * D_MODEL)

    state_cn = jnp.concatenate(
        [state_C, jnp.broadcast_to(state_n[..., None], state_n.shape + (HEAD_DIM,))], axis=-1)
    xall = jnp.concatenate([x_prompt.reshape(N_TOK_CTX, D_MODEL), x_sample.reshape(-1, D_MODEL)], axis=0)
    new_states = [jnp.zeros((BATCH, DEPTH, 2, N_HEADS, HEAD_DIM, HEAD_DIM), F32),
                  jnp.zeros((BATCH, DEPTH, 2, N_HEADS, HEAD_DIM), F32),
                  jnp.zeros((BATCH, DEPTH, 2, N_HEADS), F32)]
    for l in range(DEPTH):
        xall, *new_states = _mixer_call(l, xall, mods4, prm, "ctx", state_outs=new_states)
        (xall,) = _mixer_call(l, xall, mods4, prm, "lat", states=(state_cn, state_m))
        xall = _ffn_call(l, xall, mods4, prm, g_final, final=(l == DEPTH - 1))
    y_prompt, y_sample = xall
    return (y_prompt.reshape(BATCH, SEQ, D_MODEL), y_sample.reshape(DEC_BATCH, DEC_SEQ, D_MODEL), *new_states)
```

```python
import functools

import jax
import jax.numpy as jnp
from jax import lax
from jax.experimental import pallas as pl
from jax.experimental.pallas import tpu as pltpu

F32 = jnp.float32
BF16 = jnp.bfloat16

D_MODEL = 1024
DEPTH = 4
BATCH = 16
SEQ = 256
DEC_BATCH = 2
DEC_SEQ = 1024
GRID_W = 64
D_BRANCH = 512
POOL_WINDOWS = (2, 4, 8, 16)
N_GROUPS = 4
GROUP = D_BRANCH // N_GROUPS
SGU_CHUNK = 128
N_HEADS = 4
HEAD_DIM = D_BRANCH // N_HEADS
N_BRANCH = 3
D_FF = 4 * D_MODEL
EPS = 1e-6

N_MAIN = 7 * D_BRANCH
N_GATE = 4 * N_HEADS
GATE_PAD = 128
N_TOK_CTX = BATCH * SEQ
N_TOK = N_TOK_CTX + DEC_BATCH * DEC_SEQ
COND_ROWS = 8

TILE_M = 256
CTX_SEQS_PER_STEP = 2
MLSTM_L = 256
MLSTM_ROW_BLOCK = 128
MERGE_COLS = 256
POOL_PAD = 8
FFN_TILE_M = 512
FFN_CHUNK = 512
FFN_STAGES = 2
ADA_TILE_N = 1536
VMEM_LIMIT_BYTES = 60 * 1024 * 1024


_dot = functools.partial(jnp.dot, preferred_element_type=F32)
_dot_nt = functools.partial(lax.dot_general, dimension_numbers=(((1,), (1,)), ((), ())),
                            preferred_element_type=F32)


def _aligned(offset, align):
    return offset if isinstance(offset, int) else pl.multiple_of(offset, align)


def _ada_kernel(cond_ref, w_ref, b_ref, out_ref):
    cond = cond_ref[...]
    s = (cond * jax.nn.sigmoid(cond)).astype(BF16)
    out_ref[...] = _dot(s, w_ref[...].astype(BF16)) + b_ref[...]


def _ada_call(cond8, w_ada, b_ada):
    n_out = 6 * D_MODEL
    return pl.pallas_call(
        _ada_kernel,
        grid=(DEPTH, n_out // ADA_TILE_N),
        in_specs=[
            pl.BlockSpec((COND_ROWS, D_MODEL), lambda l, j: (0, 0)),
            pl.BlockSpec((None, D_MODEL, ADA_TILE_N), lambda l, j: (l, 0, j)),
            pl.BlockSpec((None, 1, ADA_TILE_N), lambda l, j: (l, 0, j)),
        ],
        out_specs=pl.BlockSpec((None, COND_ROWS, ADA_TILE_N), lambda l, j: (l, 0, j)),
        out_shape=jax.ShapeDtypeStruct((DEPTH, COND_ROWS, n_out), F32),
        compiler_params=pltpu.CompilerParams(
            dimension_semantics=("arbitrary", "arbitrary"), vmem_limit_bytes=VMEM_LIMIT_BYTES),
        name="ada_mod",
    )(cond8, w_ada, b_ada.reshape(DEPTH, 1, n_out))


def _seg_scan(x, pos, seg, total, op, reverse):
    d = 1
    while d < seg:
        if reverse:
            shifted = pltpu.roll(x, total - d, 1)
            valid = pos < seg - d
        else:
            shifted = pltpu.roll(x, d, 1)
            valid = pos >= d
        if op == "sum":
            x = x + jnp.where(valid, shifted, 0.0)
        else:
            x = jnp.maximum(x, jnp.where(valid, shifted, -jnp.inf))
        d *= 2
    return x


def _interleave(first, second):
    out, i, j = [], 0, 0
    while i < len(first) or j < len(second):
        if j >= len(second) or (i < len(first) and i * len(second) <= j * len(first)):
            out.append(first[i])
            i += 1
        else:
            out.append(second[j])
            j += 1
    return out


def _mixer_kernel(*refs, T, grid2d, has_state, n_elems, n_in, n_pass, n_out):
    ins, outs, scr = refs[:n_in - n_pass], refs[n_in:n_in + n_out], refs[n_in + n_out:]
    stages = []
    for e in range(n_elems):
        rows = slice(e * T, (e + 1) * T)
        ins_e = [ins[0].at[rows, :]] + list(ins[1:])
        outs_e = [outs[0].at[rows, :]] + [o.at[e] for o in outs[1:]]
        scr_e = [s.at[e] for s in scr]
        stages.append(_mixer_sequence(*ins_e, *outs_e, *scr_e, T=T, grid2d=grid2d, has_state=has_state))
    order = list(stages[0][0])
    for e in range(n_elems):
        order += _interleave(stages[e][1], stages[e + 1][0] if e + 1 < n_elems else [])
    for thunk in order:
        thunk()


def _mixer_sequence(*refs, T, grid2d, has_state):
    it = iter(refs)
    x_ref = next(it)
    mod_ref = next(it)
    g1_ref = next(it)
    wmain_ref = next(it)
    wkt_ref = next(it)
    wgate_ref = next(it)
    wbrg_ref = next(it)
    wbranch_ref = next(it)
    wout_ref = next(it)
    wpool_ref = next(it)
    pscale_ref = next(it)
    gsgu_ref = next(it)
    wsgu_ref = next(it)
    bsgu_ref = next(it)
    bgates_ref = next(it)
    gml_ref = next(it)
    if has_state:
        cn0_ref, m0_ref = next(it), next(it)
    out_ref = next(it)
    if not has_state:
        cout_ref, nout_ref, mout_ref = next(it), next(it), next(it)
    (u_scr, xp_scr, su_scr, vn_scr, q_scr, kt_scr, vaug_scr, o_scr, grow_scr, colf_scr,
     h_scr, ya_scr, yb_scr, yc_scr, mab_scr, gc_scr, merged_scr) = [next(it) for _ in range(17)]
    if has_state:
        cn_scr = next(it)
    if grid2d:
        m1_scr, dlt_scr = next(it), next(it)

    L = MLSTM_L
    n_tiles = T // TILE_M
    n_chunks = T // L
    RB = MLSTM_ROW_BLOCK
    NW = MERGE_COLS
    n_merge = D_MODEL // NW

    def tile_rows(i):
        return pl.ds(_aligned(i * TILE_M, TILE_M), TILE_M)

    def a_norm(i):
        mod = mod_ref[...]
        sh1 = mod[:, 0:D_MODEL]
        sc1 = mod[:, D_MODEL:2 * D_MODEL]
        x = x_ref[tile_rows(i), :]
        ms = jnp.mean(x * x, axis=-1, keepdims=True)
        u = x * lax.rsqrt(ms + EPS) * g1_ref[...] * (1.0 + sc1) + sh1
        u_scr[tile_rows(i), :] = u.astype(BF16)

    def a_gates(i):
        lane = lax.broadcasted_iota(jnp.int32, (2 * N_HEADS, TILE_M), 1)
        is_fwd = lax.broadcasted_iota(jnp.int32, (2 * N_HEADS, TILE_M), 0) < N_HEADS
        gate_t = _dot(u_scr[tile_rows(i), :], wgate_ref[...]).T
        gp = gate_t[0:N_GATE, :] + bgates_ref[...]
        i_all = gp[0:2 * N_HEADS, :]
        f_all = gp[2 * N_HEADS:4 * N_HEADS, :]
        logf = jnp.minimum(f_all, 0.0) - jnp.log1p(jnp.exp(-jnp.abs(f_all)))
        b_row = jnp.where(is_fwd, _seg_scan(logf, lane, L, L, "sum", False),
                          _seg_scan(logf, lane, L, L, "sum", True))
        g_row = i_all - b_row
        cm_row = jnp.where(is_fwd, _seg_scan(g_row, lane, L, L, "max", False),
                           _seg_scan(g_row, lane, L, L, "max", True))
        grow_scr[:, tile_rows(i)] = g_row
        stacked = jnp.concatenate(
            [b_row, cm_row, jnp.zeros((GATE_PAD - 4 * N_HEADS, TILE_M), F32)], axis=0)
        colf_scr[tile_rows(i), :] = stacked.T

    def proj(i, k):
        return _dot(u_scr[tile_rows(i), :], wmain_ref[:, k * D_BRANCH:(k + 1) * D_BRANCH])

    def a_xp(i):
        xp_scr[pl.ds(_aligned(i * TILE_M + POOL_PAD, 8), TILE_M), :] = proj(i, 0)

    def a_su(i):
        su_scr[...] = proj(i, 1)

    def a_sv(i):
        sv = proj(i, 2)
        vn = sv * lax.rsqrt(jnp.mean(sv * sv, axis=-1, keepdims=True) + EPS) * gsgu_ref[...]
        vn_scr[...] = vn.astype(BF16)

    def a_q(i):
        q_scr[tile_rows(i), :] = proj(i, 3).astype(BF16)

    def a_k(i):
        kt = _dot_nt(wkt_ref[...], u_scr[tile_rows(i), :])
        kt_scr[:, tile_rows(i)] = (kt * (HEAD_DIM ** -0.5)).astype(BF16)

    def a_v(i):
        v = proj(i, 5).astype(BF16)
        ones_blk = jnp.ones((TILE_M, HEAD_DIM), BF16)
        for h in range(N_HEADS):
            vaug_scr[tile_rows(i), 2 * h * HEAD_DIM:(2 * h + 1) * HEAD_DIM] = v[:, h * HEAD_DIM:(h + 1) * HEAD_DIM]
            vaug_scr[tile_rows(i), (2 * h + 1) * HEAD_DIM:(2 * h + 2) * HEAD_DIM] = ones_blk

    def a_o(i):
        o_scr[tile_rows(i), :] = proj(i, 6)

    def a_sgu(ch, i):
        crow = slice(ch * SGU_CHUNK, (ch + 1) * SGU_CHUNK)
        for g in range(N_GROUPS):
            gc = slice(g * GROUP, (g + 1) * GROUP)
            mixed = _dot(wsgu_ref[g], vn_scr[crow, gc]) + bsgu_ref[:, gc]
            yb = su_scr[crow, gc] * mixed
            yb_scr[pl.ds(_aligned(i * TILE_M + ch * SGU_CHUNK, SGU_CHUNK), SGU_CHUNK), gc] = yb.astype(BF16)

    a_parts = [a_norm, a_gates, a_xp, a_su, a_sv, a_q, a_k, a_v, a_o]
    a_parts += [functools.partial(a_sgu, ch) for ch in range(TILE_M // SGU_CHUNK)]

    def pool_pad():
        zpad = jnp.zeros((POOL_PAD, D_BRANCH), F32)
        xp_scr[0:POOL_PAD, :] = zpad
        xp_scr[T + POOL_PAD:T + 2 * POOL_PAD, :] = zpad

    def pool_seq(g):
        win = POOL_WINDOWS[g]
        gc = slice(g * GROUP, (g + 1) * GROUP)
        half = win // 2
        t_idx = lax.broadcasted_iota(jnp.int32, (T, GROUP), 0)
        total = jnp.zeros((T, GROUP), F32)
        for j in range(-half, win - half):
            total = total + xp_scr[POOL_PAD + j:POOL_PAD + j + T, gc]
        cnt = (jnp.minimum(t_idx + (win - half), T) - jnp.maximum(t_idx - half, 0)).astype(F32)
        dlt = total / cnt - xp_scr[POOL_PAD:POOL_PAD + T, gc]
        ya = _dot(dlt.astype(BF16), wpool_ref[g]) * pscale_ref[...][:, gc]
        ya_scr[:, gc] = ya.astype(BF16)

    def a_colpool(i):
        c_idx = lax.broadcasted_iota(jnp.int32, (GRID_W, GROUP), 0)
        for g, win in enumerate(POOL_WINDOWS):
            gc = slice(g * GROUP, (g + 1) * GROUP)
            half = win // 2
            inv_cnt = 1.0 / (jnp.minimum(c_idx + (win - half), GRID_W) - jnp.maximum(c_idx - half, 0)).astype(F32)
            for r in range(TILE_M // GRID_W):
                r0 = i * TILE_M + r * GRID_W
                xg = xp_scr[pl.ds(_aligned(r0 + POOL_PAD, 8), GRID_W), gc]
                total = xg
                for j in range(-half, win - half):
                    if j != 0:
                        valid = (c_idx + j >= 0) & (c_idx + j < GRID_W)
                        total = total + jnp.where(valid, pltpu.roll(xg, (-j) % GRID_W, 0), 0.0)
                m1_scr[pl.ds(_aligned(r0, GRID_W), GRID_W), gc] = total * inv_cnt

    def pool_grid_rows(g):
        win = POOL_WINDOWS[g]
        gc = slice(g * GROUP, (g + 1) * GROUP)
        half = win // 2
        n_rows = T // GRID_W
        for r in range(n_rows):
            lo = max(r - half, 0)
            hi = min(r + (win - half), n_rows)
            acc = m1_scr[lo * GRID_W:(lo + 1) * GRID_W, gc]
            for rr in range(lo + 1, hi):
                acc = acc + m1_scr[rr * GRID_W:(rr + 1) * GRID_W, gc]
            xg = xp_scr[POOL_PAD + r * GRID_W:POOL_PAD + (r + 1) * GRID_W, gc]
            dlt_scr[r * GRID_W:(r + 1) * GRID_W, :] = acc / float(hi - lo) - xg
        ya = _dot(dlt_scr[...].astype(BF16), wpool_ref[g]) * pscale_ref[...][:, gc]
        ya_scr[:, gc] = ya.astype(BF16)

    if grid2d:
        assert TILE_M % GRID_W == 0
        a_parts.insert(a_parts.index(a_xp) + 1, a_colpool)
        pool_parts = [functools.partial(pool_grid_rows, g) for g in range(N_GROUPS)]
    else:
        pool_parts = [pool_pad] + [functools.partial(pool_seq, g) for g in range(N_GROUPS)]

    def keep_mask(d, rb, col_lo, ncols):
        t_loc = rb * RB + lax.broadcasted_iota(jnp.int32, (RB, ncols), 0)
        s_loc = col_lo + lax.broadcasted_iota(jnp.int32, (RB, ncols), 1)
        return s_loc <= t_loc if d == 0 else s_loc >= t_loc

    def mlstm_prepare(d, c, m_in):
        c0 = _aligned(c * L, L)
        colf = colf_scr[pl.ds(c0, L), :]
        bcol = colf[:, N_HEADS * d:N_HEADS * (d + 1)]
        cmcol = colf[:, 2 * N_HEADS + N_HEADS * d:2 * N_HEADS + N_HEADS * (d + 1)]
        mx = jnp.maximum(m_in, cmcol)
        last = L - 1 if d == 0 else 0
        mx_last = mx[last:last + 1, :]
        return dict(c0=c0, mx=mx, mx_last=mx_last, w_inter=jnp.exp(m_in - mx), e_negm=jnp.exp(-(bcol + mx)),
                    decay=jnp.exp(m_in - mx_last), m_new=bcol[last:last + 1, :] + mx_last)

    def block_cols(d, rb):
        return (0, (rb + 1) * RB) if d == 0 else (rb * RB, L)

    def mlstm_scores(d, p, h):
        c0 = p["c0"]
        hc = slice(h * HEAD_DIM, (h + 1) * HEAD_DIM)
        qks = []
        for rb in range(L // RB):
            col_lo, col_hi = block_cols(d, rb)
            qb = q_scr[pl.ds(_aligned(c0 + rb * RB, RB), RB), hc]
            qks.append(_dot(qb, kt_scr[hc, pl.ds(_aligned(c0 + col_lo, RB), col_hi - col_lo)]))
        p["qk", h] = qks

    def mlstm_head(d, p, h):
        c0, mx, mx_last = p["c0"], p["mx"], p["mx_last"]
        rows = pl.ds(c0, L)
        hc = slice(h * HEAD_DIM, (h + 1) * HEAD_DIM)
        ac = slice(2 * h * HEAD_DIM, (2 * h + 2) * HEAD_DIM)
        grow = grow_scr[N_HEADS * d + h:N_HEADS * d + h + 1, rows]
        if has_state:
            cn_dh = cn_scr[d, h]
            cn_bf = cn_dh.astype(BF16)
        for rb in range(L // RB):
            col_lo, col_hi = block_cols(d, rb)
            ncols = col_hi - col_lo
            trow = slice(rb * RB, (rb + 1) * RB)
            rrows = pl.ds(_aligned(c0 + rb * RB, RB), RB)
            crows = pl.ds(_aligned(c0 + col_lo, RB), ncols)
            qb = q_scr[rrows, hc]
            qk = p["qk", h][rb]
            w = jnp.exp(grow[:, col_lo:col_hi] - mx[trow, h:h + 1])
            s = (qk * jnp.where(keep_mask(d, rb, col_lo, ncols), w, 0.0)).astype(BF16)
            na = _dot(s, vaug_scr[crows, ac])
            if has_state:
                na = na + p["w_inter"][trow, h:h + 1] * _dot(qb, cn_bf)
            hval = na[:, 0:HEAD_DIM] / jnp.maximum(jnp.abs(na[:, HEAD_DIM:]), p["e_negm"][trow, h:h + 1])
            if d == 0:
                h_scr[rrows, hc] = hval
            else:
                h_scr[rrows, hc] += hval
        w_state = jnp.exp(grow - mx_last[:, h:h + 1])
        kw_t = (kt_scr[hc, rows].astype(F32) * w_state).astype(BF16)
        upd = _dot(kw_t, vaug_scr[rows, ac])
        if has_state:
            cn_scr[d, h] = p["decay"][:, h:h + 1] * cn_dh + upd
        else:
            cout_ref[d, h] = upd[:, 0:HEAD_DIM]
            nout_ref[d, h:h + 1, :] = upd[:, HEAD_DIM:].T[0:1, :]

    def merge_ab(c, nj):
        nc = slice(nj * NW, (nj + 1) * NW)
        ub = u_scr[tile_rows(c), :]
        acc = None
        for r, y_scr in enumerate((ya_scr, yb_scr)):
            gate = jax.nn.sigmoid(_dot(ub, wbrg_ref[:, r * D_MODEL + nj * NW:r * D_MODEL + (nj + 1) * NW]))
            term = gate * _dot(y_scr[tile_rows(c), :], wbranch_ref[r, :, nc])
            acc = term if acc is None else acc + term
        mab_scr[tile_rows(c), nc] = acc

    def gate_c(c, nj):
        w = wbrg_ref[:, 2 * D_MODEL + nj * NW:2 * D_MODEL + (nj + 1) * NW]
        gc_scr[:, nj * NW:(nj + 1) * NW] = jax.nn.sigmoid(_dot(u_scr[tile_rows(c), :], w))

    def tail_yc(c):
        gml = gml_ref[...]
        for h in range(N_HEADS):
            hc = slice(h * HEAD_DIM, (h + 1) * HEAD_DIM)
            hh = h_scr[tile_rows(c), hc]
            hn = hh * lax.rsqrt(jnp.mean(hh * hh, axis=-1, keepdims=True) + EPS) * gml[:, hc]
            yc_scr[:, hc] = (jax.nn.sigmoid(o_scr[tile_rows(c), hc]) * hn).astype(BF16)

    def tail_merge(c, nj):
        nc = slice(nj * NW, (nj + 1) * NW)
        acc = mab_scr[tile_rows(c), nc] + gc_scr[:, nc] * _dot(yc_scr[...], wbranch_ref[2, :, nc])
        merged_scr[:, nc] = acc.astype(BF16)

    def tail_out(c):
        gt1 = mod_ref[...][:, 2 * D_MODEL:3 * D_MODEL]
        out_ref[tile_rows(c), :] = x_ref[tile_rows(c), :] + gt1 * _dot(merged_scr[...], wout_ref[...])

    def tail_parts(c):
        return ([functools.partial(tail_yc, c)] + [functools.partial(tail_merge, c, nj) for nj in range(n_merge)]
                + [functools.partial(tail_out, c)])

    def scan_parts(d, c, m_in, fillers, out):
        p = {}

        def prepare():
            p.update(mlstm_prepare(d, c, m_in))
            out[d] = p["m_new"]

        parts = [prepare]
        for h in range(N_HEADS):
            parts.append(functools.partial(mlstm_scores, d, p, h))
            if h < len(fillers):
                parts.append(fillers[h])
            parts.append(functools.partial(mlstm_head, d, p, h))
        return parts + list(fillers[N_HEADS:])

    if n_tiles == 1 and n_chunks == 1:
        zeros = jnp.zeros((1, N_HEADS), F32)
        m_out = {}
        stage_a = [functools.partial(part, 0) for part in a_parts] + pool_parts
        stage_b = scan_parts(0, 0, zeros, [functools.partial(merge_ab, 0, nj) for nj in range(n_merge)], m_out)
        stage_b += scan_parts(1, 0, zeros, [functools.partial(gate_c, 0, nj) for nj in range(n_merge)], m_out)
        stage_b += tail_parts(0)

        def write_m():
            mout_ref[0:1, :] = m_out[0]
            mout_ref[1:2, :] = m_out[1]

        return stage_a, stage_b + [write_m]

    def run_loops():
        def phase_a(i, carry):
            for part in a_parts:
                part(i)
            return carry

        lax.fori_loop(0, n_tiles, phase_a, 0)
        for part in pool_parts:
            part()
        cn_scr[...] = cn0_ref[...]
        m_init = m0_ref[...]

        def fwd_step(j, m_in):
            m_out = {}
            for part in scan_parts(0, j, m_in, [functools.partial(merge_ab, j, nj) for nj in range(n_merge)], m_out):
                part()
            return m_out[0]

        def bwd_step(j, m_in):
            c = n_chunks - 1 - j
            m_out = {}
            for part in scan_parts(1, c, m_in, [functools.partial(gate_c, c, nj) for nj in range(n_merge)], m_out):
                part()
            for part in tail_parts(c):
                part()
            return m_out[1]

        lax.fori_loop(0, n_chunks, fwd_step, m_init[0:1, :])
        lax.fori_loop(0, n_chunks, bwd_step, m_init[1:2, :])

    assert has_state, "multi-chunk sequences carry the cached state"
    return [run_loops], []


def _mixer_call(l, xall, mods4, prm, stream, states=None, state_outs=None):
    ctx = stream == "ctx"
    T = SEQ if ctx else DEC_SEQ
    assert T % TILE_M == 0 and TILE_M == MLSTM_L and MLSTM_L % MLSTM_ROW_BLOCK == 0
    n_e = CTX_SEQS_PER_STEP if ctx else 1
    n_b = (BATCH if ctx else DEC_BATCH) // n_e
    blk0 = 0 if ctx else N_TOK_CTX // T
    once = pl.Buffered(1)

    def wspec(shape):
        nd = len(shape)
        return pl.BlockSpec((None,) + shape, lambda b: (l,) + (0,) * nd, pipeline_mode=once)

    x_spec = pl.BlockSpec((n_e * T, D_MODEL), lambda b: (blk0 + b, 0), pipeline_mode=None if ctx else once)
    if ctx:
        mod_spec = pl.BlockSpec((None, None, 1, 6 * D_MODEL), lambda b: (l, 0, 0, 0))
    else:
        mod_spec = pl.BlockSpec((None, None, 1, 6 * D_MODEL), lambda b: (l, 1 + b, 0, 0))
    in_specs = [
        x_spec, mod_spec,
        wspec((1, D_MODEL)),
        wspec((D_MODEL, N_MAIN)),
        wspec((D_BRANCH, D_MODEL)),
        wspec((D_MODEL, GATE_PAD)),
        wspec((D_MODEL, N_BRANCH * D_MODEL)),
        wspec((N_BRANCH, D_BRANCH, D_MODEL)),
        wspec((D_MODEL, D_MODEL)),
        wspec((N_GROUPS, GROUP, GROUP)),
        wspec((1, D_BRANCH)),
        wspec((1, D_BRANCH)),
        wspec((N_GROUPS, SGU_CHUNK, SGU_CHUNK)),
        wspec((SGU_CHUNK, D_BRANCH)),
        wspec((N_GATE, 1)),
        wspec((1, D_BRANCH)),
    ]
    args = [xall, mods4, prm["g_norm1"], prm["w_in16"], prm["w_kt"], prm["w_gate"], prm["w_brg"],
            prm["w_branch"], prm["w_out"], prm["w_pool"], prm["pool_scale"], prm["g_sgu"], prm["w_sgu"],
            prm["b_sgu_tile"], prm["b_gates"], prm["g_mlstm"]]
    out_shape = [jax.ShapeDtypeStruct(xall.shape, xall.dtype)]
    out_specs = [x_spec]
    aliases = {0: 0}
    passthrough = []
    if ctx:
        out_shape += [jax.ShapeDtypeStruct(a.shape, a.dtype) for a in state_outs]
        out_specs += [
            pl.BlockSpec((n_e, None, 2, N_HEADS, HEAD_DIM, HEAD_DIM), lambda b: (b, l, 0, 0, 0, 0)),
            pl.BlockSpec((n_e, None, 2, N_HEADS, HEAD_DIM), lambda b: (b, l, 0, 0, 0)),
            pl.BlockSpec((n_e, None, 2, N_HEADS), lambda b: (b, l, 0, 0)),
        ]
        passthrough += [(a, 1 + k) for k, a in enumerate(state_outs)]
    else:
        state_cn, state_m = states
        in_specs += [
            pl.BlockSpec((None, None, 2, N_HEADS, HEAD_DIM, 2 * HEAD_DIM), lambda b: (b, l, 0, 0, 0, 0)),
            pl.BlockSpec((None, None, 2, N_HEADS), lambda b: (b, l, 0, 0)),
        ]
        args += [state_cn, state_m]
    for buf, out_idx in passthrough:
        aliases[len(args)] = out_idx
        in_specs.append(pl.BlockSpec(memory_space=pl.ANY))
        args.append(buf)

    scratch = [
        pltpu.VMEM((T, D_MODEL), BF16),
        pltpu.VMEM((T + 2 * POOL_PAD, D_BRANCH), F32),
        pltpu.VMEM((TILE_M, D_BRANCH), F32),
        pltpu.VMEM((TILE_M, D_BRANCH), BF16),
        pltpu.VMEM((T, D_BRANCH), BF16),
        pltpu.VMEM((D_BRANCH, T), BF16),
        pltpu.VMEM((T, 2 * D_BRANCH), BF16),
        pltpu.VMEM((T, D_BRANCH), F32),
        pltpu.VMEM((2 * N_HEADS, T), F32),
        pltpu.VMEM((T, GATE_PAD), F32),
        pltpu.VMEM((T, D_BRANCH), F32),
        pltpu.VMEM((T, D_BRANCH), BF16),
        pltpu.VMEM((T, D_BRANCH), BF16),
        pltpu.VMEM((TILE_M, D_BRANCH), BF16),
        pltpu.VMEM((T, D_MODEL), F32),
        pltpu.VMEM((TILE_M, D_MODEL), F32),
        pltpu.VMEM((TILE_M, D_MODEL), BF16),
    ]
    if not ctx:
        scratch += [pltpu.VMEM((2, N_HEADS, HEAD_DIM, 2 * HEAD_DIM), F32)]
        scratch += [pltpu.VMEM((T, D_BRANCH), F32), pltpu.VMEM((T, GROUP), F32)]
    scratch = [pltpu.VMEM((n_e,) + tuple(s.shape), s.dtype) for s in scratch]

    outs = pl.pallas_call(
        functools.partial(_mixer_kernel, T=T, grid2d=not ctx, has_state=not ctx, n_elems=n_e,
                          n_in=len(in_specs), n_pass=len(passthrough), n_out=len(out_specs)),
        grid=(n_b,),
        in_specs=in_specs,
        out_specs=out_specs,
        out_shape=out_shape,
        scratch_shapes=scratch,
        input_output_aliases=aliases,
        compiler_params=pltpu.CompilerParams(
            dimension_semantics=("arbitrary",), vmem_limit_bytes=VMEM_LIMIT_BYTES),
        name="mixer_ctx" if ctx else "mixer_lat",
    )(*args)
    return outs


def _ffn_kernel(x_ref, mod_ref, g2_ref, w1_hbm, w2_hbm, gfin_ref, *rest, layer, final):
    n_outs = 2 if final else 1
    out_refs = rest[:n_outs]
    acc_scr, h_scr, w1_scr, w2_scr, st1_scr, st2_scr, sem = rest[n_outs:]
    n_chunks = D_FF // FFN_CHUNK
    first_tile = pl.program_id(0) == 0

    def w1_copy(c):
        slot = c % FFN_STAGES
        c0 = _aligned(c * FFN_CHUNK, FFN_CHUNK)
        return pltpu.make_async_copy(w1_hbm.at[layer, :, pl.ds(c0, FFN_CHUNK)], st1_scr.at[slot], sem.at[0, slot])

    def w2_copy(c):
        slot = c % FFN_STAGES
        c0 = _aligned(c * FFN_CHUNK, FFN_CHUNK)
        return pltpu.make_async_copy(w2_hbm.at[layer, pl.ds(c0, FFN_CHUNK), :], st2_scr.at[slot], sem.at[1, slot])

    def land_w1(c):
        w1_copy(c).wait()
        c0 = _aligned(c * FFN_CHUNK, FFN_CHUNK)
        w1_scr[:, pl.ds(c0, FFN_CHUNK)] = st1_scr[c % FFN_STAGES].astype(BF16)

    def land_w2(c):
        w2_copy(c).wait()
        c0 = _aligned(c * FFN_CHUNK, FFN_CHUNK)
        w2_scr[pl.ds(c0, FFN_CHUNK), :] = st2_scr[c % FFN_STAGES].astype(BF16)

    def normed_input():
        mod = mod_ref[...]
        sh2 = mod[:, 3 * D_MODEL:4 * D_MODEL]
        sc2 = mod[:, 4 * D_MODEL:5 * D_MODEL]
        x = x_ref[...]
        ms = jnp.mean(x * x, axis=-1, keepdims=True)
        return (x * lax.rsqrt(ms + EPS) * g2_ref[...] * (1.0 + sc2) + sh2).astype(BF16)

    def hidden(ub, c):
        c0 = _aligned(c * FFN_CHUNK, FFN_CHUNK)
        h = _dot(ub, w1_scr[:, pl.ds(c0, FFN_CHUNK)])
        h_scr[:, pl.ds(c0, FFN_CHUNK)] = jnp.square(jnp.maximum(h, 0.0)).astype(BF16)

    def finish(ffn_out):
        gt2 = mod_ref[...][:, 5 * D_MODEL:6 * D_MODEL]
        y = x_ref[...] + gt2 * ffn_out
        if not final:
            out_refs[0][...] = y
            return
        y = y * lax.rsqrt(jnp.mean(y * y, axis=-1, keepdims=True) + EPS) * gfin_ref[...]
        in_ctx = pl.program_id(0) < N_TOK_CTX // FFN_TILE_M

        @pl.when(in_ctx)
        def _():
            out_refs[0][...] = y

        @pl.when(jnp.logical_not(in_ctx))
        def _():
            out_refs[1][...] = y

    @pl.when(first_tile)
    def _():
        for c in range(FFN_STAGES):
            w1_copy(c).start()
            w2_copy(c).start()
        ub = normed_input()
        acc_scr[...] = jnp.zeros_like(acc_scr)

        def chunk(c, carry):
            land_w1(c)
            land_w2(c)

            @pl.when(c + FFN_STAGES < n_chunks)
            def _():
                w1_copy(c + FFN_STAGES).start()
                w2_copy(c + FFN_STAGES).start()

            hidden(ub, c)
            c0 = _aligned(c * FFN_CHUNK, FFN_CHUNK)
            acc_scr[...] += _dot(h_scr[:, pl.ds(c0, FFN_CHUNK)], w2_scr[pl.ds(c0, FFN_CHUNK), :])
            return carry

        lax.fori_loop(0, n_chunks, chunk, 0)
        finish(acc_scr[...])

    @pl.when(jnp.logical_not(first_tile))
    def _():
        ub = normed_input()
        for c in range(n_chunks):
            hidden(ub, c)
        finish(_dot(h_scr[...], w2_scr[...]))


def _ffn_call(l, xall, mods4, prm, g_final, final):
    ctx_tiles = N_TOK_CTX // FFN_TILE_M
    lat_tiles = DEC_SEQ // FFN_TILE_M
    once = pl.Buffered(1)

    def mod_map(t):
        return (l, jnp.where(t < ctx_tiles, 0, 1 + (t - ctx_tiles) // lat_tiles), 0, 0)

    x_spec = pl.BlockSpec((FFN_TILE_M, D_MODEL), lambda t: (t, 0))
    if final:
        out_specs = [
            pl.BlockSpec((FFN_TILE_M, D_MODEL), lambda t: (jnp.minimum(t, ctx_tiles - 1), 0)),
            pl.BlockSpec((FFN_TILE_M, D_MODEL), lambda t: (jnp.maximum(t - ctx_tiles, 0), 0)),
        ]
        out_shape = [jax.ShapeDtypeStruct((N_TOK_CTX, D_MODEL), F32),
                     jax.ShapeDtypeStruct((N_TOK - N_TOK_CTX, D_MODEL), F32)]
        aliases = {}
    else:
        out_specs, out_shape, aliases = x_spec, jax.ShapeDtypeStruct(xall.shape, xall.dtype), {0: 0}
    return pl.pallas_call(
        functools.partial(_ffn_kernel, layer=l, final=final),
        grid=(N_TOK // FFN_TILE_M,),
        in_specs=[
            x_spec,
            pl.BlockSpec((None, None, 1, 6 * D_MODEL), mod_map),
            pl.BlockSpec((None, 1, D_MODEL), lambda t: (l, 0, 0), pipeline_mode=once),
            pl.BlockSpec(memory_space=pl.ANY),
            pl.BlockSpec(memory_space=pl.ANY),
            pl.BlockSpec((1, D_MODEL), lambda t: (0, 0)),
        ],
        out_specs=out_specs,
        out_shape=out_shape,
        scratch_shapes=[
            pltpu.VMEM((FFN_TILE_M, D_MODEL), F32),
            pltpu.VMEM((FFN_TILE_M, D_FF), BF16),
            pltpu.VMEM((D_MODEL, D_FF), BF16),
            pltpu.VMEM((D_FF, D_MODEL), BF16),
            pltpu.VMEM((FFN_STAGES, D_MODEL, FFN_CHUNK), F32),
            pltpu.VMEM((FFN_STAGES, FFN_CHUNK, D_MODEL), F32),
            pltpu.SemaphoreType.DMA((2, FFN_STAGES)),
        ],
        input_output_aliases=aliases,
        compiler_params=pltpu.CompilerParams(
            dimension_semantics=("arbitrary",), vmem_limit_bytes=VMEM_LIMIT_BYTES),
        name="ffn_final" if final else "ffn",
    )(xall, mods4, prm["g_norm2"], prm["w_ff1"], prm["w_ff2"], g_final.reshape(1, D_MODEL))


def kernel(x_prompt, x_sample, state_C, state_n, state_m, c, c_ctx, w_ada, b_ada, g_norm1, g_norm2, w_in,
           b_gates, w_pool, pool_scale, g_sgu, w_sgu, b_sgu, g_mlstm, w_branch, w_out, w_ff1, w_ff2, g_final):
    w_in16 = w_in.astype(BF16)
    prm = {
        "w_in16": w_in16,
        "w_gate": jnp.pad(w_in16[:, :, N_MAIN:N_MAIN + N_GATE], ((0, 0), (0, 0), (0, GATE_PAD - N_GATE))),
        "w_brg": w_in16[:, :, N_MAIN + N_GATE:],
        "w_kt": jnp.swapaxes(w_in16[:, :, 4 * D_BRANCH:5 * D_BRANCH], 1, 2),
        "w_branch": w_branch.astype(BF16),
        "w_out": w_out.astype(BF16),
        "w_pool": w_pool.astype(BF16),
        "w_sgu": w_sgu.astype(BF16),
        "w_ff1": w_ff1,
        "w_ff2": w_ff2,
        "g_norm1": g_norm1.reshape(DEPTH, 1, D_MODEL),
        "g_norm2": g_norm2.reshape(DEPTH, 1, D_MODEL),
        "pool_scale": pool_scale.reshape(DEPTH, 1, D_BRANCH),
        "g_sgu": g_sgu.reshape(DEPTH, 1, D_BRANCH),
        "g_mlstm": g_mlstm.reshape(DEPTH, 1, D_BRANCH),
        "b_gates": b_gates.reshape(DEPTH, N_GATE, 1),
        "b_sgu_tile": jnp.repeat(jnp.swapaxes(b_sgu, 1, 2), GROUP, axis=2),
    }
    cond8 = jnp.zeros((COND_ROWS, D_MODEL), F32).at[0].set(c_ctx).at[1:1 + DEC_BATCH].set(c)
    mods = _ada_call(cond8, w_ada, b_ada)
    mods4 = mods.reshape(DEPTH, COND_ROWS, 1, 6 * D_MODEL)

    state_cn = jnp.concatenate(
        [state_C, jnp.broadcast_to(state_n[..., None], state_n.shape + (HEAD_DIM,))], axis=-1)
    xall = jnp.concatenate([x_prompt.reshape(N_TOK_CTX, D_MODEL), x_sample.reshape(-1, D_MODEL)], axis=0)
    new_states = [jnp.zeros((BATCH, DEPTH, 2, N_HEADS, HEAD_DIM, HEAD_DIM), F32),
                  jnp.zeros((BATCH, DEPTH, 2, N_HEADS, HEAD_DIM), F32),
                  jnp.zeros((BATCH, DEPTH, 2, N_HEADS), F32)]
    for l in range(DEPTH):
        xall, *new_states = _mixer_call(l, xall, mods4, prm, "ctx", state_outs=new_states)
        (xall,) = _mixer_call(l, xall, mods4, prm, "lat", states=(state_cn, state_m))
        xall = _ffn_call(l, xall, mods4, prm, g_final, final=(l == DEPTH - 1))
    y_prompt, y_sample = xall
    return (y_prompt.reshape(BATCH, SEQ, D_MODEL), y_sample.reshape(DEC_BATCH, DEC_SEQ, D_MODEL), *new_states)
```

```python
import functools

import jax
import jax.numpy as jnp
from jax import lax
from jax.experimental import pallas as pl
from jax.experimental.pallas import tpu as pltpu

F32 = jnp.float32
BF16 = jnp.bfloat16

D_MODEL = 1024
DEPTH = 4
BATCH = 16
SEQ = 256
DEC_BATCH = 2
DEC_SEQ = 1024
GRID_W = 64
D_BRANCH = 512
POOL_WINDOWS = (2, 4, 8, 16)
N_GROUPS = 4
GROUP = D_BRANCH // N_GROUPS
SGU_CHUNK = 128
N_HEADS = 4
HEAD_DIM = D_BRANCH // N_HEADS
N_BRANCH = 3
D_FF = 4 * D_MODEL
EPS = 1e-6

N_MAIN = 7 * D_BRANCH
N_GATE = 4 * N_HEADS
GATE_PAD = 128
N_TOK_CTX = BATCH * SEQ
N_TOK = N_TOK_CTX + DEC_BATCH * DEC_SEQ
COND_ROWS = 8

TILE_M = 256
CTX_SEQS_PER_STEP = 2
MLSTM_L = 256
MLSTM_ROW_BLOCK = 128
MERGE_COLS = 256
POOL_PAD = 8
FFN_TILE_M = 512
FFN_CHUNK = 512
FFN_STAGES = 2
ADA_TILE_N = 1536
VMEM_LIMIT_BYTES = 60 * 1024 * 1024


_dot = functools.partial(jnp.dot, preferred_element_type=F32)
_dot_nt = functools.partial(lax.dot_general, dimension_numbers=(((1,), (1,)), ((), ())),
                            preferred_element_type=F32)


def _aligned(offset, align):
    return offset if isinstance(offset, int) else pl.multiple_of(offset, align)


def _ada_block(cond_ref, w_ref, b_ref, out_ref):
    cond = cond_ref[...]
    s = (cond * jax.nn.sigmoid(cond)).astype(BF16)
    out_ref[...] = _dot(s, w_ref[...].astype(BF16)) + b_ref[...]


def _ada_call(cond8, w_ada, b_ada3):
    n_out = 6 * D_MODEL
    return pl.pallas_call(
        _ada_block,
        grid=(n_out // ADA_TILE_N,),
        in_specs=[
            pl.BlockSpec((COND_ROWS, D_MODEL), lambda j: (0, 0)),
            pl.BlockSpec((None, D_MODEL, ADA_TILE_N), lambda j: (0, 0, j)),
            pl.BlockSpec((None, 1, ADA_TILE_N), lambda j: (0, 0, j)),
        ],
        out_specs=pl.BlockSpec((COND_ROWS, ADA_TILE_N), lambda j: (0, j)),
        out_shape=jax.ShapeDtypeStruct((COND_ROWS, n_out), F32),
        compiler_params=pltpu.CompilerParams(
            dimension_semantics=("arbitrary",), vmem_limit_bytes=VMEM_LIMIT_BYTES),
        name="ada_mod",
    )(cond8, w_ada, b_ada3)


def _seg_scan(x, pos, seg, total, op, reverse):
    d = 1
    while d < seg:
        if reverse:
            shifted = pltpu.roll(x, total - d, 1)
            valid = pos < seg - d
        else:
            shifted = pltpu.roll(x, d, 1)
            valid = pos >= d
        if op == "sum":
            x = x + jnp.where(valid, shifted, 0.0)
        else:
            x = jnp.maximum(x, jnp.where(valid, shifted, -jnp.inf))
        d *= 2
    return x


def _interleave(first, second):
    out, i, j = [], 0, 0
    while i < len(first) or j < len(second):
        if j >= len(second) or (i < len(first) and i * len(second) <= j * len(first)):
            out.append(first[i])
            i += 1
        else:
            out.append(second[j])
            j += 1
    return out


def _mixer_kernel(*refs, T, grid2d, has_state, n_elems, n_in, n_pass, n_out):
    ins, outs, scr = refs[:n_in - n_pass], refs[n_in:n_in + n_out], refs[n_in + n_out:]
    stages = []
    for e in range(n_elems):
        rows = slice(e * T, (e + 1) * T)
        ins_e = [ins[0].at[rows, :]] + list(ins[1:])
        outs_e = [outs[0].at[rows, :]] + [o.at[e] for o in outs[1:]]
        scr_e = [s.at[e] for s in scr]
        stages.append(_mixer_sequence(*ins_e, *outs_e, *scr_e, T=T, grid2d=grid2d, has_state=has_state))
    order = list(stages[0][0])
    for e in range(n_elems):
        order += _interleave(stages[e][1], stages[e + 1][0] if e + 1 < n_elems else [])
    for thunk in order:
        thunk()


def _mixer_sequence(*refs, T, grid2d, has_state):
    it = iter(refs)
    x_ref = next(it)
    mod_ref = next(it)
    g1_ref = next(it)
    wmain_ref = next(it)
    wkt_ref = next(it)
    wgate_ref = next(it)
    wbrg_ref = next(it)
    wbranch_ref = next(it)
    wout_ref = next(it)
    wpool_ref = next(it)
    pscale_ref = next(it)
    gsgu_ref = next(it)
    wsgu_ref = next(it)
    bsgu_ref = next(it)
    bgates_ref = next(it)
    gml_ref = next(it)
    if has_state:
        cn0_ref, m0_ref = next(it), next(it)
    out_ref = next(it)
    if not has_state:
        cout_ref, nout_ref, mout_ref = next(it), next(it), next(it)
    (u_scr, xp_scr, su_scr, vn_scr, q_scr, kt_scr, vaug_scr, o_scr, grow_scr, colf_scr,
     h_scr, ya_scr, yb_scr, yc_scr, mab_scr, gc_scr, merged_scr) = [next(it) for _ in range(17)]
    if has_state:
        cn_scr = next(it)
    if grid2d:
        m1_scr, dlt_scr = next(it), next(it)

    L = MLSTM_L
    n_tiles = T // TILE_M
    n_chunks = T // L
    RB = MLSTM_ROW_BLOCK
    NW = MERGE_COLS
    n_merge = D_MODEL // NW

    def tile_rows(i):
        return pl.ds(_aligned(i * TILE_M, TILE_M), TILE_M)

    def a_norm(i):
        mod = mod_ref[...]
        sh1 = mod[:, 0:D_MODEL]
        sc1 = mod[:, D_MODEL:2 * D_MODEL]
        x = x_ref[tile_rows(i), :]
        ms = jnp.mean(x * x, axis=-1, keepdims=True)
        u = x * lax.rsqrt(ms + EPS) * g1_ref[...] * (1.0 + sc1) + sh1
        u_scr[tile_rows(i), :] = u.astype(BF16)

    def a_gates(i):
        lane = lax.broadcasted_iota(jnp.int32, (2 * N_HEADS, TILE_M), 1)
        is_fwd = lax.broadcasted_iota(jnp.int32, (2 * N_HEADS, TILE_M), 0) < N_HEADS
        gate_t = _dot(u_scr[tile_rows(i), :], wgate_ref[...]).T
        gp = gate_t[0:N_GATE, :] + bgates_ref[...]
        i_all = gp[0:2 * N_HEADS, :]
        f_all = gp[2 * N_HEADS:4 * N_HEADS, :]
        logf = jnp.minimum(f_all, 0.0) - jnp.log1p(jnp.exp(-jnp.abs(f_all)))
        b_row = jnp.where(is_fwd, _seg_scan(logf, lane, L, L, "sum", False),
                          _seg_scan(logf, lane, L, L, "sum", True))
        g_row = i_all - b_row
        cm_row = jnp.where(is_fwd, _seg_scan(g_row, lane, L, L, "max", False),
                           _seg_scan(g_row, lane, L, L, "max", True))
        grow_scr[:, tile_rows(i)] = g_row
        stacked = jnp.concatenate(
            [b_row, cm_row, jnp.zeros((GATE_PAD - 4 * N_HEADS, TILE_M), F32)], axis=0)
        colf_scr[tile_rows(i), :] = stacked.T

    def proj(i, k):
        return _dot(u_scr[tile_rows(i), :], wmain_ref[:, k * D_BRANCH:(k + 1) * D_BRANCH])

    def a_xp(i):
        xp_scr[pl.ds(_aligned(i * TILE_M + POOL_PAD, 8), TILE_M), :] = proj(i, 0)

    def a_su(i):
        su_scr[...] = proj(i, 1)

    def a_sv(i):
        sv = proj(i, 2)
        vn = sv * lax.rsqrt(jnp.mean(sv * sv, axis=-1, keepdims=True) + EPS) * gsgu_ref[...]
        vn_scr[...] = vn.astype(BF16)

    def a_q(i):
        q_scr[tile_rows(i), :] = proj(i, 3).astype(BF16)

    def a_k(i):
        kt = _dot_nt(wkt_ref[...], u_scr[tile_rows(i), :])
        kt_scr[:, tile_rows(i)] = (kt * (HEAD_DIM ** -0.5)).astype(BF16)

    def a_v(i):
        v = proj(i, 5).astype(BF16)
        ones_blk = jnp.ones((TILE_M, HEAD_DIM), BF16)
        for h in range(N_HEADS):
            vaug_scr[tile_rows(i), 2 * h * HEAD_DIM:(2 * h + 1) * HEAD_DIM] = v[:, h * HEAD_DIM:(h + 1) * HEAD_DIM]
            vaug_scr[tile_rows(i), (2 * h + 1) * HEAD_DIM:(2 * h + 2) * HEAD_DIM] = ones_blk

    def a_o(i):
        o_scr[tile_rows(i), :] = proj(i, 6)

    def a_sgu(ch, i):
        crow = slice(ch * SGU_CHUNK, (ch + 1) * SGU_CHUNK)
        for g in range(N_GROUPS):
            gc = slice(g * GROUP, (g + 1) * GROUP)
            mixed = _dot(wsgu_ref[g], vn_scr[crow, gc]) + bsgu_ref[:, gc]
            yb = su_scr[crow, gc] * mixed
            yb_scr[pl.ds(_aligned(i * TILE_M + ch * SGU_CHUNK, SGU_CHUNK), SGU_CHUNK), gc] = yb.astype(BF16)

    a_parts = [a_norm, a_gates, a_xp, a_su, a_sv, a_q, a_k, a_v, a_o]
    a_parts += [functools.partial(a_sgu, ch) for ch in range(TILE_M // SGU_CHUNK)]

    def pool_pad():
        zpad = jnp.zeros((POOL_PAD, D_BRANCH), F32)
        xp_scr[0:POOL_PAD, :] = zpad
        xp_scr[T + POOL_PAD:T + 2 * POOL_PAD, :] = zpad

    def pool_seq(g):
        win = POOL_WINDOWS[g]
        gc = slice(g * GROUP, (g + 1) * GROUP)
        half = win // 2
        t_idx = lax.broadcasted_iota(jnp.int32, (T, GROUP), 0)
        total = jnp.zeros((T, GROUP), F32)
        for j in range(-half, win - half):
            total = total + xp_scr[POOL_PAD + j:POOL_PAD + j + T, gc]
        cnt = (jnp.minimum(t_idx + (win - half), T) - jnp.maximum(t_idx - half, 0)).astype(F32)
        dlt = total / cnt - xp_scr[POOL_PAD:POOL_PAD + T, gc]
        ya = _dot(dlt.astype(BF16), wpool_ref[g]) * pscale_ref[...][:, gc]
        ya_scr[:, gc] = ya.astype(BF16)

    def a_colpool(i):
        c_idx = lax.broadcasted_iota(jnp.int32, (GRID_W, GROUP), 0)
        for g, win in enumerate(POOL_WINDOWS):
            gc = slice(g * GROUP, (g + 1) * GROUP)
            half = win // 2
            inv_cnt = 1.0 / (jnp.minimum(c_idx + (win - half), GRID_W) - jnp.maximum(c_idx - half, 0)).astype(F32)
            for r in range(TILE_M // GRID_W):
                r0 = i * TILE_M + r * GRID_W
                xg = xp_scr[pl.ds(_aligned(r0 + POOL_PAD, 8), GRID_W), gc]
                total = xg
                for j in range(-half, win - half):
                    if j != 0:
                        valid = (c_idx + j >= 0) & (c_idx + j < GRID_W)
                        total = total + jnp.where(valid, pltpu.roll(xg, (-j) % GRID_W, 0), 0.0)
                m1_scr[pl.ds(_aligned(r0, GRID_W), GRID_W), gc] = total * inv_cnt

    def pool_grid_rows(g):
        win = POOL_WINDOWS[g]
        gc = slice(g * GROUP, (g + 1) * GROUP)
        half = win // 2
        n_rows = T // GRID_W
        for r in range(n_rows):
            lo = max(r - half, 0)
            hi = min(r + (win - half), n_rows)
            acc = m1_scr[lo * GRID_W:(lo + 1) * GRID_W, gc]
            for rr in range(lo + 1, hi):
                acc = acc + m1_scr[rr * GRID_W:(rr + 1) * GRID_W, gc]
            xg = xp_scr[POOL_PAD + r * GRID_W:POOL_PAD + (r + 1) * GRID_W, gc]
            dlt_scr[r * GRID_W:(r + 1) * GRID_W, :] = acc / float(hi - lo) - xg
        ya = _dot(dlt_scr[...].astype(BF16), wpool_ref[g]) * pscale_ref[...][:, gc]
        ya_scr[:, gc] = ya.astype(BF16)

    if grid2d:
        assert TILE_M % GRID_W == 0
        a_parts.insert(a_parts.index(a_xp) + 1, a_colpool)
        pool_parts = [functools.partial(pool_grid_rows, g) for g in range(N_GROUPS)]
    else:
        pool_parts = [pool_pad] + [functools.partial(pool_seq, g) for g in range(N_GROUPS)]

    def keep_mask(d, rb, col_lo, ncols):
        t_loc = rb * RB + lax.broadcasted_iota(jnp.int32, (RB, ncols), 0)
        s_loc = col_lo + lax.broadcasted_iota(jnp.int32, (RB, ncols), 1)
        return s_loc <= t_loc if d == 0 else s_loc >= t_loc

    def mlstm_prepare(d, c, m_in):
        c0 = _aligned(c * L, L)
        colf = colf_scr[pl.ds(c0, L), :]
        bcol = colf[:, N_HEADS * d:N_HEADS * (d + 1)]
        cmcol = colf[:, 2 * N_HEADS + N_HEADS * d:2 * N_HEADS + N_HEADS * (d + 1)]
        mx = jnp.maximum(m_in, cmcol)
        last = L - 1 if d == 0 else 0
        mx_last = mx[last:last + 1, :]
        return dict(c0=c0, mx=mx, mx_last=mx_last, w_inter=jnp.exp(m_in - mx), e_negm=jnp.exp(-(bcol + mx)),
                    decay=jnp.exp(m_in - mx_last), m_new=bcol[last:last + 1, :] + mx_last)

    def block_cols(d, rb):
        return (0, (rb + 1) * RB) if d == 0 else (rb * RB, L)

    def mlstm_scores(d, p, h):
        c0 = p["c0"]
        hc = slice(h * HEAD_DIM, (h + 1) * HEAD_DIM)
        qks = []
        for rb in range(L // RB):
            col_lo, col_hi = block_cols(d, rb)
            qb = q_scr[pl.ds(_aligned(c0 + rb * RB, RB), RB), hc]
            qks.append(_dot(qb, kt_scr[hc, pl.ds(_aligned(c0 + col_lo, RB), col_hi - col_lo)]))
        p["qk", h] = qks

    def mlstm_head(d, p, h):
        c0, mx, mx_last = p["c0"], p["mx"], p["mx_last"]
        rows = pl.ds(c0, L)
        hc = slice(h * HEAD_DIM, (h + 1) * HEAD_DIM)
        ac = slice(2 * h * HEAD_DIM, (2 * h + 2) * HEAD_DIM)
        grow = grow_scr[N_HEADS * d + h:N_HEADS * d + h + 1, rows]
        if has_state:
            cn_dh = cn_scr[d, h]
            cn_bf = cn_dh.astype(BF16)
        for rb in range(L // RB):
            col_lo, col_hi = block_cols(d, rb)
            ncols = col_hi - col_lo
            trow = slice(rb * RB, (rb + 1) * RB)
            rrows = pl.ds(_aligned(c0 + rb * RB, RB), RB)
            crows = pl.ds(_aligned(c0 + col_lo, RB), ncols)
            qb = q_scr[rrows, hc]
            qk = p["qk", h][rb]
            w = jnp.exp(grow[:, col_lo:col_hi] - mx[trow, h:h + 1])
            s = (qk * jnp.where(keep_mask(d, rb, col_lo, ncols), w, 0.0)).astype(BF16)
            na = _dot(s, vaug_scr[crows, ac])
            if has_state:
                na = na + p["w_inter"][trow, h:h + 1] * _dot(qb, cn_bf)
            hval = na[:, 0:HEAD_DIM] / jnp.maximum(jnp.abs(na[:, HEAD_DIM:]), p["e_negm"][trow, h:h + 1])
            if d == 0:
                h_scr[rrows, hc] = hval
            else:
                h_scr[rrows, hc] += hval
        w_state = jnp.exp(grow - mx_last[:, h:h + 1])
        kw_t = (kt_scr[hc, rows].astype(F32) * w_state).astype(BF16)
        upd = _dot(kw_t, vaug_scr[rows, ac])
        if has_state:
            cn_scr[d, h] = p["decay"][:, h:h + 1] * cn_dh + upd
        else:
            cout_ref[d, h] = upd[:, 0:HEAD_DIM]
            nout_ref[d, h:h + 1, :] = upd[:, HEAD_DIM:].T[0:1, :]

    def merge_ab(c, nj):
        nc = slice(nj * NW, (nj + 1) * NW)
        ub = u_scr[tile_rows(c), :]
        acc = None
        for r, y_scr in enumerate((ya_scr, yb_scr)):
            gate = jax.nn.sigmoid(_dot(ub, wbrg_ref[:, r * D_MODEL + nj * NW:r * D_MODEL + (nj + 1) * NW]))
            term = gate * _dot(y_scr[tile_rows(c), :], wbranch_ref[r, :, nc])
            acc = term if acc is None else acc + term
        mab_scr[tile_rows(c), nc] = acc

    def gate_c(c, nj):
        w = wbrg_ref[:, 2 * D_MODEL + nj * NW:2 * D_MODEL + (nj + 1) * NW]
        gc_scr[:, nj * NW:(nj + 1) * NW] = jax.nn.sigmoid(_dot(u_scr[tile_rows(c), :], w))

    def tail_yc(c):
        gml = gml_ref[...]
        for h in range(N_HEADS):
            hc = slice(h * HEAD_DIM, (h + 1) * HEAD_DIM)
            hh = h_scr[tile_rows(c), hc]
            hn = hh * lax.rsqrt(jnp.mean(hh * hh, axis=-1, keepdims=True) + EPS) * gml[:, hc]
            yc_scr[:, hc] = (jax.nn.sigmoid(o_scr[tile_rows(c), hc]) * hn).astype(BF16)

    def tail_merge(c, nj):
        nc = slice(nj * NW, (nj + 1) * NW)
        acc = mab_scr[tile_rows(c), nc] + gc_scr[:, nc] * _dot(yc_scr[...], wbranch_ref[2, :, nc])
        merged_scr[:, nc] = acc.astype(BF16)

    def tail_out(c):
        gt1 = mod_ref[...][:, 2 * D_MODEL:3 * D_MODEL]
        out_ref[tile_rows(c), :] = x_ref[tile_rows(c), :] + gt1 * _dot(merged_scr[...], wout_ref[...])

    def tail_parts(c):
        return ([functools.partial(tail_yc, c)] + [functools.partial(tail_merge, c, nj) for nj in range(n_merge)]
                + [functools.partial(tail_out, c)])

    def scan_parts(d, c, m_in, fillers, out):
        p = {}

        def prepare():
            p.update(mlstm_prepare(d, c, m_in))
            out[d] = p["m_new"]

        parts = [prepare]
        for h in range(N_HEADS):
            parts.append(functools.partial(mlstm_scores, d, p, h))
            if h < len(fillers):
                parts.append(fillers[h])
            parts.append(functools.partial(mlstm_head, d, p, h))
        return parts + list(fillers[N_HEADS:])

    if n_tiles == 1 and n_chunks == 1:
        zeros = jnp.zeros((1, N_HEADS), F32)
        m_out = {}
        stage_a = [functools.partial(part, 0) for part in a_parts] + pool_parts
        stage_b = scan_parts(0, 0, zeros, [functools.partial(merge_ab, 0, nj) for nj in range(n_merge)], m_out)
        stage_b += scan_parts(1, 0, zeros, [functools.partial(gate_c, 0, nj) for nj in range(n_merge)], m_out)
        stage_b += tail_parts(0)

        def write_m():
            mout_ref[0:1, :] = m_out[0]
            mout_ref[1:2, :] = m_out[1]

        return stage_a, stage_b + [write_m]

    def run_loops():
        def phase_a(i, carry):
            for part in a_parts:
                part(i)
            return carry

        lax.fori_loop(0, n_tiles, phase_a, 0)
        for part in pool_parts:
            part()
        cn_scr[...] = cn0_ref[...]
        m_init = m0_ref[...]

        def fwd_step(j, m_in):
            m_out = {}
            for part in scan_parts(0, j, m_in, [functools.partial(merge_ab, j, nj) for nj in range(n_merge)], m_out):
                part()
            return m_out[0]

        def bwd_step(j, m_in):
            c = n_chunks - 1 - j
            m_out = {}
            for part in scan_parts(1, c, m_in, [functools.partial(gate_c, c, nj) for nj in range(n_merge)], m_out):
                part()
            for part in tail_parts(c):
                part()
            return m_out[1]

        lax.fori_loop(0, n_chunks, fwd_step, m_init[0:1, :])
        lax.fori_loop(0, n_chunks, bwd_step, m_init[1:2, :])

    assert has_state, "multi-chunk sequences carry the cached state"
    return [run_loops], []


def _mixer_call(l, xall, lay, prm, stream, states=None, state_outs=None):
    ctx = stream == "ctx"
    T = SEQ if ctx else DEC_SEQ
    assert T % TILE_M == 0 and TILE_M == MLSTM_L and MLSTM_L % MLSTM_ROW_BLOCK == 0
    n_e = CTX_SEQS_PER_STEP if ctx else 1
    n_b = (BATCH if ctx else DEC_BATCH) // n_e
    blk0 = 0 if ctx else N_TOK_CTX // T
    once = pl.Buffered(1)

    def wspec(shape):
        nd = len(shape)
        return pl.BlockSpec((None,) + shape, lambda b: (l,) + (0,) * nd, pipeline_mode=once)

    def lspec(shape):
        return pl.BlockSpec(shape, lambda b: (0,) * len(shape), pipeline_mode=once)

    x_spec = pl.BlockSpec((n_e * T, D_MODEL), lambda b: (blk0 + b, 0), pipeline_mode=None if ctx else once)
    if ctx:
        mod_spec = pl.BlockSpec((None, 1, 6 * D_MODEL), lambda b: (0, 0, 0))
    else:
        mod_spec = pl.BlockSpec((None, 1, 6 * D_MODEL), lambda b: (1 + b, 0, 0))
    in_specs = [
        x_spec, mod_spec,
        wspec((1, D_MODEL)),
        wspec((D_MODEL, N_MAIN)),
        wspec((D_BRANCH, D_MODEL)),
        wspec((D_MODEL, GATE_PAD)),
        wspec((D_MODEL, N_BRANCH * D_MODEL)),
        lspec((N_BRANCH, D_BRANCH, D_MODEL)),
        lspec((D_MODEL, D_MODEL)),
        wspec((N_GROUPS, GROUP, GROUP)),
        wspec((1, D_BRANCH)),
        wspec((1, D_BRANCH)),
        wspec((N_GROUPS, SGU_CHUNK, SGU_CHUNK)),
        wspec((SGU_CHUNK, D_BRANCH)),
        wspec((N_GATE, 1)),
        wspec((1, D_BRANCH)),
    ]
    args = [xall, lay["mods"].reshape(COND_ROWS, 1, 6 * D_MODEL), prm["g_norm1"], prm["w_in16"], prm["w_kt"],
            prm["w_gate"], prm["w_brg"], lay["w_branch"].reshape(N_BRANCH, D_BRANCH, D_MODEL), lay["w_out"],
            prm["w_pool"], prm["pool_scale"], prm["g_sgu"], prm["w_sgu"],
            prm["b_sgu_tile"], prm["b_gates"], prm["g_mlstm"]]
    out_shape = [jax.ShapeDtypeStruct(xall.shape, xall.dtype)]
    out_specs = [x_spec]
    aliases = {0: 0}
    passthrough = []
    if ctx:
        out_shape += [jax.ShapeDtypeStruct(a.shape, a.dtype) for a in state_outs]
        out_specs += [
            pl.BlockSpec((n_e, None, 2, N_HEADS, HEAD_DIM, HEAD_DIM), lambda b: (b, l, 0, 0, 0, 0)),
            pl.BlockSpec((n_e, None, 2, N_HEADS, HEAD_DIM), lambda b: (b, l, 0, 0, 0)),
            pl.BlockSpec((n_e, None, 2, N_HEADS), lambda b: (b, l, 0, 0)),
        ]
        passthrough += [(a, 1 + k) for k, a in enumerate(state_outs)]
    else:
        state_cn, state_m = states
        in_specs += [
            pl.BlockSpec((None, None, 2, N_HEADS, HEAD_DIM, 2 * HEAD_DIM), lambda b: (b, l, 0, 0, 0, 0)),
            pl.BlockSpec((None, None, 2, N_HEADS), lambda b: (b, l, 0, 0)),
        ]
        args += [state_cn, state_m]
    for buf, out_idx in passthrough:
        aliases[len(args)] = out_idx
        in_specs.append(pl.BlockSpec(memory_space=pl.ANY))
        args.append(buf)

    scratch = [
        pltpu.VMEM((T, D_MODEL), BF16),
        pltpu.VMEM((T + 2 * POOL_PAD, D_BRANCH), F32),
        pltpu.VMEM((TILE_M, D_BRANCH), F32),
        pltpu.VMEM((TILE_M, D_BRANCH), BF16),
        pltpu.VMEM((T, D_BRANCH), BF16),
        pltpu.VMEM((D_BRANCH, T), BF16),
        pltpu.VMEM((T, 2 * D_BRANCH), BF16),
        pltpu.VMEM((T, D_BRANCH), F32),
        pltpu.VMEM((2 * N_HEADS, T), F32),
        pltpu.VMEM((T, GATE_PAD), F32),
        pltpu.VMEM((T, D_BRANCH), F32),
        pltpu.VMEM((T, D_BRANCH), BF16),
        pltpu.VMEM((T, D_BRANCH), BF16),
        pltpu.VMEM((TILE_M, D_BRANCH), BF16),
        pltpu.VMEM((T, D_MODEL), F32),
        pltpu.VMEM((TILE_M, D_MODEL), F32),
        pltpu.VMEM((TILE_M, D_MODEL), BF16),
    ]
    if not ctx:
        scratch += [pltpu.VMEM((2, N_HEADS, HEAD_DIM, 2 * HEAD_DIM), F32)]
        scratch += [pltpu.VMEM((T, D_BRANCH), F32), pltpu.VMEM((T, GROUP), F32)]
    scratch = [pltpu.VMEM((n_e,) + tuple(s.shape), s.dtype) for s in scratch]

    outs = pl.pallas_call(
        functools.partial(_mixer_kernel, T=T, grid2d=not ctx, has_state=not ctx, n_elems=n_e,
                          n_in=len(in_specs), n_pass=len(passthrough), n_out=len(out_specs)),
        grid=(n_b,),
        in_specs=in_specs,
        out_specs=out_specs,
        out_shape=out_shape,
        scratch_shapes=scratch,
        input_output_aliases=aliases,
        compiler_params=pltpu.CompilerParams(
            dimension_semantics=("arbitrary",), vmem_limit_bytes=VMEM_LIMIT_BYTES),
        name="mixer_ctx" if ctx else "mixer_lat",
    )(*args)
    return outs


def _ffn_kernel(x_ref, mod_ref, g2_ref, w1_hbm, w2_hbm, gfin_ref, *rest, layer, final):
    if final:
        out_refs, rest = rest[:2], rest[2:]
    else:
        (cond_ref, wada_ref, bada_ref, wbr_ref, wo_ref), rest = rest[:5], rest[5:]
        out_refs, (mods_out, wbr_out, wo_out), rest = rest[:1], rest[1:4], rest[4:]
    acc_scr, h_scr, w1_scr, w2_scr, st1_scr, st2_scr, sem = rest
    n_chunks = D_FF // FFN_CHUNK
    first_tile = pl.program_id(0) == 0

    def w1_copy(c):
        slot = c % FFN_STAGES
        c0 = _aligned(c * FFN_CHUNK, FFN_CHUNK)
        return pltpu.make_async_copy(w1_hbm.at[layer, :, pl.ds(c0, FFN_CHUNK)], st1_scr.at[slot], sem.at[0, slot])

    def w2_copy(c):
        slot = c % FFN_STAGES
        c0 = _aligned(c * FFN_CHUNK, FFN_CHUNK)
        return pltpu.make_async_copy(w2_hbm.at[layer, pl.ds(c0, FFN_CHUNK), :], st2_scr.at[slot], sem.at[1, slot])

    def land_w1(c):
        w1_copy(c).wait()
        c0 = _aligned(c * FFN_CHUNK, FFN_CHUNK)
        w1_scr[:, pl.ds(c0, FFN_CHUNK)] = st1_scr[c % FFN_STAGES].astype(BF16)

    def land_w2(c):
        w2_copy(c).wait()
        c0 = _aligned(c * FFN_CHUNK, FFN_CHUNK)
        w2_scr[pl.ds(c0, FFN_CHUNK), :] = st2_scr[c % FFN_STAGES].astype(BF16)

    def prepare_next_layer():
        if not final:
            _ada_block(cond_ref, wada_ref, bada_ref, mods_out)
            wbr_out[...] = wbr_ref[...].astype(BF16)
            wo_out[...] = wo_ref[...].astype(BF16)

    def normed_input():
        mod = mod_ref[...]
        sh2 = mod[:, 3 * D_MODEL:4 * D_MODEL]
        sc2 = mod[:, 4 * D_MODEL:5 * D_MODEL]
        x = x_ref[...]
        ms = jnp.mean(x * x, axis=-1, keepdims=True)
        return (x * lax.rsqrt(ms + EPS) * g2_ref[...] * (1.0 + sc2) + sh2).astype(BF16)

    def hidden(ub, c):
        c0 = _aligned(c * FFN_CHUNK, FFN_CHUNK)
        h = _dot(ub, w1_scr[:, pl.ds(c0, FFN_CHUNK)])
        h_scr[:, pl.ds(c0, FFN_CHUNK)] = jnp.square(jnp.maximum(h, 0.0)).astype(BF16)

    def finish(ffn_out):
        gt2 = mod_ref[...][:, 5 * D_MODEL:6 * D_MODEL]
        y = x_ref[...] + gt2 * ffn_out
        if not final:
            out_refs[0][...] = y
            return
        y = y * lax.rsqrt(jnp.mean(y * y, axis=-1, keepdims=True) + EPS) * gfin_ref[...]
        in_ctx = pl.program_id(0) < N_TOK_CTX // FFN_TILE_M

        @pl.when(in_ctx)
        def _():
            out_refs[0][...] = y

        @pl.when(jnp.logical_not(in_ctx))
        def _():
            out_refs[1][...] = y

    @pl.when(first_tile)
    def _():
        for c in range(FFN_STAGES):
            w1_copy(c).start()
            w2_copy(c).start()
        prepare_next_layer()
        ub = normed_input()
        acc_scr[...] = jnp.zeros_like(acc_scr)

        def chunk(c, carry):
            land_w1(c)
            land_w2(c)

            @pl.when(c + FFN_STAGES < n_chunks)
            def _():
                w1_copy(c + FFN_STAGES).start()
                w2_copy(c + FFN_STAGES).start()

            hidden(ub, c)
            c0 = _aligned(c * FFN_CHUNK, FFN_CHUNK)
            acc_scr[...] += _dot(h_scr[:, pl.ds(c0, FFN_CHUNK)], w2_scr[pl.ds(c0, FFN_CHUNK), :])
            return carry

        lax.fori_loop(0, n_chunks, chunk, 0)
        finish(acc_scr[...])

    @pl.when(jnp.logical_not(first_tile))
    def _():
        prepare_next_layer()
        ub = normed_input()
        for c in range(n_chunks):
            hidden(ub, c)
        finish(_dot(h_scr[...], w2_scr[...]))


def _ffn_call(l, xall, lay, prm, g_final, final):
    n_tiles = N_TOK // FFN_TILE_M
    ctx_tiles = N_TOK_CTX // FFN_TILE_M
    lat_tiles = DEC_SEQ // FFN_TILE_M
    once = pl.Buffered(1)

    def mod_map(t):
        return (jnp.where(t < ctx_tiles, 0, 1 + (t - ctx_tiles) // lat_tiles), 0, 0)

    x_spec = pl.BlockSpec((FFN_TILE_M, D_MODEL), lambda t: (t, 0))
    in_specs = [
        x_spec,
        pl.BlockSpec((None, 1, 6 * D_MODEL), mod_map),
        pl.BlockSpec((None, 1, D_MODEL), lambda t: (l, 0, 0), pipeline_mode=once),
        pl.BlockSpec(memory_space=pl.ANY),
        pl.BlockSpec(memory_space=pl.ANY),
        pl.BlockSpec((1, D_MODEL), lambda t: (0, 0)),
    ]
    args = [xall, lay["mods"].reshape(COND_ROWS, 1, 6 * D_MODEL), prm["g_norm2"], prm["w_ff1"], prm["w_ff2"],
            g_final.reshape(1, D_MODEL)]
    if final:
        out_specs = [
            pl.BlockSpec((FFN_TILE_M, D_MODEL), lambda t: (jnp.minimum(t, ctx_tiles - 1), 0)),
            pl.BlockSpec((FFN_TILE_M, D_MODEL), lambda t: (jnp.maximum(t - ctx_tiles, 0), 0)),
        ]
        out_shape = [jax.ShapeDtypeStruct((N_TOK_CTX, D_MODEL), F32),
                     jax.ShapeDtypeStruct((N_TOK - N_TOK_CTX, D_MODEL), F32)]
        aliases = {}
    else:
        ada_cols = 6 * D_MODEL // n_tiles
        br_rows = N_BRANCH * D_BRANCH // n_tiles
        wo_tiles = D_MODEL // br_rows
        assert ada_cols % 128 == 0 and br_rows % 16 == 0 and wo_tiles <= n_tiles
        wo_map = lambda t: (jnp.minimum(t, wo_tiles - 1), 0)
        in_specs += [
            pl.BlockSpec((COND_ROWS, D_MODEL), lambda t: (0, 0)),
            pl.BlockSpec((None, D_MODEL, ada_cols), lambda t: (l + 1, 0, t)),
            pl.BlockSpec((None, 1, ada_cols), lambda t: (l + 1, 0, t)),
            pl.BlockSpec((None, br_rows, D_MODEL), lambda t: (l + 1, t, 0)),
            pl.BlockSpec((None, br_rows, D_MODEL), lambda t: (l + 1,) + wo_map(t)),
        ]
        args += [prm["cond8"], prm["w_ada"], prm["b_ada3"], prm["w_branch_rows"], prm["w_out"]]
        out_specs = [
            x_spec,
            pl.BlockSpec((COND_ROWS, ada_cols), lambda t: (0, t)),
            pl.BlockSpec((br_rows, D_MODEL), lambda t: (t, 0)),
            pl.BlockSpec((br_rows, D_MODEL), wo_map),
        ]
        out_shape = [jax.ShapeDtypeStruct(xall.shape, xall.dtype),
                     jax.ShapeDtypeStruct((COND_ROWS, 6 * D_MODEL), F32),
                     jax.ShapeDtypeStruct((N_BRANCH * D_BRANCH, D_MODEL), BF16),
                     jax.ShapeDtypeStruct((D_MODEL, D_MODEL), BF16)]
        aliases = {0: 0}
    return pl.pallas_call(
        functools.partial(_ffn_kernel, layer=l, final=final),
        grid=(n_tiles,),
        in_specs=in_specs,
        out_specs=out_specs,
        out_shape=out_shape,
        scratch_shapes=[
            pltpu.VMEM((FFN_TILE_M, D_MODEL), F32),
            pltpu.VMEM((FFN_TILE_M, D_FF), BF16),
            pltpu.VMEM((D_MODEL, D_FF), BF16),
            pltpu.VMEM((D_FF, D_MODEL), BF16),
            pltpu.VMEM((FFN_STAGES, D_MODEL, FFN_CHUNK), F32),
            pltpu.VMEM((FFN_STAGES, FFN_CHUNK, D_MODEL), F32),
            pltpu.SemaphoreType.DMA((2, FFN_STAGES)),
        ],
        input_output_aliases=aliases,
        compiler_params=pltpu.CompilerParams(
            dimension_semantics=("arbitrary",), vmem_limit_bytes=VMEM_LIMIT_BYTES),
        name="ffn_final" if final else "ffn",
    )(*args)


def kernel(x_prompt, x_sample, state_C, state_n, state_m, c, c_ctx, w_ada, b_ada, g_norm1, g_norm2, w_in,
           b_gates, w_pool, pool_scale, g_sgu, w_sgu, b_sgu, g_mlstm, w_branch, w_out, w_ff1, w_ff2, g_final):
    w_in16 = w_in.astype(BF16)
    prm = {
        "w_in16": w_in16,
        "w_gate": jnp.pad(w_in16[:, :, N_MAIN:N_MAIN + N_GATE], ((0, 0), (0, 0), (0, GATE_PAD - N_GATE))),
        "w_brg": w_in16[:, :, N_MAIN + N_GATE:],
        "w_kt": jnp.swapaxes(w_in16[:, :, 4 * D_BRANCH:5 * D_BRANCH], 1, 2),
        "w_branch_rows": w_branch.reshape(DEPTH, N_BRANCH * D_BRANCH, D_MODEL),
        "w_out": w_out,
        "w_ada": w_ada,
        "b_ada3": b_ada.reshape(DEPTH, 1, 6 * D_MODEL),
        "w_pool": w_pool.astype(BF16),
        "w_sgu": w_sgu.astype(BF16),
        "w_ff1": w_ff1,
        "w_ff2": w_ff2,
        "g_norm1": g_norm1.reshape(DEPTH, 1, D_MODEL),
        "g_norm2": g_norm2.reshape(DEPTH, 1, D_MODEL),
        "pool_scale": pool_scale.reshape(DEPTH, 1, D_BRANCH),
        "g_sgu": g_sgu.reshape(DEPTH, 1, D_BRANCH),
        "g_mlstm": g_mlstm.reshape(DEPTH, 1, D_BRANCH),
        "b_gates": b_gates.reshape(DEPTH, N_GATE, 1),
        "b_sgu_tile": jnp.repeat(jnp.swapaxes(b_sgu, 1, 2), GROUP, axis=2),
    }
    prm["cond8"] = jnp.zeros((COND_ROWS, D_MODEL), F32).at[0].set(c_ctx).at[1:1 + DEC_BATCH].set(c)
    lay = {
        "mods": _ada_call(prm["cond8"], w_ada, prm["b_ada3"]),
        "w_branch": prm["w_branch_rows"][0].astype(BF16),
        "w_out": w_out[0].astype(BF16),
    }

    state_cn = jnp.concatenate(
        [state_C, jnp.broadcast_to(state_n[..., None], state_n.shape + (HEAD_DIM,))], axis=-1)
    xall = jnp.concatenate([x_prompt.reshape(N_TOK_CTX, D_MODEL), x_sample.reshape(-1, D_MODEL)], axis=0)
    new_states = [jnp.zeros((BATCH, DEPTH, 2, N_HEADS, HEAD_DIM, HEAD_DIM), F32),
                  jnp.zeros((BATCH, DEPTH, 2, N_HEADS, HEAD_DIM), F32),
                  jnp.zeros((BATCH, DEPTH, 2, N_HEADS), F32)]
    for l in range(DEPTH):
        xall, *new_states = _mixer_call(l, xall, lay, prm, "ctx", state_outs=new_states)
        (xall,) = _mixer_call(l, xall, lay, prm, "lat", states=(state_cn, state_m))
        if l < DEPTH - 1:
            xall, *nxt = _ffn_call(l, xall, lay, prm, g_final, final=False)
            lay = dict(zip(("mods", "w_branch", "w_out"), nxt))
    y_prompt, y_sample = _ffn_call(DEPTH - 1, xall, lay, prm, g_final, final=True)
    return (y_prompt.reshape(BATCH, SEQ, D_MODEL), y_sample.reshape(DEC_BATCH, DEC_SEQ, D_MODEL), *new_states)
```

```python
import functools

import jax
import jax.numpy as jnp
from jax import lax
from jax.experimental import pallas as pl
from jax.experimental.pallas import tpu as pltpu

F32 = jnp.float32
BF16 = jnp.bfloat16

D_MODEL = 1024
DEPTH = 4
BATCH = 16
SEQ = 256
DEC_BATCH = 2
DEC_SEQ = 1024
GRID_W = 64
D_BRANCH = 512
POOL_WINDOWS = (2, 4, 8, 16)
N_GROUPS = 4
GROUP = D_BRANCH // N_GROUPS
SGU_CHUNK = 128
N_HEADS = 4
HEAD_DIM = D_BRANCH // N_HEADS
N_BRANCH = 3
D_FF = 4 * D_MODEL
EPS = 1e-6

N_MAIN = 7 * D_BRANCH
N_GATE = 4 * N_HEADS
GATE_PAD = 128
N_TOK_CTX = BATCH * SEQ
N_TOK = N_TOK_CTX + DEC_BATCH * DEC_SEQ
COND_ROWS = 8

TILE_M = 256
CTX_SEQS_PER_STEP = 2
MLSTM_L = 256
MLSTM_ROW_BLOCK = 128
MERGE_COLS = 256
POOL_PAD = 8
FFN_TILE_M = 512
FFN_CHUNK = 512
FFN_STAGES = 2
ADA_TILE_N = 1536
VMEM_LIMIT_BYTES = 60 * 1024 * 1024


_dot = functools.partial(jnp.dot, preferred_element_type=F32)
_dot_nt = functools.partial(lax.dot_general, dimension_numbers=(((1,), (1,)), ((), ())),
                            preferred_element_type=F32)


def _aligned(offset, align):
    return offset if isinstance(offset, int) else pl.multiple_of(offset, align)


def _ada_block(cond_ref, w_ref, b_ref, out_ref, *, layer=0):
    cond = cond_ref[...]
    s = (cond * jax.nn.sigmoid(cond)).astype(BF16)
    out_ref[...] = _dot(s, w_ref[...].astype(BF16)) + b_ref[layer:layer + 1, :]


def _ada_call(cond8, w_ada, b_ada):
    n_out = 6 * D_MODEL
    return pl.pallas_call(
        _ada_block,
        grid=(n_out // ADA_TILE_N,),
        in_specs=[
            pl.BlockSpec((COND_ROWS, D_MODEL), lambda j: (0, 0)),
            pl.BlockSpec((None, D_MODEL, ADA_TILE_N), lambda j: (0, 0, j)),
            pl.BlockSpec((DEPTH, ADA_TILE_N), lambda j: (0, j)),
        ],
        out_specs=pl.BlockSpec((COND_ROWS, ADA_TILE_N), lambda j: (0, j)),
        out_shape=jax.ShapeDtypeStruct((COND_ROWS, n_out), F32),
        compiler_params=pltpu.CompilerParams(
            dimension_semantics=("arbitrary",), vmem_limit_bytes=VMEM_LIMIT_BYTES),
        name="ada_mod",
    )(cond8, w_ada, b_ada)


def _seg_scan(x, pos, seg, total, op, reverse):
    d = 1
    while d < seg:
        if reverse:
            shifted = pltpu.roll(x, total - d, 1)
            valid = pos < seg - d
        else:
            shifted = pltpu.roll(x, d, 1)
            valid = pos >= d
        if op == "sum":
            x = x + jnp.where(valid, shifted, 0.0)
        else:
            x = jnp.maximum(x, jnp.where(valid, shifted, -jnp.inf))
        d *= 2
    return x


_MIXER_LAYER_ROW_INPUTS = (2, 10, 11, 14, 15)


def _interleave(first, second):
    out, i, j = [], 0, 0
    while i < len(first) or j < len(second):
        if j >= len(second) or (i < len(first) and i * len(second) <= j * len(first)):
            out.append(first[i])
            i += 1
        else:
            out.append(second[j])
            j += 1
    return out


def _mixer_kernel(*refs, T, grid2d, has_state, layer, n_elems, n_in, n_pass, n_out):
    ins, outs, scr = list(refs[:n_in - n_pass]), refs[n_in:n_in + n_out], refs[n_in + n_out:]
    ins[1] = ins[1].at[pl.ds(1 + pl.program_id(0) if has_state else 0, 1), :]
    for k in _MIXER_LAYER_ROW_INPUTS:
        ins[k] = ins[k].at[pl.ds(layer, 1), :]
    stages = []
    for e in range(n_elems):
        rows = slice(e * T, (e + 1) * T)
        ins_e = [ins[0].at[rows, :]] + list(ins[1:])
        outs_e = [outs[0].at[rows, :]] + [o.at[e] for o in outs[1:]]
        scr_e = [s.at[e] for s in scr]
        stages.append(_mixer_sequence(*ins_e, *outs_e, *scr_e, T=T, grid2d=grid2d, has_state=has_state))
    order = list(stages[0][0])
    for e in range(n_elems):
        order += _interleave(stages[e][1], stages[e + 1][0] if e + 1 < n_elems else [])
    for thunk in order:
        thunk()


def _mixer_sequence(*refs, T, grid2d, has_state):
    it = iter(refs)
    x_ref = next(it)
    mod_ref = next(it)
    g1_ref = next(it)
    wmain_ref = next(it)
    wkt_ref = next(it)
    wgate_ref = next(it)
    wbrg_ref = next(it)
    wbranch_ref = next(it)
    wout_ref = next(it)
    wpool_ref = next(it)
    pscale_ref = next(it)
    gsgu_ref = next(it)
    wsgu_ref = next(it)
    bsgu_ref = next(it)
    bgates_ref = next(it)
    gml_ref = next(it)
    if has_state:
        cn0_ref, m0_ref = next(it), next(it)
    out_ref = next(it)
    if not has_state:
        cout_ref, nout_ref, mout_ref = next(it), next(it), next(it)
    (u_scr, xp_scr, su_scr, vn_scr, q_scr, kt_scr, vaug_scr, o_scr, grow_scr, colf_scr,
     h_scr, ya_scr, yb_scr, yc_scr, mab_scr, gc_scr, merged_scr) = [next(it) for _ in range(17)]
    if has_state:
        cn_scr = next(it)
    if grid2d:
        m1_scr, dlt_scr = next(it), next(it)

    L = MLSTM_L
    n_tiles = T // TILE_M
    n_chunks = T // L
    RB = MLSTM_ROW_BLOCK
    NW = MERGE_COLS
    n_merge = D_MODEL // NW

    def tile_rows(i):
        return pl.ds(_aligned(i * TILE_M, TILE_M), TILE_M)

    def a_norm(i):
        mod = mod_ref[...]
        sh1 = mod[:, 0:D_MODEL]
        sc1 = mod[:, D_MODEL:2 * D_MODEL]
        x = x_ref[tile_rows(i), :]
        ms = jnp.mean(x * x, axis=-1, keepdims=True)
        u = x * lax.rsqrt(ms + EPS) * g1_ref[...] * (1.0 + sc1) + sh1
        u_scr[tile_rows(i), :] = u.astype(BF16)

    def a_gates(i):
        lane = lax.broadcasted_iota(jnp.int32, (2 * N_HEADS, TILE_M), 1)
        is_fwd = lax.broadcasted_iota(jnp.int32, (2 * N_HEADS, TILE_M), 0) < N_HEADS
        bias = jnp.concatenate([bgates_ref[...], jnp.zeros((1, GATE_PAD - N_GATE), F32)], axis=1)
        gate_t = (_dot(u_scr[tile_rows(i), :], wgate_ref[...]) + bias).T
        gp = gate_t[0:N_GATE, :]
        i_all = gp[0:2 * N_HEADS, :]
        f_all = gp[2 * N_HEADS:4 * N_HEADS, :]
        logf = jnp.minimum(f_all, 0.0) - jnp.log1p(jnp.exp(-jnp.abs(f_all)))
        b_row = jnp.where(is_fwd, _seg_scan(logf, lane, L, L, "sum", False),
                          _seg_scan(logf, lane, L, L, "sum", True))
        g_row = i_all - b_row
        cm_row = jnp.where(is_fwd, _seg_scan(g_row, lane, L, L, "max", False),
                           _seg_scan(g_row, lane, L, L, "max", True))
        grow_scr[:, tile_rows(i)] = g_row
        stacked = jnp.concatenate(
            [b_row, cm_row, jnp.zeros((GATE_PAD - 4 * N_HEADS, TILE_M), F32)], axis=0)
        colf_scr[tile_rows(i), :] = stacked.T

    def proj(i, k):
        return _dot(u_scr[tile_rows(i), :], wmain_ref[:, k * D_BRANCH:(k + 1) * D_BRANCH])

    def a_xp(i):
        xp_scr[pl.ds(_aligned(i * TILE_M + POOL_PAD, 8), TILE_M), :] = proj(i, 0)

    def a_su(i):
        su_scr[...] = proj(i, 1)

    def a_sv(i):
        sv = proj(i, 2)
        vn = sv * lax.rsqrt(jnp.mean(sv * sv, axis=-1, keepdims=True) + EPS) * gsgu_ref[...]
        vn_scr[...] = vn.astype(BF16)

    def a_q(i):
        q_scr[tile_rows(i), :] = proj(i, 3).astype(BF16)

    def a_k(i):
        kt = _dot_nt(wkt_ref[...], u_scr[tile_rows(i), :])
        kt_scr[:, tile_rows(i)] = (kt * (HEAD_DIM ** -0.5)).astype(BF16)

    def a_v(i):
        v = proj(i, 5).astype(BF16)
        ones_blk = jnp.ones((TILE_M, HEAD_DIM), BF16)
        for h in range(N_HEADS):
            vaug_scr[tile_rows(i), 2 * h * HEAD_DIM:(2 * h + 1) * HEAD_DIM] = v[:, h * HEAD_DIM:(h + 1) * HEAD_DIM]
            vaug_scr[tile_rows(i), (2 * h + 1) * HEAD_DIM:(2 * h + 2) * HEAD_DIM] = ones_blk

    def a_o(i):
        o_scr[tile_rows(i), :] = proj(i, 6)

    def a_sgu(ch, i):
        crow = slice(ch * SGU_CHUNK, (ch + 1) * SGU_CHUNK)
        for g in range(N_GROUPS):
            gc = slice(g * GROUP, (g + 1) * GROUP)
            mixed = _dot(wsgu_ref[g], vn_scr[crow, gc]) + bsgu_ref[:, gc]
            yb = su_scr[crow, gc] * mixed
            yb_scr[pl.ds(_aligned(i * TILE_M + ch * SGU_CHUNK, SGU_CHUNK), SGU_CHUNK), gc] = yb.astype(BF16)

    a_parts = [a_norm, a_gates, a_xp, a_su, a_sv, a_q, a_k, a_v, a_o]
    a_parts += [functools.partial(a_sgu, ch) for ch in range(TILE_M // SGU_CHUNK)]

    def pool_pad():
        zpad = jnp.zeros((POOL_PAD, D_BRANCH), F32)
        xp_scr[0:POOL_PAD, :] = zpad
        xp_scr[T + POOL_PAD:T + 2 * POOL_PAD, :] = zpad

    def pool_seq(g):
        win = POOL_WINDOWS[g]
        gc = slice(g * GROUP, (g + 1) * GROUP)
        half = win // 2
        t_idx = lax.broadcasted_iota(jnp.int32, (T, GROUP), 0)
        total = jnp.zeros((T, GROUP), F32)
        for j in range(-half, win - half):
            total = total + xp_scr[POOL_PAD + j:POOL_PAD + j + T, gc]
        cnt = (jnp.minimum(t_idx + (win - half), T) - jnp.maximum(t_idx - half, 0)).astype(F32)
        dlt = total / cnt - xp_scr[POOL_PAD:POOL_PAD + T, gc]
        ya = _dot(dlt.astype(BF16), wpool_ref[g]) * pscale_ref[...][:, gc]
        ya_scr[:, gc] = ya.astype(BF16)

    def a_colpool(i):
        c_idx = lax.broadcasted_iota(jnp.int32, (GRID_W, GROUP), 0)
        for g, win in enumerate(POOL_WINDOWS):
            gc = slice(g * GROUP, (g + 1) * GROUP)
            half = win // 2
            inv_cnt = 1.0 / (jnp.minimum(c_idx + (win - half), GRID_W) - jnp.maximum(c_idx - half, 0)).astype(F32)
            for r in range(TILE_M // GRID_W):
                r0 = i * TILE_M + r * GRID_W
                xg = xp_scr[pl.ds(_aligned(r0 + POOL_PAD, 8), GRID_W), gc]
                total = xg
                for j in range(-half, win - half):
                    if j != 0:
                        valid = (c_idx + j >= 0) & (c_idx + j < GRID_W)
                        total = total + jnp.where(valid, pltpu.roll(xg, (-j) % GRID_W, 0), 0.0)
                m1_scr[pl.ds(_aligned(r0, GRID_W), GRID_W), gc] = total * inv_cnt

    def pool_grid_rows(g):
        win = POOL_WINDOWS[g]
        gc = slice(g * GROUP, (g + 1) * GROUP)
        half = win // 2
        n_rows = T // GRID_W
        for r in range(n_rows):
            lo = max(r - half, 0)
            hi = min(r + (win - half), n_rows)
            acc = m1_scr[lo * GRID_W:(lo + 1) * GRID_W, gc]
            for rr in range(lo + 1, hi):
                acc = acc + m1_scr[rr * GRID_W:(rr + 1) * GRID_W, gc]
            xg = xp_scr[POOL_PAD + r * GRID_W:POOL_PAD + (r + 1) * GRID_W, gc]
            dlt_scr[r * GRID_W:(r + 1) * GRID_W, :] = acc / float(hi - lo) - xg
        ya = _dot(dlt_scr[...].astype(BF16), wpool_ref[g]) * pscale_ref[...][:, gc]
        ya_scr[:, gc] = ya.astype(BF16)

    if grid2d:
        assert TILE_M % GRID_W == 0
        a_parts.insert(a_parts.index(a_xp) + 1, a_colpool)
        pool_parts = [functools.partial(pool_grid_rows, g) for g in range(N_GROUPS)]
    else:
        pool_parts = [pool_pad] + [functools.partial(pool_seq, g) for g in range(N_GROUPS)]

    def keep_mask(d, rb, col_lo, ncols):
        t_loc = rb * RB + lax.broadcasted_iota(jnp.int32, (RB, ncols), 0)
        s_loc = col_lo + lax.broadcasted_iota(jnp.int32, (RB, ncols), 1)
        return s_loc <= t_loc if d == 0 else s_loc >= t_loc

    def mlstm_prepare(d, c, m_in):
        c0 = _aligned(c * L, L)
        colf = colf_scr[pl.ds(c0, L), :]
        bcol = colf[:, N_HEADS * d:N_HEADS * (d + 1)]
        cmcol = colf[:, 2 * N_HEADS + N_HEADS * d:2 * N_HEADS + N_HEADS * (d + 1)]
        mx = jnp.maximum(m_in, cmcol)
        last = L - 1 if d == 0 else 0
        mx_last = mx[last:last + 1, :]
        return dict(c0=c0, mx=mx, mx_last=mx_last, w_inter=jnp.exp(m_in - mx), e_negm=jnp.exp(-(bcol + mx)),
                    decay=jnp.exp(m_in - mx_last), m_new=bcol[last:last + 1, :] + mx_last)

    def block_cols(d, rb):
        return (0, (rb + 1) * RB) if d == 0 else (rb * RB, L)

    def mlstm_scores(d, p, h):
        c0 = p["c0"]
        hc = slice(h * HEAD_DIM, (h + 1) * HEAD_DIM)
        qks = []
        for rb in range(L // RB):
            col_lo, col_hi = block_cols(d, rb)
            qb = q_scr[pl.ds(_aligned(c0 + rb * RB, RB), RB), hc]
            qks.append(_dot(qb, kt_scr[hc, pl.ds(_aligned(c0 + col_lo, RB), col_hi - col_lo)]))
        p["qk", h] = qks

    def mlstm_head(d, p, h):
        c0, mx, mx_last = p["c0"], p["mx"], p["mx_last"]
        rows = pl.ds(c0, L)
        hc = slice(h * HEAD_DIM, (h + 1) * HEAD_DIM)
        ac = slice(2 * h * HEAD_DIM, (2 * h + 2) * HEAD_DIM)
        grow = grow_scr[N_HEADS * d + h:N_HEADS * d + h + 1, rows]
        if has_state:
            cn_dh = cn_scr[d, h]
            cn_bf = cn_dh.astype(BF16)
        for rb in range(L // RB):
            col_lo, col_hi = block_cols(d, rb)
            ncols = col_hi - col_lo
            trow = slice(rb * RB, (rb + 1) * RB)
            rrows = pl.ds(_aligned(c0 + rb * RB, RB), RB)
            crows = pl.ds(_aligned(c0 + col_lo, RB), ncols)
            qb = q_scr[rrows, hc]
            qk = p["qk", h][rb]
            w = jnp.exp(grow[:, col_lo:col_hi] - mx[trow, h:h + 1])
            s = (qk * jnp.where(keep_mask(d, rb, col_lo, ncols), w, 0.0)).astype(BF16)
            na = _dot(s, vaug_scr[crows, ac])
            if has_state:
                na = na + p["w_inter"][trow, h:h + 1] * _dot(qb, cn_bf)
            hval = na[:, 0:HEAD_DIM] / jnp.maximum(jnp.abs(na[:, HEAD_DIM:]), p["e_negm"][trow, h:h + 1])
            if d == 0:
                h_scr[rrows, hc] = hval
            else:
                h_scr[rrows, hc] += hval
        w_state = jnp.exp(grow - mx_last[:, h:h + 1])
        kw_t = (kt_scr[hc, rows].astype(F32) * w_state).astype(BF16)
        upd = _dot(kw_t, vaug_scr[rows, ac])
        if has_state:
            cn_scr[d, h] = p["decay"][:, h:h + 1] * cn_dh + upd
        else:
            cout_ref[d, h] = upd[:, 0:HEAD_DIM]
            nout_ref[d, h:h + 1, :] = upd[:, HEAD_DIM:].T[0:1, :]

    def merge_ab(c, nj):
        nc = slice(nj * NW, (nj + 1) * NW)
        ub = u_scr[tile_rows(c), :]
        acc = None
        for r, y_scr in enumerate((ya_scr, yb_scr)):
            gate = jax.nn.sigmoid(_dot(ub, wbrg_ref[:, r * D_MODEL + nj * NW:r * D_MODEL + (nj + 1) * NW]))
            term = gate * _dot(y_scr[tile_rows(c), :], wbranch_ref[r * D_BRANCH:(r + 1) * D_BRANCH, nc])
            acc = term if acc is None else acc + term
        mab_scr[tile_rows(c), nc] = acc

    def gate_c(c, nj):
        w = wbrg_ref[:, 2 * D_MODEL + nj * NW:2 * D_MODEL + (nj + 1) * NW]
        gc_scr[:, nj * NW:(nj + 1) * NW] = jax.nn.sigmoid(_dot(u_scr[tile_rows(c), :], w))

    def tail_yc(c):
        gml = gml_ref[...]
        for h in range(N_HEADS):
            hc = slice(h * HEAD_DIM, (h + 1) * HEAD_DIM)
            hh = h_scr[tile_rows(c), hc]
            hn = hh * lax.rsqrt(jnp.mean(hh * hh, axis=-1, keepdims=True) + EPS) * gml[:, hc]
            yc_scr[:, hc] = (jax.nn.sigmoid(o_scr[tile_rows(c), hc]) * hn).astype(BF16)

    def tail_merge(c, nj):
        nc = slice(nj * NW, (nj + 1) * NW)
        acc = mab_scr[tile_rows(c), nc] + gc_scr[:, nc] * _dot(yc_scr[...], wbranch_ref[2 * D_BRANCH:, nc])
        merged_scr[:, nc] = acc.astype(BF16)

    def tail_out(c):
        gt1 = mod_ref[...][:, 2 * D_MODEL:3 * D_MODEL]
        out_ref[tile_rows(c), :] = x_ref[tile_rows(c), :] + gt1 * _dot(merged_scr[...], wout_ref[...])

    def tail_parts(c):
        return ([functools.partial(tail_yc, c)] + [functools.partial(tail_merge, c, nj) for nj in range(n_merge)]
                + [functools.partial(tail_out, c)])

    def scan_parts(d, c, m_in, fillers, out):
        p = {}

        def prepare():
            p.update(mlstm_prepare(d, c, m_in))
            out[d] = p["m_new"]

        parts = [prepare]
        for h in range(N_HEADS):
            parts.append(functools.partial(mlstm_scores, d, p, h))
            if h < len(fillers):
                parts.append(fillers[h])
            parts.append(functools.partial(mlstm_head, d, p, h))
        return parts + list(fillers[N_HEADS:])

    if n_tiles == 1 and n_chunks == 1:
        zeros = jnp.zeros((1, N_HEADS), F32)
        m_out = {}
        stage_a = [functools.partial(part, 0) for part in a_parts] + pool_parts
        stage_b = scan_parts(0, 0, zeros, [functools.partial(merge_ab, 0, nj) for nj in range(n_merge)], m_out)
        stage_b += scan_parts(1, 0, zeros, [functools.partial(gate_c, 0, nj) for nj in range(n_merge)], m_out)
        stage_b += tail_parts(0)

        def write_m():
            mout_ref[0:1, :] = m_out[0]
            mout_ref[1:2, :] = m_out[1]

        return stage_a, stage_b + [write_m]

    def run_loops():
        def phase_a(i, carry):
            for part in a_parts:
                part(i)
            return carry

        lax.fori_loop(0, n_tiles, phase_a, 0)
        for part in pool_parts:
            part()
        cn_scr[...] = cn0_ref[...]
        m_init = m0_ref[...]

        def fwd_step(j, m_in):
            m_out = {}
            for part in scan_parts(0, j, m_in, [functools.partial(merge_ab, j, nj) for nj in range(n_merge)], m_out):
                part()
            return m_out[0]

        def bwd_step(j, m_in):
            c = n_chunks - 1 - j
            m_out = {}
            for part in scan_parts(1, c, m_in, [functools.partial(gate_c, c, nj) for nj in range(n_merge)], m_out):
                part()
            for part in tail_parts(c):
                part()
            return m_out[1]

        lax.fori_loop(0, n_chunks, fwd_step, m_init[0:1, :])
        lax.fori_loop(0, n_chunks, bwd_step, m_init[1:2, :])

    assert has_state, "multi-chunk sequences carry the cached state"
    return [run_loops], []


def _mixer_call(l, xall, lay, prm, stream, states=None, state_outs=None):
    ctx = stream == "ctx"
    T = SEQ if ctx else DEC_SEQ
    assert T % TILE_M == 0 and TILE_M == MLSTM_L and MLSTM_L % MLSTM_ROW_BLOCK == 0
    n_e = CTX_SEQS_PER_STEP if ctx else 1
    n_b = (BATCH if ctx else DEC_BATCH) // n_e
    blk0 = 0 if ctx else N_TOK_CTX // T
    once = pl.Buffered(1)

    def wspec(shape):
        nd = len(shape)
        return pl.BlockSpec((None,) + shape, lambda b: (l,) + (0,) * nd, pipeline_mode=once)

    def lspec(shape):
        return pl.BlockSpec(shape, lambda b: (0,) * len(shape), pipeline_mode=once)

    x_spec = pl.BlockSpec((n_e * T, D_MODEL), lambda b: (blk0 + b, 0), pipeline_mode=None if ctx else once)
    in_specs = [
        x_spec,
        lspec((COND_ROWS, 6 * D_MODEL)),
        lspec((DEPTH, D_MODEL)),
        wspec((D_MODEL, N_MAIN)),
        wspec((D_BRANCH, D_MODEL)),
        wspec((D_MODEL, GATE_PAD)),
        wspec((D_MODEL, N_BRANCH * D_MODEL)),
        lspec((N_BRANCH * D_BRANCH, D_MODEL)),
        lspec((D_MODEL, D_MODEL)),
        wspec((N_GROUPS, GROUP, GROUP)),
        lspec((DEPTH, D_BRANCH)),
        lspec((DEPTH, D_BRANCH)),
        wspec((N_GROUPS, SGU_CHUNK, SGU_CHUNK)),
        wspec((SGU_CHUNK, D_BRANCH)),
        lspec((DEPTH, N_GATE)),
        lspec((DEPTH, D_BRANCH)),
    ]
    args = [xall, lay["mods"], prm["g_norm1"], prm["w_in16"], prm["w_kt"],
            prm["w_gate"], prm["w_brg"], lay["w_branch"], lay["w_out"],
            prm["w_pool"], prm["pool_scale"], prm["g_sgu"], prm["w_sgu"],
            prm["b_sgu_tile"], prm["b_gates"], prm["g_mlstm"]]
    out_shape = [jax.ShapeDtypeStruct(xall.shape, xall.dtype)]
    out_specs = [x_spec]
    aliases = {0: 0}
    passthrough = []
    if ctx:
        out_shape += [jax.ShapeDtypeStruct(a.shape, a.dtype) for a in state_outs]
        out_specs += [
            pl.BlockSpec((n_e, None, 2, N_HEADS, HEAD_DIM, HEAD_DIM), lambda b: (b, l, 0, 0, 0, 0)),
            pl.BlockSpec((n_e, None, 2, N_HEADS, HEAD_DIM), lambda b: (b, l, 0, 0, 0)),
            pl.BlockSpec((n_e, None, 2, N_HEADS), lambda b: (b, l, 0, 0)),
        ]
        passthrough += [(a, 1 + k) for k, a in enumerate(state_outs)]
    else:
        state_cn, state_m = states
        in_specs += [
            pl.BlockSpec((None, None, 2, N_HEADS, HEAD_DIM, 2 * HEAD_DIM), lambda b: (b, l, 0, 0, 0, 0)),
            pl.BlockSpec((None, None, 2, N_HEADS), lambda b: (b, l, 0, 0)),
        ]
        args += [state_cn, state_m]
    for buf, out_idx in passthrough:
        aliases[len(args)] = out_idx
        in_specs.append(pl.BlockSpec(memory_space=pl.ANY))
        args.append(buf)

    scratch = [
        pltpu.VMEM((T, D_MODEL), BF16),
        pltpu.VMEM((T + 2 * POOL_PAD, D_BRANCH), F32),
        pltpu.VMEM((TILE_M, D_BRANCH), F32),
        pltpu.VMEM((TILE_M, D_BRANCH), BF16),
        pltpu.VMEM((T, D_BRANCH), BF16),
        pltpu.VMEM((D_BRANCH, T), BF16),
        pltpu.VMEM((T, 2 * D_BRANCH), BF16),
        pltpu.VMEM((T, D_BRANCH), F32),
        pltpu.VMEM((2 * N_HEADS, T), F32),
        pltpu.VMEM((T, GATE_PAD), F32),
        pltpu.VMEM((T, D_BRANCH), F32),
        pltpu.VMEM((T, D_BRANCH), BF16),
        pltpu.VMEM((T, D_BRANCH), BF16),
        pltpu.VMEM((TILE_M, D_BRANCH), BF16),
        pltpu.VMEM((T, D_MODEL), F32),
        pltpu.VMEM((TILE_M, D_MODEL), F32),
        pltpu.VMEM((TILE_M, D_MODEL), BF16),
    ]
    if not ctx:
        scratch += [pltpu.VMEM((2, N_HEADS, HEAD_DIM, 2 * HEAD_DIM), F32)]
        scratch += [pltpu.VMEM((T, D_BRANCH), F32), pltpu.VMEM((T, GROUP), F32)]
    scratch = [pltpu.VMEM((n_e,) + tuple(s.shape), s.dtype) for s in scratch]

    outs = pl.pallas_call(
        functools.partial(_mixer_kernel, T=T, grid2d=not ctx, has_state=not ctx, layer=l, n_elems=n_e,
                          n_in=len(in_specs), n_pass=len(passthrough), n_out=len(out_specs)),
        grid=(n_b,),
        in_specs=in_specs,
        out_specs=out_specs,
        out_shape=out_shape,
        scratch_shapes=scratch,
        input_output_aliases=aliases,
        compiler_params=pltpu.CompilerParams(
            dimension_semantics=("arbitrary",), vmem_limit_bytes=VMEM_LIMIT_BYTES),
        name="mixer_ctx" if ctx else "mixer_lat",
    )(*args)
    return outs


def _ffn_kernel(x_ref, mod_ref, g2_ref, w1_hbm, w2_hbm, gfin_ref, *rest, layer, final):
    if final:
        out_refs, rest = rest[:2], rest[2:]
    else:
        (cond_ref, wada_ref, bada_ref, wbr_ref, wo_ref), rest = rest[:5], rest[5:]
        out_refs, (mods_out, wbr_out, wo_out), rest = rest[:1], rest[1:4], rest[4:]
    acc_scr, h_scr, w1_scr, w2_scr, st1_scr, st2_scr, sem = rest
    n_chunks = D_FF // FFN_CHUNK
    tile = pl.program_id(0)
    first_tile = tile == 0
    ctx_tiles, lat_tiles = N_TOK_CTX // FFN_TILE_M, DEC_SEQ // FFN_TILE_M
    mod_ref = mod_ref.at[pl.ds(jnp.where(tile < ctx_tiles, 0, 1 + (tile - ctx_tiles) // lat_tiles), 1), :]
    g2_ref = g2_ref.at[pl.ds(layer, 1), :]

    def w1_copy(c):
        slot = c % FFN_STAGES
        c0 = _aligned(c * FFN_CHUNK, FFN_CHUNK)
        return pltpu.make_async_copy(w1_hbm.at[layer, :, pl.ds(c0, FFN_CHUNK)], st1_scr.at[slot], sem.at[0, slot])

    def w2_copy(c):
        slot = c % FFN_STAGES
        c0 = _aligned(c * FFN_CHUNK, FFN_CHUNK)
        return pltpu.make_async_copy(w2_hbm.at[layer, pl.ds(c0, FFN_CHUNK), :], st2_scr.at[slot], sem.at[1, slot])

    def land_w1(c):
        w1_copy(c).wait()
        c0 = _aligned(c * FFN_CHUNK, FFN_CHUNK)
        w1_scr[:, pl.ds(c0, FFN_CHUNK)] = st1_scr[c % FFN_STAGES].astype(BF16)

    def land_w2(c):
        w2_copy(c).wait()
        c0 = _aligned(c * FFN_CHUNK, FFN_CHUNK)
        w2_scr[pl.ds(c0, FFN_CHUNK), :] = st2_scr[c % FFN_STAGES].astype(BF16)

    def prepare_next_layer():
        if not final:
            _ada_block(cond_ref, wada_ref, bada_ref, mods_out, layer=layer + 1)
            wbr_out[...] = wbr_ref[...].astype(BF16)
            wo_out[...] = wo_ref[...].astype(BF16)

    def normed_input():
        mod = mod_ref[...]
        sh2 = mod[:, 3 * D_MODEL:4 * D_MODEL]
        sc2 = mod[:, 4 * D_MODEL:5 * D_MODEL]
        x = x_ref[...]
        ms = jnp.mean(x * x, axis=-1, keepdims=True)
        return (x * lax.rsqrt(ms + EPS) * g2_ref[...] * (1.0 + sc2) + sh2).astype(BF16)

    def hidden(ub, c):
        c0 = _aligned(c * FFN_CHUNK, FFN_CHUNK)
        h = _dot(ub, w1_scr[:, pl.ds(c0, FFN_CHUNK)])
        h_scr[:, pl.ds(c0, FFN_CHUNK)] = jnp.square(jnp.maximum(h, 0.0)).astype(BF16)

    def finish(ffn_out):
        gt2 = mod_ref[...][:, 5 * D_MODEL:6 * D_MODEL]
        y = x_ref[...] + gt2 * ffn_out
        if not final:
            out_refs[0][...] = y
            return
        y = y * lax.rsqrt(jnp.mean(y * y, axis=-1, keepdims=True) + EPS) * gfin_ref[...]
        in_ctx = tile < ctx_tiles

        @pl.when(in_ctx)
        def _():
            out_refs[0][...] = y

        @pl.when(jnp.logical_not(in_ctx))
        def _():
            out_refs[1][...] = y

    @pl.when(first_tile)
    def _():
        for c in range(FFN_STAGES):
            w1_copy(c).start()
            w2_copy(c).start()
        prepare_next_layer()
        ub = normed_input()
        acc_scr[...] = jnp.zeros_like(acc_scr)

        def chunk(c, carry):
            land_w1(c)
            land_w2(c)

            @pl.when(c + FFN_STAGES < n_chunks)
            def _():
                w1_copy(c + FFN_STAGES).start()
                w2_copy(c + FFN_STAGES).start()

            hidden(ub, c)
            c0 = _aligned(c * FFN_CHUNK, FFN_CHUNK)
            acc_scr[...] += _dot(h_scr[:, pl.ds(c0, FFN_CHUNK)], w2_scr[pl.ds(c0, FFN_CHUNK), :])
            return carry

        lax.fori_loop(0, n_chunks, chunk, 0)
        finish(acc_scr[...])

    @pl.when(jnp.logical_not(first_tile))
    def _():
        prepare_next_layer()
        ub = normed_input()
        for c in range(n_chunks):
            hidden(ub, c)
        finish(_dot(h_scr[...], w2_scr[...]))


def _ffn_call(l, xall, lay, prm, g_final, final):
    n_tiles = N_TOK // FFN_TILE_M
    ctx_tiles = N_TOK_CTX // FFN_TILE_M
    lat_tiles = DEC_SEQ // FFN_TILE_M
    once = pl.Buffered(1)

    x_spec = pl.BlockSpec((FFN_TILE_M, D_MODEL), lambda t: (t, 0))
    in_specs = [
        x_spec,
        pl.BlockSpec((COND_ROWS, 6 * D_MODEL), lambda t: (0, 0), pipeline_mode=once),
        pl.BlockSpec((DEPTH, D_MODEL), lambda t: (0, 0), pipeline_mode=once),
        pl.BlockSpec(memory_space=pl.ANY),
        pl.BlockSpec(memory_space=pl.ANY),
        pl.BlockSpec((1, D_MODEL), lambda t: (0, 0)),
    ]
    args = [xall, lay["mods"], prm["g_norm2"], prm["w_ff1"], prm["w_ff2"], g_final.reshape(1, D_MODEL)]
    if final:
        out_specs = [
            pl.BlockSpec((FFN_TILE_M, D_MODEL), lambda t: (jnp.minimum(t, ctx_tiles - 1), 0)),
            pl.BlockSpec((FFN_TILE_M, D_MODEL), lambda t: (jnp.maximum(t - ctx_tiles, 0), 0)),
        ]
        out_shape = [jax.ShapeDtypeStruct((N_TOK_CTX, D_MODEL), F32),
                     jax.ShapeDtypeStruct((N_TOK - N_TOK_CTX, D_MODEL), F32)]
        aliases = {}
    else:
        ada_cols = 6 * D_MODEL // n_tiles
        br_rows = N_BRANCH * D_BRANCH // n_tiles
        wo_tiles = D_MODEL // br_rows
        assert ada_cols % 128 == 0 and br_rows % 16 == 0 and wo_tiles <= n_tiles
        wo_map = lambda t: (jnp.minimum(t, wo_tiles - 1), 0)
        in_specs += [
            pl.BlockSpec((COND_ROWS, D_MODEL), lambda t: (0, 0)),
            pl.BlockSpec((None, D_MODEL, ada_cols), lambda t: (l + 1, 0, t)),
            pl.BlockSpec((DEPTH, ada_cols), lambda t: (0, t)),
            pl.BlockSpec((None, br_rows, D_MODEL), lambda t: (l + 1, t, 0)),
            pl.BlockSpec((None, br_rows, D_MODEL), lambda t: (l + 1,) + wo_map(t)),
        ]
        args += [prm["cond8"], prm["w_ada"], prm["b_ada"], prm["w_branch_rows"], prm["w_out"]]
        out_specs = [
            x_spec,
            pl.BlockSpec((COND_ROWS, ada_cols), lambda t: (0, t)),
            pl.BlockSpec((br_rows, D_MODEL), lambda t: (t, 0)),
            pl.BlockSpec((br_rows, D_MODEL), wo_map),
        ]
        out_shape = [jax.ShapeDtypeStruct(xall.shape, xall.dtype),
                     jax.ShapeDtypeStruct((COND_ROWS, 6 * D_MODEL), F32),
                     jax.ShapeDtypeStruct((N_BRANCH * D_BRANCH, D_MODEL), BF16),
                     jax.ShapeDtypeStruct((D_MODEL, D_MODEL), BF16)]
        aliases = {0: 0}
    return pl.pallas_call(
        functools.partial(_ffn_kernel, layer=l, final=final),
        grid=(n_tiles,),
        in_specs=in_specs,
        out_specs=out_specs,
        out_shape=out_shape,
        scratch_shapes=[
            pltpu.VMEM((FFN_TILE_M, D_MODEL), F32),
            pltpu.VMEM((FFN_TILE_M, D_FF), BF16),
            pltpu.VMEM((D_MODEL, D_FF), BF16),
            pltpu.VMEM((D_FF, D_MODEL), BF16),
            pltpu.VMEM((FFN_STAGES, D_MODEL, FFN_CHUNK), F32),
            pltpu.VMEM((FFN_STAGES, FFN_CHUNK, D_MODEL), F32),
            pltpu.SemaphoreType.DMA((2, FFN_STAGES)),
        ],
        input_output_aliases=aliases,
        compiler_params=pltpu.CompilerParams(
            dimension_semantics=("arbitrary",), vmem_limit_bytes=VMEM_LIMIT_BYTES),
        name="ffn_final" if final else "ffn",
    )(*args)


def kernel(x_prompt, x_sample, state_C, state_n, state_m, c, c_ctx, w_ada, b_ada, g_norm1, g_norm2, w_in,
           b_gates, w_pool, pool_scale, g_sgu, w_sgu, b_sgu, g_mlstm, w_branch, w_out, w_ff1, w_ff2, g_final):
    w_in16 = w_in.astype(BF16)
    prm = {
        "w_in16": w_in16,
        "w_gate": jnp.pad(w_in16[:, :, N_MAIN:N_MAIN + N_GATE], ((0, 0), (0, 0), (0, GATE_PAD - N_GATE))),
        "w_brg": w_in16[:, :, N_MAIN + N_GATE:],
        "w_kt": jnp.swapaxes(w_in16[:, :, 4 * D_BRANCH:5 * D_BRANCH], 1, 2),
        "w_branch_rows": w_branch.reshape(DEPTH, N_BRANCH * D_BRANCH, D_MODEL),
        "w_out": w_out,
        "w_ada": w_ada,
        "b_ada": b_ada,
        "w_pool": w_pool.astype(BF16),
        "w_sgu": w_sgu.astype(BF16),
        "w_ff1": w_ff1,
        "w_ff2": w_ff2,
        "g_norm1": g_norm1,
        "g_norm2": g_norm2,
        "pool_scale": pool_scale,
        "g_sgu": g_sgu,
        "g_mlstm": g_mlstm,
        "b_gates": b_gates,
        "b_sgu_tile": jnp.repeat(jnp.swapaxes(b_sgu, 1, 2), GROUP, axis=2),
        "cond8": jnp.concatenate(
            [c_ctx[None, :], c, jnp.zeros((COND_ROWS - 1 - DEC_BATCH, D_MODEL), F32)], axis=0),
    }
    lay = {
        "mods": _ada_call(prm["cond8"], w_ada, b_ada),
        "w_branch": prm["w_branch_rows"][0].astype(BF16),
        "w_out": w_out[0].astype(BF16),
    }

    state_cn = jnp.concatenate(
        [state_C, jnp.broadcast_to(state_n[..., None], state_n.shape + (HEAD_DIM,))], axis=-1)
    xall = jnp.concatenate([x_prompt.reshape(N_TOK_CTX, D_MODEL), x_sample.reshape(-1, D_MODEL)], axis=0)
    new_states = [jnp.zeros((BATCH, DEPTH, 2, N_HEADS, HEAD_DIM, HEAD_DIM), F32),
                  jnp.zeros((BATCH, DEPTH, 2, N_HEADS, HEAD_DIM), F32),
                  jnp.zeros((BATCH, DEPTH, 2, N_HEADS), F32)]
    for l in range(DEPTH):
        xall, *new_states = _mixer_call(l, xall, lay, prm, "ctx", state_outs=new_states)
        (xall,) = _mixer_call(l, xall, lay, prm, "lat", states=(state_cn, state_m))
        if l < DEPTH - 1:
            xall, *nxt = _ffn_call(l, xall, lay, prm, g_final, final=False)
            lay = dict(zip(("mods", "w_branch", "w_out"), nxt))
    y_prompt, y_sample = _ffn_call(DEPTH - 1, xall, lay, prm, g_final, final=True)
    return (y_prompt.reshape(BATCH, SEQ, D_MODEL), y_sample.reshape(DEC_BATCH, DEC_SEQ, D_MODEL), *new_states)
```

```python
import functools

import jax
import jax.numpy as jnp
from jax import lax
from jax.experimental import pallas as pl
from jax.experimental.pallas import tpu as pltpu

F32 = jnp.float32
BF16 = jnp.bfloat16

D_MODEL = 1024
DEPTH = 4
BATCH = 16
SEQ = 256
DEC_BATCH = 2
DEC_SEQ = 1024
GRID_W = 64
D_BRANCH = 512
POOL_WINDOWS = (2, 4, 8, 16)
N_GROUPS = 4
GROUP = D_BRANCH // N_GROUPS
SGU_CHUNK = 128
N_HEADS = 4
HEAD_DIM = D_BRANCH // N_HEADS
N_BRANCH = 3
D_FF = 4 * D_MODEL
EPS = 1e-6

N_MAIN = 7 * D_BRANCH
N_GATE = 4 * N_HEADS
GATE_PAD = 128
N_TOK_CTX = BATCH * SEQ
N_TOK = N_TOK_CTX + DEC_BATCH * DEC_SEQ
COND_ROWS = 8

TILE_M = 256
CTX_SEQS_PER_STEP = 1
MLSTM_L = 256
MLSTM_ROW_BLOCK = 128
MERGE_COLS = 256
POOL_PAD = 8
FFN_TILE_M = 512
FFN_CHUNK = 512
FFN_STAGES = 2
ADA_TILE_N = 1536
VMEM_LIMIT_BYTES = 60 * 1024 * 1024


_dot = functools.partial(jnp.dot, preferred_element_type=F32)
_dot_nt = functools.partial(lax.dot_general, dimension_numbers=(((1,), (1,)), ((), ())),
                            preferred_element_type=F32)


def _aligned(offset, align):
    return offset if isinstance(offset, int) else pl.multiple_of(offset, align)


def _ada_block(cond_ref, w_ref, b_ref, out_ref, *, layer=0):
    cond = cond_ref[...]
    s = (cond * jax.nn.sigmoid(cond)).astype(BF16)
    out_ref[...] = _dot(s, w_ref[...].astype(BF16)) + b_ref[layer:layer + 1, :]


def _ada_call(cond8, w_ada, b_ada):
    n_out = 6 * D_MODEL
    return pl.pallas_call(
        _ada_block,
        grid=(n_out // ADA_TILE_N,),
        in_specs=[
            pl.BlockSpec((COND_ROWS, D_MODEL), lambda j: (0, 0)),
            pl.BlockSpec((None, D_MODEL, ADA_TILE_N), lambda j: (0, 0, j)),
            pl.BlockSpec((DEPTH, ADA_TILE_N), lambda j: (0, j)),
        ],
        out_specs=pl.BlockSpec((COND_ROWS, ADA_TILE_N), lambda j: (0, j)),
        out_shape=jax.ShapeDtypeStruct((COND_ROWS, n_out), F32),
        compiler_params=pltpu.CompilerParams(
            dimension_semantics=("arbitrary",), vmem_limit_bytes=VMEM_LIMIT_BYTES),
        name="ada_mod",
    )(cond8, w_ada, b_ada)


def _seg_scan(x, pos, seg, total, op, reverse):
    d = 1
    while d < seg:
        if reverse:
            shifted = pltpu.roll(x, total - d, 1)
            valid = pos < seg - d
        else:
            shifted = pltpu.roll(x, d, 1)
            valid = pos >= d
        if op == "sum":
            x = x + jnp.where(valid, shifted, 0.0)
        else:
            x = jnp.maximum(x, jnp.where(valid, shifted, -jnp.inf))
        d *= 2
    return x


_MIXER_LAYER_ROW_INPUTS = (2, 10, 11, 14, 15)


def _interleave(first, second):
    out, i, j = [], 0, 0
    while i < len(first) or j < len(second):
        if j >= len(second) or (i < len(first) and i * len(second) <= j * len(first)):
            out.append(first[i])
            i += 1
        else:
            out.append(second[j])
            j += 1
    return out


def _mixer_kernel(*refs, T, grid2d, has_state, layer, n_elems, n_in, n_pass, n_out):
    ins, outs, scr = list(refs[:n_in - n_pass]), refs[n_in:n_in + n_out], refs[n_in + n_out:]
    ins[1] = ins[1].at[pl.ds(1 + pl.program_id(0) if has_state else 0, 1), :]
    for k in _MIXER_LAYER_ROW_INPUTS:
        ins[k] = ins[k].at[pl.ds(layer, 1), :]
    state_outs = list(outs[1:])
    if state_outs and n_pass == 0:
        for k, o in enumerate(state_outs):
            o[...] = jnp.zeros(o.shape, o.dtype)
            state_outs[k] = o.at[:, layer]
    stages = []
    for e in range(n_elems):
        rows = slice(e * T, (e + 1) * T)
        ins_e = [ins[0].at[rows, :]] + list(ins[1:])
        outs_e = [outs[0].at[rows, :]] + [o.at[e] for o in state_outs]
        scr_e = [s.at[e] for s in scr]
        stages.append(_mixer_sequence(*ins_e, *outs_e, *scr_e, T=T, grid2d=grid2d, has_state=has_state))
    order = list(stages[0][0])
    for e in range(n_elems):
        order += _interleave(stages[e][1], stages[e + 1][0] if e + 1 < n_elems else [])
    for thunk in order:
        thunk()


def _mixer_sequence(*refs, T, grid2d, has_state):
    it = iter(refs)
    x_ref = next(it)
    mod_ref = next(it)
    g1_ref = next(it)
    wmain_ref = next(it)
    wkt_ref = next(it)
    wgate_ref = next(it)
    wbrg_ref = next(it)
    wbranch_ref = next(it)
    wout_ref = next(it)
    wpool_ref = next(it)
    pscale_ref = next(it)
    gsgu_ref = next(it)
    wsgu_ref = next(it)
    bsgu_ref = next(it)
    bgates_ref = next(it)
    gml_ref = next(it)
    if has_state:
        c0_ref, n0_ref, m0_ref = next(it), next(it), next(it)
    out_ref = next(it)
    if not has_state:
        cout_ref, nout_ref, mout_ref = next(it), next(it), next(it)
    (u_scr, xp_scr, su_scr, vn_scr, q_scr, kt_scr, vaug_scr, o_scr, grow_scr, colf_scr,
     h_scr, ya_scr, yb_scr, yc_scr, mab_scr, gc_scr, merged_scr) = [next(it) for _ in range(17)]
    if has_state:
        cn_scr = next(it)
    if grid2d:
        m1_scr, dlt_scr = next(it), next(it)

    L = MLSTM_L
    n_tiles = T // TILE_M
    n_chunks = T // L
    RB = MLSTM_ROW_BLOCK
    NW = MERGE_COLS
    n_merge = D_MODEL // NW

    def tile_rows(i):
        return pl.ds(_aligned(i * TILE_M, TILE_M), TILE_M)

    def a_norm(i):
        mod = mod_ref[...]
        sh1 = mod[:, 0:D_MODEL]
        sc1 = mod[:, D_MODEL:2 * D_MODEL]
        x = x_ref[tile_rows(i), :]
        ms = jnp.mean(x * x, axis=-1, keepdims=True)
        u = x * lax.rsqrt(ms + EPS) * g1_ref[...] * (1.0 + sc1) + sh1
        u_scr[tile_rows(i), :] = u.astype(BF16)

    def a_gates(i):
        lane = lax.broadcasted_iota(jnp.int32, (2 * N_HEADS, TILE_M), 1)
        is_fwd = lax.broadcasted_iota(jnp.int32, (2 * N_HEADS, TILE_M), 0) < N_HEADS
        bias = jnp.concatenate([bgates_ref[...], jnp.zeros((1, GATE_PAD - N_GATE), F32)], axis=1)
        gate_t = (_dot(u_scr[tile_rows(i), :], wgate_ref[...]) + bias).T
        gp = gate_t[0:N_GATE, :]
        i_all = gp[0:2 * N_HEADS, :]
        f_all = gp[2 * N_HEADS:4 * N_HEADS, :]
        logf = jnp.minimum(f_all, 0.0) - jnp.log1p(jnp.exp(-jnp.abs(f_all)))
        b_row = jnp.where(is_fwd, _seg_scan(logf, lane, L, L, "sum", False),
                          _seg_scan(logf, lane, L, L, "sum", True))
        g_row = i_all - b_row
        cm_row = jnp.where(is_fwd, _seg_scan(g_row, lane, L, L, "max", False),
                           _seg_scan(g_row, lane, L, L, "max", True))
        grow_scr[:, tile_rows(i)] = g_row
        stacked = jnp.concatenate(
            [b_row, cm_row, jnp.zeros((GATE_PAD - 4 * N_HEADS, TILE_M), F32)], axis=0)
        colf_scr[tile_rows(i), :] = stacked.T

    def proj(i, k):
        return _dot(u_scr[tile_rows(i), :], wmain_ref[:, k * D_BRANCH:(k + 1) * D_BRANCH])

    def a_xp(i):
        xp_scr[pl.ds(_aligned(i * TILE_M + POOL_PAD, 8), TILE_M), :] = proj(i, 0)

    def a_su(i):
        su_scr[...] = proj(i, 1)

    def a_sv(i):
        sv = proj(i, 2)
        vn = sv * lax.rsqrt(jnp.mean(sv * sv, axis=-1, keepdims=True) + EPS) * gsgu_ref[...]
        vn_scr[...] = vn.astype(BF16)

    def a_q(i):
        q_scr[tile_rows(i), :] = proj(i, 3).astype(BF16)

    def a_k(i):
        kt = _dot_nt(wkt_ref[...], u_scr[tile_rows(i), :])
        kt_scr[:, tile_rows(i)] = (kt * (HEAD_DIM ** -0.5)).astype(BF16)

    def a_v(i):
        v = proj(i, 5).astype(BF16)
        ones_blk = jnp.ones((TILE_M, HEAD_DIM), BF16)
        for h in range(N_HEADS):
            vaug_scr[tile_rows(i), 2 * h * HEAD_DIM:(2 * h + 1) * HEAD_DIM] = v[:, h * HEAD_DIM:(h + 1) * HEAD_DIM]
            vaug_scr[tile_rows(i), (2 * h + 1) * HEAD_DIM:(2 * h + 2) * HEAD_DIM] = ones_blk

    def a_o(i):
        o_scr[tile_rows(i), :] = proj(i, 6)

    def a_sgu(ch, i):
        crow = slice(ch * SGU_CHUNK, (ch + 1) * SGU_CHUNK)
        for g in range(N_GROUPS):
            gc = slice(g * GROUP, (g + 1) * GROUP)
            mixed = _dot(wsgu_ref[g], vn_scr[crow, gc]) + bsgu_ref[:, gc]
            yb = su_scr[crow, gc] * mixed
            yb_scr[pl.ds(_aligned(i * TILE_M + ch * SGU_CHUNK, SGU_CHUNK), SGU_CHUNK), gc] = yb.astype(BF16)

    a_parts = [a_norm, a_gates, a_xp, a_su, a_sv, a_q, a_k, a_v, a_o]
    a_parts += [functools.partial(a_sgu, ch) for ch in range(TILE_M // SGU_CHUNK)]

    def pool_pad():
        zpad = jnp.zeros((POOL_PAD, D_BRANCH), F32)
        xp_scr[0:POOL_PAD, :] = zpad
        xp_scr[T + POOL_PAD:T + 2 * POOL_PAD, :] = zpad

    def pool_seq(g):
        win = POOL_WINDOWS[g]
        gc = slice(g * GROUP, (g + 1) * GROUP)
        half = win // 2
        t_idx = lax.broadcasted_iota(jnp.int32, (T, GROUP), 0)
        total = jnp.zeros((T, GROUP), F32)
        for j in range(-half, win - half):
            total = total + xp_scr[POOL_PAD + j:POOL_PAD + j + T, gc]
        cnt = (jnp.minimum(t_idx + (win - half), T) - jnp.maximum(t_idx - half, 0)).astype(F32)
        dlt = total / cnt - xp_scr[POOL_PAD:POOL_PAD + T, gc]
        ya = _dot(dlt.astype(BF16), wpool_ref[g]) * pscale_ref[...][:, gc]
        ya_scr[:, gc] = ya.astype(BF16)

    def a_colpool(i):
        c_idx = lax.broadcasted_iota(jnp.int32, (GRID_W, GROUP), 0)
        for g, win in enumerate(POOL_WINDOWS):
            gc = slice(g * GROUP, (g + 1) * GROUP)
            half = win // 2
            inv_cnt = 1.0 / (jnp.minimum(c_idx + (win - half), GRID_W) - jnp.maximum(c_idx - half, 0)).astype(F32)
            for r in range(TILE_M // GRID_W):
                r0 = i * TILE_M + r * GRID_W
                xg = xp_scr[pl.ds(_aligned(r0 + POOL_PAD, 8), GRID_W), gc]
                total = xg
                for j in range(-half, win - half):
                    if j != 0:
                        valid = (c_idx + j >= 0) & (c_idx + j < GRID_W)
                        total = total + jnp.where(valid, pltpu.roll(xg, (-j) % GRID_W, 0), 0.0)
                m1_scr[pl.ds(_aligned(r0, GRID_W), GRID_W), gc] = total * inv_cnt

    def pool_grid_rows(g):
        win = POOL_WINDOWS[g]
        gc = slice(g * GROUP, (g + 1) * GROUP)
        half = win // 2
        n_rows = T // GRID_W
        for r in range(n_rows):
            lo = max(r - half, 0)
            hi = min(r + (win - half), n_rows)
            acc = m1_scr[lo * GRID_W:(lo + 1) * GRID_W, gc]
            for rr in range(lo + 1, hi):
                acc = acc + m1_scr[rr * GRID_W:(rr + 1) * GRID_W, gc]
            xg = xp_scr[POOL_PAD + r * GRID_W:POOL_PAD + (r + 1) * GRID_W, gc]
            dlt_scr[r * GRID_W:(r + 1) * GRID_W, :] = acc / float(hi - lo) - xg
        ya = _dot(dlt_scr[...].astype(BF16), wpool_ref[g]) * pscale_ref[...][:, gc]
        ya_scr[:, gc] = ya.astype(BF16)

    if grid2d:
        assert TILE_M % GRID_W == 0
        a_parts.insert(a_parts.index(a_xp) + 1, a_colpool)
        pool_parts = [functools.partial(pool_grid_rows, g) for g in range(N_GROUPS)]
    else:
        pool_parts = [pool_pad] + [functools.partial(pool_seq, g) for g in range(N_GROUPS)]

    def keep_mask(d, rb, col_lo, ncols):
        t_loc = rb * RB + lax.broadcasted_iota(jnp.int32, (RB, ncols), 0)
        s_loc = col_lo + lax.broadcasted_iota(jnp.int32, (RB, ncols), 1)
        return s_loc <= t_loc if d == 0 else s_loc >= t_loc

    def mlstm_prepare(d, c, m_in):
        c0 = _aligned(c * L, L)
        colf = colf_scr[pl.ds(c0, L), :]
        bcol = colf[:, N_HEADS * d:N_HEADS * (d + 1)]
        cmcol = colf[:, 2 * N_HEADS + N_HEADS * d:2 * N_HEADS + N_HEADS * (d + 1)]
        mx = jnp.maximum(m_in, cmcol)
        last = L - 1 if d == 0 else 0
        mx_last = mx[last:last + 1, :]
        return dict(c0=c0, mx=mx, mx_last=mx_last, w_inter=jnp.exp(m_in - mx), e_negm=jnp.exp(-(bcol + mx)),
                    decay=jnp.exp(m_in - mx_last), m_new=bcol[last:last + 1, :] + mx_last)

    def block_cols(d, rb):
        return (0, (rb + 1) * RB) if d == 0 else (rb * RB, L)

    def mlstm_scores(d, p, h):
        c0 = p["c0"]
        hc = slice(h * HEAD_DIM, (h + 1) * HEAD_DIM)
        qks = []
        for rb in range(L // RB):
            col_lo, col_hi = block_cols(d, rb)
            qb = q_scr[pl.ds(_aligned(c0 + rb * RB, RB), RB), hc]
            qks.append(_dot(qb, kt_scr[hc, pl.ds(_aligned(c0 + col_lo, RB), col_hi - col_lo)]))
        p["qk", h] = qks

    def mlstm_head(d, p, h):
        c0, mx, mx_last = p["c0"], p["mx"], p["mx_last"]
        rows = pl.ds(c0, L)
        hc = slice(h * HEAD_DIM, (h + 1) * HEAD_DIM)
        ac = slice(2 * h * HEAD_DIM, (2 * h + 2) * HEAD_DIM)
        grow = grow_scr[N_HEADS * d + h:N_HEADS * d + h + 1, rows]
        if has_state:
            cn_dh = cn_scr[d, h]
            cn_bf = cn_dh.astype(BF16)
        for rb in range(L // RB):
            col_lo, col_hi = block_cols(d, rb)
            ncols = col_hi - col_lo
            trow = slice(rb * RB, (rb + 1) * RB)
            rrows = pl.ds(_aligned(c0 + rb * RB, RB), RB)
            crows = pl.ds(_aligned(c0 + col_lo, RB), ncols)
            qb = q_scr[rrows, hc]
            qk = p["qk", h][rb]
            w = jnp.exp(grow[:, col_lo:col_hi] - mx[trow, h:h + 1])
            s = (qk * jnp.where(keep_mask(d, rb, col_lo, ncols), w, 0.0)).astype(BF16)
            na = _dot(s, vaug_scr[crows, ac])
            if has_state:
                na = na + p["w_inter"][trow, h:h + 1] * _dot(qb, cn_bf)
            hval = na[:, 0:HEAD_DIM] / jnp.maximum(jnp.abs(na[:, HEAD_DIM:]), p["e_negm"][trow, h:h + 1])
            if d == 0:
                h_scr[rrows, hc] = hval
            else:
                h_scr[rrows, hc] += hval
        w_state = jnp.exp(grow - mx_last[:, h:h + 1])
        kw_t = (kt_scr[hc, rows].astype(F32) * w_state).astype(BF16)
        upd = _dot(kw_t, vaug_scr[rows, ac])
        if has_state:
            cn_scr[d, h] = p["decay"][:, h:h + 1] * cn_dh + upd
        else:
            cout_ref[d, h] = upd[:, 0:HEAD_DIM]
            nout_ref[d, h:h + 1, :] = upd[:, HEAD_DIM:].T[0:1, :]

    def merge_ab(c, nj):
        nc = slice(nj * NW, (nj + 1) * NW)
        ub = u_scr[tile_rows(c), :]
        acc = None
        for r, y_scr in enumerate((ya_scr, yb_scr)):
            gate = jax.nn.sigmoid(_dot(ub, wbrg_ref[:, r * D_MODEL + nj * NW:r * D_MODEL + (nj + 1) * NW]))
            term = gate * _dot(y_scr[tile_rows(c), :], wbranch_ref[r * D_BRANCH:(r + 1) * D_BRANCH, nc])
            acc = term if acc is None else acc + term
        mab_scr[tile_rows(c), nc] = acc

    def gate_c(c, nj):
        w = wbrg_ref[:, 2 * D_MODEL + nj * NW:2 * D_MODEL + (nj + 1) * NW]
        gc_scr[:, nj * NW:(nj + 1) * NW] = jax.nn.sigmoid(_dot(u_scr[tile_rows(c), :], w))

    def tail_yc(c):
        gml = gml_ref[...]
        for h in range(N_HEADS):
            hc = slice(h * HEAD_DIM, (h + 1) * HEAD_DIM)
            hh = h_scr[tile_rows(c), hc]
            hn = hh * lax.rsqrt(jnp.mean(hh * hh, axis=-1, keepdims=True) + EPS) * gml[:, hc]
            yc_scr[:, hc] = (jax.nn.sigmoid(o_scr[tile_rows(c), hc]) * hn).astype(BF16)

    def tail_merge(c, nj):
        nc = slice(nj * NW, (nj + 1) * NW)
        acc = mab_scr[tile_rows(c), nc] + gc_scr[:, nc] * _dot(yc_scr[...], wbranch_ref[2 * D_BRANCH:, nc])
        merged_scr[:, nc] = acc.astype(BF16)

    def tail_out(c):
        gt1 = mod_ref[...][:, 2 * D_MODEL:3 * D_MODEL]
        out_ref[tile_rows(c), :] = x_ref[tile_rows(c), :] + gt1 * _dot(merged_scr[...], wout_ref[...])

    def tail_parts(c):
        return ([functools.partial(tail_yc, c)] + [functools.partial(tail_merge, c, nj) for nj in range(n_merge)]
                + [functools.partial(tail_out, c)])

    def scan_parts(d, c, m_in, fillers, out):
        p = {}

        def prepare():
            p.update(mlstm_prepare(d, c, m_in))
            out[d] = p["m_new"]

        parts = [prepare]
        for h in range(N_HEADS):
            parts.append(functools.partial(mlstm_scores, d, p, h))
            if h < len(fillers):
                parts.append(fillers[h])
            parts.append(functools.partial(mlstm_head, d, p, h))
        return parts + list(fillers[N_HEADS:])

    if n_tiles == 1 and n_chunks == 1:
        zeros = jnp.zeros((1, N_HEADS), F32)
        m_out = {}
        stage_a = [functools.partial(part, 0) for part in a_parts] + pool_parts
        stage_b = scan_parts(0, 0, zeros, [functools.partial(merge_ab, 0, nj) for nj in range(n_merge)], m_out)
        stage_b += scan_parts(1, 0, zeros, [functools.partial(gate_c, 0, nj) for nj in range(n_merge)], m_out)
        stage_b += tail_parts(0)

        def write_m():
            mout_ref[0:1, :] = m_out[0]
            mout_ref[1:2, :] = m_out[1]

        return stage_a, stage_b + [write_m]

    def run_loops():
        def phase_a(i, carry):
            for part in a_parts:
                part(i)
            return carry

        lax.fori_loop(0, n_tiles, phase_a, 0)
        for part in pool_parts:
            part()
        for d in range(2):
            for h in range(N_HEADS):
                n_rows = jnp.broadcast_to(n0_ref[d, h:h + 1, :], (HEAD_DIM, HEAD_DIM))
                cn_scr[d, h] = jnp.concatenate([c0_ref[d, h], n_rows.T], axis=1)
        m_init = m0_ref[...]

        def fwd_step(j, m_in):
            m_out = {}
            for part in scan_parts(0, j, m_in, [functools.partial(merge_ab, j, nj) for nj in range(n_merge)], m_out):
                part()
            return m_out[0]

        def bwd_step(j, m_in):
            c = n_chunks - 1 - j
            m_out = {}
            for part in scan_parts(1, c, m_in, [functools.partial(gate_c, c, nj) for nj in range(n_merge)], m_out):
                part()
            for part in tail_parts(c):
                part()
            return m_out[1]

        lax.fori_loop(0, n_chunks, fwd_step, m_init[0:1, :])
        lax.fori_loop(0, n_chunks, bwd_step, m_init[1:2, :])

    assert has_state, "multi-chunk sequences carry the cached state"
    return [run_loops], []


def _mixer_call(l, xall, lay, prm, stream, states=None, state_outs=None):
    ctx = stream == "ctx"
    T = SEQ if ctx else DEC_SEQ
    assert T % TILE_M == 0 and TILE_M == MLSTM_L and MLSTM_L % MLSTM_ROW_BLOCK == 0
    n_e = CTX_SEQS_PER_STEP if ctx else 1
    n_b = (BATCH if ctx else DEC_BATCH) // n_e
    blk0 = 0 if ctx else N_TOK_CTX // T
    once = pl.Buffered(1)

    def wspec(shape):
        nd = len(shape)
        return pl.BlockSpec((None,) + shape, lambda b: (l,) + (0,) * nd, pipeline_mode=once)

    def lspec(shape):
        return pl.BlockSpec(shape, lambda b: (0,) * len(shape), pipeline_mode=once)

    x_spec = pl.BlockSpec((n_e * T, D_MODEL), lambda b: (blk0 + b, 0), pipeline_mode=None if ctx else once)
    in_specs = [
        x_spec,
        lspec((COND_ROWS, 6 * D_MODEL)),
        lspec((DEPTH, D_MODEL)),
        wspec((D_MODEL, N_MAIN)),
        wspec((D_BRANCH, D_MODEL)),
        wspec((D_MODEL, GATE_PAD)),
        wspec((D_MODEL, N_BRANCH * D_MODEL)),
        lspec((N_BRANCH * D_BRANCH, D_MODEL)),
        lspec((D_MODEL, D_MODEL)),
        wspec((N_GROUPS, GROUP, GROUP)),
        lspec((DEPTH, D_BRANCH)),
        lspec((DEPTH, D_BRANCH)),
        wspec((N_GROUPS, SGU_CHUNK, SGU_CHUNK)),
        wspec((SGU_CHUNK, D_BRANCH)),
        lspec((DEPTH, N_GATE)),
        lspec((DEPTH, D_BRANCH)),
    ]
    args = [xall, lay["mods"], prm["g_norm1"], prm["w_in16"], prm["w_kt"],
            prm["w_gate"], prm["w_brg"], lay["w_branch"], lay["w_out"],
            prm["w_pool"], prm["pool_scale"], prm["g_sgu"], prm["w_sgu"],
            prm["b_sgu_tile"], prm["b_gates"], prm["g_mlstm"]]
    out_shape = [jax.ShapeDtypeStruct(xall.shape, xall.dtype)]
    out_specs = [x_spec]
    aliases = {0: 0}
    passthrough = []
    if ctx:
        per_seq = [(2, N_HEADS, HEAD_DIM, HEAD_DIM), (2, N_HEADS, HEAD_DIM), (2, N_HEADS)]
        out_shape += [jax.ShapeDtypeStruct((BATCH, DEPTH) + s, F32) for s in per_seq]
        if state_outs is None:
            out_specs += [pl.BlockSpec((n_e, DEPTH) + s, lambda b, nd=len(s): (b,) + (0,) * (nd + 1))
                          for s in per_seq]
        else:
            out_specs += [pl.BlockSpec((n_e, None) + s, lambda b, nd=len(s): (b, l) + (0,) * nd) for s in per_seq]
            passthrough += [(a, 1 + k) for k, a in enumerate(state_outs)]
    else:
        in_specs += [
            pl.BlockSpec((None, None, 2, N_HEADS, HEAD_DIM, HEAD_DIM), lambda b: (b, l, 0, 0, 0, 0)),
            pl.BlockSpec((None, None, 2, N_HEADS, HEAD_DIM), lambda b: (b, l, 0, 0, 0)),
            pl.BlockSpec((None, None, 2, N_HEADS), lambda b: (b, l, 0, 0)),
        ]
        args += list(states)
    for buf, out_idx in passthrough:
        aliases[len(args)] = out_idx
        in_specs.append(pl.BlockSpec(memory_space=pl.ANY))
        args.append(buf)

    scratch = [
        pltpu.VMEM((T, D_MODEL), BF16),
        pltpu.VMEM((T + 2 * POOL_PAD, D_BRANCH), F32),
        pltpu.VMEM((TILE_M, D_BRANCH), F32),
        pltpu.VMEM((TILE_M, D_BRANCH), BF16),
        pltpu.VMEM((T, D_BRANCH), BF16),
        pltpu.VMEM((D_BRANCH, T), BF16),
        pltpu.VMEM((T, 2 * D_BRANCH), BF16),
        pltpu.VMEM((T, D_BRANCH), F32),
        pltpu.VMEM((2 * N_HEADS, T), F32),
        pltpu.VMEM((T, GATE_PAD), F32),
        pltpu.VMEM((T, D_BRANCH), F32),
        pltpu.VMEM((T, D_BRANCH), BF16),
        pltpu.VMEM((T, D_BRANCH), BF16),
        pltpu.VMEM((TILE_M, D_BRANCH), BF16),
        pltpu.VMEM((T, D_MODEL), F32),
        pltpu.VMEM((TILE_M, D_MODEL), F32),
        pltpu.VMEM((TILE_M, D_MODEL), BF16),
    ]
    if not ctx:
        scratch += [pltpu.VMEM((2, N_HEADS, HEAD_DIM, 2 * HEAD_DIM), F32)]
        scratch += [pltpu.VMEM((T, D_BRANCH), F32), pltpu.VMEM((T, GROUP), F32)]
    scratch = [pltpu.VMEM((n_e,) + tuple(s.shape), s.dtype) for s in scratch]

    outs = pl.pallas_call(
        functools.partial(_mixer_kernel, T=T, grid2d=not ctx, has_state=not ctx, layer=l, n_elems=n_e,
                          n_in=len(in_specs), n_pass=len(passthrough), n_out=len(out_specs)),
        grid=(n_b,),
        in_specs=in_specs,
        out_specs=out_specs,
        out_shape=out_shape,
        scratch_shapes=scratch,
        input_output_aliases=aliases,
        compiler_params=pltpu.CompilerParams(
            dimension_semantics=("arbitrary",), vmem_limit_bytes=VMEM_LIMIT_BYTES),
        name="mixer_ctx" if ctx else "mixer_lat",
    )(*args)
    return outs


def _ffn_kernel(x_ref, mod_ref, g2_ref, w1_hbm, w2_hbm, gfin_ref, *rest, layer, final):
    if final:
        out_refs, rest = rest[:2], rest[2:]
    else:
        (cond_ref, wada_ref, bada_ref, wbr_ref, wo_ref), rest = rest[:5], rest[5:]
        out_refs, (mods_out, wbr_out, wo_out), rest = rest[:1], rest[1:4], rest[4:]
    acc_scr, h_scr, w1_scr, w2_scr, st1_scr, st2_scr, sem = rest
    n_chunks = D_FF // FFN_CHUNK
    tile = pl.program_id(0)
    first_tile = tile == 0
    ctx_tiles, lat_tiles = N_TOK_CTX // FFN_TILE_M, DEC_SEQ // FFN_TILE_M
    mod_ref = mod_ref.at[pl.ds(jnp.where(tile < ctx_tiles, 0, 1 + (tile - ctx_tiles) // lat_tiles), 1), :]
    g2_ref = g2_ref.at[pl.ds(layer, 1), :]

    def w1_copy(c):
        slot = c % FFN_STAGES
        c0 = _aligned(c * FFN_CHUNK, FFN_CHUNK)
        return pltpu.make_async_copy(w1_hbm.at[layer, :, pl.ds(c0, FFN_CHUNK)], st1_scr.at[slot], sem.at[0, slot])

    def w2_copy(c):
        slot = c % FFN_STAGES
        c0 = _aligned(c * FFN_CHUNK, FFN_CHUNK)
        return pltpu.make_async_copy(w2_hbm.at[layer, pl.ds(c0, FFN_CHUNK), :], st2_scr.at[slot], sem.at[1, slot])

    def land_w1(c):
        w1_copy(c).wait()
        c0 = _aligned(c * FFN_CHUNK, FFN_CHUNK)
        w1_scr[:, pl.ds(c0, FFN_CHUNK)] = st1_scr[c % FFN_STAGES].astype(BF16)

    def land_w2(c):
        w2_copy(c).wait()
        c0 = _aligned(c * FFN_CHUNK, FFN_CHUNK)
        w2_scr[pl.ds(c0, FFN_CHUNK), :] = st2_scr[c % FFN_STAGES].astype(BF16)

    def prepare_next_layer():
        if not final:
            _ada_block(cond_ref, wada_ref, bada_ref, mods_out, layer=layer + 1)
            wbr_out[...] = wbr_ref[...].astype(BF16)
            wo_out[...] = wo_ref[...].astype(BF16)

    def normed_input():
        mod = mod_ref[...]
        sh2 = mod[:, 3 * D_MODEL:4 * D_MODEL]
        sc2 = mod[:, 4 * D_MODEL:5 * D_MODEL]
        x = x_ref[...]
        ms = jnp.mean(x * x, axis=-1, keepdims=True)
        return (x * lax.rsqrt(ms + EPS) * g2_ref[...] * (1.0 + sc2) + sh2).astype(BF16)

    def hidden(ub, c):
        c0 = _aligned(c * FFN_CHUNK, FFN_CHUNK)
        h = _dot(ub, w1_scr[:, pl.ds(c0, FFN_CHUNK)])
        h_scr[:, pl.ds(c0, FFN_CHUNK)] = jnp.square(jnp.maximum(h, 0.0)).astype(BF16)

    def finish(ffn_out):
        gt2 = mod_ref[...][:, 5 * D_MODEL:6 * D_MODEL]
        y = x_ref[...] + gt2 * ffn_out
        if not final:
            out_refs[0][...] = y
            return
        y = y * lax.rsqrt(jnp.mean(y * y, axis=-1, keepdims=True) + EPS) * gfin_ref[...]
        in_ctx = tile < ctx_tiles

        @pl.when(in_ctx)
        def _():
            out_refs[0][...] = y

        @pl.when(jnp.logical_not(in_ctx))
        def _():
            out_refs[1][...] = y

    @pl.when(first_tile)
    def _():
        for c in range(FFN_STAGES):
            w1_copy(c).start()
            w2_copy(c).start()
        prepare_next_layer()
        ub = normed_input()
        acc_scr[...] = jnp.zeros_like(acc_scr)

        def chunk(c, carry):
            land_w1(c)
            land_w2(c)

            @pl.when(c + FFN_STAGES < n_chunks)
            def _():
                w1_copy(c + FFN_STAGES).start()
                w2_copy(c + FFN_STAGES).start()

            hidden(ub, c)
            c0 = _aligned(c * FFN_CHUNK, FFN_CHUNK)
            acc_scr[...] += _dot(h_scr[:, pl.ds(c0, FFN_CHUNK)], w2_scr[pl.ds(c0, FFN_CHUNK), :])
            return carry

        lax.fori_loop(0, n_chunks, chunk, 0)
        finish(acc_scr[...])

    @pl.when(jnp.logical_not(first_tile))
    def _():
        prepare_next_layer()
        ub = normed_input()
        for c in range(n_chunks):
            hidden(ub, c)
        finish(_dot(h_scr[...], w2_scr[...]))


def _ffn_call(l, xall, lay, prm, g_final, final):
    n_tiles = N_TOK // FFN_TILE_M
    ctx_tiles = N_TOK_CTX // FFN_TILE_M
    lat_tiles = DEC_SEQ // FFN_TILE_M
    once = pl.Buffered(1)

    x_spec = pl.BlockSpec((FFN_TILE_M, D_MODEL), lambda t: (t, 0))
    in_specs = [
        x_spec,
        pl.BlockSpec((COND_ROWS, 6 * D_MODEL), lambda t: (0, 0), pipeline_mode=once),
        pl.BlockSpec((DEPTH, D_MODEL), lambda t: (0, 0), pipeline_mode=once),
        pl.BlockSpec(memory_space=pl.ANY),
        pl.BlockSpec(memory_space=pl.ANY),
        pl.BlockSpec((1, D_MODEL), lambda t: (0, 0)),
    ]
    args = [xall, lay["mods"], prm["g_norm2"], prm["w_ff1"], prm["w_ff2"], g_final.reshape(1, D_MODEL)]
    if final:
        out_specs = [
            pl.BlockSpec((FFN_TILE_M, D_MODEL), lambda t: (jnp.minimum(t, ctx_tiles - 1), 0)),
            pl.BlockSpec((FFN_TILE_M, D_MODEL), lambda t: (jnp.maximum(t - ctx_tiles, 0), 0)),
        ]
        out_shape = [jax.ShapeDtypeStruct((N_TOK_CTX, D_MODEL), F32),
                     jax.ShapeDtypeStruct((N_TOK - N_TOK_CTX, D_MODEL), F32)]
        aliases = {}
    else:
        ada_cols = 6 * D_MODEL // n_tiles
        br_rows = N_BRANCH * D_BRANCH // n_tiles
        wo_tiles = D_MODEL // br_rows
        assert ada_cols % 128 == 0 and br_rows % 16 == 0 and wo_tiles <= n_tiles
        wo_map = lambda t: (jnp.minimum(t, wo_tiles - 1), 0)
        in_specs += [
            pl.BlockSpec((COND_ROWS, D_MODEL), lambda t: (0, 0)),
            pl.BlockSpec((None, D_MODEL, ada_cols), lambda t: (l + 1, 0, t)),
            pl.BlockSpec((DEPTH, ada_cols), lambda t: (0, t)),
            pl.BlockSpec((None, br_rows, D_MODEL), lambda t: (l + 1, t, 0)),
            pl.BlockSpec((None, br_rows, D_MODEL), lambda t: (l + 1,) + wo_map(t)),
        ]
        args += [prm["cond8"], prm["w_ada"], prm["b_ada"], prm["w_branch_rows"], prm["w_out"]]
        out_specs = [
            x_spec,
            pl.BlockSpec((COND_ROWS, ada_cols), lambda t: (0, t)),
            pl.BlockSpec((br_rows, D_MODEL), lambda t: (t, 0)),
            pl.BlockSpec((br_rows, D_MODEL), wo_map),
        ]
        out_shape = [jax.ShapeDtypeStruct(xall.shape, xall.dtype),
                     jax.ShapeDtypeStruct((COND_ROWS, 6 * D_MODEL), F32),
                     jax.ShapeDtypeStruct((N_BRANCH * D_BRANCH, D_MODEL), BF16),
                     jax.ShapeDtypeStruct((D_MODEL, D_MODEL), BF16)]
        aliases = {0: 0}
    return pl.pallas_call(
        functools.partial(_ffn_kernel, layer=l, final=final),
        grid=(n_tiles,),
        in_specs=in_specs,
        out_specs=out_specs,
        out_shape=out_shape,
        scratch_shapes=[
            pltpu.VMEM((FFN_TILE_M, D_MODEL), F32),
            pltpu.VMEM((FFN_TILE_M, D_FF), BF16),
            pltpu.VMEM((D_MODEL, D_FF), BF16),
            pltpu.VMEM((D_FF, D_MODEL), BF16),
            pltpu.VMEM((FFN_STAGES, D_MODEL, FFN_CHUNK), F32),
            pltpu.VMEM((FFN_STAGES, FFN_CHUNK, D_MODEL), F32),
            pltpu.SemaphoreType.DMA((2, FFN_STAGES)),
        ],
        input_output_aliases=aliases,
        compiler_params=pltpu.CompilerParams(
            dimension_semantics=("arbitrary",), vmem_limit_bytes=VMEM_LIMIT_BYTES),
        name="ffn_final" if final else "ffn",
    )(*args)


def kernel(x_prompt, x_sample, state_C, state_n, state_m, c, c_ctx, w_ada, b_ada, g_norm1, g_norm2, w_in,
           b_gates, w_pool, pool_scale, g_sgu, w_sgu, b_sgu, g_mlstm, w_branch, w_out, w_ff1, w_ff2, g_final):
    w_in16 = w_in.astype(BF16)
    prm = {
        "w_in16": w_in16,
        "w_gate": jnp.pad(w_in16[:, :, N_MAIN:N_MAIN + N_GATE], ((0, 0), (0, 0), (0, GATE_PAD - N_GATE))),
        "w_brg": w_in16[:, :, N_MAIN + N_GATE:],
        "w_kt": jnp.swapaxes(w_in16[:, :, 4 * D_BRANCH:5 * D_BRANCH], 1, 2),
        "w_branch_rows": w_branch.reshape(DEPTH, N_BRANCH * D_BRANCH, D_MODEL),
        "w_out": w_out,
        "w_ada": w_ada,
        "b_ada": b_ada,
        "w_pool": w_pool.astype(BF16),
        "w_sgu": w_sgu.astype(BF16),
        "w_ff1": w_ff1,
        "w_ff2": w_ff2,
        "g_norm1": g_norm1,
        "g_norm2": g_norm2,
        "pool_scale": pool_scale,
        "g_sgu": g_sgu,
        "g_mlstm": g_mlstm,
        "b_gates": b_gates,
        "b_sgu_tile": jnp.repeat(jnp.swapaxes(b_sgu, 1, 2), GROUP, axis=2),
        "cond8": jnp.concatenate(
            [c_ctx[None, :], c, jnp.zeros((COND_ROWS - 1 - DEC_BATCH, D_MODEL), F32)], axis=0),
    }
    lay = {
        "mods": _ada_call(prm["cond8"], w_ada, b_ada),
        "w_branch": prm["w_branch_rows"][0].astype(BF16),
        "w_out": w_out[0].astype(BF16),
    }

    xall = jnp.concatenate([x_prompt.reshape(N_TOK_CTX, D_MODEL), x_sample.reshape(-1, D_MODEL)], axis=0)
    new_states = None
    for l in range(DEPTH):
        xall, *new_states = _mixer_call(l, xall, lay, prm, "ctx", state_outs=new_states)
        (xall,) = _mixer_call(l, xall, lay, prm, "lat", states=(state_C, state_n, state_m))
        if l < DEPTH - 1:
            xall, *nxt = _ffn_call(l, xall, lay, prm, g_final, final=False)
            lay = dict(zip(("mods", "w_branch", "w_out"), nxt))
    y_prompt, y_sample = _ffn_call(DEPTH - 1, xall, lay, prm, g_final, final=True)
    return (y_prompt.reshape(BATCH, SEQ, D_MODEL), y_sample.reshape(DEC_BATCH, DEC_SEQ, D_MODEL), *new_states)
```

```python
import functools

import jax
import jax.numpy as jnp
from jax import lax
from jax.experimental import pallas as pl
from jax.experimental.pallas import tpu as pltpu

F32 = jnp.float32
BF16 = jnp.bfloat16

D_MODEL = 1024
DEPTH = 4
BATCH = 16
SEQ = 256
DEC_BATCH = 2
DEC_SEQ = 1024
GRID_W = 64
D_BRANCH = 512
POOL_WINDOWS = (2, 4, 8, 16)
N_GROUPS = 4
GROUP = D_BRANCH // N_GROUPS
SGU_CHUNK = 128
N_HEADS = 4
HEAD_DIM = D_BRANCH // N_HEADS
N_BRANCH = 3
D_FF = 4 * D_MODEL
EPS = 1e-6

N_MAIN = 7 * D_BRANCH
N_GATE = 4 * N_HEADS
GATE_PAD = 128
N_TOK_CTX = BATCH * SEQ
N_TOK = N_TOK_CTX + DEC_BATCH * DEC_SEQ
COND_ROWS = 8

TILE_M = 256
MLSTM_L = 256
MLSTM_ROW_BLOCK = 128
MERGE_COLS = 256
POOL_PAD = 8
FFN_TILE_M = 512
FFN_CHUNK = 512
FFN_STAGES = 2
ADA_TILE_N = 1536
VMEM_LIMIT_BYTES = 60 * 1024 * 1024


_dot = functools.partial(jnp.dot, preferred_element_type=F32)
_dot_nt = functools.partial(lax.dot_general, dimension_numbers=(((1,), (1,)), ((), ())),
                            preferred_element_type=F32)


def _aligned(offset, align):
    return offset if isinstance(offset, int) else pl.multiple_of(offset, align)


def _ada_block(cond_ref, w_ref, b_ref, out_ref, *, layer=0):
    cond = cond_ref[...]
    s = (cond * jax.nn.sigmoid(cond)).astype(BF16)
    out_ref[...] = _dot(s, w_ref[...].astype(BF16)) + b_ref[layer:layer + 1, :]


def _ada_call(cond8, w_ada, b_ada):
    n_out = 6 * D_MODEL
    return pl.pallas_call(
        _ada_block,
        grid=(n_out // ADA_TILE_N,),
        in_specs=[
            pl.BlockSpec((COND_ROWS, D_MODEL), lambda j: (0, 0)),
            pl.BlockSpec((None, D_MODEL, ADA_TILE_N), lambda j: (0, 0, j)),
            pl.BlockSpec((DEPTH, ADA_TILE_N), lambda j: (0, j)),
        ],
        out_specs=pl.BlockSpec((COND_ROWS, ADA_TILE_N), lambda j: (0, j)),
        out_shape=jax.ShapeDtypeStruct((COND_ROWS, n_out), F32),
        compiler_params=pltpu.CompilerParams(
            dimension_semantics=("arbitrary",), vmem_limit_bytes=VMEM_LIMIT_BYTES),
        name="ada_mod",
    )(cond8, w_ada, b_ada)


def _seg_scan(x, pos, seg, total, op, reverse):
    d = 1
    while d < seg:
        if reverse:
            shifted = pltpu.roll(x, total - d, 1)
            valid = pos < seg - d
        else:
            shifted = pltpu.roll(x, d, 1)
            valid = pos >= d
        if op == "sum":
            x = x + jnp.where(valid, shifted, 0.0)
        else:
            x = jnp.maximum(x, jnp.where(valid, shifted, -jnp.inf))
        d *= 2
    return x


_MIXER_LAYER_ROW_INPUTS = (2, 10, 11, 14, 15)


def _mixer_kernel(*refs, T, grid2d, has_state, layer, n_in, n_pass, n_out):
    ins, outs, scr = list(refs[:n_in - n_pass]), refs[n_in:n_in + n_out], refs[n_in + n_out:]
    ins[1] = ins[1].at[pl.ds(1 + pl.program_id(0) if has_state else 0, 1), :]
    for k in _MIXER_LAYER_ROW_INPUTS:
        ins[k] = ins[k].at[pl.ds(layer, 1), :]
    state_outs = list(outs[1:])
    if state_outs and n_pass == 0:
        for k, o in enumerate(state_outs):
            o[...] = jnp.zeros(o.shape, o.dtype)
            state_outs[k] = o.at[layer]
    for thunk in _mixer_sequence(*ins, outs[0], *state_outs, *scr, T=T, grid2d=grid2d, has_state=has_state):
        thunk()


def _mixer_sequence(*refs, T, grid2d, has_state):
    it = iter(refs)
    x_ref = next(it)
    mod_ref = next(it)
    g1_ref = next(it)
    wmain_ref = next(it)
    wkt_ref = next(it)
    wgate_ref = next(it)
    wbrg_ref = next(it)
    wbranch_ref = next(it)
    wout_ref = next(it)
    wpool_ref = next(it)
    pscale_ref = next(it)
    gsgu_ref = next(it)
    wsgu_ref = next(it)
    bsgu_ref = next(it)
    bgates_ref = next(it)
    gml_ref = next(it)
    if has_state:
        c0_ref, n0_ref, m0_ref = next(it), next(it), next(it)
    out_ref = next(it)
    if not has_state:
        cout_ref, nout_ref, mout_ref = next(it), next(it), next(it)
    (u_scr, xp_scr, su_scr, vn_scr, q_scr, kt_scr, vaug_scr, o_scr, grow_scr, colf_scr,
     h_scr, ya_scr, yb_scr, yc_scr, mab_scr, gc_scr, merged_scr) = [next(it) for _ in range(17)]
    if has_state:
        cn_scr = next(it)
    if grid2d:
        m1_scr, dlt_scr = next(it), next(it)

    L = MLSTM_L
    n_tiles = T // TILE_M
    n_chunks = T // L
    RB = MLSTM_ROW_BLOCK
    NW = MERGE_COLS
    n_merge = D_MODEL // NW

    def tile_rows(i):
        return pl.ds(_aligned(i * TILE_M, TILE_M), TILE_M)

    def a_norm(i):
        mod = mod_ref[...]
        sh1 = mod[:, 0:D_MODEL]
        sc1 = mod[:, D_MODEL:2 * D_MODEL]
        x = x_ref[tile_rows(i), :]
        ms = jnp.mean(x * x, axis=-1, keepdims=True)
        u = x * lax.rsqrt(ms + EPS) * g1_ref[...] * (1.0 + sc1) + sh1
        u_scr[tile_rows(i), :] = u.astype(BF16)

    def a_gates(i):
        lane = lax.broadcasted_iota(jnp.int32, (2 * N_HEADS, TILE_M), 1)
        is_fwd = lax.broadcasted_iota(jnp.int32, (2 * N_HEADS, TILE_M), 0) < N_HEADS
        bias = jnp.concatenate([bgates_ref[...], jnp.zeros((1, GATE_PAD - N_GATE), F32)], axis=1)
        gate_t = (_dot(u_scr[tile_rows(i), :], wgate_ref[...]) + bias).T
        gp = gate_t[0:N_GATE, :]
        i_all = gp[0:2 * N_HEADS, :]
        f_all = gp[2 * N_HEADS:4 * N_HEADS, :]
        logf = jnp.minimum(f_all, 0.0) - jnp.log1p(jnp.exp(-jnp.abs(f_all)))
        b_row = jnp.where(is_fwd, _seg_scan(logf, lane, L, L, "sum", False),
                          _seg_scan(logf, lane, L, L, "sum", True))
        g_row = i_all - b_row
        cm_row = jnp.where(is_fwd, _seg_scan(g_row, lane, L, L, "max", False),
                           _seg_scan(g_row, lane, L, L, "max", True))
        grow_scr[:, tile_rows(i)] = g_row
        stacked = jnp.concatenate(
            [b_row, cm_row, jnp.zeros((GATE_PAD - 4 * N_HEADS, TILE_M), F32)], axis=0)
        colf_scr[tile_rows(i), :] = stacked.T

    def proj(i, k):
        return _dot(u_scr[tile_rows(i), :], wmain_ref[:, k * D_BRANCH:(k + 1) * D_BRANCH])

    def a_xp(i):
        xp_scr[pl.ds(_aligned(i * TILE_M + POOL_PAD, 8), TILE_M), :] = proj(i, 0)

    def a_su(i):
        su_scr[...] = proj(i, 1)

    def a_sv(i):
        sv = proj(i, 2)
        vn = sv * lax.rsqrt(jnp.mean(sv * sv, axis=-1, keepdims=True) + EPS) * gsgu_ref[...]
        vn_scr[...] = vn.astype(BF16)

    def a_q(i):
        q_scr[tile_rows(i), :] = proj(i, 3).astype(BF16)

    def a_k(i):
        kt = _dot_nt(wkt_ref[...], u_scr[tile_rows(i), :])
        kt_scr[:, tile_rows(i)] = (kt * (HEAD_DIM ** -0.5)).astype(BF16)

    def a_v(i):
        v = proj(i, 5).astype(BF16)
        ones_blk = jnp.ones((TILE_M, HEAD_DIM), BF16)
        for h in range(N_HEADS):
            vaug_scr[tile_rows(i), 2 * h * HEAD_DIM:(2 * h + 1) * HEAD_DIM] = v[:, h * HEAD_DIM:(h + 1) * HEAD_DIM]
            vaug_scr[tile_rows(i), (2 * h + 1) * HEAD_DIM:(2 * h + 2) * HEAD_DIM] = ones_blk

    def a_o(i):
        o_scr[tile_rows(i), :] = proj(i, 6)

    def a_sgu(ch, i):
        crow = slice(ch * SGU_CHUNK, (ch + 1) * SGU_CHUNK)
        for g in range(N_GROUPS):
            gc = slice(g * GROUP, (g + 1) * GROUP)
            mixed = _dot(wsgu_ref[g], vn_scr[crow, gc]) + bsgu_ref[:, gc]
            yb = su_scr[crow, gc] * mixed
            yb_scr[pl.ds(_aligned(i * TILE_M + ch * SGU_CHUNK, SGU_CHUNK), SGU_CHUNK), gc] = yb.astype(BF16)

    a_parts = [a_norm, a_gates, a_xp, a_su, a_sv, a_q, a_k, a_v, a_o]
    a_parts += [functools.partial(a_sgu, ch) for ch in range(TILE_M // SGU_CHUNK)]

    def pool_pad():
        zpad = jnp.zeros((POOL_PAD, D_BRANCH), F32)
        xp_scr[0:POOL_PAD, :] = zpad
        xp_scr[T + POOL_PAD:T + 2 * POOL_PAD, :] = zpad

    def pool_seq(g):
        win = POOL_WINDOWS[g]
        gc = slice(g * GROUP, (g + 1) * GROUP)
        half = win // 2
        t_idx = lax.broadcasted_iota(jnp.int32, (T, GROUP), 0)
        total = jnp.zeros((T, GROUP), F32)
        for j in range(-half, win - half):
            total = total + xp_scr[POOL_PAD + j:POOL_PAD + j + T, gc]
        cnt = (jnp.minimum(t_idx + (win - half), T) - jnp.maximum(t_idx - half, 0)).astype(F32)
        dlt = total / cnt - xp_scr[POOL_PAD:POOL_PAD + T, gc]
        ya = _dot(dlt.astype(BF16), wpool_ref[g]) * pscale_ref[...][:, gc]
        ya_scr[:, gc] = ya.astype(BF16)

    def a_colpool(i):
        c_idx = lax.broadcasted_iota(jnp.int32, (GRID_W, GROUP), 0)
        for g, win in enumerate(POOL_WINDOWS):
            gc = slice(g * GROUP, (g + 1) * GROUP)
            half = win // 2
            inv_cnt = 1.0 / (jnp.minimum(c_idx + (win - half), GRID_W) - jnp.maximum(c_idx - half, 0)).astype(F32)
            for r in range(TILE_M // GRID_W):
                r0 = i * TILE_M + r * GRID_W
                xg = xp_scr[pl.ds(_aligned(r0 + POOL_PAD, 8), GRID_W), gc]
                total = xg
                for j in range(-half, win - half):
                    if j != 0:
                        valid = (c_idx + j >= 0) & (c_idx + j < GRID_W)
                        total = total + jnp.where(valid, pltpu.roll(xg, (-j) % GRID_W, 0), 0.0)
                m1_scr[pl.ds(_aligned(r0, GRID_W), GRID_W), gc] = total * inv_cnt

    def pool_grid_rows(g):
        win = POOL_WINDOWS[g]
        gc = slice(g * GROUP, (g + 1) * GROUP)
        half = win // 2
        n_rows = T // GRID_W
        for r in range(n_rows):
            lo = max(r - half, 0)
            hi = min(r + (win - half), n_rows)
            acc = m1_scr[lo * GRID_W:(lo + 1) * GRID_W, gc]
            for rr in range(lo + 1, hi):
                acc = acc + m1_scr[rr * GRID_W:(rr + 1) * GRID_W, gc]
            xg = xp_scr[POOL_PAD + r * GRID_W:POOL_PAD + (r + 1) * GRID_W, gc]
            dlt_scr[r * GRID_W:(r + 1) * GRID_W, :] = acc / float(hi - lo) - xg
        ya = _dot(dlt_scr[...].astype(BF16), wpool_ref[g]) * pscale_ref[...][:, gc]
        ya_scr[:, gc] = ya.astype(BF16)

    if grid2d:
        assert TILE_M % GRID_W == 0
        a_parts.insert(a_parts.index(a_xp) + 1, a_colpool)
        pool_parts = [functools.partial(pool_grid_rows, g) for g in range(N_GROUPS)]
    else:
        pool_parts = [pool_pad] + [functools.partial(pool_seq, g) for g in range(N_GROUPS)]

    def keep_mask(d, rb, col_lo, ncols):
        t_loc = rb * RB + lax.broadcasted_iota(jnp.int32, (RB, ncols), 0)
        s_loc = col_lo + lax.broadcasted_iota(jnp.int32, (RB, ncols), 1)
        return s_loc <= t_loc if d == 0 else s_loc >= t_loc

    def mlstm_prepare(d, c, m_in):
        c0 = _aligned(c * L, L)
        colf = colf_scr[pl.ds(c0, L), :]
        bcol = colf[:, N_HEADS * d:N_HEADS * (d + 1)]
        cmcol = colf[:, 2 * N_HEADS + N_HEADS * d:2 * N_HEADS + N_HEADS * (d + 1)]
        mx = jnp.maximum(m_in, cmcol)
        last = L - 1 if d == 0 else 0
        mx_last = mx[last:last + 1, :]
        return dict(c0=c0, mx=mx, mx_last=mx_last, w_inter=jnp.exp(m_in - mx), e_negm=jnp.exp(-(bcol + mx)),
                    decay=jnp.exp(m_in - mx_last), m_new=bcol[last:last + 1, :] + mx_last)

    def block_cols(d, rb):
        return (0, (rb + 1) * RB) if d == 0 else (rb * RB, L)

    def mlstm_scores(d, p, h):
        c0 = p["c0"]
        hc = slice(h * HEAD_DIM, (h + 1) * HEAD_DIM)
        qks = []
        for rb in range(L // RB):
            col_lo, col_hi = block_cols(d, rb)
            qb = q_scr[pl.ds(_aligned(c0 + rb * RB, RB), RB), hc]
            qks.append(_dot(qb, kt_scr[hc, pl.ds(_aligned(c0 + col_lo, RB), col_hi - col_lo)]))
        p["qk", h] = qks

    def mlstm_head(d, p, h):
        c0, mx, mx_last = p["c0"], p["mx"], p["mx_last"]
        rows = pl.ds(c0, L)
        hc = slice(h * HEAD_DIM, (h + 1) * HEAD_DIM)
        ac = slice(2 * h * HEAD_DIM, (2 * h + 2) * HEAD_DIM)
        grow = grow_scr[N_HEADS * d + h:N_HEADS * d + h + 1, rows]
        if has_state:
            cn_dh = cn_scr[d, h]
            cn_bf = cn_dh.astype(BF16)
        for rb in range(L // RB):
            col_lo, col_hi = block_cols(d, rb)
            ncols = col_hi - col_lo
            trow = slice(rb * RB, (rb + 1) * RB)
            rrows = pl.ds(_aligned(c0 + rb * RB, RB), RB)
            crows = pl.ds(_aligned(c0 + col_lo, RB), ncols)
            qb = q_scr[rrows, hc]
            qk = p["qk", h][rb]
            w = jnp.exp(grow[:, col_lo:col_hi] - mx[trow, h:h + 1])
            s = (qk * jnp.where(keep_mask(d, rb, col_lo, ncols), w, 0.0)).astype(BF16)
            na = _dot(s, vaug_scr[crows, ac])
            if has_state:
                na = na + p["w_inter"][trow, h:h + 1] * _dot(qb, cn_bf)
            hval = na[:, 0:HEAD_DIM] / jnp.maximum(jnp.abs(na[:, HEAD_DIM:]), p["e_negm"][trow, h:h + 1])
            if d == 0:
                h_scr[rrows, hc] = hval
            else:
                h_scr[rrows, hc] += hval
        w_state = jnp.exp(grow - mx_last[:, h:h + 1])
        kw_t = (kt_scr[hc, rows].astype(F32) * w_state).astype(BF16)
        upd = _dot(kw_t, vaug_scr[rows, ac])
        if has_state:
            cn_scr[d, h] = p["decay"][:, h:h + 1] * cn_dh + upd
        else:
            cout_ref[d, h] = upd[:, 0:HEAD_DIM]
            nout_ref[d, h:h + 1, :] = upd[:, HEAD_DIM:].T[0:1, :]

    def merge_ab(c, nj):
        nc = slice(nj * NW, (nj + 1) * NW)
        ub = u_scr[tile_rows(c), :]
        acc = None
        for r, y_scr in enumerate((ya_scr, yb_scr)):
            gate = jax.nn.sigmoid(_dot(ub, wbrg_ref[:, r * D_MODEL + nj * NW:r * D_MODEL + (nj + 1) * NW]))
            term = gate * _dot(y_scr[tile_rows(c), :], wbranch_ref[r * D_BRANCH:(r + 1) * D_BRANCH, nc])
            acc = term if acc is None else acc + term
        mab_scr[tile_rows(c), nc] = acc

    def gate_c(c, nj):
        w = wbrg_ref[:, 2 * D_MODEL + nj * NW:2 * D_MODEL + (nj + 1) * NW]
        gc_scr[:, nj * NW:(nj + 1) * NW] = jax.nn.sigmoid(_dot(u_scr[tile_rows(c), :], w))

    def tail_yc(c):
        gml = gml_ref[...]
        for h in range(N_HEADS):
            hc = slice(h * HEAD_DIM, (h + 1) * HEAD_DIM)
            hh = h_scr[tile_rows(c), hc]
            hn = hh * lax.rsqrt(jnp.mean(hh * hh, axis=-1, keepdims=True) + EPS) * gml[:, hc]
            yc_scr[:, hc] = (jax.nn.sigmoid(o_scr[tile_rows(c), hc]) * hn).astype(BF16)

    def tail_merge(c, nj):
        nc = slice(nj * NW, (nj + 1) * NW)
        acc = mab_scr[tile_rows(c), nc] + gc_scr[:, nc] * _dot(yc_scr[...], wbranch_ref[2 * D_BRANCH:, nc])
        merged_scr[:, nc] = acc.astype(BF16)

    def tail_out(c):
        gt1 = mod_ref[...][:, 2 * D_MODEL:3 * D_MODEL]
        out_ref[tile_rows(c), :] = x_ref[tile_rows(c), :] + gt1 * _dot(merged_scr[...], wout_ref[...])

    def tail_parts(c):
        return ([functools.partial(tail_yc, c)] + [functools.partial(tail_merge, c, nj) for nj in range(n_merge)]
                + [functools.partial(tail_out, c)])

    def scan_parts(d, c, m_in, fillers, out):
        p = {}

        def prepare():
            p.update(mlstm_prepare(d, c, m_in))
            out[d] = p["m_new"]

        parts = [prepare]
        for h in range(N_HEADS):
            parts.append(functools.partial(mlstm_scores, d, p, h))
            if h < len(fillers):
                parts.append(fillers[h])
            parts.append(functools.partial(mlstm_head, d, p, h))
        return parts + list(fillers[N_HEADS:])

    if n_tiles == 1 and n_chunks == 1:
        zeros = jnp.zeros((1, N_HEADS), F32)
        m_out = {}
        parts = [functools.partial(part, 0) for part in a_parts] + pool_parts
        parts += scan_parts(0, 0, zeros, [functools.partial(merge_ab, 0, nj) for nj in range(n_merge)], m_out)
        parts += scan_parts(1, 0, zeros, [functools.partial(gate_c, 0, nj) for nj in range(n_merge)], m_out)
        parts += tail_parts(0)

        def write_m():
            mout_ref[0:1, :] = m_out[0]
            mout_ref[1:2, :] = m_out[1]

        return parts + [write_m]

    def run_loops():
        def phase_a(i, carry):
            for part in a_parts:
                part(i)
            return carry

        lax.fori_loop(0, n_tiles, phase_a, 0)
        for part in pool_parts:
            part()
        for d in range(2):
            for h in range(N_HEADS):
                n_rows = jnp.broadcast_to(n0_ref[d, h:h + 1, :], (HEAD_DIM, HEAD_DIM))
                cn_scr[d, h] = jnp.concatenate([c0_ref[d, h], n_rows.T], axis=1)
        m_init = m0_ref[...]

        def fwd_step(j, m_in):
            m_out = {}
            for part in scan_parts(0, j, m_in, [functools.partial(merge_ab, j, nj) for nj in range(n_merge)], m_out):
                part()
            return m_out[0]

        def bwd_step(j, m_in):
            c = n_chunks - 1 - j
            m_out = {}
            for part in scan_parts(1, c, m_in, [functools.partial(gate_c, c, nj) for nj in range(n_merge)], m_out):
                part()
            for part in tail_parts(c):
                part()
            return m_out[1]

        lax.fori_loop(0, n_chunks, fwd_step, m_init[0:1, :])
        lax.fori_loop(0, n_chunks, bwd_step, m_init[1:2, :])

    assert has_state, "multi-chunk sequences carry the cached state"
    return [run_loops]


def _mixer_call(l, xall, lay, prm, stream, states=None, state_outs=None):
    ctx = stream == "ctx"
    T = SEQ if ctx else DEC_SEQ
    assert T % TILE_M == 0 and TILE_M == MLSTM_L and MLSTM_L % MLSTM_ROW_BLOCK == 0
    n_b = BATCH if ctx else DEC_BATCH
    blk0 = 0 if ctx else N_TOK_CTX // T
    once = pl.Buffered(1)

    def wspec(shape):
        nd = len(shape)
        return pl.BlockSpec((None,) + shape, lambda b: (l,) + (0,) * nd, pipeline_mode=once)

    def lspec(shape):
        return pl.BlockSpec(shape, lambda b: (0,) * len(shape), pipeline_mode=once)

    x_spec = pl.BlockSpec((T, D_MODEL), lambda b: (blk0 + b, 0), pipeline_mode=None if ctx else once)
    in_specs = [
        x_spec,
        lspec((COND_ROWS, 6 * D_MODEL)),
        lspec((DEPTH, D_MODEL)),
        wspec((D_MODEL, N_MAIN)),
        wspec((D_BRANCH, D_MODEL)),
        wspec((D_MODEL, GATE_PAD)),
        lspec((D_MODEL, N_BRANCH * D_MODEL)),
        lspec((N_BRANCH * D_BRANCH, D_MODEL)),
        lspec((D_MODEL, D_MODEL)),
        wspec((N_GROUPS, GROUP, GROUP)),
        lspec((DEPTH, D_BRANCH)),
        lspec((DEPTH, D_BRANCH)),
        wspec((N_GROUPS, SGU_CHUNK, SGU_CHUNK)),
        wspec((SGU_CHUNK, D_BRANCH)),
        lspec((DEPTH, N_GATE)),
        lspec((DEPTH, D_BRANCH)),
    ]
    args = [xall, lay["mods"], prm["g_norm1"], prm["w_in16"], prm["w_kt"],
            prm["w_gate"], lay["w_brg"], lay["w_branch"], lay["w_out"],
            prm["w_pool"], prm["pool_scale"], prm["g_sgu"], prm["w_sgu"],
            prm["b_sgu_tile"], prm["b_gates"], prm["g_mlstm"]]
    out_shape = [jax.ShapeDtypeStruct(xall.shape, xall.dtype)]
    out_specs = [x_spec]
    aliases = {0: 0}
    passthrough = []
    if ctx:
        per_seq = [(2, N_HEADS, HEAD_DIM, HEAD_DIM), (2, N_HEADS, HEAD_DIM), (2, N_HEADS)]
        out_shape += [jax.ShapeDtypeStruct((BATCH, DEPTH) + s, F32) for s in per_seq]
        if state_outs is None:
            out_specs += [pl.BlockSpec((None, DEPTH) + s, lambda b, nd=len(s): (b,) + (0,) * (nd + 1))
                          for s in per_seq]
        else:
            out_specs += [pl.BlockSpec((None, None) + s, lambda b, nd=len(s): (b, l) + (0,) * nd) for s in per_seq]
            passthrough += [(a, 1 + k) for k, a in enumerate(state_outs)]
    else:
        in_specs += [
            pl.BlockSpec((None, None, 2, N_HEADS, HEAD_DIM, HEAD_DIM), lambda b: (b, l, 0, 0, 0, 0)),
            pl.BlockSpec((None, None, 2, N_HEADS, HEAD_DIM), lambda b: (b, l, 0, 0, 0)),
            pl.BlockSpec((None, None, 2, N_HEADS), lambda b: (b, l, 0, 0)),
        ]
        args += list(states)
    for buf, out_idx in passthrough:
        aliases[len(args)] = out_idx
        in_specs.append(pl.BlockSpec(memory_space=pl.ANY))
        args.append(buf)

    scratch = [
        pltpu.VMEM((T, D_MODEL), BF16),
        pltpu.VMEM((T + 2 * POOL_PAD, D_BRANCH), F32),
        pltpu.VMEM((TILE_M, D_BRANCH), F32),
        pltpu.VMEM((TILE_M, D_BRANCH), BF16),
        pltpu.VMEM((T, D_BRANCH), BF16),
        pltpu.VMEM((D_BRANCH, T), BF16),
        pltpu.VMEM((T, 2 * D_BRANCH), BF16),
        pltpu.VMEM((T, D_BRANCH), F32),
        pltpu.VMEM((2 * N_HEADS, T), F32),
        pltpu.VMEM((T, GATE_PAD), F32),
        pltpu.VMEM((T, D_BRANCH), F32),
        pltpu.VMEM((T, D_BRANCH), BF16),
        pltpu.VMEM((T, D_BRANCH), BF16),
        pltpu.VMEM((TILE_M, D_BRANCH), BF16),
        pltpu.VMEM((T, D_MODEL), F32),
        pltpu.VMEM((TILE_M, D_MODEL), F32),
        pltpu.VMEM((TILE_M, D_MODEL), BF16),
    ]
    if not ctx:
        scratch += [pltpu.VMEM((2, N_HEADS, HEAD_DIM, 2 * HEAD_DIM), F32)]
        scratch += [pltpu.VMEM((T, D_BRANCH), F32), pltpu.VMEM((T, GROUP), F32)]

    outs = pl.pallas_call(
        functools.partial(_mixer_kernel, T=T, grid2d=not ctx, has_state=not ctx, layer=l,
                          n_in=len(in_specs), n_pass=len(passthrough), n_out=len(out_specs)),
        grid=(n_b,),
        in_specs=in_specs,
        out_specs=out_specs,
        out_shape=out_shape,
        scratch_shapes=scratch,
        input_output_aliases=aliases,
        compiler_params=pltpu.CompilerParams(
            dimension_semantics=("arbitrary",), vmem_limit_bytes=VMEM_LIMIT_BYTES),
        name="mixer_ctx" if ctx else "mixer_lat",
    )(*args)
    return outs


def _ffn_kernel(x_ref, mod_ref, g2_ref, w1_hbm, w2_hbm, gfin_ref, *rest, layer, final):
    if final:
        out_refs, rest = rest[:2], rest[2:]
    else:
        (cond_ref, wada_ref, bada_ref, wbr_ref, wo_ref, win_ref), rest = rest[:6], rest[6:]
        out_refs, (mods_out, wbr_out, wo_out, wbrg_out), rest = rest[:1], rest[1:5], rest[5:]
    acc_scr, h_scr, w1_scr, w2_scr, st1_scr, st2_scr, sem = rest
    n_chunks = D_FF // FFN_CHUNK
    tile = pl.program_id(0)
    first_tile = tile == 0
    ctx_tiles, lat_tiles = N_TOK_CTX // FFN_TILE_M, DEC_SEQ // FFN_TILE_M
    mod_ref = mod_ref.at[pl.ds(jnp.where(tile < ctx_tiles, 0, 1 + (tile - ctx_tiles) // lat_tiles), 1), :]
    g2_ref = g2_ref.at[pl.ds(layer, 1), :]

    def w1_copy(c):
        slot = c % FFN_STAGES
        c0 = _aligned(c * FFN_CHUNK, FFN_CHUNK)
        return pltpu.make_async_copy(w1_hbm.at[layer, :, pl.ds(c0, FFN_CHUNK)], st1_scr.at[slot], sem.at[0, slot])

    def w2_copy(c):
        slot = c % FFN_STAGES
        c0 = _aligned(c * FFN_CHUNK, FFN_CHUNK)
        return pltpu.make_async_copy(w2_hbm.at[layer, pl.ds(c0, FFN_CHUNK), :], st2_scr.at[slot], sem.at[1, slot])

    def land_w1(c):
        w1_copy(c).wait()
        c0 = _aligned(c * FFN_CHUNK, FFN_CHUNK)
        w1_scr[:, pl.ds(c0, FFN_CHUNK)] = st1_scr[c % FFN_STAGES].astype(BF16)

    def land_w2(c):
        w2_copy(c).wait()
        c0 = _aligned(c * FFN_CHUNK, FFN_CHUNK)
        w2_scr[pl.ds(c0, FFN_CHUNK), :] = st2_scr[c % FFN_STAGES].astype(BF16)

    def prepare_next_layer():
        if not final:
            _ada_block(cond_ref, wada_ref, bada_ref, mods_out, layer=layer + 1)
            wbr_out[...] = wbr_ref[...].astype(BF16)
            wo_out[...] = wo_ref[...].astype(BF16)
            n_br = N_BRANCH * D_MODEL
            for c0 in range(0, n_br, FFN_CHUNK):
                width = min(FFN_CHUNK + GATE_PAD, n_br + N_GATE - c0)
                window = win_ref[:, N_MAIN + c0:N_MAIN + c0 + width]
                wbrg_out[:, c0:c0 + FFN_CHUNK] = window[:, N_GATE:N_GATE + FFN_CHUNK]

    def normed_input():
        mod = mod_ref[...]
        sh2 = mod[:, 3 * D_MODEL:4 * D_MODEL]
        sc2 = mod[:, 4 * D_MODEL:5 * D_MODEL]
        x = x_ref[...]
        ms = jnp.mean(x * x, axis=-1, keepdims=True)
        return (x * lax.rsqrt(ms + EPS) * g2_ref[...] * (1.0 + sc2) + sh2).astype(BF16)

    def hidden(ub, c):
        c0 = _aligned(c * FFN_CHUNK, FFN_CHUNK)
        h = _dot(ub, w1_scr[:, pl.ds(c0, FFN_CHUNK)])
        h_scr[:, pl.ds(c0, FFN_CHUNK)] = jnp.square(jnp.maximum(h, 0.0)).astype(BF16)

    def finish(ffn_out):
        gt2 = mod_ref[...][:, 5 * D_MODEL:6 * D_MODEL]
        y = x_ref[...] + gt2 * ffn_out
        if not final:
            out_refs[0][...] = y
            return
        y = y * lax.rsqrt(jnp.mean(y * y, axis=-1, keepdims=True) + EPS) * gfin_ref[...]
        in_ctx = tile < ctx_tiles

        @pl.when(in_ctx)
        def _():
            out_refs[0][...] = y

        @pl.when(jnp.logical_not(in_ctx))
        def _():
            out_refs[1][...] = y

    @pl.when(first_tile)
    def _():
        for c in range(FFN_STAGES):
            w1_copy(c).start()
            w2_copy(c).start()
        prepare_next_layer()
        ub = normed_input()
        acc_scr[...] = jnp.zeros_like(acc_scr)

        def chunk(c, carry):
            land_w1(c)
            land_w2(c)

            @pl.when(c + FFN_STAGES < n_chunks)
            def _():
                w1_copy(c + FFN_STAGES).start()
                w2_copy(c + FFN_STAGES).start()

            hidden(ub, c)
            c0 = _aligned(c * FFN_CHUNK, FFN_CHUNK)
            acc_scr[...] += _dot(h_scr[:, pl.ds(c0, FFN_CHUNK)], w2_scr[pl.ds(c0, FFN_CHUNK), :])
            return carry

        lax.fori_loop(0, n_chunks, chunk, 0)
        finish(acc_scr[...])

    @pl.when(jnp.logical_not(first_tile))
    def _():
        prepare_next_layer()
        ub = normed_input()
        for c in range(n_chunks):
            hidden(ub, c)
        finish(_dot(h_scr[...], w2_scr[...]))


def _ffn_call(l, xall, lay, prm, g_final, final):
    n_tiles = N_TOK // FFN_TILE_M
    ctx_tiles = N_TOK_CTX // FFN_TILE_M
    lat_tiles = DEC_SEQ // FFN_TILE_M
    once = pl.Buffered(1)

    x_spec = pl.BlockSpec((FFN_TILE_M, D_MODEL), lambda t: (t, 0))
    in_specs = [
        x_spec,
        pl.BlockSpec((COND_ROWS, 6 * D_MODEL), lambda t: (0, 0), pipeline_mode=once),
        pl.BlockSpec((DEPTH, D_MODEL), lambda t: (0, 0), pipeline_mode=once),
        pl.BlockSpec(memory_space=pl.ANY),
        pl.BlockSpec(memory_space=pl.ANY),
        pl.BlockSpec((1, D_MODEL), lambda t: (0, 0)),
    ]
    args = [xall, lay["mods"], prm["g_norm2"], prm["w_ff1"], prm["w_ff2"], g_final.reshape(1, D_MODEL)]
    if final:
        out_specs = [
            pl.BlockSpec((FFN_TILE_M, D_MODEL), lambda t: (jnp.minimum(t, ctx_tiles - 1), 0)),
            pl.BlockSpec((FFN_TILE_M, D_MODEL), lambda t: (jnp.maximum(t - ctx_tiles, 0), 0)),
        ]
        out_shape = [jax.ShapeDtypeStruct((N_TOK_CTX, D_MODEL), F32),
                     jax.ShapeDtypeStruct((N_TOK - N_TOK_CTX, D_MODEL), F32)]
        aliases = {}
    else:
        ada_cols = 6 * D_MODEL // n_tiles
        br_rows = N_BRANCH * D_BRANCH // n_tiles
        wo_tiles = D_MODEL // br_rows
        assert ada_cols % 128 == 0 and br_rows % 16 == 0 and wo_tiles <= n_tiles
        wo_map = lambda t: (jnp.minimum(t, wo_tiles - 1), 0)
        in_specs += [
            pl.BlockSpec((COND_ROWS, D_MODEL), lambda t: (0, 0)),
            pl.BlockSpec((None, D_MODEL, ada_cols), lambda t: (l + 1, 0, t)),
            pl.BlockSpec((DEPTH, ada_cols), lambda t: (0, t)),
            pl.BlockSpec((None, br_rows, D_MODEL), lambda t: (l + 1, t, 0)),
            pl.BlockSpec((None, br_rows, D_MODEL), lambda t: (l + 1,) + wo_map(t)),
            pl.BlockSpec((None, br_rows, prm["w_in16"].shape[-1]), lambda t: (l + 1,) + wo_map(t)),
        ]
        args += [prm["cond8"], prm["w_ada"], prm["b_ada"], prm["w_branch_rows"], prm["w_out"], prm["w_in16"]]
        out_specs = [
            x_spec,
            pl.BlockSpec((COND_ROWS, ada_cols), lambda t: (0, t)),
            pl.BlockSpec((br_rows, D_MODEL), lambda t: (t, 0)),
            pl.BlockSpec((br_rows, D_MODEL), wo_map),
            pl.BlockSpec((br_rows, N_BRANCH * D_MODEL), wo_map),
        ]
        out_shape = [jax.ShapeDtypeStruct(xall.shape, xall.dtype),
                     jax.ShapeDtypeStruct((COND_ROWS, 6 * D_MODEL), F32),
                     jax.ShapeDtypeStruct((N_BRANCH * D_BRANCH, D_MODEL), BF16),
                     jax.ShapeDtypeStruct((D_MODEL, D_MODEL), BF16),
                     jax.ShapeDtypeStruct((D_MODEL, N_BRANCH * D_MODEL), BF16)]
        aliases = {0: 0}
    return pl.pallas_call(
        functools.partial(_ffn_kernel, layer=l, final=final),
        grid=(n_tiles,),
        in_specs=in_specs,
        out_specs=out_specs,
        out_shape=out_shape,
        scratch_shapes=[
            pltpu.VMEM((FFN_TILE_M, D_MODEL), F32),
            pltpu.VMEM((FFN_TILE_M, D_FF), BF16),
            pltpu.VMEM((D_MODEL, D_FF), BF16),
            pltpu.VMEM((D_FF, D_MODEL), BF16),
            pltpu.VMEM((FFN_STAGES, D_MODEL, FFN_CHUNK), F32),
            pltpu.VMEM((FFN_STAGES, FFN_CHUNK, D_MODEL), F32),
            pltpu.SemaphoreType.DMA((2, FFN_STAGES)),
        ],
        input_output_aliases=aliases,
        compiler_params=pltpu.CompilerParams(
            dimension_semantics=("arbitrary",), vmem_limit_bytes=VMEM_LIMIT_BYTES),
        name="ffn_final" if final else "ffn",
    )(*args)


def kernel(x_prompt, x_sample, state_C, state_n, state_m, c, c_ctx, w_ada, b_ada, g_norm1, g_norm2, w_in,
           b_gates, w_pool, pool_scale, g_sgu, w_sgu, b_sgu, g_mlstm, w_branch, w_out, w_ff1, w_ff2, g_final):
    w_in16 = w_in.astype(BF16)
    prm = {
        "w_in16": w_in16,
        "w_gate": jnp.pad(w_in16[:, :, N_MAIN:N_MAIN + N_GATE], ((0, 0), (0, 0), (0, GATE_PAD - N_GATE))),
        "w_kt": jnp.swapaxes(w_in16[:, :, 4 * D_BRANCH:5 * D_BRANCH], 1, 2),
        "w_branch_rows": w_branch.reshape(DEPTH, N_BRANCH * D_BRANCH, D_MODEL),
        "w_out": w_out,
        "w_ada": w_ada,
        "b_ada": b_ada,
        "w_pool": w_pool.astype(BF16),
        "w_sgu": w_sgu.astype(BF16),
        "w_ff1": w_ff1,
        "w_ff2": w_ff2,
        "g_norm1": g_norm1,
        "g_norm2": g_norm2,
        "pool_scale": pool_scale,
        "g_sgu": g_sgu,
        "g_mlstm": g_mlstm,
        "b_gates": b_gates,
        "b_sgu_tile": jnp.repeat(jnp.swapaxes(b_sgu, 1, 2), GROUP, axis=2),
        "cond8": jnp.concatenate(
            [c_ctx[None, :], c, jnp.zeros((COND_ROWS - 1 - DEC_BATCH, D_MODEL), F32)], axis=0),
    }
    lay = {
        "mods": _ada_call(prm["cond8"], w_ada, b_ada),
        "w_branch": prm["w_branch_rows"][0].astype(BF16),
        "w_out": w_out[0].astype(BF16),
        "w_brg": w_in16[0, :, N_MAIN + N_GATE:],
    }

    xall = jnp.concatenate([x_prompt.reshape(N_TOK_CTX, D_MODEL), x_sample.reshape(-1, D_MODEL)], axis=0)
    new_states = None
    for l in range(DEPTH):
        xall, *new_states = _mixer_call(l, xall, lay, prm, "ctx", state_outs=new_states)
        (xall,) = _mixer_call(l, xall, lay, prm, "lat", states=(state_C, state_n, state_m))
        if l < DEPTH - 1:
            xall, *nxt = _ffn_call(l, xall, lay, prm, g_final, final=False)
            lay = dict(zip(("mods", "w_branch", "w_out", "w_brg"), nxt))
    y_prompt, y_sample = _ffn_call(DEPTH - 1, xall, lay, prm, g_final, final=True)
    return (y_prompt.reshape(BATCH, SEQ, D_MODEL), y_sample.reshape(DEC_BATCH, DEC_SEQ, D_MODEL), *new_states)
```

```python
import functools
import math

import jax
import jax.numpy as jnp
from jax import lax
from jax.experimental import pallas as pl
from jax.experimental.pallas import tpu as pltpu

F32 = jnp.float32
BF16 = jnp.bfloat16

D_MODEL = 1024
DEPTH = 4
BATCH = 16
SEQ = 256
DEC_BATCH = 2
DEC_SEQ = 1024
GRID_W = 64
D_BRANCH = 512
POOL_WINDOWS = (2, 4, 8, 16)
N_GROUPS = 4
GROUP = D_BRANCH // N_GROUPS
SGU_CHUNK = 128
N_HEADS = 4
HEAD_DIM = D_BRANCH // N_HEADS
N_BRANCH = 3
D_FF = 4 * D_MODEL
EPS = 1e-6

LANES = 128
BF16_SUBLANES = 16

N_MAIN = 7 * D_BRANCH
N_GATE = 4 * N_HEADS
GATE_PAD = LANES
N_TOK_CTX = BATCH * SEQ
N_TOK = N_TOK_CTX + DEC_BATCH * DEC_SEQ
COND_ROWS = 8

TILE_M = 256
MLSTM_L = 256
MLSTM_ROW_BLOCK = 128
MERGE_COLS = 256
POOL_PAD = 8
FFN_TILE_M = 512
FFN_CHUNK = 512
FFN_STAGES = 2
ADA_TILE_N = 1536
V7X_VMEM_BYTES = 64 * 1024 * 1024
VMEM_COMPILER_RESERVE = 8 * 1024 * 1024


def _vmem_request(windows, scratch):
    total = VMEM_COMPILER_RESERVE
    for arr, spec in windows:
        if spec.block_shape is not None:
            n_buf = 2 if spec.pipeline_mode is None else spec.pipeline_mode.buffer_count
            total += n_buf * math.prod(d for d in spec.block_shape if d is not None) * jnp.dtype(arr.dtype).itemsize
    for buf in scratch:
        total += math.prod(buf.shape) * jnp.dtype(buf.dtype).itemsize
    return min(total, V7X_VMEM_BYTES)


_dot =functools.partial(jnp.dot, preferred_element_type=F32)
_dot_nt = functools.partial(lax.dot_general, dimension_numbers=(((1,), (1,)), ((), ())),
                            preferred_element_type=F32)


def _aligned(offset, align):
    return offset if isinstance(offset, int) else pl.multiple_of(offset, align)


def _ada_block(cond_ref, w_ref, b_ref, out_ref, *, layer=0):
    cond = cond_ref[...]
    s = (cond * jax.nn.sigmoid(cond)).astype(BF16)
    out_ref[...] = _dot(s, w_ref[...].astype(BF16)) + b_ref[layer:layer + 1, :]


def _ada_call(cond8, w_ada, b_ada):
    n_out = 6 * D_MODEL
    args = (cond8, w_ada, b_ada)
    in_specs = [
        pl.BlockSpec((COND_ROWS, D_MODEL), lambda j: (0, 0)),
        pl.BlockSpec((None, D_MODEL, ADA_TILE_N), lambda j: (0, 0, j)),
        pl.BlockSpec((DEPTH, ADA_TILE_N), lambda j: (0, j)),
    ]
    out_spec = pl.BlockSpec((COND_ROWS, ADA_TILE_N), lambda j: (0, j))
    out_shape = jax.ShapeDtypeStruct((COND_ROWS, n_out), F32)
    return pl.pallas_call(
        _ada_block,
        grid=(n_out // ADA_TILE_N,),
        in_specs=in_specs,
        out_specs=out_spec,
        out_shape=out_shape,
        compiler_params=pltpu.CompilerParams(
            dimension_semantics=("arbitrary",),
            vmem_limit_bytes=_vmem_request(list(zip(args, in_specs)) + [(out_shape, out_spec)], [])),
        name="ada_mod",
    )(*args)


def _seg_scan(x, pos, seg, total, op, reverse):
    d = 1
    while d < seg:
        if reverse:
            shifted = pltpu.roll(x, total - d, 1)
            valid = pos < seg - d
        else:
            shifted = pltpu.roll(x, d, 1)
            valid = pos >= d
        if op == "sum":
            x = x + jnp.where(valid, shifted, 0.0)
        else:
            x = jnp.maximum(x, jnp.where(valid, shifted, -jnp.inf))
        d *= 2
    return x


_MIXER_LAYER_ROW_INPUTS = (2, 10, 11, 14, 15)


def _mixer_kernel(*refs, T, grid2d, has_state, layer, n_in, n_pass, n_out):
    ins, outs, scr = list(refs[:n_in - n_pass]), refs[n_in:n_in + n_out], refs[n_in + n_out:]
    ins[1] = ins[1].at[pl.ds(1 + pl.program_id(0) if has_state else 0, 1), :]
    for k in _MIXER_LAYER_ROW_INPUTS:
        ins[k] = ins[k].at[pl.ds(layer, 1), :]
    state_outs = list(outs[1:])
    if state_outs and n_pass == 0:
        for k, o in enumerate(state_outs):
            o[...] = jnp.zeros(o.shape, o.dtype)
            state_outs[k] = o.at[layer]
    for thunk in _mixer_sequence(*ins, outs[0], *state_outs, *scr, T=T, grid2d=grid2d, has_state=has_state):
        thunk()


def _mixer_sequence(*refs, T, grid2d, has_state):
    it = iter(refs)
    x_ref = next(it)
    mod_ref = next(it)
    g1_ref = next(it)
    wmain_ref = next(it)
    wkt_ref = next(it)
    wgate_ref = next(it)
    wbrg_ref = next(it)
    wbranch_ref = next(it)
    wout_ref = next(it)
    wpool_ref = next(it)
    pscale_ref = next(it)
    gsgu_ref = next(it)
    wsgu_ref = next(it)
    bsgu_ref = next(it)
    bgates_ref = next(it)
    gml_ref = next(it)
    if has_state:
        c0_ref, n0_ref, m0_ref = next(it), next(it), next(it)
    out_ref = next(it)
    if not has_state:
        cout_ref, nout_ref, mout_ref = next(it), next(it), next(it)
    (u_scr, xp_scr, su_scr, vn_scr, q_scr, kt_scr, vaug_scr, o_scr, grow_scr, colf_scr,
     h_scr, ya_scr, yb_scr, yc_scr, mab_scr, gc_scr, merged_scr) = [next(it) for _ in range(17)]
    if has_state:
        cn_scr = next(it)
    if grid2d:
        m1_scr, dlt_scr = next(it), next(it)

    L = MLSTM_L
    n_tiles = T // TILE_M
    n_chunks = T // L
    RB = MLSTM_ROW_BLOCK
    NW = MERGE_COLS
    n_merge = D_MODEL // NW

    def tile_rows(i):
        return pl.ds(_aligned(i * TILE_M, TILE_M), TILE_M)

    def a_norm(i):
        mod = mod_ref[...]
        sh1 = mod[:, 0:D_MODEL]
        sc1 = mod[:, D_MODEL:2 * D_MODEL]
        x = x_ref[tile_rows(i), :]
        ms = jnp.mean(x * x, axis=-1, keepdims=True)
        u = x * lax.rsqrt(ms + EPS) * g1_ref[...] * (1.0 + sc1) + sh1
        u_scr[tile_rows(i), :] = u.astype(BF16)

    def a_gates(i):
        lane = lax.broadcasted_iota(jnp.int32, (2 * N_HEADS, TILE_M), 1)
        is_fwd = lax.broadcasted_iota(jnp.int32, (2 * N_HEADS, TILE_M), 0) < N_HEADS
        bias = jnp.concatenate([bgates_ref[...], jnp.zeros((1, GATE_PAD - N_GATE), F32)], axis=1)
        gate_t = (_dot(u_scr[tile_rows(i), :], wgate_ref[...]) + bias).T
        gp = gate_t[0:N_GATE, :]
        i_all = gp[0:2 * N_HEADS, :]
        f_all = gp[2 * N_HEADS:4 * N_HEADS, :]
        logf = jnp.minimum(f_all, 0.0) - jnp.log1p(jnp.exp(-jnp.abs(f_all)))
        b_row = jnp.where(is_fwd, _seg_scan(logf, lane, L, L, "sum", False),
                          _seg_scan(logf, lane, L, L, "sum", True))
        g_row = i_all - b_row
        cm_row = jnp.where(is_fwd, _seg_scan(g_row, lane, L, L, "max", False),
                           _seg_scan(g_row, lane, L, L, "max", True))
        grow_scr[:, tile_rows(i)] = g_row
        stacked = jnp.concatenate(
            [b_row, cm_row, jnp.zeros((GATE_PAD - 4 * N_HEADS, TILE_M), F32)], axis=0)
        colf_scr[tile_rows(i), :] = stacked.T

    def proj(i, k):
        return _dot(u_scr[tile_rows(i), :], wmain_ref[:, k * D_BRANCH:(k + 1) * D_BRANCH])

    def a_xp(i):
        xp_scr[pl.ds(_aligned(i * TILE_M + POOL_PAD, 8), TILE_M), :] = proj(i, 0)

    def a_su(i):
        su_scr[...] = proj(i, 1)

    def a_sv(i):
        sv = proj(i, 2)
        vn = sv * lax.rsqrt(jnp.mean(sv * sv, axis=-1, keepdims=True) + EPS) * gsgu_ref[...]
        vn_scr[...] = vn.astype(BF16)

    def a_q(i):
        q_scr[tile_rows(i), :] = proj(i, 3).astype(BF16)

    def a_k(i):
        kt = _dot_nt(wkt_ref[...], u_scr[tile_rows(i), :])
        kt_scr[:, tile_rows(i)] = (kt * (HEAD_DIM ** -0.5)).astype(BF16)

    def a_v(i):
        v = proj(i, 5).astype(BF16)
        ones_blk = jnp.ones((TILE_M, HEAD_DIM), BF16)
        for h in range(N_HEADS):
            vaug_scr[tile_rows(i), 2 * h * HEAD_DIM:(2 * h + 1) * HEAD_DIM] = v[:, h * HEAD_DIM:(h + 1) * HEAD_DIM]
            vaug_scr[tile_rows(i), (2 * h + 1) * HEAD_DIM:(2 * h + 2) * HEAD_DIM] = ones_blk

    def a_o(i):
        o_scr[tile_rows(i), :] = proj(i, 6)

    def a_sgu(ch, i):
        crow = slice(ch * SGU_CHUNK, (ch + 1) * SGU_CHUNK)
        for g in range(N_GROUPS):
            gc = slice(g * GROUP, (g + 1) * GROUP)
            mixed = _dot(wsgu_ref[g], vn_scr[crow, gc]) + bsgu_ref[:, gc]
            yb = su_scr[crow, gc] * mixed
            yb_scr[pl.ds(_aligned(i * TILE_M + ch * SGU_CHUNK, SGU_CHUNK), SGU_CHUNK), gc] = yb.astype(BF16)

    a_parts = [a_norm, a_gates, a_xp, a_su, a_sv, a_q, a_k, a_v, a_o]
    a_parts += [functools.partial(a_sgu, ch) for ch in range(TILE_M // SGU_CHUNK)]

    def pool_pad():
        zpad = jnp.zeros((POOL_PAD, D_BRANCH), F32)
        xp_scr[0:POOL_PAD, :] = zpad
        xp_scr[T + POOL_PAD:T + 2 * POOL_PAD, :] = zpad

    def pool_seq(g):
        win = POOL_WINDOWS[g]
        gc = slice(g * GROUP, (g + 1) * GROUP)
        half = win // 2
        t_idx = lax.broadcasted_iota(jnp.int32, (T, GROUP), 0)
        total = jnp.zeros((T, GROUP), F32)
        for j in range(-half, win - half):
            total = total + xp_scr[POOL_PAD + j:POOL_PAD + j + T, gc]
        cnt = (jnp.minimum(t_idx + (win - half), T) - jnp.maximum(t_idx - half, 0)).astype(F32)
        dlt = total / cnt - xp_scr[POOL_PAD:POOL_PAD + T, gc]
        ya = _dot(dlt.astype(BF16), wpool_ref[g]) * pscale_ref[...][:, gc]
        ya_scr[:, gc] = ya.astype(BF16)

    def a_colpool(i):
        c_idx = lax.broadcasted_iota(jnp.int32, (GRID_W, GROUP), 0)
        for g, win in enumerate(POOL_WINDOWS):
            gc = slice(g * GROUP, (g + 1) * GROUP)
            half = win // 2
            inv_cnt = 1.0 / (jnp.minimum(c_idx + (win - half), GRID_W) - jnp.maximum(c_idx - half, 0)).astype(F32)
            for r in range(TILE_M // GRID_W):
                r0 = i * TILE_M + r * GRID_W
                xg = xp_scr[pl.ds(_aligned(r0 + POOL_PAD, 8), GRID_W), gc]
                total = xg
                for j in range(-half, win - half):
                    if j != 0:
                        valid = (c_idx + j >= 0) & (c_idx + j < GRID_W)
                        total = total + jnp.where(valid, pltpu.roll(xg, (-j) % GRID_W, 0), 0.0)
                m1_scr[pl.ds(_aligned(r0, GRID_W), GRID_W), gc] = total * inv_cnt

    def pool_grid_rows(g):
        win = POOL_WINDOWS[g]
        gc = slice(g * GROUP, (g + 1) * GROUP)
        half = win // 2
        n_rows = T // GRID_W
        for r in range(n_rows):
            lo = max(r - half, 0)
            hi = min(r + (win - half), n_rows)
            acc = m1_scr[lo * GRID_W:(lo + 1) * GRID_W, gc]
            for rr in range(lo + 1, hi):
                acc = acc + m1_scr[rr * GRID_W:(rr + 1) * GRID_W, gc]
            xg = xp_scr[POOL_PAD + r * GRID_W:POOL_PAD + (r + 1) * GRID_W, gc]
            dlt_scr[r * GRID_W:(r + 1) * GRID_W, :] = acc / float(hi - lo) - xg
        ya = _dot(dlt_scr[...].astype(BF16), wpool_ref[g]) * pscale_ref[...][:, gc]
        ya_scr[:, gc] = ya.astype(BF16)

    if grid2d:
        assert TILE_M % GRID_W == 0
        a_parts.insert(a_parts.index(a_xp) + 1, a_colpool)
        pool_parts = [functools.partial(pool_grid_rows, g) for g in range(N_GROUPS)]
    else:
        pool_parts = [pool_pad] + [functools.partial(pool_seq, g) for g in range(N_GROUPS)]

    def keep_mask(d, rb, col_lo, ncols):
        t_loc = rb * RB + lax.broadcasted_iota(jnp.int32, (RB, ncols), 0)
        s_loc = col_lo + lax.broadcasted_iota(jnp.int32, (RB, ncols), 1)
        return s_loc <= t_loc if d == 0 else s_loc >= t_loc

    def mlstm_prepare(d, c, m_in):
        c0 = _aligned(c * L, L)
        colf = colf_scr[pl.ds(c0, L), :]
        bcol = colf[:, N_HEADS * d:N_HEADS * (d + 1)]
        cmcol = colf[:, 2 * N_HEADS + N_HEADS * d:2 * N_HEADS + N_HEADS * (d + 1)]
        mx = jnp.maximum(m_in, cmcol)
        last = L - 1 if d == 0 else 0
        mx_last = mx[last:last + 1, :]
        return dict(c0=c0, mx=mx, mx_last=mx_last, w_inter=jnp.exp(m_in - mx), e_negm=jnp.exp(-(bcol + mx)),
                    decay=jnp.exp(m_in - mx_last), m_new=bcol[last:last + 1, :] + mx_last)

    def block_cols(d, rb):
        return (0, (rb + 1) * RB) if d == 0 else (rb * RB, L)

    def mlstm_scores(d, p, h):
        c0 = p["c0"]
        hc = slice(h * HEAD_DIM, (h + 1) * HEAD_DIM)
        qks = []
        for rb in range(L // RB):
            col_lo, col_hi = block_cols(d, rb)
            qb = q_scr[pl.ds(_aligned(c0 + rb * RB, RB), RB), hc]
            qks.append(_dot(qb, kt_scr[hc, pl.ds(_aligned(c0 + col_lo, RB), col_hi - col_lo)]))
        p["qk", h] = qks

    def mlstm_head(d, p, h):
        c0, mx, mx_last = p["c0"], p["mx"], p["mx_last"]
        rows = pl.ds(c0, L)
        hc = slice(h * HEAD_DIM, (h + 1) * HEAD_DIM)
        ac = slice(2 * h * HEAD_DIM, (2 * h + 2) * HEAD_DIM)
        grow = grow_scr[N_HEADS * d + h:N_HEADS * d + h + 1, rows]
        if has_state:
            cn_dh = cn_scr[d, h]
            cn_bf = cn_dh.astype(BF16)
        for rb in range(L // RB):
            col_lo, col_hi = block_cols(d, rb)
            ncols = col_hi - col_lo
            trow = slice(rb * RB, (rb + 1) * RB)
            rrows = pl.ds(_aligned(c0 + rb * RB, RB), RB)
            crows = pl.ds(_aligned(c0 + col_lo, RB), ncols)
            qb = q_scr[rrows, hc]
            qk = p["qk", h][rb]
            w = jnp.exp(grow[:, col_lo:col_hi] - mx[trow, h:h + 1])
            s = (qk * jnp.where(keep_mask(d, rb, col_lo, ncols), w, 0.0)).astype(BF16)
            na = _dot(s, vaug_scr[crows, ac])
            if has_state:
                na = na + p["w_inter"][trow, h:h + 1] * _dot(qb, cn_bf)
            hval = na[:, 0:HEAD_DIM] / jnp.maximum(jnp.abs(na[:, HEAD_DIM:]), p["e_negm"][trow, h:h + 1])
            if d == 0:
                h_scr[rrows, hc] = hval
            else:
                h_scr[rrows, hc] += hval
        w_state = jnp.exp(grow - mx_last[:, h:h + 1])
        kw_t = (kt_scr[hc, rows].astype(F32) * w_state).astype(BF16)
        upd = _dot(kw_t, vaug_scr[rows, ac])
        if has_state:
            cn_scr[d, h] = p["decay"][:, h:h + 1] * cn_dh + upd
        else:
            cout_ref[d, h] = upd[:, 0:HEAD_DIM]
            nout_ref[d, h:h + 1, :] = upd[:, HEAD_DIM:].T[0:1, :]

    def merge_ab(c, nj):
        nc = slice(nj * NW, (nj + 1) * NW)
        ub = u_scr[tile_rows(c), :]
        acc = None
        for r, y_scr in enumerate((ya_scr, yb_scr)):
            gate = jax.nn.sigmoid(_dot(ub, wbrg_ref[:, r * D_MODEL + nj * NW:r * D_MODEL + (nj + 1) * NW]))
            term = gate * _dot(y_scr[tile_rows(c), :], wbranch_ref[r * D_BRANCH:(r + 1) * D_BRANCH, nc])
            acc = term if acc is None else acc + term
        mab_scr[tile_rows(c), nc] = acc

    def gate_c(c, nj):
        w = wbrg_ref[:, 2 * D_MODEL + nj * NW:2 * D_MODEL + (nj + 1) * NW]
        gc_scr[:, nj * NW:(nj + 1) * NW] = jax.nn.sigmoid(_dot(u_scr[tile_rows(c), :], w))

    def tail_yc(c):
        gml = gml_ref[...]
        for h in range(N_HEADS):
            hc = slice(h * HEAD_DIM, (h + 1) * HEAD_DIM)
            hh = h_scr[tile_rows(c), hc]
            hn = hh * lax.rsqrt(jnp.mean(hh * hh, axis=-1, keepdims=True) + EPS) * gml[:, hc]
            yc_scr[:, hc] = (jax.nn.sigmoid(o_scr[tile_rows(c), hc]) * hn).astype(BF16)

    def tail_merge(c, nj):
        nc = slice(nj * NW, (nj + 1) * NW)
        acc = mab_scr[tile_rows(c), nc] + gc_scr[:, nc] * _dot(yc_scr[...], wbranch_ref[2 * D_BRANCH:, nc])
        merged_scr[:, nc] = acc.astype(BF16)

    def tail_out(c):
        gt1 = mod_ref[...][:, 2 * D_MODEL:3 * D_MODEL]
        out_ref[tile_rows(c), :] = x_ref[tile_rows(c), :] + gt1 * _dot(merged_scr[...], wout_ref[...])

    def tail_parts(c):
        return ([functools.partial(tail_yc, c)] + [functools.partial(tail_merge, c, nj) for nj in range(n_merge)]
                + [functools.partial(tail_out, c)])

    def scan_parts(d, c, m_in, fillers, out):
        p = {}

        def prepare():
            p.update(mlstm_prepare(d, c, m_in))
            out[d] = p["m_new"]

        parts = [prepare]
        for h in range(N_HEADS):
            parts.append(functools.partial(mlstm_scores, d, p, h))
            if h < len(fillers):
                parts.append(fillers[h])
            parts.append(functools.partial(mlstm_head, d, p, h))
        return parts + list(fillers[N_HEADS:])

    if n_tiles == 1 and n_chunks == 1:
        zeros = jnp.zeros((1, N_HEADS), F32)
        m_out = {}
        parts = [functools.partial(part, 0) for part in a_parts] + pool_parts
        parts += scan_parts(0, 0, zeros, [functools.partial(merge_ab, 0, nj) for nj in range(n_merge)], m_out)
        parts += scan_parts(1, 0, zeros, [functools.partial(gate_c, 0, nj) for nj in range(n_merge)], m_out)
        parts += tail_parts(0)

        def write_m():
            mout_ref[0:1, :] = m_out[0]
            mout_ref[1:2, :] = m_out[1]

        return parts + [write_m]

    def run_loops():
        def phase_a(i, carry):
            for part in a_parts:
                part(i)
            return carry

        lax.fori_loop(0, n_tiles, phase_a, 0)
        for part in pool_parts:
            part()
        for d in range(2):
            for h in range(N_HEADS):
                n_rows = jnp.broadcast_to(n0_ref[d, h:h + 1, :], (HEAD_DIM, HEAD_DIM))
                cn_scr[d, h] = jnp.concatenate([c0_ref[d, h], n_rows.T], axis=1)
        m_init = m0_ref[...]

        def fwd_step(j, m_in):
            m_out = {}
            for part in scan_parts(0, j, m_in, [functools.partial(merge_ab, j, nj) for nj in range(n_merge)], m_out):
                part()
            return m_out[0]

        def bwd_step(j, m_in):
            c = n_chunks - 1 - j
            m_out = {}
            for part in scan_parts(1, c, m_in, [functools.partial(gate_c, c, nj) for nj in range(n_merge)], m_out):
                part()
            for part in tail_parts(c):
                part()
            return m_out[1]

        lax.fori_loop(0, n_chunks, fwd_step, m_init[0:1, :])
        lax.fori_loop(0, n_chunks, bwd_step, m_init[1:2, :])

    assert has_state, "multi-chunk sequences carry the cached state"
    return [run_loops]


def _mixer_call(l, xall, lay, prm, stream, states=None, state_outs=None):
    ctx = stream == "ctx"
    T = SEQ if ctx else DEC_SEQ
    assert T % TILE_M == 0 and TILE_M == MLSTM_L and MLSTM_L % MLSTM_ROW_BLOCK == 0
    n_b = BATCH if ctx else DEC_BATCH
    blk0 = 0 if ctx else N_TOK_CTX // T
    once = pl.Buffered(1)

    def wspec(shape):
        nd = len(shape)
        return pl.BlockSpec((None,) + shape, lambda b: (l,) + (0,) * nd, pipeline_mode=once)

    def lspec(shape):
        return pl.BlockSpec(shape, lambda b: (0,) * len(shape), pipeline_mode=once)

    x_spec = pl.BlockSpec((T, D_MODEL), lambda b: (blk0 + b, 0), pipeline_mode=None if ctx else once)
    in_specs = [
        x_spec,
        lspec((COND_ROWS, 6 * D_MODEL)),
        lspec((DEPTH, D_MODEL)),
        wspec((D_MODEL, N_MAIN)),
        wspec((D_BRANCH, D_MODEL)),
        wspec((D_MODEL, GATE_PAD)),
        wspec((D_MODEL, N_BRANCH * D_MODEL)),
        lspec((N_BRANCH * D_BRANCH, D_MODEL)),
        lspec((D_MODEL, D_MODEL)),
        wspec((N_GROUPS, GROUP, GROUP)),
        lspec((DEPTH, D_BRANCH)),
        lspec((DEPTH, D_BRANCH)),
        wspec((N_GROUPS, SGU_CHUNK, SGU_CHUNK)),
        wspec((SGU_CHUNK, D_BRANCH)),
        lspec((DEPTH, N_GATE)),
        lspec((DEPTH, D_BRANCH)),
    ]
    args = [xall, lay["mods"], prm["g_norm1"], prm["w_in16"], prm["w_kt"],
            prm["w_gate"], prm["w_brg"], lay["w_branch"], lay["w_out"],
            prm["w_pool"], prm["pool_scale"], prm["g_sgu"], prm["w_sgu"],
            prm["b_sgu_tile"], prm["b_gates"], prm["g_mlstm"]]
    out_shape = [jax.ShapeDtypeStruct(xall.shape, xall.dtype)]
    out_specs = [x_spec]
    aliases = {0: 0}
    passthrough = []
    if ctx:
        per_seq = [(2, N_HEADS, HEAD_DIM, HEAD_DIM), (2, N_HEADS, HEAD_DIM), (2, N_HEADS)]
        out_shape += [jax.ShapeDtypeStruct((BATCH, DEPTH) + s, F32) for s in per_seq]
        if state_outs is None:
            out_specs += [pl.BlockSpec((None, DEPTH) + s, lambda b, nd=len(s): (b,) + (0,) * (nd + 1))
                          for s in per_seq]
        else:
            out_specs += [pl.BlockSpec((None, None) + s, lambda b, nd=len(s): (b, l) + (0,) * nd) for s in per_seq]
            passthrough += [(a, 1 + k) for k, a in enumerate(state_outs)]
    else:
        in_specs += [
            pl.BlockSpec((None, None, 2, N_HEADS, HEAD_DIM, HEAD_DIM), lambda b: (b, l, 0, 0, 0, 0)),
            pl.BlockSpec((None, None, 2, N_HEADS, HEAD_DIM), lambda b: (b, l, 0, 0, 0)),
            pl.BlockSpec((None, None, 2, N_HEADS), lambda b: (b, l, 0, 0)),
        ]
        args += list(states)
    for buf, out_idx in passthrough:
        aliases[len(args)] = out_idx
        in_specs.append(pl.BlockSpec(memory_space=pl.ANY))
        args.append(buf)

    scratch = [
        pltpu.VMEM((T, D_MODEL), BF16),
        pltpu.VMEM((T + 2 * POOL_PAD, D_BRANCH), F32),
        pltpu.VMEM((TILE_M, D_BRANCH), F32),
        pltpu.VMEM((TILE_M, D_BRANCH), BF16),
        pltpu.VMEM((T, D_BRANCH), BF16),
        pltpu.VMEM((D_BRANCH, T), BF16),
        pltpu.VMEM((T, 2 * D_BRANCH), BF16),
        pltpu.VMEM((T, D_BRANCH), F32),
        pltpu.VMEM((2 * N_HEADS, T), F32),
        pltpu.VMEM((T, GATE_PAD), F32),
        pltpu.VMEM((T, D_BRANCH), F32),
        pltpu.VMEM((T, D_BRANCH), BF16),
        pltpu.VMEM((T, D_BRANCH), BF16),
        pltpu.VMEM((TILE_M, D_BRANCH), BF16),
        pltpu.VMEM((T, D_MODEL), F32),
        pltpu.VMEM((TILE_M, D_MODEL), F32),
        pltpu.VMEM((TILE_M, D_MODEL), BF16),
    ]
    if not ctx:
        scratch += [pltpu.VMEM((2, N_HEADS, HEAD_DIM, 2 * HEAD_DIM), F32)]
        scratch += [pltpu.VMEM((T, D_BRANCH), F32), pltpu.VMEM((T, GROUP), F32)]

    outs = pl.pallas_call(
        functools.partial(_mixer_kernel, T=T, grid2d=not ctx, has_state=not ctx, layer=l,
                          n_in=len(in_specs), n_pass=len(passthrough), n_out=len(out_specs)),
        grid=(n_b,),
        in_specs=in_specs,
        out_specs=out_specs,
        out_shape=out_shape,
        scratch_shapes=scratch,
        input_output_aliases=aliases,
        compiler_params=pltpu.CompilerParams(
            dimension_semantics=("arbitrary",),
            vmem_limit_bytes=_vmem_request(list(zip(args, in_specs)) + list(zip(out_shape, out_specs)), scratch)),
        name="mixer_ctx" if ctx else "mixer_lat",
    )(*args)
    return outs


def _ffn_kernel(x_ref, mod_ref, g2_ref, w1_hbm, w2_hbm, gfin_ref, *rest, layer, final):
    if final:
        out_refs, rest = rest[:2], rest[2:]
    else:
        (cond_ref, wada_ref, bada_ref, wbr_ref, wo_ref), rest = rest[:5], rest[5:]
        out_refs, (mods_out, wbr_out, wo_out), rest = rest[:1], rest[1:4], rest[4:]
    acc_scr, h_scr, w1_scr, w2_scr, st1_scr, st2_scr, sem = rest
    n_chunks = D_FF // FFN_CHUNK
    tile = pl.program_id(0)
    first_tile = tile == 0
    ctx_tiles, lat_tiles = N_TOK_CTX // FFN_TILE_M, DEC_SEQ // FFN_TILE_M
    mod_ref = mod_ref.at[pl.ds(jnp.where(tile < ctx_tiles, 0, 1 + (tile - ctx_tiles) // lat_tiles), 1), :]
    g2_ref = g2_ref.at[pl.ds(layer, 1), :]

    def w1_copy(c):
        slot = c % FFN_STAGES
        c0 = _aligned(c * FFN_CHUNK, FFN_CHUNK)
        return pltpu.make_async_copy(w1_hbm.at[layer, :, pl.ds(c0, FFN_CHUNK)], st1_scr.at[slot], sem.at[0, slot])

    def w2_copy(c):
        slot = c % FFN_STAGES
        c0 = _aligned(c * FFN_CHUNK, FFN_CHUNK)
        return pltpu.make_async_copy(w2_hbm.at[layer, pl.ds(c0, FFN_CHUNK), :], st2_scr.at[slot], sem.at[1, slot])

    def land_w1(c):
        w1_copy(c).wait()
        c0 = _aligned(c * FFN_CHUNK, FFN_CHUNK)
        w1_scr[:, pl.ds(c0, FFN_CHUNK)] = st1_scr[c % FFN_STAGES].astype(BF16)

    def land_w2(c):
        w2_copy(c).wait()
        c0 = _aligned(c * FFN_CHUNK, FFN_CHUNK)
        w2_scr[pl.ds(c0, FFN_CHUNK), :] = st2_scr[c % FFN_STAGES].astype(BF16)

    def prepare_next_layer():
        if not final:
            _ada_block(cond_ref, wada_ref, bada_ref, mods_out, layer=layer + 1)
            wbr_out[...] = wbr_ref[...].astype(BF16)
            wo_out[...] = wo_ref[...].astype(BF16)

    def normed_input():
        mod = mod_ref[...]
        sh2 = mod[:, 3 * D_MODEL:4 * D_MODEL]
        sc2 = mod[:, 4 * D_MODEL:5 * D_MODEL]
        x = x_ref[...]
        ms = jnp.mean(x * x, axis=-1, keepdims=True)
        return (x * lax.rsqrt(ms + EPS) * g2_ref[...] * (1.0 + sc2) + sh2).astype(BF16)

    def hidden(ub, c):
        c0 = _aligned(c * FFN_CHUNK, FFN_CHUNK)
        h = _dot(ub, w1_scr[:, pl.ds(c0, FFN_CHUNK)])
        h_scr[:, pl.ds(c0, FFN_CHUNK)] = jnp.square(jnp.maximum(h, 0.0)).astype(BF16)

    def finish(ffn_out):
        gt2 = mod_ref[...][:, 5 * D_MODEL:6 * D_MODEL]
        y = x_ref[...] + gt2 * ffn_out
        if not final:
            out_refs[0][...] = y
            return
        y = y * lax.rsqrt(jnp.mean(y * y, axis=-1, keepdims=True) + EPS) * gfin_ref[...]
        in_ctx = tile < ctx_tiles

        @pl.when(in_ctx)
        def _():
            out_refs[0][...] = y

        @pl.when(jnp.logical_not(in_ctx))
        def _():
            out_refs[1][...] = y

    @pl.when(first_tile)
    def _():
        for c in range(FFN_STAGES):
            w1_copy(c).start()
            w2_copy(c).start()
        prepare_next_layer()
        ub = normed_input()
        acc_scr[...] = jnp.zeros_like(acc_scr)

        def chunk(c, carry):
            land_w1(c)
            land_w2(c)

            @pl.when(c + FFN_STAGES < n_chunks)
            def _():
                w1_copy(c + FFN_STAGES).start()
                w2_copy(c + FFN_STAGES).start()

            hidden(ub, c)
            c0 = _aligned(c * FFN_CHUNK, FFN_CHUNK)
            acc_scr[...] += _dot(h_scr[:, pl.ds(c0, FFN_CHUNK)], w2_scr[pl.ds(c0, FFN_CHUNK), :])
            return carry

        lax.fori_loop(0, n_chunks, chunk, 0)
        finish(acc_scr[...])

    @pl.when(jnp.logical_not(first_tile))
    def _():
        prepare_next_layer()
        ub = normed_input()
        for c in range(n_chunks):
            hidden(ub, c)
        finish(_dot(h_scr[...], w2_scr[...]))


def _ffn_call(l, xall, lay, prm, g_final, final):
    n_tiles = N_TOK // FFN_TILE_M
    ctx_tiles = N_TOK_CTX // FFN_TILE_M
    lat_tiles = DEC_SEQ // FFN_TILE_M
    once = pl.Buffered(1)

    x_spec = pl.BlockSpec((FFN_TILE_M, D_MODEL), lambda t: (t, 0))
    in_specs = [
        x_spec,
        pl.BlockSpec((COND_ROWS, 6 * D_MODEL), lambda t: (0, 0), pipeline_mode=once),
        pl.BlockSpec((DEPTH, D_MODEL), lambda t: (0, 0), pipeline_mode=once),
        pl.BlockSpec(memory_space=pl.ANY),
        pl.BlockSpec(memory_space=pl.ANY),
        pl.BlockSpec((1, D_MODEL), lambda t: (0, 0)),
    ]
    args = [xall, lay["mods"], prm["g_norm2"], prm["w_ff1"], prm["w_ff2"], g_final.reshape(1, D_MODEL)]
    if final:
        out_specs = [
            pl.BlockSpec((FFN_TILE_M, D_MODEL), lambda t: (jnp.minimum(t, ctx_tiles - 1), 0)),
            pl.BlockSpec((FFN_TILE_M, D_MODEL), lambda t: (jnp.maximum(t - ctx_tiles, 0), 0)),
        ]
        out_shape = [jax.ShapeDtypeStruct((N_TOK_CTX, D_MODEL), F32),
                     jax.ShapeDtypeStruct((N_TOK - N_TOK_CTX, D_MODEL), F32)]
        aliases = {}
    else:
        ada_cols = 6 * D_MODEL // n_tiles
        br_rows = N_BRANCH * D_BRANCH // n_tiles
        wo_tiles = D_MODEL // br_rows
        assert ada_cols % LANES == 0 and br_rows % BF16_SUBLANES == 0 and wo_tiles <= n_tiles
        wo_map = lambda t: (jnp.minimum(t, wo_tiles - 1), 0)
        in_specs += [
            pl.BlockSpec((COND_ROWS, D_MODEL), lambda t: (0, 0)),
            pl.BlockSpec((None, D_MODEL, ada_cols), lambda t: (l + 1, 0, t)),
            pl.BlockSpec((DEPTH, ada_cols), lambda t: (0, t)),
            pl.BlockSpec((None, br_rows, D_MODEL), lambda t: (l + 1, t, 0)),
            pl.BlockSpec((None, br_rows, D_MODEL), lambda t: (l + 1,) + wo_map(t)),
        ]
        args += [prm["cond8"], prm["w_ada"], prm["b_ada"], prm["w_branch_rows"], prm["w_out"]]
        out_specs = [
            x_spec,
            pl.BlockSpec((COND_ROWS, ada_cols), lambda t: (0, t)),
            pl.BlockSpec((br_rows, D_MODEL), lambda t: (t, 0)),
            pl.BlockSpec((br_rows, D_MODEL), wo_map),
        ]
        out_shape = [jax.ShapeDtypeStruct(xall.shape, xall.dtype),
                     jax.ShapeDtypeStruct((COND_ROWS, 6 * D_MODEL), F32),
                     jax.ShapeDtypeStruct((N_BRANCH * D_BRANCH, D_MODEL), BF16),
                     jax.ShapeDtypeStruct((D_MODEL, D_MODEL), BF16)]
        aliases = {0: 0}
    scratch = [
        pltpu.VMEM((FFN_TILE_M, D_MODEL), F32),
        pltpu.VMEM((FFN_TILE_M, D_FF), BF16),
        pltpu.VMEM((D_MODEL, D_FF), BF16),
        pltpu.VMEM((D_FF, D_MODEL), BF16),
        pltpu.VMEM((FFN_STAGES, D_MODEL, FFN_CHUNK), F32),
        pltpu.VMEM((FFN_STAGES, FFN_CHUNK, D_MODEL), F32),
    ]
    return pl.pallas_call(
        functools.partial(_ffn_kernel, layer=l, final=final),
        grid=(n_tiles,),
        in_specs=in_specs,
        out_specs=out_specs,
        out_shape=out_shape,
        scratch_shapes=scratch + [pltpu.SemaphoreType.DMA((2, FFN_STAGES))],
        input_output_aliases=aliases,
        compiler_params=pltpu.CompilerParams(
            dimension_semantics=("arbitrary",),
            vmem_limit_bytes=_vmem_request(list(zip(args, in_specs)) + list(zip(out_shape, out_specs)), scratch)),
        name="ffn_final" if final else "ffn",
    )(*args)


def kernel(x_prompt, x_sample, state_C, state_n, state_m, c, c_ctx, w_ada, b_ada, g_norm1, g_norm2, w_in,
           b_gates, w_pool, pool_scale, g_sgu, w_sgu, b_sgu, g_mlstm, w_branch, w_out, w_ff1, w_ff2, g_final):
    w_in16 = w_in.astype(BF16)
    prm = {
        "w_in16": w_in16,
        "w_gate": jnp.pad(w_in16[:, :, N_MAIN:N_MAIN + N_GATE], ((0, 0), (0, 0), (0, GATE_PAD - N_GATE))),
        "w_brg": w_in16[:, :, N_MAIN + N_GATE:],
        "w_kt": jnp.swapaxes(w_in16[:, :, 4 * D_BRANCH:5 * D_BRANCH], 1, 2),
        "w_branch_rows": w_branch.reshape(DEPTH, N_BRANCH * D_BRANCH, D_MODEL),
        "w_out": w_out,
        "w_ada": w_ada,
        "b_ada": b_ada,
        "w_pool": w_pool.astype(BF16),
        "w_sgu": w_sgu.astype(BF16),
        "w_ff1": w_ff1,
        "w_ff2": w_ff2,
        "g_norm1": g_norm1,
        "g_norm2": g_norm2,
        "pool_scale": pool_scale,
        "g_sgu": g_sgu,
        "g_mlstm": g_mlstm,
        "b_gates": b_gates,
        "b_sgu_tile": jnp.repeat(jnp.swapaxes(b_sgu, 1, 2), GROUP, axis=2),
        "cond8": jnp.concatenate(
            [c_ctx[None, :], c, jnp.zeros((COND_ROWS - 1 - DEC_BATCH, D_MODEL), F32)], axis=0),
    }
    lay = {
        "mods": _ada_call(prm["cond8"], w_ada, b_ada),
        "w_branch": prm["w_branch_rows"][0].astype(BF16),
        "w_out": w_out[0].astype(BF16),
    }

    xall = jnp.concatenate([x_prompt.reshape(N_TOK_CTX, D_MODEL), x_sample.reshape(-1, D_MODEL)], axis=0)
    new_states = None
    for l in range(DEPTH):
        xall, *new_states = _mixer_call(l, xall, lay, prm, "ctx", state_outs=new_states)
        (xall,) = _mixer_call(l, xall, lay, prm, "lat", states=(state_C, state_n, state_m))
        if l < DEPTH - 1:
            xall, *nxt = _ffn_call(l, xall, lay, prm, g_final, final=False)
            lay = dict(zip(("mods", "w_branch", "w_out"), nxt))
    y_prompt, y_sample = _ffn_call(DEPTH - 1, xall, lay, prm, g_final, final=True)
    return (y_prompt.reshape(BATCH, SEQ, D_MODEL), y_sample.reshape(DEC_BATCH, DEC_SEQ, D_MODEL), *new_states)
```

```python
import functools
import math

import jax
import jax.numpy as jnp
from jax import lax
from jax.experimental import pallas as pl
from jax.experimental.pallas import tpu as pltpu

F32 = jnp.float32
BF16 = jnp.bfloat16

D_MODEL = 1024
DEPTH = 4
BATCH = 16
SEQ = 256
DEC_BATCH = 2
DEC_SEQ = 1024
GRID_W = 64
D_BRANCH = 512
POOL_WINDOWS = (2, 4, 8, 16)
N_GROUPS = 4
GROUP = D_BRANCH // N_GROUPS
SGU_CHUNK = 128
N_HEADS = 4
HEAD_DIM = D_BRANCH // N_HEADS
N_BRANCH = 3
D_FF = 4 * D_MODEL
EPS = 1e-6

LANES = 128
BF16_SUBLANES = 16

N_MAIN = 7 * D_BRANCH
N_GATE = 4 * N_HEADS
GATE_PAD = LANES
N_TOK_CTX = BATCH * SEQ
N_TOK = N_TOK_CTX + DEC_BATCH * DEC_SEQ
COND_ROWS = 8

TILE_M = 256
MLSTM_L = 256
MLSTM_ROW_BLOCK = 128
MERGE_COLS = 256
POOL_PAD = 8
FFN_TILE_M = 512
FFN_CHUNK = 512
FFN_STAGES = 2
ADA_TILE_N = 1536
V7X_VMEM_BYTES = 64 * 1024 * 1024
VMEM_COMPILER_RESERVE = 6 * 1024 * 1024


def _vmem_request(windows, scratch):
    total = VMEM_COMPILER_RESERVE
    for arr, spec in windows:
        if spec.block_shape is not None:
            n_buf = 2 if spec.pipeline_mode is None else spec.pipeline_mode.buffer_count
            total += n_buf * math.prod(d for d in spec.block_shape if d is not None) * jnp.dtype(arr.dtype).itemsize
    for buf in scratch:
        total += math.prod(buf.shape) * jnp.dtype(buf.dtype).itemsize
    return min(total, V7X_VMEM_BYTES)


_dot =functools.partial(jnp.dot, preferred_element_type=F32)
_dot_nt = functools.partial(lax.dot_general, dimension_numbers=(((1,), (1,)), ((), ())),
                            preferred_element_type=F32)


def _aligned(offset, align):
    return offset if isinstance(offset, int) else pl.multiple_of(offset, align)


def _ada_block(cond_ref, w_ref, b_ref, out_ref, *, layer=0):
    cond = cond_ref[...]
    s = (cond * jax.nn.sigmoid(cond)).astype(BF16)
    out_ref[...] = _dot(s, w_ref[...].astype(BF16)) + b_ref[layer:layer + 1, :]


def _ada_call(cond8, w_ada, b_ada):
    n_out = 6 * D_MODEL
    args = (cond8, w_ada, b_ada)
    in_specs = [
        pl.BlockSpec((COND_ROWS, D_MODEL), lambda j: (0, 0)),
        pl.BlockSpec((None, D_MODEL, ADA_TILE_N), lambda j: (0, 0, j)),
        pl.BlockSpec((DEPTH, ADA_TILE_N), lambda j: (0, j)),
    ]
    out_spec = pl.BlockSpec((COND_ROWS, ADA_TILE_N), lambda j: (0, j))
    out_shape = jax.ShapeDtypeStruct((COND_ROWS, n_out), F32)
    return pl.pallas_call(
        _ada_block,
        grid=(n_out // ADA_TILE_N,),
        in_specs=in_specs,
        out_specs=out_spec,
        out_shape=out_shape,
        compiler_params=pltpu.CompilerParams(
            dimension_semantics=("arbitrary",),
            vmem_limit_bytes=_vmem_request(list(zip(args, in_specs)) + [(out_shape, out_spec)], [])),
        name="ada_mod",
    )(*args)


def _seg_scan(x, pos, seg, total, op, reverse):
    d = 1
    while d < seg:
        if reverse:
            shifted = pltpu.roll(x, total - d, 1)
            valid = pos < seg - d
        else:
            shifted = pltpu.roll(x, d, 1)
            valid = pos >= d
        if op == "sum":
            x = x + jnp.where(valid, shifted, 0.0)
        else:
            x = jnp.maximum(x, jnp.where(valid, shifted, -jnp.inf))
        d *= 2
    return x


_MIXER_LAYER_ROW_INPUTS = (2, 10, 11, 14, 15)


def _mixer_kernel(*refs, T, grid2d, has_state, layer, n_in, n_pass, n_out):
    ins, outs, scr = list(refs[:n_in - n_pass]), refs[n_in:n_in + n_out], refs[n_in + n_out:]
    ins[1] = ins[1].at[pl.ds(1 + pl.program_id(0) if has_state else 0, 1), :]
    for k in _MIXER_LAYER_ROW_INPUTS:
        ins[k] = ins[k].at[pl.ds(layer, 1), :]
    state_outs = list(outs[1:])
    if state_outs and n_pass == 0:
        for k, o in enumerate(state_outs):
            o[...] = jnp.zeros(o.shape, o.dtype)
            state_outs[k] = o.at[layer]
    for thunk in _mixer_sequence(*ins, outs[0], *state_outs, *scr, T=T, grid2d=grid2d, has_state=has_state):
        thunk()


def _mixer_sequence(*refs, T, grid2d, has_state):
    it = iter(refs)
    x_ref = next(it)
    mod_ref = next(it)
    g1_ref = next(it)
    wmain_ref = next(it)
    wkt_ref = next(it)
    wgate_ref = next(it)
    wbrg_ref = next(it)
    wbranch_ref = next(it)
    wout_ref = next(it)
    wpool_ref = next(it)
    pscale_ref = next(it)
    gsgu_ref = next(it)
    wsgu_ref = next(it)
    bsgu_ref = next(it)
    bgates_ref = next(it)
    gml_ref = next(it)
    if has_state:
        c0_ref, n0_ref, m0_ref = next(it), next(it), next(it)
    out_ref = next(it)
    if not has_state:
        cout_ref, nout_ref, mout_ref = next(it), next(it), next(it)
    (u_scr, xp_scr, su_scr, vn_scr, q_scr, kt_scr, vaug_scr, o_scr, grow_scr, colf_scr,
     h_scr, ya_scr, yb_scr, yc_scr, mab_scr, gc_scr, merged_scr) = [next(it) for _ in range(17)]
    if has_state:
        cn_scr = next(it)
    if grid2d:
        m1_scr, dlt_scr = next(it), next(it)

    L = MLSTM_L
    n_tiles = T // TILE_M
    n_chunks = T // L
    RB = MLSTM_ROW_BLOCK
    NW = MERGE_COLS
    n_merge = D_MODEL // NW

    def tile_rows(i):
        return pl.ds(_aligned(i * TILE_M, TILE_M), TILE_M)

    def a_norm(i):
        mod = mod_ref[...]
        sh1 = mod[:, 0:D_MODEL]
        sc1 = mod[:, D_MODEL:2 * D_MODEL]
        x = x_ref[tile_rows(i), :]
        ms = jnp.mean(x * x, axis=-1, keepdims=True)
        u = x * lax.rsqrt(ms + EPS) * g1_ref[...] * (1.0 + sc1) + sh1
        u_scr[tile_rows(i), :] = u.astype(BF16)

    def a_gates(i):
        lane = lax.broadcasted_iota(jnp.int32, (2 * N_HEADS, TILE_M), 1)
        is_fwd = lax.broadcasted_iota(jnp.int32, (2 * N_HEADS, TILE_M), 0) < N_HEADS
        bias = jnp.concatenate([bgates_ref[...], jnp.zeros((1, GATE_PAD - N_GATE), F32)], axis=1)
        gate_t = (_dot(u_scr[tile_rows(i), :], wgate_ref[...]) + bias).T
        gp = gate_t[0:N_GATE, :]
        i_all = gp[0:2 * N_HEADS, :]
        f_all = gp[2 * N_HEADS:4 * N_HEADS, :]
        logf = jnp.minimum(f_all, 0.0) - jnp.log1p(jnp.exp(-jnp.abs(f_all)))
        b_row = jnp.where(is_fwd, _seg_scan(logf, lane, L, L, "sum", False),
                          _seg_scan(logf, lane, L, L, "sum", True))
        g_row = i_all - b_row
        cm_row = jnp.where(is_fwd, _seg_scan(g_row, lane, L, L, "max", False),
                           _seg_scan(g_row, lane, L, L, "max", True))
        grow_scr[:, tile_rows(i)] = g_row
        stacked = jnp.concatenate(
            [b_row, cm_row, jnp.zeros((GATE_PAD - 4 * N_HEADS, TILE_M), F32)], axis=0)
        colf_scr[tile_rows(i), :] = stacked.T

    def proj(i, k):
        return _dot(u_scr[tile_rows(i), :], wmain_ref[:, k * D_BRANCH:(k + 1) * D_BRANCH])

    def a_xp(i):
        xp_scr[pl.ds(_aligned(i * TILE_M + POOL_PAD, 8), TILE_M), :] = proj(i, 0)

    def a_su(i):
        su_scr[...] = proj(i, 1)

    def a_sv(i):
        sv = proj(i, 2)
        vn = sv * lax.rsqrt(jnp.mean(sv * sv, axis=-1, keepdims=True) + EPS) * gsgu_ref[...]
        vn_scr[...] = vn.astype(BF16)

    def a_q(i):
        q_scr[tile_rows(i), :] = proj(i, 3).astype(BF16)

    def a_k(i):
        kt = _dot_nt(wkt_ref[...], u_scr[tile_rows(i), :])
        kt_scr[:, tile_rows(i)] = (kt * (HEAD_DIM ** -0.5)).astype(BF16)

    def a_v(i):
        v = proj(i, 5).astype(BF16)
        ones_blk = jnp.ones((TILE_M, HEAD_DIM), BF16)
        for h in range(N_HEADS):
            vaug_scr[tile_rows(i), 2 * h * HEAD_DIM:(2 * h + 1) * HEAD_DIM] = v[:, h * HEAD_DIM:(h + 1) * HEAD_DIM]
            vaug_scr[tile_rows(i), (2 * h + 1) * HEAD_DIM:(2 * h + 2) * HEAD_DIM] = ones_blk

    def a_o(i):
        o_scr[tile_rows(i), :] = proj(i, 6)

    def a_sgu(ch, i):
        crow = slice(ch * SGU_CHUNK, (ch + 1) * SGU_CHUNK)
        for g in range(N_GROUPS):
            gc = slice(g * GROUP, (g + 1) * GROUP)
            mixed = _dot(wsgu_ref[g], vn_scr[crow, gc]) + bsgu_ref[:, gc]
            yb = su_scr[crow, gc] * mixed
            yb_scr[pl.ds(_aligned(i * TILE_M + ch * SGU_CHUNK, SGU_CHUNK), SGU_CHUNK), gc] = yb.astype(BF16)

    a_parts = [a_norm, a_gates, a_xp, a_su, a_sv, a_q, a_k, a_v, a_o]
    a_parts += [functools.partial(a_sgu, ch) for ch in range(TILE_M // SGU_CHUNK)]

    def pool_pad():
        zpad = jnp.zeros((POOL_PAD, D_BRANCH), F32)
        xp_scr[0:POOL_PAD, :] = zpad
        xp_scr[T + POOL_PAD:T + 2 * POOL_PAD, :] = zpad

    def pool_seq(g):
        win = POOL_WINDOWS[g]
        gc = slice(g * GROUP, (g + 1) * GROUP)
        half = win // 2
        t_idx = lax.broadcasted_iota(jnp.int32, (T, GROUP), 0)
        total = jnp.zeros((T, GROUP), F32)
        for j in range(-half, win - half):
            total = total + xp_scr[POOL_PAD + j:POOL_PAD + j + T, gc]
        cnt = (jnp.minimum(t_idx + (win - half), T) - jnp.maximum(t_idx - half, 0)).astype(F32)
        dlt = total / cnt - xp_scr[POOL_PAD:POOL_PAD + T, gc]
        ya = _dot(dlt.astype(BF16), wpool_ref[g]) * pscale_ref[...][:, gc]
        ya_scr[:, gc] = ya.astype(BF16)

    def a_colpool(i):
        c_idx = lax.broadcasted_iota(jnp.int32, (GRID_W, GROUP), 0)
        for g, win in enumerate(POOL_WINDOWS):
            gc = slice(g * GROUP, (g + 1) * GROUP)
            half = win // 2
            inv_cnt = 1.0 / (jnp.minimum(c_idx + (win - half), GRID_W) - jnp.maximum(c_idx - half, 0)).astype(F32)
            for r in range(TILE_M // GRID_W):
                r0 = i * TILE_M + r * GRID_W
                xg = xp_scr[pl.ds(_aligned(r0 + POOL_PAD, 8), GRID_W), gc]
                total = xg
                for j in range(-half, win - half):
                    if j != 0:
                        valid = (c_idx + j >= 0) & (c_idx + j < GRID_W)
                        total = total + jnp.where(valid, pltpu.roll(xg, (-j) % GRID_W, 0), 0.0)
                m1_scr[pl.ds(_aligned(r0, GRID_W), GRID_W), gc] = total * inv_cnt

    def pool_grid_rows(g):
        win = POOL_WINDOWS[g]
        gc = slice(g * GROUP, (g + 1) * GROUP)
        half = win // 2
        n_rows = T // GRID_W
        for r in range(n_rows):
            lo = max(r - half, 0)
            hi = min(r + (win - half), n_rows)
            acc = m1_scr[lo * GRID_W:(lo + 1) * GRID_W, gc]
            for rr in range(lo + 1, hi):
                acc = acc + m1_scr[rr * GRID_W:(rr + 1) * GRID_W, gc]
            xg = xp_scr[POOL_PAD + r * GRID_W:POOL_PAD + (r + 1) * GRID_W, gc]
            dlt_scr[r * GRID_W:(r + 1) * GRID_W, :] = acc / float(hi - lo) - xg
        ya = _dot(dlt_scr[...].astype(BF16), wpool_ref[g]) * pscale_ref[...][:, gc]
        ya_scr[:, gc] = ya.astype(BF16)

    if grid2d:
        assert TILE_M % GRID_W == 0
        a_parts.insert(a_parts.index(a_xp) + 1, a_colpool)
        pool_parts = [functools.partial(pool_grid_rows, g) for g in range(N_GROUPS)]
    else:
        pool_parts = [pool_pad] + [functools.partial(pool_seq, g) for g in range(N_GROUPS)]

    def keep_mask(d, rb, col_lo, ncols):
        t_loc = rb * RB + lax.broadcasted_iota(jnp.int32, (RB, ncols), 0)
        s_loc = col_lo + lax.broadcasted_iota(jnp.int32, (RB, ncols), 1)
        return s_loc <= t_loc if d == 0 else s_loc >= t_loc

    def mlstm_prepare(d, c, m_in):
        c0 = _aligned(c * L, L)
        colf = colf_scr[pl.ds(c0, L), :]
        bcol = colf[:, N_HEADS * d:N_HEADS * (d + 1)]
        cmcol = colf[:, 2 * N_HEADS + N_HEADS * d:2 * N_HEADS + N_HEADS * (d + 1)]
        mx = jnp.maximum(m_in, cmcol)
        last = L - 1 if d == 0 else 0
        mx_last = mx[last:last + 1, :]
        return dict(c0=c0, mx=mx, mx_last=mx_last, w_inter=jnp.exp(m_in - mx), e_negm=jnp.exp(-(bcol + mx)),
                    decay=jnp.exp(m_in - mx_last), m_new=bcol[last:last + 1, :] + mx_last)

    def block_cols(d, rb):
        return (0, (rb + 1) * RB) if d == 0 else (rb * RB, L)

    def mlstm_scores(d, p, h):
        c0 = p["c0"]
        hc = slice(h * HEAD_DIM, (h + 1) * HEAD_DIM)
        qks = []
        for rb in range(L // RB):
            col_lo, col_hi = block_cols(d, rb)
            qb = q_scr[pl.ds(_aligned(c0 + rb * RB, RB), RB), hc]
            qks.append(_dot(qb, kt_scr[hc, pl.ds(_aligned(c0 + col_lo, RB), col_hi - col_lo)]))
        p["qk", h] = qks

    def mlstm_head(d, p, h):
        c0, mx, mx_last = p["c0"], p["mx"], p["mx_last"]
        rows = pl.ds(c0, L)
        hc = slice(h * HEAD_DIM, (h + 1) * HEAD_DIM)
        ac = slice(2 * h * HEAD_DIM, (2 * h + 2) * HEAD_DIM)
        grow = grow_scr[N_HEADS * d + h:N_HEADS * d + h + 1, rows]
        if has_state:
            cn_dh = cn_scr[d, h]
            cn_bf = cn_dh.astype(BF16)
        for rb in range(L // RB):
            col_lo, col_hi = block_cols(d, rb)
            ncols = col_hi - col_lo
            trow = slice(rb * RB, (rb + 1) * RB)
            rrows = pl.ds(_aligned(c0 + rb * RB, RB), RB)
            crows = pl.ds(_aligned(c0 + col_lo, RB), ncols)
            qb = q_scr[rrows, hc]
            qk = p["qk", h][rb]
            w = jnp.exp(grow[:, col_lo:col_hi] - mx[trow, h:h + 1])
            s = (qk * jnp.where(keep_mask(d, rb, col_lo, ncols), w, 0.0)).astype(BF16)
            na = _dot(s, vaug_scr[crows, ac])
            if has_state:
                na = na + p["w_inter"][trow, h:h + 1] * _dot(qb, cn_bf)
            hval = na[:, 0:HEAD_DIM] / jnp.maximum(jnp.abs(na[:, HEAD_DIM:]), p["e_negm"][trow, h:h + 1])
            if d == 0:
                h_scr[rrows, hc] = hval
            else:
                h_scr[rrows, hc] += hval
        w_state = jnp.exp(grow - mx_last[:, h:h + 1])
        kw_t = (kt_scr[hc, rows].astype(F32) * w_state).astype(BF16)
        upd = _dot(kw_t, vaug_scr[rows, ac])
        if has_state:
            cn_scr[d, h] = p["decay"][:, h:h + 1] * cn_dh + upd
        else:
            cout_ref[d, h] = upd[:, 0:HEAD_DIM]
            nout_ref[d, h:h + 1, :] = upd[:, HEAD_DIM:].T[0:1, :]

    def merge_ab(c, nj):
        nc = slice(nj * NW, (nj + 1) * NW)
        ub = u_scr[tile_rows(c), :]
        acc = None
        for r, y_scr in enumerate((ya_scr, yb_scr)):
            gate = jax.nn.sigmoid(_dot(ub, wbrg_ref[:, r * D_MODEL + nj * NW:r * D_MODEL + (nj + 1) * NW]))
            term = gate * _dot(y_scr[tile_rows(c), :], wbranch_ref[r * D_BRANCH:(r + 1) * D_BRANCH, nc])
            acc = term if acc is None else acc + term
        mab_scr[tile_rows(c), nc] = acc

    def gate_c(c, nj):
        w = wbrg_ref[:, 2 * D_MODEL + nj * NW:2 * D_MODEL + (nj + 1) * NW]
        gc_scr[:, nj * NW:(nj + 1) * NW] = jax.nn.sigmoid(_dot(u_scr[tile_rows(c), :], w))

    def tail_yc(c):
        gml = gml_ref[...]
        for h in range(N_HEADS):
            hc = slice(h * HEAD_DIM, (h + 1) * HEAD_DIM)
            hh = h_scr[tile_rows(c), hc]
            hn = hh * lax.rsqrt(jnp.mean(hh * hh, axis=-1, keepdims=True) + EPS) * gml[:, hc]
            yc_scr[:, hc] = (jax.nn.sigmoid(o_scr[tile_rows(c), hc]) * hn).astype(BF16)

    def tail_merge(c, nj):
        nc = slice(nj * NW, (nj + 1) * NW)
        acc = mab_scr[tile_rows(c), nc] + gc_scr[:, nc] * _dot(yc_scr[...], wbranch_ref[2 * D_BRANCH:, nc])
        merged_scr[:, nc] = acc.astype(BF16)

    def tail_out(c):
        gt1 = mod_ref[...][:, 2 * D_MODEL:3 * D_MODEL]
        out_ref[tile_rows(c), :] = x_ref[tile_rows(c), :] + gt1 * _dot(merged_scr[...], wout_ref[...])

    def tail_parts(c):
        return ([functools.partial(tail_yc, c)] + [functools.partial(tail_merge, c, nj) for nj in range(n_merge)]
                + [functools.partial(tail_out, c)])

    def scan_parts(d, c, m_in, fillers, out):
        p = {}

        def prepare():
            p.update(mlstm_prepare(d, c, m_in))
            out[d] = p["m_new"]

        parts = [prepare]
        for h in range(N_HEADS):
            parts.append(functools.partial(mlstm_scores, d, p, h))
            if h < len(fillers):
                parts.append(fillers[h])
            parts.append(functools.partial(mlstm_head, d, p, h))
        return parts + list(fillers[N_HEADS:])

    if n_tiles == 1 and n_chunks == 1:
        zeros = jnp.zeros((1, N_HEADS), F32)
        m_out = {}
        parts = [functools.partial(part, 0) for part in a_parts] + pool_parts
        parts += scan_parts(0, 0, zeros, [functools.partial(merge_ab, 0, nj) for nj in range(n_merge)], m_out)
        parts += scan_parts(1, 0, zeros, [functools.partial(gate_c, 0, nj) for nj in range(n_merge)], m_out)
        parts += tail_parts(0)

        def write_m():
            mout_ref[0:1, :] = m_out[0]
            mout_ref[1:2, :] = m_out[1]

        return parts + [write_m]

    def run_loops():
        def phase_a(i, carry):
            for part in a_parts:
                part(i)
            return carry

        lax.fori_loop(0, n_tiles, phase_a, 0)
        for part in pool_parts:
            part()
        for d in range(2):
            for h in range(N_HEADS):
                n_rows = jnp.broadcast_to(n0_ref[d, h:h + 1, :], (HEAD_DIM, HEAD_DIM))
                cn_scr[d, h] = jnp.concatenate([c0_ref[d, h], n_rows.T], axis=1)
        m_init = m0_ref[...]

        def fwd_step(j, m_in):
            m_out = {}
            for part in scan_parts(0, j, m_in, [functools.partial(merge_ab, j, nj) for nj in range(n_merge)], m_out):
                part()
            return m_out[0]

        def bwd_step(j, m_in):
            c = n_chunks - 1 - j
            m_out = {}
            for part in scan_parts(1, c, m_in, [functools.partial(gate_c, c, nj) for nj in range(n_merge)], m_out):
                part()
            for part in tail_parts(c):
                part()
            return m_out[1]

        lax.fori_loop(0, n_chunks, fwd_step, m_init[0:1, :])
        lax.fori_loop(0, n_chunks, bwd_step, m_init[1:2, :])

    assert has_state, "multi-chunk sequences carry the cached state"
    return [run_loops]


def _mixer_call(l, xall, lay, prm, stream, states=None, state_outs=None):
    ctx = stream == "ctx"
    T = SEQ if ctx else DEC_SEQ
    assert T % TILE_M == 0 and TILE_M == MLSTM_L and MLSTM_L % MLSTM_ROW_BLOCK == 0
    n_b = BATCH if ctx else DEC_BATCH
    blk0 = 0 if ctx else N_TOK_CTX // T
    once = pl.Buffered(1)

    def wspec(shape):
        nd = len(shape)
        return pl.BlockSpec((None,) + shape, lambda b: (l,) + (0,) * nd, pipeline_mode=once)

    def lspec(shape):
        return pl.BlockSpec(shape, lambda b: (0,) * len(shape), pipeline_mode=once)

    x_spec = pl.BlockSpec((T, D_MODEL), lambda b: (blk0 + b, 0), pipeline_mode=None if ctx else once)
    in_specs = [
        x_spec,
        lspec((COND_ROWS, 6 * D_MODEL)),
        lspec((DEPTH, D_MODEL)),
        wspec((D_MODEL, N_MAIN)),
        wspec((D_BRANCH, D_MODEL)),
        wspec((D_MODEL, GATE_PAD)),
        lspec((D_MODEL, N_BRANCH * D_MODEL)),
        lspec((N_BRANCH * D_BRANCH, D_MODEL)),
        lspec((D_MODEL, D_MODEL)),
        wspec((N_GROUPS, GROUP, GROUP)),
        lspec((DEPTH, D_BRANCH)),
        lspec((DEPTH, D_BRANCH)),
        wspec((N_GROUPS, SGU_CHUNK, SGU_CHUNK)),
        wspec((SGU_CHUNK, D_BRANCH)),
        lspec((DEPTH, N_GATE)),
        lspec((DEPTH, D_BRANCH)),
    ]
    args = [xall, lay["mods"], prm["g_norm1"], prm["w_in16"], prm["w_kt"],
            prm["w_gate"], lay["w_brg"], lay["w_branch"], lay["w_out"],
            prm["w_pool"], prm["pool_scale"], prm["g_sgu"], prm["w_sgu"],
            prm["b_sgu_tile"], prm["b_gates"], prm["g_mlstm"]]
    out_shape = [jax.ShapeDtypeStruct(xall.shape, xall.dtype)]
    out_specs = [x_spec]
    aliases = {0: 0}
    passthrough = []
    if ctx:
        per_seq = [(2, N_HEADS, HEAD_DIM, HEAD_DIM), (2, N_HEADS, HEAD_DIM), (2, N_HEADS)]
        out_shape += [jax.ShapeDtypeStruct((BATCH, DEPTH) + s, F32) for s in per_seq]
        if state_outs is None:
            out_specs += [pl.BlockSpec((None, DEPTH) + s, lambda b, nd=len(s): (b,) + (0,) * (nd + 1))
                          for s in per_seq]
        else:
            out_specs += [pl.BlockSpec((None, None) + s, lambda b, nd=len(s): (b, l) + (0,) * nd) for s in per_seq]
            passthrough += [(a, 1 + k) for k, a in enumerate(state_outs)]
    else:
        in_specs += [
            pl.BlockSpec((None, None, 2, N_HEADS, HEAD_DIM, HEAD_DIM), lambda b: (b, l, 0, 0, 0, 0)),
            pl.BlockSpec((None, None, 2, N_HEADS, HEAD_DIM), lambda b: (b, l, 0, 0, 0)),
            pl.BlockSpec((None, None, 2, N_HEADS), lambda b: (b, l, 0, 0)),
        ]
        args += list(states)
    for buf, out_idx in passthrough:
        aliases[len(args)] = out_idx
        in_specs.append(pl.BlockSpec(memory_space=pl.ANY))
        args.append(buf)

    scratch = [
        pltpu.VMEM((T, D_MODEL), BF16),
        pltpu.VMEM((T + 2 * POOL_PAD, D_BRANCH), F32),
        pltpu.VMEM((TILE_M, D_BRANCH), F32),
        pltpu.VMEM((TILE_M, D_BRANCH), BF16),
        pltpu.VMEM((T, D_BRANCH), BF16),
        pltpu.VMEM((D_BRANCH, T), BF16),
        pltpu.VMEM((T, 2 * D_BRANCH), BF16),
        pltpu.VMEM((T, D_BRANCH), F32),
        pltpu.VMEM((2 * N_HEADS, T), F32),
        pltpu.VMEM((T, GATE_PAD), F32),
        pltpu.VMEM((T, D_BRANCH), F32),
        pltpu.VMEM((T, D_BRANCH), BF16),
        pltpu.VMEM((T, D_BRANCH), BF16),
        pltpu.VMEM((TILE_M, D_BRANCH), BF16),
        pltpu.VMEM((T, D_MODEL), F32),
        pltpu.VMEM((TILE_M, D_MODEL), F32),
        pltpu.VMEM((TILE_M, D_MODEL), BF16),
    ]
    if not ctx:
        scratch += [pltpu.VMEM((2, N_HEADS, HEAD_DIM, 2 * HEAD_DIM), F32)]
        scratch += [pltpu.VMEM((T, D_BRANCH), F32), pltpu.VMEM((T, GROUP), F32)]

    outs = pl.pallas_call(
        functools.partial(_mixer_kernel, T=T, grid2d=not ctx, has_state=not ctx, layer=l,
                          n_in=len(in_specs), n_pass=len(passthrough), n_out=len(out_specs)),
        grid=(n_b,),
        in_specs=in_specs,
        out_specs=out_specs,
        out_shape=out_shape,
        scratch_shapes=scratch,
        input_output_aliases=aliases,
        compiler_params=pltpu.CompilerParams(
            dimension_semantics=("arbitrary",),
            vmem_limit_bytes=_vmem_request(list(zip(args, in_specs)) + list(zip(out_shape, out_specs)), scratch)),
        name="mixer_ctx" if ctx else "mixer_lat",
    )(*args)
    return outs


def _ffn_kernel(x_ref, mod_ref, g2_ref, w1_hbm, w2_hbm, gfin_ref, *rest, layer, final):
    if final:
        out_refs, rest = rest[:2], rest[2:]
    else:
        (cond_ref, wada_ref, bada_ref, wbr_ref, wo_ref, win_ref), rest = rest[:6], rest[6:]
        out_refs, (mods_out, wbr_out, wo_out, wbrg_out), rest = rest[:1], rest[1:5], rest[5:]
    acc_scr, h_scr, w1_scr, w2_scr, st1_scr, st2_scr, sem = rest
    n_chunks = D_FF // FFN_CHUNK
    tile = pl.program_id(0)
    first_tile = tile == 0
    ctx_tiles, lat_tiles = N_TOK_CTX // FFN_TILE_M, DEC_SEQ // FFN_TILE_M
    mod_ref = mod_ref.at[pl.ds(jnp.where(tile < ctx_tiles, 0, 1 + (tile - ctx_tiles) // lat_tiles), 1), :]
    g2_ref = g2_ref.at[pl.ds(layer, 1), :]

    def w1_copy(c):
        slot = c % FFN_STAGES
        c0 = _aligned(c * FFN_CHUNK, FFN_CHUNK)
        return pltpu.make_async_copy(w1_hbm.at[layer, :, pl.ds(c0, FFN_CHUNK)], st1_scr.at[slot], sem.at[0, slot])

    def w2_copy(c):
        slot = c % FFN_STAGES
        c0 = _aligned(c * FFN_CHUNK, FFN_CHUNK)
        return pltpu.make_async_copy(w2_hbm.at[layer, pl.ds(c0, FFN_CHUNK), :], st2_scr.at[slot], sem.at[1, slot])

    def land_w1(c):
        w1_copy(c).wait()
        c0 = _aligned(c * FFN_CHUNK, FFN_CHUNK)
        w1_scr[:, pl.ds(c0, FFN_CHUNK)] = st1_scr[c % FFN_STAGES].astype(BF16)

    def land_w2(c):
        w2_copy(c).wait()
        c0 = _aligned(c * FFN_CHUNK, FFN_CHUNK)
        w2_scr[pl.ds(c0, FFN_CHUNK), :] = st2_scr[c % FFN_STAGES].astype(BF16)

    def prepare_next_layer():
        if not final:
            _ada_block(cond_ref, wada_ref, bada_ref, mods_out, layer=layer + 1)
            wbr_out[...] = wbr_ref[...].astype(BF16)
            wo_out[...] = wo_ref[...].astype(BF16)
            n_br = N_BRANCH * D_MODEL
            for c0 in range(0, n_br, FFN_CHUNK):
                width = min(FFN_CHUNK + GATE_PAD, n_br + N_GATE - c0)
                window = win_ref[:, N_MAIN + c0:N_MAIN + c0 + width]
                wbrg_out[:, c0:c0 + FFN_CHUNK] = window[:, N_GATE:N_GATE + FFN_CHUNK]

    def normed_input():
        mod = mod_ref[...]
        sh2 = mod[:, 3 * D_MODEL:4 * D_MODEL]
        sc2 = mod[:, 4 * D_MODEL:5 * D_MODEL]
        x = x_ref[...]
        ms = jnp.mean(x * x, axis=-1, keepdims=True)
        return (x * lax.rsqrt(ms + EPS) * g2_ref[...] * (1.0 + sc2) + sh2).astype(BF16)

    def hidden(ub, c):
        c0 = _aligned(c * FFN_CHUNK, FFN_CHUNK)
        h = _dot(ub, w1_scr[:, pl.ds(c0, FFN_CHUNK)])
        h_scr[:, pl.ds(c0, FFN_CHUNK)] = jnp.square(jnp.maximum(h, 0.0)).astype(BF16)

    def finish(ffn_out):
        gt2 = mod_ref[...][:, 5 * D_MODEL:6 * D_MODEL]
        y = x_ref[...] + gt2 * ffn_out
        if not final:
            out_refs[0][...] = y
            return
        y = y * lax.rsqrt(jnp.mean(y * y, axis=-1, keepdims=True) + EPS) * gfin_ref[...]
        in_ctx = tile < ctx_tiles

        @pl.when(in_ctx)
        def _():
            out_refs[0][...] = y

        @pl.when(jnp.logical_not(in_ctx))
        def _():
            out_refs[1][...] = y

    @pl.when(first_tile)
    def _():
        for c in range(FFN_STAGES):
            w1_copy(c).start()
            w2_copy(c).start()
        prepare_next_layer()
        ub = normed_input()
        acc_scr[...] = jnp.zeros_like(acc_scr)

        def chunk(c, carry):
            land_w1(c)
            land_w2(c)

            @pl.when(c + FFN_STAGES < n_chunks)
            def _():
                w1_copy(c + FFN_STAGES).start()
                w2_copy(c + FFN_STAGES).start()

            hidden(ub, c)
            c0 = _aligned(c * FFN_CHUNK, FFN_CHUNK)
            acc_scr[...] += _dot(h_scr[:, pl.ds(c0, FFN_CHUNK)], w2_scr[pl.ds(c0, FFN_CHUNK), :])
            return carry

        lax.fori_loop(0, n_chunks, chunk, 0)
        finish(acc_scr[...])

    @pl.when(jnp.logical_not(first_tile))
    def _():
        prepare_next_layer()
        ub = normed_input()
        for c in range(n_chunks):
            hidden(ub, c)
        finish(_dot(h_scr[...], w2_scr[...]))


def _ffn_call(l, xall, lay, prm, g_final, final):
    n_tiles = N_TOK // FFN_TILE_M
    ctx_tiles = N_TOK_CTX // FFN_TILE_M
    lat_tiles = DEC_SEQ // FFN_TILE_M
    once = pl.Buffered(1)

    x_spec = pl.BlockSpec((FFN_TILE_M, D_MODEL), lambda t: (t, 0))
    in_specs = [
        x_spec,
        pl.BlockSpec((COND_ROWS, 6 * D_MODEL), lambda t: (0, 0), pipeline_mode=once),
        pl.BlockSpec((DEPTH, D_MODEL), lambda t: (0, 0), pipeline_mode=once),
        pl.BlockSpec(memory_space=pl.ANY),
        pl.BlockSpec(memory_space=pl.ANY),
        pl.BlockSpec((1, D_MODEL), lambda t: (0, 0)),
    ]
    args = [xall, lay["mods"], prm["g_norm2"], prm["w_ff1"], prm["w_ff2"], g_final.reshape(1, D_MODEL)]
    if final:
        out_specs = [
            pl.BlockSpec((FFN_TILE_M, D_MODEL), lambda t: (jnp.minimum(t, ctx_tiles - 1), 0)),
            pl.BlockSpec((FFN_TILE_M, D_MODEL), lambda t: (jnp.maximum(t - ctx_tiles, 0), 0)),
        ]
        out_shape = [jax.ShapeDtypeStruct((N_TOK_CTX, D_MODEL), F32),
                     jax.ShapeDtypeStruct((N_TOK - N_TOK_CTX, D_MODEL), F32)]
        aliases = {}
    else:
        ada_cols = 6 * D_MODEL // n_tiles
        br_rows = N_BRANCH * D_BRANCH // n_tiles
        wo_tiles = D_MODEL // br_rows
        assert ada_cols % LANES == 0 and br_rows % BF16_SUBLANES == 0 and wo_tiles <= n_tiles
        wo_map = lambda t: (jnp.minimum(t, wo_tiles - 1), 0)
        in_specs += [
            pl.BlockSpec((COND_ROWS, D_MODEL), lambda t: (0, 0)),
            pl.BlockSpec((None, D_MODEL, ada_cols), lambda t: (l + 1, 0, t)),
            pl.BlockSpec((DEPTH, ada_cols), lambda t: (0, t)),
            pl.BlockSpec((None, br_rows, D_MODEL), lambda t: (l + 1, t, 0)),
            pl.BlockSpec((None, br_rows, D_MODEL), lambda t: (l + 1,) + wo_map(t)),
            pl.BlockSpec((None, br_rows, prm["w_in16"].shape[-1]), lambda t: (l + 1,) + wo_map(t)),
        ]
        args += [prm["cond8"], prm["w_ada"], prm["b_ada"], prm["w_branch_rows"], prm["w_out"], prm["w_in16"]]
        out_specs = [
            x_spec,
            pl.BlockSpec((COND_ROWS, ada_cols), lambda t: (0, t)),
            pl.BlockSpec((br_rows, D_MODEL), lambda t: (t, 0)),
            pl.BlockSpec((br_rows, D_MODEL), wo_map),
            pl.BlockSpec((br_rows, N_BRANCH * D_MODEL), wo_map),
        ]
        out_shape = [jax.ShapeDtypeStruct(xall.shape, xall.dtype),
                     jax.ShapeDtypeStruct((COND_ROWS, 6 * D_MODEL), F32),
                     jax.ShapeDtypeStruct((N_BRANCH * D_BRANCH, D_MODEL), BF16),
                     jax.ShapeDtypeStruct((D_MODEL, D_MODEL), BF16),
                     jax.ShapeDtypeStruct((D_MODEL, N_BRANCH * D_MODEL), BF16)]
        aliases = {0: 0}
    scratch = [
        pltpu.VMEM((FFN_TILE_M, D_MODEL), F32),
        pltpu.VMEM((FFN_TILE_M, D_FF), BF16),
        pltpu.VMEM((D_MODEL, D_FF), BF16),
        pltpu.VMEM((D_FF, D_MODEL), BF16),
        pltpu.VMEM((FFN_STAGES, D_MODEL, FFN_CHUNK), F32),
        pltpu.VMEM((FFN_STAGES, FFN_CHUNK, D_MODEL), F32),
    ]
    return pl.pallas_call(
        functools.partial(_ffn_kernel, layer=l, final=final),
        grid=(n_tiles,),
        in_specs=in_specs,
        out_specs=out_specs,
        out_shape=out_shape,
        scratch_shapes=scratch + [pltpu.SemaphoreType.DMA((2, FFN_STAGES))],
        input_output_aliases=aliases,
        compiler_params=pltpu.CompilerParams(
            dimension_semantics=("arbitrary",),
            vmem_limit_bytes=_vmem_request(list(zip(args, in_specs)) + list(zip(out_shape, out_specs)), scratch)),
        name="ffn_final" if final else "ffn",
    )(*args)


def kernel(x_prompt, x_sample, state_C, state_n, state_m, c, c_ctx, w_ada, b_ada, g_norm1, g_norm2, w_in,
           b_gates, w_pool, pool_scale, g_sgu, w_sgu, b_sgu, g_mlstm, w_branch, w_out, w_ff1, w_ff2, g_final):
    w_in16 = w_in.astype(BF16)
    prm = {
        "w_in16": w_in16,
        "w_gate": jnp.pad(w_in16[:, :, N_MAIN:N_MAIN + N_GATE], ((0, 0), (0, 0), (0, GATE_PAD - N_GATE))),
        "w_kt": jnp.swapaxes(w_in16[:, :, 4 * D_BRANCH:5 * D_BRANCH], 1, 2),
        "w_branch_rows": w_branch.reshape(DEPTH, N_BRANCH * D_BRANCH, D_MODEL),
        "w_out": w_out,
        "w_ada": w_ada,
        "b_ada": b_ada,
        "w_pool": w_pool.astype(BF16),
        "w_sgu": w_sgu.astype(BF16),
        "w_ff1": w_ff1,
        "w_ff2": w_ff2,
        "g_norm1": g_norm1,
        "g_norm2": g_norm2,
        "pool_scale": pool_scale,
        "g_sgu": g_sgu,
        "g_mlstm": g_mlstm,
        "b_gates": b_gates,
        "b_sgu_tile": jnp.repeat(jnp.swapaxes(b_sgu, 1, 2), GROUP, axis=2),
        "cond8": jnp.concatenate(
            [c_ctx[None, :], c, jnp.zeros((COND_ROWS - 1 - DEC_BATCH, D_MODEL), F32)], axis=0),
    }
    lay = {
        "mods": _ada_call(prm["cond8"], w_ada, b_ada),
        "w_branch": prm["w_branch_rows"][0].astype(BF16),
        "w_out": w_out[0].astype(BF16),
        "w_brg": w_in16[0, :, N_MAIN + N_GATE:],
    }

    xall = jnp.concatenate([x_prompt.reshape(N_TOK_CTX, D_MODEL), x_sample.reshape(-1, D_MODEL)], axis=0)
    new_states = None
    for l in range(DEPTH):
        xall, *new_states = _mixer_call(l, xall, lay, prm, "ctx", state_outs=new_states)
        (xall,) = _mixer_call(l, xall, lay, prm, "lat", states=(state_C, state_n, state_m))
        if l < DEPTH - 1:
            xall, *nxt = _ffn_call(l, xall, lay, prm, g_final, final=False)
            lay = dict(zip(("mods", "w_branch", "w_out", "w_brg"), nxt))
    y_prompt, y_sample = _ffn_call(DEPTH - 1, xall, lay, prm, g_final, final=True)
    return (y_prompt.reshape(BATCH, SEQ, D_MODEL), y_sample.reshape(DEC_BATCH, DEC_SEQ, D_MODEL), *new_states)
```

```python
import functools
import math

import jax
import jax.numpy as jnp
from jax import lax
from jax.experimental import pallas as pl
from jax.experimental.pallas import tpu as pltpu

F32 = jnp.float32
BF16 = jnp.bfloat16

D_MODEL = 1024
DEPTH = 4
BATCH = 16
SEQ = 256
DEC_BATCH = 2
DEC_SEQ = 1024
GRID_W = 64
D_BRANCH = 512
POOL_WINDOWS = (2, 4, 8, 16)
N_GROUPS = 4
GROUP = D_BRANCH // N_GROUPS
SGU_CHUNK = 128
N_HEADS = 4
HEAD_DIM = D_BRANCH // N_HEADS
N_BRANCH = 3
D_FF = 4 * D_MODEL
EPS = 1e-6

LANES = 128
BF16_SUBLANES = 16

N_MAIN = 7 * D_BRANCH
N_GATE = 4 * N_HEADS
GATE_PAD = LANES
N_TOK_CTX = BATCH * SEQ
N_TOK = N_TOK_CTX + DEC_BATCH * DEC_SEQ
COND_ROWS = 8

TILE_M = 256
MLSTM_L = 256
MLSTM_ROW_BLOCK = 128
MERGE_COLS = 256
POOL_PAD = 8
FFN_TILE_M = 512
FFN_CHUNK = 512
FFN_STAGES = 2
ADA_TILE_N = 1536
V7X_VMEM_BYTES = 64 * 1024 * 1024
VMEM_COMPILER_RESERVE = 6 * 1024 * 1024
VMEM_CALL_FLOOR = 60 * 1024 * 1024


def _vmem_request(windows, scratch):
    total = VMEM_COMPILER_RESERVE
    for arr, spec in windows:
        if spec.block_shape is not None:
            n_buf = 2 if spec.pipeline_mode is None else spec.pipeline_mode.buffer_count
            total += n_buf * math.prod(d for d in spec.block_shape if d is not None) * jnp.dtype(arr.dtype).itemsize
    for buf in scratch:
        total += math.prod(buf.shape) * jnp.dtype(buf.dtype).itemsize
    return min(max(total, VMEM_CALL_FLOOR), V7X_VMEM_BYTES)


_dot = functools.partial(jnp.dot, preferred_element_type=F32)
_dot_nt = functools.partial(lax.dot_general, dimension_numbers=(((1,), (1,)), ((), ())),
                            preferred_element_type=F32)


def _aligned(offset, align):
    return offset if isinstance(offset, int) else pl.multiple_of(offset, align)


def _ada_block(cond_ref, w_ref, b_ref, out_ref, *, layer=0):
    cond = cond_ref[...]
    s = (cond * jax.nn.sigmoid(cond)).astype(BF16)
    out_ref[...] = _dot(s, w_ref[...].astype(BF16)) + b_ref[layer:layer + 1, :]


def _ada_call(cond8, w_ada, b_ada):
    n_out = 6 * D_MODEL
    args = (cond8, w_ada, b_ada)
    in_specs = [
        pl.BlockSpec((COND_ROWS, D_MODEL), lambda j: (0, 0)),
        pl.BlockSpec((None, D_MODEL, ADA_TILE_N), lambda j: (0, 0, j)),
        pl.BlockSpec((DEPTH, ADA_TILE_N), lambda j: (0, j)),
    ]
    out_spec = pl.BlockSpec((COND_ROWS, ADA_TILE_N), lambda j: (0, j))
    out_shape = jax.ShapeDtypeStruct((COND_ROWS, n_out), F32)
    return pl.pallas_call(
        _ada_block,
        grid=(n_out // ADA_TILE_N,),
        in_specs=in_specs,
        out_specs=out_spec,
        out_shape=out_shape,
        compiler_params=pltpu.CompilerParams(
            dimension_semantics=("arbitrary",),
            vmem_limit_bytes=_vmem_request(list(zip(args, in_specs)) + [(out_shape, out_spec)], [])),
        name="ada_mod",
    )(*args)


def _seg_scan(x, pos, seg, total, op, reverse):
    d = 1
    while d < seg:
        if reverse:
            shifted = pltpu.roll(x, total - d, 1)
            valid = pos < seg - d
        else:
            shifted = pltpu.roll(x, d, 1)
            valid = pos >= d
        if op == "sum":
            x = x + jnp.where(valid, shifted, 0.0)
        else:
            x = jnp.maximum(x, jnp.where(valid, shifted, -jnp.inf))
        d *= 2
    return x


_MIXER_LAYER_ROW_INPUTS = (2, 10, 11, 14, 15)


def _mixer_kernel(*refs, T, grid2d, has_state, layer, n_in, n_pass, n_out):
    ins, outs, scr = list(refs[:n_in - n_pass]), refs[n_in:n_in + n_out], refs[n_in + n_out:]
    ins[1] = ins[1].at[pl.ds(1 + pl.program_id(0) if has_state else 0, 1), :]
    for k in _MIXER_LAYER_ROW_INPUTS:
        ins[k] = ins[k].at[pl.ds(layer, 1), :]
    state_outs = list(outs[1:])
    if state_outs and n_pass == 0:
        for k, o in enumerate(state_outs):
            o[...] = jnp.zeros(o.shape, o.dtype)
            state_outs[k] = o.at[layer]
    for thunk in _mixer_sequence(*ins, outs[0], *state_outs, *scr, T=T, grid2d=grid2d, has_state=has_state):
        thunk()


def _mixer_sequence(*refs, T, grid2d, has_state):
    it = iter(refs)
    x_ref = next(it)
    mod_ref = next(it)
    g1_ref = next(it)
    wmain_ref = next(it)
    wkt_ref = next(it)
    wgate_ref = next(it)
    wbrg_ref = next(it)
    wbranch_ref = next(it)
    wout_ref = next(it)
    wpool_ref = next(it)
    pscale_ref = next(it)
    gsgu_ref = next(it)
    wsgu_ref = next(it)
    bsgu_ref = next(it)
    bgates_ref = next(it)
    gml_ref = next(it)
    if has_state:
        c0_ref, n0_ref, m0_ref = next(it), next(it), next(it)
    out_ref = next(it)
    if not has_state:
        cout_ref, nout_ref, mout_ref = next(it), next(it), next(it)
    (u_scr, xp_scr, su_scr, vn_scr, q_scr, kt_scr, vaug_scr, o_scr, grow_scr, colf_scr,
     h_scr, ya_scr, yb_scr, yc_scr, mab_scr, gc_scr, merged_scr) = [next(it) for _ in range(17)]
    if has_state:
        cn_scr = next(it)
    if grid2d:
        m1_scr, dlt_scr = next(it), next(it)

    L = MLSTM_L
    n_tiles = T // TILE_M
    n_chunks = T // L
    RB = MLSTM_ROW_BLOCK
    NW = MERGE_COLS
    n_merge = D_MODEL // NW

    def tile_rows(i):
        return pl.ds(_aligned(i * TILE_M, TILE_M), TILE_M)

    def a_norm(i):
        mod = mod_ref[...]
        sh1 = mod[:, 0:D_MODEL]
        sc1 = mod[:, D_MODEL:2 * D_MODEL]
        x = x_ref[tile_rows(i), :]
        ms = jnp.mean(x * x, axis=-1, keepdims=True)
        u = x * lax.rsqrt(ms + EPS) * g1_ref[...] * (1.0 + sc1) + sh1
        u_scr[tile_rows(i), :] = u.astype(BF16)

    def a_gates(i):
        lane = lax.broadcasted_iota(jnp.int32, (2 * N_HEADS, TILE_M), 1)
        is_fwd = lax.broadcasted_iota(jnp.int32, (2 * N_HEADS, TILE_M), 0) < N_HEADS
        bias = jnp.concatenate([bgates_ref[...], jnp.zeros((1, GATE_PAD - N_GATE), F32)], axis=1)
        gate_t = (_dot(u_scr[tile_rows(i), :], wgate_ref[...]) + bias).T
        gp = gate_t[0:N_GATE, :]
        i_all = gp[0:2 * N_HEADS, :]
        f_all = gp[2 * N_HEADS:4 * N_HEADS, :]
        logf = jnp.minimum(f_all, 0.0) - jnp.log1p(jnp.exp(-jnp.abs(f_all)))
        b_row = jnp.where(is_fwd, _seg_scan(logf, lane, L, L, "sum", False),
                          _seg_scan(logf, lane, L, L, "sum", True))
        g_row = i_all - b_row
        cm_row = jnp.where(is_fwd, _seg_scan(g_row, lane, L, L, "max", False),
                           _seg_scan(g_row, lane, L, L, "max", True))
        grow_scr[:, tile_rows(i)] = g_row
        stacked = jnp.concatenate(
            [b_row, cm_row, jnp.zeros((GATE_PAD - 4 * N_HEADS, TILE_M), F32)], axis=0)
        colf_scr[tile_rows(i), :] = stacked.T

    def proj(i, k):
        return _dot(u_scr[tile_rows(i), :], wmain_ref[:, k * D_BRANCH:(k + 1) * D_BRANCH])

    def a_xp(i):
        xp_scr[pl.ds(_aligned(i * TILE_M + POOL_PAD, 8), TILE_M), :] = proj(i, 0)

    def a_su(i):
        su_scr[...] = proj(i, 1)

    def a_sv(i):
        sv = proj(i, 2)
        vn = sv * lax.rsqrt(jnp.mean(sv * sv, axis=-1, keepdims=True) + EPS) * gsgu_ref[...]
        vn_scr[...] = vn.astype(BF16)

    def a_q(i):
        q_scr[tile_rows(i), :] = proj(i, 3).astype(BF16)

    def a_k(i):
        kt = _dot_nt(wkt_ref[...], u_scr[tile_rows(i), :])
        kt_scr[:, tile_rows(i)] = (kt * (HEAD_DIM ** -0.5)).astype(BF16)

    def a_v(i):
        v = proj(i, 5).astype(BF16)
        ones_blk = jnp.ones((TILE_M, HEAD_DIM), BF16)
        for h in range(N_HEADS):
            vaug_scr[tile_rows(i), 2 * h * HEAD_DIM:(2 * h + 1) * HEAD_DIM] = v[:, h * HEAD_DIM:(h + 1) * HEAD_DIM]
            vaug_scr[tile_rows(i), (2 * h + 1) * HEAD_DIM:(2 * h + 2) * HEAD_DIM] = ones_blk

    def a_o(i):
        o_scr[tile_rows(i), :] = proj(i, 6)

    def a_sgu(ch, i):
        crow = slice(ch * SGU_CHUNK, (ch + 1) * SGU_CHUNK)
        for g in range(N_GROUPS):
            gc = slice(g * GROUP, (g + 1) * GROUP)
            mixed = _dot(wsgu_ref[g], vn_scr[crow, gc]) + bsgu_ref[:, gc]
            yb = su_scr[crow, gc] * mixed
            yb_scr[pl.ds(_aligned(i * TILE_M + ch * SGU_CHUNK, SGU_CHUNK), SGU_CHUNK), gc] = yb.astype(BF16)

    a_parts = [a_norm, a_gates, a_xp, a_su, a_sv, a_q, a_k, a_v, a_o]
    a_parts += [functools.partial(a_sgu, ch) for ch in range(TILE_M // SGU_CHUNK)]

    def pool_pad():
        zpad = jnp.zeros((POOL_PAD, D_BRANCH), F32)
        xp_scr[0:POOL_PAD, :] = zpad
        xp_scr[T + POOL_PAD:T + 2 * POOL_PAD, :] = zpad

    def pool_seq(g):
        win = POOL_WINDOWS[g]
        gc = slice(g * GROUP, (g + 1) * GROUP)
        half = win // 2
        t_idx = lax.broadcasted_iota(jnp.int32, (T, GROUP), 0)
        total = jnp.zeros((T, GROUP), F32)
        for j in range(-half, win - half):
            total = total + xp_scr[POOL_PAD + j:POOL_PAD + j + T, gc]
        cnt = (jnp.minimum(t_idx + (win - half), T) - jnp.maximum(t_idx - half, 0)).astype(F32)
        dlt = total / cnt - xp_scr[POOL_PAD:POOL_PAD + T, gc]
        ya = _dot(dlt.astype(BF16), wpool_ref[g]) * pscale_ref[...][:, gc]
        ya_scr[:, gc] = ya.astype(BF16)

    def a_colpool(i):
        c_idx = lax.broadcasted_iota(jnp.int32, (GRID_W, GROUP), 0)
        for g, win in enumerate(POOL_WINDOWS):
            gc = slice(g * GROUP, (g + 1) * GROUP)
            half = win // 2
            inv_cnt = 1.0 / (jnp.minimum(c_idx + (win - half), GRID_W) - jnp.maximum(c_idx - half, 0)).astype(F32)
            for r in range(TILE_M // GRID_W):
                r0 = i * TILE_M + r * GRID_W
                xg = xp_scr[pl.ds(_aligned(r0 + POOL_PAD, 8), GRID_W), gc]
                total = xg
                for j in range(-half, win - half):
                    if j != 0:
                        valid = (c_idx + j >= 0) & (c_idx + j < GRID_W)
                        total = total + jnp.where(valid, pltpu.roll(xg, (-j) % GRID_W, 0), 0.0)
                m1_scr[pl.ds(_aligned(r0, GRID_W), GRID_W), gc] = total * inv_cnt

    def pool_grid_rows(g):
        win = POOL_WINDOWS[g]
        gc = slice(g * GROUP, (g + 1) * GROUP)
        half = win // 2
        n_rows = T // GRID_W
        for r in range(n_rows):
            lo = max(r - half, 0)
            hi = min(r + (win - half), n_rows)
            acc = m1_scr[lo * GRID_W:(lo + 1) * GRID_W, gc]
            for rr in range(lo + 1, hi):
                acc = acc + m1_scr[rr * GRID_W:(rr + 1) * GRID_W, gc]
            xg = xp_scr[POOL_PAD + r * GRID_W:POOL_PAD + (r + 1) * GRID_W, gc]
            dlt_scr[r * GRID_W:(r + 1) * GRID_W, :] = acc / float(hi - lo) - xg
        ya = _dot(dlt_scr[...].astype(BF16), wpool_ref[g]) * pscale_ref[...][:, gc]
        ya_scr[:, gc] = ya.astype(BF16)

    if grid2d:
        assert TILE_M % GRID_W == 0
        a_parts.insert(a_parts.index(a_xp) + 1, a_colpool)
        pool_parts = [functools.partial(pool_grid_rows, g) for g in range(N_GROUPS)]
    else:
        pool_parts = [pool_pad] + [functools.partial(pool_seq, g) for g in range(N_GROUPS)]

    def keep_mask(d, rb, col_lo, ncols):
        t_loc = rb * RB + lax.broadcasted_iota(jnp.int32, (RB, ncols), 0)
        s_loc = col_lo + lax.broadcasted_iota(jnp.int32, (RB, ncols), 1)
        return s_loc <= t_loc if d == 0 else s_loc >= t_loc

    def mlstm_prepare(d, c, m_in):
        c0 = _aligned(c * L, L)
        colf = colf_scr[pl.ds(c0, L), :]
        bcol = colf[:, N_HEADS * d:N_HEADS * (d + 1)]
        cmcol = colf[:, 2 * N_HEADS + N_HEADS * d:2 * N_HEADS + N_HEADS * (d + 1)]
        mx = jnp.maximum(m_in, cmcol)
        last = L - 1 if d == 0 else 0
        mx_last = mx[last:last + 1, :]
        return dict(c0=c0, mx=mx, mx_last=mx_last, w_inter=jnp.exp(m_in - mx), e_negm=jnp.exp(-(bcol + mx)),
                    decay=jnp.exp(m_in - mx_last), m_new=bcol[last:last + 1, :] + mx_last)

    def block_cols(d, rb):
        return (0, (rb + 1) * RB) if d == 0 else (rb * RB, L)

    def mlstm_scores(d, p, h):
        c0 = p["c0"]
        hc = slice(h * HEAD_DIM, (h + 1) * HEAD_DIM)
        qks = []
        for rb in range(L // RB):
            col_lo, col_hi = block_cols(d, rb)
            qb = q_scr[pl.ds(_aligned(c0 + rb * RB, RB), RB), hc]
            qks.append(_dot(qb, kt_scr[hc, pl.ds(_aligned(c0 + col_lo, RB), col_hi - col_lo)]))
        p["qk", h] = qks

    def mlstm_head(d, p, h):
        c0, mx, mx_last = p["c0"], p["mx"], p["mx_last"]
        rows = pl.ds(c0, L)
        hc = slice(h * HEAD_DIM, (h + 1) * HEAD_DIM)
        ac = slice(2 * h * HEAD_DIM, (2 * h + 2) * HEAD_DIM)
        grow = grow_scr[N_HEADS * d + h:N_HEADS * d + h + 1, rows]
        if has_state:
            cn_dh = cn_scr[d, h]
            cn_bf = cn_dh.astype(BF16)
        for rb in range(L // RB):
            col_lo, col_hi = block_cols(d, rb)
            ncols = col_hi - col_lo
            trow = slice(rb * RB, (rb + 1) * RB)
            rrows = pl.ds(_aligned(c0 + rb * RB, RB), RB)
            crows = pl.ds(_aligned(c0 + col_lo, RB), ncols)
            qb = q_scr[rrows, hc]
            qk = p["qk", h][rb]
            w = jnp.exp(grow[:, col_lo:col_hi] - mx[trow, h:h + 1])
            s = (qk * jnp.where(keep_mask(d, rb, col_lo, ncols), w, 0.0)).astype(BF16)
            na = _dot(s, vaug_scr[crows, ac])
            if has_state:
                na = na + p["w_inter"][trow, h:h + 1] * _dot(qb, cn_bf)
            hval = na[:, 0:HEAD_DIM] / jnp.maximum(jnp.abs(na[:, HEAD_DIM:]), p["e_negm"][trow, h:h + 1])
            if d == 0:
                h_scr[rrows, hc] = hval
            else:
                h_scr[rrows, hc] += hval
        w_state = jnp.exp(grow - mx_last[:, h:h + 1])
        kw_t = (kt_scr[hc, rows].astype(F32) * w_state).astype(BF16)
        upd = _dot(kw_t, vaug_scr[rows, ac])
        if has_state:
            cn_scr[d, h] = p["decay"][:, h:h + 1] * cn_dh + upd
        else:
            cout_ref[d, h] = upd[:, 0:HEAD_DIM]
            nout_ref[d, h:h + 1, :] = upd[:, HEAD_DIM:].T[0:1, :]

    def merge_ab(c, nj):
        nc = slice(nj * NW, (nj + 1) * NW)
        ub = u_scr[tile_rows(c), :]
        acc = None
        for r, y_scr in enumerate((ya_scr, yb_scr)):
            gate = jax.nn.sigmoid(_dot(ub, wbrg_ref[:, r * D_MODEL + nj * NW:r * D_MODEL + (nj + 1) * NW]))
            term = gate * _dot(y_scr[tile_rows(c), :], wbranch_ref[r * D_BRANCH:(r + 1) * D_BRANCH, nc])
            acc = term if acc is None else acc + term
        mab_scr[tile_rows(c), nc] = acc

    def gate_c(c, nj):
        w = wbrg_ref[:, 2 * D_MODEL + nj * NW:2 * D_MODEL + (nj + 1) * NW]
        gc_scr[:, nj * NW:(nj + 1) * NW] = jax.nn.sigmoid(_dot(u_scr[tile_rows(c), :], w))

    def tail_yc(c):
        gml = gml_ref[...]
        for h in range(N_HEADS):
            hc = slice(h * HEAD_DIM, (h + 1) * HEAD_DIM)
            hh = h_scr[tile_rows(c), hc]
            hn = hh * lax.rsqrt(jnp.mean(hh * hh, axis=-1, keepdims=True) + EPS) * gml[:, hc]
            yc_scr[:, hc] = (jax.nn.sigmoid(o_scr[tile_rows(c), hc]) * hn).astype(BF16)

    def tail_merge(c, nj):
        nc = slice(nj * NW, (nj + 1) * NW)
        acc = mab_scr[tile_rows(c), nc] + gc_scr[:, nc] * _dot(yc_scr[...], wbranch_ref[2 * D_BRANCH:, nc])
        merged_scr[:, nc] = acc.astype(BF16)

    def tail_out(c):
        gt1 = mod_ref[...][:, 2 * D_MODEL:3 * D_MODEL]
        out_ref[tile_rows(c), :] = x_ref[tile_rows(c), :] + gt1 * _dot(merged_scr[...], wout_ref[...])

    def tail_parts(c):
        return ([functools.partial(tail_yc, c)] + [functools.partial(tail_merge, c, nj) for nj in range(n_merge)]
                + [functools.partial(tail_out, c)])

    def scan_parts(d, c, m_in, fillers, out):
        p = {}

        def prepare():
            p.update(mlstm_prepare(d, c, m_in))
            out[d] = p["m_new"]

        parts = [prepare]
        for h in range(N_HEADS):
            parts.append(functools.partial(mlstm_scores, d, p, h))
            if h < len(fillers):
                parts.append(fillers[h])
            parts.append(functools.partial(mlstm_head, d, p, h))
        return parts + list(fillers[N_HEADS:])

    if n_tiles == 1 and n_chunks == 1:
        zeros = jnp.zeros((1, N_HEADS), F32)
        m_out = {}
        parts = [functools.partial(part, 0) for part in a_parts] + pool_parts
        parts += scan_parts(0, 0, zeros, [functools.partial(merge_ab, 0, nj) for nj in range(n_merge)], m_out)
        parts += scan_parts(1, 0, zeros, [functools.partial(gate_c, 0, nj) for nj in range(n_merge)], m_out)
        parts += tail_parts(0)

        def write_m():
            mout_ref[0:1, :] = m_out[0]
            mout_ref[1:2, :] = m_out[1]

        return parts + [write_m]

    def run_loops():
        def phase_a(i, carry):
            for part in a_parts:
                part(i)
            return carry

        lax.fori_loop(0, n_tiles, phase_a, 0)
        for part in pool_parts:
            part()
        for d in range(2):
            for h in range(N_HEADS):
                n_rows = jnp.broadcast_to(n0_ref[d, h:h + 1, :], (HEAD_DIM, HEAD_DIM))
                cn_scr[d, h] = jnp.concatenate([c0_ref[d, h], n_rows.T], axis=1)
        m_init = m0_ref[...]

        def fwd_step(j, m_in):
            m_out = {}
            for part in scan_parts(0, j, m_in, [functools.partial(merge_ab, j, nj) for nj in range(n_merge)], m_out):
                part()
            return m_out[0]

        def bwd_step(j, m_in):
            c = n_chunks - 1 - j
            m_out = {}
            for part in scan_parts(1, c, m_in, [functools.partial(gate_c, c, nj) for nj in range(n_merge)], m_out):
                part()
            for part in tail_parts(c):
                part()
            return m_out[1]

        lax.fori_loop(0, n_chunks, fwd_step, m_init[0:1, :])
        lax.fori_loop(0, n_chunks, bwd_step, m_init[1:2, :])

    assert has_state, "multi-chunk sequences carry the cached state"
    return [run_loops]


def _mixer_call(l, xall, lay, prm, stream, states=None, state_outs=None):
    ctx = stream == "ctx"
    T = SEQ if ctx else DEC_SEQ
    assert T % TILE_M == 0 and TILE_M == MLSTM_L and MLSTM_L % MLSTM_ROW_BLOCK == 0
    n_b = BATCH if ctx else DEC_BATCH
    blk0 = 0 if ctx else N_TOK_CTX // T
    once = pl.Buffered(1)

    def wspec(shape):
        nd = len(shape)
        return pl.BlockSpec((None,) + shape, lambda b: (l,) + (0,) * nd, pipeline_mode=once)

    def lspec(shape):
        return pl.BlockSpec(shape, lambda b: (0,) * len(shape), pipeline_mode=once)

    x_spec = pl.BlockSpec((T, D_MODEL), lambda b: (blk0 + b, 0), pipeline_mode=None if ctx else once)
    in_specs = [
        x_spec,
        lspec((COND_ROWS, 6 * D_MODEL)),
        lspec((DEPTH, D_MODEL)),
        wspec((D_MODEL, N_MAIN)),
        wspec((D_BRANCH, D_MODEL)),
        wspec((D_MODEL, GATE_PAD)),
        lspec((D_MODEL, N_BRANCH * D_MODEL)),
        lspec((N_BRANCH * D_BRANCH, D_MODEL)),
        lspec((D_MODEL, D_MODEL)),
        wspec((N_GROUPS, GROUP, GROUP)),
        lspec((DEPTH, D_BRANCH)),
        lspec((DEPTH, D_BRANCH)),
        wspec((N_GROUPS, SGU_CHUNK, SGU_CHUNK)),
        wspec((SGU_CHUNK, D_BRANCH)),
        lspec((DEPTH, N_GATE)),
        lspec((DEPTH, D_BRANCH)),
    ]
    args = [xall, lay["mods"], prm["g_norm1"], prm["w_in16"], prm["w_kt"],
            prm["w_gate"], lay["w_brg"], lay["w_branch"], lay["w_out"],
            prm["w_pool"], prm["pool_scale"], prm["g_sgu"], prm["w_sgu"],
            prm["b_sgu_tile"], prm["b_gates"], prm["g_mlstm"]]
    out_shape = [jax.ShapeDtypeStruct(xall.shape, xall.dtype)]
    out_specs = [x_spec]
    aliases = {0: 0}
    passthrough = []
    if ctx:
        per_seq = [(2, N_HEADS, HEAD_DIM, HEAD_DIM), (2, N_HEADS, HEAD_DIM), (2, N_HEADS)]
        out_shape += [jax.ShapeDtypeStruct((BATCH, DEPTH) + s, F32) for s in per_seq]
        if state_outs is None:
            out_specs += [pl.BlockSpec((None, DEPTH) + s, lambda b, nd=len(s): (b,) + (0,) * (nd + 1))
                          for s in per_seq]
        else:
            out_specs += [pl.BlockSpec((None, None) + s, lambda b, nd=len(s): (b, l) + (0,) * nd) for s in per_seq]
            passthrough += [(a, 1 + k) for k, a in enumerate(state_outs)]
    else:
        in_specs += [
            pl.BlockSpec((None, None, 2, N_HEADS, HEAD_DIM, HEAD_DIM), lambda b: (b, l, 0, 0, 0, 0)),
            pl.BlockSpec((None, None, 2, N_HEADS, HEAD_DIM), lambda b: (b, l, 0, 0, 0)),
            pl.BlockSpec((None, None, 2, N_HEADS), lambda b: (b, l, 0, 0)),
        ]
        args += list(states)
    for buf, out_idx in passthrough:
        aliases[len(args)] = out_idx
        in_specs.append(pl.BlockSpec(memory_space=pl.ANY))
        args.append(buf)

    scratch = [
        pltpu.VMEM((T, D_MODEL), BF16),
        pltpu.VMEM((T + 2 * POOL_PAD, D_BRANCH), F32),
        pltpu.VMEM((TILE_M, D_BRANCH), F32),
        pltpu.VMEM((TILE_M, D_BRANCH), BF16),
        pltpu.VMEM((T, D_BRANCH), BF16),
        pltpu.VMEM((D_BRANCH, T), BF16),
        pltpu.VMEM((T, 2 * D_BRANCH), BF16),
        pltpu.VMEM((T, D_BRANCH), F32),
        pltpu.VMEM((2 * N_HEADS, T), F32),
        pltpu.VMEM((T, GATE_PAD), F32),
        pltpu.VMEM((T, D_BRANCH), F32),
        pltpu.VMEM((T, D_BRANCH), BF16),
        pltpu.VMEM((T, D_BRANCH), BF16),
        pltpu.VMEM((TILE_M, D_BRANCH), BF16),
        pltpu.VMEM((T, D_MODEL), F32),
        pltpu.VMEM((TILE_M, D_MODEL), F32),
        pltpu.VMEM((TILE_M, D_MODEL), BF16),
    ]
    if not ctx:
        scratch += [pltpu.VMEM((2, N_HEADS, HEAD_DIM, 2 * HEAD_DIM), F32)]
        scratch += [pltpu.VMEM((T, D_BRANCH), F32), pltpu.VMEM((T, GROUP), F32)]

    outs = pl.pallas_call(
        functools.partial(_mixer_kernel, T=T, grid2d=not ctx, has_state=not ctx, layer=l,
                          n_in=len(in_specs), n_pass=len(passthrough), n_out=len(out_specs)),
        grid=(n_b,),
        in_specs=in_specs,
        out_specs=out_specs,
        out_shape=out_shape,
        scratch_shapes=scratch,
        input_output_aliases=aliases,
        compiler_params=pltpu.CompilerParams(
            dimension_semantics=("arbitrary",),
            vmem_limit_bytes=_vmem_request(list(zip(args, in_specs)) + list(zip(out_shape, out_specs)), scratch)),
        name="mixer_ctx" if ctx else "mixer_lat",
    )(*args)
    return outs


def _ffn_kernel(x_ref, mod_ref, g2_ref, w1_hbm, w2_hbm, gfin_ref, *rest, layer, final):
    if final:
        out_refs, rest = rest[:2], rest[2:]
    else:
        (cond_ref, wada_ref, bada_ref, wbr_ref, wo_ref, win_ref), rest = rest[:6], rest[6:]
        out_refs, (mods_out, wbr_out, wo_out, wbrg_out), rest = rest[:1], rest[1:5], rest[5:]
    acc_scr, h_scr, w1_scr, w2_scr, st1_scr, st2_scr, sem = rest
    n_chunks = D_FF // FFN_CHUNK
    tile = pl.program_id(0)
    first_tile = tile == 0
    ctx_tiles, lat_tiles = N_TOK_CTX // FFN_TILE_M, DEC_SEQ // FFN_TILE_M
    mod_ref = mod_ref.at[pl.ds(jnp.where(tile < ctx_tiles, 0, 1 + (tile - ctx_tiles) // lat_tiles), 1), :]
    g2_ref = g2_ref.at[pl.ds(layer, 1), :]

    def w1_copy(c):
        slot = c % FFN_STAGES
        c0 = _aligned(c * FFN_CHUNK, FFN_CHUNK)
        return pltpu.make_async_copy(w1_hbm.at[layer, :, pl.ds(c0, FFN_CHUNK)], st1_scr.at[slot], sem.at[0, slot])

    def w2_copy(c):
        slot = c % FFN_STAGES
        c0 = _aligned(c * FFN_CHUNK, FFN_CHUNK)
        return pltpu.make_async_copy(w2_hbm.at[layer, pl.ds(c0, FFN_CHUNK), :], st2_scr.at[slot], sem.at[1, slot])

    def land_w1(c):
        w1_copy(c).wait()
        c0 = _aligned(c * FFN_CHUNK, FFN_CHUNK)
        w1_scr[:, pl.ds(c0, FFN_CHUNK)] = st1_scr[c % FFN_STAGES].astype(BF16)

    def land_w2(c):
        w2_copy(c).wait()
        c0 = _aligned(c * FFN_CHUNK, FFN_CHUNK)
        w2_scr[pl.ds(c0, FFN_CHUNK), :] = st2_scr[c % FFN_STAGES].astype(BF16)

    def prepare_next_layer():
        if not final:
            _ada_block(cond_ref, wada_ref, bada_ref, mods_out, layer=layer + 1)
            wbr_out[...] = wbr_ref[...].astype(BF16)
            wo_out[...] = wo_ref[...].astype(BF16)
            n_br = N_BRANCH * D_MODEL
            for c0 in range(0, n_br, FFN_CHUNK):
                width = min(FFN_CHUNK + GATE_PAD, n_br + N_GATE - c0)
                window = win_ref[:, N_MAIN + c0:N_MAIN + c0 + width]
                wbrg_out[:, c0:c0 + FFN_CHUNK] = window[:, N_GATE:N_GATE + FFN_CHUNK]

    def normed_input():
        mod = mod_ref[...]
        sh2 = mod[:, 3 * D_MODEL:4 * D_MODEL]
        sc2 = mod[:, 4 * D_MODEL:5 * D_MODEL]
        x = x_ref[...]
        ms = jnp.mean(x * x, axis=-1, keepdims=True)
        return (x * lax.rsqrt(ms + EPS) * g2_ref[...] * (1.0 + sc2) + sh2).astype(BF16)

    def hidden(ub, c):
        c0 = _aligned(c * FFN_CHUNK, FFN_CHUNK)
        h = _dot(ub, w1_scr[:, pl.ds(c0, FFN_CHUNK)])
        h_scr[:, pl.ds(c0, FFN_CHUNK)] = jnp.square(jnp.maximum(h, 0.0)).astype(BF16)

    def finish(ffn_out):
        gt2 = mod_ref[...][:, 5 * D_MODEL:6 * D_MODEL]
        y = x_ref[...] + gt2 * ffn_out
        if not final:
            out_refs[0][...] = y
            return
        y = y * lax.rsqrt(jnp.mean(y * y, axis=-1, keepdims=True) + EPS) * gfin_ref[...]
        in_ctx = tile < ctx_tiles

        @pl.when(in_ctx)
        def _():
            out_refs[0][...] = y

        @pl.when(jnp.logical_not(in_ctx))
        def _():
            out_refs[1][...] = y

    @pl.when(first_tile)
    def _():
        for c in range(FFN_STAGES):
            w1_copy(c).start()
            w2_copy(c).start()
        prepare_next_layer()
        ub = normed_input()
        acc_scr[...] = jnp.zeros_like(acc_scr)

        def chunk(c, carry):
            land_w1(c)
            land_w2(c)

            @pl.when(c + FFN_STAGES < n_chunks)
            def _():
                w1_copy(c + FFN_STAGES).start()
                w2_copy(c + FFN_STAGES).start()

            hidden(ub, c)
            c0 = _aligned(c * FFN_CHUNK, FFN_CHUNK)
            acc_scr[...] += _dot(h_scr[:, pl.ds(c0, FFN_CHUNK)], w2_scr[pl.ds(c0, FFN_CHUNK), :])
            return carry

        lax.fori_loop(0, n_chunks, chunk, 0)
        finish(acc_scr[...])

    @pl.when(jnp.logical_not(first_tile))
    def _():
        prepare_next_layer()
        ub = normed_input()
        for c in range(n_chunks):
            hidden(ub, c)
        finish(_dot(h_scr[...], w2_scr[...]))


def _ffn_call(l, xall, lay, prm, g_final, final):
    n_tiles = N_TOK // FFN_TILE_M
    ctx_tiles = N_TOK_CTX // FFN_TILE_M
    lat_tiles = DEC_SEQ // FFN_TILE_M
    once = pl.Buffered(1)

    x_spec = pl.BlockSpec((FFN_TILE_M, D_MODEL), lambda t: (t, 0))
    in_specs = [
        x_spec,
        pl.BlockSpec((COND_ROWS, 6 * D_MODEL), lambda t: (0, 0), pipeline_mode=once),
        pl.BlockSpec((DEPTH, D_MODEL), lambda t: (0, 0), pipeline_mode=once),
        pl.BlockSpec(memory_space=pl.ANY),
        pl.BlockSpec(memory_space=pl.ANY),
        pl.BlockSpec((1, D_MODEL), lambda t: (0, 0)),
    ]
    args = [xall, lay["mods"], prm["g_norm2"], prm["w_ff1"], prm["w_ff2"], g_final.reshape(1, D_MODEL)]
    if final:
        out_specs = [
            pl.BlockSpec((FFN_TILE_M, D_MODEL), lambda t: (jnp.minimum(t, ctx_tiles - 1), 0)),
            pl.BlockSpec((FFN_TILE_M, D_MODEL), lambda t: (jnp.maximum(t - ctx_tiles, 0), 0)),
        ]
        out_shape = [jax.ShapeDtypeStruct((N_TOK_CTX, D_MODEL), F32),
                     jax.ShapeDtypeStruct((N_TOK - N_TOK_CTX, D_MODEL), F32)]
        aliases = {}
    else:
        ada_cols = 6 * D_MODEL // n_tiles
        br_rows = N_BRANCH * D_BRANCH // n_tiles
        wo_tiles = D_MODEL // br_rows
        assert ada_cols % LANES == 0 and br_rows % BF16_SUBLANES == 0 and wo_tiles <= n_tiles
        wo_map = lambda t: (jnp.minimum(t, wo_tiles - 1), 0)
        in_specs += [
            pl.BlockSpec((COND_ROWS, D_MODEL), lambda t: (0, 0)),
            pl.BlockSpec((None, D_MODEL, ada_cols), lambda t: (l + 1, 0, t)),
            pl.BlockSpec((DEPTH, ada_cols), lambda t: (0, t)),
            pl.BlockSpec((None, br_rows, D_MODEL), lambda t: (l + 1, t, 0)),
            pl.BlockSpec((None, br_rows, D_MODEL), lambda t: (l + 1,) + wo_map(t)),
            pl.BlockSpec((None, br_rows, prm["w_in16"].shape[-1]), lambda t: (l + 1,) + wo_map(t)),
        ]
        args += [prm["cond8"], prm["w_ada"], prm["b_ada"], prm["w_branch_rows"], prm["w_out"], prm["w_in16"]]
        out_specs = [
            x_spec,
            pl.BlockSpec((COND_ROWS, ada_cols), lambda t: (0, t)),
            pl.BlockSpec((br_rows, D_MODEL), lambda t: (t, 0)),
            pl.BlockSpec((br_rows, D_MODEL), wo_map),
            pl.BlockSpec((br_rows, N_BRANCH * D_MODEL), wo_map),
        ]
        out_shape = [jax.ShapeDtypeStruct(xall.shape, xall.dtype),
                     jax.ShapeDtypeStruct((COND_ROWS, 6 * D_MODEL), F32),
                     jax.ShapeDtypeStruct((N_BRANCH * D_BRANCH, D_MODEL), BF16),
                     jax.ShapeDtypeStruct((D_MODEL, D_MODEL), BF16),
                     jax.ShapeDtypeStruct((D_MODEL, N_BRANCH * D_MODEL), BF16)]
        aliases = {0: 0}
    scratch = [
        pltpu.VMEM((FFN_TILE_M, D_MODEL), F32),
        pltpu.VMEM((FFN_TILE_M, D_FF), BF16),
        pltpu.VMEM((D_MODEL, D_FF), BF16),
        pltpu.VMEM((D_FF, D_MODEL), BF16),
        pltpu.VMEM((FFN_STAGES, D_MODEL, FFN_CHUNK), F32),
        pltpu.VMEM((FFN_STAGES, FFN_CHUNK, D_MODEL), F32),
    ]
    return pl.pallas_call(
        functools.partial(_ffn_kernel, layer=l, final=final),
        grid=(n_tiles,),
        in_specs=in_specs,
        out_specs=out_specs,
        out_shape=out_shape,
        scratch_shapes=scratch + [pltpu.SemaphoreType.DMA((2, FFN_STAGES))],
        input_output_aliases=aliases,
        compiler_params=pltpu.CompilerParams(
            dimension_semantics=("arbitrary",),
            vmem_limit_bytes=_vmem_request(list(zip(args, in_specs)) + list(zip(out_shape, out_specs)), scratch)),
        name="ffn_final" if final else "ffn",
    )(*args)


def kernel(x_prompt, x_sample, state_C, state_n, state_m, c, c_ctx, w_ada, b_ada, g_norm1, g_norm2, w_in,
           b_gates, w_pool, pool_scale, g_sgu, w_sgu, b_sgu, g_mlstm, w_branch, w_out, w_ff1, w_ff2, g_final):
    w_in16 = w_in.astype(BF16)
    prm = {
        "w_in16": w_in16,
        "w_gate": jnp.pad(w_in16[:, :, N_MAIN:N_MAIN + N_GATE], ((0, 0), (0, 0), (0, GATE_PAD - N_GATE))),
        "w_kt": jnp.swapaxes(w_in16[:, :, 4 * D_BRANCH:5 * D_BRANCH], 1, 2),
        "w_branch_rows": w_branch.reshape(DEPTH, N_BRANCH * D_BRANCH, D_MODEL),
        "w_out": w_out,
        "w_ada": w_ada,
        "b_ada": b_ada,
        "w_pool": w_pool.astype(BF16),
        "w_sgu": w_sgu.astype(BF16),
        "w_ff1": w_ff1,
        "w_ff2": w_ff2,
        "g_norm1": g_norm1,
        "g_norm2": g_norm2,
        "pool_scale": pool_scale,
        "g_sgu": g_sgu,
        "g_mlstm": g_mlstm,
        "b_gates": b_gates,
        "b_sgu_tile": jnp.repeat(jnp.swapaxes(b_sgu, 1, 2), GROUP, axis=2),
        "cond8": jnp.concatenate(
            [c_ctx[None, :], c, jnp.zeros((COND_ROWS - 1 - DEC_BATCH, D_MODEL), F32)], axis=0),
    }
    lay = {
        "mods": _ada_call(prm["cond8"], w_ada, b_ada),
        "w_branch": prm["w_branch_rows"][0].astype(BF16),
        "w_out": w_out[0].astype(BF16),
        "w_brg": w_in16[0, :, N_MAIN + N_GATE:],
    }

    xall = jnp.concatenate([x_prompt.reshape(N_TOK_CTX, D_MODEL), x_sample.reshape(-1, D_MODEL)], axis=0)
    new_states = None
    for l in range(DEPTH):
        xall, *new_states = _mixer_call(l, xall, lay, prm, "ctx", state_outs=new_states)
        (xall,) = _mixer_call(l, xall, lay, prm, "lat", states=(state_C, state_n, state_m))
        if l < DEPTH - 1:
            xall, *nxt = _ffn_call(l, xall, lay, prm, g_final, final=False)
            lay = dict(zip(("mods", "w_branch", "w_out", "w_brg"), nxt))
    y_prompt, y_sample = _ffn_call(DEPTH - 1, xall, lay, prm, g_final, final=True)
    return (y_prompt.reshape(BATCH, SEQ, D_MODEL), y_sample.reshape(DEC_BATCH, DEC_SEQ, D_MODEL), *new_states)
```

```python
import functools
import math

import jax
import jax.numpy as jnp
from jax import lax
from jax.experimental import pallas as pl
from jax.experimental.pallas import tpu as pltpu

F32 = jnp.float32
BF16 = jnp.bfloat16

D_MODEL = 1024
DEPTH = 4
BATCH = 16
SEQ = 256
DEC_BATCH = 2
DEC_SEQ = 1024
GRID_W = 64
D_BRANCH = 512
POOL_WINDOWS = (2, 4, 8, 16)
N_GROUPS = 4
GROUP = D_BRANCH // N_GROUPS
SGU_CHUNK = 128
N_HEADS = 4
HEAD_DIM = D_BRANCH // N_HEADS
N_BRANCH = 3
D_FF = 4 * D_MODEL
EPS = 1e-6

LANES = 128
BF16_SUBLANES = 16

N_MAIN = 7 * D_BRANCH
N_GATE = 4 * N_HEADS
GATE_PAD = LANES
N_TOK_CTX = BATCH * SEQ
N_TOK = N_TOK_CTX + DEC_BATCH * DEC_SEQ
COND_ROWS = 8

TILE_M = 256
MLSTM_L = 256
MLSTM_ROW_BLOCK = 128
MERGE_COLS = 256
POOL_PAD = 8
FFN_TILE_M = 512
FFN_CHUNK = 512
FFN_STAGES = 2
ADA_TILE_N = 1536
V7X_VMEM_BYTES = 64 * 1024 * 1024
VMEM_COMPILER_RESERVE = 6 * 1024 * 1024
VMEM_CALL_FLOOR = 60 * 1024 * 1024


def _vmem_request(windows, scratch):
    total = VMEM_COMPILER_RESERVE
    for arr, spec in windows:
        if spec.block_shape is not None:
            n_buf = 2 if spec.pipeline_mode is None else spec.pipeline_mode.buffer_count
            total += n_buf * math.prod(d for d in spec.block_shape if d is not None) * jnp.dtype(arr.dtype).itemsize
    for buf in scratch:
        total += math.prod(buf.shape) * jnp.dtype(buf.dtype).itemsize
    return min(max(total, VMEM_CALL_FLOOR), V7X_VMEM_BYTES)


_dot = functools.partial(jnp.dot, preferred_element_type=F32)
_dot_nt = functools.partial(lax.dot_general, dimension_numbers=(((1,), (1,)), ((), ())),
                            preferred_element_type=F32)


def _aligned(offset, align):
    return offset if isinstance(offset, int) else pl.multiple_of(offset, align)


def _ada_block(cond_ref, w_ref, b_ref, out_ref, *, layer=0):
    cond = cond_ref[...]
    s = (cond * jax.nn.sigmoid(cond)).astype(BF16)
    out_ref[...] = _dot(s, w_ref[...].astype(BF16)) + b_ref[layer:layer + 1, :]


def _ada_call(cond8, w_ada, b_ada):
    n_out = 6 * D_MODEL
    args = (cond8, w_ada, b_ada)
    in_specs = [
        pl.BlockSpec((COND_ROWS, D_MODEL), lambda j: (0, 0)),
        pl.BlockSpec((None, D_MODEL, ADA_TILE_N), lambda j: (0, 0, j)),
        pl.BlockSpec((DEPTH, ADA_TILE_N), lambda j: (0, j)),
    ]
    out_spec = pl.BlockSpec((COND_ROWS, ADA_TILE_N), lambda j: (0, j))
    out_shape = jax.ShapeDtypeStruct((COND_ROWS, n_out), F32)
    return pl.pallas_call(
        _ada_block,
        grid=(n_out // ADA_TILE_N,),
        in_specs=in_specs,
        out_specs=out_spec,
        out_shape=out_shape,
        compiler_params=pltpu.CompilerParams(
            dimension_semantics=("arbitrary",),
            vmem_limit_bytes=_vmem_request(list(zip(args, in_specs)) + [(out_shape, out_spec)], [])),
        name="ada_mod",
    )(*args)


def _seg_scan(x, pos, seg, total, op, reverse):
    d = 1
    while d < seg:
        if reverse:
            shifted = pltpu.roll(x, total - d, 1)
            valid = pos < seg - d
        else:
            shifted = pltpu.roll(x, d, 1)
            valid = pos >= d
        if op == "sum":
            x = x + jnp.where(valid, shifted, 0.0)
        else:
            x = jnp.maximum(x, jnp.where(valid, shifted, -jnp.inf))
        d *= 2
    return x


_MIXER_LAYER_ROW_INPUTS = (2, 10, 11, 14, 15)


def _mixer_kernel(*refs, T, grid2d, has_state, layer, n_in, n_pass, n_out):
    ins, outs, scr = list(refs[:n_in - n_pass]), refs[n_in:n_in + n_out], refs[n_in + n_out:]
    ins[1] = ins[1].at[pl.ds(1 + pl.program_id(0) if has_state else 0, 1), :]
    for k in _MIXER_LAYER_ROW_INPUTS:
        ins[k] = ins[k].at[pl.ds(layer, 1), :]
    state_outs = list(outs[1:])
    if state_outs and n_pass == 0:
        for k, o in enumerate(state_outs):
            o[...] = jnp.zeros(o.shape, o.dtype)
            state_outs[k] = o.at[layer]
    for thunk in _mixer_sequence(*ins, outs[0], *state_outs, *scr, T=T, grid2d=grid2d, has_state=has_state):
        thunk()


def _mixer_sequence(*refs, T, grid2d, has_state):
    it = iter(refs)
    x_ref = next(it)
    mod_ref = next(it)
    g1_ref = next(it)
    wmain_ref = next(it)
    wkt_ref = next(it)
    wgate_ref = next(it)
    wbrg_ref = next(it)
    wbranch_ref = next(it)
    wout_ref = next(it)
    wpool_ref = next(it)
    pscale_ref = next(it)
    gsgu_ref = next(it)
    wsgu_ref = next(it)
    bsgu_ref = next(it)
    bgates_ref = next(it)
    gml_ref = next(it)
    if has_state:
        c0_ref, n0_ref, m0_ref = next(it), next(it), next(it)
    out_ref = next(it)
    if not has_state:
        cout_ref, nout_ref, mout_ref = next(it), next(it), next(it)
    (u_scr, xp_scr, su_scr, vn_scr, q_scr, kt_scr, vaug_scr, o_scr, grow_scr, colf_scr,
     h_scr, ya_scr, yb_scr, yc_scr, mab_scr, gc_scr, merged_scr) = [next(it) for _ in range(17)]
    if has_state:
        cn_scr = next(it)
    if grid2d:
        m1_scr, dlt_scr = next(it), next(it)

    L = MLSTM_L
    n_tiles = T // TILE_M
    n_chunks = T // L
    RB = MLSTM_ROW_BLOCK
    NW = MERGE_COLS
    n_merge = D_MODEL // NW

    def tile_rows(i):
        return pl.ds(_aligned(i * TILE_M, TILE_M), TILE_M)

    def a_norm(i):
        mod = mod_ref[...]
        sh1 = mod[:, 0:D_MODEL]
        sc1 = mod[:, D_MODEL:2 * D_MODEL]
        x = x_ref[tile_rows(i), :]
        ms = jnp.mean(x * x, axis=-1, keepdims=True)
        u = x * lax.rsqrt(ms + EPS) * g1_ref[...] * (1.0 + sc1) + sh1
        u_scr[tile_rows(i), :] = u.astype(BF16)

    def a_gates(i):
        lane = lax.broadcasted_iota(jnp.int32, (2 * N_HEADS, TILE_M), 1)
        is_fwd = lax.broadcasted_iota(jnp.int32, (2 * N_HEADS, TILE_M), 0) < N_HEADS
        bias = jnp.concatenate([bgates_ref[...], jnp.zeros((1, GATE_PAD - N_GATE), F32)], axis=1)
        gate_t = (_dot(u_scr[tile_rows(i), :], wgate_ref[...]) + bias).T
        gp = gate_t[0:N_GATE, :]
        i_all = gp[0:2 * N_HEADS, :]
        f_all = gp[2 * N_HEADS:4 * N_HEADS, :]
        logf = jnp.minimum(f_all, 0.0) - jnp.log1p(jnp.exp(-jnp.abs(f_all)))
        b_row = jnp.where(is_fwd, _seg_scan(logf, lane, L, L, "sum", False),
                          _seg_scan(logf, lane, L, L, "sum", True))
        g_row = i_all - b_row
        cm_row = jnp.where(is_fwd, _seg_scan(g_row, lane, L, L, "max", False),
                           _seg_scan(g_row, lane, L, L, "max", True))
        grow_scr[:, tile_rows(i)] = g_row
        stacked = jnp.concatenate(
            [b_row, cm_row, jnp.zeros((GATE_PAD - 4 * N_HEADS, TILE_M), F32)], axis=0)
        colf_scr[tile_rows(i), :] = stacked.T

    def proj(i, k):
        return _dot(u_scr[tile_rows(i), :], wmain_ref[:, k * D_BRANCH:(k + 1) * D_BRANCH])

    def a_xp(i):
        xp_scr[pl.ds(_aligned(i * TILE_M + POOL_PAD, 8), TILE_M), :] = proj(i, 0)

    def a_su(i):
        su_scr[...] = proj(i, 1)

    def a_sv(i):
        sv = proj(i, 2)
        vn = sv * lax.rsqrt(jnp.mean(sv * sv, axis=-1, keepdims=True) + EPS) * gsgu_ref[...]
        vn_scr[...] = vn.astype(BF16)

    def a_q(i):
        q_scr[tile_rows(i), :] = proj(i, 3).astype(BF16)

    def a_k(i):
        kt = _dot_nt(wkt_ref[...], u_scr[tile_rows(i), :])
        kt_scr[:, tile_rows(i)] = (kt * (HEAD_DIM ** -0.5)).astype(BF16)

    def a_v(i):
        v = proj(i, 5).astype(BF16)
        ones_blk = jnp.ones((TILE_M, HEAD_DIM), BF16)
        for h in range(N_HEADS):
            vaug_scr[tile_rows(i), 2 * h * HEAD_DIM:(2 * h + 1) * HEAD_DIM] = v[:, h * HEAD_DIM:(h + 1) * HEAD_DIM]
            vaug_scr[tile_rows(i), (2 * h + 1) * HEAD_DIM:(2 * h + 2) * HEAD_DIM] = ones_blk

    def a_o(i):
        o_scr[tile_rows(i), :] = proj(i, 6)

    def a_sgu(ch, i):
        crow = slice(ch * SGU_CHUNK, (ch + 1) * SGU_CHUNK)
        for g in range(N_GROUPS):
            gc = slice(g * GROUP, (g + 1) * GROUP)
            mixed = _dot(wsgu_ref[g], vn_scr[crow, gc]) + bsgu_ref[:, gc]
            yb = su_scr[crow, gc] * mixed
            yb_scr[pl.ds(_aligned(i * TILE_M + ch * SGU_CHUNK, SGU_CHUNK), SGU_CHUNK), gc] = yb.astype(BF16)

    a_parts = [a_norm, a_gates, a_xp, a_su, a_sv, a_q, a_k, a_v, a_o]
    a_parts += [functools.partial(a_sgu, ch) for ch in range(TILE_M // SGU_CHUNK)]

    def pool_pad():
        zpad = jnp.zeros((POOL_PAD, D_BRANCH), F32)
        xp_scr[0:POOL_PAD, :] = zpad
        xp_scr[T + POOL_PAD:T + 2 * POOL_PAD, :] = zpad

    def pool_seq(g):
        win = POOL_WINDOWS[g]
        gc = slice(g * GROUP, (g + 1) * GROUP)
        half = win // 2
        t_idx = lax.broadcasted_iota(jnp.int32, (T, GROUP), 0)
        total = jnp.zeros((T, GROUP), F32)
        for j in range(-half, win - half):
            total = total + xp_scr[POOL_PAD + j:POOL_PAD + j + T, gc]
        cnt = (jnp.minimum(t_idx + (win - half), T) - jnp.maximum(t_idx - half, 0)).astype(F32)
        dlt = total / cnt - xp_scr[POOL_PAD:POOL_PAD + T, gc]
        ya = _dot(dlt.astype(BF16), wpool_ref[g]) * pscale_ref[...][:, gc]
        ya_scr[:, gc] = ya.astype(BF16)

    def a_colpool(i):
        c_idx = lax.broadcasted_iota(jnp.int32, (GRID_W, GROUP), 0)
        for g, win in enumerate(POOL_WINDOWS):
            gc = slice(g * GROUP, (g + 1) * GROUP)
            half = win // 2
            inv_cnt = 1.0 / (jnp.minimum(c_idx + (win - half), GRID_W) - jnp.maximum(c_idx - half, 0)).astype(F32)
            for r in range(TILE_M // GRID_W):
                r0 = i * TILE_M + r * GRID_W
                xg = xp_scr[pl.ds(_aligned(r0 + POOL_PAD, 8), GRID_W), gc]
                total = xg
                for j in range(-half, win - half):
                    if j != 0:
                        valid = (c_idx + j >= 0) & (c_idx + j < GRID_W)
                        total = total + jnp.where(valid, pltpu.roll(xg, (-j) % GRID_W, 0), 0.0)
                m1_scr[pl.ds(_aligned(r0, GRID_W), GRID_W), gc] = total * inv_cnt

    def pool_grid_rows(g):
        win = POOL_WINDOWS[g]
        gc = slice(g * GROUP, (g + 1) * GROUP)
        half = win // 2
        n_rows = T // GRID_W
        for r in range(n_rows):
            lo = max(r - half, 0)
            hi = min(r + (win - half), n_rows)
            acc = m1_scr[lo * GRID_W:(lo + 1) * GRID_W, gc]
            for rr in range(lo + 1, hi):
                acc = acc + m1_scr[rr * GRID_W:(rr + 1) * GRID_W, gc]
            xg = xp_scr[POOL_PAD + r * GRID_W:POOL_PAD + (r + 1) * GRID_W, gc]
            dlt_scr[r * GRID_W:(r + 1) * GRID_W, :] = acc / float(hi - lo) - xg
        ya = _dot(dlt_scr[...].astype(BF16), wpool_ref[g]) * pscale_ref[...][:, gc]
        ya_scr[:, gc] = ya.astype(BF16)

    if grid2d:
        assert TILE_M % GRID_W == 0
        a_parts.insert(a_parts.index(a_xp) + 1, a_colpool)
        pool_parts = [functools.partial(pool_grid_rows, g) for g in range(N_GROUPS)]
    else:
        pool_parts = [pool_pad] + [functools.partial(pool_seq, g) for g in range(N_GROUPS)]

    def keep_mask(d, rb, col_lo, ncols):
        t_loc = rb * RB + lax.broadcasted_iota(jnp.int32, (RB, ncols), 0)
        s_loc = col_lo + lax.broadcasted_iota(jnp.int32, (RB, ncols), 1)
        return s_loc <= t_loc if d == 0 else s_loc >= t_loc

    def mlstm_prepare(d, c, m_in):
        c0 = _aligned(c * L, L)
        colf = colf_scr[pl.ds(c0, L), :]
        bcol = colf[:, N_HEADS * d:N_HEADS * (d + 1)]
        cmcol = colf[:, 2 * N_HEADS + N_HEADS * d:2 * N_HEADS + N_HEADS * (d + 1)]
        mx = jnp.maximum(m_in, cmcol)
        last = L - 1 if d == 0 else 0
        mx_last = mx[last:last + 1, :]
        return dict(c0=c0, mx=mx, mx_last=mx_last, w_inter=jnp.exp(m_in - mx), e_negm=jnp.exp(-(bcol + mx)),
                    decay=jnp.exp(m_in - mx_last), m_new=bcol[last:last + 1, :] + mx_last)

    def block_cols(d, rb):
        return (0, (rb + 1) * RB) if d == 0 else (rb * RB, L)

    def mlstm_scores(d, p, h):
        c0 = p["c0"]
        hc = slice(h * HEAD_DIM, (h + 1) * HEAD_DIM)
        qks = []
        for rb in range(L // RB):
            col_lo, col_hi = block_cols(d, rb)
            qb = q_scr[pl.ds(_aligned(c0 + rb * RB, RB), RB), hc]
            qks.append(_dot(qb, kt_scr[hc, pl.ds(_aligned(c0 + col_lo, RB), col_hi - col_lo)]))
        p["qk", h] = qks

    def mlstm_head(d, p, h):
        c0, mx, mx_last = p["c0"], p["mx"], p["mx_last"]
        rows = pl.ds(c0, L)
        hc = slice(h * HEAD_DIM, (h + 1) * HEAD_DIM)
        ac = slice(2 * h * HEAD_DIM, (2 * h + 2) * HEAD_DIM)
        grow = grow_scr[N_HEADS * d + h:N_HEADS * d + h + 1, rows]
        if has_state:
            cn_dh = cn_scr[d, h]
            cn_bf = cn_dh.astype(BF16)
        for rb in range(L // RB):
            col_lo, col_hi = block_cols(d, rb)
            ncols = col_hi - col_lo
            trow = slice(rb * RB, (rb + 1) * RB)
            rrows = pl.ds(_aligned(c0 + rb * RB, RB), RB)
            crows = pl.ds(_aligned(c0 + col_lo, RB), ncols)
            qb = q_scr[rrows, hc]
            qk = p["qk", h][rb]
            w = jnp.exp(grow[:, col_lo:col_hi] - mx[trow, h:h + 1])
            s = (qk * jnp.where(keep_mask(d, rb, col_lo, ncols), w, 0.0)).astype(BF16)
            na = _dot(s, vaug_scr[crows, ac])
            if has_state:
                na = na + p["w_inter"][trow, h:h + 1] * _dot(qb, cn_bf)
            hval = na[:, 0:HEAD_DIM] / jnp.maximum(jnp.abs(na[:, HEAD_DIM:]), p["e_negm"][trow, h:h + 1])
            if d == 0:
                h_scr[rrows, hc] = hval
            else:
                h_scr[rrows, hc] += hval
        w_state = jnp.exp(grow - mx_last[:, h:h + 1])
        kw_t = (kt_scr[hc, rows].astype(F32) * w_state).astype(BF16)
        upd = _dot(kw_t, vaug_scr[rows, ac])
        if has_state:
            cn_scr[d, h] = p["decay"][:, h:h + 1] * cn_dh + upd
        else:
            cout_ref[d, h] = upd[:, 0:HEAD_DIM]
            nout_ref[d, h:h + 1, :] = upd[:, HEAD_DIM:].T[0:1, :]

    def merge_ab(c, nj):
        nc = slice(nj * NW, (nj + 1) * NW)
        ub = u_scr[tile_rows(c), :]
        acc = None
        for r, y_scr in enumerate((ya_scr, yb_scr)):
            gate = jax.nn.sigmoid(_dot(ub, wbrg_ref[:, r * D_MODEL + nj * NW:r * D_MODEL + (nj + 1) * NW]))
            term = gate * _dot(y_scr[tile_rows(c), :], wbranch_ref[r * D_BRANCH:(r + 1) * D_BRANCH, nc])
            acc = term if acc is None else acc + term
        mab_scr[tile_rows(c), nc] = acc

    def gate_c(c, nj):
        w = wbrg_ref[:, 2 * D_MODEL + nj * NW:2 * D_MODEL + (nj + 1) * NW]
        gc_scr[:, nj * NW:(nj + 1) * NW] = jax.nn.sigmoid(_dot(u_scr[tile_rows(c), :], w))

    def tail_yc(c):
        gml = gml_ref[...]
        for h in range(N_HEADS):
            hc = slice(h * HEAD_DIM, (h + 1) * HEAD_DIM)
            hh = h_scr[tile_rows(c), hc]
            hn = hh * lax.rsqrt(jnp.mean(hh * hh, axis=-1, keepdims=True) + EPS) * gml[:, hc]
            yc_scr[:, hc] = (jax.nn.sigmoid(o_scr[tile_rows(c), hc]) * hn).astype(BF16)

    def tail_merge(c, nj):
        nc = slice(nj * NW, (nj + 1) * NW)
        acc = mab_scr[tile_rows(c), nc] + gc_scr[:, nc] * _dot(yc_scr[...], wbranch_ref[2 * D_BRANCH:, nc])
        merged_scr[:, nc] = acc.astype(BF16)

    def tail_out(c):
        gt1 = mod_ref[...][:, 2 * D_MODEL:3 * D_MODEL]
        out_ref[tile_rows(c), :] = x_ref[tile_rows(c), :] + gt1 * _dot(merged_scr[...], wout_ref[...])

    def tail_parts(c):
        return ([functools.partial(tail_yc, c)] + [functools.partial(tail_merge, c, nj) for nj in range(n_merge)]
                + [functools.partial(tail_out, c)])

    def scan_parts(d, c, m_in, fillers, out):
        p = {}

        def prepare():
            p.update(mlstm_prepare(d, c, m_in))
            out[d] = p["m_new"]

        parts = [prepare]
        for h in range(N_HEADS):
            parts.append(functools.partial(mlstm_scores, d, p, h))
            if h < len(fillers):
                parts.append(fillers[h])
            parts.append(functools.partial(mlstm_head, d, p, h))
        return parts + list(fillers[N_HEADS:])

    if n_tiles == 1 and n_chunks == 1:
        zeros = jnp.zeros((1, N_HEADS), F32)
        m_out = {}
        parts = [functools.partial(part, 0) for part in a_parts] + pool_parts
        parts += scan_parts(0, 0, zeros, [functools.partial(merge_ab, 0, nj) for nj in range(n_merge)], m_out)
        parts += scan_parts(1, 0, zeros, [functools.partial(gate_c, 0, nj) for nj in range(n_merge)], m_out)
        parts += tail_parts(0)

        def write_m():
            mout_ref[0:1, :] = m_out[0]
            mout_ref[1:2, :] = m_out[1]

        return parts + [write_m]

    def run_loops():
        def phase_a(i, carry):
            for part in a_parts:
                part(i)
            return carry

        lax.fori_loop(0, n_tiles, phase_a, 0)
        for part in pool_parts:
            part()
        for d in range(2):
            for h in range(N_HEADS):
                n_rows = jnp.broadcast_to(n0_ref[d, h:h + 1, :], (HEAD_DIM, HEAD_DIM))
                cn_scr[d, h] = jnp.concatenate([c0_ref[d, h], n_rows.T], axis=1)
        m_init = m0_ref[...]

        def fwd_step(j, m_in):
            m_out = {}
            for part in scan_parts(0, j, m_in, [functools.partial(merge_ab, j, nj) for nj in range(n_merge)], m_out):
                part()
            return m_out[0]

        def bwd_step(j, m_in):
            c = n_chunks - 1 - j
            m_out = {}
            for part in scan_parts(1, c, m_in, [functools.partial(gate_c, c, nj) for nj in range(n_merge)], m_out):
                part()
            for part in tail_parts(c):
                part()
            return m_out[1]

        lax.fori_loop(0, n_chunks, fwd_step, m_init[0:1, :])
        lax.fori_loop(0, n_chunks, bwd_step, m_init[1:2, :])

    assert has_state, "multi-chunk sequences carry the cached state"
    return [run_loops]


def _mixer_call(l, xall, lay, prm, stream, states=None, state_outs=None):
    ctx = stream == "ctx"
    T = SEQ if ctx else DEC_SEQ
    assert T % TILE_M == 0 and TILE_M == MLSTM_L and MLSTM_L % MLSTM_ROW_BLOCK == 0
    n_b = BATCH if ctx else DEC_BATCH
    own_array = xall.shape[0] != N_TOK
    blk0 = 0 if ctx or own_array else N_TOK_CTX // T
    once = pl.Buffered(1)

    def wspec(shape):
        nd = len(shape)
        return pl.BlockSpec((None,) + shape, lambda b: (l,) + (0,) * nd, pipeline_mode=once)

    def lspec(shape):
        return pl.BlockSpec(shape, lambda b: (0,) * len(shape), pipeline_mode=once)

    x_spec = pl.BlockSpec((T, D_MODEL), lambda b: (blk0 + b, 0), pipeline_mode=None if ctx else once)
    in_specs = [
        x_spec,
        lspec((COND_ROWS, 6 * D_MODEL)),
        lspec((DEPTH, D_MODEL)),
        wspec((D_MODEL, N_MAIN)),
        wspec((D_BRANCH, D_MODEL)),
        wspec((D_MODEL, GATE_PAD)),
        lspec((D_MODEL, N_BRANCH * D_MODEL)),
        lspec((N_BRANCH * D_BRANCH, D_MODEL)),
        lspec((D_MODEL, D_MODEL)),
        wspec((N_GROUPS, GROUP, GROUP)),
        lspec((DEPTH, D_BRANCH)),
        lspec((DEPTH, D_BRANCH)),
        wspec((N_GROUPS, SGU_CHUNK, SGU_CHUNK)),
        wspec((SGU_CHUNK, D_BRANCH)),
        lspec((DEPTH, N_GATE)),
        lspec((DEPTH, D_BRANCH)),
    ]
    args = [xall, lay["mods"], prm["g_norm1"], prm["w_in16"], prm["w_kt"],
            prm["w_gate"], lay["w_brg"], lay["w_branch"], lay["w_out"],
            prm["w_pool"], prm["pool_scale"], prm["g_sgu"], prm["w_sgu"],
            prm["b_sgu_tile"], prm["b_gates"], prm["g_mlstm"]]
    out_shape = [jax.ShapeDtypeStruct(xall.shape, xall.dtype)]
    out_specs = [x_spec]
    aliases = {} if own_array else {0: 0}
    passthrough = []
    if ctx:
        per_seq = [(2, N_HEADS, HEAD_DIM, HEAD_DIM), (2, N_HEADS, HEAD_DIM), (2, N_HEADS)]
        out_shape += [jax.ShapeDtypeStruct((BATCH, DEPTH) + s, F32) for s in per_seq]
        if state_outs is None:
            out_specs += [pl.BlockSpec((None, DEPTH) + s, lambda b, nd=len(s): (b,) + (0,) * (nd + 1))
                          for s in per_seq]
        else:
            out_specs += [pl.BlockSpec((None, None) + s, lambda b, nd=len(s): (b, l) + (0,) * nd) for s in per_seq]
            passthrough += [(a, 1 + k) for k, a in enumerate(state_outs)]
    else:
        in_specs += [
            pl.BlockSpec((None, None, 2, N_HEADS, HEAD_DIM, HEAD_DIM), lambda b: (b, l, 0, 0, 0, 0)),
            pl.BlockSpec((None, None, 2, N_HEADS, HEAD_DIM), lambda b: (b, l, 0, 0, 0)),
            pl.BlockSpec((None, None, 2, N_HEADS), lambda b: (b, l, 0, 0)),
        ]
        args += list(states)
    for buf, out_idx in passthrough:
        aliases[len(args)] = out_idx
        in_specs.append(pl.BlockSpec(memory_space=pl.ANY))
        args.append(buf)

    scratch = [
        pltpu.VMEM((T, D_MODEL), BF16),
        pltpu.VMEM((T + 2 * POOL_PAD, D_BRANCH), F32),
        pltpu.VMEM((TILE_M, D_BRANCH), F32),
        pltpu.VMEM((TILE_M, D_BRANCH), BF16),
        pltpu.VMEM((T, D_BRANCH), BF16),
        pltpu.VMEM((D_BRANCH, T), BF16),
        pltpu.VMEM((T, 2 * D_BRANCH), BF16),
        pltpu.VMEM((T, D_BRANCH), F32),
        pltpu.VMEM((2 * N_HEADS, T), F32),
        pltpu.VMEM((T, GATE_PAD), F32),
        pltpu.VMEM((T, D_BRANCH), F32),
        pltpu.VMEM((T, D_BRANCH), BF16),
        pltpu.VMEM((T, D_BRANCH), BF16),
        pltpu.VMEM((TILE_M, D_BRANCH), BF16),
        pltpu.VMEM((T, D_MODEL), F32),
        pltpu.VMEM((TILE_M, D_MODEL), F32),
        pltpu.VMEM((TILE_M, D_MODEL), BF16),
    ]
    if not ctx:
        scratch += [pltpu.VMEM((2, N_HEADS, HEAD_DIM, 2 * HEAD_DIM), F32)]
        scratch += [pltpu.VMEM((T, D_BRANCH), F32), pltpu.VMEM((T, GROUP), F32)]

    outs = pl.pallas_call(
        functools.partial(_mixer_kernel, T=T, grid2d=not ctx, has_state=not ctx, layer=l,
                          n_in=len(in_specs), n_pass=len(passthrough), n_out=len(out_specs)),
        grid=(n_b,),
        in_specs=in_specs,
        out_specs=out_specs,
        out_shape=out_shape,
        scratch_shapes=scratch,
        input_output_aliases=aliases,
        compiler_params=pltpu.CompilerParams(
            dimension_semantics=("arbitrary",),
            vmem_limit_bytes=_vmem_request(list(zip(args, in_specs)) + list(zip(out_shape, out_specs)), scratch)),
        name="mixer_ctx" if ctx else "mixer_lat",
    )(*args)
    return outs


def _ffn_kernel(*refs, layer, final, split_input):
    n_x = 2 if split_input else 1
    x_refs, (mod_ref, g2_ref, w1_hbm, w2_hbm, gfin_ref), rest = refs[:n_x], refs[n_x:n_x + 5], refs[n_x + 5:]
    if final:
        out_refs, rest = rest[:2], rest[2:]
    else:
        (cond_ref, wada_ref, bada_ref, wbr_ref, wo_ref, win_ref), rest = rest[:6], rest[6:]
        out_refs, (mods_out, wbr_out, wo_out, wbrg_out), rest = rest[:1], rest[1:5], rest[5:]
    acc_scr, h_scr, w1_scr, w2_scr, st1_scr, st2_scr, sem = rest[:7]
    n_chunks = D_FF // FFN_CHUNK
    tile = pl.program_id(0)
    first_tile = tile == 0
    ctx_tiles, lat_tiles = N_TOK_CTX // FFN_TILE_M, DEC_SEQ // FFN_TILE_M
    if split_input:
        x_ref = rest[7]

        @pl.when(tile < ctx_tiles)
        def _():
            x_ref[...] = x_refs[0][...]

        @pl.when(tile >= ctx_tiles)
        def _():
            x_ref[...] = x_refs[1][...]
    else:
        x_ref = x_refs[0]
    mod_ref = mod_ref.at[pl.ds(jnp.where(tile < ctx_tiles, 0, 1 + (tile - ctx_tiles) // lat_tiles), 1), :]
    g2_ref = g2_ref.at[pl.ds(layer, 1), :]

    def w1_copy(c):
        slot = c % FFN_STAGES
        c0 = _aligned(c * FFN_CHUNK, FFN_CHUNK)
        return pltpu.make_async_copy(w1_hbm.at[layer, :, pl.ds(c0, FFN_CHUNK)], st1_scr.at[slot], sem.at[0, slot])

    def w2_copy(c):
        slot = c % FFN_STAGES
        c0 = _aligned(c * FFN_CHUNK, FFN_CHUNK)
        return pltpu.make_async_copy(w2_hbm.at[layer, pl.ds(c0, FFN_CHUNK), :], st2_scr.at[slot], sem.at[1, slot])

    def land_w1(c):
        w1_copy(c).wait()
        c0 = _aligned(c * FFN_CHUNK, FFN_CHUNK)
        w1_scr[:, pl.ds(c0, FFN_CHUNK)] = st1_scr[c % FFN_STAGES].astype(BF16)

    def land_w2(c):
        w2_copy(c).wait()
        c0 = _aligned(c * FFN_CHUNK, FFN_CHUNK)
        w2_scr[pl.ds(c0, FFN_CHUNK), :] = st2_scr[c % FFN_STAGES].astype(BF16)

    def prepare_next_layer():
        if not final:
            _ada_block(cond_ref, wada_ref, bada_ref, mods_out, layer=layer + 1)
            wbr_out[...] = wbr_ref[...].astype(BF16)
            wo_out[...] = wo_ref[...].astype(BF16)
            n_br = N_BRANCH * D_MODEL
            for c0 in range(0, n_br, FFN_CHUNK):
                width = min(FFN_CHUNK + GATE_PAD, n_br + N_GATE - c0)
                window = win_ref[:, N_MAIN + c0:N_MAIN + c0 + width]
                wbrg_out[:, c0:c0 + FFN_CHUNK] = window[:, N_GATE:N_GATE + FFN_CHUNK]

    def normed_input():
        mod = mod_ref[...]
        sh2 = mod[:, 3 * D_MODEL:4 * D_MODEL]
        sc2 = mod[:, 4 * D_MODEL:5 * D_MODEL]
        x = x_ref[...]
        ms = jnp.mean(x * x, axis=-1, keepdims=True)
        return (x * lax.rsqrt(ms + EPS) * g2_ref[...] * (1.0 + sc2) + sh2).astype(BF16)

    def hidden(ub, c):
        c0 = _aligned(c * FFN_CHUNK, FFN_CHUNK)
        h = _dot(ub, w1_scr[:, pl.ds(c0, FFN_CHUNK)])
        h_scr[:, pl.ds(c0, FFN_CHUNK)] = jnp.square(jnp.maximum(h, 0.0)).astype(BF16)

    def finish(ffn_out):
        gt2 = mod_ref[...][:, 5 * D_MODEL:6 * D_MODEL]
        y = x_ref[...] + gt2 * ffn_out
        if not final:
            out_refs[0][...] = y
            return
        y = y * lax.rsqrt(jnp.mean(y * y, axis=-1, keepdims=True) + EPS) * gfin_ref[...]
        in_ctx = tile < ctx_tiles

        @pl.when(in_ctx)
        def _():
            out_refs[0][...] = y

        @pl.when(jnp.logical_not(in_ctx))
        def _():
            out_refs[1][...] = y

    @pl.when(first_tile)
    def _():
        for c in range(FFN_STAGES):
            w1_copy(c).start()
            w2_copy(c).start()
        prepare_next_layer()
        ub = normed_input()
        acc_scr[...] = jnp.zeros_like(acc_scr)

        def chunk(c, carry):
            land_w1(c)
            land_w2(c)

            @pl.when(c + FFN_STAGES < n_chunks)
            def _():
                w1_copy(c + FFN_STAGES).start()
                w2_copy(c + FFN_STAGES).start()

            hidden(ub, c)
            c0 = _aligned(c * FFN_CHUNK, FFN_CHUNK)
            acc_scr[...] += _dot(h_scr[:, pl.ds(c0, FFN_CHUNK)], w2_scr[pl.ds(c0, FFN_CHUNK), :])
            return carry

        lax.fori_loop(0, n_chunks, chunk, 0)
        finish(acc_scr[...])

    @pl.when(jnp.logical_not(first_tile))
    def _():
        prepare_next_layer()
        ub = normed_input()
        for c in range(n_chunks):
            hidden(ub, c)
        finish(_dot(h_scr[...], w2_scr[...]))


def _ffn_call(l, xall, lay, prm, g_final, final):
    n_tiles = N_TOK // FFN_TILE_M
    ctx_tiles = N_TOK_CTX // FFN_TILE_M
    once = pl.Buffered(1)
    split_input = isinstance(xall, tuple)
    assert not (split_input and final)

    x_spec = pl.BlockSpec((FFN_TILE_M, D_MODEL), lambda t: (t, 0))
    ctx_spec = pl.BlockSpec((FFN_TILE_M, D_MODEL), lambda t: (jnp.minimum(t, ctx_tiles - 1), 0))
    lat_spec = pl.BlockSpec((FFN_TILE_M, D_MODEL), lambda t: (jnp.maximum(t - ctx_tiles, 0), 0))
    xs = list(xall) if split_input else [xall]
    in_specs = ([ctx_spec, lat_spec] if split_input else [x_spec]) + [
        pl.BlockSpec((COND_ROWS, 6 * D_MODEL), lambda t: (0, 0), pipeline_mode=once),
        pl.BlockSpec((DEPTH, D_MODEL), lambda t: (0, 0), pipeline_mode=once),
        pl.BlockSpec(memory_space=pl.ANY),
        pl.BlockSpec(memory_space=pl.ANY),
        pl.BlockSpec((1, D_MODEL), lambda t: (0, 0)),
    ]
    args = xs + [lay["mods"], prm["g_norm2"], prm["w_ff1"], prm["w_ff2"], g_final.reshape(1, D_MODEL)]
    if final:
        out_specs = [ctx_spec, lat_spec]
        out_shape = [jax.ShapeDtypeStruct((N_TOK_CTX, D_MODEL), F32),
                     jax.ShapeDtypeStruct((N_TOK - N_TOK_CTX, D_MODEL), F32)]
        aliases = {}
    else:
        ada_cols = 6 * D_MODEL // n_tiles
        br_rows = N_BRANCH * D_BRANCH // n_tiles
        wo_tiles = D_MODEL // br_rows
        assert ada_cols % LANES == 0 and br_rows % BF16_SUBLANES == 0 and wo_tiles <= n_tiles
        wo_map = lambda t: (jnp.minimum(t, wo_tiles - 1), 0)
        in_specs += [
            pl.BlockSpec((COND_ROWS, D_MODEL), lambda t: (0, 0)),
            pl.BlockSpec((None, D_MODEL, ada_cols), lambda t: (l + 1, 0, t)),
            pl.BlockSpec((DEPTH, ada_cols), lambda t: (0, t)),
            pl.BlockSpec((None, br_rows, D_MODEL), lambda t: (l + 1, t, 0)),
            pl.BlockSpec((None, br_rows, D_MODEL), lambda t: (l + 1,) + wo_map(t)),
            pl.BlockSpec((None, br_rows, prm["w_in16"].shape[-1]), lambda t: (l + 1,) + wo_map(t)),
        ]
        args += [prm["cond8"], prm["w_ada"], prm["b_ada"], prm["w_branch_rows"], prm["w_out"], prm["w_in16"]]
        out_specs = [
            x_spec,
            pl.BlockSpec((COND_ROWS, ada_cols), lambda t: (0, t)),
            pl.BlockSpec((br_rows, D_MODEL), lambda t: (t, 0)),
            pl.BlockSpec((br_rows, D_MODEL), wo_map),
            pl.BlockSpec((br_rows, N_BRANCH * D_MODEL), wo_map),
        ]
        out_shape = [jax.ShapeDtypeStruct((N_TOK, D_MODEL), F32),
                     jax.ShapeDtypeStruct((COND_ROWS, 6 * D_MODEL), F32),
                     jax.ShapeDtypeStruct((N_BRANCH * D_BRANCH, D_MODEL), BF16),
                     jax.ShapeDtypeStruct((D_MODEL, D_MODEL), BF16),
                     jax.ShapeDtypeStruct((D_MODEL, N_BRANCH * D_MODEL), BF16)]
        aliases = {} if split_input else {0: 0}
    sems = [pltpu.SemaphoreType.DMA((2, FFN_STAGES))]
    gather = [pltpu.VMEM((FFN_TILE_M, D_MODEL), F32)] if split_input else []
    scratch = [
        pltpu.VMEM((FFN_TILE_M, D_MODEL), F32),
        pltpu.VMEM((FFN_TILE_M, D_FF), BF16),
        pltpu.VMEM((D_MODEL, D_FF), BF16),
        pltpu.VMEM((D_FF, D_MODEL), BF16),
        pltpu.VMEM((FFN_STAGES, D_MODEL, FFN_CHUNK), F32),
        pltpu.VMEM((FFN_STAGES, FFN_CHUNK, D_MODEL), F32),
    ]
    return pl.pallas_call(
        functools.partial(_ffn_kernel, layer=l, final=final, split_input=split_input),
        grid=(n_tiles,),
        in_specs=in_specs,
        out_specs=out_specs,
        out_shape=out_shape,
        scratch_shapes=scratch + sems + gather,
        input_output_aliases=aliases,
        compiler_params=pltpu.CompilerParams(
            dimension_semantics=("arbitrary",),
            vmem_limit_bytes=_vmem_request(list(zip(args, in_specs)) + list(zip(out_shape, out_specs)),
                                           scratch + gather)),
        name="ffn_final" if final else "ffn",
    )(*args)


def kernel(x_prompt, x_sample, state_C, state_n, state_m, c, c_ctx, w_ada, b_ada, g_norm1, g_norm2, w_in,
           b_gates, w_pool, pool_scale, g_sgu, w_sgu, b_sgu, g_mlstm, w_branch, w_out, w_ff1, w_ff2, g_final):
    w_in16 = w_in.astype(BF16)
    prm = {
        "w_in16": w_in16,
        "w_gate": jnp.pad(w_in16[:, :, N_MAIN:N_MAIN + N_GATE], ((0, 0), (0, 0), (0, GATE_PAD - N_GATE))),
        "w_kt": jnp.swapaxes(w_in16[:, :, 4 * D_BRANCH:5 * D_BRANCH], 1, 2),
        "w_branch_rows": w_branch.reshape(DEPTH, N_BRANCH * D_BRANCH, D_MODEL),
        "w_out": w_out,
        "w_ada": w_ada,
        "b_ada": b_ada,
        "w_pool": w_pool.astype(BF16),
        "w_sgu": w_sgu.astype(BF16),
        "w_ff1": w_ff1,
        "w_ff2": w_ff2,
        "g_norm1": g_norm1,
        "g_norm2": g_norm2,
        "pool_scale": pool_scale,
        "g_sgu": g_sgu,
        "g_mlstm": g_mlstm,
        "b_gates": b_gates,
        "b_sgu_tile": jnp.repeat(jnp.swapaxes(b_sgu, 1, 2), GROUP, axis=2),
        "cond8": jnp.concatenate(
            [c_ctx[None, :], c, jnp.zeros((COND_ROWS - 1 - DEC_BATCH, D_MODEL), F32)], axis=0),
    }
    lay = {
        "mods": _ada_call(prm["cond8"], w_ada, b_ada),
        "w_branch": prm["w_branch_rows"][0].astype(BF16),
        "w_out": w_out[0].astype(BF16),
        "w_brg": w_in16[0, :, N_MAIN + N_GATE:],
    }

    xc, *new_states = _mixer_call(0, x_prompt.reshape(N_TOK_CTX, D_MODEL), lay, prm, "ctx")
    (xs,) = _mixer_call(0, x_sample.reshape(-1, D_MODEL), lay, prm, "lat", states=(state_C, state_n, state_m))
    xall = (xc, xs)
    for l in range(DEPTH):
        if l > 0:
            xall, *new_states = _mixer_call(l, xall, lay, prm, "ctx", state_outs=new_states)
            (xall,) = _mixer_call(l, xall, lay, prm, "lat", states=(state_C, state_n, state_m))
        if l < DEPTH - 1:
            xall, *nxt = _ffn_call(l, xall, lay, prm, g_final, final=False)
            lay = dict(zip(("mods", "w_branch", "w_out", "w_brg"), nxt))
    y_prompt, y_sample = _ffn_call(DEPTH - 1, xall, lay, prm, g_final, final=True)
    return (y_prompt.reshape(BATCH, SEQ, D_MODEL), y_sample.reshape(DEC_BATCH, DEC_SEQ, D_MODEL), *new_states)
```

```python
import functools
import math

import jax
import jax.numpy as jnp
from jax import lax
from jax.experimental import pallas as pl
from jax.experimental.pallas import tpu as pltpu

F32 = jnp.float32
BF16 = jnp.bfloat16

D_MODEL = 1024
DEPTH = 4
BATCH = 16
SEQ = 256
DEC_BATCH = 2
DEC_SEQ = 1024
GRID_W = 64
D_BRANCH = 512
POOL_WINDOWS = (2, 4, 8, 16)
N_GROUPS = 4
GROUP = D_BRANCH // N_GROUPS
SGU_CHUNK = 128
N_HEADS = 4
HEAD_DIM = D_BRANCH // N_HEADS
N_BRANCH = 3
D_FF = 4 * D_MODEL
EPS = 1e-6

LANES = 128
BF16_SUBLANES = 16

N_MAIN = 7 * D_BRANCH
N_GATE = 4 * N_HEADS
GATE_PAD = LANES
N_TOK_CTX = BATCH * SEQ
N_TOK = N_TOK_CTX + DEC_BATCH * DEC_SEQ
COND_ROWS = 8

TILE_M = 256
MLSTM_L = 256
MLSTM_ROW_BLOCK = 128
MERGE_COLS = 256
POOL_PAD = 8
FFN_TILE_M = 512
FFN_CHUNK = 512
FFN_STAGES = 2
ADA_TILE_N = 768
V7X_VMEM_BYTES = 64 * 1024 * 1024
VMEM_COMPILER_RESERVE = 6 * 1024 * 1024
VMEM_CALL_FLOOR = 60 * 1024 * 1024


def _vmem_request(windows, scratch):
    total = VMEM_COMPILER_RESERVE
    for arr, spec in windows:
        if spec.block_shape is not None:
            n_buf = 2 if spec.pipeline_mode is None else spec.pipeline_mode.buffer_count
            total += n_buf * math.prod(d for d in spec.block_shape if d is not None) * jnp.dtype(arr.dtype).itemsize
    for buf in scratch:
        total += math.prod(buf.shape) * jnp.dtype(buf.dtype).itemsize
    return min(max(total, VMEM_CALL_FLOOR), V7X_VMEM_BYTES)


_dot = functools.partial(jnp.dot, preferred_element_type=F32)
_dot_nt = functools.partial(lax.dot_general, dimension_numbers=(((1,), (1,)), ((), ())),
                            preferred_element_type=F32)


def _aligned(offset, align):
    return offset if isinstance(offset, int) else pl.multiple_of(offset, align)


def _ada_block(cond_ref, w_ref, b_ref, out_ref, *, layer=0):
    cond = cond_ref[...]
    s = (cond * jax.nn.sigmoid(cond)).astype(BF16)
    out_ref[...] = _dot(s, w_ref[...].astype(BF16)) + b_ref[layer:layer + 1, :]


def _ada_call(cond8, w_ada, b_ada):
    n_out = 6 * D_MODEL
    args = (cond8, w_ada, b_ada)
    in_specs = [
        pl.BlockSpec((COND_ROWS, D_MODEL), lambda j: (0, 0)),
        pl.BlockSpec((None, D_MODEL, ADA_TILE_N), lambda j: (0, 0, j)),
        pl.BlockSpec((DEPTH, ADA_TILE_N), lambda j: (0, j)),
    ]
    out_spec = pl.BlockSpec((COND_ROWS, ADA_TILE_N), lambda j: (0, j))
    out_shape = jax.ShapeDtypeStruct((COND_ROWS, n_out), F32)
    return pl.pallas_call(
        _ada_block,
        grid=(n_out // ADA_TILE_N,),
        in_specs=in_specs,
        out_specs=out_spec,
        out_shape=out_shape,
        compiler_params=pltpu.CompilerParams(
            dimension_semantics=("arbitrary",),
            vmem_limit_bytes=_vmem_request(list(zip(args, in_specs)) + [(out_shape, out_spec)], [])),
        name="ada_mod",
    )(*args)


def _seg_scan(x, pos, seg, total, op, reverse):
    d = 1
    while d < seg:
        if reverse:
            shifted = pltpu.roll(x, total - d, 1)
            valid = pos < seg - d
        else:
            shifted = pltpu.roll(x, d, 1)
            valid = pos >= d
        if op == "sum":
            x = x + jnp.where(valid, shifted, 0.0)
        else:
            x = jnp.maximum(x, jnp.where(valid, shifted, -jnp.inf))
        d *= 2
    return x


_MIXER_LAYER_ROW_INPUTS = (2, 10, 11, 14, 15)


def _mixer_kernel(*refs, T, grid2d, has_state, layer, n_in, n_pass, n_out):
    ins, outs, scr = list(refs[:n_in - n_pass]), refs[n_in:n_in + n_out], refs[n_in + n_out:]
    ins[1] = ins[1].at[pl.ds(1 + pl.program_id(0) if has_state else 0, 1), :]
    for k in _MIXER_LAYER_ROW_INPUTS:
        ins[k] = ins[k].at[pl.ds(layer, 1), :]
    state_outs = list(outs[1:])
    if state_outs and n_pass == 0:
        for k, o in enumerate(state_outs):
            o[...] = jnp.zeros(o.shape, o.dtype)
            state_outs[k] = o.at[layer]
    for thunk in _mixer_sequence(*ins, outs[0], *state_outs, *scr, T=T, grid2d=grid2d, has_state=has_state):
        thunk()


def _mixer_sequence(*refs, T, grid2d, has_state):
    it = iter(refs)
    x_ref = next(it)
    mod_ref = next(it)
    g1_ref = next(it)
    wmain_ref = next(it)
    wkt_ref = next(it)
    wgate_ref = next(it)
    wbrg_ref = next(it)
    wbranch_ref = next(it)
    wout_ref = next(it)
    wpool_ref = next(it)
    pscale_ref = next(it)
    gsgu_ref = next(it)
    wsgu_ref = next(it)
    bsgu_ref = next(it)
    bgates_ref = next(it)
    gml_ref = next(it)
    if has_state:
        c0_ref, n0_ref, m0_ref = next(it), next(it), next(it)
    out_ref = next(it)
    if not has_state:
        cout_ref, nout_ref, mout_ref = next(it), next(it), next(it)
    (u_scr, xp_scr, su_scr, vn_scr, q_scr, kt_scr, vaug_scr, o_scr, grow_scr, colf_scr,
     h_scr, ya_scr, yb_scr, yc_scr, mab_scr, gc_scr, merged_scr) = [next(it) for _ in range(17)]
    if has_state:
        cn_scr = next(it)
    if grid2d:
        m1_scr, dlt_scr = next(it), next(it)

    L = MLSTM_L
    n_tiles = T // TILE_M
    n_chunks = T // L
    RB = MLSTM_ROW_BLOCK
    NW = MERGE_COLS
    n_merge = D_MODEL // NW

    def tile_rows(i):
        return pl.ds(_aligned(i * TILE_M, TILE_M), TILE_M)

    def a_norm(i):
        mod = mod_ref[...]
        sh1 = mod[:, 0:D_MODEL]
        sc1 = mod[:, D_MODEL:2 * D_MODEL]
        x = x_ref[tile_rows(i), :]
        ms = jnp.mean(x * x, axis=-1, keepdims=True)
        u = x * lax.rsqrt(ms + EPS) * g1_ref[...] * (1.0 + sc1) + sh1
        u_scr[tile_rows(i), :] = u.astype(BF16)

    def a_gates(i):
        lane = lax.broadcasted_iota(jnp.int32, (2 * N_HEADS, TILE_M), 1)
        is_fwd = lax.broadcasted_iota(jnp.int32, (2 * N_HEADS, TILE_M), 0) < N_HEADS
        bias = jnp.concatenate([bgates_ref[...], jnp.zeros((1, GATE_PAD - N_GATE), F32)], axis=1)
        gate_t = (_dot(u_scr[tile_rows(i), :], wgate_ref[...]) + bias).T
        gp = gate_t[0:N_GATE, :]
        i_all = gp[0:2 * N_HEADS, :]
        f_all = gp[2 * N_HEADS:4 * N_HEADS, :]
        logf = jnp.minimum(f_all, 0.0) - jnp.log1p(jnp.exp(-jnp.abs(f_all)))
        b_row = jnp.where(is_fwd, _seg_scan(logf, lane, L, L, "sum", False),
                          _seg_scan(logf, lane, L, L, "sum", True))
        g_row = i_all - b_row
        cm_row = jnp.where(is_fwd, _seg_scan(g_row, lane, L, L, "max", False),
                           _seg_scan(g_row, lane, L, L, "max", True))
        grow_scr[:, tile_rows(i)] = g_row
        stacked = jnp.concatenate(
            [b_row, cm_row, jnp.zeros((GATE_PAD - 4 * N_HEADS, TILE_M), F32)], axis=0)
        colf_scr[tile_rows(i), :] = stacked.T

    def proj(i, k):
        return _dot(u_scr[tile_rows(i), :], wmain_ref[:, k * D_BRANCH:(k + 1) * D_BRANCH])

    def a_xp(i):
        xp_scr[pl.ds(_aligned(i * TILE_M + POOL_PAD, 8), TILE_M), :] = proj(i, 0)

    def a_su(i):
        su_scr[...] = proj(i, 1)

    def a_sv(i):
        sv = proj(i, 2)
        vn = sv * lax.rsqrt(jnp.mean(sv * sv, axis=-1, keepdims=True) + EPS) * gsgu_ref[...]
        vn_scr[...] = vn.astype(BF16)

    def a_q(i):
        q_scr[tile_rows(i), :] = proj(i, 3).astype(BF16)

    def a_k(i):
        kt = _dot_nt(wkt_ref[...], u_scr[tile_rows(i), :])
        kt_scr[:, tile_rows(i)] = (kt * (HEAD_DIM ** -0.5)).astype(BF16)

    def a_v(i):
        v = proj(i, 5).astype(BF16)
        ones_blk = jnp.ones((TILE_M, HEAD_DIM), BF16)
        for h in range(N_HEADS):
            vaug_scr[tile_rows(i), 2 * h * HEAD_DIM:(2 * h + 1) * HEAD_DIM] = v[:, h * HEAD_DIM:(h + 1) * HEAD_DIM]
            vaug_scr[tile_rows(i), (2 * h + 1) * HEAD_DIM:(2 * h + 2) * HEAD_DIM] = ones_blk

    def a_o(i):
        o_scr[tile_rows(i), :] = proj(i, 6)

    def a_sgu(ch, i):
        crow = slice(ch * SGU_CHUNK, (ch + 1) * SGU_CHUNK)
        for g in range(N_GROUPS):
            gc = slice(g * GROUP, (g + 1) * GROUP)
            mixed = _dot(wsgu_ref[g], vn_scr[crow, gc]) + bsgu_ref[:, gc]
            yb = su_scr[crow, gc] * mixed
            yb_scr[pl.ds(_aligned(i * TILE_M + ch * SGU_CHUNK, SGU_CHUNK), SGU_CHUNK), gc] = yb.astype(BF16)

    a_parts = [a_norm, a_gates, a_xp, a_su, a_sv, a_q, a_k, a_v, a_o]
    a_parts += [functools.partial(a_sgu, ch) for ch in range(TILE_M // SGU_CHUNK)]

    def pool_pad():
        zpad = jnp.zeros((POOL_PAD, D_BRANCH), F32)
        xp_scr[0:POOL_PAD, :] = zpad
        xp_scr[T + POOL_PAD:T + 2 * POOL_PAD, :] = zpad

    def pool_seq(g):
        win = POOL_WINDOWS[g]
        gc = slice(g * GROUP, (g + 1) * GROUP)
        half = win // 2
        t_idx = lax.broadcasted_iota(jnp.int32, (T, GROUP), 0)
        total = jnp.zeros((T, GROUP), F32)
        for j in range(-half, win - half):
            total = total + xp_scr[POOL_PAD + j:POOL_PAD + j + T, gc]
        cnt = (jnp.minimum(t_idx + (win - half), T) - jnp.maximum(t_idx - half, 0)).astype(F32)
        dlt = total / cnt - xp_scr[POOL_PAD:POOL_PAD + T, gc]
        ya = _dot(dlt.astype(BF16), wpool_ref[g]) * pscale_ref[...][:, gc]
        ya_scr[:, gc] = ya.astype(BF16)

    def a_colpool(i):
        c_idx = lax.broadcasted_iota(jnp.int32, (GRID_W, GROUP), 0)
        for g, win in enumerate(POOL_WINDOWS):
            gc = slice(g * GROUP, (g + 1) * GROUP)
            half = win // 2
            inv_cnt = 1.0 / (jnp.minimum(c_idx + (win - half), GRID_W) - jnp.maximum(c_idx - half, 0)).astype(F32)
            for r in range(TILE_M // GRID_W):
                r0 = i * TILE_M + r * GRID_W
                xg = xp_scr[pl.ds(_aligned(r0 + POOL_PAD, 8), GRID_W), gc]
                total = xg
                for j in range(-half, win - half):
                    if j != 0:
                        valid = (c_idx + j >= 0) & (c_idx + j < GRID_W)
                        total = total + jnp.where(valid, pltpu.roll(xg, (-j) % GRID_W, 0), 0.0)
                m1_scr[pl.ds(_aligned(r0, GRID_W), GRID_W), gc] = total * inv_cnt

    def pool_grid_rows(g):
        win = POOL_WINDOWS[g]
        gc = slice(g * GROUP, (g + 1) * GROUP)
        half = win // 2
        n_rows = T // GRID_W
        for r in range(n_rows):
            lo = max(r - half, 0)
            hi = min(r + (win - half), n_rows)
            acc = m1_scr[lo * GRID_W:(lo + 1) * GRID_W, gc]
            for rr in range(lo + 1, hi):
                acc = acc + m1_scr[rr * GRID_W:(rr + 1) * GRID_W, gc]
            xg = xp_scr[POOL_PAD + r * GRID_W:POOL_PAD + (r + 1) * GRID_W, gc]
            dlt_scr[r * GRID_W:(r + 1) * GRID_W, :] = acc / float(hi - lo) - xg
        ya = _dot(dlt_scr[...].astype(BF16), wpool_ref[g]) * pscale_ref[...][:, gc]
        ya_scr[:, gc] = ya.astype(BF16)

    if grid2d:
        assert TILE_M % GRID_W == 0
        a_parts.insert(a_parts.index(a_xp) + 1, a_colpool)
        pool_parts = [functools.partial(pool_grid_rows, g) for g in range(N_GROUPS)]
    else:
        pool_parts = [pool_pad] + [functools.partial(pool_seq, g) for g in range(N_GROUPS)]

    def keep_mask(d, rb, col_lo, ncols):
        t_loc = rb * RB + lax.broadcasted_iota(jnp.int32, (RB, ncols), 0)
        s_loc = col_lo + lax.broadcasted_iota(jnp.int32, (RB, ncols), 1)
        return s_loc <= t_loc if d == 0 else s_loc >= t_loc

    def mlstm_prepare(d, c, m_in):
        c0 = _aligned(c * L, L)
        colf = colf_scr[pl.ds(c0, L), :]
        bcol = colf[:, N_HEADS * d:N_HEADS * (d + 1)]
        cmcol = colf[:, 2 * N_HEADS + N_HEADS * d:2 * N_HEADS + N_HEADS * (d + 1)]
        mx = jnp.maximum(m_in, cmcol)
        last = L - 1 if d == 0 else 0
        mx_last = mx[last:last + 1, :]
        return dict(c0=c0, mx=mx, mx_last=mx_last, w_inter=jnp.exp(m_in - mx), e_negm=jnp.exp(-(bcol + mx)),
                    decay=jnp.exp(m_in - mx_last), m_new=bcol[last:last + 1, :] + mx_last)

    def block_cols(d, rb):
        return (0, (rb + 1) * RB) if d == 0 else (rb * RB, L)

    def mlstm_scores(d, p, h):
        c0 = p["c0"]
        hc = slice(h * HEAD_DIM, (h + 1) * HEAD_DIM)
        qks = []
        for rb in range(L // RB):
            col_lo, col_hi = block_cols(d, rb)
            qb = q_scr[pl.ds(_aligned(c0 + rb * RB, RB), RB), hc]
            qks.append(_dot(qb, kt_scr[hc, pl.ds(_aligned(c0 + col_lo, RB), col_hi - col_lo)]))
        p["qk", h] = qks

    def mlstm_head(d, p, h):
        c0, mx, mx_last = p["c0"], p["mx"], p["mx_last"]
        rows = pl.ds(c0, L)
        hc = slice(h * HEAD_DIM, (h + 1) * HEAD_DIM)
        ac = slice(2 * h * HEAD_DIM, (2 * h + 2) * HEAD_DIM)
        grow = grow_scr[N_HEADS * d + h:N_HEADS * d + h + 1, rows]
        if has_state:
            cn_dh = cn_scr[d, h]
            cn_bf = cn_dh.astype(BF16)
        for rb in range(L // RB):
            col_lo, col_hi = block_cols(d, rb)
            ncols = col_hi - col_lo
            trow = slice(rb * RB, (rb + 1) * RB)
            rrows = pl.ds(_aligned(c0 + rb * RB, RB), RB)
            crows = pl.ds(_aligned(c0 + col_lo, RB), ncols)
            qb = q_scr[rrows, hc]
            qk = p["qk", h][rb]
            w = jnp.exp(grow[:, col_lo:col_hi] - mx[trow, h:h + 1])
            s = (qk * jnp.where(keep_mask(d, rb, col_lo, ncols), w, 0.0)).astype(BF16)
            na = _dot(s, vaug_scr[crows, ac])
            if has_state:
                na = na + p["w_inter"][trow, h:h + 1] * _dot(qb, cn_bf)
            hval = na[:, 0:HEAD_DIM] / jnp.maximum(jnp.abs(na[:, HEAD_DIM:]), p["e_negm"][trow, h:h + 1])
            if d == 0:
                h_scr[rrows, hc] = hval
            else:
                h_scr[rrows, hc] += hval
        w_state = jnp.exp(grow - mx_last[:, h:h + 1])
        kw_t = (kt_scr[hc, rows].astype(F32) * w_state).astype(BF16)
        upd = _dot(kw_t, vaug_scr[rows, ac])
        if has_state:
            cn_scr[d, h] = p["decay"][:, h:h + 1] * cn_dh + upd
        else:
            cout_ref[d, h] = upd[:, 0:HEAD_DIM]
            nout_ref[d, h:h + 1, :] = upd[:, HEAD_DIM:].T[0:1, :]

    def merge_ab(c, nj):
        nc = slice(nj * NW, (nj + 1) * NW)
        ub = u_scr[tile_rows(c), :]
        acc = None
        for r, y_scr in enumerate((ya_scr, yb_scr)):
            gate = jax.nn.sigmoid(_dot(ub, wbrg_ref[:, r * D_MODEL + nj * NW:r * D_MODEL + (nj + 1) * NW]))
            term = gate * _dot(y_scr[tile_rows(c), :], wbranch_ref[r * D_BRANCH:(r + 1) * D_BRANCH, nc])
            acc = term if acc is None else acc + term
        mab_scr[tile_rows(c), nc] = acc

    def gate_c(c, nj):
        w = wbrg_ref[:, 2 * D_MODEL + nj * NW:2 * D_MODEL + (nj + 1) * NW]
        gc_scr[:, nj * NW:(nj + 1) * NW] = jax.nn.sigmoid(_dot(u_scr[tile_rows(c), :], w))

    def tail_yc(c):
        gml = gml_ref[...]
        for h in range(N_HEADS):
            hc = slice(h * HEAD_DIM, (h + 1) * HEAD_DIM)
            hh = h_scr[tile_rows(c), hc]
            hn = hh * lax.rsqrt(jnp.mean(hh * hh, axis=-1, keepdims=True) + EPS) * gml[:, hc]
            yc_scr[:, hc] = (jax.nn.sigmoid(o_scr[tile_rows(c), hc]) * hn).astype(BF16)

    def tail_merge(c, nj):
        nc = slice(nj * NW, (nj + 1) * NW)
        acc = mab_scr[tile_rows(c), nc] + gc_scr[:, nc] * _dot(yc_scr[...], wbranch_ref[2 * D_BRANCH:, nc])
        merged_scr[:, nc] = acc.astype(BF16)

    def tail_out(c):
        gt1 = mod_ref[...][:, 2 * D_MODEL:3 * D_MODEL]
        out_ref[tile_rows(c), :] = x_ref[tile_rows(c), :] + gt1 * _dot(merged_scr[...], wout_ref[...])

    def tail_parts(c):
        return ([functools.partial(tail_yc, c)] + [functools.partial(tail_merge, c, nj) for nj in range(n_merge)]
                + [functools.partial(tail_out, c)])

    def scan_parts(d, c, m_in, fillers, out):
        p = {}

        def prepare():
            p.update(mlstm_prepare(d, c, m_in))
            out[d] = p["m_new"]

        parts = [prepare]
        for h in range(N_HEADS):
            parts.append(functools.partial(mlstm_scores, d, p, h))
            if h < len(fillers):
                parts.append(fillers[h])
            parts.append(functools.partial(mlstm_head, d, p, h))
        return parts + list(fillers[N_HEADS:])

    if n_tiles == 1 and n_chunks == 1:
        zeros = jnp.zeros((1, N_HEADS), F32)
        m_out = {}
        parts = [functools.partial(part, 0) for part in a_parts] + pool_parts
        parts += scan_parts(0, 0, zeros, [functools.partial(merge_ab, 0, nj) for nj in range(n_merge)], m_out)
        parts += scan_parts(1, 0, zeros, [functools.partial(gate_c, 0, nj) for nj in range(n_merge)], m_out)
        parts += tail_parts(0)

        def write_m():
            mout_ref[0:1, :] = m_out[0]
            mout_ref[1:2, :] = m_out[1]

        return parts + [write_m]

    def run_loops():
        def phase_a(i, carry):
            for part in a_parts:
                part(i)
            return carry

        lax.fori_loop(0, n_tiles, phase_a, 0)
        for part in pool_parts:
            part()
        for d in range(2):
            for h in range(N_HEADS):
                n_rows = jnp.broadcast_to(n0_ref[d, h:h + 1, :], (HEAD_DIM, HEAD_DIM))
                cn_scr[d, h] = jnp.concatenate([c0_ref[d, h], n_rows.T], axis=1)
        m_init = m0_ref[...]

        def fwd_step(j, m_in):
            m_out = {}
            for part in scan_parts(0, j, m_in, [functools.partial(merge_ab, j, nj) for nj in range(n_merge)], m_out):
                part()
            return m_out[0]

        def bwd_step(j, m_in):
            c = n_chunks - 1 - j
            m_out = {}
            for part in scan_parts(1, c, m_in, [functools.partial(gate_c, c, nj) for nj in range(n_merge)], m_out):
                part()
            for part in tail_parts(c):
                part()
            return m_out[1]

        lax.fori_loop(0, n_chunks, fwd_step, m_init[0:1, :])
        lax.fori_loop(0, n_chunks, bwd_step, m_init[1:2, :])

    assert has_state, "multi-chunk sequences carry the cached state"
    return [run_loops]


def _mixer_call(l, xall, lay, prm, stream, states=None, state_outs=None):
    ctx = stream == "ctx"
    T = SEQ if ctx else DEC_SEQ
    assert T % TILE_M == 0 and TILE_M == MLSTM_L and MLSTM_L % MLSTM_ROW_BLOCK == 0
    n_b = BATCH if ctx else DEC_BATCH
    own_array = xall.shape[0] != N_TOK
    blk0 = 0 if ctx or own_array else N_TOK_CTX // T
    once = pl.Buffered(1)

    def wspec(shape):
        nd = len(shape)
        return pl.BlockSpec((None,) + shape, lambda b: (l,) + (0,) * nd, pipeline_mode=once)

    def lspec(shape):
        return pl.BlockSpec(shape, lambda b: (0,) * len(shape), pipeline_mode=once)

    x_spec = pl.BlockSpec((T, D_MODEL), lambda b: (blk0 + b, 0))
    in_specs = [
        x_spec,
        lspec((COND_ROWS, 6 * D_MODEL)),
        lspec((DEPTH, D_MODEL)),
        wspec((D_MODEL, N_MAIN)),
        wspec((D_BRANCH, D_MODEL)),
        wspec((D_MODEL, GATE_PAD)),
        lspec((D_MODEL, N_BRANCH * D_MODEL)),
        lspec((N_BRANCH * D_BRANCH, D_MODEL)),
        lspec((D_MODEL, D_MODEL)),
        wspec((N_GROUPS, GROUP, GROUP)),
        lspec((DEPTH, D_BRANCH)),
        lspec((DEPTH, D_BRANCH)),
        wspec((N_GROUPS, SGU_CHUNK, SGU_CHUNK)),
        wspec((SGU_CHUNK, D_BRANCH)),
        lspec((DEPTH, N_GATE)),
        lspec((DEPTH, D_BRANCH)),
    ]
    args = [xall, lay["mods"], prm["g_norm1"], prm["w_in16"], prm["w_kt"],
            prm["w_gate"], lay["w_brg"], lay["w_branch"], lay["w_out"],
            prm["w_pool"], prm["pool_scale"], prm["g_sgu"], prm["w_sgu"],
            prm["b_sgu_tile"], prm["b_gates"], prm["g_mlstm"]]
    out_shape = [jax.ShapeDtypeStruct(xall.shape, xall.dtype)]
    out_specs = [pl.BlockSpec((T, D_MODEL), lambda b: (blk0 + b, 0), pipeline_mode=None if ctx else once)]
    aliases = {} if own_array else {0: 0}
    passthrough = []
    if ctx:
        per_seq = [(2, N_HEADS, HEAD_DIM, HEAD_DIM), (2, N_HEADS, HEAD_DIM), (2, N_HEADS)]
        out_shape += [jax.ShapeDtypeStruct((BATCH, DEPTH) + s, F32) for s in per_seq]
        if state_outs is None:
            out_specs += [pl.BlockSpec((None, DEPTH) + s, lambda b, nd=len(s): (b,) + (0,) * (nd + 1))
                          for s in per_seq]
        else:
            out_specs += [pl.BlockSpec((None, None) + s, lambda b, nd=len(s): (b, l) + (0,) * nd) for s in per_seq]
            passthrough += [(a, 1 + k) for k, a in enumerate(state_outs)]
    else:
        in_specs += [
            pl.BlockSpec((None, None, 2, N_HEADS, HEAD_DIM, HEAD_DIM), lambda b: (b, l, 0, 0, 0, 0)),
            pl.BlockSpec((None, None, 2, N_HEADS, HEAD_DIM), lambda b: (b, l, 0, 0, 0)),
            pl.BlockSpec((None, None, 2, N_HEADS), lambda b: (b, l, 0, 0)),
        ]
        args += list(states)
    for buf, out_idx in passthrough:
        aliases[len(args)] = out_idx
        in_specs.append(pl.BlockSpec(memory_space=pl.ANY))
        args.append(buf)

    scratch = [
        pltpu.VMEM((T, D_MODEL), BF16),
        pltpu.VMEM((T + 2 * POOL_PAD, D_BRANCH), F32),
        pltpu.VMEM((TILE_M, D_BRANCH), F32),
        pltpu.VMEM((TILE_M, D_BRANCH), BF16),
        pltpu.VMEM((T, D_BRANCH), BF16),
        pltpu.VMEM((D_BRANCH, T), BF16),
        pltpu.VMEM((T, 2 * D_BRANCH), BF16),
        pltpu.VMEM((T, D_BRANCH), F32),
        pltpu.VMEM((2 * N_HEADS, T), F32),
        pltpu.VMEM((T, GATE_PAD), F32),
        pltpu.VMEM((T, D_BRANCH), F32),
        pltpu.VMEM((T, D_BRANCH), BF16),
        pltpu.VMEM((T, D_BRANCH), BF16),
        pltpu.VMEM((TILE_M, D_BRANCH), BF16),
        pltpu.VMEM((T, D_MODEL), F32),
        pltpu.VMEM((TILE_M, D_MODEL), F32),
        pltpu.VMEM((TILE_M, D_MODEL), BF16),
    ]
    if not ctx:
        scratch += [pltpu.VMEM((2, N_HEADS, HEAD_DIM, 2 * HEAD_DIM), F32)]
        scratch += [pltpu.VMEM((T, D_BRANCH), F32), pltpu.VMEM((T, GROUP), F32)]

    outs = pl.pallas_call(
        functools.partial(_mixer_kernel, T=T, grid2d=not ctx, has_state=not ctx, layer=l,
                          n_in=len(in_specs), n_pass=len(passthrough), n_out=len(out_specs)),
        grid=(n_b,),
        in_specs=in_specs,
        out_specs=out_specs,
        out_shape=out_shape,
        scratch_shapes=scratch,
        input_output_aliases=aliases,
        compiler_params=pltpu.CompilerParams(
            dimension_semantics=("arbitrary",),
            vmem_limit_bytes=_vmem_request(list(zip(args, in_specs)) + list(zip(out_shape, out_specs)), scratch)),
        name="mixer_ctx" if ctx else "mixer_lat",
    )(*args)
    return outs


def _ffn_kernel(*refs, layer, final, split_input):
    n_x = 2 if split_input else 1
    x_refs, (mod_ref, g2_ref, w1_hbm, w2_hbm, gfin_ref), rest = refs[:n_x], refs[n_x:n_x + 5], refs[n_x + 5:]
    if final:
        out_refs, rest = rest[:2], rest[2:]
    else:
        (cond_ref, wada_ref, bada_ref, wbr_ref, wo_ref, win_ref), rest = rest[:6], rest[6:]
        out_refs, (mods_out, wbr_out, wo_out, wbrg_out), rest = rest[:1], rest[1:5], rest[5:]
    acc_scr, h_scr, w1_scr, w2_scr, st1_scr, st2_scr, sem = rest[:7]
    n_chunks = D_FF // FFN_CHUNK
    tile = pl.program_id(0)
    first_tile = tile == 0
    ctx_tiles, lat_tiles = N_TOK_CTX // FFN_TILE_M, DEC_SEQ // FFN_TILE_M
    if split_input:
        x_ref = rest[7]

        @pl.when(tile < ctx_tiles)
        def _():
            x_ref[...] = x_refs[0][...]

        @pl.when(tile >= ctx_tiles)
        def _():
            x_ref[...] = x_refs[1][...]
    else:
        x_ref = x_refs[0]
    mod_ref = mod_ref.at[pl.ds(jnp.where(tile < ctx_tiles, 0, 1 + (tile - ctx_tiles) // lat_tiles), 1), :]
    g2_ref = g2_ref.at[pl.ds(layer, 1), :]

    def w1_copy(c):
        slot = c % FFN_STAGES
        c0 = _aligned(c * FFN_CHUNK, FFN_CHUNK)
        return pltpu.make_async_copy(w1_hbm.at[layer, :, pl.ds(c0, FFN_CHUNK)], st1_scr.at[slot], sem.at[0, slot])

    def w2_copy(c):
        slot = c % FFN_STAGES
        c0 = _aligned(c * FFN_CHUNK, FFN_CHUNK)
        return pltpu.make_async_copy(w2_hbm.at[layer, pl.ds(c0, FFN_CHUNK), :], st2_scr.at[slot], sem.at[1, slot])

    def land_w1(c):
        w1_copy(c).wait()
        c0 = _aligned(c * FFN_CHUNK, FFN_CHUNK)
        w1_scr[:, pl.ds(c0, FFN_CHUNK)] = st1_scr[c % FFN_STAGES].astype(BF16)

    def land_w2(c):
        w2_copy(c).wait()
        c0 = _aligned(c * FFN_CHUNK, FFN_CHUNK)
        w2_scr[pl.ds(c0, FFN_CHUNK), :] = st2_scr[c % FFN_STAGES].astype(BF16)

    def prepare_next_layer():
        if not final:
            _ada_block(cond_ref, wada_ref, bada_ref, mods_out, layer=layer + 1)
            wbr_out[...] = wbr_ref[...].astype(BF16)
            wo_out[...] = wo_ref[...].astype(BF16)
            n_br = N_BRANCH * D_MODEL
            for c0 in range(0, n_br, FFN_CHUNK):
                width = min(FFN_CHUNK + GATE_PAD, n_br + N_GATE - c0)
                window = win_ref[:, N_MAIN + c0:N_MAIN + c0 + width]
                wbrg_out[:, c0:c0 + FFN_CHUNK] = window[:, N_GATE:N_GATE + FFN_CHUNK]

    def normed_input():
        mod = mod_ref[...]
        sh2 = mod[:, 3 * D_MODEL:4 * D_MODEL]
        sc2 = mod[:, 4 * D_MODEL:5 * D_MODEL]
        x = x_ref[...]
        ms = jnp.mean(x * x, axis=-1, keepdims=True)
        return (x * lax.rsqrt(ms + EPS) * g2_ref[...] * (1.0 + sc2) + sh2).astype(BF16)

    def hidden(ub, c):
        c0 = _aligned(c * FFN_CHUNK, FFN_CHUNK)
        h = _dot(ub, w1_scr[:, pl.ds(c0, FFN_CHUNK)])
        h_scr[:, pl.ds(c0, FFN_CHUNK)] = jnp.square(jnp.maximum(h, 0.0)).astype(BF16)

    def finish(ffn_out):
        gt2 = mod_ref[...][:, 5 * D_MODEL:6 * D_MODEL]
        y = x_ref[...] + gt2 * ffn_out
        if not final:
            out_refs[0][...] = y
            return
        y = y * lax.rsqrt(jnp.mean(y * y, axis=-1, keepdims=True) + EPS) * gfin_ref[...]
        in_ctx = tile < ctx_tiles

        @pl.when(in_ctx)
        def _():
            out_refs[0][...] = y

        @pl.when(jnp.logical_not(in_ctx))
        def _():
            out_refs[1][...] = y

    @pl.when(first_tile)
    def _():
        for c in range(FFN_STAGES):
            w1_copy(c).start()
            w2_copy(c).start()
        prepare_next_layer()
        ub = normed_input()
        acc_scr[...] = jnp.zeros_like(acc_scr)

        def chunk(c, carry):
            land_w1(c)
            land_w2(c)

            @pl.when(c + FFN_STAGES < n_chunks)
            def _():
                w1_copy(c + FFN_STAGES).start()
                w2_copy(c + FFN_STAGES).start()

            hidden(ub, c)
            c0 = _aligned(c * FFN_CHUNK, FFN_CHUNK)
            acc_scr[...] += _dot(h_scr[:, pl.ds(c0, FFN_CHUNK)], w2_scr[pl.ds(c0, FFN_CHUNK), :])
            return carry

        lax.fori_loop(0, n_chunks, chunk, 0)
        finish(acc_scr[...])

    @pl.when(jnp.logical_not(first_tile))
    def _():
        prepare_next_layer()
        ub = normed_input()
        for c in range(n_chunks):
            hidden(ub, c)
        finish(_dot(h_scr[...], w2_scr[...]))


def _ffn_call(l, xall, lay, prm, g_final, final):
    n_tiles = N_TOK // FFN_TILE_M
    ctx_tiles = N_TOK_CTX // FFN_TILE_M
    once = pl.Buffered(1)
    split_input = isinstance(xall, tuple)
    assert not (split_input and final)

    x_spec = pl.BlockSpec((FFN_TILE_M, D_MODEL), lambda t: (t, 0))
    ctx_spec = pl.BlockSpec((FFN_TILE_M, D_MODEL), lambda t: (jnp.minimum(t, ctx_tiles - 1), 0))
    lat_spec = pl.BlockSpec((FFN_TILE_M, D_MODEL), lambda t: (jnp.maximum(t - ctx_tiles, 0), 0))
    xs = list(xall) if split_input else [xall]
    in_specs = ([ctx_spec, lat_spec] if split_input else [x_spec]) + [
        pl.BlockSpec((COND_ROWS, 6 * D_MODEL), lambda t: (0, 0), pipeline_mode=once),
        pl.BlockSpec((DEPTH, D_MODEL), lambda t: (0, 0), pipeline_mode=once),
        pl.BlockSpec(memory_space=pl.ANY),
        pl.BlockSpec(memory_space=pl.ANY),
        pl.BlockSpec((1, D_MODEL), lambda t: (0, 0)),
    ]
    args = xs + [lay["mods"], prm["g_norm2"], prm["w_ff1"], prm["w_ff2"], g_final.reshape(1, D_MODEL)]
    if final:
        out_specs = [ctx_spec, lat_spec]
        out_shape = [jax.ShapeDtypeStruct((N_TOK_CTX, D_MODEL), F32),
                     jax.ShapeDtypeStruct((N_TOK - N_TOK_CTX, D_MODEL), F32)]
        aliases = {}
    else:
        ada_cols = 6 * D_MODEL // n_tiles
        br_rows = N_BRANCH * D_BRANCH // n_tiles
        wo_tiles = D_MODEL // br_rows
        assert ada_cols % LANES == 0 and br_rows % BF16_SUBLANES == 0 and wo_tiles <= n_tiles
        wo_map = lambda t: (jnp.minimum(t, wo_tiles - 1), 0)
        in_specs += [
            pl.BlockSpec((COND_ROWS, D_MODEL), lambda t: (0, 0)),
            pl.BlockSpec((None, D_MODEL, ada_cols), lambda t: (l + 1, 0, t)),
            pl.BlockSpec((DEPTH, ada_cols), lambda t: (0, t)),
            pl.BlockSpec((None, br_rows, D_MODEL), lambda t: (l + 1, t, 0)),
            pl.BlockSpec((None, br_rows, D_MODEL), lambda t: (l + 1,) + wo_map(t)),
            pl.BlockSpec((None, br_rows, prm["w_in16"].shape[-1]), lambda t: (l + 1,) + wo_map(t)),
        ]
        args += [prm["cond8"], prm["w_ada"], prm["b_ada"], prm["w_branch_rows"], prm["w_out"], prm["w_in16"]]
        out_specs = [
            x_spec,
            pl.BlockSpec((COND_ROWS, ada_cols), lambda t: (0, t)),
            pl.BlockSpec((br_rows, D_MODEL), lambda t: (t, 0)),
            pl.BlockSpec((br_rows, D_MODEL), wo_map),
            pl.BlockSpec((br_rows, N_BRANCH * D_MODEL), wo_map),
        ]
        out_shape = [jax.ShapeDtypeStruct((N_TOK, D_MODEL), F32),
                     jax.ShapeDtypeStruct((COND_ROWS, 6 * D_MODEL), F32),
                     jax.ShapeDtypeStruct((N_BRANCH * D_BRANCH, D_MODEL), BF16),
                     jax.ShapeDtypeStruct((D_MODEL, D_MODEL), BF16),
                     jax.ShapeDtypeStruct((D_MODEL, N_BRANCH * D_MODEL), BF16)]
        aliases = {} if split_input else {0: 0}
    sems = [pltpu.SemaphoreType.DMA((2, FFN_STAGES))]
    gather = [pltpu.VMEM((FFN_TILE_M, D_MODEL), F32)] if split_input else []
    scratch = [
        pltpu.VMEM((FFN_TILE_M, D_MODEL), F32),
        pltpu.VMEM((FFN_TILE_M, D_FF), BF16),
        pltpu.VMEM((D_MODEL, D_FF), BF16),
        pltpu.VMEM((D_FF, D_MODEL), BF16),
        pltpu.VMEM((FFN_STAGES, D_MODEL, FFN_CHUNK), F32),
        pltpu.VMEM((FFN_STAGES, FFN_CHUNK, D_MODEL), F32),
    ]
    return pl.pallas_call(
        functools.partial(_ffn_kernel, layer=l, final=final, split_input=split_input),
        grid=(n_tiles,),
        in_specs=in_specs,
        out_specs=out_specs,
        out_shape=out_shape,
        scratch_shapes=scratch + sems + gather,
        input_output_aliases=aliases,
        compiler_params=pltpu.CompilerParams(
            dimension_semantics=("arbitrary",),
            vmem_limit_bytes=_vmem_request(list(zip(args, in_specs)) + list(zip(out_shape, out_specs)),
                                           scratch + gather)),
        name="ffn_final" if final else "ffn",
    )(*args)


def kernel(x_prompt, x_sample, state_C, state_n, state_m, c, c_ctx, w_ada, b_ada, g_norm1, g_norm2, w_in,
           b_gates, w_pool, pool_scale, g_sgu, w_sgu, b_sgu, g_mlstm, w_branch, w_out, w_ff1, w_ff2, g_final):
    w_in16 = w_in.astype(BF16)
    prm = {
        "w_in16": w_in16,
        "w_gate": jnp.pad(w_in16[:, :, N_MAIN:N_MAIN + N_GATE], ((0, 0), (0, 0), (0, GATE_PAD - N_GATE))),
        "w_kt": jnp.swapaxes(w_in16[:, :, 4 * D_BRANCH:5 * D_BRANCH], 1, 2),
        "w_branch_rows": w_branch.reshape(DEPTH, N_BRANCH * D_BRANCH, D_MODEL),
        "w_out": w_out,
        "w_ada": w_ada,
        "b_ada": b_ada,
        "w_pool": w_pool.astype(BF16),
        "w_sgu": w_sgu.astype(BF16),
        "w_ff1": w_ff1,
        "w_ff2": w_ff2,
        "g_norm1": g_norm1,
        "g_norm2": g_norm2,
        "pool_scale": pool_scale,
        "g_sgu": g_sgu,
        "g_mlstm": g_mlstm,
        "b_gates": b_gates,
        "b_sgu_tile": jnp.repeat(jnp.swapaxes(b_sgu, 1, 2), GROUP, axis=2),
        "cond8": jnp.concatenate(
            [c_ctx[None, :], c, jnp.zeros((COND_ROWS - 1 - DEC_BATCH, D_MODEL), F32)], axis=0),
    }
    lay = {
        "mods": _ada_call(prm["cond8"], w_ada, b_ada),
        "w_branch": prm["w_branch_rows"][0].astype(BF16),
        "w_out": w_out[0].astype(BF16),
        "w_brg": w_in16[0, :, N_MAIN + N_GATE:],
    }

    xc, *new_states = _mixer_call(0, x_prompt.reshape(N_TOK_CTX, D_MODEL), lay, prm, "ctx")
    (xs,) = _mixer_call(0, x_sample.reshape(-1, D_MODEL), lay, prm, "lat", states=(state_C, state_n, state_m))
    xall = (xc, xs)
    for l in range(DEPTH):
        if l > 0:
            xall, *new_states = _mixer_call(l, xall, lay, prm, "ctx", state_outs=new_states)
            (xall,) = _mixer_call(l, xall, lay, prm, "lat", states=(state_C, state_n, state_m))
        if l < DEPTH - 1:
            xall, *nxt = _ffn_call(l, xall, lay, prm, g_final, final=False)
            lay = dict(zip(("mods", "w_branch", "w_out", "w_brg"), nxt))
    y_prompt, y_sample = _ffn_call(DEPTH - 1, xall, lay, prm, g_final, final=True)
    return (y_prompt.reshape(BATCH, SEQ, D_MODEL), y_sample.reshape(DEC_BATCH, DEC_SEQ, D_MODEL), *new_states)
```

```python
import functools
import math

import jax
import jax.numpy as jnp
from jax import lax
from jax.experimental import pallas as pl
from jax.experimental.pallas import tpu as pltpu

F32 = jnp.float32
BF16 = jnp.bfloat16

D_MODEL = 1024
DEPTH = 4
BATCH = 16
SEQ = 256
DEC_BATCH = 2
DEC_SEQ = 1024
GRID_W = 64
D_BRANCH = 512
POOL_WINDOWS = (2, 4, 8, 16)
N_GROUPS = 4
GROUP = D_BRANCH // N_GROUPS
SGU_CHUNK = 128
N_HEADS = 4
HEAD_DIM = D_BRANCH // N_HEADS
N_BRANCH = 3
D_FF = 4 * D_MODEL
EPS = 1e-6

LANES = 128
BF16_SUBLANES = 16

N_MAIN = 7 * D_BRANCH
N_GATE = 4 * N_HEADS
GATE_PAD = LANES
N_TOK_CTX = BATCH * SEQ
N_TOK = N_TOK_CTX + DEC_BATCH * DEC_SEQ
COND_ROWS = 8

TILE_M = 256
MLSTM_L = 256
MLSTM_ROW_BLOCK = 128
MERGE_COLS = 256
POOL_PAD = 8
FFN_TILE_M = 512
FFN_CHUNK = 512
FFN_STAGES = 2
ADA_TILE_N = 768
V7X_VMEM_BYTES = 64 * 1024 * 1024
VMEM_COMPILER_RESERVE = 6 * 1024 * 1024
VMEM_CALL_FLOOR = 60 * 1024 * 1024


def _vmem_request(windows, scratch):
    total = VMEM_COMPILER_RESERVE
    for arr, spec in windows:
        if spec.block_shape is not None:
            n_buf = 2 if spec.pipeline_mode is None else spec.pipeline_mode.buffer_count
            total += n_buf * math.prod(d for d in spec.block_shape if d is not None) * jnp.dtype(arr.dtype).itemsize
    for buf in scratch:
        total += math.prod(buf.shape) * jnp.dtype(buf.dtype).itemsize
    return min(max(total, VMEM_CALL_FLOOR), V7X_VMEM_BYTES)


_dot = functools.partial(jnp.dot, preferred_element_type=F32)
_dot_nt = functools.partial(lax.dot_general, dimension_numbers=(((1,), (1,)), ((), ())),
                            preferred_element_type=F32)


def _aligned(offset, align):
    return offset if isinstance(offset, int) else pl.multiple_of(offset, align)


def _ada_block(cond_ref, w_ref, b_ref, out_ref, *, layer=0):
    cond = cond_ref[...]
    s = (cond * jax.nn.sigmoid(cond)).astype(BF16)
    out_ref[...] = _dot(s, w_ref[...].astype(BF16)) + b_ref[layer:layer + 1, :]


def _ada_call(cond8, w_ada, b_ada):
    n_out = 6 * D_MODEL
    args = (cond8, w_ada, b_ada)
    in_specs = [
        pl.BlockSpec((COND_ROWS, D_MODEL), lambda j: (0, 0)),
        pl.BlockSpec((None, D_MODEL, ADA_TILE_N), lambda j: (0, 0, j)),
        pl.BlockSpec((DEPTH, ADA_TILE_N), lambda j: (0, j)),
    ]
    out_spec = pl.BlockSpec((COND_ROWS, ADA_TILE_N), lambda j: (0, j))
    out_shape = jax.ShapeDtypeStruct((COND_ROWS, n_out), F32)
    return pl.pallas_call(
        _ada_block,
        grid=(n_out // ADA_TILE_N,),
        in_specs=in_specs,
        out_specs=out_spec,
        out_shape=out_shape,
        compiler_params=pltpu.CompilerParams(
            dimension_semantics=("arbitrary",),
            vmem_limit_bytes=_vmem_request(list(zip(args, in_specs)) + [(out_shape, out_spec)], [])),
        name="ada_mod",
    )(*args)


def _seg_scan(x, pos, seg, total, op, reverse):
    d = 1
    while d < seg:
        if reverse:
            shifted = pltpu.roll(x, total - d, 1)
            valid = pos < seg - d
        else:
            shifted = pltpu.roll(x, d, 1)
            valid = pos >= d
        if op == "sum":
            x = x + jnp.where(valid, shifted, 0.0)
        else:
            x = jnp.maximum(x, jnp.where(valid, shifted, -jnp.inf))
        d *= 2
    return x


_MIXER_LAYER_ROW_INPUTS = (2, 10, 11, 14, 15)


def _mixer_kernel(*refs, T, grid2d, has_state, layer, n_in, n_pass, n_out):
    ins, outs, scr = list(refs[:n_in - n_pass]), refs[n_in:n_in + n_out], refs[n_in + n_out:]
    ins[1] = ins[1].at[pl.ds(1 + pl.program_id(0) if has_state else 0, 1), :]
    for k in _MIXER_LAYER_ROW_INPUTS:
        ins[k] = ins[k].at[pl.ds(layer, 1), :]
    state_outs = list(outs[1:])
    if state_outs and n_pass == 0:
        for k, o in enumerate(state_outs):
            o[...] = jnp.zeros(o.shape, o.dtype)
            state_outs[k] = o.at[layer]
    for thunk in _mixer_sequence(*ins, outs[0], *state_outs, *scr, T=T, grid2d=grid2d, has_state=has_state):
        thunk()


def _mixer_sequence(*refs, T, grid2d, has_state):
    it = iter(refs)
    x_ref = next(it)
    mod_ref = next(it)
    g1_ref = next(it)
    wmain_ref = next(it)
    wkt_ref = next(it)
    wgate_ref = next(it)
    wbrg_ref = next(it)
    wbranch_ref = next(it)
    wout_ref = next(it)
    wpool_ref = next(it)
    pscale_ref = next(it)
    gsgu_ref = next(it)
    wsgu_ref = next(it)
    bsgu_ref = next(it)
    bgates_ref = next(it)
    gml_ref = next(it)
    if has_state:
        c0_ref, n0_ref, m0_ref = next(it), next(it), next(it)
    out_ref = next(it)
    if not has_state:
        cout_ref, nout_ref, mout_ref = next(it), next(it), next(it)
    (u_scr, xp_scr, su_scr, vn_scr, q_scr, kt_scr, vaug_scr, o_scr, grow_scr, colf_scr,
     h_scr, ya_scr, yb_scr, yc_scr, mab_scr, gc_scr, merged_scr) = [next(it) for _ in range(17)]
    if has_state:
        cn_scr = next(it)
    if grid2d:
        m1_scr, dlt_scr = next(it), next(it)

    L = MLSTM_L
    n_tiles = T // TILE_M
    n_chunks = T // L
    RB = MLSTM_ROW_BLOCK
    NW = MERGE_COLS
    n_merge = D_MODEL // NW

    def tile_rows(i):
        return pl.ds(_aligned(i * TILE_M, TILE_M), TILE_M)

    def a_norm(i):
        mod = mod_ref[...]
        sh1 = mod[:, 0:D_MODEL]
        sc1 = mod[:, D_MODEL:2 * D_MODEL]
        x = x_ref[tile_rows(i), :]
        ms = jnp.mean(x * x, axis=-1, keepdims=True)
        u = x * lax.rsqrt(ms + EPS) * g1_ref[...] * (1.0 + sc1) + sh1
        u_scr[tile_rows(i), :] = u.astype(BF16)

    def a_gates(i):
        lane = lax.broadcasted_iota(jnp.int32, (2 * N_HEADS, TILE_M), 1)
        is_fwd = lax.broadcasted_iota(jnp.int32, (2 * N_HEADS, TILE_M), 0) < N_HEADS
        bias = jnp.concatenate([bgates_ref[...], jnp.zeros((1, GATE_PAD - N_GATE), F32)], axis=1)
        gate_t = (_dot(u_scr[tile_rows(i), :], wgate_ref[...]) + bias).T
        gp = gate_t[0:N_GATE, :]
        i_all = gp[0:2 * N_HEADS, :]
        f_all = gp[2 * N_HEADS:4 * N_HEADS, :]
        logf = jnp.minimum(f_all, 0.0) - jnp.log1p(jnp.exp(-jnp.abs(f_all)))
        b_row = jnp.where(is_fwd, _seg_scan(logf, lane, L, L, "sum", False),
                          _seg_scan(logf, lane, L, L, "sum", True))
        g_row = i_all - b_row
        cm_row = jnp.where(is_fwd, _seg_scan(g_row, lane, L, L, "max", False),
                           _seg_scan(g_row, lane, L, L, "max", True))
        grow_scr[:, tile_rows(i)] = g_row
        stacked = jnp.concatenate(
            [b_row, cm_row, jnp.zeros((GATE_PAD - 4 * N_HEADS, TILE_M), F32)], axis=0)
        colf_scr[tile_rows(i), :] = stacked.T

    def proj(i, k):
        return _dot(u_scr[tile_rows(i), :], wmain_ref[:, k * D_BRANCH:(k + 1) * D_BRANCH])

    def a_xp(i):
        xp_scr[pl.ds(_aligned(i * TILE_M + POOL_PAD, 8), TILE_M), :] = proj(i, 0)

    def a_su(i):
        su_scr[...] = proj(i, 1)

    def a_sv(i):
        sv = proj(i, 2)
        vn = sv * lax.rsqrt(jnp.mean(sv * sv, axis=-1, keepdims=True) + EPS) * gsgu_ref[...]
        vn_scr[...] = vn.astype(BF16)

    def a_q(i):
        q_scr[tile_rows(i), :] = proj(i, 3).astype(BF16)

    def a_k(i):
        kt = _dot_nt(wkt_ref[...], u_scr[tile_rows(i), :])
        kt_scr[:, tile_rows(i)] = (kt * (HEAD_DIM ** -0.5)).astype(BF16)

    def a_v(i):
        v = proj(i, 5).astype(BF16)
        ones_blk = jnp.ones((TILE_M, HEAD_DIM), BF16)
        for h in range(N_HEADS):
            vaug_scr[tile_rows(i), 2 * h * HEAD_DIM:(2 * h + 1) * HEAD_DIM] = v[:, h * HEAD_DIM:(h + 1) * HEAD_DIM]
            vaug_scr[tile_rows(i), (2 * h + 1) * HEAD_DIM:(2 * h + 2) * HEAD_DIM] = ones_blk

    def a_o(i):
        o_scr[tile_rows(i), :] = proj(i, 6)

    def a_sgu(ch, i):
        crow = slice(ch * SGU_CHUNK, (ch + 1) * SGU_CHUNK)
        for g in range(N_GROUPS):
            gc = slice(g * GROUP, (g + 1) * GROUP)
            mixed = _dot(wsgu_ref[g], vn_scr[crow, gc]) + bsgu_ref[:, gc]
            yb = su_scr[crow, gc] * mixed
            yb_scr[pl.ds(_aligned(i * TILE_M + ch * SGU_CHUNK, SGU_CHUNK), SGU_CHUNK), gc] = yb.astype(BF16)

    a_parts = [a_norm, a_gates, a_xp, a_su, a_sv, a_q, a_k, a_v, a_o]
    a_parts += [functools.partial(a_sgu, ch) for ch in range(TILE_M // SGU_CHUNK)]

    def pool_pad():
        zpad = jnp.zeros((POOL_PAD, D_BRANCH), F32)
        xp_scr[0:POOL_PAD, :] = zpad
        xp_scr[T + POOL_PAD:T + 2 * POOL_PAD, :] = zpad

    def pool_seq(g):
        win = POOL_WINDOWS[g]
        gc = slice(g * GROUP, (g + 1) * GROUP)
        half = win // 2
        t_idx = lax.broadcasted_iota(jnp.int32, (T, GROUP), 0)
        total = jnp.zeros((T, GROUP), F32)
        for j in range(-half, win - half):
            total = total + xp_scr[POOL_PAD + j:POOL_PAD + j + T, gc]
        cnt = (jnp.minimum(t_idx + (win - half), T) - jnp.maximum(t_idx - half, 0)).astype(F32)
        dlt = total / cnt - xp_scr[POOL_PAD:POOL_PAD + T, gc]
        ya = _dot(dlt.astype(BF16), wpool_ref[g]) * pscale_ref[...][:, gc]
        ya_scr[:, gc] = ya.astype(BF16)

    def a_colpool(i):
        c_idx = lax.broadcasted_iota(jnp.int32, (GRID_W, GROUP), 0)
        for g, win in enumerate(POOL_WINDOWS):
            gc = slice(g * GROUP, (g + 1) * GROUP)
            half = win // 2
            inv_cnt = 1.0 / (jnp.minimum(c_idx + (win - half), GRID_W) - jnp.maximum(c_idx - half, 0)).astype(F32)
            for r in range(TILE_M // GRID_W):
                r0 = i * TILE_M + r * GRID_W
                xg = xp_scr[pl.ds(_aligned(r0 + POOL_PAD, 8), GRID_W), gc]
                total = xg
                for j in range(-half, win - half):
                    if j != 0:
                        valid = (c_idx + j >= 0) & (c_idx + j < GRID_W)
                        total = total + jnp.where(valid, pltpu.roll(xg, (-j) % GRID_W, 0), 0.0)
                m1_scr[pl.ds(_aligned(r0, GRID_W), GRID_W), gc] = total * inv_cnt

    def pool_grid_rows(g):
        win = POOL_WINDOWS[g]
        gc = slice(g * GROUP, (g + 1) * GROUP)
        half = win // 2
        n_rows = T // GRID_W
        for r in range(n_rows):
            lo = max(r - half, 0)
            hi = min(r + (win - half), n_rows)
            acc = m1_scr[lo * GRID_W:(lo + 1) * GRID_W, gc]
            for rr in range(lo + 1, hi):
                acc = acc + m1_scr[rr * GRID_W:(rr + 1) * GRID_W, gc]
            xg = xp_scr[POOL_PAD + r * GRID_W:POOL_PAD + (r + 1) * GRID_W, gc]
            dlt_scr[r * GRID_W:(r + 1) * GRID_W, :] = acc / float(hi - lo) - xg
        ya = _dot(dlt_scr[...].astype(BF16), wpool_ref[g]) * pscale_ref[...][:, gc]
        ya_scr[:, gc] = ya.astype(BF16)

    if grid2d:
        assert TILE_M % GRID_W == 0
        a_parts.insert(a_parts.index(a_xp) + 1, a_colpool)
        pool_parts = [functools.partial(pool_grid_rows, g) for g in range(N_GROUPS)]
    else:
        pool_parts = [pool_pad] + [functools.partial(pool_seq, g) for g in range(N_GROUPS)]

    def keep_mask(d, rb, col_lo, ncols):
        t_loc = rb * RB + lax.broadcasted_iota(jnp.int32, (RB, ncols), 0)
        s_loc = col_lo + lax.broadcasted_iota(jnp.int32, (RB, ncols), 1)
        return s_loc <= t_loc if d == 0 else s_loc >= t_loc

    def mlstm_prepare(d, c, m_in):
        c0 = _aligned(c * L, L)
        colf = colf_scr[pl.ds(c0, L), :]
        bcol = colf[:, N_HEADS * d:N_HEADS * (d + 1)]
        cmcol = colf[:, 2 * N_HEADS + N_HEADS * d:2 * N_HEADS + N_HEADS * (d + 1)]
        mx = jnp.maximum(m_in, cmcol)
        last = L - 1 if d == 0 else 0
        mx_last = mx[last:last + 1, :]
        return dict(c0=c0, mx=mx, mx_last=mx_last, w_inter=jnp.exp(m_in - mx), e_negm=jnp.exp(-(bcol + mx)),
                    decay=jnp.exp(m_in - mx_last), m_new=bcol[last:last + 1, :] + mx_last)

    def block_cols(d, rb):
        return (0, (rb + 1) * RB) if d == 0 else (rb * RB, L)

    def mlstm_scores(d, p, h):
        c0 = p["c0"]
        hc = slice(h * HEAD_DIM, (h + 1) * HEAD_DIM)
        qks = []
        for rb in range(L // RB):
            col_lo, col_hi = block_cols(d, rb)
            qb = q_scr[pl.ds(_aligned(c0 + rb * RB, RB), RB), hc]
            qks.append(_dot(qb, kt_scr[hc, pl.ds(_aligned(c0 + col_lo, RB), col_hi - col_lo)]))
        p["qk", h] = qks

    def mlstm_head(d, p, h):
        c0, mx, mx_last = p["c0"], p["mx"], p["mx_last"]
        rows = pl.ds(c0, L)
        hc = slice(h * HEAD_DIM, (h + 1) * HEAD_DIM)
        ac = slice(2 * h * HEAD_DIM, (2 * h + 2) * HEAD_DIM)
        grow = grow_scr[N_HEADS * d + h:N_HEADS * d + h + 1, rows]
        if has_state:
            cn_dh = cn_scr[d, h]
            cn_bf = cn_dh.astype(BF16)
        for rb in range(L // RB):
            col_lo, col_hi = block_cols(d, rb)
            ncols = col_hi - col_lo
            trow = slice(rb * RB, (rb + 1) * RB)
            rrows = pl.ds(_aligned(c0 + rb * RB, RB), RB)
            crows = pl.ds(_aligned(c0 + col_lo, RB), ncols)
            qb = q_scr[rrows, hc]
            qk = p["qk", h][rb]
            w = jnp.exp(grow[:, col_lo:col_hi] - mx[trow, h:h + 1])
            s = (qk * jnp.where(keep_mask(d, rb, col_lo, ncols), w, 0.0)).astype(BF16)
            na = _dot(s, vaug_scr[crows, ac])
            if has_state:
                na = na + p["w_inter"][trow, h:h + 1] * _dot(qb, cn_bf)
            hval = na[:, 0:HEAD_DIM] / jnp.maximum(jnp.abs(na[:, HEAD_DIM:]), p["e_negm"][trow, h:h + 1])
            if d == 0:
                h_scr[rrows, hc] = hval
            else:
                h_scr[rrows, hc] += hval
        w_state = jnp.exp(grow - mx_last[:, h:h + 1])
        kw_t = (kt_scr[hc, rows].astype(F32) * w_state).astype(BF16)
        upd = _dot(kw_t, vaug_scr[rows, ac])
        if has_state:
            cn_scr[d, h] = p["decay"][:, h:h + 1] * cn_dh + upd
        else:
            cout_ref[d, h] = upd[:, 0:HEAD_DIM]
            nout_ref[d, h:h + 1, :] = upd[:, HEAD_DIM:].T[0:1, :]

    def merge_ab(c, nj):
        nc = slice(nj * NW, (nj + 1) * NW)
        ub = u_scr[tile_rows(c), :]
        acc = None
        for r, y_scr in enumerate((ya_scr, yb_scr)):
            gate = jax.nn.sigmoid(_dot(ub, wbrg_ref[:, r * D_MODEL + nj * NW:r * D_MODEL + (nj + 1) * NW]))
            term = gate * _dot(y_scr[tile_rows(c), :], wbranch_ref[r * D_BRANCH:(r + 1) * D_BRANCH, nc])
            acc = term if acc is None else acc + term
        mab_scr[tile_rows(c), nc] = acc

    def gate_c(c, nj):
        w = wbrg_ref[:, 2 * D_MODEL + nj * NW:2 * D_MODEL + (nj + 1) * NW]
        gc_scr[:, nj * NW:(nj + 1) * NW] = jax.nn.sigmoid(_dot(u_scr[tile_rows(c), :], w))

    def tail_yc(c):
        gml = gml_ref[...]
        for h in range(N_HEADS):
            hc = slice(h * HEAD_DIM, (h + 1) * HEAD_DIM)
            hh = h_scr[tile_rows(c), hc]
            hn = hh * lax.rsqrt(jnp.mean(hh * hh, axis=-1, keepdims=True) + EPS) * gml[:, hc]
            yc_scr[:, hc] = (jax.nn.sigmoid(o_scr[tile_rows(c), hc]) * hn).astype(BF16)

    def tail_merge(c, nj):
        nc = slice(nj * NW, (nj + 1) * NW)
        acc = mab_scr[tile_rows(c), nc] + gc_scr[:, nc] * _dot(yc_scr[...], wbranch_ref[2 * D_BRANCH:, nc])
        merged_scr[:, nc] = acc.astype(BF16)

    def tail_out(c):
        gt1 = mod_ref[...][:, 2 * D_MODEL:3 * D_MODEL]
        out_ref[tile_rows(c), :] = x_ref[tile_rows(c), :] + gt1 * _dot(merged_scr[...], wout_ref[...])

    def tail_parts(c):
        return ([functools.partial(tail_yc, c)] + [functools.partial(tail_merge, c, nj) for nj in range(n_merge)]
                + [functools.partial(tail_out, c)])

    def scan_parts(d, c, m_in, fillers, out):
        p = {}

        def prepare():
            p.update(mlstm_prepare(d, c, m_in))
            out[d] = p["m_new"]

        parts = [prepare]
        for h in range(N_HEADS):
            parts.append(functools.partial(mlstm_scores, d, p, h))
            if h < len(fillers):
                parts.append(fillers[h])
            parts.append(functools.partial(mlstm_head, d, p, h))
        return parts + list(fillers[N_HEADS:])

    if n_tiles == 1 and n_chunks == 1:
        zeros = jnp.zeros((1, N_HEADS), F32)
        m_out = {}
        parts = [functools.partial(part, 0) for part in a_parts] + pool_parts
        parts += scan_parts(0, 0, zeros, [functools.partial(merge_ab, 0, nj) for nj in range(n_merge)], m_out)
        parts += scan_parts(1, 0, zeros, [functools.partial(gate_c, 0, nj) for nj in range(n_merge)], m_out)
        parts += tail_parts(0)

        def write_m():
            mout_ref[0:1, :] = m_out[0]
            mout_ref[1:2, :] = m_out[1]

        return parts + [write_m]

    def run_loops():
        def phase_a(i, carry):
            for part in a_parts:
                part(i)
            return carry

        lax.fori_loop(0, n_tiles, phase_a, 0)
        for part in pool_parts:
            part()
        for d in range(2):
            for h in range(N_HEADS):
                n_rows = jnp.broadcast_to(n0_ref[d, h:h + 1, :], (HEAD_DIM, HEAD_DIM))
                cn_scr[d, h] = jnp.concatenate([c0_ref[d, h], n_rows.T], axis=1)
        m_init = m0_ref[...]

        def fwd_step(j, m_in):
            m_out = {}
            for part in scan_parts(0, j, m_in, [functools.partial(merge_ab, j, nj) for nj in range(n_merge)], m_out):
                part()
            return m_out[0]

        def bwd_step(j, m_in):
            c = n_chunks - 1 - j
            m_out = {}
            for part in scan_parts(1, c, m_in, [functools.partial(gate_c, c, nj) for nj in range(n_merge)], m_out):
                part()
            for part in tail_parts(c):
                part()
            return m_out[1]

        lax.fori_loop(0, n_chunks, fwd_step, m_init[0:1, :])
        lax.fori_loop(0, n_chunks, bwd_step, m_init[1:2, :])

    assert has_state, "multi-chunk sequences carry the cached state"
    return [run_loops]


def _mixer_call(l, xall, lay, prm, stream, states=None, state_outs=None):
    ctx = stream == "ctx"
    T = SEQ if ctx else DEC_SEQ
    assert T % TILE_M == 0 and TILE_M == MLSTM_L and MLSTM_L % MLSTM_ROW_BLOCK == 0
    n_b = BATCH if ctx else DEC_BATCH
    own_array = xall.shape[0] != N_TOK
    blk0 = 0 if ctx or own_array else N_TOK_CTX // T
    once = pl.Buffered(1)

    def wspec(shape):
        nd = len(shape)
        return pl.BlockSpec((None,) + shape, lambda b: (l,) + (0,) * nd, pipeline_mode=once)

    def lspec(shape):
        return pl.BlockSpec(shape, lambda b: (0,) * len(shape), pipeline_mode=once)

    x_spec = pl.BlockSpec((T, D_MODEL), lambda b: (blk0 + b, 0))
    in_specs = [
        x_spec,
        lspec((COND_ROWS, 6 * D_MODEL)),
        lspec((DEPTH, D_MODEL)),
        wspec((D_MODEL, N_MAIN)),
        lspec((D_BRANCH, D_MODEL)),
        lspec((D_MODEL, GATE_PAD)),
        lspec((D_MODEL, N_BRANCH * D_MODEL)),
        lspec((N_BRANCH * D_BRANCH, D_MODEL)),
        lspec((D_MODEL, D_MODEL)),
        wspec((N_GROUPS, GROUP, GROUP)),
        lspec((DEPTH, D_BRANCH)),
        lspec((DEPTH, D_BRANCH)),
        wspec((N_GROUPS, SGU_CHUNK, SGU_CHUNK)),
        wspec((SGU_CHUNK, D_BRANCH)),
        lspec((DEPTH, N_GATE)),
        lspec((DEPTH, D_BRANCH)),
    ]
    args = [xall, lay["mods"], prm["g_norm1"], prm["w_in16"], lay["w_kt"],
            lay["w_gate"], lay["w_brg"], lay["w_branch"], lay["w_out"],
            prm["w_pool"], prm["pool_scale"], prm["g_sgu"], prm["w_sgu"],
            prm["b_sgu_tile"], prm["b_gates"], prm["g_mlstm"]]
    out_shape = [jax.ShapeDtypeStruct(xall.shape, xall.dtype)]
    out_specs = [pl.BlockSpec((T, D_MODEL), lambda b: (blk0 + b, 0), pipeline_mode=None if ctx else once)]
    aliases = {} if own_array else {0: 0}
    passthrough = []
    if ctx:
        per_seq = [(2, N_HEADS, HEAD_DIM, HEAD_DIM), (2, N_HEADS, HEAD_DIM), (2, N_HEADS)]
        out_shape += [jax.ShapeDtypeStruct((BATCH, DEPTH) + s, F32) for s in per_seq]
        if state_outs is None:
            out_specs += [pl.BlockSpec((None, DEPTH) + s, lambda b, nd=len(s): (b,) + (0,) * (nd + 1))
                          for s in per_seq]
        else:
            out_specs += [pl.BlockSpec((None, None) + s, lambda b, nd=len(s): (b, l) + (0,) * nd) for s in per_seq]
            passthrough += [(a, 1 + k) for k, a in enumerate(state_outs)]
    else:
        in_specs += [
            pl.BlockSpec((None, None, 2, N_HEADS, HEAD_DIM, HEAD_DIM), lambda b: (b, l, 0, 0, 0, 0)),
            pl.BlockSpec((None, None, 2, N_HEADS, HEAD_DIM), lambda b: (b, l, 0, 0, 0)),
            pl.BlockSpec((None, None, 2, N_HEADS), lambda b: (b, l, 0, 0)),
        ]
        args += list(states)
    for buf, out_idx in passthrough:
        aliases[len(args)] = out_idx
        in_specs.append(pl.BlockSpec(memory_space=pl.ANY))
        args.append(buf)

    scratch = [
        pltpu.VMEM((T, D_MODEL), BF16),
        pltpu.VMEM((T + 2 * POOL_PAD, D_BRANCH), F32),
        pltpu.VMEM((TILE_M, D_BRANCH), F32),
        pltpu.VMEM((TILE_M, D_BRANCH), BF16),
        pltpu.VMEM((T, D_BRANCH), BF16),
        pltpu.VMEM((D_BRANCH, T), BF16),
        pltpu.VMEM((T, 2 * D_BRANCH), BF16),
        pltpu.VMEM((T, D_BRANCH), F32),
        pltpu.VMEM((2 * N_HEADS, T), F32),
        pltpu.VMEM((T, GATE_PAD), F32),
        pltpu.VMEM((T, D_BRANCH), F32),
        pltpu.VMEM((T, D_BRANCH), BF16),
        pltpu.VMEM((T, D_BRANCH), BF16),
        pltpu.VMEM((TILE_M, D_BRANCH), BF16),
        pltpu.VMEM((T, D_MODEL), F32),
        pltpu.VMEM((TILE_M, D_MODEL), F32),
        pltpu.VMEM((TILE_M, D_MODEL), BF16),
    ]
    if not ctx:
        scratch += [pltpu.VMEM((2, N_HEADS, HEAD_DIM, 2 * HEAD_DIM), F32)]
        scratch += [pltpu.VMEM((T, D_BRANCH), F32), pltpu.VMEM((T, GROUP), F32)]

    outs = pl.pallas_call(
        functools.partial(_mixer_kernel, T=T, grid2d=not ctx, has_state=not ctx, layer=l,
                          n_in=len(in_specs), n_pass=len(passthrough), n_out=len(out_specs)),
        grid=(n_b,),
        in_specs=in_specs,
        out_specs=out_specs,
        out_shape=out_shape,
        scratch_shapes=scratch,
        input_output_aliases=aliases,
        compiler_params=pltpu.CompilerParams(
            dimension_semantics=("arbitrary",),
            vmem_limit_bytes=_vmem_request(list(zip(args, in_specs)) + list(zip(out_shape, out_specs)), scratch)),
        name="mixer_ctx" if ctx else "mixer_lat",
    )(*args)
    return outs


def _ffn_kernel(*refs, layer, final, split_input):
    n_x = 2 if split_input else 1
    x_refs, (mod_ref, g2_ref, w1_hbm, w2_hbm, gfin_ref), rest = refs[:n_x], refs[n_x:n_x + 5], refs[n_x + 5:]
    if final:
        out_refs, rest = rest[:2], rest[2:]
    else:
        (cond_ref, wada_ref, bada_ref, wbr_ref, wo_ref, win_ref), rest = rest[:6], rest[6:]
        out_refs, (mods_out, wbr_out, wo_out, wbrg_out, wgate_out, wkt_out), rest = rest[:1], rest[1:7], rest[7:]
    acc_scr, h_scr, w1_scr, w2_scr, st1_scr, st2_scr, sem = rest[:7]
    n_chunks = D_FF // FFN_CHUNK
    tile = pl.program_id(0)
    first_tile = tile == 0
    ctx_tiles, lat_tiles = N_TOK_CTX // FFN_TILE_M, DEC_SEQ // FFN_TILE_M
    if split_input:
        x_ref = rest[7]

        @pl.when(tile < ctx_tiles)
        def _():
            x_ref[...] = x_refs[0][...]

        @pl.when(tile >= ctx_tiles)
        def _():
            x_ref[...] = x_refs[1][...]
    else:
        x_ref = x_refs[0]
    mod_ref = mod_ref.at[pl.ds(jnp.where(tile < ctx_tiles, 0, 1 + (tile - ctx_tiles) // lat_tiles), 1), :]
    g2_ref = g2_ref.at[pl.ds(layer, 1), :]

    def w1_copy(c):
        slot = c % FFN_STAGES
        c0 = _aligned(c * FFN_CHUNK, FFN_CHUNK)
        return pltpu.make_async_copy(w1_hbm.at[layer, :, pl.ds(c0, FFN_CHUNK)], st1_scr.at[slot], sem.at[0, slot])

    def w2_copy(c):
        slot = c % FFN_STAGES
        c0 = _aligned(c * FFN_CHUNK, FFN_CHUNK)
        return pltpu.make_async_copy(w2_hbm.at[layer, pl.ds(c0, FFN_CHUNK), :], st2_scr.at[slot], sem.at[1, slot])

    def land_w1(c):
        w1_copy(c).wait()
        c0 = _aligned(c * FFN_CHUNK, FFN_CHUNK)
        w1_scr[:, pl.ds(c0, FFN_CHUNK)] = st1_scr[c % FFN_STAGES].astype(BF16)

    def land_w2(c):
        w2_copy(c).wait()
        c0 = _aligned(c * FFN_CHUNK, FFN_CHUNK)
        w2_scr[pl.ds(c0, FFN_CHUNK), :] = st2_scr[c % FFN_STAGES].astype(BF16)

    def prepare_next_layer():
        if not final:
            _ada_block(cond_ref, wada_ref, bada_ref, mods_out, layer=layer + 1)
            wbr_out[...] = wbr_ref[...].astype(BF16)
            wo_out[...] = wo_ref[...].astype(BF16)
            n_br = N_BRANCH * D_MODEL
            for c0 in range(0, n_br, FFN_CHUNK):
                width = min(FFN_CHUNK + GATE_PAD, n_br + N_GATE - c0)
                window = win_ref[:, N_MAIN + c0:N_MAIN + c0 + width]
                wbrg_out[:, c0:c0 + FFN_CHUNK] = window[:, N_GATE:N_GATE + FFN_CHUNK]
            gate_tile = win_ref[:, N_MAIN:N_MAIN + GATE_PAD]
            is_gate = lax.broadcasted_iota(jnp.int32, gate_tile.shape, 1) < N_GATE
            wgate_out[...] = jnp.where(is_gate, gate_tile, jnp.zeros_like(gate_tile))
            wkt_out[...] = win_ref[:, 4 * D_BRANCH:5 * D_BRANCH].astype(F32).T.astype(BF16)

    def normed_input():
        mod = mod_ref[...]
        sh2 = mod[:, 3 * D_MODEL:4 * D_MODEL]
        sc2 = mod[:, 4 * D_MODEL:5 * D_MODEL]
        x = x_ref[...]
        ms = jnp.mean(x * x, axis=-1, keepdims=True)
        return (x * lax.rsqrt(ms + EPS) * g2_ref[...] * (1.0 + sc2) + sh2).astype(BF16)

    def hidden(ub, c):
        c0 = _aligned(c * FFN_CHUNK, FFN_CHUNK)
        h = _dot(ub, w1_scr[:, pl.ds(c0, FFN_CHUNK)])
        h_scr[:, pl.ds(c0, FFN_CHUNK)] = jnp.square(jnp.maximum(h, 0.0)).astype(BF16)

    def finish(ffn_out):
        gt2 = mod_ref[...][:, 5 * D_MODEL:6 * D_MODEL]
        y = x_ref[...] + gt2 * ffn_out
        if not final:
            out_refs[0][...] = y
            return
        y = y * lax.rsqrt(jnp.mean(y * y, axis=-1, keepdims=True) + EPS) * gfin_ref[...]
        in_ctx = tile < ctx_tiles

        @pl.when(in_ctx)
        def _():
            out_refs[0][...] = y

        @pl.when(jnp.logical_not(in_ctx))
        def _():
            out_refs[1][...] = y

    @pl.when(first_tile)
    def _():
        for c in range(FFN_STAGES):
            w1_copy(c).start()
            w2_copy(c).start()
        prepare_next_layer()
        ub = normed_input()
        acc_scr[...] = jnp.zeros_like(acc_scr)

        def chunk(c, carry):
            land_w1(c)
            land_w2(c)

            @pl.when(c + FFN_STAGES < n_chunks)
            def _():
                w1_copy(c + FFN_STAGES).start()
                w2_copy(c + FFN_STAGES).start()

            hidden(ub, c)
            c0 = _aligned(c * FFN_CHUNK, FFN_CHUNK)
            acc_scr[...] += _dot(h_scr[:, pl.ds(c0, FFN_CHUNK)], w2_scr[pl.ds(c0, FFN_CHUNK), :])
            return carry

        lax.fori_loop(0, n_chunks, chunk, 0)
        finish(acc_scr[...])

    @pl.when(jnp.logical_not(first_tile))
    def _():
        prepare_next_layer()
        ub = normed_input()
        for c in range(n_chunks):
            hidden(ub, c)
        finish(_dot(h_scr[...], w2_scr[...]))


def _ffn_call(l, xall, lay, prm, g_final, final):
    n_tiles = N_TOK // FFN_TILE_M
    ctx_tiles = N_TOK_CTX // FFN_TILE_M
    once = pl.Buffered(1)
    split_input = isinstance(xall, tuple)
    assert not (split_input and final)

    x_spec = pl.BlockSpec((FFN_TILE_M, D_MODEL), lambda t: (t, 0))
    ctx_spec = pl.BlockSpec((FFN_TILE_M, D_MODEL), lambda t: (jnp.minimum(t, ctx_tiles - 1), 0))
    lat_spec = pl.BlockSpec((FFN_TILE_M, D_MODEL), lambda t: (jnp.maximum(t - ctx_tiles, 0), 0))
    xs = list(xall) if split_input else [xall]
    in_specs = ([ctx_spec, lat_spec] if split_input else [x_spec]) + [
        pl.BlockSpec((COND_ROWS, 6 * D_MODEL), lambda t: (0, 0), pipeline_mode=once),
        pl.BlockSpec((DEPTH, D_MODEL), lambda t: (0, 0), pipeline_mode=once),
        pl.BlockSpec(memory_space=pl.ANY),
        pl.BlockSpec(memory_space=pl.ANY),
        pl.BlockSpec((1, D_MODEL), lambda t: (0, 0)),
    ]
    args = xs + [lay["mods"], prm["g_norm2"], prm["w_ff1"], prm["w_ff2"], g_final.reshape(1, D_MODEL)]
    if final:
        out_specs = [ctx_spec, lat_spec]
        out_shape = [jax.ShapeDtypeStruct((N_TOK_CTX, D_MODEL), F32),
                     jax.ShapeDtypeStruct((N_TOK - N_TOK_CTX, D_MODEL), F32)]
        aliases = {}
    else:
        ada_cols = 6 * D_MODEL // n_tiles
        br_rows = N_BRANCH * D_BRANCH // n_tiles
        wo_tiles = D_MODEL // br_rows
        assert ada_cols % LANES == 0 and br_rows % BF16_SUBLANES == 0 and wo_tiles <= n_tiles
        wo_map = lambda t: (jnp.minimum(t, wo_tiles - 1), 0)
        in_specs += [
            pl.BlockSpec((COND_ROWS, D_MODEL), lambda t: (0, 0)),
            pl.BlockSpec((None, D_MODEL, ada_cols), lambda t: (l + 1, 0, t)),
            pl.BlockSpec((DEPTH, ada_cols), lambda t: (0, t)),
            pl.BlockSpec((None, br_rows, D_MODEL), lambda t: (l + 1, t, 0)),
            pl.BlockSpec((None, br_rows, D_MODEL), lambda t: (l + 1,) + wo_map(t)),
            pl.BlockSpec((None, br_rows, prm["w_in16"].shape[-1]), lambda t: (l + 1,) + wo_map(t)),
        ]
        args += [prm["cond8"], prm["w_ada"], prm["b_ada"], prm["w_branch_rows"], prm["w_out"], prm["w_in16"]]
        out_specs = [
            x_spec,
            pl.BlockSpec((COND_ROWS, ada_cols), lambda t: (0, t)),
            pl.BlockSpec((br_rows, D_MODEL), lambda t: (t, 0)),
            pl.BlockSpec((br_rows, D_MODEL), wo_map),
            pl.BlockSpec((br_rows, N_BRANCH * D_MODEL), wo_map),
            pl.BlockSpec((br_rows, GATE_PAD), wo_map),
            pl.BlockSpec((D_BRANCH, br_rows), lambda t: wo_map(t)[::-1]),
        ]
        out_shape = [jax.ShapeDtypeStruct((N_TOK, D_MODEL), F32),
                     jax.ShapeDtypeStruct((COND_ROWS, 6 * D_MODEL), F32),
                     jax.ShapeDtypeStruct((N_BRANCH * D_BRANCH, D_MODEL), BF16),
                     jax.ShapeDtypeStruct((D_MODEL, D_MODEL), BF16),
                     jax.ShapeDtypeStruct((D_MODEL, N_BRANCH * D_MODEL), BF16),
                     jax.ShapeDtypeStruct((D_MODEL, GATE_PAD), BF16),
                     jax.ShapeDtypeStruct((D_BRANCH, D_MODEL), BF16)]
        aliases = {} if split_input else {0: 0}
    sems = [pltpu.SemaphoreType.DMA((2, FFN_STAGES))]
    gather = [pltpu.VMEM((FFN_TILE_M, D_MODEL), F32)] if split_input else []
    scratch = [
        pltpu.VMEM((FFN_TILE_M, D_MODEL), F32),
        pltpu.VMEM((FFN_TILE_M, D_FF), BF16),
        pltpu.VMEM((D_MODEL, D_FF), BF16),
        pltpu.VMEM((D_FF, D_MODEL), BF16),
        pltpu.VMEM((FFN_STAGES, D_MODEL, FFN_CHUNK), F32),
        pltpu.VMEM((FFN_STAGES, FFN_CHUNK, D_MODEL), F32),
    ]
    return pl.pallas_call(
        functools.partial(_ffn_kernel, layer=l, final=final, split_input=split_input),
        grid=(n_tiles,),
        in_specs=in_specs,
        out_specs=out_specs,
        out_shape=out_shape,
        scratch_shapes=scratch + sems + gather,
        input_output_aliases=aliases,
        compiler_params=pltpu.CompilerParams(
            dimension_semantics=("arbitrary",),
            vmem_limit_bytes=_vmem_request(list(zip(args, in_specs)) + list(zip(out_shape, out_specs)),
                                           scratch + gather)),
        name="ffn_final" if final else "ffn",
    )(*args)


def kernel(x_prompt, x_sample, state_C, state_n, state_m, c, c_ctx, w_ada, b_ada, g_norm1, g_norm2, w_in,
           b_gates, w_pool, pool_scale, g_sgu, w_sgu, b_sgu, g_mlstm, w_branch, w_out, w_ff1, w_ff2, g_final):
    w_in16 = w_in.astype(BF16)
    prm = {
        "w_in16": w_in16,
        "w_branch_rows": w_branch.reshape(DEPTH, N_BRANCH * D_BRANCH, D_MODEL),
        "w_out": w_out,
        "w_ada": w_ada,
        "b_ada": b_ada,
        "w_pool": w_pool.astype(BF16),
        "w_sgu": w_sgu.astype(BF16),
        "w_ff1": w_ff1,
        "w_ff2": w_ff2,
        "g_norm1": g_norm1,
        "g_norm2": g_norm2,
        "pool_scale": pool_scale,
        "g_sgu": g_sgu,
        "g_mlstm": g_mlstm,
        "b_gates": b_gates,
        "b_sgu_tile": jnp.repeat(jnp.swapaxes(b_sgu, 1, 2), GROUP, axis=2),
        "cond8": jnp.concatenate(
            [c_ctx[None, :], c, jnp.zeros((COND_ROWS - 1 - DEC_BATCH, D_MODEL), F32)], axis=0),
    }
    lay = {
        "mods": _ada_call(prm["cond8"], w_ada, b_ada),
        "w_branch": prm["w_branch_rows"][0].astype(BF16),
        "w_out": w_out[0].astype(BF16),
        "w_brg": w_in16[0, :, N_MAIN + N_GATE:],
        "w_gate": jnp.pad(w_in16[0, :, N_MAIN:N_MAIN + N_GATE], ((0, 0), (0, GATE_PAD - N_GATE))),
        "w_kt": w_in16[0, :, 4 * D_BRANCH:5 * D_BRANCH].T,
    }

    xc, *new_states = _mixer_call(0, x_prompt.reshape(N_TOK_CTX, D_MODEL), lay, prm, "ctx")
    (xs,) = _mixer_call(0, x_sample.reshape(-1, D_MODEL), lay, prm, "lat", states=(state_C, state_n, state_m))
    xall = (xc, xs)
    for l in range(DEPTH):
        if l > 0:
            xall, *new_states = _mixer_call(l, xall, lay, prm, "ctx", state_outs=new_states)
            (xall,) = _mixer_call(l, xall, lay, prm, "lat", states=(state_C, state_n, state_m))
        if l < DEPTH - 1:
            xall, *nxt = _ffn_call(l, xall, lay, prm, g_final, final=False)
            lay = dict(zip(("mods", "w_branch", "w_out", "w_brg", "w_gate", "w_kt"), nxt))
    y_prompt, y_sample = _ffn_call(DEPTH - 1, xall, lay, prm, g_final, final=True)
    return (y_prompt.reshape(BATCH, SEQ, D_MODEL), y_sample.reshape(DEC_BATCH, DEC_SEQ, D_MODEL), *new_states)
```

```python
import functools
import math

import jax
import jax.numpy as jnp
from jax import lax
from jax.experimental import pallas as pl
from jax.experimental.pallas import tpu as pltpu

F32 = jnp.float32
BF16 = jnp.bfloat16

D_MODEL = 1024
DEPTH = 4
BATCH = 16
SEQ = 256
DEC_BATCH = 2
DEC_SEQ = 1024
GRID_W = 64
D_BRANCH = 512
POOL_WINDOWS = (2, 4, 8, 16)
N_GROUPS = 4
GROUP = D_BRANCH // N_GROUPS
SGU_CHUNK = 128
N_HEADS = 4
HEAD_DIM = D_BRANCH // N_HEADS
N_BRANCH = 3
D_FF = 4 * D_MODEL
EPS = 1e-6

LANES = 128
BF16_SUBLANES = 16

N_MAIN = 7 * D_BRANCH
N_GATE = 4 * N_HEADS
GATE_PAD = LANES
N_TOK_CTX = BATCH * SEQ
N_TOK = N_TOK_CTX + DEC_BATCH * DEC_SEQ
COND_ROWS = 8

TILE_M = 256
MLSTM_L = 256
MLSTM_ROW_BLOCK = 128
MERGE_COLS = 256
POOL_PAD = 8
FFN_TILE_M = 512
FFN_CHUNK = 512
FFN_STAGES = 2
V7X_VMEM_BYTES = 64 * 1024 * 1024
VMEM_COMPILER_RESERVE = 6 * 1024 * 1024
VMEM_CALL_FLOOR = 60 * 1024 * 1024


def _vmem_request(windows, scratch):
    total = VMEM_COMPILER_RESERVE
    for arr, spec in windows:
        if spec.block_shape is not None:
            n_buf = 2 if spec.pipeline_mode is None else spec.pipeline_mode.buffer_count
            total += n_buf * math.prod(d for d in spec.block_shape if d is not None) * jnp.dtype(arr.dtype).itemsize
    for buf in scratch:
        total += math.prod(buf.shape) * jnp.dtype(buf.dtype).itemsize
    return min(max(total, VMEM_CALL_FLOOR), V7X_VMEM_BYTES)


_dot = functools.partial(jnp.dot, preferred_element_type=F32)
_dot_nt = functools.partial(lax.dot_general, dimension_numbers=(((1,), (1,)), ((), ())),
                            preferred_element_type=F32)


def _aligned(offset, align):
    return offset if isinstance(offset, int) else pl.multiple_of(offset, align)


LAYER_ARRAYS = ("mods", "w_branch", "w_out", "w_brg", "w_gate", "w_kt")
N_PREP_OPERANDS = 6
N_PREP_STEPS = N_TOK // FFN_TILE_M


def _prepare_layer(cond_ref, wada_ref, bada_ref, wbr_ref, wo_ref, win_ref,
                   mods_out, wbr_out, wo_out, wbrg_out, wgate_out, wkt_out, *, layer):
    cond = cond_ref[...]
    s = (cond * jax.nn.sigmoid(cond)).astype(BF16)
    mods_out[...] = _dot(s, wada_ref[...].astype(BF16)) + bada_ref[layer:layer + 1, :]
    wbr_out[...] = wbr_ref[...].astype(BF16)
    wo_out[...] = wo_ref[...].astype(BF16)
    n_br = N_BRANCH * D_MODEL
    for c0 in range(0, n_br, FFN_CHUNK):
        width = min(FFN_CHUNK + GATE_PAD, n_br + N_GATE - c0)
        window = win_ref[:, N_MAIN + c0:N_MAIN + c0 + width]
        wbrg_out[:, c0:c0 + FFN_CHUNK] = window[:, N_GATE:N_GATE + FFN_CHUNK]
    gate_tile = win_ref[:, N_MAIN:N_MAIN + GATE_PAD]
    is_gate = lax.broadcasted_iota(jnp.int32, gate_tile.shape, 1) < N_GATE
    wgate_out[...] = jnp.where(is_gate, gate_tile, jnp.zeros_like(gate_tile))
    wkt_out[...] = win_ref[:, 4 * D_BRANCH:5 * D_BRANCH].astype(F32).T.astype(BF16)


def _layer_prep_specs(layer, prm):
    ada_cols = 6 * D_MODEL // N_PREP_STEPS
    br_rows = N_BRANCH * D_BRANCH // N_PREP_STEPS
    wo_steps = D_MODEL // br_rows
    assert ada_cols % LANES == 0 and br_rows % BF16_SUBLANES == 0 and wo_steps <= N_PREP_STEPS
    wo_map = lambda t: (jnp.minimum(t, wo_steps - 1), 0)
    args = [prm["cond8"], prm["w_ada"], prm["b_ada"], prm["w_branch_rows"], prm["w_out"], prm["w_in16"]]
    in_specs = [
        pl.BlockSpec((COND_ROWS, D_MODEL), lambda t: (0, 0)),
        pl.BlockSpec((None, D_MODEL, ada_cols), lambda t: (layer, 0, t)),
        pl.BlockSpec((DEPTH, ada_cols), lambda t: (0, t)),
        pl.BlockSpec((None, br_rows, D_MODEL), lambda t: (layer, t, 0)),
        pl.BlockSpec((None, br_rows, D_MODEL), lambda t: (layer,) + wo_map(t)),
        pl.BlockSpec((None, br_rows, prm["w_in16"].shape[-1]), lambda t: (layer,) + wo_map(t)),
    ]
    out_specs = [
        pl.BlockSpec((COND_ROWS, ada_cols), lambda t: (0, t)),
        pl.BlockSpec((br_rows, D_MODEL), lambda t: (t, 0)),
        pl.BlockSpec((br_rows, D_MODEL), wo_map),
        pl.BlockSpec((br_rows, N_BRANCH * D_MODEL), wo_map),
        pl.BlockSpec((br_rows, GATE_PAD), wo_map),
        pl.BlockSpec((D_BRANCH, br_rows), lambda t: wo_map(t)[::-1]),
    ]
    out_shape = [jax.ShapeDtypeStruct((COND_ROWS, 6 * D_MODEL), F32),
                 jax.ShapeDtypeStruct((N_BRANCH * D_BRANCH, D_MODEL), BF16),
                 jax.ShapeDtypeStruct((D_MODEL, D_MODEL), BF16),
                 jax.ShapeDtypeStruct((D_MODEL, N_BRANCH * D_MODEL), BF16),
                 jax.ShapeDtypeStruct((D_MODEL, GATE_PAD), BF16),
                 jax.ShapeDtypeStruct((D_BRANCH, D_MODEL), BF16)]
    return args, in_specs, out_specs, out_shape


def _first_layer_prep_call(prm):
    args, in_specs, out_specs, out_shape = _layer_prep_specs(0, prm)
    return pl.pallas_call(
        functools.partial(_prepare_layer, layer=0),
        grid=(N_PREP_STEPS,),
        in_specs=in_specs,
        out_specs=out_specs,
        out_shape=out_shape,
        compiler_params=pltpu.CompilerParams(
            dimension_semantics=("arbitrary",),
            vmem_limit_bytes=_vmem_request(list(zip(args, in_specs)) + list(zip(out_shape, out_specs)), [])),
        name="prep_layer0",
    )(*args)


def _seg_scan(x, pos, seg, total, op, reverse):
    d = 1
    while d < seg:
        if reverse:
            shifted = pltpu.roll(x, total - d, 1)
            valid = pos < seg - d
        else:
            shifted = pltpu.roll(x, d, 1)
            valid = pos >= d
        if op == "sum":
            x = x + jnp.where(valid, shifted, 0.0)
        else:
            x = jnp.maximum(x, jnp.where(valid, shifted, -jnp.inf))
        d *= 2
    return x


_MIXER_LAYER_ROW_INPUTS = (2, 10, 11, 14, 15)


def _mixer_kernel(*refs, T, grid2d, has_state, layer, n_in, n_pass, n_out):
    ins, outs, scr = list(refs[:n_in - n_pass]), refs[n_in:n_in + n_out], refs[n_in + n_out:]
    ins[1] = ins[1].at[pl.ds(1 + pl.program_id(0) if has_state else 0, 1), :]
    for k in _MIXER_LAYER_ROW_INPUTS:
        ins[k] = ins[k].at[pl.ds(layer, 1), :]
    state_outs = list(outs[1:])
    if state_outs and n_pass == 0:
        for k, o in enumerate(state_outs):
            o[...] = jnp.zeros(o.shape, o.dtype)
            state_outs[k] = o.at[layer]
    for thunk in _mixer_sequence(*ins, outs[0], *state_outs, *scr, T=T, grid2d=grid2d, has_state=has_state):
        thunk()


def _mixer_sequence(*refs, T, grid2d, has_state):
    it = iter(refs)
    x_ref = next(it)
    mod_ref = next(it)
    g1_ref = next(it)
    wmain_ref = next(it)
    wkt_ref = next(it)
    wgate_ref = next(it)
    wbrg_ref = next(it)
    wbranch_ref = next(it)
    wout_ref = next(it)
    wpool_ref = next(it)
    pscale_ref = next(it)
    gsgu_ref = next(it)
    wsgu_ref = next(it)
    bsgu_ref = next(it)
    bgates_ref = next(it)
    gml_ref = next(it)
    if has_state:
        c0_ref, n0_ref, m0_ref = next(it), next(it), next(it)
    out_ref = next(it)
    if not has_state:
        cout_ref, nout_ref, mout_ref = next(it), next(it), next(it)
    (u_scr, xp_scr, su_scr, vn_scr, q_scr, kt_scr, vaug_scr, o_scr, grow_scr, colf_scr,
     h_scr, ya_scr, yb_scr, yc_scr, mab_scr, gc_scr, merged_scr) = [next(it) for _ in range(17)]
    if has_state:
        cn_scr = next(it)
    if grid2d:
        m1_scr, dlt_scr = next(it), next(it)

    L = MLSTM_L
    n_tiles = T // TILE_M
    n_chunks = T // L
    RB = MLSTM_ROW_BLOCK
    NW = MERGE_COLS
    n_merge = D_MODEL // NW

    def tile_rows(i):
        return pl.ds(_aligned(i * TILE_M, TILE_M), TILE_M)

    def a_norm(i):
        mod = mod_ref[...]
        sh1 = mod[:, 0:D_MODEL]
        sc1 = mod[:, D_MODEL:2 * D_MODEL]
        x = x_ref[tile_rows(i), :]
        ms = jnp.mean(x * x, axis=-1, keepdims=True)
        u = x * lax.rsqrt(ms + EPS) * g1_ref[...] * (1.0 + sc1) + sh1
        u_scr[tile_rows(i), :] = u.astype(BF16)

    def a_gates(i):
        lane = lax.broadcasted_iota(jnp.int32, (2 * N_HEADS, TILE_M), 1)
        is_fwd = lax.broadcasted_iota(jnp.int32, (2 * N_HEADS, TILE_M), 0) < N_HEADS
        bias = jnp.concatenate([bgates_ref[...], jnp.zeros((1, GATE_PAD - N_GATE), F32)], axis=1)
        gate_t = (_dot(u_scr[tile_rows(i), :], wgate_ref[...]) + bias).T
        gp = gate_t[0:N_GATE, :]
        i_all = gp[0:2 * N_HEADS, :]
        f_all = gp[2 * N_HEADS:4 * N_HEADS, :]
        logf = jnp.minimum(f_all, 0.0) - jnp.log1p(jnp.exp(-jnp.abs(f_all)))
        b_row = jnp.where(is_fwd, _seg_scan(logf, lane, L, L, "sum", False),
                          _seg_scan(logf, lane, L, L, "sum", True))
        g_row = i_all - b_row
        cm_row = jnp.where(is_fwd, _seg_scan(g_row, lane, L, L, "max", False),
                           _seg_scan(g_row, lane, L, L, "max", True))
        grow_scr[:, tile_rows(i)] = g_row
        stacked = jnp.concatenate(
            [b_row, cm_row, jnp.zeros((GATE_PAD - 4 * N_HEADS, TILE_M), F32)], axis=0)
        colf_scr[tile_rows(i), :] = stacked.T

    def proj(i, k):
        return _dot(u_scr[tile_rows(i), :], wmain_ref[:, k * D_BRANCH:(k + 1) * D_BRANCH])

    def a_xp(i):
        xp_scr[pl.ds(_aligned(i * TILE_M + POOL_PAD, 8), TILE_M), :] = proj(i, 0)

    def a_su(i):
        su_scr[...] = proj(i, 1)

    def a_sv(i):
        sv = proj(i, 2)
        vn = sv * lax.rsqrt(jnp.mean(sv * sv, axis=-1, keepdims=True) + EPS) * gsgu_ref[...]
        vn_scr[...] = vn.astype(BF16)

    def a_q(i):
        q_scr[tile_rows(i), :] = proj(i, 3).astype(BF16)

    def a_k(i):
        kt = _dot_nt(wkt_ref[...], u_scr[tile_rows(i), :])
        kt_scr[:, tile_rows(i)] = (kt * (HEAD_DIM ** -0.5)).astype(BF16)

    def a_v(i):
        v = proj(i, 5).astype(BF16)
        ones_blk = jnp.ones((TILE_M, HEAD_DIM), BF16)
        for h in range(N_HEADS):
            vaug_scr[tile_rows(i), 2 * h * HEAD_DIM:(2 * h + 1) * HEAD_DIM] = v[:, h * HEAD_DIM:(h + 1) * HEAD_DIM]
            vaug_scr[tile_rows(i), (2 * h + 1) * HEAD_DIM:(2 * h + 2) * HEAD_DIM] = ones_blk

    def a_o(i):
        o_scr[tile_rows(i), :] = proj(i, 6)

    def a_sgu(ch, i):
        crow = slice(ch * SGU_CHUNK, (ch + 1) * SGU_CHUNK)
        for g in range(N_GROUPS):
            gc = slice(g * GROUP, (g + 1) * GROUP)
            mixed = _dot(wsgu_ref[g], vn_scr[crow, gc]) + bsgu_ref[:, gc]
            yb = su_scr[crow, gc] * mixed
            yb_scr[pl.ds(_aligned(i * TILE_M + ch * SGU_CHUNK, SGU_CHUNK), SGU_CHUNK), gc] = yb.astype(BF16)

    a_parts = [a_norm, a_gates, a_xp, a_su, a_sv, a_q, a_k, a_v, a_o]
    a_parts += [functools.partial(a_sgu, ch) for ch in range(TILE_M // SGU_CHUNK)]

    def pool_pad():
        zpad = jnp.zeros((POOL_PAD, D_BRANCH), F32)
        xp_scr[0:POOL_PAD, :] = zpad
        xp_scr[T + POOL_PAD:T + 2 * POOL_PAD, :] = zpad

    def pool_seq(g):
        win = POOL_WINDOWS[g]
        gc = slice(g * GROUP, (g + 1) * GROUP)
        half = win // 2
        t_idx = lax.broadcasted_iota(jnp.int32, (T, GROUP), 0)
        total = jnp.zeros((T, GROUP), F32)
        for j in range(-half, win - half):
            total = total + xp_scr[POOL_PAD + j:POOL_PAD + j + T, gc]
        cnt = (jnp.minimum(t_idx + (win - half), T) - jnp.maximum(t_idx - half, 0)).astype(F32)
        dlt = total / cnt - xp_scr[POOL_PAD:POOL_PAD + T, gc]
        ya = _dot(dlt.astype(BF16), wpool_ref[g]) * pscale_ref[...][:, gc]
        ya_scr[:, gc] = ya.astype(BF16)

    def a_colpool(i):
        c_idx = lax.broadcasted_iota(jnp.int32, (GRID_W, GROUP), 0)
        for g, win in enumerate(POOL_WINDOWS):
            gc = slice(g * GROUP, (g + 1) * GROUP)
            half = win // 2
            inv_cnt = 1.0 / (jnp.minimum(c_idx + (win - half), GRID_W) - jnp.maximum(c_idx - half, 0)).astype(F32)
            for r in range(TILE_M // GRID_W):
                r0 = i * TILE_M + r * GRID_W
                xg = xp_scr[pl.ds(_aligned(r0 + POOL_PAD, 8), GRID_W), gc]
                total = xg
                for j in range(-half, win - half):
                    if j != 0:
                        valid = (c_idx + j >= 0) & (c_idx + j < GRID_W)
                        total = total + jnp.where(valid, pltpu.roll(xg, (-j) % GRID_W, 0), 0.0)
                m1_scr[pl.ds(_aligned(r0, GRID_W), GRID_W), gc] = total * inv_cnt

    def pool_grid_rows(g):
        win = POOL_WINDOWS[g]
        gc = slice(g * GROUP, (g + 1) * GROUP)
        half = win // 2
        n_rows = T // GRID_W
        for r in range(n_rows):
            lo = max(r - half, 0)
            hi = min(r + (win - half), n_rows)
            acc = m1_scr[lo * GRID_W:(lo + 1) * GRID_W, gc]
            for rr in range(lo + 1, hi):
                acc = acc + m1_scr[rr * GRID_W:(rr + 1) * GRID_W, gc]
            xg = xp_scr[POOL_PAD + r * GRID_W:POOL_PAD + (r + 1) * GRID_W, gc]
            dlt_scr[r * GRID_W:(r + 1) * GRID_W, :] = acc / float(hi - lo) - xg
        ya = _dot(dlt_scr[...].astype(BF16), wpool_ref[g]) * pscale_ref[...][:, gc]
        ya_scr[:, gc] = ya.astype(BF16)

    if grid2d:
        assert TILE_M % GRID_W == 0
        a_parts.insert(a_parts.index(a_xp) + 1, a_colpool)
        pool_parts = [functools.partial(pool_grid_rows, g) for g in range(N_GROUPS)]
    else:
        pool_parts = [pool_pad] + [functools.partial(pool_seq, g) for g in range(N_GROUPS)]

    def keep_mask(d, rb, col_lo, ncols):
        t_loc = rb * RB + lax.broadcasted_iota(jnp.int32, (RB, ncols), 0)
        s_loc = col_lo + lax.broadcasted_iota(jnp.int32, (RB, ncols), 1)
        return s_loc <= t_loc if d == 0 else s_loc >= t_loc

    def mlstm_prepare(d, c, m_in):
        c0 = _aligned(c * L, L)
        colf = colf_scr[pl.ds(c0, L), :]
        bcol = colf[:, N_HEADS * d:N_HEADS * (d + 1)]
        cmcol = colf[:, 2 * N_HEADS + N_HEADS * d:2 * N_HEADS + N_HEADS * (d + 1)]
        mx = jnp.maximum(m_in, cmcol)
        last = L - 1 if d == 0 else 0
        mx_last = mx[last:last + 1, :]
        return dict(c0=c0, mx=mx, mx_last=mx_last, w_inter=jnp.exp(m_in - mx), e_negm=jnp.exp(-(bcol + mx)),
                    decay=jnp.exp(m_in - mx_last), m_new=bcol[last:last + 1, :] + mx_last)

    def block_cols(d, rb):
        return (0, (rb + 1) * RB) if d == 0 else (rb * RB, L)

    def mlstm_scores(d, p, h):
        c0 = p["c0"]
        hc = slice(h * HEAD_DIM, (h + 1) * HEAD_DIM)
        qks = []
        for rb in range(L // RB):
            col_lo, col_hi = block_cols(d, rb)
            qb = q_scr[pl.ds(_aligned(c0 + rb * RB, RB), RB), hc]
            qks.append(_dot(qb, kt_scr[hc, pl.ds(_aligned(c0 + col_lo, RB), col_hi - col_lo)]))
        p["qk", h] = qks

    def mlstm_head(d, p, h):
        c0, mx, mx_last = p["c0"], p["mx"], p["mx_last"]
        rows = pl.ds(c0, L)
        hc = slice(h * HEAD_DIM, (h + 1) * HEAD_DIM)
        ac = slice(2 * h * HEAD_DIM, (2 * h + 2) * HEAD_DIM)
        grow = grow_scr[N_HEADS * d + h:N_HEADS * d + h + 1, rows]
        if has_state:
            cn_dh = cn_scr[d, h]
            cn_bf = cn_dh.astype(BF16)
        for rb in range(L // RB):
            col_lo, col_hi = block_cols(d, rb)
            ncols = col_hi - col_lo
            trow = slice(rb * RB, (rb + 1) * RB)
            rrows = pl.ds(_aligned(c0 + rb * RB, RB), RB)
            crows = pl.ds(_aligned(c0 + col_lo, RB), ncols)
            qb = q_scr[rrows, hc]
            qk = p["qk", h][rb]
            w = jnp.exp(grow[:, col_lo:col_hi] - mx[trow, h:h + 1])
            s = (qk * jnp.where(keep_mask(d, rb, col_lo, ncols), w, 0.0)).astype(BF16)
            na = _dot(s, vaug_scr[crows, ac])
            if has_state:
                na = na + p["w_inter"][trow, h:h + 1] * _dot(qb, cn_bf)
            hval = na[:, 0:HEAD_DIM] / jnp.maximum(jnp.abs(na[:, HEAD_DIM:]), p["e_negm"][trow, h:h + 1])
            if d == 0:
                h_scr[rrows, hc] = hval
            else:
                h_scr[rrows, hc] += hval
        w_state = jnp.exp(grow - mx_last[:, h:h + 1])
        kw_t = (kt_scr[hc, rows].astype(F32) * w_state).astype(BF16)
        upd = _dot(kw_t, vaug_scr[rows, ac])
        if has_state:
            cn_scr[d, h] = p["decay"][:, h:h + 1] * cn_dh + upd
        else:
            cout_ref[d, h] = upd[:, 0:HEAD_DIM]
            nout_ref[d, h:h + 1, :] = upd[:, HEAD_DIM:].T[0:1, :]

    def merge_ab(c, nj):
        nc = slice(nj * NW, (nj + 1) * NW)
        ub = u_scr[tile_rows(c), :]
        acc = None
        for r, y_scr in enumerate((ya_scr, yb_scr)):
            gate = jax.nn.sigmoid(_dot(ub, wbrg_ref[:, r * D_MODEL + nj * NW:r * D_MODEL + (nj + 1) * NW]))
            term = gate * _dot(y_scr[tile_rows(c), :], wbranch_ref[r * D_BRANCH:(r + 1) * D_BRANCH, nc])
            acc = term if acc is None else acc + term
        mab_scr[tile_rows(c), nc] = acc

    def gate_c(c, nj):
        w = wbrg_ref[:, 2 * D_MODEL + nj * NW:2 * D_MODEL + (nj + 1) * NW]
        gc_scr[:, nj * NW:(nj + 1) * NW] = jax.nn.sigmoid(_dot(u_scr[tile_rows(c), :], w))

    def tail_yc(c):
        gml = gml_ref[...]
        for h in range(N_HEADS):
            hc = slice(h * HEAD_DIM, (h + 1) * HEAD_DIM)
            hh = h_scr[tile_rows(c), hc]
            hn = hh * lax.rsqrt(jnp.mean(hh * hh, axis=-1, keepdims=True) + EPS) * gml[:, hc]
            yc_scr[:, hc] = (jax.nn.sigmoid(o_scr[tile_rows(c), hc]) * hn).astype(BF16)

    def tail_merge(c, nj):
        nc = slice(nj * NW, (nj + 1) * NW)
        acc = mab_scr[tile_rows(c), nc] + gc_scr[:, nc] * _dot(yc_scr[...], wbranch_ref[2 * D_BRANCH:, nc])
        merged_scr[:, nc] = acc.astype(BF16)

    def tail_out(c):
        gt1 = mod_ref[...][:, 2 * D_MODEL:3 * D_MODEL]
        out_ref[tile_rows(c), :] = x_ref[tile_rows(c), :] + gt1 * _dot(merged_scr[...], wout_ref[...])

    def tail_parts(c):
        return ([functools.partial(tail_yc, c)] + [functools.partial(tail_merge, c, nj) for nj in range(n_merge)]
                + [functools.partial(tail_out, c)])

    def scan_parts(d, c, m_in, fillers, out):
        p = {}

        def prepare():
            p.update(mlstm_prepare(d, c, m_in))
            out[d] = p["m_new"]

        parts = [prepare]
        for h in range(N_HEADS):
            parts.append(functools.partial(mlstm_scores, d, p, h))
            if h < len(fillers):
                parts.append(fillers[h])
            parts.append(functools.partial(mlstm_head, d, p, h))
        return parts + list(fillers[N_HEADS:])

    if n_tiles == 1 and n_chunks == 1:
        zeros = jnp.zeros((1, N_HEADS), F32)
        m_out = {}
        parts = [functools.partial(part, 0) for part in a_parts] + pool_parts
        parts += scan_parts(0, 0, zeros, [functools.partial(merge_ab, 0, nj) for nj in range(n_merge)], m_out)
        parts += scan_parts(1, 0, zeros, [functools.partial(gate_c, 0, nj) for nj in range(n_merge)], m_out)
        parts += tail_parts(0)

        def write_m():
            mout_ref[0:1, :] = m_out[0]
            mout_ref[1:2, :] = m_out[1]

        return parts + [write_m]

    def run_loops():
        def phase_a(i, carry):
            for part in a_parts:
                part(i)
            return carry

        lax.fori_loop(0, n_tiles, phase_a, 0)
        for part in pool_parts:
            part()
        for d in range(2):
            for h in range(N_HEADS):
                n_rows = jnp.broadcast_to(n0_ref[d, h:h + 1, :], (HEAD_DIM, HEAD_DIM))
                cn_scr[d, h] = jnp.concatenate([c0_ref[d, h], n_rows.T], axis=1)
        m_init = m0_ref[...]

        def fwd_step(j, m_in):
            m_out = {}
            for part in scan_parts(0, j, m_in, [functools.partial(merge_ab, j, nj) for nj in range(n_merge)], m_out):
                part()
            return m_out[0]

        def bwd_step(j, m_in):
            c = n_chunks - 1 - j
            m_out = {}
            for part in scan_parts(1, c, m_in, [functools.partial(gate_c, c, nj) for nj in range(n_merge)], m_out):
                part()
            for part in tail_parts(c):
                part()
            return m_out[1]

        lax.fori_loop(0, n_chunks, fwd_step, m_init[0:1, :])
        lax.fori_loop(0, n_chunks, bwd_step, m_init[1:2, :])

    assert has_state, "multi-chunk sequences carry the cached state"
    return [run_loops]


def _mixer_call(l, xall, lay, prm, stream, states=None, state_outs=None):
    ctx = stream == "ctx"
    T = SEQ if ctx else DEC_SEQ
    assert T % TILE_M == 0 and TILE_M == MLSTM_L and MLSTM_L % MLSTM_ROW_BLOCK == 0
    n_b = BATCH if ctx else DEC_BATCH
    own_array = xall.shape[0] != N_TOK
    blk0 = 0 if ctx or own_array else N_TOK_CTX // T
    once = pl.Buffered(1)

    def wspec(shape):
        nd = len(shape)
        return pl.BlockSpec((None,) + shape, lambda b: (l,) + (0,) * nd, pipeline_mode=once)

    def lspec(shape):
        return pl.BlockSpec(shape, lambda b: (0,) * len(shape), pipeline_mode=once)

    x_spec = pl.BlockSpec((T, D_MODEL), lambda b: (blk0 + b, 0))
    in_specs = [
        x_spec,
        lspec((COND_ROWS, 6 * D_MODEL)),
        lspec((DEPTH, D_MODEL)),
        wspec((D_MODEL, N_MAIN)),
        lspec((D_BRANCH, D_MODEL)),
        lspec((D_MODEL, GATE_PAD)),
        lspec((D_MODEL, N_BRANCH * D_MODEL)),
        lspec((N_BRANCH * D_BRANCH, D_MODEL)),
        lspec((D_MODEL, D_MODEL)),
        wspec((N_GROUPS, GROUP, GROUP)),
        lspec((DEPTH, D_BRANCH)),
        lspec((DEPTH, D_BRANCH)),
        wspec((N_GROUPS, SGU_CHUNK, SGU_CHUNK)),
        wspec((SGU_CHUNK, D_BRANCH)),
        lspec((DEPTH, N_GATE)),
        lspec((DEPTH, D_BRANCH)),
    ]
    args = [xall, lay["mods"], prm["g_norm1"], prm["w_in16"], lay["w_kt"],
            lay["w_gate"], lay["w_brg"], lay["w_branch"], lay["w_out"],
            prm["w_pool"], prm["pool_scale"], prm["g_sgu"], prm["w_sgu"],
            prm["b_sgu_tile"], prm["b_gates"], prm["g_mlstm"]]
    out_shape = [jax.ShapeDtypeStruct(xall.shape, xall.dtype)]
    out_specs = [pl.BlockSpec((T, D_MODEL), lambda b: (blk0 + b, 0), pipeline_mode=None if ctx else once)]
    aliases = {} if own_array else {0: 0}
    passthrough = []
    if ctx:
        per_seq = [(2, N_HEADS, HEAD_DIM, HEAD_DIM), (2, N_HEADS, HEAD_DIM), (2, N_HEADS)]
        out_shape += [jax.ShapeDtypeStruct((BATCH, DEPTH) + s, F32) for s in per_seq]
        if state_outs is None:
            out_specs += [pl.BlockSpec((None, DEPTH) + s, lambda b, nd=len(s): (b,) + (0,) * (nd + 1))
                          for s in per_seq]
        else:
            out_specs += [pl.BlockSpec((None, None) + s, lambda b, nd=len(s): (b, l) + (0,) * nd) for s in per_seq]
            passthrough += [(a, 1 + k) for k, a in enumerate(state_outs)]
    else:
        in_specs += [
            pl.BlockSpec((None, None, 2, N_HEADS, HEAD_DIM, HEAD_DIM), lambda b: (b, l, 0, 0, 0, 0)),
            pl.BlockSpec((None, None, 2, N_HEADS, HEAD_DIM), lambda b: (b, l, 0, 0, 0)),
            pl.BlockSpec((None, None, 2, N_HEADS), lambda b: (b, l, 0, 0)),
        ]
        args += list(states)
    for buf, out_idx in passthrough:
        aliases[len(args)] = out_idx
        in_specs.append(pl.BlockSpec(memory_space=pl.ANY))
        args.append(buf)

    scratch = [
        pltpu.VMEM((T, D_MODEL), BF16),
        pltpu.VMEM((T + 2 * POOL_PAD, D_BRANCH), F32),
        pltpu.VMEM((TILE_M, D_BRANCH), F32),
        pltpu.VMEM((TILE_M, D_BRANCH), BF16),
        pltpu.VMEM((T, D_BRANCH), BF16),
        pltpu.VMEM((D_BRANCH, T), BF16),
        pltpu.VMEM((T, 2 * D_BRANCH), BF16),
        pltpu.VMEM((T, D_BRANCH), F32),
        pltpu.VMEM((2 * N_HEADS, T), F32),
        pltpu.VMEM((T, GATE_PAD), F32),
        pltpu.VMEM((T, D_BRANCH), F32),
        pltpu.VMEM((T, D_BRANCH), BF16),
        pltpu.VMEM((T, D_BRANCH), BF16),
        pltpu.VMEM((TILE_M, D_BRANCH), BF16),
        pltpu.VMEM((T, D_MODEL), F32),
        pltpu.VMEM((TILE_M, D_MODEL), F32),
        pltpu.VMEM((TILE_M, D_MODEL), BF16),
    ]
    if not ctx:
        scratch += [pltpu.VMEM((2, N_HEADS, HEAD_DIM, 2 * HEAD_DIM), F32)]
        scratch += [pltpu.VMEM((T, D_BRANCH), F32), pltpu.VMEM((T, GROUP), F32)]

    outs = pl.pallas_call(
        functools.partial(_mixer_kernel, T=T, grid2d=not ctx, has_state=not ctx, layer=l,
                          n_in=len(in_specs), n_pass=len(passthrough), n_out=len(out_specs)),
        grid=(n_b,),
        in_specs=in_specs,
        out_specs=out_specs,
        out_shape=out_shape,
        scratch_shapes=scratch,
        input_output_aliases=aliases,
        compiler_params=pltpu.CompilerParams(
            dimension_semantics=("arbitrary",),
            vmem_limit_bytes=_vmem_request(list(zip(args, in_specs)) + list(zip(out_shape, out_specs)), scratch)),
        name="mixer_ctx" if ctx else "mixer_lat",
    )(*args)
    return outs


def _ffn_kernel(*refs, layer, final, split_input):
    n_x = 2 if split_input else 1
    x_refs, (mod_ref, g2_ref, w1_hbm, w2_hbm, gfin_ref), rest = refs[:n_x], refs[n_x:n_x + 5], refs[n_x + 5:]
    if final:
        out_refs, rest = rest[:2], rest[2:]
    else:
        prep_in, rest = rest[:N_PREP_OPERANDS], rest[N_PREP_OPERANDS:]
        out_refs, prep_out, rest = rest[:1], rest[1:1 + len(LAYER_ARRAYS)], rest[1 + len(LAYER_ARRAYS):]
    acc_scr, h_scr, w1_scr, w2_scr, st1_scr, st2_scr, sem = rest[:7]
    n_chunks = D_FF // FFN_CHUNK
    tile = pl.program_id(0)
    first_tile = tile == 0
    ctx_tiles, lat_tiles = N_TOK_CTX // FFN_TILE_M, DEC_SEQ // FFN_TILE_M
    if split_input:
        x_ref = rest[7]

        @pl.when(tile < ctx_tiles)
        def _():
            x_ref[...] = x_refs[0][...]

        @pl.when(tile >= ctx_tiles)
        def _():
            x_ref[...] = x_refs[1][...]
    else:
        x_ref = x_refs[0]
    mod_ref = mod_ref.at[pl.ds(jnp.where(tile < ctx_tiles, 0, 1 + (tile - ctx_tiles) // lat_tiles), 1), :]
    g2_ref = g2_ref.at[pl.ds(layer, 1), :]

    def w1_copy(c):
        slot = c % FFN_STAGES
        c0 = _aligned(c * FFN_CHUNK, FFN_CHUNK)
        return pltpu.make_async_copy(w1_hbm.at[layer, :, pl.ds(c0, FFN_CHUNK)], st1_scr.at[slot], sem.at[0, slot])

    def w2_copy(c):
        slot = c % FFN_STAGES
        c0 = _aligned(c * FFN_CHUNK, FFN_CHUNK)
        return pltpu.make_async_copy(w2_hbm.at[layer, pl.ds(c0, FFN_CHUNK), :], st2_scr.at[slot], sem.at[1, slot])

    def land_w1(c):
        w1_copy(c).wait()
        c0 = _aligned(c * FFN_CHUNK, FFN_CHUNK)
        w1_scr[:, pl.ds(c0, FFN_CHUNK)] = st1_scr[c % FFN_STAGES].astype(BF16)

    def land_w2(c):
        w2_copy(c).wait()
        c0 = _aligned(c * FFN_CHUNK, FFN_CHUNK)
        w2_scr[pl.ds(c0, FFN_CHUNK), :] = st2_scr[c % FFN_STAGES].astype(BF16)

    def prepare_next_layer():
        if not final:
            _prepare_layer(*prep_in, *prep_out, layer=layer + 1)

    def normed_input():
        mod = mod_ref[...]
        sh2 = mod[:, 3 * D_MODEL:4 * D_MODEL]
        sc2 = mod[:, 4 * D_MODEL:5 * D_MODEL]
        x = x_ref[...]
        ms = jnp.mean(x * x, axis=-1, keepdims=True)
        return (x * lax.rsqrt(ms + EPS) * g2_ref[...] * (1.0 + sc2) + sh2).astype(BF16)

    def hidden(ub, c):
        c0 = _aligned(c * FFN_CHUNK, FFN_CHUNK)
        h = _dot(ub, w1_scr[:, pl.ds(c0, FFN_CHUNK)])
        h_scr[:, pl.ds(c0, FFN_CHUNK)] = jnp.square(jnp.maximum(h, 0.0)).astype(BF16)

    def finish(ffn_out):
        gt2 = mod_ref[...][:, 5 * D_MODEL:6 * D_MODEL]
        y = x_ref[...] + gt2 * ffn_out
        if not final:
            out_refs[0][...] = y
            return
        y = y * lax.rsqrt(jnp.mean(y * y, axis=-1, keepdims=True) + EPS) * gfin_ref[...]
        in_ctx = tile < ctx_tiles

        @pl.when(in_ctx)
        def _():
            out_refs[0][...] = y

        @pl.when(jnp.logical_not(in_ctx))
        def _():
            out_refs[1][...] = y

    @pl.when(first_tile)
    def _():
        for c in range(FFN_STAGES):
            w1_copy(c).start()
            w2_copy(c).start()
        prepare_next_layer()
        ub = normed_input()
        acc_scr[...] = jnp.zeros_like(acc_scr)

        def chunk(c, carry):
            land_w1(c)
            land_w2(c)

            @pl.when(c + FFN_STAGES < n_chunks)
            def _():
                w1_copy(c + FFN_STAGES).start()
                w2_copy(c + FFN_STAGES).start()

            hidden(ub, c)
            c0 = _aligned(c * FFN_CHUNK, FFN_CHUNK)
            acc_scr[...] += _dot(h_scr[:, pl.ds(c0, FFN_CHUNK)], w2_scr[pl.ds(c0, FFN_CHUNK), :])
            return carry

        lax.fori_loop(0, n_chunks, chunk, 0)
        finish(acc_scr[...])

    @pl.when(jnp.logical_not(first_tile))
    def _():
        prepare_next_layer()
        ub = normed_input()
        for c in range(n_chunks):
            hidden(ub, c)
        finish(_dot(h_scr[...], w2_scr[...]))


def _ffn_call(l, xall, lay, prm, g_final, final):
    n_tiles = N_TOK // FFN_TILE_M
    ctx_tiles = N_TOK_CTX // FFN_TILE_M
    once = pl.Buffered(1)
    split_input = isinstance(xall, tuple)
    assert not (split_input and final)

    x_spec = pl.BlockSpec((FFN_TILE_M, D_MODEL), lambda t: (t, 0))
    ctx_spec = pl.BlockSpec((FFN_TILE_M, D_MODEL), lambda t: (jnp.minimum(t, ctx_tiles - 1), 0))
    lat_spec = pl.BlockSpec((FFN_TILE_M, D_MODEL), lambda t: (jnp.maximum(t - ctx_tiles, 0), 0))
    xs = list(xall) if split_input else [xall]
    in_specs = ([ctx_spec, lat_spec] if split_input else [x_spec]) + [
        pl.BlockSpec((COND_ROWS, 6 * D_MODEL), lambda t: (0, 0), pipeline_mode=once),
        pl.BlockSpec((DEPTH, D_MODEL), lambda t: (0, 0), pipeline_mode=once),
        pl.BlockSpec(memory_space=pl.ANY),
        pl.BlockSpec(memory_space=pl.ANY),
        pl.BlockSpec((1, D_MODEL), lambda t: (0, 0)),
    ]
    args = xs + [lay["mods"], prm["g_norm2"], prm["w_ff1"], prm["w_ff2"], g_final.reshape(1, D_MODEL)]
    if final:
        out_specs = [ctx_spec, lat_spec]
        out_shape = [jax.ShapeDtypeStruct((N_TOK_CTX, D_MODEL), F32),
                     jax.ShapeDtypeStruct((N_TOK - N_TOK_CTX, D_MODEL), F32)]
        aliases = {}
    else:
        assert n_tiles == N_PREP_STEPS
        prep_args, prep_in_specs, prep_out_specs, prep_out_shape = _layer_prep_specs(l + 1, prm)
        in_specs += prep_in_specs
        args += prep_args
        out_specs = [x_spec] + prep_out_specs
        out_shape = [jax.ShapeDtypeStruct((N_TOK, D_MODEL), F32)] + prep_out_shape
        aliases = {} if split_input else {0: 0}
    sems = [pltpu.SemaphoreType.DMA((2, FFN_STAGES))]
    gather = [pltpu.VMEM((FFN_TILE_M, D_MODEL), F32)] if split_input else []
    scratch = [
        pltpu.VMEM((FFN_TILE_M, D_MODEL), F32),
        pltpu.VMEM((FFN_TILE_M, D_FF), BF16),
        pltpu.VMEM((D_MODEL, D_FF), BF16),
        pltpu.VMEM((D_FF, D_MODEL), BF16),
        pltpu.VMEM((FFN_STAGES, D_MODEL, FFN_CHUNK), F32),
        pltpu.VMEM((FFN_STAGES, FFN_CHUNK, D_MODEL), F32),
    ]
    return pl.pallas_call(
        functools.partial(_ffn_kernel, layer=l, final=final, split_input=split_input),
        grid=(n_tiles,),
        in_specs=in_specs,
        out_specs=out_specs,
        out_shape=out_shape,
        scratch_shapes=scratch + sems + gather,
        input_output_aliases=aliases,
        compiler_params=pltpu.CompilerParams(
            dimension_semantics=("arbitrary",),
            vmem_limit_bytes=_vmem_request(list(zip(args, in_specs)) + list(zip(out_shape, out_specs)),
                                           scratch + gather)),
        name="ffn_final" if final else "ffn",
    )(*args)


def kernel(x_prompt, x_sample, state_C, state_n, state_m, c, c_ctx, w_ada, b_ada, g_norm1, g_norm2, w_in,
           b_gates, w_pool, pool_scale, g_sgu, w_sgu, b_sgu, g_mlstm, w_branch, w_out, w_ff1, w_ff2, g_final):
    w_in16 = w_in.astype(BF16)
    prm = {
        "w_in16": w_in16,
        "w_branch_rows": w_branch.reshape(DEPTH, N_BRANCH * D_BRANCH, D_MODEL),
        "w_out": w_out,
        "w_ada": w_ada,
        "b_ada": b_ada,
        "w_pool": w_pool.astype(BF16),
        "w_sgu": w_sgu.astype(BF16),
        "w_ff1": w_ff1,
        "w_ff2": w_ff2,
        "g_norm1": g_norm1,
        "g_norm2": g_norm2,
        "pool_scale": pool_scale,
        "g_sgu": g_sgu,
        "g_mlstm": g_mlstm,
        "b_gates": b_gates,
        "b_sgu_tile": jnp.repeat(jnp.swapaxes(b_sgu, 1, 2), GROUP, axis=2),
        "cond8": jnp.concatenate(
            [c_ctx[None, :], c, jnp.zeros((COND_ROWS - 1 - DEC_BATCH, D_MODEL), F32)], axis=0),
    }
    lay = dict(zip(LAYER_ARRAYS, _first_layer_prep_call(prm)))

    xc, *new_states = _mixer_call(0, x_prompt.reshape(N_TOK_CTX, D_MODEL), lay, prm, "ctx")
    (xs,) = _mixer_call(0, x_sample.reshape(-1, D_MODEL), lay, prm, "lat", states=(state_C, state_n, state_m))
    xall = (xc, xs)
    for l in range(DEPTH):
        if l > 0:
            xall, *new_states = _mixer_call(l, xall, lay, prm, "ctx", state_outs=new_states)
            (xall,) = _mixer_call(l, xall, lay, prm, "lat", states=(state_C, state_n, state_m))
        if l < DEPTH - 1:
            xall, *nxt = _ffn_call(l, xall, lay, prm, g_final, final=False)
            lay = dict(zip(LAYER_ARRAYS, nxt))
    y_prompt, y_sample = _ffn_call(DEPTH - 1, xall, lay, prm, g_final, final=True)
    return (y_prompt.reshape(BATCH, SEQ, D_MODEL), y_sample.reshape(DEC_BATCH, DEC_SEQ, D_MODEL), *new_states)
```

```python
import functools
import math

import jax
import jax.numpy as jnp
from jax import lax
from jax.experimental import pallas as pl
from jax.experimental.pallas import tpu as pltpu

F32 = jnp.float32
BF16 = jnp.bfloat16

D_MODEL = 1024
DEPTH = 4
BATCH = 16
SEQ = 256
DEC_BATCH = 2
DEC_SEQ = 1024
GRID_W = 64
D_BRANCH = 512
POOL_WINDOWS = (2, 4, 8, 16)
N_GROUPS = 4
GROUP = D_BRANCH // N_GROUPS
SGU_CHUNK = 128
N_HEADS = 4
HEAD_DIM = D_BRANCH // N_HEADS
N_BRANCH = 3
D_FF = 4 * D_MODEL
EPS = 1e-6

LANES = 128
BF16_SUBLANES = 16

N_MAIN = 7 * D_BRANCH
N_GATE = 4 * N_HEADS
GATE_PAD = LANES
N_TOK_CTX = BATCH * SEQ
N_TOK = N_TOK_CTX + DEC_BATCH * DEC_SEQ
COND_ROWS = 8

TILE_M = 256
MLSTM_L = 256
MLSTM_ROW_BLOCK = 128
MERGE_COLS = 256
POOL_PAD = 8
FFN_TILE_M = 512
FFN_CHUNK = 512
FFN_STAGES = 2
V7X_VMEM_BYTES = 64 * 1024 * 1024
VMEM_COMPILER_RESERVE = 6 * 1024 * 1024
VMEM_CALL_FLOOR = 60 * 1024 * 1024


def _vmem_request(windows, scratch):
    total = VMEM_COMPILER_RESERVE
    for arr, spec in windows:
        if spec.block_shape is not None:
            n_buf = 2 if spec.pipeline_mode is None else spec.pipeline_mode.buffer_count
            total += n_buf * math.prod(d for d in spec.block_shape if d is not None) * jnp.dtype(arr.dtype).itemsize
    for buf in scratch:
        total += math.prod(buf.shape) * jnp.dtype(buf.dtype).itemsize
    return min(max(total, VMEM_CALL_FLOOR), V7X_VMEM_BYTES)


_dot = functools.partial(jnp.dot, preferred_element_type=F32)
_dot_nt = functools.partial(lax.dot_general, dimension_numbers=(((1,), (1,)), ((), ())),
                            preferred_element_type=F32)


def _aligned(offset, align):
    return offset if isinstance(offset, int) else pl.multiple_of(offset, align)


LAYER_ARRAYS = ("mods", "w_branch", "w_out", "w_brg", "w_gate", "w_kt")
N_PREP_OPERANDS = 6
N_PREP_STEPS = N_TOK // FFN_TILE_M


def _prepare_layer(cond_ref, wada_ref, bada_ref, wbr_ref, wo_ref, win_ref,
                   mods_out, wbr_out, wo_out, wbrg_out, wgate_out, wkt_out, *, layer):
    cond = cond_ref[...]
    s = (cond * jax.nn.sigmoid(cond)).astype(BF16)
    mods_out[...] = _dot(s, wada_ref[...].astype(BF16)) + bada_ref[layer:layer + 1, :]
    wbr_out[...] = wbr_ref[...].astype(BF16)
    wo_out[...] = wo_ref[...].astype(BF16)
    n_br = N_BRANCH * D_MODEL
    for c0 in range(0, n_br, FFN_CHUNK):
        width = min(FFN_CHUNK + GATE_PAD, n_br + N_GATE - c0)
        window = win_ref[:, N_MAIN + c0:N_MAIN + c0 + width]
        wbrg_out[:, c0:c0 + FFN_CHUNK] = window[:, N_GATE:N_GATE + FFN_CHUNK]
    gate_tile = win_ref[:, N_MAIN:N_MAIN + GATE_PAD]
    is_gate = lax.broadcasted_iota(jnp.int32, gate_tile.shape, 1) < N_GATE
    wgate_out[...] = jnp.where(is_gate, gate_tile, jnp.zeros_like(gate_tile))
    wkt_out[...] = win_ref[:, 4 * D_BRANCH:5 * D_BRANCH].astype(F32).T.astype(BF16)


def _layer_prep_specs(layer, prm):
    ada_cols = 6 * D_MODEL // N_PREP_STEPS
    br_rows = N_BRANCH * D_BRANCH // N_PREP_STEPS
    wo_steps = D_MODEL // br_rows
    assert ada_cols % LANES == 0 and br_rows % BF16_SUBLANES == 0 and wo_steps <= N_PREP_STEPS
    wo_map = lambda t: (jnp.minimum(t, wo_steps - 1), 0)
    args = [prm["cond8"], prm["w_ada"], prm["b_ada"], prm["w_branch_rows"], prm["w_out"], prm["w_in16"]]
    in_specs = [
        pl.BlockSpec((COND_ROWS, D_MODEL), lambda t: (0, 0)),
        pl.BlockSpec((None, D_MODEL, ada_cols), lambda t: (layer, 0, t)),
        pl.BlockSpec((DEPTH, ada_cols), lambda t: (0, t)),
        pl.BlockSpec((None, br_rows, D_MODEL), lambda t: (layer, t, 0)),
        pl.BlockSpec((None, br_rows, D_MODEL), lambda t: (layer,) + wo_map(t)),
        pl.BlockSpec((None, br_rows, prm["w_in16"].shape[-1]), lambda t: (layer,) + wo_map(t)),
    ]
    out_specs = [
        pl.BlockSpec((COND_ROWS, ada_cols), lambda t: (0, t)),
        pl.BlockSpec((br_rows, D_MODEL), lambda t: (t, 0)),
        pl.BlockSpec((br_rows, D_MODEL), wo_map),
        pl.BlockSpec((br_rows, N_BRANCH * D_MODEL), wo_map),
        pl.BlockSpec((br_rows, GATE_PAD), wo_map),
        pl.BlockSpec((D_BRANCH, br_rows), lambda t: wo_map(t)[::-1]),
    ]
    out_shape = [jax.ShapeDtypeStruct((COND_ROWS, 6 * D_MODEL), F32),
                 jax.ShapeDtypeStruct((N_BRANCH * D_BRANCH, D_MODEL), BF16),
                 jax.ShapeDtypeStruct((D_MODEL, D_MODEL), BF16),
                 jax.ShapeDtypeStruct((D_MODEL, N_BRANCH * D_MODEL), BF16),
                 jax.ShapeDtypeStruct((D_MODEL, GATE_PAD), BF16),
                 jax.ShapeDtypeStruct((D_BRANCH, D_MODEL), BF16)]
    return args, in_specs, out_specs, out_shape


def _first_layer_prep_call(prm):
    args, in_specs, out_specs, out_shape = _layer_prep_specs(0, prm)
    return pl.pallas_call(
        functools.partial(_prepare_layer, layer=0),
        grid=(N_PREP_STEPS,),
        in_specs=in_specs,
        out_specs=out_specs,
        out_shape=out_shape,
        compiler_params=pltpu.CompilerParams(
            dimension_semantics=("arbitrary",),
            vmem_limit_bytes=_vmem_request(list(zip(args, in_specs)) + list(zip(out_shape, out_specs)), [])),
        name="prep_layer0",
    )(*args)


def _seg_scan(x, pos, seg, total, op, reverse):
    d = 1
    while d < seg:
        if reverse:
            shifted = pltpu.roll(x, total - d, 1)
            valid = pos < seg - d
        else:
            shifted = pltpu.roll(x, d, 1)
            valid = pos >= d
        if op == "sum":
            x = x + jnp.where(valid, shifted, 0.0)
        else:
            x = jnp.maximum(x, jnp.where(valid, shifted, -jnp.inf))
        d *= 2
    return x


_MIXER_LAYER_ROW_INPUTS = (2, 10, 11, 14, 15)
_MIXER_LATE_INPUTS = (6, 7, 8)


def _mixer_kernel(*refs, T, grid2d, has_state, layer, n_in, n_pass, n_out):
    ins, outs, scr = list(refs[:n_in - n_pass]), refs[n_in:n_in + n_out], refs[n_in + n_out:]
    *scr, late_sem = scr
    scr, late_bufs = scr[:-len(_MIXER_LATE_INPUTS)], scr[-len(_MIXER_LATE_INPUTS):]
    late_copies = [pltpu.make_async_copy(ins[k], buf, late_sem.at[n])
                   for n, (k, buf) in enumerate(zip(_MIXER_LATE_INPUTS, late_bufs))]
    first_step = pl.program_id(0) == 0

    @pl.when(first_step)
    def _():
        for cp in late_copies:
            cp.start()

    def wait_late():
        @pl.when(first_step)
        def _():
            for cp in late_copies:
                cp.wait()

    for k, buf in zip(_MIXER_LATE_INPUTS, late_bufs):
        ins[k] = buf
    ins[1] = ins[1].at[pl.ds(1 + pl.program_id(0) if has_state else 0, 1), :]
    for k in _MIXER_LAYER_ROW_INPUTS:
        ins[k] = ins[k].at[pl.ds(layer, 1), :]
    state_outs = list(outs[1:])
    if state_outs and n_pass == 0:
        for k, o in enumerate(state_outs):
            o[...] = jnp.zeros(o.shape, o.dtype)
            state_outs[k] = o.at[layer]
    for thunk in _mixer_sequence(*ins, outs[0], *state_outs, *scr, T=T, grid2d=grid2d, has_state=has_state,
                                 wait_late=wait_late):
        thunk()


def _mixer_sequence(*refs, T, grid2d, has_state, wait_late):
    it = iter(refs)
    x_ref = next(it)
    mod_ref = next(it)
    g1_ref = next(it)
    wmain_ref = next(it)
    wkt_ref = next(it)
    wgate_ref = next(it)
    wbrg_ref = next(it)
    wbranch_ref = next(it)
    wout_ref = next(it)
    wpool_ref = next(it)
    pscale_ref = next(it)
    gsgu_ref = next(it)
    wsgu_ref = next(it)
    bsgu_ref = next(it)
    bgates_ref = next(it)
    gml_ref = next(it)
    if has_state:
        c0_ref, n0_ref, m0_ref = next(it), next(it), next(it)
    out_ref = next(it)
    if not has_state:
        cout_ref, nout_ref, mout_ref = next(it), next(it), next(it)
    (u_scr, xp_scr, su_scr, vn_scr, q_scr, kt_scr, vaug_scr, o_scr, grow_scr, colf_scr,
     h_scr, ya_scr, yb_scr, yc_scr, mab_scr, gc_scr, merged_scr) = [next(it) for _ in range(17)]
    if has_state:
        cn_scr = next(it)
    if grid2d:
        m1_scr, dlt_scr = next(it), next(it)

    L = MLSTM_L
    n_tiles = T // TILE_M
    n_chunks = T // L
    RB = MLSTM_ROW_BLOCK
    NW = MERGE_COLS
    n_merge = D_MODEL // NW

    def tile_rows(i):
        return pl.ds(_aligned(i * TILE_M, TILE_M), TILE_M)

    def a_norm(i):
        mod = mod_ref[...]
        sh1 = mod[:, 0:D_MODEL]
        sc1 = mod[:, D_MODEL:2 * D_MODEL]
        x = x_ref[tile_rows(i), :]
        ms = jnp.mean(x * x, axis=-1, keepdims=True)
        u = x * lax.rsqrt(ms + EPS) * g1_ref[...] * (1.0 + sc1) + sh1
        u_scr[tile_rows(i), :] = u.astype(BF16)

    def a_gates(i):
        lane = lax.broadcasted_iota(jnp.int32, (2 * N_HEADS, TILE_M), 1)
        is_fwd = lax.broadcasted_iota(jnp.int32, (2 * N_HEADS, TILE_M), 0) < N_HEADS
        bias = jnp.concatenate([bgates_ref[...], jnp.zeros((1, GATE_PAD - N_GATE), F32)], axis=1)
        gate_t = (_dot(u_scr[tile_rows(i), :], wgate_ref[...]) + bias).T
        gp = gate_t[0:N_GATE, :]
        i_all = gp[0:2 * N_HEADS, :]
        f_all = gp[2 * N_HEADS:4 * N_HEADS, :]
        logf = jnp.minimum(f_all, 0.0) - jnp.log1p(jnp.exp(-jnp.abs(f_all)))
        b_row = jnp.where(is_fwd, _seg_scan(logf, lane, L, L, "sum", False),
                          _seg_scan(logf, lane, L, L, "sum", True))
        g_row = i_all - b_row
        cm_row = jnp.where(is_fwd, _seg_scan(g_row, lane, L, L, "max", False),
                           _seg_scan(g_row, lane, L, L, "max", True))
        grow_scr[:, tile_rows(i)] = g_row
        stacked = jnp.concatenate(
            [b_row, cm_row, jnp.zeros((GATE_PAD - 4 * N_HEADS, TILE_M), F32)], axis=0)
        colf_scr[tile_rows(i), :] = stacked.T

    def proj(i, k):
        return _dot(u_scr[tile_rows(i), :], wmain_ref[:, k * D_BRANCH:(k + 1) * D_BRANCH])

    def a_xp(i):
        xp_scr[pl.ds(_aligned(i * TILE_M + POOL_PAD, 8), TILE_M), :] = proj(i, 0)

    def a_su(i):
        su_scr[...] = proj(i, 1)

    def a_sv(i):
        sv = proj(i, 2)
        vn = sv * lax.rsqrt(jnp.mean(sv * sv, axis=-1, keepdims=True) + EPS) * gsgu_ref[...]
        vn_scr[...] = vn.astype(BF16)

    def a_q(i):
        q_scr[tile_rows(i), :] = proj(i, 3).astype(BF16)

    def a_k(i):
        kt = _dot_nt(wkt_ref[...], u_scr[tile_rows(i), :])
        kt_scr[:, tile_rows(i)] = (kt * (HEAD_DIM ** -0.5)).astype(BF16)

    def a_v(i):
        v = proj(i, 5).astype(BF16)
        ones_blk = jnp.ones((TILE_M, HEAD_DIM), BF16)
        for h in range(N_HEADS):
            vaug_scr[tile_rows(i), 2 * h * HEAD_DIM:(2 * h + 1) * HEAD_DIM] = v[:, h * HEAD_DIM:(h + 1) * HEAD_DIM]
            vaug_scr[tile_rows(i), (2 * h + 1) * HEAD_DIM:(2 * h + 2) * HEAD_DIM] = ones_blk

    def a_o(i):
        o_scr[tile_rows(i), :] = proj(i, 6)

    def a_sgu(ch, i):
        crow = slice(ch * SGU_CHUNK, (ch + 1) * SGU_CHUNK)
        for g in range(N_GROUPS):
            gc = slice(g * GROUP, (g + 1) * GROUP)
            mixed = _dot(wsgu_ref[g], vn_scr[crow, gc]) + bsgu_ref[:, gc]
            yb = su_scr[crow, gc] * mixed
            yb_scr[pl.ds(_aligned(i * TILE_M + ch * SGU_CHUNK, SGU_CHUNK), SGU_CHUNK), gc] = yb.astype(BF16)

    a_parts = [a_norm, a_gates, a_xp, a_su, a_sv, a_q, a_k, a_v, a_o]
    a_parts += [functools.partial(a_sgu, ch) for ch in range(TILE_M // SGU_CHUNK)]

    def pool_pad():
        zpad = jnp.zeros((POOL_PAD, D_BRANCH), F32)
        xp_scr[0:POOL_PAD, :] = zpad
        xp_scr[T + POOL_PAD:T + 2 * POOL_PAD, :] = zpad

    def pool_seq(g):
        win = POOL_WINDOWS[g]
        gc = slice(g * GROUP, (g + 1) * GROUP)
        half = win // 2
        t_idx = lax.broadcasted_iota(jnp.int32, (T, GROUP), 0)
        total = jnp.zeros((T, GROUP), F32)
        for j in range(-half, win - half):
            total = total + xp_scr[POOL_PAD + j:POOL_PAD + j + T, gc]
        cnt = (jnp.minimum(t_idx + (win - half), T) - jnp.maximum(t_idx - half, 0)).astype(F32)
        dlt = total / cnt - xp_scr[POOL_PAD:POOL_PAD + T, gc]
        ya = _dot(dlt.astype(BF16), wpool_ref[g]) * pscale_ref[...][:, gc]
        ya_scr[:, gc] = ya.astype(BF16)

    def a_colpool(i):
        c_idx = lax.broadcasted_iota(jnp.int32, (GRID_W, GROUP), 0)
        for g, win in enumerate(POOL_WINDOWS):
            gc = slice(g * GROUP, (g + 1) * GROUP)
            half = win // 2
            inv_cnt = 1.0 / (jnp.minimum(c_idx + (win - half), GRID_W) - jnp.maximum(c_idx - half, 0)).astype(F32)
            for r in range(TILE_M // GRID_W):
                r0 = i * TILE_M + r * GRID_W
                xg = xp_scr[pl.ds(_aligned(r0 + POOL_PAD, 8), GRID_W), gc]
                total = xg
                for j in range(-half, win - half):
                    if j != 0:
                        valid = (c_idx + j >= 0) & (c_idx + j < GRID_W)
                        total = total + jnp.where(valid, pltpu.roll(xg, (-j) % GRID_W, 0), 0.0)
                m1_scr[pl.ds(_aligned(r0, GRID_W), GRID_W), gc] = total * inv_cnt

    def pool_grid_rows(g):
        win = POOL_WINDOWS[g]
        gc = slice(g * GROUP, (g + 1) * GROUP)
        half = win // 2
        n_rows = T // GRID_W
        for r in range(n_rows):
            lo = max(r - half, 0)
            hi = min(r + (win - half), n_rows)
            acc = m1_scr[lo * GRID_W:(lo + 1) * GRID_W, gc]
            for rr in range(lo + 1, hi):
                acc = acc + m1_scr[rr * GRID_W:(rr + 1) * GRID_W, gc]
            xg = xp_scr[POOL_PAD + r * GRID_W:POOL_PAD + (r + 1) * GRID_W, gc]
            dlt_scr[r * GRID_W:(r + 1) * GRID_W, :] = acc / float(hi - lo) - xg
        ya = _dot(dlt_scr[...].astype(BF16), wpool_ref[g]) * pscale_ref[...][:, gc]
        ya_scr[:, gc] = ya.astype(BF16)

    if grid2d:
        assert TILE_M % GRID_W == 0
        a_parts.insert(a_parts.index(a_xp) + 1, a_colpool)
        pool_parts = [functools.partial(pool_grid_rows, g) for g in range(N_GROUPS)]
    else:
        pool_parts = [pool_pad] + [functools.partial(pool_seq, g) for g in range(N_GROUPS)]

    def keep_mask(d, rb, col_lo, ncols):
        t_loc = rb * RB + lax.broadcasted_iota(jnp.int32, (RB, ncols), 0)
        s_loc = col_lo + lax.broadcasted_iota(jnp.int32, (RB, ncols), 1)
        return s_loc <= t_loc if d == 0 else s_loc >= t_loc

    def mlstm_prepare(d, c, m_in):
        c0 = _aligned(c * L, L)
        colf = colf_scr[pl.ds(c0, L), :]
        bcol = colf[:, N_HEADS * d:N_HEADS * (d + 1)]
        cmcol = colf[:, 2 * N_HEADS + N_HEADS * d:2 * N_HEADS + N_HEADS * (d + 1)]
        mx = jnp.maximum(m_in, cmcol)
        last = L - 1 if d == 0 else 0
        mx_last = mx[last:last + 1, :]
        return dict(c0=c0, mx=mx, mx_last=mx_last, w_inter=jnp.exp(m_in - mx), e_negm=jnp.exp(-(bcol + mx)),
                    decay=jnp.exp(m_in - mx_last), m_new=bcol[last:last + 1, :] + mx_last)

    def block_cols(d, rb):
        return (0, (rb + 1) * RB) if d == 0 else (rb * RB, L)

    def mlstm_scores(d, p, h):
        c0 = p["c0"]
        hc = slice(h * HEAD_DIM, (h + 1) * HEAD_DIM)
        qks = []
        for rb in range(L // RB):
            col_lo, col_hi = block_cols(d, rb)
            qb = q_scr[pl.ds(_aligned(c0 + rb * RB, RB), RB), hc]
            qks.append(_dot(qb, kt_scr[hc, pl.ds(_aligned(c0 + col_lo, RB), col_hi - col_lo)]))
        p["qk", h] = qks

    def mlstm_head(d, p, h):
        c0, mx, mx_last = p["c0"], p["mx"], p["mx_last"]
        rows = pl.ds(c0, L)
        hc = slice(h * HEAD_DIM, (h + 1) * HEAD_DIM)
        ac = slice(2 * h * HEAD_DIM, (2 * h + 2) * HEAD_DIM)
        grow = grow_scr[N_HEADS * d + h:N_HEADS * d + h + 1, rows]
        if has_state:
            cn_dh = cn_scr[d, h]
            cn_bf = cn_dh.astype(BF16)
        for rb in range(L // RB):
            col_lo, col_hi = block_cols(d, rb)
            ncols = col_hi - col_lo
            trow = slice(rb * RB, (rb + 1) * RB)
            rrows = pl.ds(_aligned(c0 + rb * RB, RB), RB)
            crows = pl.ds(_aligned(c0 + col_lo, RB), ncols)
            qb = q_scr[rrows, hc]
            qk = p["qk", h][rb]
            w = jnp.exp(grow[:, col_lo:col_hi] - mx[trow, h:h + 1])
            s = (qk * jnp.where(keep_mask(d, rb, col_lo, ncols), w, 0.0)).astype(BF16)
            na = _dot(s, vaug_scr[crows, ac])
            if has_state:
                na = na + p["w_inter"][trow, h:h + 1] * _dot(qb, cn_bf)
            hval = na[:, 0:HEAD_DIM] / jnp.maximum(jnp.abs(na[:, HEAD_DIM:]), p["e_negm"][trow, h:h + 1])
            if d == 0:
                h_scr[rrows, hc] = hval
            else:
                h_scr[rrows, hc] += hval
        w_state = jnp.exp(grow - mx_last[:, h:h + 1])
        kw_t = (kt_scr[hc, rows].astype(F32) * w_state).astype(BF16)
        upd = _dot(kw_t, vaug_scr[rows, ac])
        if has_state:
            cn_scr[d, h] = p["decay"][:, h:h + 1] * cn_dh + upd
        else:
            cout_ref[d, h] = upd[:, 0:HEAD_DIM]
            nout_ref[d, h:h + 1, :] = upd[:, HEAD_DIM:].T[0:1, :]

    def merge_ab(c, nj):
        nc = slice(nj * NW, (nj + 1) * NW)
        ub = u_scr[tile_rows(c), :]
        acc = None
        for r, y_scr in enumerate((ya_scr, yb_scr)):
            gate = jax.nn.sigmoid(_dot(ub, wbrg_ref[:, r * D_MODEL + nj * NW:r * D_MODEL + (nj + 1) * NW]))
            term = gate * _dot(y_scr[tile_rows(c), :], wbranch_ref[r * D_BRANCH:(r + 1) * D_BRANCH, nc])
            acc = term if acc is None else acc + term
        mab_scr[tile_rows(c), nc] = acc

    def gate_c(c, nj):
        w = wbrg_ref[:, 2 * D_MODEL + nj * NW:2 * D_MODEL + (nj + 1) * NW]
        gc_scr[:, nj * NW:(nj + 1) * NW] = jax.nn.sigmoid(_dot(u_scr[tile_rows(c), :], w))

    def tail_yc(c):
        gml = gml_ref[...]
        for h in range(N_HEADS):
            hc = slice(h * HEAD_DIM, (h + 1) * HEAD_DIM)
            hh = h_scr[tile_rows(c), hc]
            hn = hh * lax.rsqrt(jnp.mean(hh * hh, axis=-1, keepdims=True) + EPS) * gml[:, hc]
            yc_scr[:, hc] = (jax.nn.sigmoid(o_scr[tile_rows(c), hc]) * hn).astype(BF16)

    def tail_merge(c, nj):
        nc = slice(nj * NW, (nj + 1) * NW)
        acc = mab_scr[tile_rows(c), nc] + gc_scr[:, nc] * _dot(yc_scr[...], wbranch_ref[2 * D_BRANCH:, nc])
        merged_scr[:, nc] = acc.astype(BF16)

    def tail_out(c):
        gt1 = mod_ref[...][:, 2 * D_MODEL:3 * D_MODEL]
        out_ref[tile_rows(c), :] = x_ref[tile_rows(c), :] + gt1 * _dot(merged_scr[...], wout_ref[...])

    def tail_parts(c):
        return ([functools.partial(tail_yc, c)] + [functools.partial(tail_merge, c, nj) for nj in range(n_merge)]
                + [functools.partial(tail_out, c)])

    def scan_parts(d, c, m_in, fillers, out):
        p = {}

        def prepare():
            p.update(mlstm_prepare(d, c, m_in))
            out[d] = p["m_new"]

        parts = [prepare]
        for h in range(N_HEADS):
            parts.append(functools.partial(mlstm_scores, d, p, h))
            if h < len(fillers):
                parts.append(fillers[h])
            parts.append(functools.partial(mlstm_head, d, p, h))
        return parts + list(fillers[N_HEADS:])

    if n_tiles == 1 and n_chunks == 1:
        zeros = jnp.zeros((1, N_HEADS), F32)
        m_out = {}
        parts = [functools.partial(part, 0) for part in a_parts] + pool_parts + [wait_late]
        parts += scan_parts(0, 0, zeros, [functools.partial(merge_ab, 0, nj) for nj in range(n_merge)], m_out)
        parts += scan_parts(1, 0, zeros, [functools.partial(gate_c, 0, nj) for nj in range(n_merge)], m_out)
        parts += tail_parts(0)

        def write_m():
            mout_ref[0:1, :] = m_out[0]
            mout_ref[1:2, :] = m_out[1]

        return parts + [write_m]

    def run_loops():
        def phase_a(i, carry):
            for part in a_parts:
                part(i)
            return carry

        lax.fori_loop(0, n_tiles, phase_a, 0)
        for part in pool_parts:
            part()
        for d in range(2):
            for h in range(N_HEADS):
                n_rows = jnp.broadcast_to(n0_ref[d, h:h + 1, :], (HEAD_DIM, HEAD_DIM))
                cn_scr[d, h] = jnp.concatenate([c0_ref[d, h], n_rows.T], axis=1)
        m_init = m0_ref[...]
        wait_late()

        def fwd_step(j, m_in):
            m_out = {}
            for part in scan_parts(0, j, m_in, [functools.partial(merge_ab, j, nj) for nj in range(n_merge)], m_out):
                part()
            return m_out[0]

        def bwd_step(j, m_in):
            c = n_chunks - 1 - j
            m_out = {}
            for part in scan_parts(1, c, m_in, [functools.partial(gate_c, c, nj) for nj in range(n_merge)], m_out):
                part()
            for part in tail_parts(c):
                part()
            return m_out[1]

        lax.fori_loop(0, n_chunks, fwd_step, m_init[0:1, :])
        lax.fori_loop(0, n_chunks, bwd_step, m_init[1:2, :])

    assert has_state, "multi-chunk sequences carry the cached state"
    return [run_loops]


def _mixer_call(l, xall, lay, prm, stream, states=None, state_outs=None):
    ctx = stream == "ctx"
    T = SEQ if ctx else DEC_SEQ
    assert T % TILE_M == 0 and TILE_M == MLSTM_L and MLSTM_L % MLSTM_ROW_BLOCK == 0
    n_b = BATCH if ctx else DEC_BATCH
    own_array = xall.shape[0] != N_TOK
    blk0 = 0 if ctx or own_array else N_TOK_CTX // T
    once = pl.Buffered(1)

    def wspec(shape):
        nd = len(shape)
        return pl.BlockSpec((None,) + shape, lambda b: (l,) + (0,) * nd, pipeline_mode=once)

    def lspec(shape):
        return pl.BlockSpec(shape, lambda b: (0,) * len(shape), pipeline_mode=once)

    x_spec = pl.BlockSpec((T, D_MODEL), lambda b: (blk0 + b, 0))
    in_specs = [
        x_spec,
        lspec((COND_ROWS, 6 * D_MODEL)),
        lspec((DEPTH, D_MODEL)),
        wspec((D_MODEL, N_MAIN)),
        lspec((D_BRANCH, D_MODEL)),
        lspec((D_MODEL, GATE_PAD)),
        pl.BlockSpec(memory_space=pl.ANY),
        pl.BlockSpec(memory_space=pl.ANY),
        pl.BlockSpec(memory_space=pl.ANY),
        wspec((N_GROUPS, GROUP, GROUP)),
        lspec((DEPTH, D_BRANCH)),
        lspec((DEPTH, D_BRANCH)),
        wspec((N_GROUPS, SGU_CHUNK, SGU_CHUNK)),
        wspec((SGU_CHUNK, D_BRANCH)),
        lspec((DEPTH, N_GATE)),
        lspec((DEPTH, D_BRANCH)),
    ]
    args = [xall, lay["mods"], prm["g_norm1"], prm["w_in16"], lay["w_kt"],
            lay["w_gate"], lay["w_brg"], lay["w_branch"], lay["w_out"],
            prm["w_pool"], prm["pool_scale"], prm["g_sgu"], prm["w_sgu"],
            prm["b_sgu_tile"], prm["b_gates"], prm["g_mlstm"]]
    out_shape = [jax.ShapeDtypeStruct(xall.shape, xall.dtype)]
    out_specs = [pl.BlockSpec((T, D_MODEL), lambda b: (blk0 + b, 0), pipeline_mode=None if ctx else once)]
    aliases = {} if own_array else {0: 0}
    passthrough = []
    if ctx:
        per_seq = [(2, N_HEADS, HEAD_DIM, HEAD_DIM), (2, N_HEADS, HEAD_DIM), (2, N_HEADS)]
        out_shape += [jax.ShapeDtypeStruct((BATCH, DEPTH) + s, F32) for s in per_seq]
        if state_outs is None:
            out_specs += [pl.BlockSpec((None, DEPTH) + s, lambda b, nd=len(s): (b,) + (0,) * (nd + 1))
                          for s in per_seq]
        else:
            out_specs += [pl.BlockSpec((None, None) + s, lambda b, nd=len(s): (b, l) + (0,) * nd) for s in per_seq]
            passthrough += [(a, 1 + k) for k, a in enumerate(state_outs)]
    else:
        in_specs += [
            pl.BlockSpec((None, None, 2, N_HEADS, HEAD_DIM, HEAD_DIM), lambda b: (b, l, 0, 0, 0, 0)),
            pl.BlockSpec((None, None, 2, N_HEADS, HEAD_DIM), lambda b: (b, l, 0, 0, 0)),
            pl.BlockSpec((None, None, 2, N_HEADS), lambda b: (b, l, 0, 0)),
        ]
        args += list(states)
    for buf, out_idx in passthrough:
        aliases[len(args)] = out_idx
        in_specs.append(pl.BlockSpec(memory_space=pl.ANY))
        args.append(buf)

    scratch = [
        pltpu.VMEM((T, D_MODEL), BF16),
        pltpu.VMEM((T + 2 * POOL_PAD, D_BRANCH), F32),
        pltpu.VMEM((TILE_M, D_BRANCH), F32),
        pltpu.VMEM((TILE_M, D_BRANCH), BF16),
        pltpu.VMEM((T, D_BRANCH), BF16),
        pltpu.VMEM((D_BRANCH, T), BF16),
        pltpu.VMEM((T, 2 * D_BRANCH), BF16),
        pltpu.VMEM((T, D_BRANCH), F32),
        pltpu.VMEM((2 * N_HEADS, T), F32),
        pltpu.VMEM((T, GATE_PAD), F32),
        pltpu.VMEM((T, D_BRANCH), F32),
        pltpu.VMEM((T, D_BRANCH), BF16),
        pltpu.VMEM((T, D_BRANCH), BF16),
        pltpu.VMEM((TILE_M, D_BRANCH), BF16),
        pltpu.VMEM((T, D_MODEL), F32),
        pltpu.VMEM((TILE_M, D_MODEL), F32),
        pltpu.VMEM((TILE_M, D_MODEL), BF16),
    ]
    if not ctx:
        scratch += [pltpu.VMEM((2, N_HEADS, HEAD_DIM, 2 * HEAD_DIM), F32)]
        scratch += [pltpu.VMEM((T, D_BRANCH), F32), pltpu.VMEM((T, GROUP), F32)]
    scratch += [pltpu.VMEM(args[k].shape, args[k].dtype) for k in _MIXER_LATE_INPUTS]
    late_sem = [pltpu.SemaphoreType.DMA((len(_MIXER_LATE_INPUTS),))]

    outs = pl.pallas_call(
        functools.partial(_mixer_kernel, T=T, grid2d=not ctx, has_state=not ctx, layer=l,
                          n_in=len(in_specs), n_pass=len(passthrough), n_out=len(out_specs)),
        grid=(n_b,),
        in_specs=in_specs,
        out_specs=out_specs,
        out_shape=out_shape,
        scratch_shapes=scratch + late_sem,
        input_output_aliases=aliases,
        compiler_params=pltpu.CompilerParams(
            dimension_semantics=("arbitrary",),
            vmem_limit_bytes=_vmem_request(list(zip(args, in_specs)) + list(zip(out_shape, out_specs)), scratch)),
        name="mixer_ctx" if ctx else "mixer_lat",
    )(*args)
    return outs


def _ffn_kernel(*refs, layer, final, split_input):
    n_x = 2 if split_input else 1
    x_refs, (mod_ref, g2_ref, w1_hbm, w2_hbm, gfin_ref), rest = refs[:n_x], refs[n_x:n_x + 5], refs[n_x + 5:]
    if final:
        out_refs, rest = rest[:2], rest[2:]
    else:
        prep_in, rest = rest[:N_PREP_OPERANDS], rest[N_PREP_OPERANDS:]
        out_refs, prep_out, rest = rest[:1], rest[1:1 + len(LAYER_ARRAYS)], rest[1 + len(LAYER_ARRAYS):]
    acc_scr, h_scr, w1_scr, w2_scr, st1_scr, st2_scr, sem = rest[:7]
    n_chunks = D_FF // FFN_CHUNK
    tile = pl.program_id(0)
    first_tile = tile == 0
    ctx_tiles, lat_tiles = N_TOK_CTX // FFN_TILE_M, DEC_SEQ // FFN_TILE_M
    if split_input:
        x_ref = rest[7]

        @pl.when(tile < ctx_tiles)
        def _():
            x_ref[...] = x_refs[0][...]

        @pl.when(tile >= ctx_tiles)
        def _():
            x_ref[...] = x_refs[1][...]
    else:
        x_ref = x_refs[0]
    mod_ref = mod_ref.at[pl.ds(jnp.where(tile < ctx_tiles, 0, 1 + (tile - ctx_tiles) // lat_tiles), 1), :]
    g2_ref = g2_ref.at[pl.ds(layer, 1), :]

    def w1_copy(c):
        slot = c % FFN_STAGES
        c0 = _aligned(c * FFN_CHUNK, FFN_CHUNK)
        return pltpu.make_async_copy(w1_hbm.at[layer, :, pl.ds(c0, FFN_CHUNK)], st1_scr.at[slot], sem.at[0, slot])

    def w2_copy(c):
        slot = c % FFN_STAGES
        c0 = _aligned(c * FFN_CHUNK, FFN_CHUNK)
        return pltpu.make_async_copy(w2_hbm.at[layer, pl.ds(c0, FFN_CHUNK), :], st2_scr.at[slot], sem.at[1, slot])

    def land_w1(c):
        w1_copy(c).wait()
        c0 = _aligned(c * FFN_CHUNK, FFN_CHUNK)
        w1_scr[:, pl.ds(c0, FFN_CHUNK)] = st1_scr[c % FFN_STAGES].astype(BF16)

    def land_w2(c):
        w2_copy(c).wait()
        c0 = _aligned(c * FFN_CHUNK, FFN_CHUNK)
        w2_scr[pl.ds(c0, FFN_CHUNK), :] = st2_scr[c % FFN_STAGES].astype(BF16)

    def prepare_next_layer():
        if not final:
            _prepare_layer(*prep_in, *prep_out, layer=layer + 1)

    def normed_input():
        mod = mod_ref[...]
        sh2 = mod[:, 3 * D_MODEL:4 * D_MODEL]
        sc2 = mod[:, 4 * D_MODEL:5 * D_MODEL]
        x = x_ref[...]
        ms = jnp.mean(x * x, axis=-1, keepdims=True)
        return (x * lax.rsqrt(ms + EPS) * g2_ref[...] * (1.0 + sc2) + sh2).astype(BF16)

    def hidden(ub, c):
        c0 = _aligned(c * FFN_CHUNK, FFN_CHUNK)
        h = _dot(ub, w1_scr[:, pl.ds(c0, FFN_CHUNK)])
        h_scr[:, pl.ds(c0, FFN_CHUNK)] = jnp.square(jnp.maximum(h, 0.0)).astype(BF16)

    def finish(ffn_out):
        gt2 = mod_ref[...][:, 5 * D_MODEL:6 * D_MODEL]
        y = x_ref[...] + gt2 * ffn_out
        if not final:
            out_refs[0][...] = y
            return
        y = y * lax.rsqrt(jnp.mean(y * y, axis=-1, keepdims=True) + EPS) * gfin_ref[...]
        in_ctx = tile < ctx_tiles

        @pl.when(in_ctx)
        def _():
            out_refs[0][...] = y

        @pl.when(jnp.logical_not(in_ctx))
        def _():
            out_refs[1][...] = y

    @pl.when(first_tile)
    def _():
        for c in range(FFN_STAGES):
            w1_copy(c).start()
            w2_copy(c).start()
        prepare_next_layer()
        ub = normed_input()
        acc_scr[...] = jnp.zeros_like(acc_scr)

        def chunk(c, carry):
            land_w1(c)
            land_w2(c)

            @pl.when(c + FFN_STAGES < n_chunks)
            def _():
                w1_copy(c + FFN_STAGES).start()
                w2_copy(c + FFN_STAGES).start()

            hidden(ub, c)
            c0 = _aligned(c * FFN_CHUNK, FFN_CHUNK)
            acc_scr[...] += _dot(h_scr[:, pl.ds(c0, FFN_CHUNK)], w2_scr[pl.ds(c0, FFN_CHUNK), :])
            return carry

        lax.fori_loop(0, n_chunks, chunk, 0)
        finish(acc_scr[...])

    @pl.when(jnp.logical_not(first_tile))
    def _():
        prepare_next_layer()
        ub = normed_input()
        for c in range(n_chunks):
            hidden(ub, c)
        finish(_dot(h_scr[...], w2_scr[...]))


def _ffn_call(l, xall, lay, prm, g_final, final):
    n_tiles = N_TOK // FFN_TILE_M
    ctx_tiles = N_TOK_CTX // FFN_TILE_M
    once = pl.Buffered(1)
    split_input = isinstance(xall, tuple)
    assert not (split_input and final)

    x_spec = pl.BlockSpec((FFN_TILE_M, D_MODEL), lambda t: (t, 0))
    ctx_spec = pl.BlockSpec((FFN_TILE_M, D_MODEL), lambda t: (jnp.minimum(t, ctx_tiles - 1), 0))
    lat_spec = pl.BlockSpec((FFN_TILE_M, D_MODEL), lambda t: (jnp.maximum(t - ctx_tiles, 0), 0))
    xs = list(xall) if split_input else [xall]
    in_specs = ([ctx_spec, lat_spec] if split_input else [x_spec]) + [
        pl.BlockSpec((COND_ROWS, 6 * D_MODEL), lambda t: (0, 0), pipeline_mode=once),
        pl.BlockSpec((DEPTH, D_MODEL), lambda t: (0, 0), pipeline_mode=once),
        pl.BlockSpec(memory_space=pl.ANY),
        pl.BlockSpec(memory_space=pl.ANY),
        pl.BlockSpec((1, D_MODEL), lambda t: (0, 0)),
    ]
    args = xs + [lay["mods"], prm["g_norm2"], prm["w_ff1"], prm["w_ff2"], g_final.reshape(1, D_MODEL)]
    if final:
        out_specs = [ctx_spec, lat_spec]
        out_shape = [jax.ShapeDtypeStruct((N_TOK_CTX, D_MODEL), F32),
                     jax.ShapeDtypeStruct((N_TOK - N_TOK_CTX, D_MODEL), F32)]
        aliases = {}
    else:
        assert n_tiles == N_PREP_STEPS
        prep_args, prep_in_specs, prep_out_specs, prep_out_shape = _layer_prep_specs(l + 1, prm)
        in_specs += prep_in_specs
        args += prep_args
        out_specs = [x_spec] + prep_out_specs
        out_shape = [jax.ShapeDtypeStruct((N_TOK, D_MODEL), F32)] + prep_out_shape
        aliases = {} if split_input else {0: 0}
    sems = [pltpu.SemaphoreType.DMA((2, FFN_STAGES))]
    gather = [pltpu.VMEM((FFN_TILE_M, D_MODEL), F32)] if split_input else []
    scratch = [
        pltpu.VMEM((FFN_TILE_M, D_MODEL), F32),
        pltpu.VMEM((FFN_TILE_M, D_FF), BF16),
        pltpu.VMEM((D_MODEL, D_FF), BF16),
        pltpu.VMEM((D_FF, D_MODEL), BF16),
        pltpu.VMEM((FFN_STAGES, D_MODEL, FFN_CHUNK), F32),
        pltpu.VMEM((FFN_STAGES, FFN_CHUNK, D_MODEL), F32),
    ]
    return pl.pallas_call(
        functools.partial(_ffn_kernel, layer=l, final=final, split_input=split_input),
        grid=(n_tiles,),
        in_specs=in_specs,
        out_specs=out_specs,
        out_shape=out_shape,
        scratch_shapes=scratch + sems + gather,
        input_output_aliases=aliases,
        compiler_params=pltpu.CompilerParams(
            dimension_semantics=("arbitrary",),
            vmem_limit_bytes=_vmem_request(list(zip(args, in_specs)) + list(zip(out_shape, out_specs)),
                                           scratch + gather)),
        name="ffn_final" if final else "ffn",
    )(*args)


def kernel(x_prompt, x_sample, state_C, state_n, state_m, c, c_ctx, w_ada, b_ada, g_norm1, g_norm2, w_in,
           b_gates, w_pool, pool_scale, g_sgu, w_sgu, b_sgu, g_mlstm, w_branch, w_out, w_ff1, w_ff2, g_final):
    w_in16 = w_in.astype(BF16)
    prm = {
        "w_in16": w_in16,
        "w_branch_rows": w_branch.reshape(DEPTH, N_BRANCH * D_BRANCH, D_MODEL),
        "w_out": w_out,
        "w_ada": w_ada,
        "b_ada": b_ada,
        "w_pool": w_pool.astype(BF16),
        "w_sgu": w_sgu.astype(BF16),
        "w_ff1": w_ff1,
        "w_ff2": w_ff2,
        "g_norm1": g_norm1,
        "g_norm2": g_norm2,
        "pool_scale": pool_scale,
        "g_sgu": g_sgu,
        "g_mlstm": g_mlstm,
        "b_gates": b_gates,
        "b_sgu_tile": jnp.repeat(jnp.swapaxes(b_sgu, 1, 2), GROUP, axis=2),
        "cond8": jnp.concatenate(
            [c_ctx[None, :], c, jnp.zeros((COND_ROWS - 1 - DEC_BATCH, D_MODEL), F32)], axis=0),
    }
    lay = dict(zip(LAYER_ARRAYS, _first_layer_prep_call(prm)))

    xc, *new_states = _mixer_call(0, x_prompt.reshape(N_TOK_CTX, D_MODEL), lay, prm, "ctx")
    (xs,) = _mixer_call(0, x_sample.reshape(-1, D_MODEL), lay, prm, "lat", states=(state_C, state_n, state_m))
    xall = (xc, xs)
    for l in range(DEPTH):
        if l > 0:
            xall, *new_states = _mixer_call(l, xall, lay, prm, "ctx", state_outs=new_states)
            (xall,) = _mixer_call(l, xall, lay, prm, "lat", states=(state_C, state_n, state_m))
        if l < DEPTH - 1:
            xall, *nxt = _ffn_call(l, xall, lay, prm, g_final, final=False)
            lay = dict(zip(LAYER_ARRAYS, nxt))
    y_prompt, y_sample = _ffn_call(DEPTH - 1, xall, lay, prm, g_final, final=True)
    return (y_prompt.reshape(BATCH, SEQ, D_MODEL), y_sample.reshape(DEC_BATCH, DEC_SEQ, D_MODEL), *new_states)
```

```python
import functools
import math

import jax
import jax.numpy as jnp
from jax import lax
from jax.experimental import pallas as pl
from jax.experimental.pallas import tpu as pltpu

F32 = jnp.float32
BF16 = jnp.bfloat16

D_MODEL = 1024
DEPTH = 4
BATCH = 16
SEQ = 256
DEC_BATCH = 2
DEC_SEQ = 1024
GRID_W = 64
D_BRANCH = 512
POOL_WINDOWS = (2, 4, 8, 16)
N_GROUPS = 4
GROUP = D_BRANCH // N_GROUPS
SGU_CHUNK = 128
N_HEADS = 4
HEAD_DIM = D_BRANCH // N_HEADS
N_BRANCH = 3
D_FF = 4 * D_MODEL
EPS = 1e-6

LANES = 128
BF16_SUBLANES = 16

N_MAIN = 7 * D_BRANCH
N_GATE = 4 * N_HEADS
GATE_PAD = LANES
N_TOK_CTX = BATCH * SEQ
N_TOK = N_TOK_CTX + DEC_BATCH * DEC_SEQ
COND_ROWS = 8

TILE_M = 256
MLSTM_L = 256
MLSTM_ROW_BLOCK = 128
MERGE_COLS = 256
POOL_PAD = 8
FFN_TILE_M = 512
FFN_CHUNK = 512
FFN_STAGES = 2
V7X_VMEM_BYTES = 64 * 1024 * 1024
VMEM_COMPILER_RESERVE = 6 * 1024 * 1024
VMEM_CALL_FLOOR = 60 * 1024 * 1024


def _vmem_request(windows, scratch):
    total = VMEM_COMPILER_RESERVE
    for arr, spec in windows:
        if spec.block_shape is not None:
            n_buf = 2 if spec.pipeline_mode is None else spec.pipeline_mode.buffer_count
            total += n_buf * math.prod(d for d in spec.block_shape if d is not None) * jnp.dtype(arr.dtype).itemsize
    for buf in scratch:
        total += math.prod(buf.shape) * jnp.dtype(buf.dtype).itemsize
    return min(max(total, VMEM_CALL_FLOOR), V7X_VMEM_BYTES)


_dot = functools.partial(jnp.dot, preferred_element_type=F32)
_dot_nt = functools.partial(lax.dot_general, dimension_numbers=(((1,), (1,)), ((), ())),
                            preferred_element_type=F32)


def _aligned(offset, align):
    return offset if isinstance(offset, int) else pl.multiple_of(offset, align)


LAYER_ARRAYS = ("mods", "w_branch", "w_out", "w_brg", "w_gate", "w_kt")
N_PREP_OPERANDS = 6
N_PREP_STEPS = N_TOK // FFN_TILE_M


def _prepare_layer(cond_ref, wada_ref, bada_ref, wbr_ref, wo_ref, win_ref,
                   mods_out, wbr_out, wo_out, wbrg_out, wgate_out, wkt_out, *, layer):
    cond = cond_ref[...]
    s = (cond * jax.nn.sigmoid(cond)).astype(BF16)
    mods_out[...] = _dot(s, wada_ref[...].astype(BF16)) + bada_ref[layer:layer + 1, :]
    wbr_out[...] = wbr_ref[...].astype(BF16)
    wo_out[...] = wo_ref[...].astype(BF16)
    n_br = N_BRANCH * D_MODEL
    for c0 in range(0, n_br, FFN_CHUNK):
        width = min(FFN_CHUNK + GATE_PAD, n_br + N_GATE - c0)
        window = win_ref[:, N_MAIN + c0:N_MAIN + c0 + width]
        wbrg_out[:, c0:c0 + FFN_CHUNK] = window[:, N_GATE:N_GATE + FFN_CHUNK]
    gate_tile = win_ref[:, N_MAIN:N_MAIN + GATE_PAD]
    is_gate = lax.broadcasted_iota(jnp.int32, gate_tile.shape, 1) < N_GATE
    wgate_out[...] = jnp.where(is_gate, gate_tile, jnp.zeros_like(gate_tile))
    wkt_out[...] = win_ref[:, 4 * D_BRANCH:5 * D_BRANCH].astype(F32).T.astype(BF16)


def _layer_prep_specs(layer, prm):
    ada_cols = 6 * D_MODEL // N_PREP_STEPS
    br_rows = N_BRANCH * D_BRANCH // N_PREP_STEPS
    wo_steps = D_MODEL // br_rows
    assert ada_cols % LANES == 0 and br_rows % BF16_SUBLANES == 0 and wo_steps <= N_PREP_STEPS
    wo_map = lambda t: (jnp.minimum(t, wo_steps - 1), 0)
    args = [prm["cond8"], prm["w_ada"], prm["b_ada"], prm["w_branch_rows"], prm["w_out"], prm["w_in16"]]
    in_specs = [
        pl.BlockSpec((COND_ROWS, D_MODEL), lambda t: (0, 0)),
        pl.BlockSpec((None, D_MODEL, ada_cols), lambda t: (layer, 0, t)),
        pl.BlockSpec((DEPTH, ada_cols), lambda t: (0, t)),
        pl.BlockSpec((None, br_rows, D_MODEL), lambda t: (layer, t, 0)),
        pl.BlockSpec((None, br_rows, D_MODEL), lambda t: (layer,) + wo_map(t)),
        pl.BlockSpec((None, br_rows, prm["w_in16"].shape[-1]), lambda t: (layer,) + wo_map(t)),
    ]
    out_specs = [
        pl.BlockSpec((COND_ROWS, ada_cols), lambda t: (0, t)),
        pl.BlockSpec((br_rows, D_MODEL), lambda t: (t, 0)),
        pl.BlockSpec((br_rows, D_MODEL), wo_map),
        pl.BlockSpec((br_rows, N_BRANCH * D_MODEL), wo_map),
        pl.BlockSpec((br_rows, GATE_PAD), wo_map),
        pl.BlockSpec((D_BRANCH, br_rows), lambda t: wo_map(t)[::-1]),
    ]
    out_shape = [jax.ShapeDtypeStruct((COND_ROWS, 6 * D_MODEL), F32),
                 jax.ShapeDtypeStruct((N_BRANCH * D_BRANCH, D_MODEL), BF16),
                 jax.ShapeDtypeStruct((D_MODEL, D_MODEL), BF16),
                 jax.ShapeDtypeStruct((D_MODEL, N_BRANCH * D_MODEL), BF16),
                 jax.ShapeDtypeStruct((D_MODEL, GATE_PAD), BF16),
                 jax.ShapeDtypeStruct((D_BRANCH, D_MODEL), BF16)]
    return args, in_specs, out_specs, out_shape


def _first_layer_prep_call(prm):
    args, in_specs, out_specs, out_shape = _layer_prep_specs(0, prm)
    return pl.pallas_call(
        functools.partial(_prepare_layer, layer=0),
        grid=(N_PREP_STEPS,),
        in_specs=in_specs,
        out_specs=out_specs,
        out_shape=out_shape,
        compiler_params=pltpu.CompilerParams(
            dimension_semantics=("arbitrary",),
            vmem_limit_bytes=_vmem_request(list(zip(args, in_specs)) + list(zip(out_shape, out_specs)), [])),
        name="prep_layer0",
    )(*args)


def _seg_scan(x, pos, seg, total, op, reverse):
    d = 1
    while d < seg:
        if reverse:
            shifted = pltpu.roll(x, total - d, 1)
            valid = pos < seg - d
        else:
            shifted = pltpu.roll(x, d, 1)
            valid = pos >= d
        if op == "sum":
            x = x + jnp.where(valid, shifted, 0.0)
        else:
            x = jnp.maximum(x, jnp.where(valid, shifted, -jnp.inf))
        d *= 2
    return x


_MIXER_LAYER_ROW_INPUTS = (2, 10, 11, 14, 15)
_MIXER_LATE_INPUTS = (6, 7, 8)
_MIXER_MAIN_GROUPS = (0, 1, 2, 3, 5, 6)


def _mixer_kernel(*refs, T, grid2d, has_state, layer, n_in, n_pass, n_out, stream_main):
    ins, outs, scr = list(refs[:n_in - n_pass]), refs[n_in:n_in + n_out], refs[n_in + n_out:]
    *scr, sem = scr
    copies = {}

    def add_copy(key, src, dst):
        copies[key] = pltpu.make_async_copy(src, dst, sem.at[len(copies)])

    if stream_main:
        *scr, main_buf = scr
        for k in _MIXER_MAIN_GROUPS:
            cols = pl.ds(k * D_BRANCH, D_BRANCH)
            add_copy(f"main{k}", ins[3].at[layer, :, cols], main_buf.at[:, cols])
        ins[3] = main_buf
    scr, (brg_buf, branch_buf, out_buf) = scr[:-3], scr[-3:]
    brg_hbm, branch_hbm, out_hbm = (ins[k] for k in _MIXER_LATE_INPUTS)
    ab_cols, ab_rows = pl.ds(0, 2 * D_MODEL), pl.ds(0, 2 * D_BRANCH)
    c_cols, c_rows = pl.ds(2 * D_MODEL, D_MODEL), pl.ds(2 * D_BRANCH, D_BRANCH)
    add_copy("brg_ab", brg_hbm.at[:, ab_cols], brg_buf.at[:, ab_cols])
    add_copy("branch_ab", branch_hbm.at[ab_rows, :], branch_buf.at[ab_rows, :])
    add_copy("brg_c", brg_hbm.at[:, c_cols], brg_buf.at[:, c_cols])
    add_copy("branch_c", branch_hbm.at[c_rows, :], branch_buf.at[c_rows, :])
    add_copy("out", out_hbm, out_buf)
    assert len(copies) == sem.shape[0]
    for k, buf in zip(_MIXER_LATE_INPUTS, (brg_buf, branch_buf, out_buf)):
        ins[k] = buf
    first_step = pl.program_id(0) == 0

    def start_copies():
        for cp in copies.values():
            cp.start()

    def waiter(conditional):
        pending = dict(copies)

        def wait(key):
            cp = pending.pop(key, None)
            if cp is None:
                return
            if conditional:
                pl.when(first_step)(cp.wait)
            else:
                cp.wait()
        return wait

    ins[1] = ins[1].at[pl.ds(1 + pl.program_id(0) if has_state else 0, 1), :]
    for k in _MIXER_LAYER_ROW_INPUTS:
        ins[k] = ins[k].at[pl.ds(layer, 1), :]
    state_outs = list(outs[1:])
    if state_outs and n_pass == 0:
        for k, o in enumerate(state_outs):
            o[...] = jnp.zeros(o.shape, o.dtype)
            state_outs[k] = o.at[layer]
    def run(wait):
        for thunk in _mixer_sequence(*ins, outs[0], *state_outs, *scr, T=T, grid2d=grid2d, has_state=has_state,
                                     wait=wait):
            thunk()

    if stream_main:
        @pl.when(first_step)
        def _():
            start_copies()
            run(waiter(conditional=False))

        @pl.when(jnp.logical_not(first_step))
        def _():
            run(lambda key: None)
    else:
        pl.when(first_step)(start_copies)
        run(waiter(conditional=True))


def _mixer_sequence(*refs, T, grid2d, has_state, wait):
    it = iter(refs)
    x_ref = next(it)
    mod_ref = next(it)
    g1_ref = next(it)
    wmain_ref = next(it)
    wkt_ref = next(it)
    wgate_ref = next(it)
    wbrg_ref = next(it)
    wbranch_ref = next(it)
    wout_ref = next(it)
    wpool_ref = next(it)
    pscale_ref = next(it)
    gsgu_ref = next(it)
    wsgu_ref = next(it)
    bsgu_ref = next(it)
    bgates_ref = next(it)
    gml_ref = next(it)
    if has_state:
        c0_ref, n0_ref, m0_ref = next(it), next(it), next(it)
    out_ref = next(it)
    if not has_state:
        cout_ref, nout_ref, mout_ref = next(it), next(it), next(it)
    (u_scr, xp_scr, su_scr, vn_scr, q_scr, kt_scr, vaug_scr, o_scr, grow_scr, colf_scr,
     h_scr, ya_scr, yb_scr, yc_scr, mab_scr, gc_scr, merged_scr) = [next(it) for _ in range(17)]
    if has_state:
        cn_scr = next(it)
    if grid2d:
        m1_scr, dlt_scr = next(it), next(it)

    L = MLSTM_L
    n_tiles = T // TILE_M
    n_chunks = T // L
    RB = MLSTM_ROW_BLOCK
    NW = MERGE_COLS
    n_merge = D_MODEL // NW

    def tile_rows(i):
        return pl.ds(_aligned(i * TILE_M, TILE_M), TILE_M)

    def a_norm(i):
        mod = mod_ref[...]
        sh1 = mod[:, 0:D_MODEL]
        sc1 = mod[:, D_MODEL:2 * D_MODEL]
        x = x_ref[tile_rows(i), :]
        ms = jnp.mean(x * x, axis=-1, keepdims=True)
        u = x * lax.rsqrt(ms + EPS) * g1_ref[...] * (1.0 + sc1) + sh1
        u_scr[tile_rows(i), :] = u.astype(BF16)

    def a_gates(i):
        lane = lax.broadcasted_iota(jnp.int32, (2 * N_HEADS, TILE_M), 1)
        is_fwd = lax.broadcasted_iota(jnp.int32, (2 * N_HEADS, TILE_M), 0) < N_HEADS
        bias = jnp.concatenate([bgates_ref[...], jnp.zeros((1, GATE_PAD - N_GATE), F32)], axis=1)
        gate_t = (_dot(u_scr[tile_rows(i), :], wgate_ref[...]) + bias).T
        gp = gate_t[0:N_GATE, :]
        i_all = gp[0:2 * N_HEADS, :]
        f_all = gp[2 * N_HEADS:4 * N_HEADS, :]
        logf = jnp.minimum(f_all, 0.0) - jnp.log1p(jnp.exp(-jnp.abs(f_all)))
        b_row = jnp.where(is_fwd, _seg_scan(logf, lane, L, L, "sum", False),
                          _seg_scan(logf, lane, L, L, "sum", True))
        g_row = i_all - b_row
        cm_row = jnp.where(is_fwd, _seg_scan(g_row, lane, L, L, "max", False),
                           _seg_scan(g_row, lane, L, L, "max", True))
        grow_scr[:, tile_rows(i)] = g_row
        stacked = jnp.concatenate(
            [b_row, cm_row, jnp.zeros((GATE_PAD - 4 * N_HEADS, TILE_M), F32)], axis=0)
        colf_scr[tile_rows(i), :] = stacked.T

    def proj(i, k):
        wait(f"main{k}")
        return _dot(u_scr[tile_rows(i), :], wmain_ref[:, k * D_BRANCH:(k + 1) * D_BRANCH])

    def a_xp(i):
        xp_scr[pl.ds(_aligned(i * TILE_M + POOL_PAD, 8), TILE_M), :] = proj(i, 0)

    def a_su(i):
        su_scr[...] = proj(i, 1)

    def a_sv(i):
        sv = proj(i, 2)
        vn = sv * lax.rsqrt(jnp.mean(sv * sv, axis=-1, keepdims=True) + EPS) * gsgu_ref[...]
        vn_scr[...] = vn.astype(BF16)

    def a_q(i):
        q_scr[tile_rows(i), :] = proj(i, 3).astype(BF16)

    def a_k(i):
        kt = _dot_nt(wkt_ref[...], u_scr[tile_rows(i), :])
        kt_scr[:, tile_rows(i)] = (kt * (HEAD_DIM ** -0.5)).astype(BF16)

    def a_v(i):
        v = proj(i, 5).astype(BF16)
        ones_blk = jnp.ones((TILE_M, HEAD_DIM), BF16)
        for h in range(N_HEADS):
            vaug_scr[tile_rows(i), 2 * h * HEAD_DIM:(2 * h + 1) * HEAD_DIM] = v[:, h * HEAD_DIM:(h + 1) * HEAD_DIM]
            vaug_scr[tile_rows(i), (2 * h + 1) * HEAD_DIM:(2 * h + 2) * HEAD_DIM] = ones_blk

    def a_o(i):
        o_scr[tile_rows(i), :] = proj(i, 6)

    def a_sgu(ch, i):
        crow = slice(ch * SGU_CHUNK, (ch + 1) * SGU_CHUNK)
        for g in range(N_GROUPS):
            gc = slice(g * GROUP, (g + 1) * GROUP)
            mixed = _dot(wsgu_ref[g], vn_scr[crow, gc]) + bsgu_ref[:, gc]
            yb = su_scr[crow, gc] * mixed
            yb_scr[pl.ds(_aligned(i * TILE_M + ch * SGU_CHUNK, SGU_CHUNK), SGU_CHUNK), gc] = yb.astype(BF16)

    a_parts = [a_norm, a_gates, a_xp, a_su, a_sv, a_q, a_k, a_v, a_o]
    a_parts += [functools.partial(a_sgu, ch) for ch in range(TILE_M // SGU_CHUNK)]

    def pool_pad():
        zpad = jnp.zeros((POOL_PAD, D_BRANCH), F32)
        xp_scr[0:POOL_PAD, :] = zpad
        xp_scr[T + POOL_PAD:T + 2 * POOL_PAD, :] = zpad

    def pool_seq(g):
        win = POOL_WINDOWS[g]
        gc = slice(g * GROUP, (g + 1) * GROUP)
        half = win // 2
        t_idx = lax.broadcasted_iota(jnp.int32, (T, GROUP), 0)
        total = jnp.zeros((T, GROUP), F32)
        for j in range(-half, win - half):
            total = total + xp_scr[POOL_PAD + j:POOL_PAD + j + T, gc]
        cnt = (jnp.minimum(t_idx + (win - half), T) - jnp.maximum(t_idx - half, 0)).astype(F32)
        dlt = total / cnt - xp_scr[POOL_PAD:POOL_PAD + T, gc]
        ya = _dot(dlt.astype(BF16), wpool_ref[g]) * pscale_ref[...][:, gc]
        ya_scr[:, gc] = ya.astype(BF16)

    def a_colpool(i):
        c_idx = lax.broadcasted_iota(jnp.int32, (GRID_W, GROUP), 0)
        for g, win in enumerate(POOL_WINDOWS):
            gc = slice(g * GROUP, (g + 1) * GROUP)
            half = win // 2
            inv_cnt = 1.0 / (jnp.minimum(c_idx + (win - half), GRID_W) - jnp.maximum(c_idx - half, 0)).astype(F32)
            for r in range(TILE_M // GRID_W):
                r0 = i * TILE_M + r * GRID_W
                xg = xp_scr[pl.ds(_aligned(r0 + POOL_PAD, 8), GRID_W), gc]
                total = xg
                for j in range(-half, win - half):
                    if j != 0:
                        valid = (c_idx + j >= 0) & (c_idx + j < GRID_W)
                        total = total + jnp.where(valid, pltpu.roll(xg, (-j) % GRID_W, 0), 0.0)
                m1_scr[pl.ds(_aligned(r0, GRID_W), GRID_W), gc] = total * inv_cnt

    def pool_grid_rows(g):
        win = POOL_WINDOWS[g]
        gc = slice(g * GROUP, (g + 1) * GROUP)
        half = win // 2
        n_rows = T // GRID_W
        for r in range(n_rows):
            lo = max(r - half, 0)
            hi = min(r + (win - half), n_rows)
            acc = m1_scr[lo * GRID_W:(lo + 1) * GRID_W, gc]
            for rr in range(lo + 1, hi):
                acc = acc + m1_scr[rr * GRID_W:(rr + 1) * GRID_W, gc]
            xg = xp_scr[POOL_PAD + r * GRID_W:POOL_PAD + (r + 1) * GRID_W, gc]
            dlt_scr[r * GRID_W:(r + 1) * GRID_W, :] = acc / float(hi - lo) - xg
        ya = _dot(dlt_scr[...].astype(BF16), wpool_ref[g]) * pscale_ref[...][:, gc]
        ya_scr[:, gc] = ya.astype(BF16)

    if grid2d:
        assert TILE_M % GRID_W == 0
        a_parts.insert(a_parts.index(a_xp) + 1, a_colpool)
        pool_parts = [functools.partial(pool_grid_rows, g) for g in range(N_GROUPS)]
    else:
        pool_parts = [pool_pad] + [functools.partial(pool_seq, g) for g in range(N_GROUPS)]

    def keep_mask(d, rb, col_lo, ncols):
        t_loc = rb * RB + lax.broadcasted_iota(jnp.int32, (RB, ncols), 0)
        s_loc = col_lo + lax.broadcasted_iota(jnp.int32, (RB, ncols), 1)
        return s_loc <= t_loc if d == 0 else s_loc >= t_loc

    def mlstm_prepare(d, c, m_in):
        c0 = _aligned(c * L, L)
        colf = colf_scr[pl.ds(c0, L), :]
        bcol = colf[:, N_HEADS * d:N_HEADS * (d + 1)]
        cmcol = colf[:, 2 * N_HEADS + N_HEADS * d:2 * N_HEADS + N_HEADS * (d + 1)]
        mx = jnp.maximum(m_in, cmcol)
        last = L - 1 if d == 0 else 0
        mx_last = mx[last:last + 1, :]
        return dict(c0=c0, mx=mx, mx_last=mx_last, w_inter=jnp.exp(m_in - mx), e_negm=jnp.exp(-(bcol + mx)),
                    decay=jnp.exp(m_in - mx_last), m_new=bcol[last:last + 1, :] + mx_last)

    def block_cols(d, rb):
        return (0, (rb + 1) * RB) if d == 0 else (rb * RB, L)

    def mlstm_scores(d, p, h):
        c0 = p["c0"]
        hc = slice(h * HEAD_DIM, (h + 1) * HEAD_DIM)
        qks = []
        for rb in range(L // RB):
            col_lo, col_hi = block_cols(d, rb)
            qb = q_scr[pl.ds(_aligned(c0 + rb * RB, RB), RB), hc]
            qks.append(_dot(qb, kt_scr[hc, pl.ds(_aligned(c0 + col_lo, RB), col_hi - col_lo)]))
        p["qk", h] = qks

    def mlstm_head(d, p, h):
        c0, mx, mx_last = p["c0"], p["mx"], p["mx_last"]
        rows = pl.ds(c0, L)
        hc = slice(h * HEAD_DIM, (h + 1) * HEAD_DIM)
        ac = slice(2 * h * HEAD_DIM, (2 * h + 2) * HEAD_DIM)
        grow = grow_scr[N_HEADS * d + h:N_HEADS * d + h + 1, rows]
        if has_state:
            cn_dh = cn_scr[d, h]
            cn_bf = cn_dh.astype(BF16)
        for rb in range(L // RB):
            col_lo, col_hi = block_cols(d, rb)
            ncols = col_hi - col_lo
            trow = slice(rb * RB, (rb + 1) * RB)
            rrows = pl.ds(_aligned(c0 + rb * RB, RB), RB)
            crows = pl.ds(_aligned(c0 + col_lo, RB), ncols)
            qb = q_scr[rrows, hc]
            qk = p["qk", h][rb]
            w = jnp.exp(grow[:, col_lo:col_hi] - mx[trow, h:h + 1])
            s = (qk * jnp.where(keep_mask(d, rb, col_lo, ncols), w, 0.0)).astype(BF16)
            na = _dot(s, vaug_scr[crows, ac])
            if has_state:
                na = na + p["w_inter"][trow, h:h + 1] * _dot(qb, cn_bf)
            hval = na[:, 0:HEAD_DIM] / jnp.maximum(jnp.abs(na[:, HEAD_DIM:]), p["e_negm"][trow, h:h + 1])
            if d == 0:
                h_scr[rrows, hc] = hval
            else:
                h_scr[rrows, hc] += hval
        w_state = jnp.exp(grow - mx_last[:, h:h + 1])
        kw_t = (kt_scr[hc, rows].astype(F32) * w_state).astype(BF16)
        upd = _dot(kw_t, vaug_scr[rows, ac])
        if has_state:
            cn_scr[d, h] = p["decay"][:, h:h + 1] * cn_dh + upd
        else:
            cout_ref[d, h] = upd[:, 0:HEAD_DIM]
            nout_ref[d, h:h + 1, :] = upd[:, HEAD_DIM:].T[0:1, :]

    def merge_ab(c, nj):
        wait("brg_ab")
        wait("branch_ab")
        nc = slice(nj * NW, (nj + 1) * NW)
        ub = u_scr[tile_rows(c), :]
        acc = None
        for r, y_scr in enumerate((ya_scr, yb_scr)):
            gate = jax.nn.sigmoid(_dot(ub, wbrg_ref[:, r * D_MODEL + nj * NW:r * D_MODEL + (nj + 1) * NW]))
            term = gate * _dot(y_scr[tile_rows(c), :], wbranch_ref[r * D_BRANCH:(r + 1) * D_BRANCH, nc])
            acc = term if acc is None else acc + term
        mab_scr[tile_rows(c), nc] = acc

    def gate_c(c, nj):
        wait("brg_c")
        w = wbrg_ref[:, 2 * D_MODEL + nj * NW:2 * D_MODEL + (nj + 1) * NW]
        gc_scr[:, nj * NW:(nj + 1) * NW] = jax.nn.sigmoid(_dot(u_scr[tile_rows(c), :], w))

    def tail_yc(c):
        gml = gml_ref[...]
        for h in range(N_HEADS):
            hc = slice(h * HEAD_DIM, (h + 1) * HEAD_DIM)
            hh = h_scr[tile_rows(c), hc]
            hn = hh * lax.rsqrt(jnp.mean(hh * hh, axis=-1, keepdims=True) + EPS) * gml[:, hc]
            yc_scr[:, hc] = (jax.nn.sigmoid(o_scr[tile_rows(c), hc]) * hn).astype(BF16)

    def tail_merge(c, nj):
        wait("branch_c")
        nc = slice(nj * NW, (nj + 1) * NW)
        acc = mab_scr[tile_rows(c), nc] + gc_scr[:, nc] * _dot(yc_scr[...], wbranch_ref[2 * D_BRANCH:, nc])
        merged_scr[:, nc] = acc.astype(BF16)

    def tail_out(c):
        wait("out")
        gt1 = mod_ref[...][:, 2 * D_MODEL:3 * D_MODEL]
        out_ref[tile_rows(c), :] = x_ref[tile_rows(c), :] + gt1 * _dot(merged_scr[...], wout_ref[...])

    def tail_parts(c):
        return ([functools.partial(tail_yc, c)] + [functools.partial(tail_merge, c, nj) for nj in range(n_merge)]
                + [functools.partial(tail_out, c)])

    def scan_parts(d, c, m_in, fillers, out):
        p = {}

        def prepare():
            p.update(mlstm_prepare(d, c, m_in))
            out[d] = p["m_new"]

        parts = [prepare]
        for h in range(N_HEADS):
            parts.append(functools.partial(mlstm_scores, d, p, h))
            if h < len(fillers):
                parts.append(fillers[h])
            parts.append(functools.partial(mlstm_head, d, p, h))
        return parts + list(fillers[N_HEADS:])

    if n_tiles == 1 and n_chunks == 1:
        zeros = jnp.zeros((1, N_HEADS), F32)
        m_out = {}
        parts = [functools.partial(part, 0) for part in a_parts] + pool_parts
        parts += scan_parts(0, 0, zeros, [functools.partial(merge_ab, 0, nj) for nj in range(n_merge)], m_out)
        parts += scan_parts(1, 0, zeros, [functools.partial(gate_c, 0, nj) for nj in range(n_merge)], m_out)
        parts += tail_parts(0)

        def write_m():
            mout_ref[0:1, :] = m_out[0]
            mout_ref[1:2, :] = m_out[1]

        return parts + [write_m]

    def run_loops():
        def phase_a(i, carry):
            for part in a_parts:
                part(i)
            return carry

        lax.fori_loop(0, n_tiles, phase_a, 0)
        for part in pool_parts:
            part()
        for d in range(2):
            for h in range(N_HEADS):
                n_rows = jnp.broadcast_to(n0_ref[d, h:h + 1, :], (HEAD_DIM, HEAD_DIM))
                cn_scr[d, h] = jnp.concatenate([c0_ref[d, h], n_rows.T], axis=1)
        m_init = m0_ref[...]
        for key in ("brg_ab", "branch_ab", "brg_c", "branch_c", "out"):
            wait(key)

        def fwd_step(j, m_in):
            m_out = {}
            for part in scan_parts(0, j, m_in, [functools.partial(merge_ab, j, nj) for nj in range(n_merge)], m_out):
                part()
            return m_out[0]

        def bwd_step(j, m_in):
            c = n_chunks - 1 - j
            m_out = {}
            for part in scan_parts(1, c, m_in, [functools.partial(gate_c, c, nj) for nj in range(n_merge)], m_out):
                part()
            for part in tail_parts(c):
                part()
            return m_out[1]

        lax.fori_loop(0, n_chunks, fwd_step, m_init[0:1, :])
        lax.fori_loop(0, n_chunks, bwd_step, m_init[1:2, :])

    assert has_state, "multi-chunk sequences carry the cached state"
    return [run_loops]


def _mixer_call(l, xall, lay, prm, stream, states=None, state_outs=None):
    ctx = stream == "ctx"
    T = SEQ if ctx else DEC_SEQ
    assert T % TILE_M == 0 and TILE_M == MLSTM_L and MLSTM_L % MLSTM_ROW_BLOCK == 0
    n_b = BATCH if ctx else DEC_BATCH
    own_array = xall.shape[0] != N_TOK
    blk0 = 0 if ctx or own_array else N_TOK_CTX // T
    once = pl.Buffered(1)

    def wspec(shape):
        nd = len(shape)
        return pl.BlockSpec((None,) + shape, lambda b: (l,) + (0,) * nd, pipeline_mode=once)

    def lspec(shape):
        return pl.BlockSpec(shape, lambda b: (0,) * len(shape), pipeline_mode=once)

    x_spec = pl.BlockSpec((T, D_MODEL), lambda b: (blk0 + b, 0))
    in_specs = [
        x_spec,
        lspec((COND_ROWS, 6 * D_MODEL)),
        lspec((DEPTH, D_MODEL)),
        pl.BlockSpec(memory_space=pl.ANY) if ctx else wspec((D_MODEL, N_MAIN)),
        lspec((D_BRANCH, D_MODEL)),
        lspec((D_MODEL, GATE_PAD)),
        pl.BlockSpec(memory_space=pl.ANY),
        pl.BlockSpec(memory_space=pl.ANY),
        pl.BlockSpec(memory_space=pl.ANY),
        wspec((N_GROUPS, GROUP, GROUP)),
        lspec((DEPTH, D_BRANCH)),
        lspec((DEPTH, D_BRANCH)),
        wspec((N_GROUPS, SGU_CHUNK, SGU_CHUNK)),
        wspec((SGU_CHUNK, D_BRANCH)),
        lspec((DEPTH, N_GATE)),
        lspec((DEPTH, D_BRANCH)),
    ]
    args = [xall, lay["mods"], prm["g_norm1"], prm["w_in16"], lay["w_kt"],
            lay["w_gate"], lay["w_brg"], lay["w_branch"], lay["w_out"],
            prm["w_pool"], prm["pool_scale"], prm["g_sgu"], prm["w_sgu"],
            prm["b_sgu_tile"], prm["b_gates"], prm["g_mlstm"]]
    out_shape = [jax.ShapeDtypeStruct(xall.shape, xall.dtype)]
    out_specs = [pl.BlockSpec((T, D_MODEL), lambda b: (blk0 + b, 0), pipeline_mode=None if ctx else once)]
    aliases = {} if own_array else {0: 0}
    passthrough = []
    if ctx:
        per_seq = [(2, N_HEADS, HEAD_DIM, HEAD_DIM), (2, N_HEADS, HEAD_DIM), (2, N_HEADS)]
        out_shape += [jax.ShapeDtypeStruct((BATCH, DEPTH) + s, F32) for s in per_seq]
        if state_outs is None:
            out_specs += [pl.BlockSpec((None, DEPTH) + s, lambda b, nd=len(s): (b,) + (0,) * (nd + 1))
                          for s in per_seq]
        else:
            out_specs += [pl.BlockSpec((None, None) + s, lambda b, nd=len(s): (b, l) + (0,) * nd) for s in per_seq]
            passthrough += [(a, 1 + k) for k, a in enumerate(state_outs)]
    else:
        in_specs += [
            pl.BlockSpec((None, None, 2, N_HEADS, HEAD_DIM, HEAD_DIM), lambda b: (b, l, 0, 0, 0, 0)),
            pl.BlockSpec((None, None, 2, N_HEADS, HEAD_DIM), lambda b: (b, l, 0, 0, 0)),
            pl.BlockSpec((None, None, 2, N_HEADS), lambda b: (b, l, 0, 0)),
        ]
        args += list(states)
    for buf, out_idx in passthrough:
        aliases[len(args)] = out_idx
        in_specs.append(pl.BlockSpec(memory_space=pl.ANY))
        args.append(buf)

    scratch = [
        pltpu.VMEM((T, D_MODEL), BF16),
        pltpu.VMEM((T + 2 * POOL_PAD, D_BRANCH), F32),
        pltpu.VMEM((TILE_M, D_BRANCH), F32),
        pltpu.VMEM((TILE_M, D_BRANCH), BF16),
        pltpu.VMEM((T, D_BRANCH), BF16),
        pltpu.VMEM((D_BRANCH, T), BF16),
        pltpu.VMEM((T, 2 * D_BRANCH), BF16),
        pltpu.VMEM((T, D_BRANCH), F32),
        pltpu.VMEM((2 * N_HEADS, T), F32),
        pltpu.VMEM((T, GATE_PAD), F32),
        pltpu.VMEM((T, D_BRANCH), F32),
        pltpu.VMEM((T, D_BRANCH), BF16),
        pltpu.VMEM((T, D_BRANCH), BF16),
        pltpu.VMEM((TILE_M, D_BRANCH), BF16),
        pltpu.VMEM((T, D_MODEL), F32),
        pltpu.VMEM((TILE_M, D_MODEL), F32),
        pltpu.VMEM((TILE_M, D_MODEL), BF16),
    ]
    if not ctx:
        scratch += [pltpu.VMEM((2, N_HEADS, HEAD_DIM, 2 * HEAD_DIM), F32)]
        scratch += [pltpu.VMEM((T, D_BRANCH), F32), pltpu.VMEM((T, GROUP), F32)]
    scratch += [pltpu.VMEM(args[k].shape, args[k].dtype) for k in _MIXER_LATE_INPUTS]
    n_copies = 5
    if ctx:
        scratch += [pltpu.VMEM((D_MODEL, N_MAIN), BF16)]
        n_copies += len(_MIXER_MAIN_GROUPS)
    late_sem = [pltpu.SemaphoreType.DMA((n_copies,))]

    outs = pl.pallas_call(
        functools.partial(_mixer_kernel, T=T, grid2d=not ctx, has_state=not ctx, layer=l,
                          n_in=len(in_specs), n_pass=len(passthrough), n_out=len(out_specs), stream_main=ctx),
        grid=(n_b,),
        in_specs=in_specs,
        out_specs=out_specs,
        out_shape=out_shape,
        scratch_shapes=scratch + late_sem,
        input_output_aliases=aliases,
        compiler_params=pltpu.CompilerParams(
            dimension_semantics=("arbitrary",),
            vmem_limit_bytes=_vmem_request(list(zip(args, in_specs)) + list(zip(out_shape, out_specs)), scratch)),
        name="mixer_ctx" if ctx else "mixer_lat",
    )(*args)
    return outs


def _ffn_kernel(*refs, layer, final, split_input):
    n_x = 2 if split_input else 1
    x_refs, (mod_ref, g2_ref, w1_hbm, w2_hbm, gfin_ref), rest = refs[:n_x], refs[n_x:n_x + 5], refs[n_x + 5:]
    if final:
        out_refs, rest = rest[:2], rest[2:]
    else:
        prep_in, rest = rest[:N_PREP_OPERANDS], rest[N_PREP_OPERANDS:]
        out_refs, prep_out, rest = rest[:1], rest[1:1 + len(LAYER_ARRAYS)], rest[1 + len(LAYER_ARRAYS):]
    acc_scr, h_scr, w1_scr, w2_scr, st1_scr, st2_scr, sem = rest[:7]
    n_chunks = D_FF // FFN_CHUNK
    tile = pl.program_id(0)
    first_tile = tile == 0
    ctx_tiles, lat_tiles = N_TOK_CTX // FFN_TILE_M, DEC_SEQ // FFN_TILE_M
    if split_input:
        x_ref = rest[7]

        @pl.when(tile < ctx_tiles)
        def _():
            x_ref[...] = x_refs[0][...]

        @pl.when(tile >= ctx_tiles)
        def _():
            x_ref[...] = x_refs[1][...]
    else:
        x_ref = x_refs[0]
    mod_ref = mod_ref.at[pl.ds(jnp.where(tile < ctx_tiles, 0, 1 + (tile - ctx_tiles) // lat_tiles), 1), :]
    g2_ref = g2_ref.at[pl.ds(layer, 1), :]

    def w1_copy(c):
        slot = c % FFN_STAGES
        c0 = _aligned(c * FFN_CHUNK, FFN_CHUNK)
        return pltpu.make_async_copy(w1_hbm.at[layer, :, pl.ds(c0, FFN_CHUNK)], st1_scr.at[slot], sem.at[0, slot])

    def w2_copy(c):
        slot = c % FFN_STAGES
        c0 = _aligned(c * FFN_CHUNK, FFN_CHUNK)
        return pltpu.make_async_copy(w2_hbm.at[layer, pl.ds(c0, FFN_CHUNK), :], st2_scr.at[slot], sem.at[1, slot])

    def land_w1(c):
        w1_copy(c).wait()
        c0 = _aligned(c * FFN_CHUNK, FFN_CHUNK)
        w1_scr[:, pl.ds(c0, FFN_CHUNK)] = st1_scr[c % FFN_STAGES].astype(BF16)

    def land_w2(c):
        w2_copy(c).wait()
        c0 = _aligned(c * FFN_CHUNK, FFN_CHUNK)
        w2_scr[pl.ds(c0, FFN_CHUNK), :] = st2_scr[c % FFN_STAGES].astype(BF16)

    def prepare_next_layer():
        if not final:
            _prepare_layer(*prep_in, *prep_out, layer=layer + 1)

    def normed_input():
        mod = mod_ref[...]
        sh2 = mod[:, 3 * D_MODEL:4 * D_MODEL]
        sc2 = mod[:, 4 * D_MODEL:5 * D_MODEL]
        x = x_ref[...]
        ms = jnp.mean(x * x, axis=-1, keepdims=True)
        return (x * lax.rsqrt(ms + EPS) * g2_ref[...] * (1.0 + sc2) + sh2).astype(BF16)

    def hidden(ub, c):
        c0 = _aligned(c * FFN_CHUNK, FFN_CHUNK)
        h = _dot(ub, w1_scr[:, pl.ds(c0, FFN_CHUNK)])
        h_scr[:, pl.ds(c0, FFN_CHUNK)] = jnp.square(jnp.maximum(h, 0.0)).astype(BF16)

    def finish(ffn_out):
        gt2 = mod_ref[...][:, 5 * D_MODEL:6 * D_MODEL]
        y = x_ref[...] + gt2 * ffn_out
        if not final:
            out_refs[0][...] = y
            return
        y = y * lax.rsqrt(jnp.mean(y * y, axis=-1, keepdims=True) + EPS) * gfin_ref[...]
        in_ctx = tile < ctx_tiles

        @pl.when(in_ctx)
        def _():
            out_refs[0][...] = y

        @pl.when(jnp.logical_not(in_ctx))
        def _():
            out_refs[1][...] = y

    @pl.when(first_tile)
    def _():
        for c in range(FFN_STAGES):
            w1_copy(c).start()
            w2_copy(c).start()
        prepare_next_layer()
        ub = normed_input()
        acc_scr[...] = jnp.zeros_like(acc_scr)

        def chunk(c, carry):
            land_w1(c)
            land_w2(c)

            @pl.when(c + FFN_STAGES < n_chunks)
            def _():
                w1_copy(c + FFN_STAGES).start()
                w2_copy(c + FFN_STAGES).start()

            hidden(ub, c)
            c0 = _aligned(c * FFN_CHUNK, FFN_CHUNK)
            acc_scr[...] += _dot(h_scr[:, pl.ds(c0, FFN_CHUNK)], w2_scr[pl.ds(c0, FFN_CHUNK), :])
            return carry

        lax.fori_loop(0, n_chunks, chunk, 0)
        finish(acc_scr[...])

    @pl.when(jnp.logical_not(first_tile))
    def _():
        prepare_next_layer()
        ub = normed_input()
        for c in range(n_chunks):
            hidden(ub, c)
        finish(_dot(h_scr[...], w2_scr[...]))


def _ffn_call(l, xall, lay, prm, g_final, final):
    n_tiles = N_TOK // FFN_TILE_M
    ctx_tiles = N_TOK_CTX // FFN_TILE_M
    once = pl.Buffered(1)
    split_input = isinstance(xall, tuple)
    assert not (split_input and final)

    x_spec = pl.BlockSpec((FFN_TILE_M, D_MODEL), lambda t: (t, 0))
    ctx_spec = pl.BlockSpec((FFN_TILE_M, D_MODEL), lambda t: (jnp.minimum(t, ctx_tiles - 1), 0))
    lat_spec = pl.BlockSpec((FFN_TILE_M, D_MODEL), lambda t: (jnp.maximum(t - ctx_tiles, 0), 0))
    xs = list(xall) if split_input else [xall]
    in_specs = ([ctx_spec, lat_spec] if split_input else [x_spec]) + [
        pl.BlockSpec((COND_ROWS, 6 * D_MODEL), lambda t: (0, 0), pipeline_mode=once),
        pl.BlockSpec((DEPTH, D_MODEL), lambda t: (0, 0), pipeline_mode=once),
        pl.BlockSpec(memory_space=pl.ANY),
        pl.BlockSpec(memory_space=pl.ANY),
        pl.BlockSpec((1, D_MODEL), lambda t: (0, 0)),
    ]
    args = xs + [lay["mods"], prm["g_norm2"], prm["w_ff1"], prm["w_ff2"], g_final.reshape(1, D_MODEL)]
    if final:
        out_specs = [ctx_spec, lat_spec]
        out_shape = [jax.ShapeDtypeStruct((N_TOK_CTX, D_MODEL), F32),
                     jax.ShapeDtypeStruct((N_TOK - N_TOK_CTX, D_MODEL), F32)]
        aliases = {}
    else:
        assert n_tiles == N_PREP_STEPS
        prep_args, prep_in_specs, prep_out_specs, prep_out_shape = _layer_prep_specs(l + 1, prm)
        in_specs += prep_in_specs
        args += prep_args
        out_specs = [x_spec] + prep_out_specs
        out_shape = [jax.ShapeDtypeStruct((N_TOK, D_MODEL), F32)] + prep_out_shape
        aliases = {} if split_input else {0: 0}
    sems = [pltpu.SemaphoreType.DMA((2, FFN_STAGES))]
    gather = [pltpu.VMEM((FFN_TILE_M, D_MODEL), F32)] if split_input else []
    scratch = [
        pltpu.VMEM((FFN_TILE_M, D_MODEL), F32),
        pltpu.VMEM((FFN_TILE_M, D_FF), BF16),
        pltpu.VMEM((D_MODEL, D_FF), BF16),
        pltpu.VMEM((D_FF, D_MODEL), BF16),
        pltpu.VMEM((FFN_STAGES, D_MODEL, FFN_CHUNK), F32),
        pltpu.VMEM((FFN_STAGES, FFN_CHUNK, D_MODEL), F32),
    ]
    return pl.pallas_call(
        functools.partial(_ffn_kernel, layer=l, final=final, split_input=split_input),
        grid=(n_tiles,),
        in_specs=in_specs,
        out_specs=out_specs,
        out_shape=out_shape,
        scratch_shapes=scratch + sems + gather,
        input_output_aliases=aliases,
        compiler_params=pltpu.CompilerParams(
            dimension_semantics=("arbitrary",),
            vmem_limit_bytes=_vmem_request(list(zip(args, in_specs)) + list(zip(out_shape, out_specs)),
                                           scratch + gather)),
        name="ffn_final" if final else "ffn",
    )(*args)


def kernel(x_prompt, x_sample, state_C, state_n, state_m, c, c_ctx, w_ada, b_ada, g_norm1, g_norm2, w_in,
           b_gates, w_pool, pool_scale, g_sgu, w_sgu, b_sgu, g_mlstm, w_branch, w_out, w_ff1, w_ff2, g_final):
    w_in16 = w_in.astype(BF16)
    prm = {
        "w_in16": w_in16,
        "w_branch_rows": w_branch.reshape(DEPTH, N_BRANCH * D_BRANCH, D_MODEL),
        "w_out": w_out,
        "w_ada": w_ada,
        "b_ada": b_ada,
        "w_pool": w_pool.astype(BF16),
        "w_sgu": w_sgu.astype(BF16),
        "w_ff1": w_ff1,
        "w_ff2": w_ff2,
        "g_norm1": g_norm1,
        "g_norm2": g_norm2,
        "pool_scale": pool_scale,
        "g_sgu": g_sgu,
        "g_mlstm": g_mlstm,
        "b_gates": b_gates,
        "b_sgu_tile": jnp.repeat(jnp.swapaxes(b_sgu, 1, 2), GROUP, axis=2),
        "cond8": jnp.concatenate(
            [c_ctx[None, :], c, jnp.zeros((COND_ROWS - 1 - DEC_BATCH, D_MODEL), F32)], axis=0),
    }
    lay = dict(zip(LAYER_ARRAYS, _first_layer_prep_call(prm)))

    xc, *new_states = _mixer_call(0, x_prompt.reshape(N_TOK_CTX, D_MODEL), lay, prm, "ctx")
    (xs,) = _mixer_call(0, x_sample.reshape(-1, D_MODEL), lay, prm, "lat", states=(state_C, state_n, state_m))
    xall = (xc, xs)
    for l in range(DEPTH):
        if l > 0:
            xall, *new_states = _mixer_call(l, xall, lay, prm, "ctx", state_outs=new_states)
            (xall,) = _mixer_call(l, xall, lay, prm, "lat", states=(state_C, state_n, state_m))
        if l < DEPTH - 1:
            xall, *nxt = _ffn_call(l, xall, lay, prm, g_final, final=False)
            lay = dict(zip(LAYER_ARRAYS, nxt))
    y_prompt, y_sample = _ffn_call(DEPTH - 1, xall, lay, prm, g_final, final=True)
    return (y_prompt.reshape(BATCH, SEQ, D_MODEL), y_sample.reshape(DEC_BATCH, DEC_SEQ, D_MODEL), *new_states)
```

```python
import functools
import math

import jax
import jax.numpy as jnp
from jax import lax
from jax.experimental import pallas as pl
from jax.experimental.pallas import tpu as pltpu

F32 = jnp.float32
BF16 = jnp.bfloat16

D_MODEL = 1024
DEPTH = 4
BATCH = 16
SEQ = 256
DEC_BATCH = 2
DEC_SEQ = 1024
GRID_W = 64
D_BRANCH = 512
POOL_WINDOWS = (2, 4, 8, 16)
N_GROUPS = 4
GROUP = D_BRANCH // N_GROUPS
SGU_CHUNK = 128
N_HEADS = 4
HEAD_DIM = D_BRANCH // N_HEADS
N_BRANCH = 3
D_FF = 4 * D_MODEL
EPS = 1e-6

LANES = 128
BF16_SUBLANES = 16

N_MAIN = 7 * D_BRANCH
N_GATE = 4 * N_HEADS
GATE_PAD = LANES
N_TOK_CTX = BATCH * SEQ
N_TOK = N_TOK_CTX + DEC_BATCH * DEC_SEQ
COND_ROWS = 8

TILE_M = 256
MLSTM_L = 256
MLSTM_ROW_BLOCK = 128
MERGE_COLS = 256
POOL_PAD = 8
FFN_TILE_M = 512
FFN_CHUNK = 512
FFN_STAGES = 2
V7X_VMEM_BYTES = 64 * 1024 * 1024
VMEM_COMPILER_RESERVE = 6 * 1024 * 1024
VMEM_CALL_FLOOR = 60 * 1024 * 1024


def _vmem_request(windows, scratch):
    total = VMEM_COMPILER_RESERVE
    for arr, spec in windows:
        if spec.block_shape is not None:
            n_buf = 2 if spec.pipeline_mode is None else spec.pipeline_mode.buffer_count
            total += n_buf * math.prod(d for d in spec.block_shape if d is not None) * jnp.dtype(arr.dtype).itemsize
    for buf in scratch:
        total += math.prod(buf.shape) * jnp.dtype(buf.dtype).itemsize
    return min(max(total, VMEM_CALL_FLOOR), V7X_VMEM_BYTES)


_dot = functools.partial(jnp.dot, preferred_element_type=F32)
_dot_nt = functools.partial(lax.dot_general, dimension_numbers=(((1,), (1,)), ((), ())),
                            preferred_element_type=F32)


def _aligned(offset, align):
    return offset if isinstance(offset, int) else pl.multiple_of(offset, align)


LAYER_ARRAYS = ("mods", "w_branch", "w_out", "w_brg", "w_gate", "w_kt")
N_PREP_OPERANDS = 6
N_PREP_STEPS = N_TOK // FFN_TILE_M


def _prepare_layer(cond_ref, wada_ref, bada_ref, wbr_ref, wo_ref, win_ref,
                   mods_out, wbr_out, wo_out, wbrg_out, wgate_out, wkt_out, *, layer):
    cond = cond_ref[...]
    s = (cond * jax.nn.sigmoid(cond)).astype(BF16)
    mods_out[...] = _dot(s, wada_ref[...].astype(BF16)) + bada_ref[layer:layer + 1, :]
    wbr_out[...] = wbr_ref[...].astype(BF16)
    wo_out[...] = wo_ref[...].astype(BF16)
    n_br = N_BRANCH * D_MODEL
    for c0 in range(0, n_br, FFN_CHUNK):
        width = min(FFN_CHUNK + GATE_PAD, n_br + N_GATE - c0)
        window = win_ref[:, N_MAIN + c0:N_MAIN + c0 + width]
        wbrg_out[:, c0:c0 + FFN_CHUNK] = window[:, N_GATE:N_GATE + FFN_CHUNK]
    gate_tile = win_ref[:, N_MAIN:N_MAIN + GATE_PAD]
    is_gate = lax.broadcasted_iota(jnp.int32, gate_tile.shape, 1) < N_GATE
    wgate_out[...] = jnp.where(is_gate, gate_tile, jnp.zeros_like(gate_tile))
    wkt_out[...] = win_ref[:, 4 * D_BRANCH:5 * D_BRANCH].astype(F32).T.astype(BF16)


def _layer_prep_specs(layer, prm):
    ada_cols = 6 * D_MODEL // N_PREP_STEPS
    br_rows = N_BRANCH * D_BRANCH // N_PREP_STEPS
    wo_steps = D_MODEL // br_rows
    assert ada_cols % LANES == 0 and br_rows % BF16_SUBLANES == 0 and wo_steps <= N_PREP_STEPS
    wo_map = lambda t: (jnp.minimum(t, wo_steps - 1), 0)
    args = [prm["cond8"], prm["w_ada"], prm["b_ada"], prm["w_branch_rows"], prm["w_out"], prm["w_in16"]]
    in_specs = [
        pl.BlockSpec((COND_ROWS, D_MODEL), lambda t: (0, 0)),
        pl.BlockSpec((None, D_MODEL, ada_cols), lambda t: (layer, 0, t)),
        pl.BlockSpec((DEPTH, ada_cols), lambda t: (0, t)),
        pl.BlockSpec((None, br_rows, D_MODEL), lambda t: (layer, t, 0)),
        pl.BlockSpec((None, br_rows, D_MODEL), lambda t: (layer,) + wo_map(t)),
        pl.BlockSpec((None, br_rows, prm["w_in16"].shape[-1]), lambda t: (layer,) + wo_map(t)),
    ]
    out_specs = [
        pl.BlockSpec((COND_ROWS, ada_cols), lambda t: (0, t)),
        pl.BlockSpec((br_rows, D_MODEL), lambda t: (t, 0)),
        pl.BlockSpec((br_rows, D_MODEL), wo_map),
        pl.BlockSpec((br_rows, N_BRANCH * D_MODEL), wo_map),
        pl.BlockSpec((br_rows, GATE_PAD), wo_map),
        pl.BlockSpec((D_BRANCH, br_rows), lambda t: wo_map(t)[::-1]),
    ]
    out_shape = [jax.ShapeDtypeStruct((COND_ROWS, 6 * D_MODEL), F32),
                 jax.ShapeDtypeStruct((N_BRANCH * D_BRANCH, D_MODEL), BF16),
                 jax.ShapeDtypeStruct((D_MODEL, D_MODEL), BF16),
                 jax.ShapeDtypeStruct((D_MODEL, N_BRANCH * D_MODEL), BF16),
                 jax.ShapeDtypeStruct((D_MODEL, GATE_PAD), BF16),
                 jax.ShapeDtypeStruct((D_BRANCH, D_MODEL), BF16)]
    return args, in_specs, out_specs, out_shape


def _first_layer_prep_call(prm):
    args, in_specs, out_specs, out_shape = _layer_prep_specs(0, prm)
    return pl.pallas_call(
        functools.partial(_prepare_layer, layer=0),
        grid=(N_PREP_STEPS,),
        in_specs=in_specs,
        out_specs=out_specs,
        out_shape=out_shape,
        compiler_params=pltpu.CompilerParams(
            dimension_semantics=("arbitrary",),
            vmem_limit_bytes=_vmem_request(list(zip(args, in_specs)) + list(zip(out_shape, out_specs)), [])),
        name="prep_layer0",
    )(*args)


def _seg_scan(x, pos, seg, total, op, reverse):
    d = 1
    while d < seg:
        if reverse:
            shifted = pltpu.roll(x, total - d, 1)
            valid = pos < seg - d
        else:
            shifted = pltpu.roll(x, d, 1)
            valid = pos >= d
        if op == "sum":
            x = x + jnp.where(valid, shifted, 0.0)
        else:
            x = jnp.maximum(x, jnp.where(valid, shifted, -jnp.inf))
        d *= 2
    return x


_MIXER_LAYER_ROW_INPUTS = (2, 10, 11, 14, 15)
_MIXER_LATE_INPUTS = (6, 7, 8)
_MIXER_MAIN_GROUPS = (0, 1, 2, 3, 5, 6)


def _mixer_kernel(*refs, T, grid2d, has_state, layer, n_in, n_pass, n_out, stream_main):
    ins, outs, scr = list(refs[:n_in - n_pass]), refs[n_in:n_in + n_out], refs[n_in + n_out:]
    *scr, sem = scr
    copies = {}

    def add_copy(key, src, dst):
        copies[key] = pltpu.make_async_copy(src, dst, sem.at[len(copies)])

    if stream_main:
        *scr, main_buf = scr
        for k in _MIXER_MAIN_GROUPS:
            cols = pl.ds(k * D_BRANCH, D_BRANCH)
            add_copy(f"main{k}", ins[3].at[layer, :, cols], main_buf.at[:, cols])
        ins[3] = main_buf
    scr, (brg_buf, branch_buf, out_buf) = scr[:-3], scr[-3:]
    brg_hbm, branch_hbm, out_hbm = (ins[k] for k in _MIXER_LATE_INPUTS)
    ab_cols, ab_rows = pl.ds(0, 2 * D_MODEL), pl.ds(0, 2 * D_BRANCH)
    c_cols, c_rows = pl.ds(2 * D_MODEL, D_MODEL), pl.ds(2 * D_BRANCH, D_BRANCH)
    add_copy("brg_ab", brg_hbm.at[:, ab_cols], brg_buf.at[:, ab_cols])
    add_copy("branch_ab", branch_hbm.at[ab_rows, :], branch_buf.at[ab_rows, :])
    add_copy("brg_c", brg_hbm.at[:, c_cols], brg_buf.at[:, c_cols])
    add_copy("branch_c", branch_hbm.at[c_rows, :], branch_buf.at[c_rows, :])
    add_copy("out", out_hbm, out_buf)
    assert len(copies) == sem.shape[0]
    for k, buf in zip(_MIXER_LATE_INPUTS, (brg_buf, branch_buf, out_buf)):
        ins[k] = buf
    first_step = pl.program_id(0) == 0

    def start_copies():
        for n, cp in enumerate(copies.values()):
            cp.start(priority=n % 2)

    def waiter(conditional):
        pending = dict(copies)

        def wait(key):
            cp = pending.pop(key, None)
            if cp is None:
                return
            if conditional:
                pl.when(first_step)(cp.wait)
            else:
                cp.wait()
        return wait

    ins[1] = ins[1].at[pl.ds(1 + pl.program_id(0) if has_state else 0, 1), :]
    for k in _MIXER_LAYER_ROW_INPUTS:
        ins[k] = ins[k].at[pl.ds(layer, 1), :]
    state_outs = list(outs[1:])
    if state_outs and n_pass == 0:
        for k, o in enumerate(state_outs):
            o[...] = jnp.zeros(o.shape, o.dtype)
            state_outs[k] = o.at[layer]
    def run(wait):
        for thunk in _mixer_sequence(*ins, outs[0], *state_outs, *scr, T=T, grid2d=grid2d, has_state=has_state,
                                     wait=wait):
            thunk()

    if stream_main:
        @pl.when(first_step)
        def _():
            start_copies()
            run(waiter(conditional=False))

        @pl.when(jnp.logical_not(first_step))
        def _():
            run(lambda key: None)
    else:
        pl.when(first_step)(start_copies)
        run(waiter(conditional=True))


def _mixer_sequence(*refs, T, grid2d, has_state, wait):
    it = iter(refs)
    x_ref = next(it)
    mod_ref = next(it)
    g1_ref = next(it)
    wmain_ref = next(it)
    wkt_ref = next(it)
    wgate_ref = next(it)
    wbrg_ref = next(it)
    wbranch_ref = next(it)
    wout_ref = next(it)
    wpool_ref = next(it)
    pscale_ref = next(it)
    gsgu_ref = next(it)
    wsgu_ref = next(it)
    bsgu_ref = next(it)
    bgates_ref = next(it)
    gml_ref = next(it)
    if has_state:
        c0_ref, n0_ref, m0_ref = next(it), next(it), next(it)
    out_ref = next(it)
    if not has_state:
        cout_ref, nout_ref, mout_ref = next(it), next(it), next(it)
    (u_scr, xp_scr, su_scr, vn_scr, q_scr, kt_scr, vaug_scr, o_scr, grow_scr, colf_scr,
     h_scr, ya_scr, yb_scr, yc_scr, mab_scr, gc_scr, merged_scr) = [next(it) for _ in range(17)]
    if has_state:
        cn_scr = next(it)
    if grid2d:
        m1_scr, dlt_scr = next(it), next(it)

    L = MLSTM_L
    n_tiles = T // TILE_M
    n_chunks = T // L
    RB = MLSTM_ROW_BLOCK
    NW = MERGE_COLS
    n_merge = D_MODEL // NW

    def tile_rows(i):
        return pl.ds(_aligned(i * TILE_M, TILE_M), TILE_M)

    def a_norm(i):
        mod = mod_ref[...]
        sh1 = mod[:, 0:D_MODEL]
        sc1 = mod[:, D_MODEL:2 * D_MODEL]
        x = x_ref[tile_rows(i), :]
        ms = jnp.mean(x * x, axis=-1, keepdims=True)
        u = x * lax.rsqrt(ms + EPS) * g1_ref[...] * (1.0 + sc1) + sh1
        u_scr[tile_rows(i), :] = u.astype(BF16)

    def a_gates(i):
        lane = lax.broadcasted_iota(jnp.int32, (2 * N_HEADS, TILE_M), 1)
        is_fwd = lax.broadcasted_iota(jnp.int32, (2 * N_HEADS, TILE_M), 0) < N_HEADS
        bias = jnp.concatenate([bgates_ref[...], jnp.zeros((1, GATE_PAD - N_GATE), F32)], axis=1)
        gate_t = (_dot(u_scr[tile_rows(i), :], wgate_ref[...]) + bias).T
        gp = gate_t[0:N_GATE, :]
        i_all = gp[0:2 * N_HEADS, :]
        f_all = gp[2 * N_HEADS:4 * N_HEADS, :]
        logf = jnp.minimum(f_all, 0.0) - jnp.log1p(jnp.exp(-jnp.abs(f_all)))
        b_row = jnp.where(is_fwd, _seg_scan(logf, lane, L, L, "sum", False),
                          _seg_scan(logf, lane, L, L, "sum", True))
        g_row = i_all - b_row
        cm_row = jnp.where(is_fwd, _seg_scan(g_row, lane, L, L, "max", False),
                           _seg_scan(g_row, lane, L, L, "max", True))
        grow_scr[:, tile_rows(i)] = g_row
        stacked = jnp.concatenate(
            [b_row, cm_row, jnp.zeros((GATE_PAD - 4 * N_HEADS, TILE_M), F32)], axis=0)
        colf_scr[tile_rows(i), :] = stacked.T

    def proj(i, k):
        wait(f"main{k}")
        return _dot(u_scr[tile_rows(i), :], wmain_ref[:, k * D_BRANCH:(k + 1) * D_BRANCH])

    def a_xp(i):
        xp_scr[pl.ds(_aligned(i * TILE_M + POOL_PAD, 8), TILE_M), :] = proj(i, 0)

    def a_su(i):
        su_scr[...] = proj(i, 1)

    def a_sv(i):
        sv = proj(i, 2)
        vn = sv * lax.rsqrt(jnp.mean(sv * sv, axis=-1, keepdims=True) + EPS) * gsgu_ref[...]
        vn_scr[...] = vn.astype(BF16)

    def a_q(i):
        q_scr[tile_rows(i), :] = proj(i, 3).astype(BF16)

    def a_k(i):
        kt = _dot_nt(wkt_ref[...], u_scr[tile_rows(i), :])
        kt_scr[:, tile_rows(i)] = (kt * (HEAD_DIM ** -0.5)).astype(BF16)

    def a_v(i):
        v = proj(i, 5).astype(BF16)
        ones_blk = jnp.ones((TILE_M, HEAD_DIM), BF16)
        for h in range(N_HEADS):
            vaug_scr[tile_rows(i), 2 * h * HEAD_DIM:(2 * h + 1) * HEAD_DIM] = v[:, h * HEAD_DIM:(h + 1) * HEAD_DIM]
            vaug_scr[tile_rows(i), (2 * h + 1) * HEAD_DIM:(2 * h + 2) * HEAD_DIM] = ones_blk

    def a_o(i):
        o_scr[tile_rows(i), :] = proj(i, 6)

    def a_sgu(ch, i):
        crow = slice(ch * SGU_CHUNK, (ch + 1) * SGU_CHUNK)
        for g in range(N_GROUPS):
            gc = slice(g * GROUP, (g + 1) * GROUP)
            mixed = _dot(wsgu_ref[g], vn_scr[crow, gc]) + bsgu_ref[:, gc]
            yb = su_scr[crow, gc] * mixed
            yb_scr[pl.ds(_aligned(i * TILE_M + ch * SGU_CHUNK, SGU_CHUNK), SGU_CHUNK), gc] = yb.astype(BF16)

    a_parts = [a_norm, a_gates, a_xp, a_su, a_sv, a_q, a_k, a_v, a_o]
    a_parts += [functools.partial(a_sgu, ch) for ch in range(TILE_M // SGU_CHUNK)]

    def pool_pad():
        zpad = jnp.zeros((POOL_PAD, D_BRANCH), F32)
        xp_scr[0:POOL_PAD, :] = zpad
        xp_scr[T + POOL_PAD:T + 2 * POOL_PAD, :] = zpad

    def pool_seq(g):
        win = POOL_WINDOWS[g]
        gc = slice(g * GROUP, (g + 1) * GROUP)
        half = win // 2
        t_idx = lax.broadcasted_iota(jnp.int32, (T, GROUP), 0)
        total = jnp.zeros((T, GROUP), F32)
        for j in range(-half, win - half):
            total = total + xp_scr[POOL_PAD + j:POOL_PAD + j + T, gc]
        cnt = (jnp.minimum(t_idx + (win - half), T) - jnp.maximum(t_idx - half, 0)).astype(F32)
        dlt = total / cnt - xp_scr[POOL_PAD:POOL_PAD + T, gc]
        ya = _dot(dlt.astype(BF16), wpool_ref[g]) * pscale_ref[...][:, gc]
        ya_scr[:, gc] = ya.astype(BF16)

    def a_colpool(i):
        c_idx = lax.broadcasted_iota(jnp.int32, (GRID_W, GROUP), 0)
        for g, win in enumerate(POOL_WINDOWS):
            gc = slice(g * GROUP, (g + 1) * GROUP)
            half = win // 2
            inv_cnt = 1.0 / (jnp.minimum(c_idx + (win - half), GRID_W) - jnp.maximum(c_idx - half, 0)).astype(F32)
            for r in range(TILE_M // GRID_W):
                r0 = i * TILE_M + r * GRID_W
                xg = xp_scr[pl.ds(_aligned(r0 + POOL_PAD, 8), GRID_W), gc]
                total = xg
                for j in range(-half, win - half):
                    if j != 0:
                        valid = (c_idx + j >= 0) & (c_idx + j < GRID_W)
                        total = total + jnp.where(valid, pltpu.roll(xg, (-j) % GRID_W, 0), 0.0)
                m1_scr[pl.ds(_aligned(r0, GRID_W), GRID_W), gc] = total * inv_cnt

    def pool_grid_rows(g):
        win = POOL_WINDOWS[g]
        gc = slice(g * GROUP, (g + 1) * GROUP)
        half = win // 2
        n_rows = T // GRID_W
        for r in range(n_rows):
            lo = max(r - half, 0)
            hi = min(r + (win - half), n_rows)
            acc = m1_scr[lo * GRID_W:(lo + 1) * GRID_W, gc]
            for rr in range(lo + 1, hi):
                acc = acc + m1_scr[rr * GRID_W:(rr + 1) * GRID_W, gc]
            xg = xp_scr[POOL_PAD + r * GRID_W:POOL_PAD + (r + 1) * GRID_W, gc]
            dlt_scr[r * GRID_W:(r + 1) * GRID_W, :] = acc / float(hi - lo) - xg
        ya = _dot(dlt_scr[...].astype(BF16), wpool_ref[g]) * pscale_ref[...][:, gc]
        ya_scr[:, gc] = ya.astype(BF16)

    if grid2d:
        assert TILE_M % GRID_W == 0
        a_parts.insert(a_parts.index(a_xp) + 1, a_colpool)
        pool_parts = [functools.partial(pool_grid_rows, g) for g in range(N_GROUPS)]
    else:
        pool_parts = [pool_pad] + [functools.partial(pool_seq, g) for g in range(N_GROUPS)]

    def keep_mask(d, rb, col_lo, ncols):
        t_loc = rb * RB + lax.broadcasted_iota(jnp.int32, (RB, ncols), 0)
        s_loc = col_lo + lax.broadcasted_iota(jnp.int32, (RB, ncols), 1)
        return s_loc <= t_loc if d == 0 else s_loc >= t_loc

    def mlstm_prepare(d, c, m_in):
        c0 = _aligned(c * L, L)
        colf = colf_scr[pl.ds(c0, L), :]
        bcol = colf[:, N_HEADS * d:N_HEADS * (d + 1)]
        cmcol = colf[:, 2 * N_HEADS + N_HEADS * d:2 * N_HEADS + N_HEADS * (d + 1)]
        mx = jnp.maximum(m_in, cmcol)
        last = L - 1 if d == 0 else 0
        mx_last = mx[last:last + 1, :]
        return dict(c0=c0, mx=mx, mx_last=mx_last, w_inter=jnp.exp(m_in - mx), e_negm=jnp.exp(-(bcol + mx)),
                    decay=jnp.exp(m_in - mx_last), m_new=bcol[last:last + 1, :] + mx_last)

    def block_cols(d, rb):
        return (0, (rb + 1) * RB) if d == 0 else (rb * RB, L)

    def mlstm_scores(d, p, h):
        c0 = p["c0"]
        hc = slice(h * HEAD_DIM, (h + 1) * HEAD_DIM)
        qks = []
        for rb in range(L // RB):
            col_lo, col_hi = block_cols(d, rb)
            qb = q_scr[pl.ds(_aligned(c0 + rb * RB, RB), RB), hc]
            qks.append(_dot(qb, kt_scr[hc, pl.ds(_aligned(c0 + col_lo, RB), col_hi - col_lo)]))
        p["qk", h] = qks

    def mlstm_head(d, p, h):
        c0, mx, mx_last = p["c0"], p["mx"], p["mx_last"]
        rows = pl.ds(c0, L)
        hc = slice(h * HEAD_DIM, (h + 1) * HEAD_DIM)
        ac = slice(2 * h * HEAD_DIM, (2 * h + 2) * HEAD_DIM)
        grow = grow_scr[N_HEADS * d + h:N_HEADS * d + h + 1, rows]
        if has_state:
            cn_dh = cn_scr[d, h]
            cn_bf = cn_dh.astype(BF16)
        for rb in range(L // RB):
            col_lo, col_hi = block_cols(d, rb)
            ncols = col_hi - col_lo
            trow = slice(rb * RB, (rb + 1) * RB)
            rrows = pl.ds(_aligned(c0 + rb * RB, RB), RB)
            crows = pl.ds(_aligned(c0 + col_lo, RB), ncols)
            qb = q_scr[rrows, hc]
            qk = p["qk", h][rb]
            w = jnp.exp(grow[:, col_lo:col_hi] - mx[trow, h:h + 1])
            s = (qk * jnp.where(keep_mask(d, rb, col_lo, ncols), w, 0.0)).astype(BF16)
            na = _dot(s, vaug_scr[crows, ac])
            if has_state:
                na = na + p["w_inter"][trow, h:h + 1] * _dot(qb, cn_bf)
            hval = na[:, 0:HEAD_DIM] / jnp.maximum(jnp.abs(na[:, HEAD_DIM:]), p["e_negm"][trow, h:h + 1])
            if d == 0:
                h_scr[rrows, hc] = hval
            else:
                h_scr[rrows, hc] += hval
        w_state = jnp.exp(grow - mx_last[:, h:h + 1])
        kw_t = (kt_scr[hc, rows].astype(F32) * w_state).astype(BF16)
        upd = _dot(kw_t, vaug_scr[rows, ac])
        if has_state:
            cn_scr[d, h] = p["decay"][:, h:h + 1] * cn_dh + upd
        else:
            cout_ref[d, h] = upd[:, 0:HEAD_DIM]
            nout_ref[d, h:h + 1, :] = upd[:, HEAD_DIM:].T[0:1, :]

    def merge_ab(c, nj):
        wait("brg_ab")
        wait("branch_ab")
        nc = slice(nj * NW, (nj + 1) * NW)
        ub = u_scr[tile_rows(c), :]
        acc = None
        for r, y_scr in enumerate((ya_scr, yb_scr)):
            gate = jax.nn.sigmoid(_dot(ub, wbrg_ref[:, r * D_MODEL + nj * NW:r * D_MODEL + (nj + 1) * NW]))
            term = gate * _dot(y_scr[tile_rows(c), :], wbranch_ref[r * D_BRANCH:(r + 1) * D_BRANCH, nc])
            acc = term if acc is None else acc + term
        mab_scr[tile_rows(c), nc] = acc

    def gate_c(c, nj):
        wait("brg_c")
        w = wbrg_ref[:, 2 * D_MODEL + nj * NW:2 * D_MODEL + (nj + 1) * NW]
        gc_scr[:, nj * NW:(nj + 1) * NW] = jax.nn.sigmoid(_dot(u_scr[tile_rows(c), :], w))

    def tail_yc(c):
        gml = gml_ref[...]
        for h in range(N_HEADS):
            hc = slice(h * HEAD_DIM, (h + 1) * HEAD_DIM)
            hh = h_scr[tile_rows(c), hc]
            hn = hh * lax.rsqrt(jnp.mean(hh * hh, axis=-1, keepdims=True) + EPS) * gml[:, hc]
            yc_scr[:, hc] = (jax.nn.sigmoid(o_scr[tile_rows(c), hc]) * hn).astype(BF16)

    def tail_merge(c, nj):
        wait("branch_c")
        nc = slice(nj * NW, (nj + 1) * NW)
        acc = mab_scr[tile_rows(c), nc] + gc_scr[:, nc] * _dot(yc_scr[...], wbranch_ref[2 * D_BRANCH:, nc])
        merged_scr[:, nc] = acc.astype(BF16)

    def tail_out(c):
        wait("out")
        gt1 = mod_ref[...][:, 2 * D_MODEL:3 * D_MODEL]
        out_ref[tile_rows(c), :] = x_ref[tile_rows(c), :] + gt1 * _dot(merged_scr[...], wout_ref[...])

    def tail_parts(c):
        return ([functools.partial(tail_yc, c)] + [functools.partial(tail_merge, c, nj) for nj in range(n_merge)]
                + [functools.partial(tail_out, c)])

    def scan_parts(d, c, m_in, fillers, out):
        p = {}

        def prepare():
            p.update(mlstm_prepare(d, c, m_in))
            out[d] = p["m_new"]

        parts = [prepare]
        for h in range(N_HEADS):
            parts.append(functools.partial(mlstm_scores, d, p, h))
            if h < len(fillers):
                parts.append(fillers[h])
            parts.append(functools.partial(mlstm_head, d, p, h))
        return parts + list(fillers[N_HEADS:])

    if n_tiles == 1 and n_chunks == 1:
        zeros = jnp.zeros((1, N_HEADS), F32)
        m_out = {}
        parts = [functools.partial(part, 0) for part in a_parts] + pool_parts
        parts += scan_parts(0, 0, zeros, [functools.partial(merge_ab, 0, nj) for nj in range(n_merge)], m_out)
        parts += scan_parts(1, 0, zeros, [functools.partial(gate_c, 0, nj) for nj in range(n_merge)], m_out)
        parts += tail_parts(0)

        def write_m():
            mout_ref[0:1, :] = m_out[0]
            mout_ref[1:2, :] = m_out[1]

        return parts + [write_m]

    def run_loops():
        def phase_a(i, carry):
            for part in a_parts:
                part(i)
            return carry

        lax.fori_loop(0, n_tiles, phase_a, 0)
        for part in pool_parts:
            part()
        for d in range(2):
            for h in range(N_HEADS):
                n_rows = jnp.broadcast_to(n0_ref[d, h:h + 1, :], (HEAD_DIM, HEAD_DIM))
                cn_scr[d, h] = jnp.concatenate([c0_ref[d, h], n_rows.T], axis=1)
        m_init = m0_ref[...]
        for key in ("brg_ab", "branch_ab", "brg_c", "branch_c", "out"):
            wait(key)

        def fwd_step(j, m_in):
            m_out = {}
            for part in scan_parts(0, j, m_in, [functools.partial(merge_ab, j, nj) for nj in range(n_merge)], m_out):
                part()
            return m_out[0]

        def bwd_step(j, m_in):
            c = n_chunks - 1 - j
            m_out = {}
            for part in scan_parts(1, c, m_in, [functools.partial(gate_c, c, nj) for nj in range(n_merge)], m_out):
                part()
            for part in tail_parts(c):
                part()
            return m_out[1]

        lax.fori_loop(0, n_chunks, fwd_step, m_init[0:1, :])
        lax.fori_loop(0, n_chunks, bwd_step, m_init[1:2, :])

    assert has_state, "multi-chunk sequences carry the cached state"
    return [run_loops]


def _mixer_call(l, xall, lay, prm, stream, states=None, state_outs=None):
    ctx = stream == "ctx"
    T = SEQ if ctx else DEC_SEQ
    assert T % TILE_M == 0 and TILE_M == MLSTM_L and MLSTM_L % MLSTM_ROW_BLOCK == 0
    n_b = BATCH if ctx else DEC_BATCH
    own_array = xall.shape[0] != N_TOK
    blk0 = 0 if ctx or own_array else N_TOK_CTX // T
    once = pl.Buffered(1)

    def wspec(shape):
        nd = len(shape)
        return pl.BlockSpec((None,) + shape, lambda b: (l,) + (0,) * nd, pipeline_mode=once)

    def lspec(shape):
        return pl.BlockSpec(shape, lambda b: (0,) * len(shape), pipeline_mode=once)

    x_spec = pl.BlockSpec((T, D_MODEL), lambda b: (blk0 + b, 0))
    in_specs = [
        x_spec,
        lspec((COND_ROWS, 6 * D_MODEL)),
        lspec((DEPTH, D_MODEL)),
        pl.BlockSpec(memory_space=pl.ANY) if ctx else wspec((D_MODEL, N_MAIN)),
        lspec((D_BRANCH, D_MODEL)),
        lspec((D_MODEL, GATE_PAD)),
        pl.BlockSpec(memory_space=pl.ANY),
        pl.BlockSpec(memory_space=pl.ANY),
        pl.BlockSpec(memory_space=pl.ANY),
        wspec((N_GROUPS, GROUP, GROUP)),
        lspec((DEPTH, D_BRANCH)),
        lspec((DEPTH, D_BRANCH)),
        wspec((N_GROUPS, SGU_CHUNK, SGU_CHUNK)),
        wspec((SGU_CHUNK, D_BRANCH)),
        lspec((DEPTH, N_GATE)),
        lspec((DEPTH, D_BRANCH)),
    ]
    args = [xall, lay["mods"], prm["g_norm1"], prm["w_in16"], lay["w_kt"],
            lay["w_gate"], lay["w_brg"], lay["w_branch"], lay["w_out"],
            prm["w_pool"], prm["pool_scale"], prm["g_sgu"], prm["w_sgu"],
            prm["b_sgu_tile"], prm["b_gates"], prm["g_mlstm"]]
    out_shape = [jax.ShapeDtypeStruct(xall.shape, xall.dtype)]
    out_specs = [pl.BlockSpec((T, D_MODEL), lambda b: (blk0 + b, 0), pipeline_mode=None if ctx else once)]
    aliases = {} if own_array else {0: 0}
    passthrough = []
    if ctx:
        per_seq = [(2, N_HEADS, HEAD_DIM, HEAD_DIM), (2, N_HEADS, HEAD_DIM), (2, N_HEADS)]
        out_shape += [jax.ShapeDtypeStruct((BATCH, DEPTH) + s, F32) for s in per_seq]
        if state_outs is None:
            out_specs += [pl.BlockSpec((None, DEPTH) + s, lambda b, nd=len(s): (b,) + (0,) * (nd + 1))
                          for s in per_seq]
        else:
            out_specs += [pl.BlockSpec((None, None) + s, lambda b, nd=len(s): (b, l) + (0,) * nd) for s in per_seq]
            passthrough += [(a, 1 + k) for k, a in enumerate(state_outs)]
    else:
        in_specs += [
            pl.BlockSpec((None, None, 2, N_HEADS, HEAD_DIM, HEAD_DIM), lambda b: (b, l, 0, 0, 0, 0)),
            pl.BlockSpec((None, None, 2, N_HEADS, HEAD_DIM), lambda b: (b, l, 0, 0, 0)),
            pl.BlockSpec((None, None, 2, N_HEADS), lambda b: (b, l, 0, 0)),
        ]
        args += list(states)
    for buf, out_idx in passthrough:
        aliases[len(args)] = out_idx
        in_specs.append(pl.BlockSpec(memory_space=pl.ANY))
        args.append(buf)

    scratch = [
        pltpu.VMEM((T, D_MODEL), BF16),
        pltpu.VMEM((T + 2 * POOL_PAD, D_BRANCH), F32),
        pltpu.VMEM((TILE_M, D_BRANCH), F32),
        pltpu.VMEM((TILE_M, D_BRANCH), BF16),
        pltpu.VMEM((T, D_BRANCH), BF16),
        pltpu.VMEM((D_BRANCH, T), BF16),
        pltpu.VMEM((T, 2 * D_BRANCH), BF16),
        pltpu.VMEM((T, D_BRANCH), F32),
        pltpu.VMEM((2 * N_HEADS, T), F32),
        pltpu.VMEM((T, GATE_PAD), F32),
        pltpu.VMEM((T, D_BRANCH), F32),
        pltpu.VMEM((T, D_BRANCH), BF16),
        pltpu.VMEM((T, D_BRANCH), BF16),
        pltpu.VMEM((TILE_M, D_BRANCH), BF16),
        pltpu.VMEM((T, D_MODEL), F32),
        pltpu.VMEM((TILE_M, D_MODEL), F32),
        pltpu.VMEM((TILE_M, D_MODEL), BF16),
    ]
    if not ctx:
        scratch += [pltpu.VMEM((2, N_HEADS, HEAD_DIM, 2 * HEAD_DIM), F32)]
        scratch += [pltpu.VMEM((T, D_BRANCH), F32), pltpu.VMEM((T, GROUP), F32)]
    scratch += [pltpu.VMEM(args[k].shape, args[k].dtype) for k in _MIXER_LATE_INPUTS]
    n_copies = 5
    if ctx:
        scratch += [pltpu.VMEM((D_MODEL, N_MAIN), BF16)]
        n_copies += len(_MIXER_MAIN_GROUPS)
    late_sem = [pltpu.SemaphoreType.DMA((n_copies,))]

    outs = pl.pallas_call(
        functools.partial(_mixer_kernel, T=T, grid2d=not ctx, has_state=not ctx, layer=l,
                          n_in=len(in_specs), n_pass=len(passthrough), n_out=len(out_specs), stream_main=ctx),
        grid=(n_b,),
        in_specs=in_specs,
        out_specs=out_specs,
        out_shape=out_shape,
        scratch_shapes=scratch + late_sem,
        input_output_aliases=aliases,
        compiler_params=pltpu.CompilerParams(
            dimension_semantics=("arbitrary",),
            vmem_limit_bytes=_vmem_request(list(zip(args, in_specs)) + list(zip(out_shape, out_specs)), scratch)),
        name="mixer_ctx" if ctx else "mixer_lat",
    )(*args)
    return outs


def _ffn_kernel(*refs, layer, final, split_input):
    n_x = 2 if split_input else 1
    x_refs, (mod_ref, g2_ref, w1_hbm, w2_hbm, gfin_ref), rest = refs[:n_x], refs[n_x:n_x + 5], refs[n_x + 5:]
    if final:
        out_refs, rest = rest[:2], rest[2:]
    else:
        prep_in, rest = rest[:N_PREP_OPERANDS], rest[N_PREP_OPERANDS:]
        out_refs, prep_out, rest = rest[:1], rest[1:1 + len(LAYER_ARRAYS)], rest[1 + len(LAYER_ARRAYS):]
    acc_scr, h_scr, w1_scr, w2_scr, st1_scr, st2_scr, sem = rest[:7]
    n_chunks = D_FF // FFN_CHUNK
    tile = pl.program_id(0)
    first_tile = tile == 0
    ctx_tiles, lat_tiles = N_TOK_CTX // FFN_TILE_M, DEC_SEQ // FFN_TILE_M
    if split_input:
        x_ref = rest[7]

        @pl.when(tile < ctx_tiles)
        def _():
            x_ref[...] = x_refs[0][...]

        @pl.when(tile >= ctx_tiles)
        def _():
            x_ref[...] = x_refs[1][...]
    else:
        x_ref = x_refs[0]
    mod_ref = mod_ref.at[pl.ds(jnp.where(tile < ctx_tiles, 0, 1 + (tile - ctx_tiles) // lat_tiles), 1), :]
    g2_ref = g2_ref.at[pl.ds(layer, 1), :]

    def w1_copy(c):
        slot = c % FFN_STAGES
        c0 = _aligned(c * FFN_CHUNK, FFN_CHUNK)
        return pltpu.make_async_copy(w1_hbm.at[layer, :, pl.ds(c0, FFN_CHUNK)], st1_scr.at[slot], sem.at[0, slot])

    def w2_copy(c):
        slot = c % FFN_STAGES
        c0 = _aligned(c * FFN_CHUNK, FFN_CHUNK)
        return pltpu.make_async_copy(w2_hbm.at[layer, pl.ds(c0, FFN_CHUNK), :], st2_scr.at[slot], sem.at[1, slot])

    def land_w1(c):
        w1_copy(c).wait()
        c0 = _aligned(c * FFN_CHUNK, FFN_CHUNK)
        w1_scr[:, pl.ds(c0, FFN_CHUNK)] = st1_scr[c % FFN_STAGES].astype(BF16)

    def land_w2(c):
        w2_copy(c).wait()
        c0 = _aligned(c * FFN_CHUNK, FFN_CHUNK)
        w2_scr[pl.ds(c0, FFN_CHUNK), :] = st2_scr[c % FFN_STAGES].astype(BF16)

    def prepare_next_layer():
        if not final:
            _prepare_layer(*prep_in, *prep_out, layer=layer + 1)

    def normed_input():
        mod = mod_ref[...]
        sh2 = mod[:, 3 * D_MODEL:4 * D_MODEL]
        sc2 = mod[:, 4 * D_MODEL:5 * D_MODEL]
        x = x_ref[...]
        ms = jnp.mean(x * x, axis=-1, keepdims=True)
        return (x * lax.rsqrt(ms + EPS) * g2_ref[...] * (1.0 + sc2) + sh2).astype(BF16)

    def hidden(ub, c):
        c0 = _aligned(c * FFN_CHUNK, FFN_CHUNK)
        h = _dot(ub, w1_scr[:, pl.ds(c0, FFN_CHUNK)])
        h_scr[:, pl.ds(c0, FFN_CHUNK)] = jnp.square(jnp.maximum(h, 0.0)).astype(BF16)

    def finish(ffn_out):
        gt2 = mod_ref[...][:, 5 * D_MODEL:6 * D_MODEL]
        y = x_ref[...] + gt2 * ffn_out
        if not final:
            out_refs[0][...] = y
            return
        y = y * lax.rsqrt(jnp.mean(y * y, axis=-1, keepdims=True) + EPS) * gfin_ref[...]
        in_ctx = tile < ctx_tiles

        @pl.when(in_ctx)
        def _():
            out_refs[0][...] = y

        @pl.when(jnp.logical_not(in_ctx))
        def _():
            out_refs[1][...] = y

    @pl.when(first_tile)
    def _():
        for c in range(FFN_STAGES):
            w1_copy(c).start()
            w2_copy(c).start()
        prepare_next_layer()
        ub = normed_input()
        acc_scr[...] = jnp.zeros_like(acc_scr)

        def chunk(c, carry):
            land_w1(c)
            land_w2(c)

            @pl.when(c + FFN_STAGES < n_chunks)
            def _():
                w1_copy(c + FFN_STAGES).start()
                w2_copy(c + FFN_STAGES).start()

            hidden(ub, c)
            c0 = _aligned(c * FFN_CHUNK, FFN_CHUNK)
            acc_scr[...] += _dot(h_scr[:, pl.ds(c0, FFN_CHUNK)], w2_scr[pl.ds(c0, FFN_CHUNK), :])
            return carry

        lax.fori_loop(0, n_chunks, chunk, 0)
        finish(acc_scr[...])

    @pl.when(jnp.logical_not(first_tile))
    def _():
        prepare_next_layer()
        ub = normed_input()
        for c in range(n_chunks):
            hidden(ub, c)
        finish(_dot(h_scr[...], w2_scr[...]))


def _ffn_call(l, xall, lay, prm, g_final, final):
    n_tiles = N_TOK // FFN_TILE_M
    ctx_tiles = N_TOK_CTX // FFN_TILE_M
    once = pl.Buffered(1)
    split_input = isinstance(xall, tuple)
    assert not (split_input and final)

    x_spec = pl.BlockSpec((FFN_TILE_M, D_MODEL), lambda t: (t, 0))
    ctx_spec = pl.BlockSpec((FFN_TILE_M, D_MODEL), lambda t: (jnp.minimum(t, ctx_tiles - 1), 0))
    lat_spec = pl.BlockSpec((FFN_TILE_M, D_MODEL), lambda t: (jnp.maximum(t - ctx_tiles, 0), 0))
    xs = list(xall) if split_input else [xall]
    in_specs = ([ctx_spec, lat_spec] if split_input else [x_spec]) + [
        pl.BlockSpec((COND_ROWS, 6 * D_MODEL), lambda t: (0, 0), pipeline_mode=once),
        pl.BlockSpec((DEPTH, D_MODEL), lambda t: (0, 0), pipeline_mode=once),
        pl.BlockSpec(memory_space=pl.ANY),
        pl.BlockSpec(memory_space=pl.ANY),
        pl.BlockSpec((1, D_MODEL), lambda t: (0, 0)),
    ]
    args = xs + [lay["mods"], prm["g_norm2"], prm["w_ff1"], prm["w_ff2"], g_final.reshape(1, D_MODEL)]
    if final:
        out_specs = [ctx_spec, lat_spec]
        out_shape = [jax.ShapeDtypeStruct((N_TOK_CTX, D_MODEL), F32),
                     jax.ShapeDtypeStruct((N_TOK - N_TOK_CTX, D_MODEL), F32)]
        aliases = {}
    else:
        assert n_tiles == N_PREP_STEPS
        prep_args, prep_in_specs, prep_out_specs, prep_out_shape = _layer_prep_specs(l + 1, prm)
        in_specs += prep_in_specs
        args += prep_args
        out_specs = [x_spec] + prep_out_specs
        out_shape = [jax.ShapeDtypeStruct((N_TOK, D_MODEL), F32)] + prep_out_shape
        aliases = {} if split_input else {0: 0}
    sems = [pltpu.SemaphoreType.DMA((2, FFN_STAGES))]
    gather = [pltpu.VMEM((FFN_TILE_M, D_MODEL), F32)] if split_input else []
    scratch = [
        pltpu.VMEM((FFN_TILE_M, D_MODEL), F32),
        pltpu.VMEM((FFN_TILE_M, D_FF), BF16),
        pltpu.VMEM((D_MODEL, D_FF), BF16),
        pltpu.VMEM((D_FF, D_MODEL), BF16),
        pltpu.VMEM((FFN_STAGES, D_MODEL, FFN_CHUNK), F32),
        pltpu.VMEM((FFN_STAGES, FFN_CHUNK, D_MODEL), F32),
    ]
    return pl.pallas_call(
        functools.partial(_ffn_kernel, layer=l, final=final, split_input=split_input),
        grid=(n_tiles,),
        in_specs=in_specs,
        out_specs=out_specs,
        out_shape=out_shape,
        scratch_shapes=scratch + sems + gather,
        input_output_aliases=aliases,
        compiler_params=pltpu.CompilerParams(
            dimension_semantics=("arbitrary",),
            vmem_limit_bytes=_vmem_request(list(zip(args, in_specs)) + list(zip(out_shape, out_specs)),
                                           scratch + gather)),
        name="ffn_final" if final else "ffn",
    )(*args)


def kernel(x_prompt, x_sample, state_C, state_n, state_m, c, c_ctx, w_ada, b_ada, g_norm1, g_norm2, w_in,
           b_gates, w_pool, pool_scale, g_sgu, w_sgu, b_sgu, g_mlstm, w_branch, w_out, w_ff1, w_ff2, g_final):
    w_in16 = w_in.astype(BF16)
    prm = {
        "w_in16": w_in16,
        "w_branch_rows": w_branch.reshape(DEPTH, N_BRANCH * D_BRANCH, D_MODEL),
        "w_out": w_out,
        "w_ada": w_ada,
        "b_ada": b_ada,
        "w_pool": w_pool.astype(BF16),
        "w_sgu": w_sgu.astype(BF16),
        "w_ff1": w_ff1,
        "w_ff2": w_ff2,
        "g_norm1": g_norm1,
        "g_norm2": g_norm2,
        "pool_scale": pool_scale,
        "g_sgu": g_sgu,
        "g_mlstm": g_mlstm,
        "b_gates": b_gates,
        "b_sgu_tile": jnp.repeat(jnp.swapaxes(b_sgu, 1, 2), GROUP, axis=2),
        "cond8": jnp.concatenate(
            [c_ctx[None, :], c, jnp.zeros((COND_ROWS - 1 - DEC_BATCH, D_MODEL), F32)], axis=0),
    }
    lay = dict(zip(LAYER_ARRAYS, _first_layer_prep_call(prm)))

    xc, *new_states = _mixer_call(0, x_prompt.reshape(N_TOK_CTX, D_MODEL), lay, prm, "ctx")
    (xs,) = _mixer_call(0, x_sample.reshape(-1, D_MODEL), lay, prm, "lat", states=(state_C, state_n, state_m))
    xall = (xc, xs)
    for l in range(DEPTH):
        if l > 0:
            xall, *new_states = _mixer_call(l, xall, lay, prm, "ctx", state_outs=new_states)
            (xall,) = _mixer_call(l, xall, lay, prm, "lat", states=(state_C, state_n, state_m))
        if l < DEPTH - 1:
            xall, *nxt = _ffn_call(l, xall, lay, prm, g_final, final=False)
            lay = dict(zip(LAYER_ARRAYS, nxt))
    y_prompt, y_sample = _ffn_call(DEPTH - 1, xall, lay, prm, g_final, final=True)
    return (y_prompt.reshape(BATCH, SEQ, D_MODEL), y_sample.reshape(DEC_BATCH, DEC_SEQ, D_MODEL), *new_states)
```
